```python
import math
import jax
import jax.numpy as jnp
from jax import lax
import numpy as np

D_MODEL = 1024
BATCH = 16
SEQ = 256
DEPTH = 2
DEC_BATCH = 2
DEC_SEQ = 1024
PAST_LEN = 256

GRID_W = 64
D_SSM = 512
SSM_GROUP = 16
N_SSM_GROUPS = D_SSM // SSM_GROUP
SSM_STATE = 64
N_DIR = 2
D_GMLP = D_MODEL - D_SSM
GMLP_HEADS = 4
GMLP_HEAD_DIM = D_GMLP // GMLP_HEADS
CHUNK = 128
D_MIX = D_SSM + D_GMLP
D_IN = D_SSM + 2 * D_GMLP
N_EXPERTS = 16
N_EXPERT_GROUPS = 4
EXPERTS_PER_GROUP = N_EXPERTS // N_EXPERT_GROUPS
TOP_K = 2
D_EXPERT = 512
N_MOD = 6
EPS = 1e-6
DT_MIN = 1e-3
DT_MAX = 1e-1

kernel_name = "hybrid_s5_gmlp_grouped_moe_diffusion_step"


def _rmsnorm(x, w):
    xf = x.astype(jnp.float32)
    y = xf * lax.rsqrt(jnp.mean(xf * xf, axis=-1, keepdims=True) + EPS)
    return (y * w.astype(jnp.float32)).astype(x.dtype)


def _lin_combine(left, right):
    a_l, b_l = left
    a_r, b_r = right
    return a_r * a_l, a_r * b_l + b_r


def _scan_direction(ug, abar, bbar, cm, s0, reverse):
    L = ug.shape[1]
    bu = jnp.einsum('gnh,blgh->blgn', bbar, ug)
    first = L - 1 if reverse else 0
    last = 0 if reverse else L - 1
    bu = bu.at[:, first].add(abar[None] * s0)
    a = jnp.broadcast_to(abar, bu.shape)
    _, s = lax.associative_scan(_lin_combine, (a, bu), axis=1, reverse=reverse)
    y = jnp.real(jnp.einsum('ghn,blgn->blgh', cm, s))
    return y, s[:, last]


def _ssm_mixer(u, s0, a_re, a_im, log_dt, b_re, b_im, c_re, c_im, d, w_glu):
    bsz, L = u.shape[0], u.shape[1]
    f32 = jnp.float32
    A = lax.complex(a_re.astype(f32), a_im.astype(f32))
    dt = jnp.exp(log_dt.astype(f32))[..., None]
    abar = jnp.exp(A * dt)
    bbar = ((abar - 1.0) / A)[..., None] * lax.complex(b_re.astype(f32), b_im.astype(f32))
    cm = lax.complex(c_re.astype(f32), c_im.astype(f32))
    uf = u.astype(f32)
    ug = uf.reshape(bsz, L, N_SSM_GROUPS, SSM_GROUP).astype(jnp.complex64)
    y_f, s_f = _scan_direction(ug, abar[0], bbar[0], cm[0], s0[:, 0], False)
    y_b, s_b = _scan_direction(ug, abar[1], bbar[1], cm[1], s0[:, 1], True)
    y = (y_f + y_b).reshape(bsz, L, D_SSM) + d.astype(f32) * uf
    y = jax.nn.gelu(y)
    y = y * jax.nn.sigmoid(y @ w_glu.astype(f32))
    st = jnp.stack([s_f, s_b], axis=1)
    return y.astype(u.dtype), jnp.real(st), jnp.imag(st)


def _gmlp_mixer(u, v, ln_w, ln_b, w_s, b_s):
    bsz, L = u.shape[0], u.shape[1]
    vf = v.astype(jnp.float32)
    mu = jnp.mean(vf, axis=-1, keepdims=True)
    var = jnp.mean(jnp.square(vf - mu), axis=-1, keepdims=True)
    vn = ((vf - mu) * lax.rsqrt(var + EPS) * ln_w.astype(jnp.float32) + ln_b.astype(jnp.float32)).astype(u.dtype)
    vc = vn.reshape(bsz, L // CHUNK, CHUNK, GMLP_HEADS, GMLP_HEAD_DIM)
    s = jnp.einsum('hpq,bcqhd->bcphd', w_s, vc) + jnp.transpose(b_s)[None, None, :, :, None]
    return u * s.reshape(bsz, L, D_GMLP)


def _moe(h, router_w, router_b, w_gate, w_up, w_down):
    T = h.shape[0]
    scores = jax.nn.sigmoid((h @ router_w).astype(jnp.float32))
    sel = scores + router_b.astype(jnp.float32)
    grouped = sel.reshape(T, N_EXPERT_GROUPS, EXPERTS_PER_GROUP)
    gscore = lax.top_k(grouped, TOP_K)[0].sum(-1)
    g = jnp.argmax(gscore, axis=-1)
    in_group = jnp.take_along_axis(grouped, g[:, None, None], axis=1)[:, 0]
    _, loc = lax.top_k(in_group, TOP_K)
    expert = g[:, None] * EXPERTS_PER_GROUP + loc
    w = jnp.take_along_axis(scores, expert, axis=1)
    w = w / jnp.sum(w, axis=-1, keepdims=True)
    gates = jnp.sum(jax.nn.one_hot(expert, N_EXPERTS, dtype=jnp.float32) * w[..., None], axis=1)
    hg = jnp.einsum('td,edf->tef', h, w_gate)
    hu = jnp.einsum('td,edf->tef', h, w_up)
    act = jax.nn.silu(hg) * hu * gates[..., None].astype(h.dtype)
    return jnp.einsum('tef,efd->td', act, w_down)


def _layer(x, mod, s0, lp, router_w, router_b):
    shift1, scale1, gate1, shift2, scale2, gate2 = [m[:, None, :] for m in jnp.split(mod, N_MOD, axis=-1)]
    h = _rmsnorm(x, lp['norm1_w']) * (1.0 + scale1) + shift1
    z = h @ lp['w_in']
    u_ssm = z[..., :D_SSM]
    zg = jax.nn.gelu(z[..., D_SSM:])
    u_g, v_g = zg[..., :D_GMLP], zg[..., D_GMLP:]
    y_ssm, st_re, st_im = _ssm_mixer(u_ssm, s0, lp['a_re'], lp['a_im'], lp['log_dt'], lp['b_re'], lp['b_im'],
                                     lp['c_re'], lp['c_im'], lp['d'], lp['w_glu'])
    y_g = _gmlp_mixer(u_g, v_g, lp['ln_w'], lp['ln_b'], lp['w_s'], lp['b_s'])
    x = x + gate1 * (jnp.concatenate([y_ssm, y_g], axis=-1) @ lp['w_out'])
    h2 = _rmsnorm(x, lp['norm2_w']) * (1.0 + scale2) + shift2
    moe_out = _moe(h2.reshape(-1, D_MODEL), router_w, router_b, lp['w_gate'], lp['w_up'], lp['w_down'])
    x = x + gate2 * moe_out.reshape(x.shape)
    return x, st_re, st_im


def setup_inputs(seed: int = 0) -> dict:
    key = jax.random.key(seed)
    ks = jax.random.split(key, 32)
    f32 = jnp.float32
    nrm = lambda k, shape, s: jax.random.normal(k, shape, f32) * s
    n_idx = jnp.arange(SSM_STATE, dtype=f32)
    a_im_base = math.pi * n_idx
    return {
        "x_prompt": nrm(ks[0], (BATCH, SEQ, D_MODEL), 1.0),
        "x_sample": nrm(ks[1], (DEC_BATCH, DEC_SEQ, D_MODEL), 1.0),
        "c": nrm(ks[2], (DEC_BATCH, D_MODEL), 1.0),
        "state_ssm_re": nrm(ks[3], (DEC_BATCH, DEPTH, N_DIR, N_SSM_GROUPS, SSM_STATE), 0.1),
        "state_ssm_im": nrm(ks[4], (DEC_BATCH, DEPTH, N_DIR, N_SSM_GROUPS, SSM_STATE), 0.1),
        "c_ctx": nrm(ks[5], (D_MODEL,), 1.0),
        "norm1_w": 1.0 + nrm(ks[6], (DEPTH, D_MODEL), 0.02),
        "norm2_w": 1.0 + nrm(ks[7], (DEPTH, D_MODEL), 0.02),
        "w_mod": nrm(ks[8], (DEPTH, D_MODEL, N_MOD * D_MODEL), 0.5 * D_MODEL ** -0.5),
        "b_mod": nrm(ks[9], (DEPTH, N_MOD * D_MODEL), 0.02),
        "w_in": nrm(ks[10], (DEPTH, D_MODEL, D_IN), D_MODEL ** -0.5),
        "ssm_a_re": -0.5 + nrm(ks[11], (DEPTH, N_DIR, N_SSM_GROUPS, SSM_STATE), 0.01),
        "ssm_a_im": a_im_base + nrm(ks[12], (DEPTH, N_DIR, N_SSM_GROUPS, SSM_STATE), 0.01),
        "ssm_log_dt": jax.random.uniform(ks[13], (DEPTH, N_DIR, N_SSM_GROUPS), f32, math.log(DT_MIN), math.log(DT_MAX)),
        "ssm_b_re": nrm(ks[14], (DEPTH, N_DIR, N_SSM_GROUPS, SSM_STATE, SSM_GROUP), (0.5 / SSM_GROUP) ** 0.5),
        "ssm_b_im": nrm(ks[15], (DEPTH, N_DIR, N_SSM_GROUPS, SSM_STATE, SSM_GROUP), (0.5 / SSM_GROUP) ** 0.5),
        "ssm_c_re": nrm(ks[16], (DEPTH, N_DIR, N_SSM_GROUPS, SSM_GROUP, SSM_STATE), (0.5 / SSM_STATE) ** 0.5),
        "ssm_c_im": nrm(ks[17], (DEPTH, N_DIR, N_SSM_GROUPS, SSM_GROUP, SSM_STATE), (0.5 / SSM_STATE) ** 0.5),
        "ssm_d": nrm(ks[18], (DEPTH, D_SSM), 1.0),
        "w_glu": nrm(ks[19], (DEPTH, D_SSM, D_SSM), D_SSM ** -0.5),
        "gmlp_ln_w": 1.0 + nrm(ks[20], (DEPTH, D_GMLP), 0.02),
        "gmlp_ln_b": nrm(ks[21], (DEPTH, D_GMLP), 0.02),
        "gmlp_w_s": nrm(ks[22], (DEPTH, GMLP_HEADS, CHUNK, CHUNK), CHUNK ** -0.5),
        "gmlp_b_s": 1.0 + nrm(ks[23], (DEPTH, GMLP_HEADS, CHUNK), 0.02),
        "w_out": nrm(ks[24], (DEPTH, D_MIX, D_MODEL), D_MIX ** -0.5),
        "router_w": nrm(ks[25], (D_MODEL, N_EXPERTS), D_MODEL ** -0.5),
        "router_b": nrm(ks[26], (N_EXPERTS,), 0.01),
        "w_gate": nrm(ks[27], (DEPTH, N_EXPERTS, D_MODEL, D_EXPERT), D_MODEL ** -0.5),
        "w_up": nrm(ks[28], (DEPTH, N_EXPERTS, D_MODEL, D_EXPERT), D_MODEL ** -0.5),
        "w_down": nrm(ks[29], (DEPTH, N_EXPERTS, D_EXPERT, D_MODEL), D_EXPERT ** -0.5),
        "final_norm_w": 1.0 + nrm(ks[30], (D_MODEL,), 0.02),
    }


def reference(x_prompt, x_sample, c, state_ssm_re, state_ssm_im, c_ctx, norm1_w, norm2_w, w_mod, b_mod, w_in,
              ssm_a_re, ssm_a_im, ssm_log_dt, ssm_b_re, ssm_b_im, ssm_c_re, ssm_c_im, ssm_d, w_glu,
              gmlp_ln_w, gmlp_ln_b, gmlp_w_s, gmlp_b_s, w_out, router_w, router_b, w_gate, w_up, w_down,
              final_norm_w):
    f32 = jnp.float32
    bsz_p = x_prompt.shape[0]
    zero_state = lax.complex(jnp.zeros((bsz_p, N_DIR, N_SSM_GROUPS, SSM_STATE), f32),
                             jnp.zeros((bsz_p, N_DIR, N_SSM_GROUPS, SSM_STATE), f32))
    xp = x_prompt
    xs = x_sample
    new_re = []
    new_im = []
    for l in range(DEPTH):
        lp = {
            'norm1_w': norm1_w[l], 'norm2_w': norm2_w[l], 'w_in': w_in[l],
            'a_re': ssm_a_re[l], 'a_im': ssm_a_im[l], 'log_dt': ssm_log_dt[l],
            'b_re': ssm_b_re[l], 'b_im': ssm_b_im[l], 'c_re': ssm_c_re[l], 'c_im': ssm_c_im[l],
            'd': ssm_d[l], 'w_glu': w_glu[l], 'ln_w': gmlp_ln_w[l], 'ln_b': gmlp_ln_b[l],
            'w_s': gmlp_w_s[l], 'b_s': gmlp_b_s[l], 'w_out': w_out[l],
            'w_gate': w_gate[l], 'w_up': w_up[l], 'w_down': w_down[l],
        }
        mod_ctx = jax.nn.silu(c_ctx)[None, :] @ w_mod[l] + b_mod[l]
        xp, st_re, st_im = _layer(xp, mod_ctx, zero_state, lp, router_w, router_b)
        new_re.append(st_re)
        new_im.append(st_im)
        mod_lat = jax.nn.silu(c) @ w_mod[l] + b_mod[l]
        s0 = lax.complex(state_ssm_re[:, l].astype(f32), state_ssm_im[:, l].astype(f32))
        xs, _, _ = _layer(xs, mod_lat, s0, lp, router_w, router_b)
    y_prompt = _rmsnorm(xp, final_norm_w)
    y_sample = _rmsnorm(xs, final_norm_w)
    new_state_ssm_re = jnp.stack(new_re, axis=1)
    new_state_ssm_im = jnp.stack(new_im, axis=1)
    return (y_prompt, y_sample, new_state_ssm_re, new_state_ssm_im)
```

```python
import functools
import math

import jax
import jax.numpy as jnp
from jax import lax
from jax.experimental import pallas as pl
from jax.experimental.pallas import tpu as pltpu

F32 = jnp.float32
BF16 = jnp.bfloat16

D_MODEL = 1024
N_PROMPT_SEQ = 16
PROMPT_LEN = 256
N_SAMPLE_SEQ = 2
SAMPLE_LEN = 1024
T_PROMPT = N_PROMPT_SEQ * PROMPT_LEN
T_SAMPLE = N_SAMPLE_SEQ * SAMPLE_LEN
T_ALL = T_PROMPT + T_SAMPLE
DEPTH = 2
D_SSM = 512
SSM_GROUP = 16
N_GROUPS = 32
SSM_STATE = 64
D_GMLP = 512
GMLP_HEADS = 4
GMLP_HEAD_DIM = 128
GMLP_CHUNK = 128
D_IN = D_SSM + 2 * D_GMLP
N_EXPERTS = 16
N_EXPERT_GROUPS = 4
EXPERTS_PER_GROUP = 4
D_EXPERT = 512
N_MOD = 6
EPS = 1e-6

SCAN_CHUNK = 16
CHUNK_WIDTH = SCAN_CHUNK * SSM_GROUP
PROMPT_CHUNKS = PROMPT_LEN // SCAN_CHUNK
SAMPLE_CHUNKS = SAMPLE_LEN // SCAN_CHUNK
PROMPT_ROWS = PROMPT_CHUNKS * N_PROMPT_SEQ
SAMPLE_ROWS = SAMPLE_CHUNKS * N_SAMPLE_SEQ
CHUNK_ROWS = PROMPT_ROWS + SAMPLE_ROWS
MOD_ROWS = 8

TOKEN_TILE = 256
MOE_TILE = 1024
VMEM_LIMIT = 56 * 1024 * 1024


def _sigmoid(x):
    return 1.0 / (1.0 + jnp.exp(-x))


def _gelu_tanh(x):
    c = math.sqrt(2.0 / math.pi)
    return x * (0.5 * (1.0 + jnp.tanh(c * (x + 0.044715 * (x * x * x)))))


def _rmsnorm(x, w):
    return x * lax.rsqrt(jnp.mean(x * x, axis=-1, keepdims=True) + EPS) * w


def _mod_row(i, tile):
    first_sample_tile = T_PROMPT // tile
    tiles_per_sample_seq = SAMPLE_LEN // tile
    return jnp.where(i < first_sample_tile, 0, 1 + (i - first_sample_tile) // tiles_per_sample_seq)


def _params(*sem):
    return pltpu.CompilerParams(dimension_semantics=sem, vmem_limit_bytes=VMEM_LIMIT)


def _mod_kernel(c_ref, w_ref, b_ref, o_ref):
    c = c_ref[...]
    s = c * _sigmoid(c)
    o_ref[...] = jnp.dot(s.astype(BF16), w_ref[...].astype(BF16), preferred_element_type=F32) + b_ref[...]


def _modulation(cvec, w_mod, b_mod):
    return pl.pallas_call(
        _mod_kernel,
        grid=(DEPTH, N_MOD),
        in_specs=[
            pl.BlockSpec((MOD_ROWS, D_MODEL), lambda l, n: (0, 0)),
            pl.BlockSpec((None, D_MODEL, D_MODEL), lambda l, n: (l, 0, n)),
            pl.BlockSpec((None, None, 1, D_MODEL), lambda l, n: (l, n, 0, 0)),
        ],
        out_specs=pl.BlockSpec((None, None, MOD_ROWS, D_MODEL), lambda l, n: (l, n, 0, 0)),
        out_shape=jax.ShapeDtypeStruct((DEPTH, N_MOD, MOD_ROWS, D_MODEL), F32),
        compiler_params=_params("arbitrary", "arbitrary"),
        name="adaln_mod",
    )(cvec, w_mod, b_mod.reshape(DEPTH, N_MOD, 1, D_MODEL))


def _in_kernel(x_ref, mod_ref, nw_ref, w_ref, lnw_ref, lnb_ref, u_ref, ug_ref, vn_ref):
    shift = mod_ref[0:1, :]
    scale = mod_ref[1:2, :]
    h = _rmsnorm(x_ref[...], nw_ref[...]) * (1.0 + scale) + shift
    z = jnp.dot(h.astype(BF16), w_ref[...].astype(BF16), preferred_element_type=F32)
    u_ref[...] = z[:, :D_SSM]
    zg = _gelu_tanh(z[:, D_SSM:])
    ug_ref[...] = zg[:, :D_GMLP]
    v = zg[:, D_GMLP:]
    mu = jnp.mean(v, axis=-1, keepdims=True)
    vc = v - mu
    var = jnp.mean(vc * vc, axis=-1, keepdims=True)
    vn_ref[...] = vc * lax.rsqrt(var + EPS) * lnw_ref[...] + lnb_ref[...]


def _input_proj(x, mods, l, norm1_w, w_in, ln_w, ln_b):
    tm = TOKEN_TILE
    tok = lambda n: pl.BlockSpec((tm, n), lambda i: (i, 0))
    return pl.pallas_call(
        _in_kernel,
        grid=(T_ALL // tm,),
        in_specs=[
            tok(D_MODEL),
            pl.BlockSpec((None, None, N_MOD, D_MODEL), lambda i: (l, _mod_row(i, tm), 0, 0)),
            pl.BlockSpec((None, 1, D_MODEL), lambda i: (l, 0, 0)),
            pl.BlockSpec((None, D_MODEL, D_IN), lambda i: (l, 0, 0)),
            pl.BlockSpec((None, 1, D_GMLP), lambda i: (l, 0, 0)),
            pl.BlockSpec((None, 1, D_GMLP), lambda i: (l, 0, 0)),
        ],
        out_specs=[tok(D_SSM), tok(D_GMLP), tok(D_GMLP)],
        out_shape=[jax.ShapeDtypeStruct((T_ALL, D_SSM), F32)] * 3,
        compiler_params=_params("arbitrary"),
        name="norm1_in_proj",
    )(x, mods, norm1_w.reshape(DEPTH, 1, D_MODEL), w_in,
      ln_w.reshape(DEPTH, 1, D_GMLP), ln_b.reshape(DEPTH, 1, D_GMLP))


def _shift_lanes_right(a, b, s, lane):
    if s == 0:
        return a, b
    if s == 128:
        return jnp.zeros_like(a), a
    if s < 128:
        ra = pltpu.roll(a, s, 1)
        rb = pltpu.roll(b, s, 1)
        return jnp.where(lane >= s, ra, 0.0), jnp.where(lane >= s, rb, ra)
    t = s - 128
    return jnp.zeros_like(a), jnp.where(lane >= t, pltpu.roll(a, t, 1), 0.0)


def _shift_lanes_left(a, b, s, lane):
    if s == 0:
        return a, b
    if s == 128:
        return b, jnp.zeros_like(b)
    if s < 128:
        ra = pltpu.roll(a, 128 - s, 1)
        rb = pltpu.roll(b, 128 - s, 1)
        return jnp.where(lane < 128 - s, ra, rb), jnp.where(lane < 128 - s, rb, 0.0)
    t = s - 128
    return jnp.where(lane < 128 - t, pltpu.roll(b, 128 - t, 1), 0.0), jnp.zeros_like(b)


def _prep_kernel(vec_ref, mat_ref, wf_ref, cct_ref, tt_ref, a16_ref, cm_scr):
    a_re = vec_ref[0:1, :]
    a_im = vec_ref[1:2, :]
    dt = jnp.exp(vec_ref[2:3, :])
    mag = jnp.exp(a_re * dt)
    ang = a_im * dt
    ab_r = mag * jnp.cos(ang)
    ab_i = mag * jnp.sin(ang)
    den = a_re * a_re + a_im * a_im
    nr = ab_r - 1.0
    q_r = (nr * a_re + ab_i * a_im) / den
    q_i = (ab_i * a_re - nr * a_im) / den
    bt_r = mat_ref[0]
    bt_i = mat_ref[1]
    c_r = mat_ref[2]
    c_i = mat_ref[3]
    bb_r = q_r * bt_r - q_i * bt_i
    bb_i = q_r * bt_i + q_i * bt_r
    p_r = [jnp.ones_like(ab_r)]
    p_i = [jnp.zeros_like(ab_r)]
    for _ in range(SCAN_CHUNK):
        pr, pi = p_r[-1], p_i[-1]
        p_r.append(pr * ab_r - pi * ab_i)
        p_i.append(pr * ab_i + pi * ab_r)
    a16_ref[...] = jnp.zeros_like(a16_ref)
    a16_ref[0:1, :] = p_r[SCAN_CHUNK]
    a16_ref[1:2, :] = p_i[SCAN_CHUNK]

    lane = lax.broadcasted_iota(jnp.int32, (1, 128), 1)
    is_fwd = lane < SSM_STATE

    def pick(mf, mb):
        return jnp.where(is_fwd, p_r[mf], p_r[mb]), jnp.where(is_fwd, p_i[mf], p_i[mb])

    for j in range(SCAN_CHUNK):
        rows = slice(j * SSM_GROUP, (j + 1) * SSM_GROUP)
        wr, wi = pick(SCAN_CHUNK - 1 - j, j)
        wf_ref[rows, 0:128] = (bb_r * wr - bb_i * wi).astype(BF16)
        wf_ref[rows, 128:256] = (bb_r * wi + bb_i * wr).astype(BF16)
        wr, wi = pick(j + 1, SCAN_CHUNK - j)
        cct_ref[rows, 0:128] = (c_r * wr - c_i * wi).astype(BF16)
        cct_ref[rows, 128:256] = (-(c_r * wi + c_i * wr)).astype(BF16)
        wr, wi = pick(j, SCAN_CHUNK - 1 - j)
        cm_scr[rows, 0:128] = c_r * wr - c_i * wi
        cm_scr[rows, 128:256] = c_r * wi + c_i * wr

    zero = jnp.zeros_like(bb_r)
    cm = cm_scr[...]
    dn = (((1,), (1,)), ((), ()))

    def lag_rows(keep):
        lhs = jnp.concatenate([jnp.where(keep, bb_r, zero), jnp.where(keep, -bb_i, zero)], axis=1)
        return lax.dot_general(lhs, cm, dn, precision=lax.Precision.HIGHEST, preferred_element_type=F32)

    mf = lag_rows(is_fwd)
    mb = lag_rows(jnp.logical_not(is_fwd))
    mf_a, mf_b = mf[:, 0:128], mf[:, 128:256]
    mb_a, mb_b = mb[:, 0:128], mb[:, 128:256]
    for jp in range(SCAN_CHUNK):
        rows = slice(jp * SSM_GROUP, (jp + 1) * SSM_GROUP)
        fa, fb = _shift_lanes_right(mf_a, mf_b, SSM_GROUP * jp, lane)
        ba, bb = _shift_lanes_left(mb_a, mb_b, SSM_GROUP * (SCAN_CHUNK - 1 - jp), lane)
        tt_ref[rows, 0:128] = (fa + ba).astype(BF16)
        tt_ref[rows, 128:256] = (fb + bb).astype(BF16)


def _ssm_operators(vecs, mats):
    op = jax.ShapeDtypeStruct((DEPTH, N_GROUPS, CHUNK_WIDTH, CHUNK_WIDTH), BF16)
    op_spec = pl.BlockSpec((None, None, CHUNK_WIDTH, CHUNK_WIDTH), lambda l, g: (l, g, 0, 0))
    return pl.pallas_call(
        _prep_kernel,
        grid=(DEPTH, N_GROUPS),
        in_specs=[
            pl.BlockSpec((None, None, 8, 128), lambda l, g: (l, g, 0, 0)),
            pl.BlockSpec((None, None, 4, SSM_GROUP, 128), lambda l, g: (l, g, 0, 0, 0)),
        ],
        out_specs=[op_spec, op_spec, op_spec,
                   pl.BlockSpec((None, None, 8, 128), lambda l, g: (l, g, 0, 0))],
        out_shape=[op, op, op, jax.ShapeDtypeStruct((DEPTH, N_GROUPS, 8, 128), F32)],
        scratch_shapes=[pltpu.VMEM((CHUNK_WIDTH, CHUNK_WIDTH), F32)],
        compiler_params=_params("arbitrary", "arbitrary"),
        name="s5_operators",
    )(vecs, mats)


def _ssm_kernel(x_ref, wf_ref, cct_ref, tt_ref, a16_ref, s0_ref, y_ref, fin_ref, f_scr, s_scr):
    x = x_ref[...]
    f_scr[...] = jnp.dot(x, wf_ref[...], preferred_element_type=F32)
    a_r = a16_ref[0:1, :]
    a_i = a16_ref[1:2, :]
    lane = lax.broadcasted_iota(jnp.int32, (1, 128), 1)
    is_fwd = lane < SSM_STATE
    half = SSM_STATE

    def scan(s_r, s_i, base, n_chunks, n_seq):
        for i in range(n_chunks):
            rf = slice(base + i * n_seq, base + (i + 1) * n_seq)
            rb = slice(base + (n_chunks - 1 - i) * n_seq, base + (n_chunks - i) * n_seq)
            s_scr[rf, 0:half] = s_r[:, 0:half]
            s_scr[rf, 128:128 + half] = s_i[:, 0:half]
            s_scr[rb, half:128] = s_r[:, half:128]
            s_scr[rb, 128 + half:256] = s_i[:, half:128]
            f_r = jnp.where(is_fwd, f_scr[rf, 0:128], f_scr[rb, 0:128])
            f_i = jnp.where(is_fwd, f_scr[rf, 128:256], f_scr[rb, 128:256])
            s_r, s_i = a_r * s_r - a_i * s_i + f_r, a_r * s_i + a_i * s_r + f_i
        return s_r, s_i

    zeros = jnp.zeros((N_PROMPT_SEQ, 128), F32)
    s_r, s_i = scan(zeros, zeros, 0, PROMPT_CHUNKS, N_PROMPT_SEQ)
    fin_ref[:, 0:128] = s_r
    fin_ref[:, 128:256] = s_i
    scan(s0_ref[0:N_SAMPLE_SEQ, 0:128], s0_ref[0:N_SAMPLE_SEQ, 128:256], PROMPT_ROWS, SAMPLE_CHUNKS, N_SAMPLE_SEQ)

    y = jnp.dot(x, tt_ref[...], preferred_element_type=F32)
    y += lax.dot_general(s_scr[...].astype(BF16), cct_ref[...], (((1,), (1,)), ((), ())),
                         preferred_element_type=F32)
    y_ref[...] = y


def _ssm_scan(xg, wf, cct, tt, a16, s0, l):
    op_spec = pl.BlockSpec((None, None, CHUNK_WIDTH, CHUNK_WIDTH), lambda g: (l, g, 0, 0))
    return pl.pallas_call(
        _ssm_kernel,
        grid=(N_GROUPS,),
        in_specs=[
            pl.BlockSpec((None, CHUNK_ROWS, CHUNK_WIDTH), lambda g: (g, 0, 0)),
            op_spec, op_spec, op_spec,
            pl.BlockSpec((None, None, 8, 128), lambda g: (l, g, 0, 0)),
            pl.BlockSpec((None, 8, CHUNK_WIDTH), lambda g: (g, 0, 0)),
        ],
        out_specs=[
            pl.BlockSpec((None, CHUNK_ROWS, CHUNK_WIDTH), lambda g: (g, 0, 0)),
            pl.BlockSpec((None, N_PROMPT_SEQ, CHUNK_WIDTH), lambda g: (g, 0, 0)),
        ],
        out_shape=[
            jax.ShapeDtypeStruct((N_GROUPS, CHUNK_ROWS, CHUNK_WIDTH), F32),
            jax.ShapeDtypeStruct((N_GROUPS, N_PROMPT_SEQ, CHUNK_WIDTH), F32),
        ],
        scratch_shapes=[pltpu.VMEM((CHUNK_ROWS, CHUNK_WIDTH), F32), pltpu.VMEM((CHUNK_ROWS, CHUNK_WIDTH), F32)],
        compiler_params=_params("arbitrary"),
        name="s5_chunk_scan",
    )(xg, wf, cct, tt, a16, s0)


def _post_kernel(x_ref, yraw_ref, u_ref, ug_ref, vn_ref, mod_ref, d_ref, wglu_ref, ws_ref, bs_ref, wout_ref,
                 nw_ref, rwt_ref, rb_ref, x1_ref, h2_ref, gates_ref, yg_scr):
    tm = x_ref.shape[0]
    gate1 = mod_ref[2:3, :]
    shift2 = mod_ref[3:4, :]
    scale2 = mod_ref[4:5, :]
    y = _gelu_tanh(yraw_ref[...] + d_ref[...] * u_ref[...])
    y = y * _sigmoid(jnp.dot(y.astype(BF16), wglu_ref[...].astype(BF16), preferred_element_type=F32))
    for ch in range(tm // GMLP_CHUNK):
        rows = slice(ch * GMLP_CHUNK, (ch + 1) * GMLP_CHUNK)
        for h in range(GMLP_HEADS):
            cols = slice(h * GMLP_HEAD_DIM, (h + 1) * GMLP_HEAD_DIM)
            s = jnp.dot(ws_ref[h].astype(BF16), vn_ref[rows, cols].astype(BF16), preferred_element_type=F32)
            yg_scr[rows, cols] = ug_ref[rows, cols] * (s + bs_ref[h])
    proj = jnp.dot(y.astype(BF16), wout_ref[0:D_SSM, :].astype(BF16), preferred_element_type=F32)
    proj += jnp.dot(yg_scr[...].astype(BF16), wout_ref[D_SSM:, :].astype(BF16), preferred_element_type=F32)
    x1 = x_ref[...] + gate1 * proj
    x1_ref[...] = x1
    h2 = _rmsnorm(x1, nw_ref[...]) * (1.0 + scale2) + shift2
    h2_ref[...] = h2.astype(BF16)
    logits = lax.dot_general(rwt_ref[...], h2, (((1,), (1,)), ((), ())), precision=lax.Precision.HIGHEST,
                             preferred_element_type=F32)
    scores = _sigmoid(logits)
    sel = scores + rb_ref[...]
    sc = [scores[e:e + 1, :] for e in range(N_EXPERTS)]
    sl = [sel[e:e + 1, :] for e in range(N_EXPERTS)]
    gscore = []
    for g in range(N_EXPERT_GROUPS):
        v0, v1, v2, v3 = sl[4 * g:4 * g + 4]
        hi01, lo01 = jnp.maximum(v0, v1), jnp.minimum(v0, v1)
        hi23, lo23 = jnp.maximum(v2, v3), jnp.minimum(v2, v3)
        top1 = jnp.maximum(hi01, hi23)
        top2 = jnp.maximum(jnp.minimum(hi01, hi23), jnp.maximum(lo01, lo23))
        gscore.append(top1 + top2)
    best = gscore[0]
    gidx = jnp.zeros_like(best, dtype=jnp.int32)
    for g in range(1, N_EXPERT_GROUPS):
        upd = gscore[g] > best
        gidx = jnp.where(upd, g, gidx)
        best = jnp.where(upd, gscore[g], best)

    def in_group(vals, k):
        out = vals[k]
        for g in range(1, N_EXPERT_GROUPS):
            out = jnp.where(gidx == g, vals[4 * g + k], out)
        return out

    v = [in_group(sl, k) for k in range(EXPERTS_PER_GROUP)]
    s = [in_group(sc, k) for k in range(EXPERTS_PER_GROUP)]
    w = []
    for k in range(EXPERTS_PER_GROUP):
        rank = jnp.zeros_like(gidx)
        for j in range(EXPERTS_PER_GROUP):
            if j == k:
                continue
            ahead = (v[j] >= v[k]) if j < k else (v[j] > v[k])
            rank = rank + ahead.astype(jnp.int32)
        w.append(jnp.where(rank < 2, s[k], 0.0))
    denom = (w[0] + w[1]) + (w[2] + w[3])
    for e in range(N_EXPERTS):
        g, k = divmod(e, EXPERTS_PER_GROUP)
        gates_ref[e:e + 1, :] = jnp.where(gidx == g, w[k] / denom, 0.0)


def _mix_out(x, yraw, u, ug, vn, mods, l, ssm_d, w_glu, w_s, b_s, w_out, norm2_w, rwt, rb):
    tm = TOKEN_TILE
    tok = lambda n: pl.BlockSpec((tm, n), lambda i: (i, 0))
    lay = lambda *shape: pl.BlockSpec((None,) + shape, lambda i: (l,) + (0,) * len(shape))
    return pl.pallas_call(
        _post_kernel,
        grid=(T_ALL // tm,),
        in_specs=[
            tok(D_MODEL), tok(D_SSM), tok(D_SSM), tok(D_GMLP), tok(D_GMLP),
            pl.BlockSpec((None, None, N_MOD, D_MODEL), lambda i: (l, _mod_row(i, tm), 0, 0)),
            lay(1, D_SSM), lay(D_SSM, D_SSM), lay(GMLP_HEADS, GMLP_CHUNK, GMLP_CHUNK),
            lay(GMLP_HEADS, GMLP_CHUNK, 1), lay(D_MODEL, D_MODEL), lay(1, D_MODEL),
            pl.BlockSpec((N_EXPERTS, D_MODEL), lambda i: (0, 0)),
            pl.BlockSpec((N_EXPERTS, 1), lambda i: (0, 0)),
        ],
        out_specs=[tok(D_MODEL), tok(D_MODEL), pl.BlockSpec((N_EXPERTS, tm), lambda i: (0, i))],
        out_shape=[
            jax.ShapeDtypeStruct((T_ALL, D_MODEL), F32),
            jax.ShapeDtypeStruct((T_ALL, D_MODEL), BF16),
            jax.ShapeDtypeStruct((N_EXPERTS, T_ALL), F32),
        ],
        scratch_shapes=[pltpu.VMEM((tm, D_GMLP), F32)],
        compiler_params=_params("arbitrary"),
        name="mixers_out_router",
    )(x, yraw, u, ug, vn, mods, ssm_d.reshape(DEPTH, 1, D_SSM), w_glu, w_s,
      b_s.reshape(DEPTH, GMLP_HEADS, GMLP_CHUNK, 1), w_out, norm2_w.reshape(DEPTH, 1, D_MODEL), rwt, rb)


def _moe_kernel(h_ref, g_ref, wg_ref, wu_ref, wd_ref, x1_ref, mod_ref, fw_ref, o_ref, *, final):
    e = pl.program_id(1)

    @pl.when(e == 0)
    def _():
        o_ref[...] = jnp.zeros_like(o_ref)

    h = h_ref[...]
    lane = lax.broadcasted_iota(jnp.int32, g_ref.shape, 1)
    gate = jnp.sum(jnp.where(lane == e, g_ref[...], 0.0), axis=1, keepdims=True)
    hg = jnp.dot(h, wg_ref[...].astype(BF16), preferred_element_type=F32)
    hu = jnp.dot(h, wu_ref[...].astype(BF16), preferred_element_type=F32)
    act = hg * _sigmoid(hg) * hu * gate
    o_ref[...] += jnp.dot(act.astype(BF16), wd_ref[...].astype(BF16), preferred_element_type=F32)

    @pl.when(e == N_EXPERTS - 1)
    def _():
        x2 = x1_ref[...] + mod_ref[5:6, :] * o_ref[...]
        o_ref[...] = _rmsnorm(x2, fw_ref[...]) if final else x2


def _experts(h2, gates, x1, mods, l, w_gate, w_up, w_down, final_w, final):
    tm = MOE_TILE
    tok = lambda n: pl.BlockSpec((tm, n), lambda i, e: (i, 0))
    return pl.pallas_call(
        functools.partial(_moe_kernel, final=final),
        grid=(T_ALL // tm, N_EXPERTS),
        in_specs=[
            tok(D_MODEL), tok(N_EXPERTS),
            pl.BlockSpec((None, None, D_MODEL, D_EXPERT), lambda i, e: (l, e, 0, 0)),
            pl.BlockSpec((None, None, D_MODEL, D_EXPERT), lambda i, e: (l, e, 0, 0)),
            pl.BlockSpec((None, None, D_EXPERT, D_MODEL), lambda i, e: (l, e, 0, 0)),
            tok(D_MODEL),
            pl.BlockSpec((None, None, N_MOD, D_MODEL), lambda i, e: (l, _mod_row(i, tm), 0, 0)),
            pl.BlockSpec((1, D_MODEL), lambda i, e: (0, 0)),
        ],
        out_specs=tok(D_MODEL),
        out_shape=jax.ShapeDtypeStruct((T_ALL, D_MODEL), F32),
        compiler_params=_params("arbitrary", "arbitrary"),
        name="experts",
    )(h2, gates, w_gate, w_up, w_down, x1, mods, final_w.reshape(1, D_MODEL))


def _to_chunk_rows(u):
    up = u[:T_PROMPT].reshape(N_PROMPT_SEQ, PROMPT_CHUNKS, SCAN_CHUNK, N_GROUPS, SSM_GROUP)
    up = up.transpose(3, 1, 0, 2, 4).reshape(N_GROUPS, PROMPT_ROWS, CHUNK_WIDTH)
    us = u[T_PROMPT:].reshape(N_SAMPLE_SEQ, SAMPLE_CHUNKS, SCAN_CHUNK, N_GROUPS, SSM_GROUP)
    us = us.transpose(3, 1, 0, 2, 4).reshape(N_GROUPS, SAMPLE_ROWS, CHUNK_WIDTH)
    return jnp.concatenate([up, us], axis=1)


def _from_chunk_rows(y):
    yp = y[:, :PROMPT_ROWS].reshape(N_GROUPS, PROMPT_CHUNKS, N_PROMPT_SEQ, SCAN_CHUNK, SSM_GROUP)
    yp = yp.transpose(2, 1, 3, 0, 4).reshape(T_PROMPT, D_SSM)
    ys = y[:, PROMPT_ROWS:].reshape(N_GROUPS, SAMPLE_CHUNKS, N_SAMPLE_SEQ, SCAN_CHUNK, SSM_GROUP)
    ys = ys.transpose(2, 1, 3, 0, 4).reshape(T_SAMPLE, D_SSM)
    return jnp.concatenate([yp, ys], axis=0)


def _dirs_on_lanes(p):
    p = jnp.moveaxis(p, 1, -2)
    return p.reshape(p.shape[:-2] + (2 * SSM_STATE,))


def kernel(x_prompt, x_sample, c, state_ssm_re, state_ssm_im, c_ctx, norm1_w, norm2_w, w_mod, b_mod, w_in,
           ssm_a_re, ssm_a_im, ssm_log_dt, ssm_b_re, ssm_b_im, ssm_c_re, ssm_c_im, ssm_d, w_glu,
           gmlp_ln_w, gmlp_ln_b, gmlp_w_s, gmlp_b_s, w_out, router_w, router_b, w_gate, w_up, w_down,
           final_norm_w):
    x = jnp.concatenate([x_prompt.reshape(T_PROMPT, D_MODEL), x_sample.reshape(T_SAMPLE, D_MODEL)], axis=0)

    cvec = jnp.concatenate([c_ctx[None, :], c, jnp.zeros((MOD_ROWS - 1 - N_SAMPLE_SEQ, D_MODEL), F32)], axis=0)
    mods = _modulation(cvec, w_mod, b_mod).transpose(0, 2, 1, 3)

    log_dt = jnp.broadcast_to(ssm_log_dt[..., None], ssm_a_re.shape)
    vecs = jnp.stack([_dirs_on_lanes(ssm_a_re), _dirs_on_lanes(ssm_a_im), _dirs_on_lanes(log_dt)], axis=2)
    vecs = jnp.concatenate([vecs, jnp.zeros((DEPTH, N_GROUPS, 5, 2 * SSM_STATE), F32)], axis=2)
    mats = jnp.stack([_dirs_on_lanes(jnp.swapaxes(ssm_b_re, -1, -2)), _dirs_on_lanes(jnp.swapaxes(ssm_b_im, -1, -2)),
                      _dirs_on_lanes(ssm_c_re), _dirs_on_lanes(ssm_c_im)], axis=2)
    wf, cct, tt, a16 = _ssm_operators(vecs, mats)

    rwt = router_w.T
    rb = router_b.reshape(N_EXPERTS, 1)

    new_re, new_im = [], []
    for l in range(DEPTH):
        u, ug, vn = _input_proj(x, mods, l, norm1_w, w_in, gmlp_ln_w, gmlp_ln_b)
        s0 = jnp.concatenate([state_ssm_re[:, l].transpose(2, 0, 1, 3).reshape(N_GROUPS, N_SAMPLE_SEQ, 128),
                              state_ssm_im[:, l].transpose(2, 0, 1, 3).reshape(N_GROUPS, N_SAMPLE_SEQ, 128)], axis=-1)
        s0 = jnp.concatenate([s0, jnp.zeros((N_GROUPS, 8 - N_SAMPLE_SEQ, CHUNK_WIDTH), F32)], axis=1)
        yg, fin = _ssm_scan(_to_chunk_rows(u).astype(BF16), wf, cct, tt, a16, s0, l)
        yraw = _from_chunk_rows(yg)
        fin = fin.reshape(N_GROUPS, N_PROMPT_SEQ, 2, 2, SSM_STATE).transpose(2, 1, 3, 0, 4)
        new_re.append(fin[0])
        new_im.append(fin[1])
        x1, h2, gates_t = _mix_out(x, yraw, u, ug, vn, mods, l, ssm_d, w_glu, gmlp_w_s, gmlp_b_s, w_out,
                                   norm2_w, rwt, rb)
        x = _experts(h2, gates_t.T, x1, mods, l, w_gate, w_up, w_down, final_norm_w, final=(l == DEPTH - 1))

    y_prompt = x[:T_PROMPT].reshape(N_PROMPT_SEQ, PROMPT_LEN, D_MODEL)
    y_sample = x[T_PROMPT:].reshape(N_SAMPLE_SEQ, SAMPLE_LEN, D_MODEL)
    return (y_prompt, y_sample, jnp.stack(new_re, axis=1), jnp.stack(new_im, axis=1))
```

```python
import functools
import math

import jax
import jax.numpy as jnp
from jax import lax
from jax.experimental import pallas as pl
from jax.experimental.pallas import tpu as pltpu

F32 = jnp.float32
BF16 = jnp.bfloat16

D_MODEL = 1024
N_PROMPT_SEQ = 16
PROMPT_LEN = 256
N_SAMPLE_SEQ = 2
SAMPLE_LEN = 1024
T_PROMPT = N_PROMPT_SEQ * PROMPT_LEN
T_SAMPLE = N_SAMPLE_SEQ * SAMPLE_LEN
T_ALL = T_PROMPT + T_SAMPLE
DEPTH = 2
D_SSM = 512
SSM_GROUP = 16
N_GROUPS = 32
SSM_STATE = 64
D_GMLP = 512
GMLP_HEADS = 4
GMLP_HEAD_DIM = 128
GMLP_CHUNK = 128
N_EXPERTS = 16
N_EXPERT_GROUPS = 4
EXPERTS_PER_GROUP = 4
D_EXPERT = 512
N_MOD = 6
EPS = 1e-6

SCAN_CHUNK = 16
CHUNK_WIDTH = SCAN_CHUNK * SSM_GROUP
SEG_LEN = 256
SEG_CHUNKS = SEG_LEN // SCAN_CHUNK
N_PS = 8
TILE_TOKENS = N_PS * SEG_LEN
N_TILES = T_ALL // TILE_TOKENS
SAMPLE_TILE = T_PROMPT // TILE_TOKENS
SEGS_PER_SAMPLE_SEQ = SAMPLE_LEN // SEG_LEN
J_ROWS = SEG_CHUNKS * N_PS
GROUP_BLOCK = 8
MOD_ROWS = 8

J_PER_STEP = 2
TOKEN_TILE = J_PER_STEP * J_ROWS
MOE_TILE = 1024
VMEM_LIMIT = 56 * 1024 * 1024
TRANS_B = (((1,), (1,)), ((), ()))


def _sigmoid(x):
    return 1.0 / (1.0 + jnp.exp(-x))


def _gelu_tanh(x):
    c = math.sqrt(2.0 / math.pi)
    return x * (0.5 * (1.0 + jnp.tanh(c * (x + 0.044715 * (x * x * x)))))


def _rmsnorm(x, w):
    return x * lax.rsqrt(jnp.mean(x * x, axis=-1, keepdims=True) + EPS) * w


def _per_ps(fn, a, *mods):
    rows, d = a.shape
    out = fn(a.reshape(rows // N_PS, N_PS, d), *[m[None] for m in mods])
    return out.reshape(rows, d)


def _params(*sem):
    return pltpu.CompilerParams(dimension_semantics=sem, vmem_limit_bytes=VMEM_LIMIT)


def _mod_kernel(c_ref, w_ref, b_ref, o_ref):
    c = c_ref[...]
    s = c * _sigmoid(c)
    o_ref[...] = jnp.dot(s.astype(BF16), w_ref[...].astype(BF16), preferred_element_type=F32) + b_ref[...]


def _modulation(cvec, w_mod, b_mod):
    return pl.pallas_call(
        _mod_kernel,
        grid=(DEPTH, N_MOD),
        in_specs=[
            pl.BlockSpec((MOD_ROWS, D_MODEL), lambda l, n: (0, 0)),
            pl.BlockSpec((None, D_MODEL, D_MODEL), lambda l, n: (l, 0, n)),
            pl.BlockSpec((None, None, 1, D_MODEL), lambda l, n: (l, n, 0, 0)),
        ],
        out_specs=pl.BlockSpec((None, None, MOD_ROWS, D_MODEL), lambda l, n: (l, n, 0, 0)),
        out_shape=jax.ShapeDtypeStruct((DEPTH, N_MOD, MOD_ROWS, D_MODEL), F32),
        compiler_params=_params("arbitrary", "arbitrary"),
        name="adaln_mod",
    )(cvec, w_mod, b_mod.reshape(DEPTH, N_MOD, 1, D_MODEL))


def _in_kernel(x_ref, mod_ref, nw_ref, wg_ref, wst_ref, lnw_ref, lnb_ref, ug_ref, vn_ref, xt_ref):
    y = _rmsnorm(x_ref[...], nw_ref[...])
    h = _per_ps(lambda a, sc, sh: a * (1.0 + sc) + sh, y, mod_ref[1], mod_ref[0]).astype(BF16)
    zg = _gelu_tanh(jnp.dot(h, wg_ref[...].astype(BF16), preferred_element_type=F32))
    v = zg[:, D_GMLP:]
    mu = jnp.mean(v, axis=-1, keepdims=True)
    vc = v - mu
    var = jnp.mean(vc * vc, axis=-1, keepdims=True)
    vn = vc * lax.rsqrt(var + EPS) * lnw_ref[...] + lnb_ref[...]
    for hd in range(GMLP_HEADS):
        cols = slice(hd * GMLP_HEAD_DIM, (hd + 1) * GMLP_HEAD_DIM)
        ug_ref[hd] = zg[:, cols]
        vn_ref[hd] = vn[:, cols]
    wst = wst_ref[...].astype(BF16)
    for k in range(J_PER_STEP):
        hk = h[k * J_ROWS:(k + 1) * J_ROWS]
        xt_ref[k] = lax.dot_general(wst, hk, TRANS_B, preferred_element_type=F32).astype(BF16)


def _input_proj(x, mod8, l, norm1_w, w_in_g, w_in_st, ln_w, ln_b):
    tm = TOKEN_TILE
    steps = TILE_TOKENS // tm
    tok = lambda n: pl.BlockSpec((tm, n), lambda t, s: (t * steps + s, 0))
    heads = pl.BlockSpec((GMLP_HEADS, tm, GMLP_HEAD_DIM), lambda t, s: (0, t * steps + s, 0))
    lay = lambda *shape: pl.BlockSpec((None,) + shape, lambda t, s: (l,) + (0,) * len(shape))
    return pl.pallas_call(
        _in_kernel,
        grid=(N_TILES, steps),
        in_specs=[
            tok(D_MODEL),
            pl.BlockSpec((None, None, N_MOD, N_PS, D_MODEL), lambda t, s: (l, t, 0, 0, 0)),
            lay(1, D_MODEL), lay(D_MODEL, 2 * D_GMLP), lay(D_SSM, D_MODEL), lay(1, D_GMLP), lay(1, D_GMLP),
        ],
        out_specs=[heads, heads,
                   pl.BlockSpec((None, J_PER_STEP, D_SSM, J_ROWS), lambda t, s: (t, s, 0, 0))],
        out_shape=[jax.ShapeDtypeStruct((GMLP_HEADS, T_ALL, GMLP_HEAD_DIM), F32)] * 2 + [
            jax.ShapeDtypeStruct((N_TILES, SCAN_CHUNK, D_SSM, J_ROWS), BF16)],
        compiler_params=_params("arbitrary", "arbitrary"),
        name="norm1_in_proj",
    )(x, mod8, norm1_w.reshape(DEPTH, 1, D_MODEL), w_in_g, w_in_st,
      ln_w.reshape(DEPTH, 1, D_GMLP), ln_b.reshape(DEPTH, 1, D_GMLP))


def _shift_lanes_right(a, b, s, lane):
    if s == 0:
        return a, b
    if s == 128:
        return jnp.zeros_like(a), a
    if s < 128:
        ra = pltpu.roll(a, s, 1)
        rb = pltpu.roll(b, s, 1)
        return jnp.where(lane >= s, ra, 0.0), jnp.where(lane >= s, rb, ra)
    t = s - 128
    return jnp.zeros_like(a), jnp.where(lane >= t, pltpu.roll(a, t, 1), 0.0)


def _shift_lanes_left(a, b, s, lane):
    if s == 0:
        return a, b
    if s == 128:
        return b, jnp.zeros_like(b)
    if s < 128:
        ra = pltpu.roll(a, 128 - s, 1)
        rb = pltpu.roll(b, 128 - s, 1)
        return jnp.where(lane < 128 - s, ra, rb), jnp.where(lane < 128 - s, rb, 0.0)
    t = s - 128
    return jnp.where(lane < 128 - t, pltpu.roll(b, 128 - t, 1), 0.0), jnp.zeros_like(b)


def _prep_kernel(vec_ref, mat_ref, wft_ref, cct_ref, ttt_ref, a16_ref, wf_scr, cm_scr, tt_scr):
    a_re = vec_ref[0:1, :]
    a_im = vec_ref[1:2, :]
    dt = jnp.exp(vec_ref[2:3, :])
    d_skip = vec_ref[3:4, :]
    mag = jnp.exp(a_re * dt)
    ang = a_im * dt
    ab_r = mag * jnp.cos(ang)
    ab_i = mag * jnp.sin(ang)
    den = a_re * a_re + a_im * a_im
    nr = ab_r - 1.0
    q_r = (nr * a_re + ab_i * a_im) / den
    q_i = (ab_i * a_re - nr * a_im) / den
    bt_r = mat_ref[0]
    bt_i = mat_ref[1]
    c_r = mat_ref[2]
    c_i = mat_ref[3]
    bb_r = q_r * bt_r - q_i * bt_i
    bb_i = q_r * bt_i + q_i * bt_r
    p_r = [jnp.ones_like(ab_r)]
    p_i = [jnp.zeros_like(ab_r)]
    for _ in range(SCAN_CHUNK):
        pr, pi = p_r[-1], p_i[-1]
        p_r.append(pr * ab_r - pi * ab_i)
        p_i.append(pr * ab_i + pi * ab_r)
    a16_ref[0] = jnp.broadcast_to(p_r[SCAN_CHUNK], (N_PS, 128))
    a16_ref[1] = jnp.broadcast_to(p_i[SCAN_CHUNK], (N_PS, 128))

    lane = lax.broadcasted_iota(jnp.int32, (1, 128), 1)
    is_fwd = lane < SSM_STATE

    def pick(mf, mb):
        return jnp.where(is_fwd, p_r[mf], p_r[mb]), jnp.where(is_fwd, p_i[mf], p_i[mb])

    for j in range(SCAN_CHUNK):
        rows = slice(j * SSM_GROUP, (j + 1) * SSM_GROUP)
        wr, wi = pick(SCAN_CHUNK - 1 - j, j)
        wf_scr[rows, 0:128] = bb_r * wr - bb_i * wi
        wf_scr[rows, 128:256] = bb_r * wi + bb_i * wr
        wr, wi = pick(j + 1, SCAN_CHUNK - j)
        cct_ref[rows, 0:128] = (c_r * wr - c_i * wi).astype(BF16)
        cct_ref[rows, 128:256] = (-(c_r * wi + c_i * wr)).astype(BF16)
        wr, wi = pick(j, SCAN_CHUNK - 1 - j)
        cm_scr[rows, 0:128] = c_r * wr - c_i * wi
        cm_scr[rows, 128:256] = c_r * wi + c_i * wr
    wft_ref[...] = wf_scr[...].T.astype(BF16)

    zero = jnp.zeros_like(bb_r)
    cm = cm_scr[...]

    def lag_rows(keep):
        lhs = jnp.concatenate([jnp.where(keep, bb_r, zero), jnp.where(keep, -bb_i, zero)], axis=1)
        return lax.dot_general(lhs, cm, TRANS_B, precision=lax.Precision.HIGHEST, preferred_element_type=F32)

    mf = lag_rows(is_fwd)
    mb = lag_rows(jnp.logical_not(is_fwd))
    mf_a, mf_b = mf[:, 0:128], mf[:, 128:256]
    mb_a, mb_b = mb[:, 0:128], mb[:, 128:256]
    row_h = lax.broadcasted_iota(jnp.int32, (SSM_GROUP, 128), 0)
    lane_h = lax.broadcasted_iota(jnp.int32, (SSM_GROUP, 128), 1)
    for jp in range(SCAN_CHUNK):
        rows = slice(jp * SSM_GROUP, (jp + 1) * SSM_GROUP)
        fa, fb = _shift_lanes_right(mf_a, mf_b, SSM_GROUP * jp, lane)
        ba, bb = _shift_lanes_left(mb_a, mb_b, SSM_GROUP * (SCAN_CHUNK - 1 - jp), lane)
        diag = SSM_GROUP * jp + row_h
        tt_scr[rows, 0:128] = fa + ba + jnp.where(lane_h == diag, d_skip, 0.0)
        tt_scr[rows, 128:256] = fb + bb + jnp.where(lane_h + 128 == diag, d_skip, 0.0)
    ttt_ref[...] = tt_scr[...].T.astype(BF16)


def _ssm_operators(vecs, mats):
    op = jax.ShapeDtypeStruct((DEPTH, N_GROUPS, CHUNK_WIDTH, CHUNK_WIDTH), BF16)
    op_spec = pl.BlockSpec((None, None, CHUNK_WIDTH, CHUNK_WIDTH), lambda l, g: (l, g, 0, 0))
    sq = pltpu.VMEM((CHUNK_WIDTH, CHUNK_WIDTH), F32)
    return pl.pallas_call(
        _prep_kernel,
        grid=(DEPTH, N_GROUPS),
        in_specs=[
            pl.BlockSpec((None, None, 8, 128), lambda l, g: (l, g, 0, 0)),
            pl.BlockSpec((None, None, 4, SSM_GROUP, 128), lambda l, g: (l, g, 0, 0, 0)),
        ],
        out_specs=[op_spec, op_spec, op_spec,
                   pl.BlockSpec((None, None, 2, N_PS, 128), lambda l, g: (l, g, 0, 0, 0))],
        out_shape=[op, op, op, jax.ShapeDtypeStruct((DEPTH, N_GROUPS, 2, N_PS, 128), F32)],
        scratch_shapes=[sq, sq, sq],
        compiler_params=_params("arbitrary", "arbitrary"),
        name="s5_operators",
    )(vecs, mats)


def _ssm_kernel(xt_ref, wft_ref, cct_ref, ttt_ref, a16_ref, s0_ref, yt_ref, fin_ref, s_scr, ft_scr):
    tile = pl.program_id(0)
    lane = lax.broadcasted_iota(jnp.int32, (N_PS, 128), 1)
    is_fwd = lane < SSM_STATE
    half = SSM_STATE

    def group(gl, carry):
        ch0 = pl.multiple_of(gl * SSM_GROUP, SSM_GROUP)
        xt = xt_ref[:, pl.ds(ch0, SSM_GROUP), :].reshape(CHUNK_WIDTH, J_ROWS)
        ft_scr[...] = jnp.dot(wft_ref[gl], xt, preferred_element_type=F32)
        f = ft_scr[...].T
        a_r = a16_ref[gl, 0]
        a_i = a16_ref[gl, 1]

        def scan(s_r, s_i):
            for i in range(SEG_CHUNKS):
                rf = slice(i * N_PS, (i + 1) * N_PS)
                rb = slice((SEG_CHUNKS - 1 - i) * N_PS, (SEG_CHUNKS - i) * N_PS)
                s_scr[rf, 0:half] = s_r[:, 0:half]
                s_scr[rf, 128:128 + half] = s_i[:, 0:half]
                s_scr[rb, half:128] = s_r[:, half:128]
                s_scr[rb, 128 + half:256] = s_i[:, half:128]
                f_r = jnp.where(is_fwd, f[rf, 0:128], f[rb, 0:128])
                f_i = jnp.where(is_fwd, f[rf, 128:256], f[rb, 128:256])
                s_r, s_i = a_r * s_r - a_i * s_i + f_r, a_r * s_i + a_i * s_r + f_i
            return s_r, s_i

        zeros = jnp.zeros((N_PS, 128), F32)
        z_r, z_i = scan(zeros, zeros)
        fin_ref[gl, :, 0:128] = z_r
        fin_ref[gl, :, 128:256] = z_i

        @pl.when(tile == SAMPLE_TILE)
        def _():
            b_r, b_i = a_r, a_i
            for _ in range(4):
                b_r, b_i = b_r * b_r - b_i * b_i, 2.0 * (b_r * b_i)
            seg = lax.broadcasted_iota(jnp.int32, (N_PS, 128), 0) % SEGS_PER_SAMPLE_SEQ
            i_r = s0_ref[gl, :, 0:128]
            i_i = s0_ref[gl, :, 128:256]
            for step in range(1, SEGS_PER_SAMPLE_SEQ):
                pr = jnp.where(is_fwd, pltpu.roll(i_r, 1, 0), pltpu.roll(i_r, N_PS - 1, 0))
                pi = jnp.where(is_fwd, pltpu.roll(i_i, 1, 0), pltpu.roll(i_i, N_PS - 1, 0))
                zr = jnp.where(is_fwd, pltpu.roll(z_r, 1, 0), pltpu.roll(z_r, N_PS - 1, 0))
                zi = jnp.where(is_fwd, pltpu.roll(z_i, 1, 0), pltpu.roll(z_i, N_PS - 1, 0))
                n_r = b_r * pr - b_i * pi + zr
                n_i = b_r * pi + b_i * pr + zi
                first = jnp.where(is_fwd, step, 0)
                last = jnp.where(is_fwd, SEGS_PER_SAMPLE_SEQ - 1, SEGS_PER_SAMPLE_SEQ - 1 - step)
                upd = jnp.logical_and(seg >= first, seg <= last)
                i_r = jnp.where(upd, n_r, i_r)
                i_i = jnp.where(upd, n_i, i_i)
            scan(i_r, i_i)

        yt = jnp.dot(ttt_ref[gl], xt, preferred_element_type=F32)
        yt += lax.dot_general(cct_ref[gl], s_scr[...].astype(BF16), TRANS_B, preferred_element_type=F32)
        yt_ref[:, pl.ds(ch0, SSM_GROUP), :] = yt.reshape(SCAN_CHUNK, SSM_GROUP, J_ROWS)
        return carry

    lax.fori_loop(0, GROUP_BLOCK, group, 0)


def _ssm_scan(xt, wft, cct, ttt, a16, s0, l):
    gb = GROUP_BLOCK
    op_spec = pl.BlockSpec((None, gb, CHUNK_WIDTH, CHUNK_WIDTH), lambda t, g: (l, g, 0, 0))
    io_spec = pl.BlockSpec((None, SCAN_CHUNK, gb * SSM_GROUP, J_ROWS), lambda t, g: (t, 0, g, 0))
    return pl.pallas_call(
        _ssm_kernel,
        grid=(N_TILES, N_GROUPS // gb),
        in_specs=[
            io_spec, op_spec, op_spec, op_spec,
            pl.BlockSpec((None, gb, 2, N_PS, 128), lambda t, g: (l, g, 0, 0, 0)),
            pl.BlockSpec((gb, N_PS, CHUNK_WIDTH), lambda t, g: (g, 0, 0)),
        ],
        out_specs=[io_spec, pl.BlockSpec((None, gb, N_PS, CHUNK_WIDTH), lambda t, g: (t, g, 0, 0))],
        out_shape=[
            jax.ShapeDtypeStruct((N_TILES, SCAN_CHUNK, D_SSM, J_ROWS), F32),
            jax.ShapeDtypeStruct((N_TILES, N_GROUPS, N_PS, CHUNK_WIDTH), F32),
        ],
        scratch_shapes=[pltpu.VMEM((J_ROWS, CHUNK_WIDTH), F32), pltpu.VMEM((CHUNK_WIDTH, J_ROWS), F32)],
        compiler_params=_params("arbitrary", "arbitrary"),
        name="s5_chunk_scan",
    )(xt, wft, cct, ttt, a16, s0)


def _post_kernel(x_ref, yt_ref, ug_ref, vn_ref, mod_ref, wglut_ref, ws_ref, bs_ref, wout_ref,
                 nw_ref, rwt_ref, rb_ref, x1_ref, h2_ref, gates_ref, yg_scr, y_scr):
    step = pl.program_id(1)

    @pl.when(step == 0)
    def _():
        def chunk(n, carry):
            ps = n % N_PS
            c_hi = n // N_PS
            base = c_hi * (GMLP_CHUNK // SCAN_CHUNK) * N_PS + ps
            rows = [pl.ds(j * J_ROWS + base, GMLP_CHUNK // SCAN_CHUNK, stride=N_PS) for j in range(SCAN_CHUNK)]
            for h in range(GMLP_HEADS):
                v = jnp.concatenate([vn_ref[h, r, :] for r in rows], axis=0).astype(BF16)
                u = jnp.concatenate([ug_ref[h, r, :] for r in rows], axis=0)
                s = jnp.dot(ws_ref[h].astype(BF16), v, preferred_element_type=F32) + bs_ref[h]
                yg = u * s
                for j, r in enumerate(rows):
                    yg_scr[h, r, :] = yg[j * 8:(j + 1) * 8]
            return carry

        lax.fori_loop(0, TILE_TOKENS // GMLP_CHUNK, chunk, 0)

    wglut = wglut_ref[...].astype(BF16)
    for k in range(J_PER_STEP):
        yt = _gelu_tanh(yt_ref[k])
        yt = yt * _sigmoid(jnp.dot(wglut, yt.astype(BF16), preferred_element_type=F32))
        y_scr[k * J_ROWS:(k + 1) * J_ROWS, :] = yt.T
    row0 = pl.multiple_of(step * TOKEN_TILE, TOKEN_TILE)
    proj = jnp.dot(y_scr[...].astype(BF16), wout_ref[0:D_SSM, :].astype(BF16), preferred_element_type=F32)
    yg = jnp.concatenate([yg_scr[h, pl.ds(row0, TOKEN_TILE), :] for h in range(GMLP_HEADS)], axis=1)
    proj += jnp.dot(yg.astype(BF16), wout_ref[D_SSM:, :].astype(BF16), preferred_element_type=F32)
    x1 = x_ref[...] + _per_ps(lambda a, g: a * g, proj, mod_ref[2])
    x1_ref[...] = x1
    h2 = _per_ps(lambda a, sc, sh: a * (1.0 + sc) + sh, _rmsnorm(x1, nw_ref[...]), mod_ref[4], mod_ref[3])
    h2_ref[...] = h2.astype(BF16)
    logits = lax.dot_general(rwt_ref[...], h2, TRANS_B, precision=lax.Precision.HIGHEST,
                             preferred_element_type=F32)
    scores = _sigmoid(logits)
    sel = scores + rb_ref[...]
    sc = [scores[e:e + 1, :] for e in range(N_EXPERTS)]
    sl = [sel[e:e + 1, :] for e in range(N_EXPERTS)]
    gscore = []
    for g in range(N_EXPERT_GROUPS):
        v0, v1, v2, v3 = sl[4 * g:4 * g + 4]
        hi01, lo01 = jnp.maximum(v0, v1), jnp.minimum(v0, v1)
        hi23, lo23 = jnp.maximum(v2, v3), jnp.minimum(v2, v3)
        top1 = jnp.maximum(hi01, hi23)
        top2 = jnp.maximum(jnp.minimum(hi01, hi23), jnp.maximum(lo01, lo23))
        gscore.append(top1 + top2)
    best = gscore[0]
    gidx = jnp.zeros_like(best, dtype=jnp.int32)
    for g in range(1, N_EXPERT_GROUPS):
        upd = gscore[g] > best
        gidx = jnp.where(upd, g, gidx)
        best = jnp.where(upd, gscore[g], best)

    def in_group(vals, k):
        out = vals[k]
        for g in range(1, N_EXPERT_GROUPS):
            out = jnp.where(gidx == g, vals[4 * g + k], out)
        return out

    v = [in_group(sl, k) for k in range(EXPERTS_PER_GROUP)]
    s = [in_group(sc, k) for k in range(EXPERTS_PER_GROUP)]
    w = []
    for k in range(EXPERTS_PER_GROUP):
        rank = jnp.zeros_like(gidx)
        for j in range(EXPERTS_PER_GROUP):
            if j == k:
                continue
            ahead = (v[j] >= v[k]) if j < k else (v[j] > v[k])
            rank = rank + ahead.astype(jnp.int32)
        w.append(jnp.where(rank < 2, s[k], 0.0))
    denom = (w[0] + w[1]) + (w[2] + w[3])
    for e in range(N_EXPERTS):
        g, k = divmod(e, EXPERTS_PER_GROUP)
        gates_ref[e:e + 1, :] = jnp.where(gidx == g, w[k] / denom, 0.0)


def _mix_out(x, yt, ug, vn, mod8, l, w_glu_t, w_s, b_s, w_out, norm2_w, rwt, rb):
    tm = TOKEN_TILE
    steps = TILE_TOKENS // tm
    tok = lambda n: pl.BlockSpec((tm, n), lambda t, s: (t * steps + s, 0))
    whole_tile = pl.BlockSpec((GMLP_HEADS, TILE_TOKENS, GMLP_HEAD_DIM), lambda t, s: (0, t, 0))
    lay = lambda *shape: pl.BlockSpec((None,) + shape, lambda t, s: (l,) + (0,) * len(shape))
    return pl.pallas_call(
        _post_kernel,
        grid=(N_TILES, steps),
        in_specs=[
            tok(D_MODEL),
            pl.BlockSpec((None, J_PER_STEP, D_SSM, J_ROWS), lambda t, s: (t, s, 0, 0)),
            whole_tile, whole_tile,
            pl.BlockSpec((None, None, N_MOD, N_PS, D_MODEL), lambda t, s: (l, t, 0, 0, 0)),
            lay(D_SSM, D_SSM), lay(GMLP_HEADS, GMLP_CHUNK, GMLP_CHUNK),
            lay(GMLP_HEADS, GMLP_CHUNK, 1), lay(D_MODEL, D_MODEL), lay(1, D_MODEL),
            pl.BlockSpec((N_EXPERTS, D_MODEL), lambda t, s: (0, 0)),
            pl.BlockSpec((N_EXPERTS, 1), lambda t, s: (0, 0)),
        ],
        out_specs=[tok(D_MODEL), tok(D_MODEL), pl.BlockSpec((N_EXPERTS, tm), lambda t, s: (0, t * steps + s))],
        out_shape=[
            jax.ShapeDtypeStruct((T_ALL, D_MODEL), F32),
            jax.ShapeDtypeStruct((T_ALL, D_MODEL), BF16),
            jax.ShapeDtypeStruct((N_EXPERTS, T_ALL), F32),
        ],
        scratch_shapes=[pltpu.VMEM((GMLP_HEADS, TILE_TOKENS, GMLP_HEAD_DIM), F32), pltpu.VMEM((tm, D_SSM), F32)],
        compiler_params=_params("arbitrary", "arbitrary"),
        name="mixers_out_router",
    )(x, yt, ug, vn, mod8, w_glu_t, w_s, b_s, w_out, norm2_w.reshape(DEPTH, 1, D_MODEL), rwt, rb)


def _moe_kernel(h_ref, g_ref, wg_ref, wu_ref, wd_ref, x1_ref, mod_ref, fw_ref, o_ref, *, final):
    e = pl.program_id(1)

    @pl.when(e == 0)
    def _():
        o_ref[...] = jnp.zeros_like(o_ref)

    h = h_ref[...]
    lane = lax.broadcasted_iota(jnp.int32, g_ref.shape, 1)
    gate = jnp.sum(jnp.where(lane == e, g_ref[...], 0.0), axis=1, keepdims=True)
    hg = jnp.dot(h, wg_ref[...].astype(BF16), preferred_element_type=F32)
    hu = jnp.dot(h, wu_ref[...].astype(BF16), preferred_element_type=F32)
    act = hg * _sigmoid(hg) * hu * gate
    o_ref[...] += jnp.dot(act.astype(BF16), wd_ref[...].astype(BF16), preferred_element_type=F32)

    @pl.when(e == N_EXPERTS - 1)
    def _():
        x2 = x1_ref[...] + _per_ps(lambda a, g: a * g, o_ref[...], mod_ref[5])
        o_ref[...] = _rmsnorm(x2, fw_ref[...]) if final else x2


def _experts(h2, gates, x1, mod8, l, w_gate, w_up, w_down, final_w, final):
    tm = MOE_TILE
    per_tile = TILE_TOKENS // tm
    tok = lambda n: pl.BlockSpec((tm, n), lambda i, e: (i, 0))
    return pl.pallas_call(
        functools.partial(_moe_kernel, final=final),
        grid=(T_ALL // tm, N_EXPERTS),
        in_specs=[
            tok(D_MODEL), tok(N_EXPERTS),
            pl.BlockSpec((None, None, D_MODEL, D_EXPERT), lambda i, e: (l, e, 0, 0)),
            pl.BlockSpec((None, None, D_MODEL, D_EXPERT), lambda i, e: (l, e, 0, 0)),
            pl.BlockSpec((None, None, D_EXPERT, D_MODEL), lambda i, e: (l, e, 0, 0)),
            tok(D_MODEL),
            pl.BlockSpec((None, None, N_MOD, N_PS, D_MODEL), lambda i, e: (l, i // per_tile, 0, 0, 0)),
            pl.BlockSpec((1, D_MODEL), lambda i, e: (0, 0)),
        ],
        out_specs=tok(D_MODEL),
        out_shape=jax.ShapeDtypeStruct((T_ALL, D_MODEL), F32),
        compiler_params=_params("arbitrary", "arbitrary"),
        name="experts",
    )(h2, gates, w_gate, w_up, w_down, x1, mod8, final_w.reshape(1, D_MODEL))


def _dirs_on_lanes(p):
    p = jnp.moveaxis(p, 1, -2)
    return p.reshape(p.shape[:-2] + (2 * SSM_STATE,))


def _to_internal_order(x):
    x = x.reshape(N_TILES, N_PS, SEG_CHUNKS, SCAN_CHUNK, D_MODEL)
    return x.transpose(0, 3, 2, 1, 4).reshape(T_ALL, D_MODEL)


def _from_internal_order(x):
    x = x.reshape(N_TILES, SCAN_CHUNK, SEG_CHUNKS, N_PS, D_MODEL)
    return x.transpose(0, 3, 2, 1, 4).reshape(T_ALL, D_MODEL)


def _gmlp_position_order(w):
    n_lo = GMLP_CHUNK // SCAN_CHUNK
    lead = w.shape[:2]
    w = w.reshape(lead + (n_lo, SCAN_CHUNK) + w.shape[3:])
    w = jnp.swapaxes(w, 2, 3)
    return w.reshape(lead + (GMLP_CHUNK,) + w.shape[4:])


def kernel(x_prompt, x_sample, c, state_ssm_re, state_ssm_im, c_ctx, norm1_w, norm2_w, w_mod, b_mod, w_in,
           ssm_a_re, ssm_a_im, ssm_log_dt, ssm_b_re, ssm_b_im, ssm_c_re, ssm_c_im, ssm_d, w_glu,
           gmlp_ln_w, gmlp_ln_b, gmlp_w_s, gmlp_b_s, w_out, router_w, router_b, w_gate, w_up, w_down,
           final_norm_w):
    x = jnp.concatenate([x_prompt.reshape(T_PROMPT, D_MODEL), x_sample.reshape(T_SAMPLE, D_MODEL)], axis=0)
    x = _to_internal_order(x)

    cvec = jnp.concatenate([c_ctx[None, :], c, jnp.zeros((MOD_ROWS - 1 - N_SAMPLE_SEQ, D_MODEL), F32)], axis=0)
    mods = _modulation(cvec, w_mod, b_mod)
    ps_row = [[0] * N_PS] * SAMPLE_TILE + [[1 + p // SEGS_PER_SAMPLE_SEQ for p in range(N_PS)]]
    mod8 = mods[:, :, jnp.array(ps_row, jnp.int32), :].transpose(0, 2, 1, 3, 4)

    log_dt = jnp.broadcast_to(ssm_log_dt[..., None], ssm_a_re.shape)
    d_lanes = jnp.tile(ssm_d.reshape(DEPTH, N_GROUPS, SSM_GROUP), (1, 1, 128 // SSM_GROUP))
    vecs = jnp.stack([_dirs_on_lanes(ssm_a_re), _dirs_on_lanes(ssm_a_im), _dirs_on_lanes(log_dt), d_lanes], axis=2)
    vecs = jnp.concatenate([vecs, jnp.zeros((DEPTH, N_GROUPS, 4, 2 * SSM_STATE), F32)], axis=2)
    mats = jnp.stack([_dirs_on_lanes(jnp.swapaxes(ssm_b_re, -1, -2)), _dirs_on_lanes(jnp.swapaxes(ssm_b_im, -1, -2)),
                      _dirs_on_lanes(ssm_c_re), _dirs_on_lanes(ssm_c_im)], axis=2)
    wft, cct, ttt, a16 = _ssm_operators(vecs, mats)

    w_in_g = w_in[:, :, D_SSM:]
    w_in_st = jnp.swapaxes(w_in[:, :, :D_SSM], 1, 2)
    w_glu_t = jnp.swapaxes(w_glu, 1, 2)
    w_s = jnp.swapaxes(_gmlp_position_order(jnp.swapaxes(_gmlp_position_order(gmlp_w_s), 2, 3)), 2, 3)
    b_s = _gmlp_position_order(gmlp_b_s)[..., None]
    rwt = router_w.T
    rb = router_b.reshape(N_EXPERTS, 1)

    new_re, new_im = [], []
    for l in range(DEPTH):
        ug, vn, xt = _input_proj(x, mod8, l, norm1_w, w_in_g, w_in_st, gmlp_ln_w, gmlp_ln_b)
        s0 = jnp.concatenate([state_ssm_re[:, l].transpose(2, 0, 1, 3).reshape(N_GROUPS, N_SAMPLE_SEQ, 128),
                              state_ssm_im[:, l].transpose(2, 0, 1, 3).reshape(N_GROUPS, N_SAMPLE_SEQ, 128)], axis=-1)
        s0 = jnp.repeat(s0, SEGS_PER_SAMPLE_SEQ, axis=1)
        yt, fin = _ssm_scan(xt, wft, cct, ttt, a16, s0, l)
        fin = fin[:SAMPLE_TILE].reshape(SAMPLE_TILE, N_GROUPS, N_PS, 2, 2, SSM_STATE)
        fin = fin.transpose(3, 0, 2, 4, 1, 5).reshape(2, N_PROMPT_SEQ, 2, N_GROUPS, SSM_STATE)
        new_re.append(fin[0])
        new_im.append(fin[1])
        x1, h2, gates_t = _mix_out(x, yt, ug, vn, mod8, l, w_glu_t, w_s, b_s, w_out, norm2_w, rwt, rb)
        x = _experts(h2, gates_t.T, x1, mod8, l, w_gate, w_up, w_down, final_norm_w, final=(l == DEPTH - 1))

    x = _from_internal_order(x)
    y_prompt = x[:T_PROMPT].reshape(N_PROMPT_SEQ, PROMPT_LEN, D_MODEL)
    y_sample = x[T_PROMPT:].reshape(N_SAMPLE_SEQ, SAMPLE_LEN, D_MODEL)
    return (y_prompt, y_sample, jnp.stack(new_re, axis=1), jnp.stack(new_im, axis=1))
```

```python
import functools
import math

import jax
import jax.numpy as jnp
import numpy as np
from jax import lax
from jax.experimental import pallas as pl
from jax.experimental.pallas import tpu as pltpu

F32 = jnp.float32
BF16 = jnp.bfloat16

D_MODEL = 1024
N_PROMPT_SEQ = 16
PROMPT_LEN = 256
N_SAMPLE_SEQ = 2
SAMPLE_LEN = 1024
T_PROMPT = N_PROMPT_SEQ * PROMPT_LEN
T_SAMPLE = N_SAMPLE_SEQ * SAMPLE_LEN
T_ALL = T_PROMPT + T_SAMPLE
DEPTH = 2
D_SSM = 512
SSM_GROUP = 16
N_GROUPS = 32
SSM_STATE = 64
D_GMLP = 512
GMLP_HEADS = 4
GMLP_HEAD_DIM = 128
GMLP_CHUNK = 128
N_EXPERTS = 16
N_EXPERT_GROUPS = 4
EXPERTS_PER_GROUP = 4
D_EXPERT = 512
N_MOD = 6
EPS = 1e-6

SCAN_CHUNK = 16
CHUNK_WIDTH = SCAN_CHUNK * SSM_GROUP
SEG_LEN = 256
SEG_CHUNKS = SEG_LEN // SCAN_CHUNK
N_PS = 8
TILE_TOKENS = N_PS * SEG_LEN
N_TILES = T_ALL // TILE_TOKENS
SAMPLE_TILE = T_PROMPT // TILE_TOKENS
SEGS_PER_SAMPLE_SEQ = SAMPLE_LEN // SEG_LEN
J_ROWS = SEG_CHUNKS * N_PS
GROUP_BLOCK = 8
MOD_ROWS = 8

J_PER_STEP = 2
TOKEN_TILE = J_PER_STEP * J_ROWS
PAIR_SLOT_A = (0, 0, 0, 1, 1, 3)
PAIR_SLOT_B = (1, 2, 3, 3, 2, 2)
MOE_TM = 128
MOE_TILES = T_ALL // MOE_TM + N_EXPERT_GROUPS * len(PAIR_SLOT_A)
MOE_ROWS = MOE_TILES * MOE_TM
VMEM_LIMIT = 56 * 1024 * 1024
TRANS_B = (((1,), (1,)), ((), ()))


def _sigmoid(x):
    return 1.0 / (1.0 + jnp.exp(-x))


def _gelu_tanh(x):
    c = math.sqrt(2.0 / math.pi)
    return x * (0.5 * (1.0 + jnp.tanh(c * (x + 0.044715 * (x * x * x)))))


def _rmsnorm(x, w):
    return x * lax.rsqrt(jnp.mean(x * x, axis=-1, keepdims=True) + EPS) * w


def _per_ps(fn, a, *mods):
    rows, d = a.shape
    out = fn(a.reshape(rows // N_PS, N_PS, d), *[m[None] for m in mods])
    return out.reshape(rows, d)


def _params(*sem):
    return pltpu.CompilerParams(dimension_semantics=sem, vmem_limit_bytes=VMEM_LIMIT)


def _mod_kernel(c_ref, w_ref, b_ref, o_ref):
    c = c_ref[...]
    s = c * _sigmoid(c)
    o_ref[...] = jnp.dot(s.astype(BF16), w_ref[...].astype(BF16), preferred_element_type=F32) + b_ref[...]


def _modulation(cvec, w_mod, b_mod):
    return pl.pallas_call(
        _mod_kernel,
        grid=(DEPTH, N_MOD),
        in_specs=[
            pl.BlockSpec((MOD_ROWS, D_MODEL), lambda l, n: (0, 0)),
            pl.BlockSpec((None, D_MODEL, D_MODEL), lambda l, n: (l, 0, n)),
            pl.BlockSpec((None, None, 1, D_MODEL), lambda l, n: (l, n, 0, 0)),
        ],
        out_specs=pl.BlockSpec((None, None, MOD_ROWS, D_MODEL), lambda l, n: (l, n, 0, 0)),
        out_shape=jax.ShapeDtypeStruct((DEPTH, N_MOD, MOD_ROWS, D_MODEL), F32),
        compiler_params=_params("arbitrary", "arbitrary"),
        name="adaln_mod",
    )(cvec, w_mod, b_mod.reshape(DEPTH, N_MOD, 1, D_MODEL))


def _in_kernel(x_ref, mod_ref, nw_ref, wg_ref, wst_ref, lnw_ref, lnb_ref, ug_ref, vn_ref, xt_ref):
    y = _rmsnorm(x_ref[...], nw_ref[...])
    h = _per_ps(lambda a, sc, sh: a * (1.0 + sc) + sh, y, mod_ref[1], mod_ref[0]).astype(BF16)
    zg = _gelu_tanh(jnp.dot(h, wg_ref[...].astype(BF16), preferred_element_type=F32))
    v = zg[:, D_GMLP:]
    mu = jnp.mean(v, axis=-1, keepdims=True)
    vc = v - mu
    var = jnp.mean(vc * vc, axis=-1, keepdims=True)
    vn = vc * lax.rsqrt(var + EPS) * lnw_ref[...] + lnb_ref[...]
    for hd in range(GMLP_HEADS):
        cols = slice(hd * GMLP_HEAD_DIM, (hd + 1) * GMLP_HEAD_DIM)
        ug_ref[hd] = zg[:, cols]
        vn_ref[hd] = vn[:, cols]
    wst = wst_ref[...].astype(BF16)
    for k in range(J_PER_STEP):
        hk = h[k * J_ROWS:(k + 1) * J_ROWS]
        xt_ref[k] = lax.dot_general(wst, hk, TRANS_B, preferred_element_type=F32).astype(BF16)


def _input_proj(x, mod8, l, norm1_w, w_in_g, w_in_st, ln_w, ln_b):
    tm = TOKEN_TILE
    steps = TILE_TOKENS // tm
    tok = lambda n: pl.BlockSpec((tm, n), lambda t, s: (t * steps + s, 0))
    heads = pl.BlockSpec((GMLP_HEADS, tm, GMLP_HEAD_DIM), lambda t, s: (0, t * steps + s, 0))
    lay = lambda *shape: pl.BlockSpec((None,) + shape, lambda t, s: (l,) + (0,) * len(shape))
    return pl.pallas_call(
        _in_kernel,
        grid=(N_TILES, steps),
        in_specs=[
            tok(D_MODEL),
            pl.BlockSpec((None, None, N_MOD, N_PS, D_MODEL), lambda t, s: (l, t, 0, 0, 0)),
            lay(1, D_MODEL), lay(D_MODEL, 2 * D_GMLP), lay(D_SSM, D_MODEL), lay(1, D_GMLP), lay(1, D_GMLP),
        ],
        out_specs=[heads, heads,
                   pl.BlockSpec((None, J_PER_STEP, D_SSM, J_ROWS), lambda t, s: (t, s, 0, 0))],
        out_shape=[jax.ShapeDtypeStruct((GMLP_HEADS, T_ALL, GMLP_HEAD_DIM), F32)] * 2 + [
            jax.ShapeDtypeStruct((N_TILES, SCAN_CHUNK, D_SSM, J_ROWS), BF16)],
        compiler_params=_params("arbitrary", "arbitrary"),
        name="norm1_in_proj",
    )(x, mod8, norm1_w.reshape(DEPTH, 1, D_MODEL), w_in_g, w_in_st,
      ln_w.reshape(DEPTH, 1, D_GMLP), ln_b.reshape(DEPTH, 1, D_GMLP))


def _shift_lanes_right(a, b, s, lane):
    if s == 0:
        return a, b
    if s == 128:
        return jnp.zeros_like(a), a
    if s < 128:
        ra = pltpu.roll(a, s, 1)
        rb = pltpu.roll(b, s, 1)
        return jnp.where(lane >= s, ra, 0.0), jnp.where(lane >= s, rb, ra)
    t = s - 128
    return jnp.zeros_like(a), jnp.where(lane >= t, pltpu.roll(a, t, 1), 0.0)


def _shift_lanes_left(a, b, s, lane):
    if s == 0:
        return a, b
    if s == 128:
        return b, jnp.zeros_like(b)
    if s < 128:
        ra = pltpu.roll(a, 128 - s, 1)
        rb = pltpu.roll(b, 128 - s, 1)
        return jnp.where(lane < 128 - s, ra, rb), jnp.where(lane < 128 - s, rb, 0.0)
    t = s - 128
    return jnp.where(lane < 128 - t, pltpu.roll(b, 128 - t, 1), 0.0), jnp.zeros_like(b)


def _prep_kernel(vec_ref, mat_ref, wft_ref, cct_ref, ttt_ref, a16_ref, wf_scr, cm_scr, tt_scr):
    a_re = vec_ref[0:1, :]
    a_im = vec_ref[1:2, :]
    dt = jnp.exp(vec_ref[2:3, :])
    d_skip = vec_ref[3:4, :]
    mag = jnp.exp(a_re * dt)
    ang = a_im * dt
    ab_r = mag * jnp.cos(ang)
    ab_i = mag * jnp.sin(ang)
    den = a_re * a_re + a_im * a_im
    nr = ab_r - 1.0
    q_r = (nr * a_re + ab_i * a_im) / den
    q_i = (ab_i * a_re - nr * a_im) / den
    bt_r = mat_ref[0]
    bt_i = mat_ref[1]
    c_r = mat_ref[2]
    c_i = mat_ref[3]
    bb_r = q_r * bt_r - q_i * bt_i
    bb_i = q_r * bt_i + q_i * bt_r
    p_r = [jnp.ones_like(ab_r)]
    p_i = [jnp.zeros_like(ab_r)]
    for _ in range(SCAN_CHUNK):
        pr, pi = p_r[-1], p_i[-1]
        p_r.append(pr * ab_r - pi * ab_i)
        p_i.append(pr * ab_i + pi * ab_r)
    a16_ref[0] = jnp.broadcast_to(p_r[SCAN_CHUNK], (N_PS, 128))
    a16_ref[1] = jnp.broadcast_to(p_i[SCAN_CHUNK], (N_PS, 128))

    lane = lax.broadcasted_iota(jnp.int32, (1, 128), 1)
    is_fwd = lane < SSM_STATE

    def pick(mf, mb):
        return jnp.where(is_fwd, p_r[mf], p_r[mb]), jnp.where(is_fwd, p_i[mf], p_i[mb])

    for j in range(SCAN_CHUNK):
        rows = slice(j * SSM_GROUP, (j + 1) * SSM_GROUP)
        wr, wi = pick(SCAN_CHUNK - 1 - j, j)
        wf_scr[rows, 0:128] = bb_r * wr - bb_i * wi
        wf_scr[rows, 128:256] = bb_r * wi + bb_i * wr
        wr, wi = pick(j + 1, SCAN_CHUNK - j)
        cct_ref[rows, 0:128] = (c_r * wr - c_i * wi).astype(BF16)
        cct_ref[rows, 128:256] = (-(c_r * wi + c_i * wr)).astype(BF16)
        wr, wi = pick(j, SCAN_CHUNK - 1 - j)
        cm_scr[rows, 0:128] = c_r * wr - c_i * wi
        cm_scr[rows, 128:256] = c_r * wi + c_i * wr
    wft_ref[...] = wf_scr[...].T.astype(BF16)

    zero = jnp.zeros_like(bb_r)
    cm = cm_scr[...]

    def lag_rows(keep):
        lhs = jnp.concatenate([jnp.where(keep, bb_r, zero), jnp.where(keep, -bb_i, zero)], axis=1)
        return lax.dot_general(lhs, cm, TRANS_B, precision=lax.Precision.HIGHEST, preferred_element_type=F32)

    mf = lag_rows(is_fwd)
    mb = lag_rows(jnp.logical_not(is_fwd))
    mf_a, mf_b = mf[:, 0:128], mf[:, 128:256]
    mb_a, mb_b = mb[:, 0:128], mb[:, 128:256]
    row_h = lax.broadcasted_iota(jnp.int32, (SSM_GROUP, 128), 0)
    lane_h = lax.broadcasted_iota(jnp.int32, (SSM_GROUP, 128), 1)
    for jp in range(SCAN_CHUNK):
        rows = slice(jp * SSM_GROUP, (jp + 1) * SSM_GROUP)
        fa, fb = _shift_lanes_right(mf_a, mf_b, SSM_GROUP * jp, lane)
        ba, bb = _shift_lanes_left(mb_a, mb_b, SSM_GROUP * (SCAN_CHUNK - 1 - jp), lane)
        diag = SSM_GROUP * jp + row_h
        tt_scr[rows, 0:128] = fa + ba + jnp.where(lane_h == diag, d_skip, 0.0)
        tt_scr[rows, 128:256] = fb + bb + jnp.where(lane_h + 128 == diag, d_skip, 0.0)
    ttt_ref[...] = tt_scr[...].T.astype(BF16)


def _ssm_operators(vecs, mats):
    op = jax.ShapeDtypeStruct((DEPTH, N_GROUPS, CHUNK_WIDTH, CHUNK_WIDTH), BF16)
    op_spec = pl.BlockSpec((None, None, CHUNK_WIDTH, CHUNK_WIDTH), lambda l, g: (l, g, 0, 0))
    sq = pltpu.VMEM((CHUNK_WIDTH, CHUNK_WIDTH), F32)
    return pl.pallas_call(
        _prep_kernel,
        grid=(DEPTH, N_GROUPS),
        in_specs=[
            pl.BlockSpec((None, None, 8, 128), lambda l, g: (l, g, 0, 0)),
            pl.BlockSpec((None, None, 4, SSM_GROUP, 128), lambda l, g: (l, g, 0, 0, 0)),
        ],
        out_specs=[op_spec, op_spec, op_spec,
                   pl.BlockSpec((None, None, 2, N_PS, 128), lambda l, g: (l, g, 0, 0, 0))],
        out_shape=[op, op, op, jax.ShapeDtypeStruct((DEPTH, N_GROUPS, 2, N_PS, 128), F32)],
        scratch_shapes=[sq, sq, sq],
        compiler_params=_params("arbitrary", "arbitrary"),
        name="s5_operators",
    )(vecs, mats)


def _ssm_kernel(xt_ref, wft_ref, cct_ref, ttt_ref, a16_ref, s0_ref, yt_ref, fin_ref, s_scr, ft_scr):
    tile = pl.program_id(0)
    lane = lax.broadcasted_iota(jnp.int32, (N_PS, 128), 1)
    is_fwd = lane < SSM_STATE
    half = SSM_STATE

    def group(gl, carry):
        ch0 = pl.multiple_of(gl * SSM_GROUP, SSM_GROUP)
        xt = xt_ref[:, pl.ds(ch0, SSM_GROUP), :].reshape(CHUNK_WIDTH, J_ROWS)
        ft_scr[...] = jnp.dot(wft_ref[gl], xt, preferred_element_type=F32)
        f = ft_scr[...].T
        a_r = a16_ref[gl, 0]
        a_i = a16_ref[gl, 1]

        def scan(s_r, s_i):
            for i in range(SEG_CHUNKS):
                rf = slice(i * N_PS, (i + 1) * N_PS)
                rb = slice((SEG_CHUNKS - 1 - i) * N_PS, (SEG_CHUNKS - i) * N_PS)
                s_scr[rf, 0:half] = s_r[:, 0:half]
                s_scr[rf, 128:128 + half] = s_i[:, 0:half]
                s_scr[rb, half:128] = s_r[:, half:128]
                s_scr[rb, 128 + half:256] = s_i[:, half:128]
                f_r = jnp.where(is_fwd, f[rf, 0:128], f[rb, 0:128])
                f_i = jnp.where(is_fwd, f[rf, 128:256], f[rb, 128:256])
                s_r, s_i = a_r * s_r - a_i * s_i + f_r, a_r * s_i + a_i * s_r + f_i
            return s_r, s_i

        zeros = jnp.zeros((N_PS, 128), F32)
        z_r, z_i = scan(zeros, zeros)
        fin_ref[gl, :, 0:128] = z_r
        fin_ref[gl, :, 128:256] = z_i

        @pl.when(tile == SAMPLE_TILE)
        def _():
            b_r, b_i = a_r, a_i
            for _ in range(4):
                b_r, b_i = b_r * b_r - b_i * b_i, 2.0 * (b_r * b_i)
            seg = lax.broadcasted_iota(jnp.int32, (N_PS, 128), 0) % SEGS_PER_SAMPLE_SEQ
            i_r = s0_ref[gl, :, 0:128]
            i_i = s0_ref[gl, :, 128:256]
            for step in range(1, SEGS_PER_SAMPLE_SEQ):
                pr = jnp.where(is_fwd, pltpu.roll(i_r, 1, 0), pltpu.roll(i_r, N_PS - 1, 0))
                pi = jnp.where(is_fwd, pltpu.roll(i_i, 1, 0), pltpu.roll(i_i, N_PS - 1, 0))
                zr = jnp.where(is_fwd, pltpu.roll(z_r, 1, 0), pltpu.roll(z_r, N_PS - 1, 0))
                zi = jnp.where(is_fwd, pltpu.roll(z_i, 1, 0), pltpu.roll(z_i, N_PS - 1, 0))
                n_r = b_r * pr - b_i * pi + zr
                n_i = b_r * pi + b_i * pr + zi
                first = jnp.where(is_fwd, step, 0)
                last = jnp.where(is_fwd, SEGS_PER_SAMPLE_SEQ - 1, SEGS_PER_SAMPLE_SEQ - 1 - step)
                upd = jnp.logical_and(seg >= first, seg <= last)
                i_r = jnp.where(upd, n_r, i_r)
                i_i = jnp.where(upd, n_i, i_i)
            scan(i_r, i_i)

        yt = jnp.dot(ttt_ref[gl], xt, preferred_element_type=F32)
        yt += lax.dot_general(cct_ref[gl], s_scr[...].astype(BF16), TRANS_B, preferred_element_type=F32)
        yt_ref[:, pl.ds(ch0, SSM_GROUP), :] = yt.reshape(SCAN_CHUNK, SSM_GROUP, J_ROWS)
        return carry

    lax.fori_loop(0, GROUP_BLOCK, group, 0)


def _ssm_scan(xt, wft, cct, ttt, a16, s0, l):
    gb = GROUP_BLOCK
    op_spec = pl.BlockSpec((None, gb, CHUNK_WIDTH, CHUNK_WIDTH), lambda t, g: (l, g, 0, 0))
    io_spec = pl.BlockSpec((None, SCAN_CHUNK, gb * SSM_GROUP, J_ROWS), lambda t, g: (t, 0, g, 0))
    return pl.pallas_call(
        _ssm_kernel,
        grid=(N_TILES, N_GROUPS // gb),
        in_specs=[
            io_spec, op_spec, op_spec, op_spec,
            pl.BlockSpec((None, gb, 2, N_PS, 128), lambda t, g: (l, g, 0, 0, 0)),
            pl.BlockSpec((gb, N_PS, CHUNK_WIDTH), lambda t, g: (g, 0, 0)),
        ],
        out_specs=[io_spec, pl.BlockSpec((None, gb, N_PS, CHUNK_WIDTH), lambda t, g: (t, g, 0, 0))],
        out_shape=[
            jax.ShapeDtypeStruct((N_TILES, SCAN_CHUNK, D_SSM, J_ROWS), F32),
            jax.ShapeDtypeStruct((N_TILES, N_GROUPS, N_PS, CHUNK_WIDTH), F32),
        ],
        scratch_shapes=[pltpu.VMEM((J_ROWS, CHUNK_WIDTH), F32), pltpu.VMEM((CHUNK_WIDTH, J_ROWS), F32)],
        compiler_params=_params("arbitrary", "arbitrary"),
        name="s5_chunk_scan",
    )(xt, wft, cct, ttt, a16, s0)


def _post_kernel(x_ref, yt_ref, ug_ref, vn_ref, mod_ref, wglut_ref, ws_ref, bs_ref, wout_ref,
                 nw_ref, rwt_ref, rb_ref, x1_ref, h2_ref, route_ref, yg_scr, y_scr):
    step = pl.program_id(1)

    @pl.when(step == 0)
    def _():
        def chunk(n, carry):
            ps = n % N_PS
            c_hi = n // N_PS
            base = c_hi * (GMLP_CHUNK // SCAN_CHUNK) * N_PS + ps
            rows = [pl.ds(j * J_ROWS + base, GMLP_CHUNK // SCAN_CHUNK, stride=N_PS) for j in range(SCAN_CHUNK)]
            for h in range(GMLP_HEADS):
                v = jnp.concatenate([vn_ref[h, r, :] for r in rows], axis=0).astype(BF16)
                u = jnp.concatenate([ug_ref[h, r, :] for r in rows], axis=0)
                s = jnp.dot(ws_ref[h].astype(BF16), v, preferred_element_type=F32) + bs_ref[h]
                yg = u * s
                for j, r in enumerate(rows):
                    yg_scr[h, r, :] = yg[j * 8:(j + 1) * 8]
            return carry

        lax.fori_loop(0, TILE_TOKENS // GMLP_CHUNK, chunk, 0)

    wglut = wglut_ref[...].astype(BF16)
    for k in range(J_PER_STEP):
        yt = _gelu_tanh(yt_ref[k])
        yt = yt * _sigmoid(jnp.dot(wglut, yt.astype(BF16), preferred_element_type=F32))
        y_scr[k * J_ROWS:(k + 1) * J_ROWS, :] = yt.T
    row0 = pl.multiple_of(step * TOKEN_TILE, TOKEN_TILE)
    proj = jnp.dot(y_scr[...].astype(BF16), wout_ref[0:D_SSM, :].astype(BF16), preferred_element_type=F32)
    yg = jnp.concatenate([yg_scr[h, pl.ds(row0, TOKEN_TILE), :] for h in range(GMLP_HEADS)], axis=1)
    proj += jnp.dot(yg.astype(BF16), wout_ref[D_SSM:, :].astype(BF16), preferred_element_type=F32)
    x1 = x_ref[...] + _per_ps(lambda a, g: a * g, proj, mod_ref[2])
    x1_ref[...] = x1
    h2 = _per_ps(lambda a, sc, sh: a * (1.0 + sc) + sh, _rmsnorm(x1, nw_ref[...]), mod_ref[4], mod_ref[3])
    h2_ref[...] = h2.astype(BF16)
    logits = lax.dot_general(rwt_ref[...], h2, TRANS_B, precision=lax.Precision.HIGHEST,
                             preferred_element_type=F32)
    scores = _sigmoid(logits)
    sel = scores + rb_ref[...]
    sc = [scores[e:e + 1, :] for e in range(N_EXPERTS)]
    sl = [sel[e:e + 1, :] for e in range(N_EXPERTS)]
    gscore = []
    for g in range(N_EXPERT_GROUPS):
        v0, v1, v2, v3 = sl[4 * g:4 * g + 4]
        hi01, lo01 = jnp.maximum(v0, v1), jnp.minimum(v0, v1)
        hi23, lo23 = jnp.maximum(v2, v3), jnp.minimum(v2, v3)
        top1 = jnp.maximum(hi01, hi23)
        top2 = jnp.maximum(jnp.minimum(hi01, hi23), jnp.maximum(lo01, lo23))
        gscore.append(top1 + top2)
    best = gscore[0]
    gidx = jnp.zeros_like(best, dtype=jnp.int32)
    for g in range(1, N_EXPERT_GROUPS):
        upd = gscore[g] > best
        gidx = jnp.where(upd, g, gidx)
        best = jnp.where(upd, gscore[g], best)

    def in_group(vals, k):
        out = vals[k]
        for g in range(1, N_EXPERT_GROUPS):
            out = jnp.where(gidx == g, vals[4 * g + k], out)
        return out

    v = [in_group(sl, k) for k in range(EXPERTS_PER_GROUP)]
    s = [in_group(sc, k) for k in range(EXPERTS_PER_GROUP)]
    w = []
    bits = jnp.zeros_like(gidx)
    for k in range(EXPERTS_PER_GROUP):
        rank = jnp.zeros_like(gidx)
        for j in range(EXPERTS_PER_GROUP):
            if j == k:
                continue
            ahead = (v[j] >= v[k]) if j < k else (v[j] > v[k])
            rank = rank + ahead.astype(jnp.int32)
        w.append(jnp.where(rank < 2, s[k], 0.0))
        bits = bits + jnp.where(rank < 2, 1 << k, 0)
    denom = (w[0] + w[1]) + (w[2] + w[3])
    gate = [wk / denom for wk in w]
    pair = jnp.full_like(gidx, len(PAIR_SLOT_A) - 1)
    for p in range(len(PAIR_SLOT_A) - 1):
        pair = jnp.where(bits == (1 << PAIR_SLOT_A[p]) + (1 << PAIR_SLOT_B[p]), p, pair)

    def slot_gate(table):
        out = gate[table[0]]
        for p in range(1, len(table)):
            out = jnp.where(pair == p, gate[table[p]], out)
        return out

    route_ref[0:1, :] = (gidx * len(PAIR_SLOT_A) + pair).astype(F32)
    route_ref[1:2, :] = slot_gate(PAIR_SLOT_A)
    route_ref[2:3, :] = slot_gate(PAIR_SLOT_B)
    route_ref[3:8, :] = jnp.zeros((5, route_ref.shape[1]), F32)


def _mix_out(x, yt, ug, vn, mod8, l, w_glu_t, w_s, b_s, w_out, norm2_w, rwt, rb):
    tm = TOKEN_TILE
    steps = TILE_TOKENS // tm
    tok = lambda n: pl.BlockSpec((tm, n), lambda t, s: (t * steps + s, 0))
    whole_tile = pl.BlockSpec((GMLP_HEADS, TILE_TOKENS, GMLP_HEAD_DIM), lambda t, s: (0, t, 0))
    lay = lambda *shape: pl.BlockSpec((None,) + shape, lambda t, s: (l,) + (0,) * len(shape))
    return pl.pallas_call(
        _post_kernel,
        grid=(N_TILES, steps),
        in_specs=[
            tok(D_MODEL),
            pl.BlockSpec((None, J_PER_STEP, D_SSM, J_ROWS), lambda t, s: (t, s, 0, 0)),
            whole_tile, whole_tile,
            pl.BlockSpec((None, None, N_MOD, N_PS, D_MODEL), lambda t, s: (l, t, 0, 0, 0)),
            lay(D_SSM, D_SSM), lay(GMLP_HEADS, GMLP_CHUNK, GMLP_CHUNK),
            lay(GMLP_HEADS, GMLP_CHUNK, 1), lay(D_MODEL, D_MODEL), lay(1, D_MODEL),
            pl.BlockSpec((N_EXPERTS, D_MODEL), lambda t, s: (0, 0)),
            pl.BlockSpec((N_EXPERTS, 1), lambda t, s: (0, 0)),
        ],
        out_specs=[tok(D_MODEL), tok(D_MODEL), pl.BlockSpec((8, tm), lambda t, s: (0, t * steps + s))],
        out_shape=[
            jax.ShapeDtypeStruct((T_ALL, D_MODEL), F32),
            jax.ShapeDtypeStruct((T_ALL, D_MODEL), BF16),
            jax.ShapeDtypeStruct((8, T_ALL), F32),
        ],
        scratch_shapes=[pltpu.VMEM((GMLP_HEADS, TILE_TOKENS, GMLP_HEAD_DIM), F32), pltpu.VMEM((tm, D_SSM), F32)],
        compiler_params=_params("arbitrary", "arbitrary"),
        name="mixers_out_router",
    )(x, yt, ug, vn, mod8, w_glu_t, w_s, b_s, w_out, norm2_w.reshape(DEPTH, 1, D_MODEL), rwt, rb)


def _moe_kernel(ea_ref, eb_ref, new_a_ref, new_b_ref, nv_ref,
                h_ref, meta_ref, x1_ref, wga_ref, wua_ref, wda_ref, wgb_ref, wub_ref, wdb_ref, g2_ref, fw_ref,
                o_ref, wga_s, wua_s, wda_s, wgb_s, wub_s, wdb_s, *, final):
    n = pl.program_id(0)

    @pl.when(new_a_ref[n] == 1)
    def _():
        wga_s[...] = wga_ref[...].astype(BF16)
        wua_s[...] = wua_ref[...].astype(BF16)
        wda_s[...] = wda_ref[...].astype(BF16)

    @pl.when(new_b_ref[n] == 1)
    def _():
        wgb_s[...] = wgb_ref[...].astype(BF16)
        wub_s[...] = wub_ref[...].astype(BF16)
        wdb_s[...] = wdb_ref[...].astype(BF16)

    @pl.when(n < nv_ref[0])
    def _():
        h = h_ref[...]
        meta = meta_ref[...]

        def expert(wg, wu, wd, gate):
            hg = jnp.dot(h, wg[...], preferred_element_type=F32)
            hu = jnp.dot(h, wu[...], preferred_element_type=F32)
            act = hg * _sigmoid(hg) * hu * gate
            return jnp.dot(act.astype(BF16), wd[...], preferred_element_type=F32)

        y = expert(wga_s, wua_s, wda_s, meta[:, 0:1]) + expert(wgb_s, wub_s, wdb_s, meta[:, 1:2])
        cond_row = meta[:, 2:3]
        gate2 = jnp.where(cond_row == 0.0, g2_ref[0:1, :], jnp.where(cond_row == 1.0, g2_ref[1:2, :], g2_ref[2:3, :]))
        x2 = x1_ref[...] + gate2 * y
        o_ref[...] = _rmsnorm(x2, fw_ref[...]) if final else x2

    @pl.when(n >= nv_ref[0])
    def _():
        o_ref[...] = jnp.zeros_like(o_ref)


def _experts(tables, hs, metas, x1s, mods, l, w_gate, w_up, w_down, final_w, final):
    tm = MOE_TM
    tok = lambda d: pl.BlockSpec((tm, d), lambda n, *_: (n, 0))
    w_a = lambda r, c: pl.BlockSpec((None, None, r, c), lambda n, ea, eb, na, nb, nv: (l, ea[n], 0, 0))
    w_b = lambda r, c: pl.BlockSpec((None, None, r, c), lambda n, ea, eb, na, nb, nv: (l, eb[n], 0, 0))
    up = pltpu.VMEM((D_MODEL, D_EXPERT), BF16)
    down = pltpu.VMEM((D_EXPERT, D_MODEL), BF16)
    return pl.pallas_call(
        functools.partial(_moe_kernel, final=final),
        grid_spec=pltpu.PrefetchScalarGridSpec(
            num_scalar_prefetch=5,
            grid=(MOE_TILES,),
            in_specs=[
                tok(D_MODEL), tok(8), tok(D_MODEL),
                w_a(D_MODEL, D_EXPERT), w_a(D_MODEL, D_EXPERT), w_a(D_EXPERT, D_MODEL),
                w_b(D_MODEL, D_EXPERT), w_b(D_MODEL, D_EXPERT), w_b(D_EXPERT, D_MODEL),
                pl.BlockSpec((None, None, MOD_ROWS, D_MODEL), lambda n, *_: (l, N_MOD - 1, 0, 0)),
                pl.BlockSpec((1, D_MODEL), lambda n, *_: (0, 0)),
            ],
            out_specs=tok(D_MODEL),
            scratch_shapes=[up, up, down, up, up, down],
        ),
        out_shape=jax.ShapeDtypeStruct((MOE_ROWS, D_MODEL), F32),
        compiler_params=_params("arbitrary"),
        name="experts",
    )(*tables, hs, metas, x1s, w_gate, w_up, w_down, w_gate, w_up, w_down, mods, final_w.reshape(1, D_MODEL))


def _routing_tables(route):
    n_cls = N_EXPERT_GROUPS * len(PAIR_SLOT_A)
    cls = route[0].astype(jnp.int32)
    onehot = (cls[:, None] == jnp.arange(n_cls, dtype=jnp.int32)[None, :]).astype(jnp.int32)
    counts = jnp.sum(onehot, axis=0)
    rank = jnp.take_along_axis(jnp.cumsum(onehot, axis=0) - onehot, cls[:, None], axis=1)[:, 0]
    tiles = (counts + MOE_TM - 1) // MOE_TM
    tile_end = jnp.cumsum(tiles)
    dst = (tile_end - tiles)[cls] * MOE_TM + rank
    src = jnp.zeros((MOE_ROWS,), jnp.int32).at[dst].set(jnp.arange(T_ALL, dtype=jnp.int32))
    n_valid = tile_end[-1]
    tile_cls = jnp.searchsorted(tile_end, jnp.minimum(jnp.arange(MOE_TILES, dtype=jnp.int32), n_valid - 1),
                                side="right").astype(jnp.int32)
    group, pair = tile_cls // len(PAIR_SLOT_A), tile_cls % len(PAIR_SLOT_A)
    e_a = group * EXPERTS_PER_GROUP + jnp.array(PAIR_SLOT_A, jnp.int32)[pair]
    e_b = group * EXPERTS_PER_GROUP + jnp.array(PAIR_SLOT_B, jnp.int32)[pair]
    first = jnp.ones((1,), jnp.int32)
    new_a = jnp.concatenate([first, (e_a[1:] != e_a[:-1]).astype(jnp.int32)])
    new_b = jnp.concatenate([first, (e_b[1:] != e_b[:-1]).astype(jnp.int32)])
    return (e_a, e_b, new_a, new_b, n_valid.reshape(1).astype(jnp.int32)), src, dst


def _dirs_on_lanes(p):
    p = jnp.moveaxis(p, 1, -2)
    return p.reshape(p.shape[:-2] + (2 * SSM_STATE,))


def _to_internal_order(x):
    x = x.reshape(N_TILES, N_PS, SEG_CHUNKS, SCAN_CHUNK, D_MODEL)
    return x.transpose(0, 3, 2, 1, 4).reshape(T_ALL, D_MODEL)


def _gmlp_position_order(w):
    n_lo = GMLP_CHUNK // SCAN_CHUNK
    lead = w.shape[:2]
    w = w.reshape(lead + (n_lo, SCAN_CHUNK) + w.shape[3:])
    w = jnp.swapaxes(w, 2, 3)
    return w.reshape(lead + (GMLP_CHUNK,) + w.shape[4:])


def kernel(x_prompt, x_sample, c, state_ssm_re, state_ssm_im, c_ctx, norm1_w, norm2_w, w_mod, b_mod, w_in,
           ssm_a_re, ssm_a_im, ssm_log_dt, ssm_b_re, ssm_b_im, ssm_c_re, ssm_c_im, ssm_d, w_glu,
           gmlp_ln_w, gmlp_ln_b, gmlp_w_s, gmlp_b_s, w_out, router_w, router_b, w_gate, w_up, w_down,
           final_norm_w):
    x = jnp.concatenate([x_prompt.reshape(T_PROMPT, D_MODEL), x_sample.reshape(T_SAMPLE, D_MODEL)], axis=0)
    x = _to_internal_order(x)

    cvec = jnp.concatenate([c_ctx[None, :], c, jnp.zeros((MOD_ROWS - 1 - N_SAMPLE_SEQ, D_MODEL), F32)], axis=0)
    mods = _modulation(cvec, w_mod, b_mod)
    ps_row = [[0] * N_PS] * SAMPLE_TILE + [[1 + p // SEGS_PER_SAMPLE_SEQ for p in range(N_PS)]]
    mod8 = mods[:, :, jnp.array(ps_row, jnp.int32), :].transpose(0, 2, 1, 3, 4)

    log_dt = jnp.broadcast_to(ssm_log_dt[..., None], ssm_a_re.shape)
    d_lanes = jnp.tile(ssm_d.reshape(DEPTH, N_GROUPS, SSM_GROUP), (1, 1, 128 // SSM_GROUP))
    vecs = jnp.stack([_dirs_on_lanes(ssm_a_re), _dirs_on_lanes(ssm_a_im), _dirs_on_lanes(log_dt), d_lanes], axis=2)
    vecs = jnp.concatenate([vecs, jnp.zeros((DEPTH, N_GROUPS, 4, 2 * SSM_STATE), F32)], axis=2)
    mats = jnp.stack([_dirs_on_lanes(jnp.swapaxes(ssm_b_re, -1, -2)), _dirs_on_lanes(jnp.swapaxes(ssm_b_im, -1, -2)),
                      _dirs_on_lanes(ssm_c_re), _dirs_on_lanes(ssm_c_im)], axis=2)
    wft, cct, ttt, a16 = _ssm_operators(vecs, mats)

    w_in_g = w_in[:, :, D_SSM:]
    w_in_st = jnp.swapaxes(w_in[:, :, :D_SSM], 1, 2)
    w_glu_t = jnp.swapaxes(w_glu, 1, 2)
    w_s = jnp.swapaxes(_gmlp_position_order(jnp.swapaxes(_gmlp_position_order(gmlp_w_s), 2, 3)), 2, 3)
    b_s = _gmlp_position_order(gmlp_b_s)[..., None]
    rwt = router_w.T
    rb = router_b.reshape(N_EXPERTS, 1)
    idx = np.arange(T_ALL).reshape(N_TILES, SCAN_CHUNK, SEG_CHUNKS, N_PS)
    ps_idx = idx % N_PS
    cond_row = np.where(idx // TILE_TOKENS < SAMPLE_TILE, 0, 1 + ps_idx // SEGS_PER_SAMPLE_SEQ)
    cond_row = jnp.asarray(cond_row.reshape(T_ALL, 1), F32)
    to_sequence_order = jnp.asarray(idx.transpose(0, 3, 2, 1).reshape(T_ALL), jnp.int32)

    new_re, new_im = [], []
    for l in range(DEPTH):
        ug, vn, xt = _input_proj(x, mod8, l, norm1_w, w_in_g, w_in_st, gmlp_ln_w, gmlp_ln_b)
        s0 = jnp.concatenate([state_ssm_re[:, l].transpose(2, 0, 1, 3).reshape(N_GROUPS, N_SAMPLE_SEQ, 128),
                              state_ssm_im[:, l].transpose(2, 0, 1, 3).reshape(N_GROUPS, N_SAMPLE_SEQ, 128)], axis=-1)
        s0 = jnp.repeat(s0, SEGS_PER_SAMPLE_SEQ, axis=1)
        yt, fin = _ssm_scan(xt, wft, cct, ttt, a16, s0, l)
        fin = fin[:SAMPLE_TILE].reshape(SAMPLE_TILE, N_GROUPS, N_PS, 2, 2, SSM_STATE)
        fin = fin.transpose(3, 0, 2, 4, 1, 5).reshape(2, N_PROMPT_SEQ, 2, N_GROUPS, SSM_STATE)
        new_re.append(fin[0])
        new_im.append(fin[1])
        x1, h2, route = _mix_out(x, yt, ug, vn, mod8, l, w_glu_t, w_s, b_s, w_out, norm2_w, rwt, rb)
        tables, src, dst = _routing_tables(route)
        meta = jnp.concatenate([route[1:3].T, cond_row, jnp.zeros((T_ALL, 5), F32)], axis=1)
        final = l == DEPTH - 1
        x2s = _experts(tables, h2[src], meta[src], x1[src], mods, l, w_gate, w_up, w_down, final_norm_w, final)
        x = x2s[dst[to_sequence_order]] if final else x2s[dst]

    y_prompt = x[:T_PROMPT].reshape(N_PROMPT_SEQ, PROMPT_LEN, D_MODEL)
    y_sample = x[T_PROMPT:].reshape(N_SAMPLE_SEQ, SAMPLE_LEN, D_MODEL)
    return (y_prompt, y_sample, jnp.stack(new_re, axis=1), jnp.stack(new_im, axis=1))
```

```python
import functools
import math

import jax
import jax.numpy as jnp
import numpy as np
from jax import lax
from jax.experimental import pallas as pl
from jax.experimental.pallas import tpu as pltpu

F32 = jnp.float32
BF16 = jnp.bfloat16

D_MODEL = 1024
N_PROMPT_SEQ = 16
PROMPT_LEN = 256
N_SAMPLE_SEQ = 2
SAMPLE_LEN = 1024
T_PROMPT = N_PROMPT_SEQ * PROMPT_LEN
T_SAMPLE = N_SAMPLE_SEQ * SAMPLE_LEN
T_ALL = T_PROMPT + T_SAMPLE
DEPTH = 2
D_SSM = 512
SSM_GROUP = 16
N_GROUPS = 32
SSM_STATE = 64
D_GMLP = 512
GMLP_HEADS = 4
GMLP_HEAD_DIM = 128
GMLP_CHUNK = 128
N_EXPERTS = 16
N_EXPERT_GROUPS = 4
EXPERTS_PER_GROUP = 4
D_EXPERT = 512
N_MOD = 6
EPS = 1e-6

SCAN_CHUNK = 16
CHUNK_WIDTH = SCAN_CHUNK * SSM_GROUP
SEG_LEN = 256
SEG_CHUNKS = SEG_LEN // SCAN_CHUNK
N_PS = 8
TILE_TOKENS = N_PS * SEG_LEN
N_TILES = T_ALL // TILE_TOKENS
SAMPLE_TILE = T_PROMPT // TILE_TOKENS
SEGS_PER_SAMPLE_SEQ = SAMPLE_LEN // SEG_LEN
J_ROWS = SEG_CHUNKS * N_PS
GROUP_BLOCK = 8
MOD_ROWS = 8

J_PER_STEP = 2
TOKEN_TILE = J_PER_STEP * J_ROWS
PAIR_SLOT_A = (0, 0, 0, 1, 1, 3)
PAIR_SLOT_B = (1, 2, 3, 3, 2, 2)
MOE_TM = 128
MOE_TILES = T_ALL // MOE_TM + N_EXPERT_GROUPS * len(PAIR_SLOT_A)
MOE_ROWS = MOE_TILES * MOE_TM
ROUTE_LANES = 128
HX_WIDTH = 2 * D_MODEL + ROUTE_LANES
VMEM_LIMIT = 56 * 1024 * 1024
TRANS_B = (((1,), (1,)), ((), ()))


def _sigmoid(x):
    return 1.0 / (1.0 + jnp.exp(-x))


def _gelu_tanh(x):
    c = math.sqrt(2.0 / math.pi)
    return x * (0.5 * (1.0 + jnp.tanh(c * (x + 0.044715 * (x * x * x)))))


def _rmsnorm(x, w):
    return x * lax.rsqrt(jnp.mean(x * x, axis=-1, keepdims=True) + EPS) * w


def _per_ps(fn, a, *mods):
    rows, d = a.shape
    out = fn(a.reshape(rows // N_PS, N_PS, d), *[m[None] for m in mods])
    return out.reshape(rows, d)


def _params(*sem):
    return pltpu.CompilerParams(dimension_semantics=sem, vmem_limit_bytes=VMEM_LIMIT)


def _mod_kernel(c_ref, w_ref, b_ref, o_ref):
    c = c_ref[...]
    s = c * _sigmoid(c)
    o_ref[...] = jnp.dot(s.astype(BF16), w_ref[...].astype(BF16), preferred_element_type=F32) + b_ref[...]


def _modulation(cvec, w_mod, b_mod):
    return pl.pallas_call(
        _mod_kernel,
        grid=(DEPTH, N_MOD),
        in_specs=[
            pl.BlockSpec((MOD_ROWS, D_MODEL), lambda l, n: (0, 0)),
            pl.BlockSpec((None, D_MODEL, D_MODEL), lambda l, n: (l, 0, n)),
            pl.BlockSpec((None, None, 1, D_MODEL), lambda l, n: (l, n, 0, 0)),
        ],
        out_specs=pl.BlockSpec((None, None, MOD_ROWS, D_MODEL), lambda l, n: (l, n, 0, 0)),
        out_shape=jax.ShapeDtypeStruct((DEPTH, N_MOD, MOD_ROWS, D_MODEL), F32),
        compiler_params=_params("arbitrary", "arbitrary"),
        name="adaln_mod",
    )(cvec, w_mod, b_mod.reshape(DEPTH, N_MOD, 1, D_MODEL))


def _in_kernel(x_ref, mod_ref, nw_ref, wg_ref, wst_ref, lnw_ref, lnb_ref, ug_ref, vn_ref, xt_ref):
    y = _rmsnorm(x_ref[...], nw_ref[...])
    h = _per_ps(lambda a, sc, sh: a * (1.0 + sc) + sh, y, mod_ref[1], mod_ref[0]).astype(BF16)
    zg = _gelu_tanh(jnp.dot(h, wg_ref[...].astype(BF16), preferred_element_type=F32))
    v = zg[:, D_GMLP:]
    mu = jnp.mean(v, axis=-1, keepdims=True)
    vc = v - mu
    var = jnp.mean(vc * vc, axis=-1, keepdims=True)
    vn = vc * lax.rsqrt(var + EPS) * lnw_ref[...] + lnb_ref[...]
    for hd in range(GMLP_HEADS):
        cols = slice(hd * GMLP_HEAD_DIM, (hd + 1) * GMLP_HEAD_DIM)
        ug_ref[hd] = zg[:, cols]
        vn_ref[hd] = vn[:, cols]
    wst = wst_ref[...].astype(BF16)
    for k in range(J_PER_STEP):
        hk = h[k * J_ROWS:(k + 1) * J_ROWS]
        xt_ref[k] = lax.dot_general(wst, hk, TRANS_B, preferred_element_type=F32).astype(BF16)


def _input_proj(x, mod8, l, norm1_w, w_in_g, w_in_st, ln_w, ln_b):
    tm = TOKEN_TILE
    steps = TILE_TOKENS // tm
    tok = lambda n: pl.BlockSpec((tm, n), lambda t, s: (t * steps + s, 0))
    heads = pl.BlockSpec((GMLP_HEADS, tm, GMLP_HEAD_DIM), lambda t, s: (0, t * steps + s, 0))
    lay = lambda *shape: pl.BlockSpec((None,) + shape, lambda t, s: (l,) + (0,) * len(shape))
    return pl.pallas_call(
        _in_kernel,
        grid=(N_TILES, steps),
        in_specs=[
            tok(D_MODEL),
            pl.BlockSpec((None, None, N_MOD, N_PS, D_MODEL), lambda t, s: (l, t, 0, 0, 0)),
            lay(1, D_MODEL), lay(D_MODEL, 2 * D_GMLP), lay(D_SSM, D_MODEL), lay(1, D_GMLP), lay(1, D_GMLP),
        ],
        out_specs=[heads, heads,
                   pl.BlockSpec((None, J_PER_STEP, D_SSM, J_ROWS), lambda t, s: (t, s, 0, 0))],
        out_shape=[jax.ShapeDtypeStruct((GMLP_HEADS, T_ALL, GMLP_HEAD_DIM), F32)] * 2 + [
            jax.ShapeDtypeStruct((N_TILES, SCAN_CHUNK, D_SSM, J_ROWS), BF16)],
        compiler_params=_params("arbitrary", "arbitrary"),
        name="norm1_in_proj",
    )(x, mod8, norm1_w.reshape(DEPTH, 1, D_MODEL), w_in_g, w_in_st,
      ln_w.reshape(DEPTH, 1, D_GMLP), ln_b.reshape(DEPTH, 1, D_GMLP))


def _shift_lanes_right(a, b, s, lane):
    if s == 0:
        return a, b
    if s == 128:
        return jnp.zeros_like(a), a
    if s < 128:
        ra = pltpu.roll(a, s, 1)
        rb = pltpu.roll(b, s, 1)
        return jnp.where(lane >= s, ra, 0.0), jnp.where(lane >= s, rb, ra)
    t = s - 128
    return jnp.zeros_like(a), jnp.where(lane >= t, pltpu.roll(a, t, 1), 0.0)


def _shift_lanes_left(a, b, s, lane):
    if s == 0:
        return a, b
    if s == 128:
        return b, jnp.zeros_like(b)
    if s < 128:
        ra = pltpu.roll(a, 128 - s, 1)
        rb = pltpu.roll(b, 128 - s, 1)
        return jnp.where(lane < 128 - s, ra, rb), jnp.where(lane < 128 - s, rb, 0.0)
    t = s - 128
    return jnp.where(lane < 128 - t, pltpu.roll(b, 128 - t, 1), 0.0), jnp.zeros_like(b)


def _prep_kernel(vec_ref, mat_ref, wft_ref, cct_ref, ttt_ref, a16_ref, wf_scr, cm_scr, tt_scr):
    a_re = vec_ref[0:1, :]
    a_im = vec_ref[1:2, :]
    dt = jnp.exp(vec_ref[2:3, :])
    d_skip = vec_ref[3:4, :]
    mag = jnp.exp(a_re * dt)
    ang = a_im * dt
    ab_r = mag * jnp.cos(ang)
    ab_i = mag * jnp.sin(ang)
    den = a_re * a_re + a_im * a_im
    nr = ab_r - 1.0
    q_r = (nr * a_re + ab_i * a_im) / den
    q_i = (ab_i * a_re - nr * a_im) / den
    bt_r = mat_ref[0]
    bt_i = mat_ref[1]
    c_r = mat_ref[2]
    c_i = mat_ref[3]
    bb_r = q_r * bt_r - q_i * bt_i
    bb_i = q_r * bt_i + q_i * bt_r
    p_r = [jnp.ones_like(ab_r)]
    p_i = [jnp.zeros_like(ab_r)]
    for _ in range(SCAN_CHUNK):
        pr, pi = p_r[-1], p_i[-1]
        p_r.append(pr * ab_r - pi * ab_i)
        p_i.append(pr * ab_i + pi * ab_r)
    a16_ref[0] = jnp.broadcast_to(p_r[SCAN_CHUNK], (N_PS, 128))
    a16_ref[1] = jnp.broadcast_to(p_i[SCAN_CHUNK], (N_PS, 128))

    lane = lax.broadcasted_iota(jnp.int32, (1, 128), 1)
    is_fwd = lane < SSM_STATE

    def pick(mf, mb):
        return jnp.where(is_fwd, p_r[mf], p_r[mb]), jnp.where(is_fwd, p_i[mf], p_i[mb])

    for j in range(SCAN_CHUNK):
        rows = slice(j * SSM_GROUP, (j + 1) * SSM_GROUP)
        wr, wi = pick(SCAN_CHUNK - 1 - j, j)
        wf_scr[rows, 0:128] = bb_r * wr - bb_i * wi
        wf_scr[rows, 128:256] = bb_r * wi + bb_i * wr
        wr, wi = pick(j + 1, SCAN_CHUNK - j)
        cct_ref[rows, 0:128] = (c_r * wr - c_i * wi).astype(BF16)
        cct_ref[rows, 128:256] = (-(c_r * wi + c_i * wr)).astype(BF16)
        wr, wi = pick(j, SCAN_CHUNK - 1 - j)
        cm_scr[rows, 0:128] = c_r * wr - c_i * wi
        cm_scr[rows, 128:256] = c_r * wi + c_i * wr
    wft_ref[...] = wf_scr[...].T.astype(BF16)

    zero = jnp.zeros_like(bb_r)
    cm = cm_scr[...]

    def lag_rows(keep):
        lhs = jnp.concatenate([jnp.where(keep, bb_r, zero), jnp.where(keep, -bb_i, zero)], axis=1)
        return lax.dot_general(lhs, cm, TRANS_B, precision=lax.Precision.HIGHEST, preferred_element_type=F32)

    mf = lag_rows(is_fwd)
    mb = lag_rows(jnp.logical_not(is_fwd))
    mf_a, mf_b = mf[:, 0:128], mf[:, 128:256]
    mb_a, mb_b = mb[:, 0:128], mb[:, 128:256]
    row_h = lax.broadcasted_iota(jnp.int32, (SSM_GROUP, 128), 0)
    lane_h = lax.broadcasted_iota(jnp.int32, (SSM_GROUP, 128), 1)
    for jp in range(SCAN_CHUNK):
        rows = slice(jp * SSM_GROUP, (jp + 1) * SSM_GROUP)
        fa, fb = _shift_lanes_right(mf_a, mf_b, SSM_GROUP * jp, lane)
        ba, bb = _shift_lanes_left(mb_a, mb_b, SSM_GROUP * (SCAN_CHUNK - 1 - jp), lane)
        diag = SSM_GROUP * jp + row_h
        tt_scr[rows, 0:128] = fa + ba + jnp.where(lane_h == diag, d_skip, 0.0)
        tt_scr[rows, 128:256] = fb + bb + jnp.where(lane_h + 128 == diag, d_skip, 0.0)
    ttt_ref[...] = tt_scr[...].T.astype(BF16)


def _ssm_operators(vecs, mats):
    op = jax.ShapeDtypeStruct((DEPTH, N_GROUPS, CHUNK_WIDTH, CHUNK_WIDTH), BF16)
    op_spec = pl.BlockSpec((None, None, CHUNK_WIDTH, CHUNK_WIDTH), lambda l, g: (l, g, 0, 0))
    sq = pltpu.VMEM((CHUNK_WIDTH, CHUNK_WIDTH), F32)
    return pl.pallas_call(
        _prep_kernel,
        grid=(DEPTH, N_GROUPS),
        in_specs=[
            pl.BlockSpec((None, None, 8, 128), lambda l, g: (l, g, 0, 0)),
            pl.BlockSpec((None, None, 4, SSM_GROUP, 128), lambda l, g: (l, g, 0, 0, 0)),
        ],
        out_specs=[op_spec, op_spec, op_spec,
                   pl.BlockSpec((None, None, 2, N_PS, 128), lambda l, g: (l, g, 0, 0, 0))],
        out_shape=[op, op, op, jax.ShapeDtypeStruct((DEPTH, N_GROUPS, 2, N_PS, 128), F32)],
        scratch_shapes=[sq, sq, sq],
        compiler_params=_params("arbitrary", "arbitrary"),
        name="s5_operators",
    )(vecs, mats)


def _ssm_kernel(xt_ref, wft_ref, cct_ref, ttt_ref, a16_ref, s0_ref, yt_ref, fin_ref, s_scr, ft_scr):
    tile = pl.program_id(0)
    lane = lax.broadcasted_iota(jnp.int32, (N_PS, 128), 1)
    is_fwd = lane < SSM_STATE
    half = SSM_STATE

    def group(gl, carry):
        ch0 = pl.multiple_of(gl * SSM_GROUP, SSM_GROUP)
        xt = xt_ref[:, pl.ds(ch0, SSM_GROUP), :].reshape(CHUNK_WIDTH, J_ROWS)
        ft_scr[...] = jnp.dot(wft_ref[gl], xt, preferred_element_type=F32)
        f = ft_scr[...].T
        a_r = a16_ref[gl, 0]
        a_i = a16_ref[gl, 1]

        def scan(s_r, s_i):
            for i in range(SEG_CHUNKS):
                rf = slice(i * N_PS, (i + 1) * N_PS)
                rb = slice((SEG_CHUNKS - 1 - i) * N_PS, (SEG_CHUNKS - i) * N_PS)
                s_scr[rf, 0:half] = s_r[:, 0:half]
                s_scr[rf, 128:128 + half] = s_i[:, 0:half]
                s_scr[rb, half:128] = s_r[:, half:128]
                s_scr[rb, 128 + half:256] = s_i[:, half:128]
                f_r = jnp.where(is_fwd, f[rf, 0:128], f[rb, 0:128])
                f_i = jnp.where(is_fwd, f[rf, 128:256], f[rb, 128:256])
                s_r, s_i = a_r * s_r - a_i * s_i + f_r, a_r * s_i + a_i * s_r + f_i
            return s_r, s_i

        zeros = jnp.zeros((N_PS, 128), F32)
        z_r, z_i = scan(zeros, zeros)
        fin_ref[gl, :, 0:128] = z_r
        fin_ref[gl, :, 128:256] = z_i

        @pl.when(tile == SAMPLE_TILE)
        def _():
            b_r, b_i = a_r, a_i
            for _ in range(4):
                b_r, b_i = b_r * b_r - b_i * b_i, 2.0 * (b_r * b_i)
            seg = lax.broadcasted_iota(jnp.int32, (N_PS, 128), 0) % SEGS_PER_SAMPLE_SEQ
            i_r = s0_ref[gl, :, 0:128]
            i_i = s0_ref[gl, :, 128:256]
            for step in range(1, SEGS_PER_SAMPLE_SEQ):
                pr = jnp.where(is_fwd, pltpu.roll(i_r, 1, 0), pltpu.roll(i_r, N_PS - 1, 0))
                pi = jnp.where(is_fwd, pltpu.roll(i_i, 1, 0), pltpu.roll(i_i, N_PS - 1, 0))
                zr = jnp.where(is_fwd, pltpu.roll(z_r, 1, 0), pltpu.roll(z_r, N_PS - 1, 0))
                zi = jnp.where(is_fwd, pltpu.roll(z_i, 1, 0), pltpu.roll(z_i, N_PS - 1, 0))
                n_r = b_r * pr - b_i * pi + zr
                n_i = b_r * pi + b_i * pr + zi
                first = jnp.where(is_fwd, step, 0)
                last = jnp.where(is_fwd, SEGS_PER_SAMPLE_SEQ - 1, SEGS_PER_SAMPLE_SEQ - 1 - step)
                upd = jnp.logical_and(seg >= first, seg <= last)
                i_r = jnp.where(upd, n_r, i_r)
                i_i = jnp.where(upd, n_i, i_i)
            scan(i_r, i_i)

        yt = jnp.dot(ttt_ref[gl], xt, preferred_element_type=F32)
        yt += lax.dot_general(cct_ref[gl], s_scr[...].astype(BF16), TRANS_B, preferred_element_type=F32)
        yt_ref[:, pl.ds(ch0, SSM_GROUP), :] = yt.reshape(SCAN_CHUNK, SSM_GROUP, J_ROWS)
        return carry

    lax.fori_loop(0, GROUP_BLOCK, group, 0)


def _ssm_scan(xt, wft, cct, ttt, a16, s0, l):
    gb = GROUP_BLOCK
    op_spec = pl.BlockSpec((None, gb, CHUNK_WIDTH, CHUNK_WIDTH), lambda t, g: (l, g, 0, 0))
    io_spec = pl.BlockSpec((None, SCAN_CHUNK, gb * SSM_GROUP, J_ROWS), lambda t, g: (t, 0, g, 0))
    return pl.pallas_call(
        _ssm_kernel,
        grid=(N_TILES, N_GROUPS // gb),
        in_specs=[
            io_spec, op_spec, op_spec, op_spec,
            pl.BlockSpec((None, gb, 2, N_PS, 128), lambda t, g: (l, g, 0, 0, 0)),
            pl.BlockSpec((gb, N_PS, CHUNK_WIDTH), lambda t, g: (g, 0, 0)),
        ],
        out_specs=[io_spec, pl.BlockSpec((None, gb, N_PS, CHUNK_WIDTH), lambda t, g: (t, g, 0, 0))],
        out_shape=[
            jax.ShapeDtypeStruct((N_TILES, SCAN_CHUNK, D_SSM, J_ROWS), F32),
            jax.ShapeDtypeStruct((N_TILES, N_GROUPS, N_PS, CHUNK_WIDTH), F32),
        ],
        scratch_shapes=[pltpu.VMEM((J_ROWS, CHUNK_WIDTH), F32), pltpu.VMEM((CHUNK_WIDTH, J_ROWS), F32)],
        compiler_params=_params("arbitrary", "arbitrary"),
        name="s5_chunk_scan",
    )(xt, wft, cct, ttt, a16, s0)


def _post_kernel(x_ref, yt_ref, ug_ref, vn_ref, mod_ref, wglut_ref, ws_ref, bs_ref, wout_ref,
                 nw_ref, rwt_ref, rb_ref, hx_ref, route_ref, yg_scr, y_scr):
    tile = pl.program_id(0)
    step = pl.program_id(1)

    @pl.when(step == 0)
    def _():
        def chunk(n, carry):
            ps = n % N_PS
            c_hi = n // N_PS
            base = c_hi * (GMLP_CHUNK // SCAN_CHUNK) * N_PS + ps
            rows = [pl.ds(j * J_ROWS + base, GMLP_CHUNK // SCAN_CHUNK, stride=N_PS) for j in range(SCAN_CHUNK)]
            for h in range(GMLP_HEADS):
                v = jnp.concatenate([vn_ref[h, r, :] for r in rows], axis=0).astype(BF16)
                u = jnp.concatenate([ug_ref[h, r, :] for r in rows], axis=0)
                s = jnp.dot(ws_ref[h].astype(BF16), v, preferred_element_type=F32) + bs_ref[h]
                yg = u * s
                for j, r in enumerate(rows):
                    yg_scr[h, r, :] = yg[j * 8:(j + 1) * 8]
            return carry

        lax.fori_loop(0, TILE_TOKENS // GMLP_CHUNK, chunk, 0)

    wglut = wglut_ref[...].astype(BF16)
    for k in range(J_PER_STEP):
        yt = _gelu_tanh(yt_ref[k])
        yt = yt * _sigmoid(jnp.dot(wglut, yt.astype(BF16), preferred_element_type=F32))
        y_scr[k * J_ROWS:(k + 1) * J_ROWS, :] = yt.T
    row0 = pl.multiple_of(step * TOKEN_TILE, TOKEN_TILE)
    proj = jnp.dot(y_scr[...].astype(BF16), wout_ref[0:D_SSM, :].astype(BF16), preferred_element_type=F32)
    yg = jnp.concatenate([yg_scr[h, pl.ds(row0, TOKEN_TILE), :] for h in range(GMLP_HEADS)], axis=1)
    proj += jnp.dot(yg.astype(BF16), wout_ref[D_SSM:, :].astype(BF16), preferred_element_type=F32)
    x1 = x_ref[...] + _per_ps(lambda a, g: a * g, proj, mod_ref[2])
    h2 = _per_ps(lambda a, sc, sh: a * (1.0 + sc) + sh, _rmsnorm(x1, nw_ref[...]), mod_ref[4], mod_ref[3])
    hx_ref[:, 0:D_MODEL] = h2
    hx_ref[:, D_MODEL:2 * D_MODEL] = x1
    logits = lax.dot_general(rwt_ref[...], h2, TRANS_B, precision=lax.Precision.HIGHEST,
                             preferred_element_type=F32)
    scores = _sigmoid(logits)
    sel = scores + rb_ref[...]
    sc = [scores[e:e + 1, :] for e in range(N_EXPERTS)]
    sl = [sel[e:e + 1, :] for e in range(N_EXPERTS)]
    gscore = []
    for g in range(N_EXPERT_GROUPS):
        v0, v1, v2, v3 = sl[4 * g:4 * g + 4]
        hi01, lo01 = jnp.maximum(v0, v1), jnp.minimum(v0, v1)
        hi23, lo23 = jnp.maximum(v2, v3), jnp.minimum(v2, v3)
        top1 = jnp.maximum(hi01, hi23)
        top2 = jnp.maximum(jnp.minimum(hi01, hi23), jnp.maximum(lo01, lo23))
        gscore.append(top1 + top2)
    best = gscore[0]
    gidx = jnp.zeros_like(best, dtype=jnp.int32)
    for g in range(1, N_EXPERT_GROUPS):
        upd = gscore[g] > best
        gidx = jnp.where(upd, g, gidx)
        best = jnp.where(upd, gscore[g], best)

    def in_group(vals, k):
        out = vals[k]
        for g in range(1, N_EXPERT_GROUPS):
            out = jnp.where(gidx == g, vals[4 * g + k], out)
        return out

    v = [in_group(sl, k) for k in range(EXPERTS_PER_GROUP)]
    s = [in_group(sc, k) for k in range(EXPERTS_PER_GROUP)]
    w = []
    bits = jnp.zeros_like(gidx)
    for k in range(EXPERTS_PER_GROUP):
        rank = jnp.zeros_like(gidx)
        for j in range(EXPERTS_PER_GROUP):
            if j == k:
                continue
            ahead = (v[j] >= v[k]) if j < k else (v[j] > v[k])
            rank = rank + ahead.astype(jnp.int32)
        w.append(jnp.where(rank < 2, s[k], 0.0))
        bits = bits + jnp.where(rank < 2, 1 << k, 0)
    denom = (w[0] + w[1]) + (w[2] + w[3])
    gate = [wk / denom for wk in w]
    pair = jnp.full_like(gidx, len(PAIR_SLOT_A) - 1)
    for p in range(len(PAIR_SLOT_A) - 1):
        pair = jnp.where(bits == (1 << PAIR_SLOT_A[p]) + (1 << PAIR_SLOT_B[p]), p, pair)

    def slot_gate(table):
        out = gate[table[0]]
        for p in range(1, len(table)):
            out = jnp.where(pair == p, gate[table[p]], out)
        return out

    n_tok = route_ref.shape[1]
    route_ref[0:1, :] = (gidx * len(PAIR_SLOT_A) + pair).astype(F32)
    ps = lax.broadcasted_iota(jnp.int32, (1, n_tok), 1) % N_PS
    cond = jnp.where(tile < SAMPLE_TILE, 0, 1 + ps // SEGS_PER_SAMPLE_SEQ).astype(F32)
    lanes = jnp.concatenate([slot_gate(PAIR_SLOT_A), slot_gate(PAIR_SLOT_B), cond,
                             jnp.zeros((ROUTE_LANES - 3, n_tok), F32)], axis=0)
    hx_ref[:, 2 * D_MODEL:] = lanes.T


def _mix_out(x, yt, ug, vn, mod8, l, w_glu_t, w_s, b_s, w_out, norm2_w, rwt, rb):
    tm = TOKEN_TILE
    steps = TILE_TOKENS // tm
    tok = lambda n: pl.BlockSpec((tm, n), lambda t, s: (t * steps + s, 0))
    whole_tile = pl.BlockSpec((GMLP_HEADS, TILE_TOKENS, GMLP_HEAD_DIM), lambda t, s: (0, t, 0))
    lay = lambda *shape: pl.BlockSpec((None,) + shape, lambda t, s: (l,) + (0,) * len(shape))
    return pl.pallas_call(
        _post_kernel,
        grid=(N_TILES, steps),
        in_specs=[
            tok(D_MODEL),
            pl.BlockSpec((None, J_PER_STEP, D_SSM, J_ROWS), lambda t, s: (t, s, 0, 0)),
            whole_tile, whole_tile,
            pl.BlockSpec((None, None, N_MOD, N_PS, D_MODEL), lambda t, s: (l, t, 0, 0, 0)),
            lay(D_SSM, D_SSM), lay(GMLP_HEADS, GMLP_CHUNK, GMLP_CHUNK),
            lay(GMLP_HEADS, GMLP_CHUNK, 1), lay(D_MODEL, D_MODEL), lay(1, D_MODEL),
            pl.BlockSpec((N_EXPERTS, D_MODEL), lambda t, s: (0, 0)),
            pl.BlockSpec((N_EXPERTS, 1), lambda t, s: (0, 0)),
        ],
        out_specs=[tok(HX_WIDTH), pl.BlockSpec((1, tm), lambda t, s: (0, t * steps + s))],
        out_shape=[
            jax.ShapeDtypeStruct((T_ALL, HX_WIDTH), F32),
            jax.ShapeDtypeStruct((1, T_ALL), F32),
        ],
        scratch_shapes=[pltpu.VMEM((GMLP_HEADS, TILE_TOKENS, GMLP_HEAD_DIM), F32), pltpu.VMEM((tm, D_SSM), F32)],
        compiler_params=_params("arbitrary", "arbitrary"),
        name="mixers_out_router",
    )(x, yt, ug, vn, mod8, w_glu_t, w_s, b_s, w_out, norm2_w.reshape(DEPTH, 1, D_MODEL), rwt, rb)


def _moe_kernel(ea_ref, eb_ref, new_a_ref, new_b_ref, nv_ref, src_ref, drow_ref,
                hx_hbm, wga_ref, wua_ref, wda_ref, wgb_ref, wub_ref, wdb_ref, g2_ref, fw_ref,
                o_hbm, wga_s, wua_s, wda_s, wgb_s, wub_s, wdb_s, hx_buf, o_buf, z_buf, g_sem, s_sem, z_sem,
                *, final):
    n = pl.program_id(0)
    n_valid = nv_ref[0]
    slot = n % 2

    def start_gather(tile, sl):
        base = tile * MOE_TM
        for r in range(MOE_TM):
            pltpu.make_async_copy(hx_hbm.at[pl.ds(src_ref[base + r], 1)], hx_buf.at[sl, pl.ds(r, 1)],
                                  g_sem.at[sl]).start()

    def wait_gather(sl):
        pltpu.make_async_copy(hx_hbm.at[pl.ds(0, MOE_TM)], hx_buf.at[sl], g_sem.at[sl]).wait()

    def start_scatter(tile, sl):
        base = tile * MOE_TM
        for r in range(MOE_TM):
            pltpu.make_async_copy(o_buf.at[sl, pl.ds(r, 1)], o_hbm.at[pl.ds(drow_ref[base + r], 1)],
                                  s_sem.at[sl]).start()

    def wait_scatter(sl):
        pltpu.make_async_copy(o_buf.at[sl], o_hbm.at[pl.ds(0, MOE_TM)], s_sem.at[sl]).wait()

    def zero_fill(tile):
        return pltpu.make_async_copy(z_buf, o_hbm.at[pl.ds(tile * MOE_TM, MOE_TM)], z_sem)

    @pl.when(n == 0)
    def _():
        z_buf[...] = jnp.zeros_like(z_buf)
        start_gather(0, 0)

    @pl.when(n >= n_valid)
    def _():
        @pl.when(n > n_valid)
        def _():
            zero_fill(n - 1).wait()

        zero_fill(n).start()

        @pl.when(n == MOE_TILES - 1)
        def _():
            zero_fill(n).wait()

    @pl.when(new_a_ref[n] == 1)
    def _():
        wga_s[...] = wga_ref[...].astype(BF16)
        wua_s[...] = wua_ref[...].astype(BF16)
        wda_s[...] = wda_ref[...].astype(BF16)

    @pl.when(new_b_ref[n] == 1)
    def _():
        wgb_s[...] = wgb_ref[...].astype(BF16)
        wub_s[...] = wub_ref[...].astype(BF16)
        wdb_s[...] = wdb_ref[...].astype(BF16)

    @pl.when(n + 1 < n_valid)
    def _():
        start_gather(n + 1, 1 - slot)

    @pl.when(n < n_valid)
    def _():
        wait_gather(slot)

        @pl.when(n >= 2)
        def _():
            wait_scatter(slot)

        h = hx_buf[slot, :, 0:D_MODEL].astype(BF16)
        lanes = hx_buf[slot, :, 2 * D_MODEL:]

        def expert(wg, wu, wd, gate):
            hg = jnp.dot(h, wg[...], preferred_element_type=F32)
            hu = jnp.dot(h, wu[...], preferred_element_type=F32)
            act = hg * _sigmoid(hg) * hu * gate
            return jnp.dot(act.astype(BF16), wd[...], preferred_element_type=F32)

        y = expert(wga_s, wua_s, wda_s, lanes[:, 0:1]) + expert(wgb_s, wub_s, wdb_s, lanes[:, 1:2])
        cond_row = lanes[:, 2:3]
        gate2 = jnp.where(cond_row == 0.0, g2_ref[0:1, :], jnp.where(cond_row == 1.0, g2_ref[1:2, :], g2_ref[2:3, :]))
        x2 = hx_buf[slot, :, D_MODEL:2 * D_MODEL] + gate2 * y
        o_buf[slot] = _rmsnorm(x2, fw_ref[...]) if final else x2
        start_scatter(n, slot)

    @pl.when(n == MOE_TILES - 1)
    def _():
        wait_scatter(0)
        wait_scatter(1)


def _experts(tables, hx, mods, l, w_gate, w_up, w_down, final_w, final):
    w_a = lambda r, c: pl.BlockSpec((None, None, r, c), lambda n, ea, eb, *_: (l, ea[n], 0, 0))
    w_b = lambda r, c: pl.BlockSpec((None, None, r, c), lambda n, ea, eb, *_: (l, eb[n], 0, 0))
    up = pltpu.VMEM((D_MODEL, D_EXPERT), BF16)
    down = pltpu.VMEM((D_EXPERT, D_MODEL), BF16)
    return pl.pallas_call(
        functools.partial(_moe_kernel, final=final),
        grid_spec=pltpu.PrefetchScalarGridSpec(
            num_scalar_prefetch=len(tables),
            grid=(MOE_TILES,),
            in_specs=[
                pl.BlockSpec(memory_space=pl.ANY),
                w_a(D_MODEL, D_EXPERT), w_a(D_MODEL, D_EXPERT), w_a(D_EXPERT, D_MODEL),
                w_b(D_MODEL, D_EXPERT), w_b(D_MODEL, D_EXPERT), w_b(D_EXPERT, D_MODEL),
                pl.BlockSpec((None, None, MOD_ROWS, D_MODEL), lambda n, *_: (l, N_MOD - 1, 0, 0)),
                pl.BlockSpec((1, D_MODEL), lambda n, *_: (0, 0)),
            ],
            out_specs=pl.BlockSpec(memory_space=pl.ANY),
            scratch_shapes=[up, up, down, up, up, down,
                            pltpu.VMEM((2, MOE_TM, HX_WIDTH), F32), pltpu.VMEM((2, MOE_TM, D_MODEL), F32),
                            pltpu.VMEM((MOE_TM, D_MODEL), F32),
                            pltpu.SemaphoreType.DMA((2,)), pltpu.SemaphoreType.DMA((2,)), pltpu.SemaphoreType.DMA(())],
        ),
        out_shape=jax.ShapeDtypeStruct((MOE_ROWS, D_MODEL), F32),
        compiler_params=_params("arbitrary"),
        name="experts",
    )(*tables, hx, w_gate, w_up, w_down, w_gate, w_up, w_down, mods, final_w.reshape(1, D_MODEL))


def _routing_tables(cls, target_row):
    n_cls = N_EXPERT_GROUPS * len(PAIR_SLOT_A)
    onehot = (cls[:, None] == jnp.arange(n_cls, dtype=jnp.int32)[None, :]).astype(jnp.int32)
    counts = jnp.sum(onehot, axis=0)
    rank = jnp.take_along_axis(jnp.cumsum(onehot, axis=0) - onehot, cls[:, None], axis=1)[:, 0]
    tiles = (counts + MOE_TM - 1) // MOE_TM
    tile_end = jnp.cumsum(tiles)
    dst = (tile_end - tiles)[cls] * MOE_TM + rank
    token_plus_1 = jnp.zeros((MOE_ROWS,), jnp.int32).at[dst].set(jnp.arange(1, T_ALL + 1, dtype=jnp.int32))
    is_pad = token_plus_1 == 0
    src = jnp.maximum(token_plus_1 - 1, 0)
    spare = T_ALL + jnp.cumsum(is_pad.astype(jnp.int32)) - 1
    drow = jnp.where(is_pad, spare, target_row[src]).astype(jnp.int32)
    n_valid = tile_end[-1]
    tile_id = jnp.minimum(jnp.arange(MOE_TILES, dtype=jnp.int32), n_valid - 1)
    tile_cls = jnp.sum((tile_end[None, :] <= tile_id[:, None]).astype(jnp.int32), axis=1)
    group, pair = tile_cls // len(PAIR_SLOT_A), tile_cls % len(PAIR_SLOT_A)
    e_a = group * EXPERTS_PER_GROUP + jnp.array(PAIR_SLOT_A, jnp.int32)[pair]
    e_b = group * EXPERTS_PER_GROUP + jnp.array(PAIR_SLOT_B, jnp.int32)[pair]
    first = jnp.ones((1,), jnp.int32)
    new_a = jnp.concatenate([first, (e_a[1:] != e_a[:-1]).astype(jnp.int32)])
    new_b = jnp.concatenate([first, (e_b[1:] != e_b[:-1]).astype(jnp.int32)])
    return e_a, e_b, new_a, new_b, n_valid.reshape(1).astype(jnp.int32), src, drow


def _dirs_on_lanes(p):
    p = jnp.moveaxis(p, 1, -2)
    return p.reshape(p.shape[:-2] + (2 * SSM_STATE,))


def _to_internal_order(x):
    x = x.reshape(N_TILES, N_PS, SEG_CHUNKS, SCAN_CHUNK, D_MODEL)
    return x.transpose(0, 3, 2, 1, 4).reshape(T_ALL, D_MODEL)


def _gmlp_position_order(w):
    n_lo = GMLP_CHUNK // SCAN_CHUNK
    lead = w.shape[:2]
    w = w.reshape(lead + (n_lo, SCAN_CHUNK) + w.shape[3:])
    w = jnp.swapaxes(w, 2, 3)
    return w.reshape(lead + (GMLP_CHUNK,) + w.shape[4:])


def kernel(x_prompt, x_sample, c, state_ssm_re, state_ssm_im, c_ctx, norm1_w, norm2_w, w_mod, b_mod, w_in,
           ssm_a_re, ssm_a_im, ssm_log_dt, ssm_b_re, ssm_b_im, ssm_c_re, ssm_c_im, ssm_d, w_glu,
           gmlp_ln_w, gmlp_ln_b, gmlp_w_s, gmlp_b_s, w_out, router_w, router_b, w_gate, w_up, w_down,
           final_norm_w):
    x = jnp.concatenate([x_prompt.reshape(T_PROMPT, D_MODEL), x_sample.reshape(T_SAMPLE, D_MODEL)], axis=0)
    x = _to_internal_order(x)

    cvec = jnp.concatenate([c_ctx[None, :], c, jnp.zeros((MOD_ROWS - 1 - N_SAMPLE_SEQ, D_MODEL), F32)], axis=0)
    mods = _modulation(cvec, w_mod, b_mod)
    ps_row = [[0] * N_PS] * SAMPLE_TILE + [[1 + p // SEGS_PER_SAMPLE_SEQ for p in range(N_PS)]]
    mod8 = mods[:, :, jnp.array(ps_row, jnp.int32), :].transpose(0, 2, 1, 3, 4)

    log_dt = jnp.broadcast_to(ssm_log_dt[..., None], ssm_a_re.shape)
    d_lanes = jnp.tile(ssm_d.reshape(DEPTH, N_GROUPS, SSM_GROUP), (1, 1, 128 // SSM_GROUP))
    vecs = jnp.stack([_dirs_on_lanes(ssm_a_re), _dirs_on_lanes(ssm_a_im), _dirs_on_lanes(log_dt), d_lanes], axis=2)
    vecs = jnp.concatenate([vecs, jnp.zeros((DEPTH, N_GROUPS, 4, 2 * SSM_STATE), F32)], axis=2)
    mats = jnp.stack([_dirs_on_lanes(jnp.swapaxes(ssm_b_re, -1, -2)), _dirs_on_lanes(jnp.swapaxes(ssm_b_im, -1, -2)),
                      _dirs_on_lanes(ssm_c_re), _dirs_on_lanes(ssm_c_im)], axis=2)
    wft, cct, ttt, a16 = _ssm_operators(vecs, mats)

    w_in_g = w_in[:, :, D_SSM:]
    w_in_st = jnp.swapaxes(w_in[:, :, :D_SSM], 1, 2)
    w_glu_t = jnp.swapaxes(w_glu, 1, 2)
    w_s = jnp.swapaxes(_gmlp_position_order(jnp.swapaxes(_gmlp_position_order(gmlp_w_s), 2, 3)), 2, 3)
    b_s = _gmlp_position_order(gmlp_b_s)[..., None]
    rwt = router_w.T
    rb = router_b.reshape(N_EXPERTS, 1)
    token_pos = jnp.arange(T_ALL, dtype=jnp.int32)
    seq_pos = np.arange(T_ALL).reshape(N_TILES, N_PS, SEG_CHUNKS, SCAN_CHUNK).transpose(0, 3, 2, 1)
    sequence_pos = jnp.asarray(seq_pos.reshape(T_ALL), jnp.int32)

    new_re, new_im = [], []
    for l in range(DEPTH):
        ug, vn, xt = _input_proj(x, mod8, l, norm1_w, w_in_g, w_in_st, gmlp_ln_w, gmlp_ln_b)
        s0 = jnp.concatenate([state_ssm_re[:, l].transpose(2, 0, 1, 3).reshape(N_GROUPS, N_SAMPLE_SEQ, 128),
                              state_ssm_im[:, l].transpose(2, 0, 1, 3).reshape(N_GROUPS, N_SAMPLE_SEQ, 128)], axis=-1)
        s0 = jnp.repeat(s0, SEGS_PER_SAMPLE_SEQ, axis=1)
        yt, fin = _ssm_scan(xt, wft, cct, ttt, a16, s0, l)
        fin = fin[:SAMPLE_TILE].reshape(SAMPLE_TILE, N_GROUPS, N_PS, 2, 2, SSM_STATE)
        fin = fin.transpose(3, 0, 2, 4, 1, 5).reshape(2, N_PROMPT_SEQ, 2, N_GROUPS, SSM_STATE)
        new_re.append(fin[0])
        new_im.append(fin[1])
        hx, route = _mix_out(x, yt, ug, vn, mod8, l, w_glu_t, w_s, b_s, w_out, norm2_w, rwt, rb)
        final = l == DEPTH - 1
        tables = _routing_tables(route[0].astype(jnp.int32), sequence_pos if final else token_pos)
        x = _experts(tables, hx, mods, l, w_gate, w_up, w_down, final_norm_w, final)

    y_prompt = x[:T_PROMPT].reshape(N_PROMPT_SEQ, PROMPT_LEN, D_MODEL)
    y_sample = x[T_PROMPT:T_ALL].reshape(N_SAMPLE_SEQ, SAMPLE_LEN, D_MODEL)
    return (y_prompt, y_sample, jnp.stack(new_re, axis=1), jnp.stack(new_im, axis=1))
```

```python
import functools
import math

import jax
import jax.numpy as jnp
import numpy as np
from jax import lax
from jax.experimental import pallas as pl
from jax.experimental.pallas import tpu as pltpu

F32 = jnp.float32
BF16 = jnp.bfloat16

D_MODEL = 1024
N_PROMPT_SEQ = 16
PROMPT_LEN = 256
N_SAMPLE_SEQ = 2
SAMPLE_LEN = 1024
T_PROMPT = N_PROMPT_SEQ * PROMPT_LEN
T_SAMPLE = N_SAMPLE_SEQ * SAMPLE_LEN
T_ALL = T_PROMPT + T_SAMPLE
DEPTH = 2
D_SSM = 512
SSM_GROUP = 16
N_GROUPS = 32
SSM_STATE = 64
D_GMLP = 512
GMLP_HEADS = 4
GMLP_HEAD_DIM = 128
GMLP_CHUNK = 128
N_EXPERTS = 16
N_EXPERT_GROUPS = 4
EXPERTS_PER_GROUP = 4
D_EXPERT = 512
N_MOD = 6
EPS = 1e-6

SCAN_CHUNK = 16
CHUNK_WIDTH = SCAN_CHUNK * SSM_GROUP
SEG_LEN = 256
SEG_CHUNKS = SEG_LEN // SCAN_CHUNK
N_PS = 8
TILE_TOKENS = N_PS * SEG_LEN
N_TILES = T_ALL // TILE_TOKENS
SAMPLE_TILE = T_PROMPT // TILE_TOKENS
SEGS_PER_SAMPLE_SEQ = SAMPLE_LEN // SEG_LEN
J_ROWS = SEG_CHUNKS * N_PS
GROUP_BLOCK = 8
MOD_ROWS = 8

J_PER_STEP = 4
TOKEN_TILE = J_PER_STEP * J_ROWS
PAIR_SLOT_A = (0, 0, 0, 1, 1, 3)
PAIR_SLOT_B = (1, 2, 3, 3, 2, 2)
MOE_TM = 128
MOE_TILES = T_ALL // MOE_TM + N_EXPERT_GROUPS * len(PAIR_SLOT_A)
MOE_ROWS = MOE_TILES * MOE_TM
ROUTE_LANES = 128
HX_WIDTH = 2 * D_MODEL + ROUTE_LANES
VMEM_LIMIT = 56 * 1024 * 1024
TRANS_B = (((1,), (1,)), ((), ()))


def _sigmoid(x):
    return 1.0 / (1.0 + jnp.exp(-x))


def _gelu_tanh(x):
    c = math.sqrt(2.0 / math.pi)
    return x * (0.5 * (1.0 + jnp.tanh(c * (x + 0.044715 * (x * x * x)))))


def _rmsnorm(x, w):
    return x * lax.rsqrt(jnp.mean(x * x, axis=-1, keepdims=True) + EPS) * w


def _per_ps(fn, a, *mods):
    rows, d = a.shape
    out = fn(a.reshape(rows // N_PS, N_PS, d), *[m[None] for m in mods])
    return out.reshape(rows, d)


def _params(*sem):
    return pltpu.CompilerParams(dimension_semantics=sem, vmem_limit_bytes=VMEM_LIMIT)


def _mod_kernel(c_ref, w_ref, b_ref, o_ref):
    c = c_ref[...]
    s = c * _sigmoid(c)
    o_ref[...] = jnp.dot(s.astype(BF16), w_ref[...].astype(BF16), preferred_element_type=F32) + b_ref[...]


def _modulation(cvec, w_mod, b_mod):
    return pl.pallas_call(
        _mod_kernel,
        grid=(DEPTH, N_MOD),
        in_specs=[
            pl.BlockSpec((MOD_ROWS, D_MODEL), lambda l, n: (0, 0)),
            pl.BlockSpec((None, D_MODEL, D_MODEL), lambda l, n: (l, 0, n)),
            pl.BlockSpec((None, None, 1, D_MODEL), lambda l, n: (l, n, 0, 0)),
        ],
        out_specs=pl.BlockSpec((None, None, MOD_ROWS, D_MODEL), lambda l, n: (l, n, 0, 0)),
        out_shape=jax.ShapeDtypeStruct((DEPTH, N_MOD, MOD_ROWS, D_MODEL), F32),
        compiler_params=_params("arbitrary", "arbitrary"),
        name="adaln_mod",
    )(cvec, w_mod, b_mod.reshape(DEPTH, N_MOD, 1, D_MODEL))


def _in_kernel(x_ref, mod_ref, nw_ref, wg_ref, wst_ref, lnw_ref, lnb_ref, ug_ref, vn_ref, xt_ref):
    y = _rmsnorm(x_ref[...], nw_ref[...])
    h = _per_ps(lambda a, sc, sh: a * (1.0 + sc) + sh, y, mod_ref[1], mod_ref[0]).astype(BF16)
    zg = _gelu_tanh(jnp.dot(h, wg_ref[...], preferred_element_type=F32))
    v = zg[:, D_GMLP:]
    mu = jnp.mean(v, axis=-1, keepdims=True)
    vc = v - mu
    var = jnp.mean(vc * vc, axis=-1, keepdims=True)
    vn = vc * lax.rsqrt(var + EPS) * lnw_ref[...] + lnb_ref[...]
    for hd in range(GMLP_HEADS):
        cols = slice(hd * GMLP_HEAD_DIM, (hd + 1) * GMLP_HEAD_DIM)
        ug_ref[hd] = zg[:, cols]
        vn_ref[hd] = vn[:, cols]
    wst = wst_ref[...]
    for k in range(J_PER_STEP):
        hk = h[k * J_ROWS:(k + 1) * J_ROWS]
        xt_ref[k] = lax.dot_general(wst, hk, TRANS_B, preferred_element_type=F32).astype(BF16)


def _input_proj(x, mod8, l, norm1_w, w_in_g, w_in_st, ln_w, ln_b):
    tm = TOKEN_TILE
    steps = TILE_TOKENS // tm
    tok = lambda n: pl.BlockSpec((tm, n), lambda t, s: (t * steps + s, 0))
    heads = pl.BlockSpec((GMLP_HEADS, tm, GMLP_HEAD_DIM), lambda t, s: (0, t * steps + s, 0))
    lay = lambda *shape: pl.BlockSpec((None,) + shape, lambda t, s: (l,) + (0,) * len(shape))
    return pl.pallas_call(
        _in_kernel,
        grid=(N_TILES, steps),
        in_specs=[
            tok(D_MODEL),
            pl.BlockSpec((None, None, N_MOD, N_PS, D_MODEL), lambda t, s: (l, t, 0, 0, 0)),
            lay(1, D_MODEL), lay(D_MODEL, 2 * D_GMLP), lay(D_SSM, D_MODEL), lay(1, D_GMLP), lay(1, D_GMLP),
        ],
        out_specs=[heads, heads,
                   pl.BlockSpec((None, J_PER_STEP, D_SSM, J_ROWS), lambda t, s: (t, s, 0, 0))],
        out_shape=[jax.ShapeDtypeStruct((GMLP_HEADS, T_ALL, GMLP_HEAD_DIM), F32)] * 2 + [
            jax.ShapeDtypeStruct((N_TILES, SCAN_CHUNK, D_SSM, J_ROWS), BF16)],
        compiler_params=_params("arbitrary", "arbitrary"),
        name="norm1_in_proj",
    )(x, mod8, norm1_w.reshape(DEPTH, 1, D_MODEL), w_in_g, w_in_st,
      ln_w.reshape(DEPTH, 1, D_GMLP), ln_b.reshape(DEPTH, 1, D_GMLP))


def _shift_lanes_right(a, b, s, lane):
    if s == 0:
        return a, b
    if s == 128:
        return jnp.zeros_like(a), a
    if s < 128:
        ra = pltpu.roll(a, s, 1)
        rb = pltpu.roll(b, s, 1)
        return jnp.where(lane >= s, ra, 0.0), jnp.where(lane >= s, rb, ra)
    t = s - 128
    return jnp.zeros_like(a), jnp.where(lane >= t, pltpu.roll(a, t, 1), 0.0)


def _shift_lanes_left(a, b, s, lane):
    if s == 0:
        return a, b
    if s == 128:
        return b, jnp.zeros_like(b)
    if s < 128:
        ra = pltpu.roll(a, 128 - s, 1)
        rb = pltpu.roll(b, 128 - s, 1)
        return jnp.where(lane < 128 - s, ra, rb), jnp.where(lane < 128 - s, rb, 0.0)
    t = s - 128
    return jnp.where(lane < 128 - t, pltpu.roll(b, 128 - t, 1), 0.0), jnp.zeros_like(b)


def _prep_kernel(vec_ref, mat_ref, wft_ref, cct_ref, ttt_ref, a16_ref, wf_scr, cm_scr, tt_scr):
    a_re = vec_ref[0:1, :]
    a_im = vec_ref[1:2, :]
    dt = jnp.exp(vec_ref[2:3, :])
    d_skip = vec_ref[3:4, :]
    mag = jnp.exp(a_re * dt)
    ang = a_im * dt
    ab_r = mag * jnp.cos(ang)
    ab_i = mag * jnp.sin(ang)
    den = a_re * a_re + a_im * a_im
    nr = ab_r - 1.0
    q_r = (nr * a_re + ab_i * a_im) / den
    q_i = (ab_i * a_re - nr * a_im) / den
    bt_r = mat_ref[0]
    bt_i = mat_ref[1]
    c_r = mat_ref[2]
    c_i = mat_ref[3]
    bb_r = q_r * bt_r - q_i * bt_i
    bb_i = q_r * bt_i + q_i * bt_r
    p_r = [jnp.ones_like(ab_r)]
    p_i = [jnp.zeros_like(ab_r)]
    for _ in range(SCAN_CHUNK):
        pr, pi = p_r[-1], p_i[-1]
        p_r.append(pr * ab_r - pi * ab_i)
        p_i.append(pr * ab_i + pi * ab_r)
    a16_ref[0] = jnp.broadcast_to(p_r[SCAN_CHUNK], (N_PS, 128))
    a16_ref[1] = jnp.broadcast_to(p_i[SCAN_CHUNK], (N_PS, 128))

    lane = lax.broadcasted_iota(jnp.int32, (1, 128), 1)
    is_fwd = lane < SSM_STATE

    def pick(mf, mb):
        return jnp.where(is_fwd, p_r[mf], p_r[mb]), jnp.where(is_fwd, p_i[mf], p_i[mb])

    for j in range(SCAN_CHUNK):
        rows = slice(j * SSM_GROUP, (j + 1) * SSM_GROUP)
        wr, wi = pick(SCAN_CHUNK - 1 - j, j)
        wf_scr[rows, 0:128] = bb_r * wr - bb_i * wi
        wf_scr[rows, 128:256] = bb_r * wi + bb_i * wr
        wr, wi = pick(j + 1, SCAN_CHUNK - j)
        cct_ref[rows, 0:128] = (c_r * wr - c_i * wi).astype(BF16)
        cct_ref[rows, 128:256] = (-(c_r * wi + c_i * wr)).astype(BF16)
        wr, wi = pick(j, SCAN_CHUNK - 1 - j)
        cm_scr[rows, 0:128] = c_r * wr - c_i * wi
        cm_scr[rows, 128:256] = c_r * wi + c_i * wr
    wft_ref[...] = wf_scr[...].T.astype(BF16)

    zero = jnp.zeros_like(bb_r)
    cm = cm_scr[...]

    def lag_rows(keep):
        lhs = jnp.concatenate([jnp.where(keep, bb_r, zero), jnp.where(keep, -bb_i, zero)], axis=1)
        return lax.dot_general(lhs, cm, TRANS_B, precision=lax.Precision.HIGHEST, preferred_element_type=F32)

    mf = lag_rows(is_fwd)
    mb = lag_rows(jnp.logical_not(is_fwd))
    mf_a, mf_b = mf[:, 0:128], mf[:, 128:256]
    mb_a, mb_b = mb[:, 0:128], mb[:, 128:256]
    row_h = lax.broadcasted_iota(jnp.int32, (SSM_GROUP, 128), 0)
    lane_h = lax.broadcasted_iota(jnp.int32, (SSM_GROUP, 128), 1)
    for jp in range(SCAN_CHUNK):
        rows = slice(jp * SSM_GROUP, (jp + 1) * SSM_GROUP)
        fa, fb = _shift_lanes_right(mf_a, mf_b, SSM_GROUP * jp, lane)
        ba, bb = _shift_lanes_left(mb_a, mb_b, SSM_GROUP * (SCAN_CHUNK - 1 - jp), lane)
        diag = SSM_GROUP * jp + row_h
        tt_scr[rows, 0:128] = fa + ba + jnp.where(lane_h == diag, d_skip, 0.0)
        tt_scr[rows, 128:256] = fb + bb + jnp.where(lane_h + 128 == diag, d_skip, 0.0)
    ttt_ref[...] = tt_scr[...].T.astype(BF16)


def _ssm_operators(vecs, mats):
    op = jax.ShapeDtypeStruct((DEPTH, N_GROUPS, CHUNK_WIDTH, CHUNK_WIDTH), BF16)
    op_spec = pl.BlockSpec((None, None, CHUNK_WIDTH, CHUNK_WIDTH), lambda l, g: (l, g, 0, 0))
    sq = pltpu.VMEM((CHUNK_WIDTH, CHUNK_WIDTH), F32)
    return pl.pallas_call(
        _prep_kernel,
        grid=(DEPTH, N_GROUPS),
        in_specs=[
            pl.BlockSpec((None, None, 8, 128), lambda l, g: (l, g, 0, 0)),
            pl.BlockSpec((None, None, 4, SSM_GROUP, 128), lambda l, g: (l, g, 0, 0, 0)),
        ],
        out_specs=[op_spec, op_spec, op_spec,
                   pl.BlockSpec((None, None, 2, N_PS, 128), lambda l, g: (l, g, 0, 0, 0))],
        out_shape=[op, op, op, jax.ShapeDtypeStruct((DEPTH, N_GROUPS, 2, N_PS, 128), F32)],
        scratch_shapes=[sq, sq, sq],
        compiler_params=_params("arbitrary", "arbitrary"),
        name="s5_operators",
    )(vecs, mats)


def _ssm_kernel(xt_ref, wft_ref, cct_ref, ttt_ref, a16_ref, s0_ref, yt_ref, fin_ref, s_scr, f_scr, ft_scr):
    tile = pl.program_id(0)
    lane = lax.broadcasted_iota(jnp.int32, (GROUP_BLOCK, N_PS, 128), 2)
    is_fwd = lane < SSM_STATE
    half = SSM_STATE

    def group_x(gl):
        ch0 = pl.multiple_of(gl * SSM_GROUP, SSM_GROUP)
        return xt_ref[:, pl.ds(ch0, SSM_GROUP), :].reshape(CHUNK_WIDTH, J_ROWS)

    def summaries(gl, carry):
        ft_scr[...] = jnp.dot(wft_ref[gl], group_x(gl), preferred_element_type=F32)
        f_scr[gl] = ft_scr[...].T
        return carry

    lax.fori_loop(0, GROUP_BLOCK, summaries, 0)

    a_r = a16_ref[:, 0]
    a_i = a16_ref[:, 1]

    def scan(s_r, s_i):
        for i in range(SEG_CHUNKS):
            rf = slice(i * N_PS, (i + 1) * N_PS)
            rb = slice((SEG_CHUNKS - 1 - i) * N_PS, (SEG_CHUNKS - i) * N_PS)
            s_scr[:, rf, 0:half] = s_r[:, :, 0:half]
            s_scr[:, rf, 128:128 + half] = s_i[:, :, 0:half]
            s_scr[:, rb, half:128] = s_r[:, :, half:128]
            s_scr[:, rb, 128 + half:256] = s_i[:, :, half:128]
            f_r = jnp.where(is_fwd, f_scr[:, rf, 0:128], f_scr[:, rb, 0:128])
            f_i = jnp.where(is_fwd, f_scr[:, rf, 128:256], f_scr[:, rb, 128:256])
            s_r, s_i = a_r * s_r - a_i * s_i + f_r, a_r * s_i + a_i * s_r + f_i
        return s_r, s_i

    zeros = jnp.zeros((GROUP_BLOCK, N_PS, 128), F32)
    z_r, z_i = scan(zeros, zeros)
    fin_ref[:, :, 0:128] = z_r
    fin_ref[:, :, 128:256] = z_i

    @pl.when(tile == SAMPLE_TILE)
    def _():
        b_r, b_i = a_r, a_i
        for _ in range(4):
            b_r, b_i = b_r * b_r - b_i * b_i, 2.0 * (b_r * b_i)
        seg = lax.broadcasted_iota(jnp.int32, (GROUP_BLOCK, N_PS, 128), 1) % SEGS_PER_SAMPLE_SEQ
        i_r = s0_ref[:, :, 0:128]
        i_i = s0_ref[:, :, 128:256]
        for step in range(1, SEGS_PER_SAMPLE_SEQ):
            pr = jnp.where(is_fwd, pltpu.roll(i_r, 1, 1), pltpu.roll(i_r, N_PS - 1, 1))
            pi = jnp.where(is_fwd, pltpu.roll(i_i, 1, 1), pltpu.roll(i_i, N_PS - 1, 1))
            zr = jnp.where(is_fwd, pltpu.roll(z_r, 1, 1), pltpu.roll(z_r, N_PS - 1, 1))
            zi = jnp.where(is_fwd, pltpu.roll(z_i, 1, 1), pltpu.roll(z_i, N_PS - 1, 1))
            n_r = b_r * pr - b_i * pi + zr
            n_i = b_r * pi + b_i * pr + zi
            first = jnp.where(is_fwd, step, 0)
            last = jnp.where(is_fwd, SEGS_PER_SAMPLE_SEQ - 1, SEGS_PER_SAMPLE_SEQ - 1 - step)
            upd = jnp.logical_and(seg >= first, seg <= last)
            i_r = jnp.where(upd, n_r, i_r)
            i_i = jnp.where(upd, n_i, i_i)
        scan(i_r, i_i)

    def outputs(gl, carry):
        ch0 = pl.multiple_of(gl * SSM_GROUP, SSM_GROUP)
        yt = jnp.dot(ttt_ref[gl], group_x(gl), preferred_element_type=F32)
        yt += lax.dot_general(cct_ref[gl], s_scr[gl].astype(BF16), TRANS_B, preferred_element_type=F32)
        yt_ref[:, pl.ds(ch0, SSM_GROUP), :] = yt.reshape(SCAN_CHUNK, SSM_GROUP, J_ROWS)
        return carry

    lax.fori_loop(0, GROUP_BLOCK, outputs, 0)


def _ssm_scan(xt, wft, cct, ttt, a16, s0, l):
    gb = GROUP_BLOCK
    op_spec = pl.BlockSpec((None, gb, CHUNK_WIDTH, CHUNK_WIDTH), lambda t, g: (l, g, 0, 0))
    io_spec = pl.BlockSpec((None, SCAN_CHUNK, gb * SSM_GROUP, J_ROWS), lambda t, g: (t, 0, g, 0))
    return pl.pallas_call(
        _ssm_kernel,
        grid=(N_TILES, N_GROUPS // gb),
        in_specs=[
            io_spec, op_spec, op_spec, op_spec,
            pl.BlockSpec((None, gb, 2, N_PS, 128), lambda t, g: (l, g, 0, 0, 0)),
            pl.BlockSpec((gb, N_PS, CHUNK_WIDTH), lambda t, g: (g, 0, 0)),
        ],
        out_specs=[io_spec, pl.BlockSpec((None, gb, N_PS, CHUNK_WIDTH), lambda t, g: (t, g, 0, 0))],
        out_shape=[
            jax.ShapeDtypeStruct((N_TILES, SCAN_CHUNK, D_SSM, J_ROWS), F32),
            jax.ShapeDtypeStruct((N_TILES, N_GROUPS, N_PS, CHUNK_WIDTH), F32),
        ],
        scratch_shapes=[pltpu.VMEM((gb, J_ROWS, CHUNK_WIDTH), F32), pltpu.VMEM((gb, J_ROWS, CHUNK_WIDTH), F32),
                        pltpu.VMEM((CHUNK_WIDTH, J_ROWS), F32)],
        compiler_params=_params("arbitrary", "arbitrary"),
        name="s5_chunk_scan",
    )(xt, wft, cct, ttt, a16, s0)


def _post_kernel(x_ref, yt_ref, ug_ref, vn_ref, mod_ref, wglut_ref, ws_ref, bs_ref, wout_ref,
                 nw_ref, rwt_ref, rb_ref, hx_ref, route_ref, yg_scr, y_scr):
    tile = pl.program_id(0)
    step = pl.program_id(1)

    @pl.when(step == 0)
    def _():
        def chunk(n, carry):
            ps = n % N_PS
            c_hi = n // N_PS
            base = c_hi * (GMLP_CHUNK // SCAN_CHUNK) * N_PS + ps
            rows = [pl.ds(j * J_ROWS + base, GMLP_CHUNK // SCAN_CHUNK, stride=N_PS) for j in range(SCAN_CHUNK)]
            for h in range(GMLP_HEADS):
                v = jnp.concatenate([vn_ref[h, r, :] for r in rows], axis=0).astype(BF16)
                u = jnp.concatenate([ug_ref[h, r, :] for r in rows], axis=0)
                s = jnp.dot(ws_ref[h], v, preferred_element_type=F32) + bs_ref[h]
                yg = u * s
                for j, r in enumerate(rows):
                    yg_scr[h, r, :] = yg[j * 8:(j + 1) * 8]
            return carry

        lax.fori_loop(0, TILE_TOKENS // GMLP_CHUNK, chunk, 0)

    wglut = wglut_ref[...]
    for k in range(J_PER_STEP):
        yt = _gelu_tanh(yt_ref[k])
        yt = yt * _sigmoid(jnp.dot(wglut, yt.astype(BF16), preferred_element_type=F32))
        y_scr[k * J_ROWS:(k + 1) * J_ROWS, :] = yt.T
    row0 = pl.multiple_of(step * TOKEN_TILE, TOKEN_TILE)
    proj = jnp.dot(y_scr[...].astype(BF16), wout_ref[0:D_SSM, :], preferred_element_type=F32)
    yg = jnp.concatenate([yg_scr[h, pl.ds(row0, TOKEN_TILE), :] for h in range(GMLP_HEADS)], axis=1)
    proj += jnp.dot(yg.astype(BF16), wout_ref[D_SSM:, :], preferred_element_type=F32)
    x1 = x_ref[...] + _per_ps(lambda a, g: a * g, proj, mod_ref[2])
    h2 = _per_ps(lambda a, sc, sh: a * (1.0 + sc) + sh, _rmsnorm(x1, nw_ref[...]), mod_ref[4], mod_ref[3])
    hx_ref[:, 0:D_MODEL] = h2
    hx_ref[:, D_MODEL:2 * D_MODEL] = x1
    logits = lax.dot_general(rwt_ref[...], h2, TRANS_B, precision=lax.Precision.HIGHEST,
                             preferred_element_type=F32)
    scores = _sigmoid(logits)
    sel = scores + rb_ref[...]
    sc = [scores[e:e + 1, :] for e in range(N_EXPERTS)]
    sl = [sel[e:e + 1, :] for e in range(N_EXPERTS)]
    gscore = []
    for g in range(N_EXPERT_GROUPS):
        v0, v1, v2, v3 = sl[4 * g:4 * g + 4]
        hi01, lo01 = jnp.maximum(v0, v1), jnp.minimum(v0, v1)
        hi23, lo23 = jnp.maximum(v2, v3), jnp.minimum(v2, v3)
        top1 = jnp.maximum(hi01, hi23)
        top2 = jnp.maximum(jnp.minimum(hi01, hi23), jnp.maximum(lo01, lo23))
        gscore.append(top1 + top2)
    best = gscore[0]
    gidx = jnp.zeros_like(best, dtype=jnp.int32)
    for g in range(1, N_EXPERT_GROUPS):
        upd = gscore[g] > best
        gidx = jnp.where(upd, g, gidx)
        best = jnp.where(upd, gscore[g], best)

    def in_group(vals, k):
        out = vals[k]
        for g in range(1, N_EXPERT_GROUPS):
            out = jnp.where(gidx == g, vals[4 * g + k], out)
        return out

    v = [in_group(sl, k) for k in range(EXPERTS_PER_GROUP)]
    s = [in_group(sc, k) for k in range(EXPERTS_PER_GROUP)]
    w = []
    bits = jnp.zeros_like(gidx)
    for k in range(EXPERTS_PER_GROUP):
        rank = jnp.zeros_like(gidx)
        for j in range(EXPERTS_PER_GROUP):
            if j == k:
                continue
            ahead = (v[j] >= v[k]) if j < k else (v[j] > v[k])
            rank = rank + ahead.astype(jnp.int32)
        w.append(jnp.where(rank < 2, s[k], 0.0))
        bits = bits + jnp.where(rank < 2, 1 << k, 0)
    denom = (w[0] + w[1]) + (w[2] + w[3])
    gate = [wk / denom for wk in w]
    pair = jnp.full_like(gidx, len(PAIR_SLOT_A) - 1)
    for p in range(len(PAIR_SLOT_A) - 1):
        pair = jnp.where(bits == (1 << PAIR_SLOT_A[p]) + (1 << PAIR_SLOT_B[p]), p, pair)

    def slot_gate(table):
        out = gate[table[0]]
        for p in range(1, len(table)):
            out = jnp.where(pair == p, gate[table[p]], out)
        return out

    n_tok = route_ref.shape[1]
    route_ref[0:1, :] = (gidx * len(PAIR_SLOT_A) + pair).astype(F32)
    ps = lax.broadcasted_iota(jnp.int32, (1, n_tok), 1) % N_PS
    cond = jnp.where(tile < SAMPLE_TILE, 0, 1 + ps // SEGS_PER_SAMPLE_SEQ).astype(F32)
    lanes = jnp.concatenate([slot_gate(PAIR_SLOT_A), slot_gate(PAIR_SLOT_B), cond,
                             jnp.zeros((ROUTE_LANES - 3, n_tok), F32)], axis=0)
    hx_ref[:, 2 * D_MODEL:] = lanes.T


def _mix_out(x, yt, ug, vn, mod8, l, w_glu_t, w_s, b_s, w_out, norm2_w, rwt, rb):
    tm = TOKEN_TILE
    steps = TILE_TOKENS // tm
    tok = lambda n: pl.BlockSpec((tm, n), lambda t, s: (t * steps + s, 0))
    whole_tile = pl.BlockSpec((GMLP_HEADS, TILE_TOKENS, GMLP_HEAD_DIM), lambda t, s: (0, t, 0))
    lay = lambda *shape: pl.BlockSpec((None,) + shape, lambda t, s: (l,) + (0,) * len(shape))
    return pl.pallas_call(
        _post_kernel,
        grid=(N_TILES, steps),
        in_specs=[
            tok(D_MODEL),
            pl.BlockSpec((None, J_PER_STEP, D_SSM, J_ROWS), lambda t, s: (t, s, 0, 0)),
            whole_tile, whole_tile,
            pl.BlockSpec((None, None, N_MOD, N_PS, D_MODEL), lambda t, s: (l, t, 0, 0, 0)),
            lay(D_SSM, D_SSM), lay(GMLP_HEADS, GMLP_CHUNK, GMLP_CHUNK),
            lay(GMLP_HEADS, GMLP_CHUNK, 1), lay(D_MODEL, D_MODEL), lay(1, D_MODEL),
            pl.BlockSpec((N_EXPERTS, D_MODEL), lambda t, s: (0, 0)),
            pl.BlockSpec((N_EXPERTS, 1), lambda t, s: (0, 0)),
        ],
        out_specs=[tok(HX_WIDTH), pl.BlockSpec((1, tm), lambda t, s: (0, t * steps + s))],
        out_shape=[
            jax.ShapeDtypeStruct((T_ALL, HX_WIDTH), F32),
            jax.ShapeDtypeStruct((1, T_ALL), F32),
        ],
        scratch_shapes=[pltpu.VMEM((GMLP_HEADS, TILE_TOKENS, GMLP_HEAD_DIM), F32), pltpu.VMEM((tm, D_SSM), F32)],
        compiler_params=_params("arbitrary", "arbitrary"),
        name="mixers_out_router",
    )(x, yt, ug, vn, mod8, w_glu_t, w_s, b_s, w_out, norm2_w.reshape(DEPTH, 1, D_MODEL), rwt, rb)


def _moe_kernel(ea_ref, eb_ref, new_a_ref, new_b_ref, nv_ref, src_ref, drow_ref,
                hx_hbm, wga_ref, wua_ref, wda_ref, wgb_ref, wub_ref, wdb_ref, g2_ref, fw_ref,
                o_hbm, wga_s, wua_s, wda_s, wgb_s, wub_s, wdb_s, hx_buf, o_buf, z_buf, g_sem, s_sem, z_sem,
                *, final):
    n = pl.program_id(0)
    n_valid = nv_ref[0]
    slot = n % 2

    def start_gather(tile, sl):
        base = tile * MOE_TM
        for r in range(MOE_TM):
            pltpu.make_async_copy(hx_hbm.at[pl.ds(src_ref[base + r], 1)], hx_buf.at[sl, pl.ds(r, 1)],
                                  g_sem.at[sl]).start()

    def wait_gather(sl):
        pltpu.make_async_copy(hx_hbm.at[pl.ds(0, MOE_TM)], hx_buf.at[sl], g_sem.at[sl]).wait()

    def start_scatter(tile, sl):
        base = tile * MOE_TM
        for r in range(MOE_TM):
            pltpu.make_async_copy(o_buf.at[sl, pl.ds(r, 1)], o_hbm.at[pl.ds(drow_ref[base + r], 1)],
                                  s_sem.at[sl]).start()

    def wait_scatter(sl):
        pltpu.make_async_copy(o_buf.at[sl], o_hbm.at[pl.ds(0, MOE_TM)], s_sem.at[sl]).wait()

    def zero_fill(tile):
        return pltpu.make_async_copy(z_buf, o_hbm.at[pl.ds(tile * MOE_TM, MOE_TM)], z_sem)

    @pl.when(n == 0)
    def _():
        z_buf[...] = jnp.zeros_like(z_buf)
        start_gather(0, 0)

    @pl.when(n >= n_valid)
    def _():
        @pl.when(n > n_valid)
        def _():
            zero_fill(n - 1).wait()

        zero_fill(n).start()

        @pl.when(n == MOE_TILES - 1)
        def _():
            zero_fill(n).wait()

    @pl.when(new_a_ref[n] == 1)
    def _():
        wga_s[...] = wga_ref[...].astype(BF16)
        wua_s[...] = wua_ref[...].astype(BF16)
        wda_s[...] = wda_ref[...].astype(BF16)

    @pl.when(new_b_ref[n] == 1)
    def _():
        wgb_s[...] = wgb_ref[...].astype(BF16)
        wub_s[...] = wub_ref[...].astype(BF16)
        wdb_s[...] = wdb_ref[...].astype(BF16)

    @pl.when(n + 1 < n_valid)
    def _():
        start_gather(n + 1, 1 - slot)

    @pl.when(n < n_valid)
    def _():
        wait_gather(slot)

        @pl.when(n >= 2)
        def _():
            wait_scatter(slot)

        h = hx_buf[slot, :, 0:D_MODEL].astype(BF16)
        lanes = hx_buf[slot, :, 2 * D_MODEL:]

        def expert(wg, wu, wd, gate):
            hg = jnp.dot(h, wg[...], preferred_element_type=F32)
            hu = jnp.dot(h, wu[...], preferred_element_type=F32)
            act = hg * _sigmoid(hg) * hu * gate
            return jnp.dot(act.astype(BF16), wd[...], preferred_element_type=F32)

        y = expert(wga_s, wua_s, wda_s, lanes[:, 0:1]) + expert(wgb_s, wub_s, wdb_s, lanes[:, 1:2])
        cond_row = lanes[:, 2:3]
        gate2 = jnp.where(cond_row == 0.0, g2_ref[0:1, :], jnp.where(cond_row == 1.0, g2_ref[1:2, :], g2_ref[2:3, :]))
        x2 = hx_buf[slot, :, D_MODEL:2 * D_MODEL] + gate2 * y
        o_buf[slot] = _rmsnorm(x2, fw_ref[...]) if final else x2
        start_scatter(n, slot)

    @pl.when(n == MOE_TILES - 1)
    def _():
        wait_scatter(0)
        wait_scatter(1)


def _experts(tables, hx, mods, l, w_gate, w_up, w_down, final_w, final):
    w_a = lambda r, c: pl.BlockSpec((None, None, r, c), lambda n, ea, eb, *_: (l, ea[n], 0, 0))
    w_b = lambda r, c: pl.BlockSpec((None, None, r, c), lambda n, ea, eb, *_: (l, eb[n], 0, 0))
    up = pltpu.VMEM((D_MODEL, D_EXPERT), BF16)
    down = pltpu.VMEM((D_EXPERT, D_MODEL), BF16)
    return pl.pallas_call(
        functools.partial(_moe_kernel, final=final),
        grid_spec=pltpu.PrefetchScalarGridSpec(
            num_scalar_prefetch=len(tables),
            grid=(MOE_TILES,),
            in_specs=[
                pl.BlockSpec(memory_space=pl.ANY),
                w_a(D_MODEL, D_EXPERT), w_a(D_MODEL, D_EXPERT), w_a(D_EXPERT, D_MODEL),
                w_b(D_MODEL, D_EXPERT), w_b(D_MODEL, D_EXPERT), w_b(D_EXPERT, D_MODEL),
                pl.BlockSpec((None, None, MOD_ROWS, D_MODEL), lambda n, *_: (l, N_MOD - 1, 0, 0)),
                pl.BlockSpec((1, D_MODEL), lambda n, *_: (0, 0)),
            ],
            out_specs=pl.BlockSpec(memory_space=pl.ANY),
            scratch_shapes=[up, up, down, up, up, down,
                            pltpu.VMEM((2, MOE_TM, HX_WIDTH), F32), pltpu.VMEM((2, MOE_TM, D_MODEL), F32),
                            pltpu.VMEM((MOE_TM, D_MODEL), F32),
                            pltpu.SemaphoreType.DMA((2,)), pltpu.SemaphoreType.DMA((2,)), pltpu.SemaphoreType.DMA(())],
        ),
        out_shape=jax.ShapeDtypeStruct((MOE_ROWS, D_MODEL), F32),
        compiler_params=_params("arbitrary"),
        name="experts",
    )(*tables, hx, w_gate, w_up, w_down, w_gate, w_up, w_down, mods, final_w.reshape(1, D_MODEL))


def _routing_tables(cls, target_row):
    n_cls = N_EXPERT_GROUPS * len(PAIR_SLOT_A)
    onehot = (cls[:, None] == jnp.arange(n_cls, dtype=jnp.int32)[None, :]).astype(jnp.int32)
    counts = jnp.sum(onehot, axis=0)
    rank = jnp.take_along_axis(jnp.cumsum(onehot, axis=0) - onehot, cls[:, None], axis=1)[:, 0]
    tiles = (counts + MOE_TM - 1) // MOE_TM
    tile_end = jnp.cumsum(tiles)
    dst = (tile_end - tiles)[cls] * MOE_TM + rank
    token_plus_1 = jnp.zeros((MOE_ROWS,), jnp.int32).at[dst].set(jnp.arange(1, T_ALL + 1, dtype=jnp.int32))
    is_pad = token_plus_1 == 0
    src = jnp.maximum(token_plus_1 - 1, 0)
    spare = T_ALL + jnp.cumsum(is_pad.astype(jnp.int32)) - 1
    drow = jnp.where(is_pad, spare, target_row[src]).astype(jnp.int32)
    n_valid = tile_end[-1]
    tile_id = jnp.minimum(jnp.arange(MOE_TILES, dtype=jnp.int32), n_valid - 1)
    tile_cls = jnp.sum((tile_end[None, :] <= tile_id[:, None]).astype(jnp.int32), axis=1)
    group, pair = tile_cls // len(PAIR_SLOT_A), tile_cls % len(PAIR_SLOT_A)
    e_a = group * EXPERTS_PER_GROUP + jnp.array(PAIR_SLOT_A, jnp.int32)[pair]
    e_b = group * EXPERTS_PER_GROUP + jnp.array(PAIR_SLOT_B, jnp.int32)[pair]
    first = jnp.ones((1,), jnp.int32)
    new_a = jnp.concatenate([first, (e_a[1:] != e_a[:-1]).astype(jnp.int32)])
    new_b = jnp.concatenate([first, (e_b[1:] != e_b[:-1]).astype(jnp.int32)])
    return e_a, e_b, new_a, new_b, n_valid.reshape(1).astype(jnp.int32), src, drow


def _dirs_on_lanes(p):
    p = jnp.moveaxis(p, 1, -2)
    return p.reshape(p.shape[:-2] + (2 * SSM_STATE,))


def _to_internal_order(x):
    x = x.reshape(N_TILES, N_PS, SEG_CHUNKS, SCAN_CHUNK, D_MODEL)
    return x.transpose(0, 3, 2, 1, 4).reshape(T_ALL, D_MODEL)


def _gmlp_position_order(w):
    n_lo = GMLP_CHUNK // SCAN_CHUNK
    lead = w.shape[:2]
    w = w.reshape(lead + (n_lo, SCAN_CHUNK) + w.shape[3:])
    w = jnp.swapaxes(w, 2, 3)
    return w.reshape(lead + (GMLP_CHUNK,) + w.shape[4:])


def kernel(x_prompt, x_sample, c, state_ssm_re, state_ssm_im, c_ctx, norm1_w, norm2_w, w_mod, b_mod, w_in,
           ssm_a_re, ssm_a_im, ssm_log_dt, ssm_b_re, ssm_b_im, ssm_c_re, ssm_c_im, ssm_d, w_glu,
           gmlp_ln_w, gmlp_ln_b, gmlp_w_s, gmlp_b_s, w_out, router_w, router_b, w_gate, w_up, w_down,
           final_norm_w):
    x = jnp.concatenate([x_prompt.reshape(T_PROMPT, D_MODEL), x_sample.reshape(T_SAMPLE, D_MODEL)], axis=0)
    x = _to_internal_order(x)

    cvec = jnp.concatenate([c_ctx[None, :], c, jnp.zeros((MOD_ROWS - 1 - N_SAMPLE_SEQ, D_MODEL), F32)], axis=0)
    mods = _modulation(cvec, w_mod, b_mod)
    ps_row = [[0] * N_PS] * SAMPLE_TILE + [[1 + p // SEGS_PER_SAMPLE_SEQ for p in range(N_PS)]]
    mod8 = mods[:, :, jnp.array(ps_row, jnp.int32), :].transpose(0, 2, 1, 3, 4)

    log_dt = jnp.broadcast_to(ssm_log_dt[..., None], ssm_a_re.shape)
    d_lanes = jnp.tile(ssm_d.reshape(DEPTH, N_GROUPS, SSM_GROUP), (1, 1, 128 // SSM_GROUP))
    vecs = jnp.stack([_dirs_on_lanes(ssm_a_re), _dirs_on_lanes(ssm_a_im), _dirs_on_lanes(log_dt), d_lanes], axis=2)
    vecs = jnp.concatenate([vecs, jnp.zeros((DEPTH, N_GROUPS, 4, 2 * SSM_STATE), F32)], axis=2)
    mats = jnp.stack([_dirs_on_lanes(jnp.swapaxes(ssm_b_re, -1, -2)), _dirs_on_lanes(jnp.swapaxes(ssm_b_im, -1, -2)),
                      _dirs_on_lanes(ssm_c_re), _dirs_on_lanes(ssm_c_im)], axis=2)
    wft, cct, ttt, a16 = _ssm_operators(vecs, mats)

    w_in_g = w_in[:, :, D_SSM:].astype(BF16)
    w_in_st = jnp.swapaxes(w_in[:, :, :D_SSM], 1, 2).astype(BF16)
    w_glu_t = jnp.swapaxes(w_glu, 1, 2).astype(BF16)
    w_s = jnp.swapaxes(_gmlp_position_order(jnp.swapaxes(_gmlp_position_order(gmlp_w_s), 2, 3)), 2, 3).astype(BF16)
    w_out = w_out.astype(BF16)
    b_s = _gmlp_position_order(gmlp_b_s)[..., None]
    rwt = router_w.T
    rb = router_b.reshape(N_EXPERTS, 1)
    token_pos = jnp.arange(T_ALL, dtype=jnp.int32)
    seq_pos = np.arange(T_ALL).reshape(N_TILES, N_PS, SEG_CHUNKS, SCAN_CHUNK).transpose(0, 3, 2, 1)
    sequence_pos = jnp.asarray(seq_pos.reshape(T_ALL), jnp.int32)

    new_re, new_im = [], []
    for l in range(DEPTH):
        ug, vn, xt = _input_proj(x, mod8, l, norm1_w, w_in_g, w_in_st, gmlp_ln_w, gmlp_ln_b)
        s0 = jnp.concatenate([state_ssm_re[:, l].transpose(2, 0, 1, 3).reshape(N_GROUPS, N_SAMPLE_SEQ, 128),
                              state_ssm_im[:, l].transpose(2, 0, 1, 3).reshape(N_GROUPS, N_SAMPLE_SEQ, 128)], axis=-1)
        s0 = jnp.repeat(s0, SEGS_PER_SAMPLE_SEQ, axis=1)
        yt, fin = _ssm_scan(xt, wft, cct, ttt, a16, s0, l)
        fin = fin[:SAMPLE_TILE].reshape(SAMPLE_TILE, N_GROUPS, N_PS, 2, 2, SSM_STATE)
        fin = fin.transpose(3, 0, 2, 4, 1, 5).reshape(2, N_PROMPT_SEQ, 2, N_GROUPS, SSM_STATE)
        new_re.append(fin[0])
        new_im.append(fin[1])
        hx, route = _mix_out(x, yt, ug, vn, mod8, l, w_glu_t, w_s, b_s, w_out, norm2_w, rwt, rb)
        final = l == DEPTH - 1
        tables = _routing_tables(route[0].astype(jnp.int32), sequence_pos if final else token_pos)
        x = _experts(tables, hx, mods, l, w_gate, w_up, w_down, final_norm_w, final)

    y_prompt = x[:T_PROMPT].reshape(N_PROMPT_SEQ, PROMPT_LEN, D_MODEL)
    y_sample = x[T_PROMPT:T_ALL].reshape(N_SAMPLE_SEQ, SAMPLE_LEN, D_MODEL)
    return (y_prompt, y_sample, jnp.stack(new_re, axis=1), jnp.stack(new_im, axis=1))
```

```python
import functools
import math

import jax
import jax.numpy as jnp
from jax import lax
from jax.experimental import pallas as pl
from jax.experimental.pallas import tpu as pltpu

F32 = jnp.float32
BF16 = jnp.bfloat16

D_MODEL = 1024
N_PROMPT_SEQ = 16
PROMPT_LEN = 256
N_SAMPLE_SEQ = 2
SAMPLE_LEN = 1024
T_PROMPT = N_PROMPT_SEQ * PROMPT_LEN
T_SAMPLE = N_SAMPLE_SEQ * SAMPLE_LEN
T_ALL = T_PROMPT + T_SAMPLE
DEPTH = 2
D_SSM = 512
SSM_GROUP = 16
N_GROUPS = 32
SSM_STATE = 64
D_GMLP = 512
GMLP_HEADS = 4
GMLP_HEAD_DIM = 128
GMLP_CHUNK = 128
N_EXPERTS = 16
N_EXPERT_GROUPS = 4
EXPERTS_PER_GROUP = 4
D_EXPERT = 512
N_MOD = 6
EPS = 1e-6

SCAN_CHUNK = 16
CHUNK_WIDTH = SCAN_CHUNK * SSM_GROUP
SEG_LEN = 256
SEG_CHUNKS = SEG_LEN // SCAN_CHUNK
N_PS = 8
TILE_TOKENS = N_PS * SEG_LEN
N_TILES = T_ALL // TILE_TOKENS
SAMPLE_TILE = T_PROMPT // TILE_TOKENS
SEGS_PER_SAMPLE_SEQ = SAMPLE_LEN // SEG_LEN
J_ROWS = SEG_CHUNKS * N_PS
GROUP_BLOCK = 8
PREP_GROUPS = 4
MOD_ROWS = 8

J_PER_STEP = 4
TOKEN_TILE = J_PER_STEP * J_ROWS
PAIR_SLOT_A = (0, 0, 0, 1, 1, 3)
PAIR_SLOT_B = (1, 2, 3, 3, 2, 2)
MOE_TM = 128
MOE_TILES = T_ALL // MOE_TM + N_EXPERT_GROUPS * len(PAIR_SLOT_A)
MOE_ROWS = MOE_TILES * MOE_TM
ROUTE_LANES = 128
HX_WIDTH = 2 * D_MODEL + ROUTE_LANES
VMEM_LIMIT = 56 * 1024 * 1024
TRANS_B = (((1,), (1,)), ((), ()))


def _sigmoid(x):
    return 1.0 / (1.0 + jnp.exp(-x))


def _gelu_tanh(x):
    c = math.sqrt(2.0 / math.pi)
    return x * (0.5 * (1.0 + jnp.tanh(c * (x + 0.044715 * (x * x * x)))))


def _rmsnorm(x, w):
    return x * lax.rsqrt(jnp.mean(x * x, axis=-1, keepdims=True) + EPS) * w


def _per_ps(fn, a, *mods):
    rows, d = a.shape
    out = fn(a.reshape(rows // N_PS, N_PS, d), *[m[None] for m in mods])
    return out.reshape(rows, d)


def _params(*sem):
    return pltpu.CompilerParams(dimension_semantics=sem, vmem_limit_bytes=VMEM_LIMIT)


def _mod_kernel(c_ref, w_ref, b_ref, o_ref):
    c = c_ref[...]
    s = c * _sigmoid(c)
    o_ref[...] = jnp.dot(s.astype(BF16), w_ref[...].astype(BF16), preferred_element_type=F32) + b_ref[...]


def _modulation(cvec, w_mod, b_mod):
    return pl.pallas_call(
        _mod_kernel,
        grid=(DEPTH, N_MOD),
        in_specs=[
            pl.BlockSpec((MOD_ROWS, D_MODEL), lambda l, n: (0, 0)),
            pl.BlockSpec((None, D_MODEL, D_MODEL), lambda l, n: (l, 0, n)),
            pl.BlockSpec((None, None, 1, D_MODEL), lambda l, n: (l, n, 0, 0)),
        ],
        out_specs=pl.BlockSpec((None, None, MOD_ROWS, D_MODEL), lambda l, n: (l, n, 0, 0)),
        out_shape=jax.ShapeDtypeStruct((DEPTH, N_MOD, MOD_ROWS, D_MODEL), F32),
        compiler_params=_params("arbitrary", "arbitrary"),
        name="adaln_mod",
    )(cvec, w_mod, b_mod.reshape(DEPTH, N_MOD, 1, D_MODEL))


def _in_kernel(x_ref, mod_ref, nw_ref, wg_ref, wst_ref, lnw_ref, lnb_ref, ug_ref, vn_ref, xt_ref):
    y = _rmsnorm(x_ref[...], nw_ref[...])
    h = _per_ps(lambda a, sc, sh: a * (1.0 + sc) + sh, y, mod_ref[1], mod_ref[0]).astype(BF16)
    zg = _gelu_tanh(jnp.dot(h, wg_ref[...], preferred_element_type=F32))
    v = zg[:, D_GMLP:]
    mu = jnp.mean(v, axis=-1, keepdims=True)
    vc = v - mu
    var = jnp.mean(vc * vc, axis=-1, keepdims=True)
    vn = vc * lax.rsqrt(var + EPS) * lnw_ref[...] + lnb_ref[...]
    for hd in range(GMLP_HEADS):
        cols = slice(hd * GMLP_HEAD_DIM, (hd + 1) * GMLP_HEAD_DIM)
        ug_ref[hd] = zg[:, cols]
        vn_ref[hd] = vn[:, cols]
    wst = wst_ref[...]
    for k in range(J_PER_STEP):
        hk = h[k * J_ROWS:(k + 1) * J_ROWS]
        xt_ref[k] = lax.dot_general(wst, hk, TRANS_B, preferred_element_type=F32).astype(BF16)


def _input_proj(x, mod8, l, norm1_w, w_in_g, w_in_st, ln_w, ln_b):
    tm = TOKEN_TILE
    steps = TILE_TOKENS // tm
    tok = lambda n: pl.BlockSpec((tm, n), lambda t, s: (t * steps + s, 0))
    heads = pl.BlockSpec((GMLP_HEADS, tm, GMLP_HEAD_DIM), lambda t, s: (0, t * steps + s, 0))
    lay = lambda *shape: pl.BlockSpec((None,) + shape, lambda t, s: (l,) + (0,) * len(shape))
    return pl.pallas_call(
        _in_kernel,
        grid=(N_TILES, steps),
        in_specs=[
            tok(D_MODEL),
            pl.BlockSpec((None, None, N_MOD, N_PS, D_MODEL), lambda t, s: (l, t, 0, 0, 0)),
            lay(1, D_MODEL), lay(D_MODEL, 2 * D_GMLP), lay(D_SSM, D_MODEL), lay(1, D_GMLP), lay(1, D_GMLP),
        ],
        out_specs=[heads, heads,
                   pl.BlockSpec((None, J_PER_STEP, D_SSM, J_ROWS), lambda t, s: (t, s, 0, 0))],
        out_shape=[jax.ShapeDtypeStruct((GMLP_HEADS, T_ALL, GMLP_HEAD_DIM), F32)] * 2 + [
            jax.ShapeDtypeStruct((N_TILES, SCAN_CHUNK, D_SSM, J_ROWS), BF16)],
        compiler_params=_params("arbitrary", "arbitrary"),
        name="norm1_in_proj",
    )(x, mod8, norm1_w.reshape(DEPTH, 1, D_MODEL), w_in_g, w_in_st,
      ln_w.reshape(DEPTH, 1, D_GMLP), ln_b.reshape(DEPTH, 1, D_GMLP))


def _shift_lanes_right(a, b, s, lane):
    if s == 0:
        return a, b
    if s == 128:
        return jnp.zeros_like(a), a
    if s < 128:
        ra = pltpu.roll(a, s, 1)
        rb = pltpu.roll(b, s, 1)
        return jnp.where(lane >= s, ra, 0.0), jnp.where(lane >= s, rb, ra)
    t = s - 128
    return jnp.zeros_like(a), jnp.where(lane >= t, pltpu.roll(a, t, 1), 0.0)


def _shift_lanes_left(a, b, s, lane):
    if s == 0:
        return a, b
    if s == 128:
        return b, jnp.zeros_like(b)
    if s < 128:
        ra = pltpu.roll(a, 128 - s, 1)
        rb = pltpu.roll(b, 128 - s, 1)
        return jnp.where(lane < 128 - s, ra, rb), jnp.where(lane < 128 - s, rb, 0.0)
    t = s - 128
    return jnp.where(lane < 128 - t, pltpu.roll(b, 128 - t, 1), 0.0), jnp.zeros_like(b)


def _prep_kernel(*refs):
    for g in range(PREP_GROUPS):
        _prep_group(*[r.at[g] for r in refs])


def _prep_group(vec_ref, mat_ref, wft_ref, cct_ref, ttt_ref, a16_ref, wf_scr, cm_scr, tt_scr):
    a_re = vec_ref[0:1, :]
    a_im = vec_ref[1:2, :]
    dt = jnp.exp(vec_ref[2:3, :])
    d_skip = vec_ref[3:4, :]
    mag = jnp.exp(a_re * dt)
    ang = a_im * dt
    ab_r = mag * jnp.cos(ang)
    ab_i = mag * jnp.sin(ang)
    den = a_re * a_re + a_im * a_im
    nr = ab_r - 1.0
    q_r = (nr * a_re + ab_i * a_im) / den
    q_i = (ab_i * a_re - nr * a_im) / den
    bt_r = mat_ref[0]
    bt_i = mat_ref[1]
    c_r = mat_ref[2]
    c_i = mat_ref[3]
    bb_r = q_r * bt_r - q_i * bt_i
    bb_i = q_r * bt_i + q_i * bt_r
    p_r = [jnp.ones_like(ab_r)]
    p_i = [jnp.zeros_like(ab_r)]
    for _ in range(SCAN_CHUNK):
        pr, pi = p_r[-1], p_i[-1]
        p_r.append(pr * ab_r - pi * ab_i)
        p_i.append(pr * ab_i + pi * ab_r)
    a16_ref[0] = jnp.broadcast_to(p_r[SCAN_CHUNK], (N_PS, 128))
    a16_ref[1] = jnp.broadcast_to(p_i[SCAN_CHUNK], (N_PS, 128))

    lane = lax.broadcasted_iota(jnp.int32, (1, 128), 1)
    is_fwd = lane < SSM_STATE

    def pick(mf, mb):
        return jnp.where(is_fwd, p_r[mf], p_r[mb]), jnp.where(is_fwd, p_i[mf], p_i[mb])

    for j in range(SCAN_CHUNK):
        rows = slice(j * SSM_GROUP, (j + 1) * SSM_GROUP)
        wr, wi = pick(SCAN_CHUNK - 1 - j, j)
        wf_scr[rows, 0:128] = bb_r * wr - bb_i * wi
        wf_scr[rows, 128:256] = bb_r * wi + bb_i * wr
        wr, wi = pick(j + 1, SCAN_CHUNK - j)
        cct_ref[rows, 0:128] = (c_r * wr - c_i * wi).astype(BF16)
        cct_ref[rows, 128:256] = (-(c_r * wi + c_i * wr)).astype(BF16)
        wr, wi = pick(j, SCAN_CHUNK - 1 - j)
        cm_scr[rows, 0:128] = c_r * wr - c_i * wi
        cm_scr[rows, 128:256] = c_r * wi + c_i * wr
    wft_ref[...] = wf_scr[...].T.astype(BF16)

    zero = jnp.zeros_like(bb_r)
    cm = cm_scr[...]

    def lag_rows(keep):
        lhs = jnp.concatenate([jnp.where(keep, bb_r, zero), jnp.where(keep, -bb_i, zero)], axis=1)
        return lax.dot_general(lhs, cm, TRANS_B, precision=lax.Precision.HIGHEST, preferred_element_type=F32)

    mf = lag_rows(is_fwd)
    mb = lag_rows(jnp.logical_not(is_fwd))
    mf_a, mf_b = mf[:, 0:128], mf[:, 128:256]
    mb_a, mb_b = mb[:, 0:128], mb[:, 128:256]
    row_h = lax.broadcasted_iota(jnp.int32, (SSM_GROUP, 128), 0)
    lane_h = lax.broadcasted_iota(jnp.int32, (SSM_GROUP, 128), 1)
    for jp in range(SCAN_CHUNK):
        rows = slice(jp * SSM_GROUP, (jp + 1) * SSM_GROUP)
        fa, fb = _shift_lanes_right(mf_a, mf_b, SSM_GROUP * jp, lane)
        ba, bb = _shift_lanes_left(mb_a, mb_b, SSM_GROUP * (SCAN_CHUNK - 1 - jp), lane)
        diag = SSM_GROUP * jp + row_h
        tt_scr[rows, 0:128] = fa + ba + jnp.where(lane_h == diag, d_skip, 0.0)
        tt_scr[rows, 128:256] = fb + bb + jnp.where(lane_h + 128 == diag, d_skip, 0.0)
    ttt_ref[...] = tt_scr[...].T.astype(BF16)


def _ssm_operators(vecs, mats):
    op = jax.ShapeDtypeStruct((DEPTH, N_GROUPS, CHUNK_WIDTH, CHUNK_WIDTH), BF16)
    pg = PREP_GROUPS
    op_spec = pl.BlockSpec((None, pg, CHUNK_WIDTH, CHUNK_WIDTH), lambda l, g: (l, g, 0, 0))
    sq = pltpu.VMEM((pg, CHUNK_WIDTH, CHUNK_WIDTH), F32)
    return pl.pallas_call(
        _prep_kernel,
        grid=(DEPTH, N_GROUPS // pg),
        in_specs=[
            pl.BlockSpec((None, pg, 8, 128), lambda l, g: (l, g, 0, 0)),
            pl.BlockSpec((None, pg, 4, SSM_GROUP, 128), lambda l, g: (l, g, 0, 0, 0)),
        ],
        out_specs=[op_spec, op_spec, op_spec,
                   pl.BlockSpec((None, pg, 2, N_PS, 128), lambda l, g: (l, g, 0, 0, 0))],
        out_shape=[op, op, op, jax.ShapeDtypeStruct((DEPTH, N_GROUPS, 2, N_PS, 128), F32)],
        scratch_shapes=[sq, sq, sq],
        compiler_params=_params("arbitrary", "arbitrary"),
        name="s5_operators",
    )(vecs, mats)


def _ssm_kernel(xt_ref, wft_ref, cct_ref, ttt_ref, a16_ref, s0_ref, yt_ref, fin_ref, s_scr, f_scr, ft_scr):
    tile = pl.program_id(0)
    lane = lax.broadcasted_iota(jnp.int32, (GROUP_BLOCK, N_PS, 128), 2)
    is_fwd = lane < SSM_STATE
    half = SSM_STATE

    def group_x(gl):
        return xt_ref[:, gl * SSM_GROUP:(gl + 1) * SSM_GROUP, :].reshape(CHUNK_WIDTH, J_ROWS)

    for gl in range(GROUP_BLOCK):
        ft_scr[gl] = jnp.dot(wft_ref[gl], group_x(gl), preferred_element_type=F32)
        f_scr[gl] = ft_scr[gl].T

    a_r = a16_ref[:, 0]
    a_i = a16_ref[:, 1]

    def scan(s_r, s_i):
        for i in range(SEG_CHUNKS):
            rf = slice(i * N_PS, (i + 1) * N_PS)
            rb = slice((SEG_CHUNKS - 1 - i) * N_PS, (SEG_CHUNKS - i) * N_PS)
            s_scr[:, rf, 0:half] = s_r[:, :, 0:half]
            s_scr[:, rf, 128:128 + half] = s_i[:, :, 0:half]
            s_scr[:, rb, half:128] = s_r[:, :, half:128]
            s_scr[:, rb, 128 + half:256] = s_i[:, :, half:128]
            f_r = jnp.where(is_fwd, f_scr[:, rf, 0:128], f_scr[:, rb, 0:128])
            f_i = jnp.where(is_fwd, f_scr[:, rf, 128:256], f_scr[:, rb, 128:256])
            s_r, s_i = a_r * s_r - a_i * s_i + f_r, a_r * s_i + a_i * s_r + f_i
        return s_r, s_i

    zeros = jnp.zeros((GROUP_BLOCK, N_PS, 128), F32)
    z_r, z_i = scan(zeros, zeros)
    fin_ref[:, :, 0:128] = z_r
    fin_ref[:, :, 128:256] = z_i

    @pl.when(tile == SAMPLE_TILE)
    def _():
        b_r, b_i = a_r, a_i
        for _ in range(4):
            b_r, b_i = b_r * b_r - b_i * b_i, 2.0 * (b_r * b_i)
        seg = lax.broadcasted_iota(jnp.int32, (GROUP_BLOCK, N_PS, 128), 1) % SEGS_PER_SAMPLE_SEQ
        i_r = s0_ref[:, :, 0:128]
        i_i = s0_ref[:, :, 128:256]
        for step in range(1, SEGS_PER_SAMPLE_SEQ):
            pr = jnp.where(is_fwd, pltpu.roll(i_r, 1, 1), pltpu.roll(i_r, N_PS - 1, 1))
            pi = jnp.where(is_fwd, pltpu.roll(i_i, 1, 1), pltpu.roll(i_i, N_PS - 1, 1))
            zr = jnp.where(is_fwd, pltpu.roll(z_r, 1, 1), pltpu.roll(z_r, N_PS - 1, 1))
            zi = jnp.where(is_fwd, pltpu.roll(z_i, 1, 1), pltpu.roll(z_i, N_PS - 1, 1))
            n_r = b_r * pr - b_i * pi + zr
            n_i = b_r * pi + b_i * pr + zi
            first = jnp.where(is_fwd, step, 0)
            last = jnp.where(is_fwd, SEGS_PER_SAMPLE_SEQ - 1, SEGS_PER_SAMPLE_SEQ - 1 - step)
            upd = jnp.logical_and(seg >= first, seg <= last)
            i_r = jnp.where(upd, n_r, i_r)
            i_i = jnp.where(upd, n_i, i_i)
        scan(i_r, i_i)

    for gl in range(GROUP_BLOCK):
        yt = jnp.dot(ttt_ref[gl], group_x(gl), preferred_element_type=F32)
        yt += lax.dot_general(cct_ref[gl], s_scr[gl].astype(BF16), TRANS_B, preferred_element_type=F32)
        yt_ref[:, gl * SSM_GROUP:(gl + 1) * SSM_GROUP, :] = yt.reshape(SCAN_CHUNK, SSM_GROUP, J_ROWS)


def _ssm_scan(xt, wft, cct, ttt, a16, s0, l):
    gb = GROUP_BLOCK
    op_spec = pl.BlockSpec((None, gb, CHUNK_WIDTH, CHUNK_WIDTH), lambda t, g: (l, g, 0, 0))
    io_spec = pl.BlockSpec((None, SCAN_CHUNK, gb * SSM_GROUP, J_ROWS), lambda t, g: (t, 0, g, 0))
    return pl.pallas_call(
        _ssm_kernel,
        grid=(N_TILES, N_GROUPS // gb),
        in_specs=[
            io_spec, op_spec, op_spec, op_spec,
            pl.BlockSpec((None, gb, 2, N_PS, 128), lambda t, g: (l, g, 0, 0, 0)),
            pl.BlockSpec((gb, N_PS, CHUNK_WIDTH), lambda t, g: (g, 0, 0)),
        ],
        out_specs=[io_spec, pl.BlockSpec((None, gb, N_PS, CHUNK_WIDTH), lambda t, g: (t, g, 0, 0))],
        out_shape=[
            jax.ShapeDtypeStruct((N_TILES, SCAN_CHUNK, D_SSM, J_ROWS), F32),
            jax.ShapeDtypeStruct((N_TILES, N_GROUPS, N_PS, CHUNK_WIDTH), F32),
        ],
        scratch_shapes=[pltpu.VMEM((gb, J_ROWS, CHUNK_WIDTH), F32), pltpu.VMEM((gb, J_ROWS, CHUNK_WIDTH), F32),
                        pltpu.VMEM((gb, CHUNK_WIDTH, J_ROWS), F32)],
        compiler_params=_params("arbitrary", "arbitrary"),
        name="s5_chunk_scan",
    )(xt, wft, cct, ttt, a16, s0)


def _post_kernel(x_ref, yt_ref, ug_ref, vn_ref, mod_ref, wglut_ref, ws_ref, bs_ref, wout_ref,
                 nw_ref, rwt_ref, rb_ref, hx_ref, route_ref, yg_scr, y_scr):
    tile = pl.program_id(0)
    step = pl.program_id(1)

    @pl.when(step == 0)
    def _():
        def chunk(n, carry):
            ps = n % N_PS
            c_hi = n // N_PS
            base = c_hi * (GMLP_CHUNK // SCAN_CHUNK) * N_PS + ps
            rows = [pl.ds(j * J_ROWS + base, GMLP_CHUNK // SCAN_CHUNK, stride=N_PS) for j in range(SCAN_CHUNK)]
            for h in range(GMLP_HEADS):
                v = jnp.concatenate([vn_ref[h, r, :] for r in rows], axis=0).astype(BF16)
                u = jnp.concatenate([ug_ref[h, r, :] for r in rows], axis=0)
                s = jnp.dot(ws_ref[h], v, preferred_element_type=F32) + bs_ref[h]
                yg = u * s
                for j, r in enumerate(rows):
                    yg_scr[h, r, :] = yg[j * 8:(j + 1) * 8]
            return carry

        lax.fori_loop(0, TILE_TOKENS // GMLP_CHUNK, chunk, 0)

    wglut = wglut_ref[...]
    for k in range(J_PER_STEP):
        yt = _gelu_tanh(yt_ref[k])
        yt = yt * _sigmoid(jnp.dot(wglut, yt.astype(BF16), preferred_element_type=F32))
        y_scr[k * J_ROWS:(k + 1) * J_ROWS, :] = yt.T
    row0 = pl.multiple_of(step * TOKEN_TILE, TOKEN_TILE)
    proj = jnp.dot(y_scr[...].astype(BF16), wout_ref[0:D_SSM, :], preferred_element_type=F32)
    yg = jnp.concatenate([yg_scr[h, pl.ds(row0, TOKEN_TILE), :] for h in range(GMLP_HEADS)], axis=1)
    proj += jnp.dot(yg.astype(BF16), wout_ref[D_SSM:, :], preferred_element_type=F32)
    x1 = x_ref[...] + _per_ps(lambda a, g: a * g, proj, mod_ref[2])
    h2 = _per_ps(lambda a, sc, sh: a * (1.0 + sc) + sh, _rmsnorm(x1, nw_ref[...]), mod_ref[4], mod_ref[3])
    hx_ref[:, 0:D_MODEL] = h2
    hx_ref[:, D_MODEL:2 * D_MODEL] = x1
    logits = lax.dot_general(rwt_ref[...], h2, TRANS_B, precision=lax.Precision.HIGHEST,
                             preferred_element_type=F32)
    scores = _sigmoid(logits)
    sel = scores + rb_ref[...]
    sc = [scores[e:e + 1, :] for e in range(N_EXPERTS)]
    sl = [sel[e:e + 1, :] for e in range(N_EXPERTS)]
    gscore = []
    for g in range(N_EXPERT_GROUPS):
        v0, v1, v2, v3 = sl[4 * g:4 * g + 4]
        hi01, lo01 = jnp.maximum(v0, v1), jnp.minimum(v0, v1)
        hi23, lo23 = jnp.maximum(v2, v3), jnp.minimum(v2, v3)
        top1 = jnp.maximum(hi01, hi23)
        top2 = jnp.maximum(jnp.minimum(hi01, hi23), jnp.maximum(lo01, lo23))
        gscore.append(top1 + top2)
    best = gscore[0]
    gidx = jnp.zeros_like(best, dtype=jnp.int32)
    for g in range(1, N_EXPERT_GROUPS):
        upd = gscore[g] > best
        gidx = jnp.where(upd, g, gidx)
        best = jnp.where(upd, gscore[g], best)

    def in_group(vals, k):
        out = vals[k]
        for g in range(1, N_EXPERT_GROUPS):
            out = jnp.where(gidx == g, vals[4 * g + k], out)
        return out

    v = [in_group(sl, k) for k in range(EXPERTS_PER_GROUP)]
    s = [in_group(sc, k) for k in range(EXPERTS_PER_GROUP)]
    w = []
    bits = jnp.zeros_like(gidx)
    for k in range(EXPERTS_PER_GROUP):
        rank = jnp.zeros_like(gidx)
        for j in range(EXPERTS_PER_GROUP):
            if j == k:
                continue
            ahead = (v[j] >= v[k]) if j < k else (v[j] > v[k])
            rank = rank + ahead.astype(jnp.int32)
        w.append(jnp.where(rank < 2, s[k], 0.0))
        bits = bits + jnp.where(rank < 2, 1 << k, 0)
    denom = (w[0] + w[1]) + (w[2] + w[3])
    gate = [wk / denom for wk in w]
    pair = jnp.full_like(gidx, len(PAIR_SLOT_A) - 1)
    for p in range(len(PAIR_SLOT_A) - 1):
        pair = jnp.where(bits == (1 << PAIR_SLOT_A[p]) + (1 << PAIR_SLOT_B[p]), p, pair)

    def slot_gate(table):
        out = gate[table[0]]
        for p in range(1, len(table)):
            out = jnp.where(pair == p, gate[table[p]], out)
        return out

    n_tok = route_ref.shape[1]
    route_ref[0:1, :] = (gidx * len(PAIR_SLOT_A) + pair).astype(F32)
    ps = lax.broadcasted_iota(jnp.int32, (1, n_tok), 1) % N_PS
    cond = jnp.where(tile < SAMPLE_TILE, 0, 1 + ps // SEGS_PER_SAMPLE_SEQ).astype(F32)
    lanes = jnp.concatenate([slot_gate(PAIR_SLOT_A), slot_gate(PAIR_SLOT_B), cond,
                             jnp.zeros((ROUTE_LANES - 3, n_tok), F32)], axis=0)
    hx_ref[:, 2 * D_MODEL:] = lanes.T


def _mix_out(x, yt, ug, vn, mod8, l, w_glu_t, w_s, b_s, w_out, norm2_w, rwt, rb):
    tm = TOKEN_TILE
    steps = TILE_TOKENS // tm
    tok = lambda n: pl.BlockSpec((tm, n), lambda t, s: (t * steps + s, 0))
    whole_tile = pl.BlockSpec((GMLP_HEADS, TILE_TOKENS, GMLP_HEAD_DIM), lambda t, s: (0, t, 0))
    lay = lambda *shape: pl.BlockSpec((None,) + shape, lambda t, s: (l,) + (0,) * len(shape))
    return pl.pallas_call(
        _post_kernel,
        grid=(N_TILES, steps),
        in_specs=[
            tok(D_MODEL),
            pl.BlockSpec((None, J_PER_STEP, D_SSM, J_ROWS), lambda t, s: (t, s, 0, 0)),
            whole_tile, whole_tile,
            pl.BlockSpec((None, None, N_MOD, N_PS, D_MODEL), lambda t, s: (l, t, 0, 0, 0)),
            lay(D_SSM, D_SSM), lay(GMLP_HEADS, GMLP_CHUNK, GMLP_CHUNK),
            lay(GMLP_HEADS, GMLP_CHUNK, 1), lay(D_MODEL, D_MODEL), lay(1, D_MODEL),
            pl.BlockSpec((N_EXPERTS, D_MODEL), lambda t, s: (0, 0)),
            pl.BlockSpec((N_EXPERTS, 1), lambda t, s: (0, 0)),
        ],
        out_specs=[tok(HX_WIDTH), pl.BlockSpec((1, tm), lambda t, s: (0, t * steps + s))],
        out_shape=[
            jax.ShapeDtypeStruct((T_ALL, HX_WIDTH), F32),
            jax.ShapeDtypeStruct((1, T_ALL), F32),
        ],
        scratch_shapes=[pltpu.VMEM((GMLP_HEADS, TILE_TOKENS, GMLP_HEAD_DIM), F32), pltpu.VMEM((tm, D_SSM), F32)],
        compiler_params=_params("arbitrary", "arbitrary"),
        name="mixers_out_router",
    )(x, yt, ug, vn, mod8, w_glu_t, w_s, b_s, w_out, norm2_w.reshape(DEPTH, 1, D_MODEL), rwt, rb)


def _moe_kernel(ea_ref, eb_ref, new_a_ref, new_b_ref, nv_ref, src_ref, drow_ref,
                hx_hbm, wga_ref, wua_ref, wda_ref, wgb_ref, wub_ref, wdb_ref, g2_ref, fw_ref,
                o_hbm, wga_s, wua_s, wda_s, wgb_s, wub_s, wdb_s, hx_buf, o_buf, z_buf, g_sem, s_sem, z_sem,
                *, final):
    n = pl.program_id(0)
    n_valid = nv_ref[0]
    slot = n % 2

    def start_gather(tile, sl):
        base = tile * MOE_TM
        for r in range(MOE_TM):
            pltpu.make_async_copy(hx_hbm.at[pl.ds(src_ref[base + r], 1)], hx_buf.at[sl, pl.ds(r, 1)],
                                  g_sem.at[sl]).start()

    def wait_gather(sl):
        pltpu.make_async_copy(hx_hbm.at[pl.ds(0, MOE_TM)], hx_buf.at[sl], g_sem.at[sl]).wait()

    def start_scatter(tile, sl):
        base = tile * MOE_TM
        for r in range(MOE_TM):
            pltpu.make_async_copy(o_buf.at[sl, pl.ds(r, 1)], o_hbm.at[pl.ds(drow_ref[base + r], 1)],
                                  s_sem.at[sl]).start()

    def wait_scatter(sl):
        pltpu.make_async_copy(o_buf.at[sl], o_hbm.at[pl.ds(0, MOE_TM)], s_sem.at[sl]).wait()

    def zero_fill(tile):
        return pltpu.make_async_copy(z_buf, o_hbm.at[pl.ds(tile * MOE_TM, MOE_TM)], z_sem)

    @pl.when(n == 0)
    def _():
        z_buf[...] = jnp.zeros_like(z_buf)
        start_gather(0, 0)

    @pl.when(n >= n_valid)
    def _():
        @pl.when(n > n_valid)
        def _():
            zero_fill(n - 1).wait()

        zero_fill(n).start()

        @pl.when(n == MOE_TILES - 1)
        def _():
            zero_fill(n).wait()

    @pl.when(new_a_ref[n] == 1)
    def _():
        wga_s[...] = wga_ref[...].astype(BF16)
        wua_s[...] = wua_ref[...].astype(BF16)
        wda_s[...] = wda_ref[...].astype(BF16)

    @pl.when(new_b_ref[n] == 1)
    def _():
        wgb_s[...] = wgb_ref[...].astype(BF16)
        wub_s[...] = wub_ref[...].astype(BF16)
        wdb_s[...] = wdb_ref[...].astype(BF16)

    @pl.when(n + 1 < n_valid)
    def _():
        start_gather(n + 1, 1 - slot)

    @pl.when(n < n_valid)
    def _():
        wait_gather(slot)

        @pl.when(n >= 2)
        def _():
            wait_scatter(slot)

        h = hx_buf[slot, :, 0:D_MODEL].astype(BF16)
        lanes = hx_buf[slot, :, 2 * D_MODEL:]

        def expert(wg, wu, wd, gate):
            hg = jnp.dot(h, wg[...], preferred_element_type=F32)
            hu = jnp.dot(h, wu[...], preferred_element_type=F32)
            act = hg * _sigmoid(hg) * hu * gate
            return jnp.dot(act.astype(BF16), wd[...], preferred_element_type=F32)

        y = expert(wga_s, wua_s, wda_s, lanes[:, 0:1]) + expert(wgb_s, wub_s, wdb_s, lanes[:, 1:2])
        cond_row = lanes[:, 2:3]
        gate2 = jnp.where(cond_row == 0.0, g2_ref[0:1, :], jnp.where(cond_row == 1.0, g2_ref[1:2, :], g2_ref[2:3, :]))
        x2 = hx_buf[slot, :, D_MODEL:2 * D_MODEL] + gate2 * y
        o_buf[slot] = _rmsnorm(x2, fw_ref[...]) if final else x2
        start_scatter(n, slot)

    @pl.when(n == MOE_TILES - 1)
    def _():
        wait_scatter(0)
        wait_scatter(1)


def _experts(tables, hx, mods, l, w_gate, w_up, w_down, final_w, final):
    w_a = lambda r, c: pl.BlockSpec((None, None, r, c), lambda n, ea, eb, *_: (l, ea[n], 0, 0))
    w_b = lambda r, c: pl.BlockSpec((None, None, r, c), lambda n, ea, eb, *_: (l, eb[n], 0, 0))
    up = pltpu.VMEM((D_MODEL, D_EXPERT), BF16)
    down = pltpu.VMEM((D_EXPERT, D_MODEL), BF16)
    return pl.pallas_call(
        functools.partial(_moe_kernel, final=final),
        grid_spec=pltpu.PrefetchScalarGridSpec(
            num_scalar_prefetch=len(tables),
            grid=(MOE_TILES,),
            in_specs=[
                pl.BlockSpec(memory_space=pl.ANY),
                w_a(D_MODEL, D_EXPERT), w_a(D_MODEL, D_EXPERT), w_a(D_EXPERT, D_MODEL),
                w_b(D_MODEL, D_EXPERT), w_b(D_MODEL, D_EXPERT), w_b(D_EXPERT, D_MODEL),
                pl.BlockSpec((None, None, MOD_ROWS, D_MODEL), lambda n, *_: (l, N_MOD - 1, 0, 0)),
                pl.BlockSpec((1, D_MODEL), lambda n, *_: (0, 0)),
            ],
            out_specs=pl.BlockSpec(memory_space=pl.ANY),
            scratch_shapes=[up, up, down, up, up, down,
                            pltpu.VMEM((2, MOE_TM, HX_WIDTH), F32), pltpu.VMEM((2, MOE_TM, D_MODEL), F32),
                            pltpu.VMEM((MOE_TM, D_MODEL), F32),
                            pltpu.SemaphoreType.DMA((2,)), pltpu.SemaphoreType.DMA((2,)), pltpu.SemaphoreType.DMA(())],
        ),
        out_shape=jax.ShapeDtypeStruct((MOE_ROWS, D_MODEL), F32),
        compiler_params=_params("arbitrary"),
        name="experts",
    )(*tables, hx, w_gate, w_up, w_down, w_gate, w_up, w_down, mods, final_w.reshape(1, D_MODEL))


def _routing_tables(cls, to_sequence_order):
    n_cls = N_EXPERT_GROUPS * len(PAIR_SLOT_A)
    onehot = (cls[:, None] == jnp.arange(n_cls, dtype=jnp.int32)[None, :]).astype(jnp.int32)
    counts = jnp.sum(onehot, axis=0)
    tiles = (counts + MOE_TM - 1) // MOE_TM
    tile_end = jnp.cumsum(tiles)
    row_in_class = (tile_end - tiles)[None, :] * MOE_TM + jnp.cumsum(onehot, axis=0) - onehot
    dst = jnp.sum(onehot * row_in_class, axis=1)
    token_plus_1 = jnp.zeros((MOE_ROWS,), jnp.int32).at[dst].set(jnp.arange(1, T_ALL + 1, dtype=jnp.int32))
    is_pad = token_plus_1 == 0
    src = jnp.maximum(token_plus_1 - 1, 0)
    spare = T_ALL + jnp.cumsum(is_pad.astype(jnp.int32)) - 1
    if to_sequence_order:
        j, c, ps = (src // J_ROWS) % SCAN_CHUNK, (src // N_PS) % SEG_CHUNKS, src % N_PS
        target = (src // TILE_TOKENS) * TILE_TOKENS + ps * SEG_LEN + c * SCAN_CHUNK + j
    else:
        target = src
    drow = jnp.where(is_pad, spare, target).astype(jnp.int32)
    n_valid = tile_end[-1]
    tile_id = jnp.minimum(jnp.arange(MOE_TILES, dtype=jnp.int32), n_valid - 1)
    tile_cls = jnp.sum((tile_end[None, :] <= tile_id[:, None]).astype(jnp.int32), axis=1)
    group, pair = tile_cls // len(PAIR_SLOT_A), tile_cls % len(PAIR_SLOT_A)

    def slot_expert(table):
        local = jnp.full_like(pair, table[-1])
        for p in range(len(table) - 1):
            local = jnp.where(pair == p, table[p], local)
        return group * EXPERTS_PER_GROUP + local

    e_a = slot_expert(PAIR_SLOT_A)
    e_b = slot_expert(PAIR_SLOT_B)
    first = jnp.ones((1,), jnp.int32)
    new_a = jnp.concatenate([first, (e_a[1:] != e_a[:-1]).astype(jnp.int32)])
    new_b = jnp.concatenate([first, (e_b[1:] != e_b[:-1]).astype(jnp.int32)])
    return e_a, e_b, new_a, new_b, n_valid.reshape(1).astype(jnp.int32), src, drow


def _dirs_on_lanes(p):
    p = jnp.moveaxis(p, 1, -2)
    return p.reshape(p.shape[:-2] + (2 * SSM_STATE,))


def _to_internal_order(x):
    x = x.reshape(N_TILES, N_PS, SEG_CHUNKS, SCAN_CHUNK, D_MODEL)
    return x.transpose(0, 3, 2, 1, 4).reshape(T_ALL, D_MODEL)


def _gmlp_position_order(w):
    n_lo = GMLP_CHUNK // SCAN_CHUNK
    lead = w.shape[:2]
    w = w.reshape(lead + (n_lo, SCAN_CHUNK) + w.shape[3:])
    w = jnp.swapaxes(w, 2, 3)
    return w.reshape(lead + (GMLP_CHUNK,) + w.shape[4:])


def kernel(x_prompt, x_sample, c, state_ssm_re, state_ssm_im, c_ctx, norm1_w, norm2_w, w_mod, b_mod, w_in,
           ssm_a_re, ssm_a_im, ssm_log_dt, ssm_b_re, ssm_b_im, ssm_c_re, ssm_c_im, ssm_d, w_glu,
           gmlp_ln_w, gmlp_ln_b, gmlp_w_s, gmlp_b_s, w_out, router_w, router_b, w_gate, w_up, w_down,
           final_norm_w):
    x = jnp.concatenate([x_prompt.reshape(T_PROMPT, D_MODEL), x_sample.reshape(T_SAMPLE, D_MODEL)], axis=0)
    x = _to_internal_order(x)

    cvec = jnp.concatenate([c_ctx[None, :], c, jnp.zeros((MOD_ROWS - 1 - N_SAMPLE_SEQ, D_MODEL), F32)], axis=0)
    mods = _modulation(cvec, w_mod, b_mod)
    ps_row = [[0] * N_PS] * SAMPLE_TILE + [[1 + p // SEGS_PER_SAMPLE_SEQ for p in range(N_PS)]]
    mod8 = mods[:, :, jnp.array(ps_row, jnp.int32), :].transpose(0, 2, 1, 3, 4)

    log_dt = jnp.broadcast_to(ssm_log_dt[..., None], ssm_a_re.shape)
    d_lanes = jnp.tile(ssm_d.reshape(DEPTH, N_GROUPS, SSM_GROUP), (1, 1, 128 // SSM_GROUP))
    vecs = jnp.stack([_dirs_on_lanes(ssm_a_re), _dirs_on_lanes(ssm_a_im), _dirs_on_lanes(log_dt), d_lanes], axis=2)
    vecs = jnp.concatenate([vecs, jnp.zeros((DEPTH, N_GROUPS, 4, 2 * SSM_STATE), F32)], axis=2)
    mats = jnp.stack([_dirs_on_lanes(jnp.swapaxes(ssm_b_re, -1, -2)), _dirs_on_lanes(jnp.swapaxes(ssm_b_im, -1, -2)),
                      _dirs_on_lanes(ssm_c_re), _dirs_on_lanes(ssm_c_im)], axis=2)
    wft, cct, ttt, a16 = _ssm_operators(vecs, mats)

    w_in_g = w_in[:, :, D_SSM:].astype(BF16)
    w_in_st = jnp.swapaxes(w_in[:, :, :D_SSM], 1, 2).astype(BF16)
    w_glu_t = jnp.swapaxes(w_glu, 1, 2).astype(BF16)
    w_s = jnp.swapaxes(_gmlp_position_order(jnp.swapaxes(_gmlp_position_order(gmlp_w_s), 2, 3)), 2, 3).astype(BF16)
    w_out = w_out.astype(BF16)
    b_s = _gmlp_position_order(gmlp_b_s)[..., None]
    rwt = router_w.T
    rb = router_b.reshape(N_EXPERTS, 1)

    new_re, new_im = [], []
    for l in range(DEPTH):
        ug, vn, xt = _input_proj(x, mod8, l, norm1_w, w_in_g, w_in_st, gmlp_ln_w, gmlp_ln_b)
        s0 = jnp.concatenate([state_ssm_re[:, l].transpose(2, 0, 1, 3).reshape(N_GROUPS, N_SAMPLE_SEQ, 128),
                              state_ssm_im[:, l].transpose(2, 0, 1, 3).reshape(N_GROUPS, N_SAMPLE_SEQ, 128)], axis=-1)
        s0 = jnp.repeat(s0, SEGS_PER_SAMPLE_SEQ, axis=1)
        yt, fin = _ssm_scan(xt, wft, cct, ttt, a16, s0, l)
        fin = fin[:SAMPLE_TILE].reshape(SAMPLE_TILE, N_GROUPS, N_PS, 2, 2, SSM_STATE)
        fin = fin.transpose(3, 0, 2, 4, 1, 5).reshape(2, N_PROMPT_SEQ, 2, N_GROUPS, SSM_STATE)
        new_re.append(fin[0])
        new_im.append(fin[1])
        hx, route = _mix_out(x, yt, ug, vn, mod8, l, w_glu_t, w_s, b_s, w_out, norm2_w, rwt, rb)
        final = l == DEPTH - 1
        tables = _routing_tables(route[0].astype(jnp.int32), to_sequence_order=final)
        x = _experts(tables, hx, mods, l, w_gate, w_up, w_down, final_norm_w, final)

    y_prompt = x[:T_PROMPT].reshape(N_PROMPT_SEQ, PROMPT_LEN, D_MODEL)
    y_sample = x[T_PROMPT:T_ALL].reshape(N_SAMPLE_SEQ, SAMPLE_LEN, D_MODEL)
    return (y_prompt, y_sample, jnp.stack(new_re, axis=1), jnp.stack(new_im, axis=1))
```

```python
import functools
import math

import jax
import jax.numpy as jnp
from jax import lax
from jax.experimental import pallas as pl
from jax.experimental.pallas import tpu as pltpu

F32 = jnp.float32
BF16 = jnp.bfloat16

D_MODEL = 1024
N_PROMPT_SEQ = 16
PROMPT_LEN = 256
N_SAMPLE_SEQ = 2
SAMPLE_LEN = 1024
T_PROMPT = N_PROMPT_SEQ * PROMPT_LEN
T_SAMPLE = N_SAMPLE_SEQ * SAMPLE_LEN
T_ALL = T_PROMPT + T_SAMPLE
DEPTH = 2
D_SSM = 512
SSM_GROUP = 16
N_GROUPS = 32
SSM_STATE = 64
D_GMLP = 512
GMLP_HEADS = 4
GMLP_HEAD_DIM = 128
GMLP_CHUNK = 128
N_EXPERTS = 16
N_EXPERT_GROUPS = 4
EXPERTS_PER_GROUP = 4
D_EXPERT = 512
N_MOD = 6
EPS = 1e-6

SCAN_CHUNK = 16
CHUNK_WIDTH = SCAN_CHUNK * SSM_GROUP
SEG_LEN = 256
SEG_CHUNKS = SEG_LEN // SCAN_CHUNK
N_PS = 8
TILE_TOKENS = N_PS * SEG_LEN
N_TILES = T_ALL // TILE_TOKENS
SAMPLE_TILE = T_PROMPT // TILE_TOKENS
SEGS_PER_SAMPLE_SEQ = SAMPLE_LEN // SEG_LEN
J_ROWS = SEG_CHUNKS * N_PS
GROUP_BLOCK = 8
PREP_GROUPS = 4
MOD_ROWS = 8

J_PER_STEP = 4
TOKEN_TILE = J_PER_STEP * J_ROWS
PAIR_SLOT_A = (0, 0, 0, 1, 1, 3)
PAIR_SLOT_B = (1, 2, 3, 3, 2, 2)
MOE_TM = 128
MOE_TILES = T_ALL // MOE_TM + N_EXPERT_GROUPS * len(PAIR_SLOT_A)
MOE_ROWS = MOE_TILES * MOE_TM
ROUTE_LANES = 128
HX_WIDTH = 2 * D_MODEL + ROUTE_LANES
VMEM_LIMIT = 56 * 1024 * 1024
TRANS_B = (((1,), (1,)), ((), ()))


def _sigmoid(x):
    return 1.0 / (1.0 + jnp.exp(-x))


def _gelu_tanh(x):
    c = math.sqrt(2.0 / math.pi)
    return x * (0.5 * (1.0 + jnp.tanh(c * (x + 0.044715 * (x * x * x)))))


def _rmsnorm(x, w):
    return x * lax.rsqrt(jnp.mean(x * x, axis=-1, keepdims=True) + EPS) * w


def _per_ps(fn, a, *mods):
    rows, d = a.shape
    out = fn(a.reshape(rows // N_PS, N_PS, d), *[m[None] for m in mods])
    return out.reshape(rows, d)


def _params(*sem):
    return pltpu.CompilerParams(dimension_semantics=sem, vmem_limit_bytes=VMEM_LIMIT)


def _mod_kernel(c_ref, w_ref, b_ref, o_ref):
    c = c_ref[...]
    s = c * _sigmoid(c)
    o_ref[...] = jnp.dot(s.astype(BF16), w_ref[...].astype(BF16), preferred_element_type=F32) + b_ref[...]


def _modulation(cvec, w_mod, b_mod):
    return pl.pallas_call(
        _mod_kernel,
        grid=(DEPTH, N_MOD),
        in_specs=[
            pl.BlockSpec((MOD_ROWS, D_MODEL), lambda l, n: (0, 0)),
            pl.BlockSpec((None, D_MODEL, D_MODEL), lambda l, n: (l, 0, n)),
            pl.BlockSpec((None, None, 1, D_MODEL), lambda l, n: (l, n, 0, 0)),
        ],
        out_specs=pl.BlockSpec((None, None, MOD_ROWS, D_MODEL), lambda l, n: (l, n, 0, 0)),
        out_shape=jax.ShapeDtypeStruct((DEPTH, N_MOD, MOD_ROWS, D_MODEL), F32),
        compiler_params=_params("arbitrary", "arbitrary"),
        name="adaln_mod",
    )(cvec, w_mod, b_mod.reshape(DEPTH, N_MOD, 1, D_MODEL))


def _in_kernel(x_ref, mod_ref, nw_ref, wg_ref, wst_ref, lnw_ref, lnb_ref, ug_ref, vn_ref, xt_ref):
    y = _rmsnorm(x_ref[...], nw_ref[...])
    h = _per_ps(lambda a, sc, sh: a * (1.0 + sc) + sh, y, mod_ref[1], mod_ref[0]).astype(BF16)
    zg = _gelu_tanh(jnp.dot(h, wg_ref[...], preferred_element_type=F32))
    v = zg[:, D_GMLP:]
    mu = jnp.mean(v, axis=-1, keepdims=True)
    vc = v - mu
    var = jnp.mean(vc * vc, axis=-1, keepdims=True)
    vn = vc * lax.rsqrt(var + EPS) * lnw_ref[...] + lnb_ref[...]
    for hd in range(GMLP_HEADS):
        cols = slice(hd * GMLP_HEAD_DIM, (hd + 1) * GMLP_HEAD_DIM)
        ug_ref[hd] = zg[:, cols]
        vn_ref[hd] = vn[:, cols]
    wst = wst_ref[...]
    for k in range(J_PER_STEP):
        hk = h[k * J_ROWS:(k + 1) * J_ROWS]
        xt_ref[k] = lax.dot_general(wst, hk, TRANS_B, preferred_element_type=F32).astype(BF16)


def _input_proj(x, mod8, l, norm1_w, w_in_g, w_in_st, ln_w, ln_b):
    tm = TOKEN_TILE
    steps = TILE_TOKENS // tm
    tok = lambda n: pl.BlockSpec((tm, n), lambda t, s: (t * steps + s, 0))
    heads = pl.BlockSpec((GMLP_HEADS, tm, GMLP_HEAD_DIM), lambda t, s: (0, t * steps + s, 0))
    lay = lambda *shape: pl.BlockSpec((None,) + shape, lambda t, s: (l,) + (0,) * len(shape))
    return pl.pallas_call(
        _in_kernel,
        grid=(N_TILES, steps),
        in_specs=[
            tok(D_MODEL),
            pl.BlockSpec((None, None, N_MOD, N_PS, D_MODEL), lambda t, s: (l, t, 0, 0, 0)),
            lay(1, D_MODEL), lay(D_MODEL, 2 * D_GMLP), lay(D_SSM, D_MODEL), lay(1, D_GMLP), lay(1, D_GMLP),
        ],
        out_specs=[heads, heads,
                   pl.BlockSpec((None, J_PER_STEP, D_SSM, J_ROWS), lambda t, s: (t, s, 0, 0))],
        out_shape=[jax.ShapeDtypeStruct((GMLP_HEADS, T_ALL, GMLP_HEAD_DIM), F32)] * 2 + [
            jax.ShapeDtypeStruct((N_TILES, SCAN_CHUNK, D_SSM, J_ROWS), BF16)],
        compiler_params=_params("arbitrary", "arbitrary"),
        name="norm1_in_proj",
    )(x, mod8, norm1_w.reshape(DEPTH, 1, D_MODEL), w_in_g, w_in_st,
      ln_w.reshape(DEPTH, 1, D_GMLP), ln_b.reshape(DEPTH, 1, D_GMLP))


def _shift_lanes_right(a, b, s, lane):
    if s == 0:
        return a, b
    if s == 128:
        return jnp.zeros_like(a), a
    if s < 128:
        ra = pltpu.roll(a, s, 1)
        rb = pltpu.roll(b, s, 1)
        return jnp.where(lane >= s, ra, 0.0), jnp.where(lane >= s, rb, ra)
    t = s - 128
    return jnp.zeros_like(a), jnp.where(lane >= t, pltpu.roll(a, t, 1), 0.0)


def _shift_lanes_left(a, b, s, lane):
    if s == 0:
        return a, b
    if s == 128:
        return b, jnp.zeros_like(b)
    if s < 128:
        ra = pltpu.roll(a, 128 - s, 1)
        rb = pltpu.roll(b, 128 - s, 1)
        return jnp.where(lane < 128 - s, ra, rb), jnp.where(lane < 128 - s, rb, 0.0)
    t = s - 128
    return jnp.where(lane < 128 - t, pltpu.roll(b, 128 - t, 1), 0.0), jnp.zeros_like(b)


def _prep_kernel(*refs):
    for g in range(PREP_GROUPS):
        _prep_group(*[r.at[g] for r in refs])


def _prep_group(vec_ref, mat_ref, wft_ref, cct_ref, ttt_ref, a16_ref, wf_scr, cm_scr, tt_scr):
    a_re = vec_ref[0:1, :]
    a_im = vec_ref[1:2, :]
    dt = jnp.exp(vec_ref[2:3, :])
    d_skip = vec_ref[3:4, :]
    mag = jnp.exp(a_re * dt)
    ang = a_im * dt
    ab_r = mag * jnp.cos(ang)
    ab_i = mag * jnp.sin(ang)
    den = a_re * a_re + a_im * a_im
    nr = ab_r - 1.0
    q_r = (nr * a_re + ab_i * a_im) / den
    q_i = (ab_i * a_re - nr * a_im) / den
    bt_r = mat_ref[0]
    bt_i = mat_ref[1]
    c_r = mat_ref[2]
    c_i = mat_ref[3]
    bb_r = q_r * bt_r - q_i * bt_i
    bb_i = q_r * bt_i + q_i * bt_r
    p_r = [jnp.ones_like(ab_r)]
    p_i = [jnp.zeros_like(ab_r)]
    for _ in range(SCAN_CHUNK):
        pr, pi = p_r[-1], p_i[-1]
        p_r.append(pr * ab_r - pi * ab_i)
        p_i.append(pr * ab_i + pi * ab_r)
    a16_ref[0] = jnp.broadcast_to(p_r[SCAN_CHUNK], (N_PS, 128))
    a16_ref[1] = jnp.broadcast_to(p_i[SCAN_CHUNK], (N_PS, 128))

    lane = lax.broadcasted_iota(jnp.int32, (1, 128), 1)
    is_fwd = lane < SSM_STATE

    def pick(mf, mb):
        return jnp.where(is_fwd, p_r[mf], p_r[mb]), jnp.where(is_fwd, p_i[mf], p_i[mb])

    for j in range(SCAN_CHUNK):
        rows = slice(j * SSM_GROUP, (j + 1) * SSM_GROUP)
        wr, wi = pick(SCAN_CHUNK - 1 - j, j)
        wf_scr[rows, 0:128] = bb_r * wr - bb_i * wi
        wf_scr[rows, 128:256] = bb_r * wi + bb_i * wr
        wr, wi = pick(j + 1, SCAN_CHUNK - j)
        cct_ref[rows, 0:128] = (c_r * wr - c_i * wi).astype(BF16)
        cct_ref[rows, 128:256] = (-(c_r * wi + c_i * wr)).astype(BF16)
        wr, wi = pick(j, SCAN_CHUNK - 1 - j)
        cm_scr[rows, 0:128] = c_r * wr - c_i * wi
        cm_scr[rows, 128:256] = c_r * wi + c_i * wr
    wft_ref[...] = wf_scr[...].T.astype(BF16)

    zero = jnp.zeros_like(bb_r)
    cm = cm_scr[...]

    def lag_rows(keep):
        lhs = jnp.concatenate([jnp.where(keep, bb_r, zero), jnp.where(keep, -bb_i, zero)], axis=1)
        return lax.dot_general(lhs, cm, TRANS_B, precision=lax.Precision.HIGHEST, preferred_element_type=F32)

    mf = lag_rows(is_fwd)
    mb = lag_rows(jnp.logical_not(is_fwd))
    mf_a, mf_b = mf[:, 0:128], mf[:, 128:256]
    mb_a, mb_b = mb[:, 0:128], mb[:, 128:256]
    row_h = lax.broadcasted_iota(jnp.int32, (SSM_GROUP, 128), 0)
    lane_h = lax.broadcasted_iota(jnp.int32, (SSM_GROUP, 128), 1)
    for jp in range(SCAN_CHUNK):
        rows = slice(jp * SSM_GROUP, (jp + 1) * SSM_GROUP)
        fa, fb = _shift_lanes_right(mf_a, mf_b, SSM_GROUP * jp, lane)
        ba, bb = _shift_lanes_left(mb_a, mb_b, SSM_GROUP * (SCAN_CHUNK - 1 - jp), lane)
        diag = SSM_GROUP * jp + row_h
        tt_scr[rows, 0:128] = fa + ba + jnp.where(lane_h == diag, d_skip, 0.0)
        tt_scr[rows, 128:256] = fb + bb + jnp.where(lane_h + 128 == diag, d_skip, 0.0)
    ttt_ref[...] = tt_scr[...].T.astype(BF16)


def _ssm_operators(vecs, mats):
    op = jax.ShapeDtypeStruct((DEPTH, N_GROUPS, CHUNK_WIDTH, CHUNK_WIDTH), BF16)
    pg = PREP_GROUPS
    op_spec = pl.BlockSpec((None, pg, CHUNK_WIDTH, CHUNK_WIDTH), lambda l, g: (l, g, 0, 0))
    sq = pltpu.VMEM((pg, CHUNK_WIDTH, CHUNK_WIDTH), F32)
    return pl.pallas_call(
        _prep_kernel,
        grid=(DEPTH, N_GROUPS // pg),
        in_specs=[
            pl.BlockSpec((None, pg, 8, 128), lambda l, g: (l, g, 0, 0)),
            pl.BlockSpec((None, pg, 4, SSM_GROUP, 128), lambda l, g: (l, g, 0, 0, 0)),
        ],
        out_specs=[op_spec, op_spec, op_spec,
                   pl.BlockSpec((None, pg, 2, N_PS, 128), lambda l, g: (l, g, 0, 0, 0))],
        out_shape=[op, op, op, jax.ShapeDtypeStruct((DEPTH, N_GROUPS, 2, N_PS, 128), F32)],
        scratch_shapes=[sq, sq, sq],
        compiler_params=_params("arbitrary", "arbitrary"),
        name="s5_operators",
    )(vecs, mats)


def _ssm_kernel(xt_ref, wft_ref, cct_ref, ttt_ref, a16_ref, s0_ref, yt_ref, fin_ref, s_scr, f_scr, ft_scr):
    tile = pl.program_id(0)
    lane = lax.broadcasted_iota(jnp.int32, (GROUP_BLOCK, N_PS, 128), 2)
    is_fwd = lane < SSM_STATE
    half = SSM_STATE

    def group_x(gl):
        return xt_ref[:, gl * SSM_GROUP:(gl + 1) * SSM_GROUP, :].reshape(CHUNK_WIDTH, J_ROWS)

    for gl in range(GROUP_BLOCK):
        ft_scr[gl] = jnp.dot(wft_ref[gl], group_x(gl), preferred_element_type=F32)
        f_scr[gl] = ft_scr[gl].T

    a_r = a16_ref[:, 0]
    a_i = a16_ref[:, 1]

    def scan(s_r, s_i):
        for i in range(SEG_CHUNKS):
            rf = slice(i * N_PS, (i + 1) * N_PS)
            rb = slice((SEG_CHUNKS - 1 - i) * N_PS, (SEG_CHUNKS - i) * N_PS)
            s_scr[:, rf, 0:half] = s_r[:, :, 0:half]
            s_scr[:, rf, 128:128 + half] = s_i[:, :, 0:half]
            s_scr[:, rb, half:128] = s_r[:, :, half:128]
            s_scr[:, rb, 128 + half:256] = s_i[:, :, half:128]
            f_r = jnp.where(is_fwd, f_scr[:, rf, 0:128], f_scr[:, rb, 0:128])
            f_i = jnp.where(is_fwd, f_scr[:, rf, 128:256], f_scr[:, rb, 128:256])
            s_r, s_i = a_r * s_r - a_i * s_i + f_r, a_r * s_i + a_i * s_r + f_i
        return s_r, s_i

    zeros = jnp.zeros((GROUP_BLOCK, N_PS, 128), F32)
    z_r, z_i = scan(zeros, zeros)
    fin_ref[:, :, 0:128] = z_r
    fin_ref[:, :, 128:256] = z_i

    @pl.when(tile == SAMPLE_TILE)
    def _():
        b_r, b_i = a_r, a_i
        for _ in range(4):
            b_r, b_i = b_r * b_r - b_i * b_i, 2.0 * (b_r * b_i)
        seg = lax.broadcasted_iota(jnp.int32, (GROUP_BLOCK, N_PS, 128), 1) % SEGS_PER_SAMPLE_SEQ
        i_r = s0_ref[:, :, 0:128]
        i_i = s0_ref[:, :, 128:256]
        for step in range(1, SEGS_PER_SAMPLE_SEQ):
            pr = jnp.where(is_fwd, pltpu.roll(i_r, 1, 1), pltpu.roll(i_r, N_PS - 1, 1))
            pi = jnp.where(is_fwd, pltpu.roll(i_i, 1, 1), pltpu.roll(i_i, N_PS - 1, 1))
            zr = jnp.where(is_fwd, pltpu.roll(z_r, 1, 1), pltpu.roll(z_r, N_PS - 1, 1))
            zi = jnp.where(is_fwd, pltpu.roll(z_i, 1, 1), pltpu.roll(z_i, N_PS - 1, 1))
            n_r = b_r * pr - b_i * pi + zr
            n_i = b_r * pi + b_i * pr + zi
            first = jnp.where(is_fwd, step, 0)
            last = jnp.where(is_fwd, SEGS_PER_SAMPLE_SEQ - 1, SEGS_PER_SAMPLE_SEQ - 1 - step)
            upd = jnp.logical_and(seg >= first, seg <= last)
            i_r = jnp.where(upd, n_r, i_r)
            i_i = jnp.where(upd, n_i, i_i)
        scan(i_r, i_i)

    for gl in range(GROUP_BLOCK):
        yt = jnp.dot(ttt_ref[gl], group_x(gl), preferred_element_type=F32)
        yt += lax.dot_general(cct_ref[gl], s_scr[gl].astype(BF16), TRANS_B, preferred_element_type=F32)
        yt_ref[:, gl * SSM_GROUP:(gl + 1) * SSM_GROUP, :] = yt.reshape(SCAN_CHUNK, SSM_GROUP, J_ROWS)


def _ssm_scan(xt, wft, cct, ttt, a16, s0, l):
    gb = GROUP_BLOCK
    op_spec = pl.BlockSpec((None, gb, CHUNK_WIDTH, CHUNK_WIDTH), lambda t, g: (l, g, 0, 0))
    io_spec = pl.BlockSpec((None, SCAN_CHUNK, gb * SSM_GROUP, J_ROWS), lambda t, g: (t, 0, g, 0))
    return pl.pallas_call(
        _ssm_kernel,
        grid=(N_TILES, N_GROUPS // gb),
        in_specs=[
            io_spec, op_spec, op_spec, op_spec,
            pl.BlockSpec((None, gb, 2, N_PS, 128), lambda t, g: (l, g, 0, 0, 0)),
            pl.BlockSpec((gb, N_PS, CHUNK_WIDTH), lambda t, g: (g, 0, 0)),
        ],
        out_specs=[io_spec, pl.BlockSpec((None, gb, N_PS, CHUNK_WIDTH), lambda t, g: (t, g, 0, 0))],
        out_shape=[
            jax.ShapeDtypeStruct((N_TILES, SCAN_CHUNK, D_SSM, J_ROWS), F32),
            jax.ShapeDtypeStruct((N_TILES, N_GROUPS, N_PS, CHUNK_WIDTH), F32),
        ],
        scratch_shapes=[pltpu.VMEM((gb, J_ROWS, CHUNK_WIDTH), F32), pltpu.VMEM((gb, J_ROWS, CHUNK_WIDTH), F32),
                        pltpu.VMEM((gb, CHUNK_WIDTH, J_ROWS), F32)],
        compiler_params=_params("arbitrary", "arbitrary"),
        name="s5_chunk_scan",
    )(xt, wft, cct, ttt, a16, s0)


def _post_kernel(x_ref, yt_ref, ug_ref, vn_ref, mod_ref, wglut_ref, ws_ref, bs_ref, wout_ref,
                 nw_ref, rwt_ref, rb_ref, hx_ref, route_ref, yg_scr, y_scr):
    tile = pl.program_id(0)
    step = pl.program_id(1)

    @pl.when(step == 0)
    def _():
        def chunk(n, carry):
            ps = n % N_PS
            c_hi = n // N_PS
            base = c_hi * (GMLP_CHUNK // SCAN_CHUNK) * N_PS + ps
            rows = [pl.ds(j * J_ROWS + base, GMLP_CHUNK // SCAN_CHUNK, stride=N_PS) for j in range(SCAN_CHUNK)]
            for h in range(GMLP_HEADS):
                v = jnp.concatenate([vn_ref[h, r, :] for r in rows], axis=0).astype(BF16)
                u = jnp.concatenate([ug_ref[h, r, :] for r in rows], axis=0)
                s = jnp.dot(ws_ref[h], v, preferred_element_type=F32) + bs_ref[h]
                yg = u * s
                for j, r in enumerate(rows):
                    yg_scr[h, r, :] = yg[j * 8:(j + 1) * 8]
            return carry

        lax.fori_loop(0, TILE_TOKENS // GMLP_CHUNK, chunk, 0)

    wglut = wglut_ref[...]
    for k in range(J_PER_STEP):
        yt = _gelu_tanh(yt_ref[k])
        yt = yt * _sigmoid(jnp.dot(wglut, yt.astype(BF16), preferred_element_type=F32))
        y_scr[k * J_ROWS:(k + 1) * J_ROWS, :] = yt.T
    row0 = pl.multiple_of(step * TOKEN_TILE, TOKEN_TILE)
    proj = jnp.dot(y_scr[...].astype(BF16), wout_ref[0:D_SSM, :], preferred_element_type=F32)
    yg = jnp.concatenate([yg_scr[h, pl.ds(row0, TOKEN_TILE), :] for h in range(GMLP_HEADS)], axis=1)
    proj += jnp.dot(yg.astype(BF16), wout_ref[D_SSM:, :], preferred_element_type=F32)
    x1 = x_ref[...] + _per_ps(lambda a, g: a * g, proj, mod_ref[2])
    h2 = _per_ps(lambda a, sc, sh: a * (1.0 + sc) + sh, _rmsnorm(x1, nw_ref[...]), mod_ref[4], mod_ref[3])
    hx_ref[:, 0:D_MODEL] = h2
    hx_ref[:, D_MODEL:2 * D_MODEL] = x1
    logits = lax.dot_general(rwt_ref[...], h2, TRANS_B, precision=lax.Precision.HIGHEST,
                             preferred_element_type=F32)
    scores = _sigmoid(logits)
    sel = scores + rb_ref[...]
    sc = [scores[e:e + 1, :] for e in range(N_EXPERTS)]
    sl = [sel[e:e + 1, :] for e in range(N_EXPERTS)]
    gscore = []
    for g in range(N_EXPERT_GROUPS):
        v0, v1, v2, v3 = sl[4 * g:4 * g + 4]
        hi01, lo01 = jnp.maximum(v0, v1), jnp.minimum(v0, v1)
        hi23, lo23 = jnp.maximum(v2, v3), jnp.minimum(v2, v3)
        top1 = jnp.maximum(hi01, hi23)
        top2 = jnp.maximum(jnp.minimum(hi01, hi23), jnp.maximum(lo01, lo23))
        gscore.append(top1 + top2)
    best = gscore[0]
    gidx = jnp.zeros_like(best, dtype=jnp.int32)
    for g in range(1, N_EXPERT_GROUPS):
        upd = gscore[g] > best
        gidx = jnp.where(upd, g, gidx)
        best = jnp.where(upd, gscore[g], best)

    def in_group(vals, k):
        out = vals[k]
        for g in range(1, N_EXPERT_GROUPS):
            out = jnp.where(gidx == g, vals[4 * g + k], out)
        return out

    v = [in_group(sl, k) for k in range(EXPERTS_PER_GROUP)]
    s = [in_group(sc, k) for k in range(EXPERTS_PER_GROUP)]
    w = []
    bits = jnp.zeros_like(gidx)
    for k in range(EXPERTS_PER_GROUP):
        rank = jnp.zeros_like(gidx)
        for j in range(EXPERTS_PER_GROUP):
            if j == k:
                continue
            ahead = (v[j] >= v[k]) if j < k else (v[j] > v[k])
            rank = rank + ahead.astype(jnp.int32)
        w.append(jnp.where(rank < 2, s[k], 0.0))
        bits = bits + jnp.where(rank < 2, 1 << k, 0)
    denom = (w[0] + w[1]) + (w[2] + w[3])
    gate = [wk / denom for wk in w]
    pair = jnp.full_like(gidx, len(PAIR_SLOT_A) - 1)
    for p in range(len(PAIR_SLOT_A) - 1):
        pair = jnp.where(bits == (1 << PAIR_SLOT_A[p]) + (1 << PAIR_SLOT_B[p]), p, pair)

    def slot_gate(table):
        out = gate[table[0]]
        for p in range(1, len(table)):
            out = jnp.where(pair == p, gate[table[p]], out)
        return out

    n_tok = route_ref.shape[1]
    route_ref[0:1, :] = (gidx * len(PAIR_SLOT_A) + pair).astype(F32)
    ps = lax.broadcasted_iota(jnp.int32, (1, n_tok), 1) % N_PS
    cond = jnp.where(tile < SAMPLE_TILE, 0, 1 + ps // SEGS_PER_SAMPLE_SEQ).astype(F32)
    lanes = jnp.concatenate([slot_gate(PAIR_SLOT_A), slot_gate(PAIR_SLOT_B), cond,
                             jnp.zeros((ROUTE_LANES - 3, n_tok), F32)], axis=0)
    hx_ref[:, 2 * D_MODEL:] = lanes.T


def _mix_out(x, yt, ug, vn, mod8, l, w_glu_t, w_s, b_s, w_out, norm2_w, rwt, rb):
    tm = TOKEN_TILE
    steps = TILE_TOKENS // tm
    tok = lambda n: pl.BlockSpec((tm, n), lambda t, s: (t * steps + s, 0))
    whole_tile = pl.BlockSpec((GMLP_HEADS, TILE_TOKENS, GMLP_HEAD_DIM), lambda t, s: (0, t, 0))
    lay = lambda *shape: pl.BlockSpec((None,) + shape, lambda t, s: (l,) + (0,) * len(shape))
    return pl.pallas_call(
        _post_kernel,
        grid=(N_TILES, steps),
        in_specs=[
            tok(D_MODEL),
            pl.BlockSpec((None, J_PER_STEP, D_SSM, J_ROWS), lambda t, s: (t, s, 0, 0)),
            whole_tile, whole_tile,
            pl.BlockSpec((None, None, N_MOD, N_PS, D_MODEL), lambda t, s: (l, t, 0, 0, 0)),
            lay(D_SSM, D_SSM), lay(GMLP_HEADS, GMLP_CHUNK, GMLP_CHUNK),
            lay(GMLP_HEADS, GMLP_CHUNK, 1), lay(D_MODEL, D_MODEL), lay(1, D_MODEL),
            pl.BlockSpec((N_EXPERTS, D_MODEL), lambda t, s: (0, 0)),
            pl.BlockSpec((N_EXPERTS, 1), lambda t, s: (0, 0)),
        ],
        out_specs=[tok(HX_WIDTH), pl.BlockSpec((1, tm), lambda t, s: (0, t * steps + s))],
        out_shape=[
            jax.ShapeDtypeStruct((T_ALL, HX_WIDTH), F32),
            jax.ShapeDtypeStruct((1, T_ALL), F32),
        ],
        scratch_shapes=[pltpu.VMEM((GMLP_HEADS, TILE_TOKENS, GMLP_HEAD_DIM), F32), pltpu.VMEM((tm, D_SSM), F32)],
        compiler_params=_params("arbitrary", "arbitrary"),
        name="mixers_out_router",
    )(x, yt, ug, vn, mod8, w_glu_t, w_s, b_s, w_out, norm2_w.reshape(DEPTH, 1, D_MODEL), rwt, rb)


def _moe_kernel(ea_ref, eb_ref, new_a_ref, new_b_ref, nv_ref, src_ref, drow_ref,
                hx_hbm, wga_ref, wua_ref, wda_ref, wgb_ref, wub_ref, wdb_ref, g2_ref, fw_ref,
                o_hbm, wga_s, wua_s, wda_s, wgb_s, wub_s, wdb_s, hx_buf, o_buf, z_buf, g_sem, s_sem, z_sem,
                *, final):
    n = pl.program_id(0)
    n_valid = nv_ref[0]
    slot = n % 2
    other = 1 - slot
    all_rows = range(MOE_TM)
    prev_tile = jnp.where(n == 0, MOE_TILES + 1, n - 1)
    next_tile = jnp.minimum(n + 1, n_valid - 1)

    def gather_rows(tile, sl, rows):
        base = tile * MOE_TM
        for r in rows:
            pltpu.make_async_copy(hx_hbm.at[pl.ds(src_ref[base + r], 1)], hx_buf.at[sl, pl.ds(r, 1)],
                                  g_sem.at[sl]).start()

    def wait_gather(sl):
        pltpu.make_async_copy(hx_hbm.at[pl.ds(0, MOE_TM)], hx_buf.at[sl], g_sem.at[sl]).wait()

    def scatter_rows(tile, sl, rows):
        base = tile * MOE_TM
        for r in rows:
            pltpu.make_async_copy(o_buf.at[sl, pl.ds(r, 1)], o_hbm.at[pl.ds(drow_ref[base + r], 1)],
                                  s_sem.at[sl]).start()

    def wait_scatter(sl):
        pltpu.make_async_copy(o_buf.at[sl], o_hbm.at[pl.ds(0, MOE_TM)], s_sem.at[sl]).wait()

    def zero_fill(tile):
        return pltpu.make_async_copy(z_buf, o_hbm.at[pl.ds(tile * MOE_TM, MOE_TM)], z_sem)

    @pl.when(n == 0)
    def _():
        z_buf[...] = jnp.zeros_like(z_buf)
        o_buf[...] = jnp.zeros_like(o_buf)
        gather_rows(0, 0, all_rows)
        scatter_rows(MOE_TILES, 0, all_rows)

    @pl.when(jnp.logical_and(n >= n_valid, n < MOE_TILES))
    def _():
        @pl.when(n > n_valid)
        def _():
            zero_fill(n - 1).wait()

        zero_fill(n).start()

        @pl.when(n == MOE_TILES - 1)
        def _():
            zero_fill(n).wait()

    @pl.when(new_a_ref[n] == 1)
    def _():
        wga_s[...] = wga_ref[...].astype(BF16)
        wua_s[...] = wua_ref[...].astype(BF16)
        wda_s[...] = wda_ref[...].astype(BF16)

    @pl.when(new_b_ref[n] == 1)
    def _():
        wgb_s[...] = wgb_ref[...].astype(BF16)
        wub_s[...] = wub_ref[...].astype(BF16)
        wdb_s[...] = wdb_ref[...].astype(BF16)

    @pl.when(n < n_valid)
    def _():
        wait_gather(slot)
        h = hx_buf[slot, :, 0:D_MODEL].astype(BF16)
        lanes = hx_buf[slot, :, 2 * D_MODEL:]
        q = MOE_TM // 4

        def silu_mul(hg, hu, gate):
            return (hg * _sigmoid(hg) * hu * gate).astype(BF16)

        hg = jnp.dot(h, wga_s[...], preferred_element_type=F32)
        gather_rows(next_tile, other, range(0, q))
        hu = jnp.dot(h, wua_s[...], preferred_element_type=F32)
        gather_rows(next_tile, other, range(q, 2 * q))
        y = jnp.dot(silu_mul(hg, hu, lanes[:, 0:1]), wda_s[...], preferred_element_type=F32)
        gather_rows(next_tile, other, range(2 * q, 3 * q))
        hg = jnp.dot(h, wgb_s[...], preferred_element_type=F32)
        gather_rows(next_tile, other, range(3 * q, 4 * q))
        hu = jnp.dot(h, wub_s[...], preferred_element_type=F32)
        scatter_rows(prev_tile, other, range(0, 2 * q))
        y += jnp.dot(silu_mul(hg, hu, lanes[:, 1:2]), wdb_s[...], preferred_element_type=F32)
        scatter_rows(prev_tile, other, range(2 * q, 4 * q))
        cond_row = lanes[:, 2:3]
        gate2 = jnp.where(cond_row == 0.0, g2_ref[0:1, :], jnp.where(cond_row == 1.0, g2_ref[1:2, :], g2_ref[2:3, :]))
        x2 = hx_buf[slot, :, D_MODEL:2 * D_MODEL] + gate2 * y
        wait_scatter(slot)
        o_buf[slot] = _rmsnorm(x2, fw_ref[...]) if final else x2

    @pl.when(n == n_valid)
    def _():
        scatter_rows(n - 1, other, all_rows)

    @pl.when(n == MOE_TILES)
    def _():
        wait_gather(n_valid % 2)
        wait_scatter(0)
        wait_scatter(1)


def _experts(tables, hx, mods, l, w_gate, w_up, w_down, final_w, final):
    w_a = lambda r, c: pl.BlockSpec((None, None, r, c), lambda n, ea, eb, *_: (l, ea[n], 0, 0))
    w_b = lambda r, c: pl.BlockSpec((None, None, r, c), lambda n, ea, eb, *_: (l, eb[n], 0, 0))
    up = pltpu.VMEM((D_MODEL, D_EXPERT), BF16)
    down = pltpu.VMEM((D_EXPERT, D_MODEL), BF16)
    return pl.pallas_call(
        functools.partial(_moe_kernel, final=final),
        grid_spec=pltpu.PrefetchScalarGridSpec(
            num_scalar_prefetch=len(tables),
            grid=(MOE_TILES + 1,),
            in_specs=[
                pl.BlockSpec(memory_space=pl.ANY),
                w_a(D_MODEL, D_EXPERT), w_a(D_MODEL, D_EXPERT), w_a(D_EXPERT, D_MODEL),
                w_b(D_MODEL, D_EXPERT), w_b(D_MODEL, D_EXPERT), w_b(D_EXPERT, D_MODEL),
                pl.BlockSpec((None, None, MOD_ROWS, D_MODEL), lambda n, *_: (l, N_MOD - 1, 0, 0)),
                pl.BlockSpec((1, D_MODEL), lambda n, *_: (0, 0)),
            ],
            out_specs=pl.BlockSpec(memory_space=pl.ANY),
            scratch_shapes=[up, up, down, up, up, down,
                            pltpu.VMEM((2, MOE_TM, HX_WIDTH), F32), pltpu.VMEM((2, MOE_TM, D_MODEL), F32),
                            pltpu.VMEM((MOE_TM, D_MODEL), F32),
                            pltpu.SemaphoreType.DMA((2,)), pltpu.SemaphoreType.DMA((2,)), pltpu.SemaphoreType.DMA(())],
        ),
        out_shape=jax.ShapeDtypeStruct((MOE_ROWS + 2 * MOE_TM, D_MODEL), F32),
        compiler_params=_params("arbitrary"),
        name="experts",
    )(*tables, hx, w_gate, w_up, w_down, w_gate, w_up, w_down, mods, final_w.reshape(1, D_MODEL))


def _routing_tables(cls, to_sequence_order):
    n_cls = N_EXPERT_GROUPS * len(PAIR_SLOT_A)
    onehot = (cls[:, None] == jnp.arange(n_cls, dtype=jnp.int32)[None, :]).astype(jnp.int32)
    counts = jnp.sum(onehot, axis=0)
    tiles = (counts + MOE_TM - 1) // MOE_TM
    tile_end = jnp.cumsum(tiles)
    row_in_class = (tile_end - tiles)[None, :] * MOE_TM + jnp.cumsum(onehot, axis=0) - onehot
    dst = jnp.sum(onehot * row_in_class, axis=1)
    token_plus_1 = jnp.zeros((MOE_ROWS,), jnp.int32).at[dst].set(jnp.arange(1, T_ALL + 1, dtype=jnp.int32))
    is_pad = token_plus_1 == 0
    src = jnp.maximum(token_plus_1 - 1, 0)
    spare = T_ALL + jnp.cumsum(is_pad.astype(jnp.int32)) - 1
    if to_sequence_order:
        j, c, ps = (src // J_ROWS) % SCAN_CHUNK, (src // N_PS) % SEG_CHUNKS, src % N_PS
        target = (src // TILE_TOKENS) * TILE_TOKENS + ps * SEG_LEN + c * SCAN_CHUNK + j
    else:
        target = src
    drow = jnp.where(is_pad, spare, target).astype(jnp.int32)
    drow = jnp.concatenate([drow, MOE_ROWS + jnp.arange(2 * MOE_TM, dtype=jnp.int32)])
    n_valid = tile_end[-1]
    tile_id = jnp.minimum(jnp.arange(MOE_TILES + 1, dtype=jnp.int32), n_valid - 1)
    tile_cls = jnp.sum((tile_end[None, :] <= tile_id[:, None]).astype(jnp.int32), axis=1)
    group, pair = tile_cls // len(PAIR_SLOT_A), tile_cls % len(PAIR_SLOT_A)

    def slot_expert(table):
        local = jnp.full_like(pair, table[-1])
        for p in range(len(table) - 1):
            local = jnp.where(pair == p, table[p], local)
        return group * EXPERTS_PER_GROUP + local

    e_a = slot_expert(PAIR_SLOT_A)
    e_b = slot_expert(PAIR_SLOT_B)
    first = jnp.ones((1,), jnp.int32)
    new_a = jnp.concatenate([first, (e_a[1:] != e_a[:-1]).astype(jnp.int32)])
    new_b = jnp.concatenate([first, (e_b[1:] != e_b[:-1]).astype(jnp.int32)])
    return e_a, e_b, new_a, new_b, n_valid.reshape(1).astype(jnp.int32), src, drow


def _dirs_on_lanes(p):
    p = jnp.moveaxis(p, 1, -2)
    return p.reshape(p.shape[:-2] + (2 * SSM_STATE,))


def _to_internal_order(x):
    x = x.reshape(N_TILES, N_PS, SEG_CHUNKS, SCAN_CHUNK, D_MODEL)
    return x.transpose(0, 3, 2, 1, 4).reshape(T_ALL, D_MODEL)


def _gmlp_position_order(w):
    n_lo = GMLP_CHUNK // SCAN_CHUNK
    lead = w.shape[:2]
    w = w.reshape(lead + (n_lo, SCAN_CHUNK) + w.shape[3:])
    w = jnp.swapaxes(w, 2, 3)
    return w.reshape(lead + (GMLP_CHUNK,) + w.shape[4:])


def kernel(x_prompt, x_sample, c, state_ssm_re, state_ssm_im, c_ctx, norm1_w, norm2_w, w_mod, b_mod, w_in,
           ssm_a_re, ssm_a_im, ssm_log_dt, ssm_b_re, ssm_b_im, ssm_c_re, ssm_c_im, ssm_d, w_glu,
           gmlp_ln_w, gmlp_ln_b, gmlp_w_s, gmlp_b_s, w_out, router_w, router_b, w_gate, w_up, w_down,
           final_norm_w):
    x = jnp.concatenate([x_prompt.reshape(T_PROMPT, D_MODEL), x_sample.reshape(T_SAMPLE, D_MODEL)], axis=0)
    x = _to_internal_order(x)

    cvec = jnp.concatenate([c_ctx[None, :], c, jnp.zeros((MOD_ROWS - 1 - N_SAMPLE_SEQ, D_MODEL), F32)], axis=0)
    mods = _modulation(cvec, w_mod, b_mod)
    ps_row = [[0] * N_PS] * SAMPLE_TILE + [[1 + p // SEGS_PER_SAMPLE_SEQ for p in range(N_PS)]]
    mod8 = mods[:, :, jnp.array(ps_row, jnp.int32), :].transpose(0, 2, 1, 3, 4)

    log_dt = jnp.broadcast_to(ssm_log_dt[..., None], ssm_a_re.shape)
    d_lanes = jnp.tile(ssm_d.reshape(DEPTH, N_GROUPS, SSM_GROUP), (1, 1, 128 // SSM_GROUP))
    vecs = jnp.stack([_dirs_on_lanes(ssm_a_re), _dirs_on_lanes(ssm_a_im), _dirs_on_lanes(log_dt), d_lanes], axis=2)
    vecs = jnp.concatenate([vecs, jnp.zeros((DEPTH, N_GROUPS, 4, 2 * SSM_STATE), F32)], axis=2)
    mats = jnp.stack([_dirs_on_lanes(jnp.swapaxes(ssm_b_re, -1, -2)), _dirs_on_lanes(jnp.swapaxes(ssm_b_im, -1, -2)),
                      _dirs_on_lanes(ssm_c_re), _dirs_on_lanes(ssm_c_im)], axis=2)
    wft, cct, ttt, a16 = _ssm_operators(vecs, mats)

    w_in_g = w_in[:, :, D_SSM:].astype(BF16)
    w_in_st = jnp.swapaxes(w_in[:, :, :D_SSM], 1, 2).astype(BF16)
    w_glu_t = jnp.swapaxes(w_glu, 1, 2).astype(BF16)
    w_s = jnp.swapaxes(_gmlp_position_order(jnp.swapaxes(_gmlp_position_order(gmlp_w_s), 2, 3)), 2, 3).astype(BF16)
    w_out = w_out.astype(BF16)
    b_s = _gmlp_position_order(gmlp_b_s)[..., None]
    rwt = router_w.T
    rb = router_b.reshape(N_EXPERTS, 1)

    new_re, new_im = [], []
    for l in range(DEPTH):
        ug, vn, xt = _input_proj(x, mod8, l, norm1_w, w_in_g, w_in_st, gmlp_ln_w, gmlp_ln_b)
        s0 = jnp.concatenate([state_ssm_re[:, l].transpose(2, 0, 1, 3).reshape(N_GROUPS, N_SAMPLE_SEQ, 128),
                              state_ssm_im[:, l].transpose(2, 0, 1, 3).reshape(N_GROUPS, N_SAMPLE_SEQ, 128)], axis=-1)
        s0 = jnp.repeat(s0, SEGS_PER_SAMPLE_SEQ, axis=1)
        yt, fin = _ssm_scan(xt, wft, cct, ttt, a16, s0, l)
        fin = fin[:SAMPLE_TILE].reshape(SAMPLE_TILE, N_GROUPS, N_PS, 2, 2, SSM_STATE)
        fin = fin.transpose(3, 0, 2, 4, 1, 5).reshape(2, N_PROMPT_SEQ, 2, N_GROUPS, SSM_STATE)
        new_re.append(fin[0])
        new_im.append(fin[1])
        hx, route = _mix_out(x, yt, ug, vn, mod8, l, w_glu_t, w_s, b_s, w_out, norm2_w, rwt, rb)
        final = l == DEPTH - 1
        tables = _routing_tables(route[0].astype(jnp.int32), to_sequence_order=final)
        x = _experts(tables, hx, mods, l, w_gate, w_up, w_down, final_norm_w, final)

    y_prompt = x[:T_PROMPT].reshape(N_PROMPT_SEQ, PROMPT_LEN, D_MODEL)
    y_sample = x[T_PROMPT:T_ALL].reshape(N_SAMPLE_SEQ, SAMPLE_LEN, D_MODEL)
    return (y_prompt, y_sample, jnp.stack(new_re, axis=1), jnp.stack(new_im, axis=1))
```

```python
import functools
import math

import jax
import jax.numpy as jnp
from jax import lax
from jax.experimental import pallas as pl
from jax.experimental.pallas import tpu as pltpu

F32 = jnp.float32
BF16 = jnp.bfloat16

D_MODEL = 1024
N_PROMPT_SEQ = 16
PROMPT_LEN = 256
N_SAMPLE_SEQ = 2
SAMPLE_LEN = 1024
T_PROMPT = N_PROMPT_SEQ * PROMPT_LEN
T_SAMPLE = N_SAMPLE_SEQ * SAMPLE_LEN
T_ALL = T_PROMPT + T_SAMPLE
DEPTH = 2
D_SSM = 512
SSM_GROUP = 16
N_GROUPS = 32
SSM_STATE = 64
D_GMLP = 512
GMLP_HEADS = 4
GMLP_HEAD_DIM = 128
GMLP_CHUNK = 128
N_EXPERTS = 16
N_EXPERT_GROUPS = 4
EXPERTS_PER_GROUP = 4
D_EXPERT = 512
N_MOD = 6
EPS = 1e-6

SCAN_CHUNK = 16
CHUNK_WIDTH = SCAN_CHUNK * SSM_GROUP
SEG_LEN = 256
SEG_CHUNKS = SEG_LEN // SCAN_CHUNK
N_PS = 8
TILE_TOKENS = N_PS * SEG_LEN
N_TILES = T_ALL // TILE_TOKENS
SAMPLE_TILE = T_PROMPT // TILE_TOKENS
SEGS_PER_SAMPLE_SEQ = SAMPLE_LEN // SEG_LEN
J_ROWS = SEG_CHUNKS * N_PS
GROUP_BLOCK = 8
PREP_GROUPS = 4
MOD_ROWS = 8

J_PER_STEP = 4
TOKEN_TILE = J_PER_STEP * J_ROWS
PAIR_SLOT_A = (0, 0, 0, 1, 1, 3)
PAIR_SLOT_B = (1, 2, 3, 3, 2, 2)
MOE_TM = 128
MOE_TILES = T_ALL // MOE_TM + N_EXPERT_GROUPS * len(PAIR_SLOT_A)
MOE_ROWS = MOE_TILES * MOE_TM
ROUTE_LANES = 128
HX_WIDTH = 2 * D_MODEL + ROUTE_LANES
VMEM_LIMIT = 56 * 1024 * 1024
TRANS_B = (((1,), (1,)), ((), ()))


def _sigmoid(x):
    return 1.0 / (1.0 + jnp.exp(-x))


def _gelu_tanh(x):
    c = math.sqrt(2.0 / math.pi)
    return x * (0.5 * (1.0 + jnp.tanh(c * (x + 0.044715 * (x * x * x)))))


def _split_bf16(a):
    hi = a.astype(BF16)
    return hi, (a - hi.astype(F32)).astype(BF16)


def _rmsnorm(x, w):
    return x * lax.rsqrt(jnp.mean(x * x, axis=-1, keepdims=True) + EPS) * w


def _per_ps(fn, a, *mods):
    rows, d = a.shape
    out = fn(a.reshape(rows // N_PS, N_PS, d), *[m[None] for m in mods])
    return out.reshape(rows, d)


def _params(*sem):
    return pltpu.CompilerParams(dimension_semantics=sem, vmem_limit_bytes=VMEM_LIMIT)


def _mod_kernel(c_ref, w_ref, b_ref, o_ref):
    c = c_ref[...]
    s = c * _sigmoid(c)
    o_ref[...] = jnp.dot(s.astype(BF16), w_ref[...].astype(BF16), preferred_element_type=F32) + b_ref[...]


def _modulation(cvec, w_mod, b_mod):
    return pl.pallas_call(
        _mod_kernel,
        grid=(DEPTH, N_MOD),
        in_specs=[
            pl.BlockSpec((MOD_ROWS, D_MODEL), lambda l, n: (0, 0)),
            pl.BlockSpec((None, D_MODEL, D_MODEL), lambda l, n: (l, 0, n)),
            pl.BlockSpec((None, None, 1, D_MODEL), lambda l, n: (l, n, 0, 0)),
        ],
        out_specs=pl.BlockSpec((None, None, MOD_ROWS, D_MODEL), lambda l, n: (l, n, 0, 0)),
        out_shape=jax.ShapeDtypeStruct((DEPTH, N_MOD, MOD_ROWS, D_MODEL), F32),
        compiler_params=_params("arbitrary", "arbitrary"),
        name="adaln_mod",
    )(cvec, w_mod, b_mod.reshape(DEPTH, N_MOD, 1, D_MODEL))


def _in_kernel(x_ref, mod_ref, nw_ref, wg_ref, wst_ref, lnw_ref, lnb_ref, ug_ref, vn_ref, xt_ref):
    y = _rmsnorm(x_ref[...], nw_ref[...])
    h = _per_ps(lambda a, sc, sh: a * (1.0 + sc) + sh, y, mod_ref[1], mod_ref[0]).astype(BF16)
    zg = _gelu_tanh(jnp.dot(h, wg_ref[...], preferred_element_type=F32))
    v = zg[:, D_GMLP:]
    mu = jnp.mean(v, axis=-1, keepdims=True)
    vc = v - mu
    var = jnp.mean(vc * vc, axis=-1, keepdims=True)
    vn = vc * lax.rsqrt(var + EPS) * lnw_ref[...] + lnb_ref[...]
    for hd in range(GMLP_HEADS):
        cols = slice(hd * GMLP_HEAD_DIM, (hd + 1) * GMLP_HEAD_DIM)
        ug_ref[hd] = zg[:, cols]
        vn_ref[hd] = vn[:, cols]
    xt = lax.dot_general(wst_ref[...], h, TRANS_B, preferred_element_type=F32).astype(BF16)
    for k in range(J_PER_STEP):
        xt_ref[k] = xt[:, k * J_ROWS:(k + 1) * J_ROWS]


def _input_proj(x, mod8, l, norm1_w, w_in_g, w_in_st, ln_w, ln_b):
    tm = TOKEN_TILE
    steps = TILE_TOKENS // tm
    tok = lambda n: pl.BlockSpec((tm, n), lambda t, s: (t * steps + s, 0))
    heads = pl.BlockSpec((GMLP_HEADS, tm, GMLP_HEAD_DIM), lambda t, s: (0, t * steps + s, 0))
    lay = lambda *shape: pl.BlockSpec((None,) + shape, lambda t, s: (l,) + (0,) * len(shape))
    return pl.pallas_call(
        _in_kernel,
        grid=(N_TILES, steps),
        in_specs=[
            tok(D_MODEL),
            pl.BlockSpec((None, None, N_MOD, N_PS, D_MODEL), lambda t, s: (l, t, 0, 0, 0)),
            lay(1, D_MODEL), lay(D_MODEL, 2 * D_GMLP), lay(D_SSM, D_MODEL), lay(1, D_GMLP), lay(1, D_GMLP),
        ],
        out_specs=[heads, heads,
                   pl.BlockSpec((None, J_PER_STEP, D_SSM, J_ROWS), lambda t, s: (t, s, 0, 0))],
        out_shape=[jax.ShapeDtypeStruct((GMLP_HEADS, T_ALL, GMLP_HEAD_DIM), F32)] * 2 + [
            jax.ShapeDtypeStruct((N_TILES, SCAN_CHUNK, D_SSM, J_ROWS), BF16)],
        compiler_params=_params("arbitrary", "arbitrary"),
        name="norm1_in_proj",
    )(x, mod8, norm1_w.reshape(DEPTH, 1, D_MODEL), w_in_g, w_in_st,
      ln_w.reshape(DEPTH, 1, D_GMLP), ln_b.reshape(DEPTH, 1, D_GMLP))


def _shift_lanes_right(a, b, s, lane):
    if s == 0:
        return a, b
    if s == 128:
        return jnp.zeros_like(a), a
    if s < 128:
        ra = pltpu.roll(a, s, 1)
        rb = pltpu.roll(b, s, 1)
        return jnp.where(lane >= s, ra, 0.0), jnp.where(lane >= s, rb, ra)
    t = s - 128
    return jnp.zeros_like(a), jnp.where(lane >= t, pltpu.roll(a, t, 1), 0.0)


def _shift_lanes_left(a, b, s, lane):
    if s == 0:
        return a, b
    if s == 128:
        return b, jnp.zeros_like(b)
    if s < 128:
        ra = pltpu.roll(a, 128 - s, 1)
        rb = pltpu.roll(b, 128 - s, 1)
        return jnp.where(lane < 128 - s, ra, rb), jnp.where(lane < 128 - s, rb, 0.0)
    t = s - 128
    return jnp.where(lane < 128 - t, pltpu.roll(b, 128 - t, 1), 0.0), jnp.zeros_like(b)


def _prep_kernel(*refs):
    for g in range(PREP_GROUPS):
        _prep_group(*[r.at[g] for r in refs])


def _prep_group(vec_ref, mat_ref, wft_ref, cct_ref, ttt_ref, a16_ref, wf_scr, cm_scr, tt_scr):
    a_re = vec_ref[0:1, :]
    a_im = vec_ref[1:2, :]
    dt = jnp.exp(vec_ref[2:3, :])
    d_skip = vec_ref[3:4, :]
    mag = jnp.exp(a_re * dt)
    ang = a_im * dt
    ab_r = mag * jnp.cos(ang)
    ab_i = mag * jnp.sin(ang)
    den = a_re * a_re + a_im * a_im
    nr = ab_r - 1.0
    q_r = (nr * a_re + ab_i * a_im) / den
    q_i = (ab_i * a_re - nr * a_im) / den
    bt_r = mat_ref[0]
    bt_i = mat_ref[1]
    c_r = mat_ref[2]
    c_i = mat_ref[3]
    bb_r = q_r * bt_r - q_i * bt_i
    bb_i = q_r * bt_i + q_i * bt_r
    p_r = [jnp.ones_like(ab_r)]
    p_i = [jnp.zeros_like(ab_r)]
    for _ in range(SCAN_CHUNK):
        pr, pi = p_r[-1], p_i[-1]
        p_r.append(pr * ab_r - pi * ab_i)
        p_i.append(pr * ab_i + pi * ab_r)
    a16_ref[0] = jnp.broadcast_to(p_r[SCAN_CHUNK], (N_PS, 128))
    a16_ref[1] = jnp.broadcast_to(p_i[SCAN_CHUNK], (N_PS, 128))

    lane = lax.broadcasted_iota(jnp.int32, (1, 128), 1)
    is_fwd = lane < SSM_STATE

    def pick(mf, mb):
        return jnp.where(is_fwd, p_r[mf], p_r[mb]), jnp.where(is_fwd, p_i[mf], p_i[mb])

    for j in range(SCAN_CHUNK):
        rows = slice(j * SSM_GROUP, (j + 1) * SSM_GROUP)
        wr, wi = pick(SCAN_CHUNK - 1 - j, j)
        wf_scr[rows, 0:128] = bb_r * wr - bb_i * wi
        wf_scr[rows, 128:256] = bb_r * wi + bb_i * wr
        wr, wi = pick(j + 1, SCAN_CHUNK - j)
        cct_ref[rows, 0:128] = (c_r * wr - c_i * wi).astype(BF16)
        cct_ref[rows, 128:256] = (-(c_r * wi + c_i * wr)).astype(BF16)
        wr, wi = pick(j, SCAN_CHUNK - 1 - j)
        cm_scr[rows, 0:128] = c_r * wr - c_i * wi
        cm_scr[rows, 128:256] = c_r * wi + c_i * wr
    wft_ref[...] = wf_scr[...].T.astype(BF16)

    zero = jnp.zeros_like(bb_r)
    cm = cm_scr[...]

    def lag_rows(keep):
        lhs = jnp.concatenate([jnp.where(keep, bb_r, zero), jnp.where(keep, -bb_i, zero)], axis=1)
        return lax.dot_general(lhs, cm, TRANS_B, precision=lax.Precision.HIGHEST, preferred_element_type=F32)

    mf = lag_rows(is_fwd)
    mb = lag_rows(jnp.logical_not(is_fwd))
    mf_a, mf_b = mf[:, 0:128], mf[:, 128:256]
    mb_a, mb_b = mb[:, 0:128], mb[:, 128:256]
    row_h = lax.broadcasted_iota(jnp.int32, (SSM_GROUP, 128), 0)
    lane_h = lax.broadcasted_iota(jnp.int32, (SSM_GROUP, 128), 1)
    for jp in range(SCAN_CHUNK):
        rows = slice(jp * SSM_GROUP, (jp + 1) * SSM_GROUP)
        fa, fb = _shift_lanes_right(mf_a, mf_b, SSM_GROUP * jp, lane)
        ba, bb = _shift_lanes_left(mb_a, mb_b, SSM_GROUP * (SCAN_CHUNK - 1 - jp), lane)
        diag = SSM_GROUP * jp + row_h
        tt_scr[rows, 0:128] = fa + ba + jnp.where(lane_h == diag, d_skip, 0.0)
        tt_scr[rows, 128:256] = fb + bb + jnp.where(lane_h + 128 == diag, d_skip, 0.0)
    ttt_ref[...] = tt_scr[...].T.astype(BF16)


def _ssm_operators(vecs, mats):
    op = jax.ShapeDtypeStruct((DEPTH, N_GROUPS, CHUNK_WIDTH, CHUNK_WIDTH), BF16)
    pg = PREP_GROUPS
    op_spec = pl.BlockSpec((None, pg, CHUNK_WIDTH, CHUNK_WIDTH), lambda l, g: (l, g, 0, 0))
    sq = pltpu.VMEM((pg, CHUNK_WIDTH, CHUNK_WIDTH), F32)
    return pl.pallas_call(
        _prep_kernel,
        grid=(DEPTH, N_GROUPS // pg),
        in_specs=[
            pl.BlockSpec((None, pg, 8, 128), lambda l, g: (l, g, 0, 0)),
            pl.BlockSpec((None, pg, 4, SSM_GROUP, 128), lambda l, g: (l, g, 0, 0, 0)),
        ],
        out_specs=[op_spec, op_spec, op_spec,
                   pl.BlockSpec((None, pg, 2, N_PS, 128), lambda l, g: (l, g, 0, 0, 0))],
        out_shape=[op, op, op, jax.ShapeDtypeStruct((DEPTH, N_GROUPS, 2, N_PS, 128), F32)],
        scratch_shapes=[sq, sq, sq],
        compiler_params=_params("arbitrary", "arbitrary"),
        name="s5_operators",
    )(vecs, mats)


def _ssm_kernel(xt_ref, wft_ref, cct_ref, ttt_ref, a16_ref, s0_ref, yt_ref, fin_ref, s_scr, f_scr, ft_scr):
    tile = pl.program_id(0)
    lane = lax.broadcasted_iota(jnp.int32, (GROUP_BLOCK, N_PS, 128), 2)
    is_fwd = lane < SSM_STATE
    half = SSM_STATE

    def group_x(gl):
        return xt_ref[:, gl * SSM_GROUP:(gl + 1) * SSM_GROUP, :].reshape(CHUNK_WIDTH, J_ROWS)

    for gl in range(GROUP_BLOCK):
        ft_scr[gl] = jnp.dot(wft_ref[gl], group_x(gl), preferred_element_type=F32)
        f_scr[gl] = ft_scr[gl].T

    a_r = a16_ref[:, 0]
    a_i = a16_ref[:, 1]

    def scan(s_r, s_i):
        for i in range(SEG_CHUNKS):
            rf = slice(i * N_PS, (i + 1) * N_PS)
            rb = slice((SEG_CHUNKS - 1 - i) * N_PS, (SEG_CHUNKS - i) * N_PS)
            s_scr[:, rf, 0:half] = s_r[:, :, 0:half]
            s_scr[:, rf, 128:128 + half] = s_i[:, :, 0:half]
            s_scr[:, rb, half:128] = s_r[:, :, half:128]
            s_scr[:, rb, 128 + half:256] = s_i[:, :, half:128]
            f_r = jnp.where(is_fwd, f_scr[:, rf, 0:128], f_scr[:, rb, 0:128])
            f_i = jnp.where(is_fwd, f_scr[:, rf, 128:256], f_scr[:, rb, 128:256])
            s_r, s_i = a_r * s_r - a_i * s_i + f_r, a_r * s_i + a_i * s_r + f_i
        return s_r, s_i

    zeros = jnp.zeros((GROUP_BLOCK, N_PS, 128), F32)
    z_r, z_i = scan(zeros, zeros)
    fin_ref[:, :, 0:128] = z_r
    fin_ref[:, :, 128:256] = z_i

    @pl.when(tile == SAMPLE_TILE)
    def _():
        b_r, b_i = a_r, a_i
        for _ in range(4):
            b_r, b_i = b_r * b_r - b_i * b_i, 2.0 * (b_r * b_i)
        seg = lax.broadcasted_iota(jnp.int32, (GROUP_BLOCK, N_PS, 128), 1) % SEGS_PER_SAMPLE_SEQ
        i_r = s0_ref[:, :, 0:128]
        i_i = s0_ref[:, :, 128:256]
        for step in range(1, SEGS_PER_SAMPLE_SEQ):
            pr = jnp.where(is_fwd, pltpu.roll(i_r, 1, 1), pltpu.roll(i_r, N_PS - 1, 1))
            pi = jnp.where(is_fwd, pltpu.roll(i_i, 1, 1), pltpu.roll(i_i, N_PS - 1, 1))
            zr = jnp.where(is_fwd, pltpu.roll(z_r, 1, 1), pltpu.roll(z_r, N_PS - 1, 1))
            zi = jnp.where(is_fwd, pltpu.roll(z_i, 1, 1), pltpu.roll(z_i, N_PS - 1, 1))
            n_r = b_r * pr - b_i * pi + zr
            n_i = b_r * pi + b_i * pr + zi
            first = jnp.where(is_fwd, step, 0)
            last = jnp.where(is_fwd, SEGS_PER_SAMPLE_SEQ - 1, SEGS_PER_SAMPLE_SEQ - 1 - step)
            upd = jnp.logical_and(seg >= first, seg <= last)
            i_r = jnp.where(upd, n_r, i_r)
            i_i = jnp.where(upd, n_i, i_i)
        scan(i_r, i_i)

    for gl in range(GROUP_BLOCK):
        yt = jnp.dot(ttt_ref[gl], group_x(gl), preferred_element_type=F32)
        yt += lax.dot_general(cct_ref[gl], s_scr[gl].astype(BF16), TRANS_B, preferred_element_type=F32)
        yt_ref[:, gl * SSM_GROUP:(gl + 1) * SSM_GROUP, :] = yt.reshape(SCAN_CHUNK, SSM_GROUP, J_ROWS)


def _ssm_scan(xt, wft, cct, ttt, a16, s0, l):
    gb = GROUP_BLOCK
    op_spec = pl.BlockSpec((None, gb, CHUNK_WIDTH, CHUNK_WIDTH), lambda t, g: (l, g, 0, 0))
    io_spec = pl.BlockSpec((None, SCAN_CHUNK, gb * SSM_GROUP, J_ROWS), lambda t, g: (t, 0, g, 0))
    return pl.pallas_call(
        _ssm_kernel,
        grid=(N_TILES, N_GROUPS // gb),
        in_specs=[
            io_spec, op_spec, op_spec, op_spec,
            pl.BlockSpec((None, gb, 2, N_PS, 128), lambda t, g: (l, g, 0, 0, 0)),
            pl.BlockSpec((gb, N_PS, CHUNK_WIDTH), lambda t, g: (g, 0, 0)),
        ],
        out_specs=[io_spec, pl.BlockSpec((None, gb, N_PS, CHUNK_WIDTH), lambda t, g: (t, g, 0, 0))],
        out_shape=[
            jax.ShapeDtypeStruct((N_TILES, SCAN_CHUNK, D_SSM, J_ROWS), F32),
            jax.ShapeDtypeStruct((N_TILES, N_GROUPS, N_PS, CHUNK_WIDTH), F32),
        ],
        scratch_shapes=[pltpu.VMEM((gb, J_ROWS, CHUNK_WIDTH), F32), pltpu.VMEM((gb, J_ROWS, CHUNK_WIDTH), F32),
                        pltpu.VMEM((gb, CHUNK_WIDTH, J_ROWS), F32)],
        compiler_params=_params("arbitrary", "arbitrary"),
        name="s5_chunk_scan",
    )(xt, wft, cct, ttt, a16, s0)


def _post_kernel(x_ref, yt_ref, ug_ref, vn_ref, mod_ref, wglut_ref, ws_ref, bs_ref, wout_ref,
                 nw_ref, rwt_ref, rb_ref, hx_ref, route_ref, yg_scr, y_scr):
    tile = pl.program_id(0)
    step = pl.program_id(1)

    @pl.when(step == 0)
    def _():
        def chunk(n, carry):
            ps = n % N_PS
            c_hi = n // N_PS
            base = c_hi * (GMLP_CHUNK // SCAN_CHUNK) * N_PS + ps
            rows = [pl.ds(j * J_ROWS + base, GMLP_CHUNK // SCAN_CHUNK, stride=N_PS) for j in range(SCAN_CHUNK)]
            for h in range(GMLP_HEADS):
                v = jnp.concatenate([vn_ref[h, r, :] for r in rows], axis=0).astype(BF16)
                u = jnp.concatenate([ug_ref[h, r, :] for r in rows], axis=0)
                s = jnp.dot(ws_ref[h], v, preferred_element_type=F32) + bs_ref[h]
                yg = u * s
                for j, r in enumerate(rows):
                    yg_scr[h, r, :] = yg[j * 8:(j + 1) * 8]
            return carry

        lax.fori_loop(0, TILE_TOKENS // GMLP_CHUNK, chunk, 0)

    yt = _gelu_tanh(jnp.concatenate([yt_ref[k] for k in range(J_PER_STEP)], axis=1))
    yt = yt * _sigmoid(jnp.dot(wglut_ref[...], yt.astype(BF16), preferred_element_type=F32))
    for k in range(J_PER_STEP):
        y_scr[k * J_ROWS:(k + 1) * J_ROWS, :] = yt[:, k * J_ROWS:(k + 1) * J_ROWS].T
    row0 = pl.multiple_of(step * TOKEN_TILE, TOKEN_TILE)
    proj = jnp.dot(y_scr[...].astype(BF16), wout_ref[0:D_SSM, :], preferred_element_type=F32)
    yg = jnp.concatenate([yg_scr[h, pl.ds(row0, TOKEN_TILE), :] for h in range(GMLP_HEADS)], axis=1)
    proj += jnp.dot(yg.astype(BF16), wout_ref[D_SSM:, :], preferred_element_type=F32)
    x1 = x_ref[...] + _per_ps(lambda a, g: a * g, proj, mod_ref[2])
    h2 = _per_ps(lambda a, sc, sh: a * (1.0 + sc) + sh, _rmsnorm(x1, nw_ref[...]), mod_ref[4], mod_ref[3])
    hx_ref[:, 0:D_MODEL] = h2
    hx_ref[:, D_MODEL:2 * D_MODEL] = x1
    r_hi, r_lo = _split_bf16(rwt_ref[...])
    h_hi, h_lo = _split_bf16(h2)
    dot_t = lambda a, b: lax.dot_general(a, b, TRANS_B, preferred_element_type=F32)
    logits = dot_t(r_hi, h_hi) + (dot_t(r_hi, h_lo) + dot_t(r_lo, h_hi))
    scores = _sigmoid(logits)
    sel = scores + rb_ref[...]
    sc = [scores[e:e + 1, :] for e in range(N_EXPERTS)]
    sl = [sel[e:e + 1, :] for e in range(N_EXPERTS)]
    gscore = []
    for g in range(N_EXPERT_GROUPS):
        v0, v1, v2, v3 = sl[4 * g:4 * g + 4]
        hi01, lo01 = jnp.maximum(v0, v1), jnp.minimum(v0, v1)
        hi23, lo23 = jnp.maximum(v2, v3), jnp.minimum(v2, v3)
        top1 = jnp.maximum(hi01, hi23)
        top2 = jnp.maximum(jnp.minimum(hi01, hi23), jnp.maximum(lo01, lo23))
        gscore.append(top1 + top2)
    best = gscore[0]
    gidx = jnp.zeros_like(best, dtype=jnp.int32)
    for g in range(1, N_EXPERT_GROUPS):
        upd = gscore[g] > best
        gidx = jnp.where(upd, g, gidx)
        best = jnp.where(upd, gscore[g], best)

    def in_group(vals, k):
        out = vals[k]
        for g in range(1, N_EXPERT_GROUPS):
            out = jnp.where(gidx == g, vals[4 * g + k], out)
        return out

    v = [in_group(sl, k) for k in range(EXPERTS_PER_GROUP)]
    s = [in_group(sc, k) for k in range(EXPERTS_PER_GROUP)]
    w = []
    bits = jnp.zeros_like(gidx)
    for k in range(EXPERTS_PER_GROUP):
        rank = jnp.zeros_like(gidx)
        for j in range(EXPERTS_PER_GROUP):
            if j == k:
                continue
            ahead = (v[j] >= v[k]) if j < k else (v[j] > v[k])
            rank = rank + ahead.astype(jnp.int32)
        w.append(jnp.where(rank < 2, s[k], 0.0))
        bits = bits + jnp.where(rank < 2, 1 << k, 0)
    denom = (w[0] + w[1]) + (w[2] + w[3])
    gate = [wk / denom for wk in w]
    pair = jnp.full_like(gidx, len(PAIR_SLOT_A) - 1)
    for p in range(len(PAIR_SLOT_A) - 1):
        pair = jnp.where(bits == (1 << PAIR_SLOT_A[p]) + (1 << PAIR_SLOT_B[p]), p, pair)

    def slot_gate(table):
        out = gate[table[0]]
        for p in range(1, len(table)):
            out = jnp.where(pair == p, gate[table[p]], out)
        return out

    n_tok = route_ref.shape[1]
    route_ref[0:1, :] = (gidx * len(PAIR_SLOT_A) + pair).astype(F32)
    ps = lax.broadcasted_iota(jnp.int32, (1, n_tok), 1) % N_PS
    cond = jnp.where(tile < SAMPLE_TILE, 0, 1 + ps // SEGS_PER_SAMPLE_SEQ).astype(F32)
    lanes = jnp.concatenate([slot_gate(PAIR_SLOT_A), slot_gate(PAIR_SLOT_B), cond,
                             jnp.zeros((ROUTE_LANES - 3, n_tok), F32)], axis=0)
    hx_ref[:, 2 * D_MODEL:] = lanes.T


def _mix_out(x, yt, ug, vn, mod8, l, w_glu_t, w_s, b_s, w_out, norm2_w, rwt, rb):
    tm = TOKEN_TILE
    steps = TILE_TOKENS // tm
    tok = lambda n: pl.BlockSpec((tm, n), lambda t, s: (t * steps + s, 0))
    whole_tile = pl.BlockSpec((GMLP_HEADS, TILE_TOKENS, GMLP_HEAD_DIM), lambda t, s: (0, t, 0))
    lay = lambda *shape: pl.BlockSpec((None,) + shape, lambda t, s: (l,) + (0,) * len(shape))
    return pl.pallas_call(
        _post_kernel,
        grid=(N_TILES, steps),
        in_specs=[
            tok(D_MODEL),
            pl.BlockSpec((None, J_PER_STEP, D_SSM, J_ROWS), lambda t, s: (t, s, 0, 0)),
            whole_tile, whole_tile,
            pl.BlockSpec((None, None, N_MOD, N_PS, D_MODEL), lambda t, s: (l, t, 0, 0, 0)),
            lay(D_SSM, D_SSM), lay(GMLP_HEADS, GMLP_CHUNK, GMLP_CHUNK),
            lay(GMLP_HEADS, GMLP_CHUNK, 1), lay(D_MODEL, D_MODEL), lay(1, D_MODEL),
            pl.BlockSpec((N_EXPERTS, D_MODEL), lambda t, s: (0, 0)),
            pl.BlockSpec((N_EXPERTS, 1), lambda t, s: (0, 0)),
        ],
        out_specs=[tok(HX_WIDTH), pl.BlockSpec((1, tm), lambda t, s: (0, t * steps + s))],
        out_shape=[
            jax.ShapeDtypeStruct((T_ALL, HX_WIDTH), F32),
            jax.ShapeDtypeStruct((1, T_ALL), F32),
        ],
        scratch_shapes=[pltpu.VMEM((GMLP_HEADS, TILE_TOKENS, GMLP_HEAD_DIM), F32), pltpu.VMEM((tm, D_SSM), F32)],
        compiler_params=_params("arbitrary", "arbitrary"),
        name="mixers_out_router",
    )(x, yt, ug, vn, mod8, w_glu_t, w_s, b_s, w_out, norm2_w.reshape(DEPTH, 1, D_MODEL), rwt, rb)


def _moe_kernel(ea_ref, eb_ref, new_a_ref, new_b_ref, nv_ref, src_ref, drow_ref,
                hx_hbm, wga_ref, wua_ref, wda_ref, wgb_ref, wub_ref, wdb_ref, g2_ref, fw_ref,
                o_hbm, wga_s, wua_s, wda_s, wgb_s, wub_s, wdb_s, hx_buf, o_buf, z_buf, g_sem, s_sem, z_sem,
                *, final):
    n = pl.program_id(0)
    n_valid = nv_ref[0]
    slot = n % 2

    def start_gather(tile, sl):
        base = tile * MOE_TM
        for r in range(MOE_TM):
            pltpu.make_async_copy(hx_hbm.at[pl.ds(src_ref[base + r], 1)], hx_buf.at[sl, pl.ds(r, 1)],
                                  g_sem.at[sl]).start()

    def wait_gather(sl):
        pltpu.make_async_copy(hx_hbm.at[pl.ds(0, MOE_TM)], hx_buf.at[sl], g_sem.at[sl]).wait()

    def start_scatter(tile, sl):
        base = tile * MOE_TM
        for r in range(MOE_TM):
            pltpu.make_async_copy(o_buf.at[sl, pl.ds(r, 1)], o_hbm.at[pl.ds(drow_ref[base + r], 1)],
                                  s_sem.at[sl]).start()

    def wait_scatter(sl):
        pltpu.make_async_copy(o_buf.at[sl], o_hbm.at[pl.ds(0, MOE_TM)], s_sem.at[sl]).wait()

    def zero_fill(tile):
        return pltpu.make_async_copy(z_buf, o_hbm.at[pl.ds(tile * MOE_TM, MOE_TM)], z_sem)

    @pl.when(n == 0)
    def _():
        z_buf[...] = jnp.zeros_like(z_buf)
        start_gather(0, 0)

    @pl.when(n >= n_valid)
    def _():
        @pl.when(n > n_valid)
        def _():
            zero_fill(n - 1).wait()

        zero_fill(n).start()

        @pl.when(n == MOE_TILES - 1)
        def _():
            zero_fill(n).wait()

    @pl.when(new_a_ref[n] == 1)
    def _():
        wga_s[...] = wga_ref[...].astype(BF16)
        wua_s[...] = wua_ref[...].astype(BF16)
        wda_s[...] = wda_ref[...].astype(BF16)

    @pl.when(new_b_ref[n] == 1)
    def _():
        wgb_s[...] = wgb_ref[...].astype(BF16)
        wub_s[...] = wub_ref[...].astype(BF16)
        wdb_s[...] = wdb_ref[...].astype(BF16)

    @pl.when(n + 1 < n_valid)
    def _():
        start_gather(n + 1, 1 - slot)

    @pl.when(n < n_valid)
    def _():
        wait_gather(slot)

        @pl.when(n >= 2)
        def _():
            wait_scatter(slot)

        h = hx_buf[slot, :, 0:D_MODEL].astype(BF16)
        lanes = hx_buf[slot, :, 2 * D_MODEL:]

        def expert(wg, wu, wd, gate):
            hg = jnp.dot(h, wg[...], preferred_element_type=F32)
            hu = jnp.dot(h, wu[...], preferred_element_type=F32)
            act = hg * _sigmoid(hg) * hu * gate
            return jnp.dot(act.astype(BF16), wd[...], preferred_element_type=F32)

        y = expert(wga_s, wua_s, wda_s, lanes[:, 0:1]) + expert(wgb_s, wub_s, wdb_s, lanes[:, 1:2])
        cond_row = lanes[:, 2:3]
        gate2 = jnp.where(cond_row == 0.0, g2_ref[0:1, :], jnp.where(cond_row == 1.0, g2_ref[1:2, :], g2_ref[2:3, :]))
        x2 = hx_buf[slot, :, D_MODEL:2 * D_MODEL] + gate2 * y
        o_buf[slot] = _rmsnorm(x2, fw_ref[...]) if final else x2
        start_scatter(n, slot)

    @pl.when(n == MOE_TILES - 1)
    def _():
        wait_scatter(0)
        wait_scatter(1)


def _experts(tables, hx, mods, l, w_gate, w_up, w_down, final_w, final):
    w_a = lambda r, c: pl.BlockSpec((None, None, r, c), lambda n, ea, eb, *_: (l, ea[n], 0, 0))
    w_b = lambda r, c: pl.BlockSpec((None, None, r, c), lambda n, ea, eb, *_: (l, eb[n], 0, 0))
    up = pltpu.VMEM((D_MODEL, D_EXPERT), BF16)
    down = pltpu.VMEM((D_EXPERT, D_MODEL), BF16)
    return pl.pallas_call(
        functools.partial(_moe_kernel, final=final),
        grid_spec=pltpu.PrefetchScalarGridSpec(
            num_scalar_prefetch=len(tables),
            grid=(MOE_TILES,),
            in_specs=[
                pl.BlockSpec(memory_space=pl.ANY),
                w_a(D_MODEL, D_EXPERT), w_a(D_MODEL, D_EXPERT), w_a(D_EXPERT, D_MODEL),
                w_b(D_MODEL, D_EXPERT), w_b(D_MODEL, D_EXPERT), w_b(D_EXPERT, D_MODEL),
                pl.BlockSpec((None, None, MOD_ROWS, D_MODEL), lambda n, *_: (l, N_MOD - 1, 0, 0)),
                pl.BlockSpec((1, D_MODEL), lambda n, *_: (0, 0)),
            ],
            out_specs=pl.BlockSpec(memory_space=pl.ANY),
            scratch_shapes=[up, up, down, up, up, down,
                            pltpu.VMEM((2, MOE_TM, HX_WIDTH), F32), pltpu.VMEM((2, MOE_TM, D_MODEL), F32),
                            pltpu.VMEM((MOE_TM, D_MODEL), F32),
                            pltpu.SemaphoreType.DMA((2,)), pltpu.SemaphoreType.DMA((2,)), pltpu.SemaphoreType.DMA(())],
        ),
        out_shape=jax.ShapeDtypeStruct((MOE_ROWS, D_MODEL), F32),
        compiler_params=_params("arbitrary"),
        name="experts",
    )(*tables, hx, w_gate, w_up, w_down, w_gate, w_up, w_down, mods, final_w.reshape(1, D_MODEL))


def _count_before(flags):
    n, k = flags.shape
    blocks = flags.reshape(n // 128, 128, k).astype(F32)
    strictly_lower = jnp.tril(jnp.ones((128, 128), F32), -1)
    within = jnp.einsum("ij,bjk->bik", strictly_lower, blocks)
    totals = jnp.sum(blocks, axis=1)
    before = jnp.cumsum(totals, axis=0) - totals
    return (within + before[:, None, :]).reshape(n, k).astype(jnp.int32)


def _routing_tables(cls, to_sequence_order):
    n_cls = N_EXPERT_GROUPS * len(PAIR_SLOT_A)
    onehot = (cls[:, None] == jnp.arange(n_cls, dtype=jnp.int32)[None, :]).astype(jnp.int32)
    counts = jnp.sum(onehot, axis=0)
    tiles = (counts + MOE_TM - 1) // MOE_TM
    tile_end = jnp.cumsum(tiles)
    row_in_class = (tile_end - tiles)[None, :] * MOE_TM + _count_before(onehot)
    dst = jnp.sum(onehot * row_in_class, axis=1)
    token_plus_1 = jnp.zeros((MOE_ROWS,), jnp.int32).at[dst].set(jnp.arange(1, T_ALL + 1, dtype=jnp.int32))
    is_pad = token_plus_1 == 0
    src = jnp.maximum(token_plus_1 - 1, 0)
    spare = T_ALL + _count_before(is_pad.astype(jnp.int32)[:, None])[:, 0]
    if to_sequence_order:
        j, c, ps = (src // J_ROWS) % SCAN_CHUNK, (src // N_PS) % SEG_CHUNKS, src % N_PS
        target = (src // TILE_TOKENS) * TILE_TOKENS + ps * SEG_LEN + c * SCAN_CHUNK + j
    else:
        target = src
    drow = jnp.where(is_pad, spare, target).astype(jnp.int32)
    n_valid = tile_end[-1]
    tile_id = jnp.minimum(jnp.arange(MOE_TILES, dtype=jnp.int32), n_valid - 1)
    tile_cls = jnp.sum((tile_end[None, :] <= tile_id[:, None]).astype(jnp.int32), axis=1)
    group, pair = tile_cls // len(PAIR_SLOT_A), tile_cls % len(PAIR_SLOT_A)

    def slot_expert(table):
        local = jnp.full_like(pair, table[-1])
        for p in range(len(table) - 1):
            local = jnp.where(pair == p, table[p], local)
        return group * EXPERTS_PER_GROUP + local

    e_a = slot_expert(PAIR_SLOT_A)
    e_b = slot_expert(PAIR_SLOT_B)
    first = jnp.ones((1,), jnp.int32)
    new_a = jnp.concatenate([first, (e_a[1:] != e_a[:-1]).astype(jnp.int32)])
    new_b = jnp.concatenate([first, (e_b[1:] != e_b[:-1]).astype(jnp.int32)])
    return e_a, e_b, new_a, new_b, n_valid.reshape(1).astype(jnp.int32), src, drow


def _dirs_on_lanes(p):
    p = jnp.moveaxis(p, 1, -2)
    return p.reshape(p.shape[:-2] + (2 * SSM_STATE,))


def _to_internal_order(x):
    x = x.reshape(N_TILES, N_PS, SEG_CHUNKS, SCAN_CHUNK, D_MODEL)
    return x.transpose(0, 3, 2, 1, 4).reshape(T_ALL, D_MODEL)


def _gmlp_position_order(w):
    n_lo = GMLP_CHUNK // SCAN_CHUNK
    lead = w.shape[:2]
    w = w.reshape(lead + (n_lo, SCAN_CHUNK) + w.shape[3:])
    w = jnp.swapaxes(w, 2, 3)
    return w.reshape(lead + (GMLP_CHUNK,) + w.shape[4:])


def kernel(x_prompt, x_sample, c, state_ssm_re, state_ssm_im, c_ctx, norm1_w, norm2_w, w_mod, b_mod, w_in,
           ssm_a_re, ssm_a_im, ssm_log_dt, ssm_b_re, ssm_b_im, ssm_c_re, ssm_c_im, ssm_d, w_glu,
           gmlp_ln_w, gmlp_ln_b, gmlp_w_s, gmlp_b_s, w_out, router_w, router_b, w_gate, w_up, w_down,
           final_norm_w):
    x = jnp.concatenate([x_prompt.reshape(T_PROMPT, D_MODEL), x_sample.reshape(T_SAMPLE, D_MODEL)], axis=0)
    x = _to_internal_order(x)

    cvec = jnp.concatenate([c_ctx[None, :], c, jnp.zeros((MOD_ROWS - 1 - N_SAMPLE_SEQ, D_MODEL), F32)], axis=0)
    mods = _modulation(cvec, w_mod, b_mod)
    ps_row = [[0] * N_PS] * SAMPLE_TILE + [[1 + p // SEGS_PER_SAMPLE_SEQ for p in range(N_PS)]]
    mod8 = mods[:, :, jnp.array(ps_row, jnp.int32), :].transpose(0, 2, 1, 3, 4)

    log_dt = jnp.broadcast_to(ssm_log_dt[..., None], ssm_a_re.shape)
    d_lanes = jnp.tile(ssm_d.reshape(DEPTH, N_GROUPS, SSM_GROUP), (1, 1, 128 // SSM_GROUP))
    vecs = jnp.stack([_dirs_on_lanes(ssm_a_re), _dirs_on_lanes(ssm_a_im), _dirs_on_lanes(log_dt), d_lanes], axis=2)
    vecs = jnp.concatenate([vecs, jnp.zeros((DEPTH, N_GROUPS, 4, 2 * SSM_STATE), F32)], axis=2)
    mats = jnp.stack([_dirs_on_lanes(jnp.swapaxes(ssm_b_re, -1, -2)), _dirs_on_lanes(jnp.swapaxes(ssm_b_im, -1, -2)),
                      _dirs_on_lanes(ssm_c_re), _dirs_on_lanes(ssm_c_im)], axis=2)
    wft, cct, ttt, a16 = _ssm_operators(vecs, mats)

    w_in_g = w_in[:, :, D_SSM:].astype(BF16)
    w_in_st = jnp.swapaxes(w_in[:, :, :D_SSM], 1, 2).astype(BF16)
    w_glu_t = jnp.swapaxes(w_glu, 1, 2).astype(BF16)
    w_s = jnp.swapaxes(_gmlp_position_order(jnp.swapaxes(_gmlp_position_order(gmlp_w_s), 2, 3)), 2, 3).astype(BF16)
    w_out = w_out.astype(BF16)
    b_s = _gmlp_position_order(gmlp_b_s)[..., None]
    rwt = router_w.T
    rb = router_b.reshape(N_EXPERTS, 1)

    new_re, new_im = [], []
    for l in range(DEPTH):
        ug, vn, xt = _input_proj(x, mod8, l, norm1_w, w_in_g, w_in_st, gmlp_ln_w, gmlp_ln_b)
        s0 = jnp.concatenate([state_ssm_re[:, l].transpose(2, 0, 1, 3).reshape(N_GROUPS, N_SAMPLE_SEQ, 128),
                              state_ssm_im[:, l].transpose(2, 0, 1, 3).reshape(N_GROUPS, N_SAMPLE_SEQ, 128)], axis=-1)
        s0 = jnp.repeat(s0, SEGS_PER_SAMPLE_SEQ, axis=1)
        yt, fin = _ssm_scan(xt, wft, cct, ttt, a16, s0, l)
        fin = fin[:SAMPLE_TILE].reshape(SAMPLE_TILE, N_GROUPS, N_PS, 2, 2, SSM_STATE)
        fin = fin.transpose(3, 0, 2, 4, 1, 5).reshape(2, N_PROMPT_SEQ, 2, N_GROUPS, SSM_STATE)
        new_re.append(fin[0])
        new_im.append(fin[1])
        hx, route = _mix_out(x, yt, ug, vn, mod8, l, w_glu_t, w_s, b_s, w_out, norm2_w, rwt, rb)
        final = l == DEPTH - 1
        tables = _routing_tables(route[0].astype(jnp.int32), to_sequence_order=final)
        x = _experts(tables, hx, mods, l, w_gate, w_up, w_down, final_norm_w, final)

    y_prompt = x[:T_PROMPT].reshape(N_PROMPT_SEQ, PROMPT_LEN, D_MODEL)
    y_sample = x[T_PROMPT:T_ALL].reshape(N_SAMPLE_SEQ, SAMPLE_LEN, D_MODEL)
    return (y_prompt, y_sample, jnp.stack(new_re, axis=1), jnp.stack(new_im, axis=1))
```

```python
import functools
import math

import jax
import jax.numpy as jnp
from jax import lax
from jax.experimental import pallas as pl
from jax.experimental.pallas import tpu as pltpu

F32 = jnp.float32
BF16 = jnp.bfloat16

D_MODEL = 1024
N_PROMPT_SEQ = 16
PROMPT_LEN = 256
N_SAMPLE_SEQ = 2
SAMPLE_LEN = 1024
T_PROMPT = N_PROMPT_SEQ * PROMPT_LEN
T_SAMPLE = N_SAMPLE_SEQ * SAMPLE_LEN
T_ALL = T_PROMPT + T_SAMPLE
DEPTH = 2
D_SSM = 512
SSM_GROUP = 16
N_GROUPS = 32
SSM_STATE = 64
D_GMLP = 512
GMLP_HEADS = 4
GMLP_HEAD_DIM = 128
GMLP_CHUNK = 128
N_EXPERTS = 16
N_EXPERT_GROUPS = 4
EXPERTS_PER_GROUP = 4
D_EXPERT = 512
N_MOD = 6
EPS = 1e-6

SCAN_CHUNK = 16
CHUNK_WIDTH = SCAN_CHUNK * SSM_GROUP
SEG_LEN = 256
SEG_CHUNKS = SEG_LEN // SCAN_CHUNK
N_PS = 8
TILE_TOKENS = N_PS * SEG_LEN
N_TILES = T_ALL // TILE_TOKENS
SAMPLE_TILE = T_PROMPT // TILE_TOKENS
SEGS_PER_SAMPLE_SEQ = SAMPLE_LEN // SEG_LEN
J_ROWS = SEG_CHUNKS * N_PS
GROUP_BLOCK = 8
PREP_GROUPS = 4
MOD_ROWS = 8
MODS_PER_STEP = 2

J_PER_STEP = 4
TOKEN_TILE = J_PER_STEP * J_ROWS
PAIR_SLOT_A = (0, 0, 0, 1, 1, 3)
PAIR_SLOT_B = (1, 2, 3, 3, 2, 2)
MOE_TM = 128
MOE_TILES = T_ALL // MOE_TM + N_EXPERT_GROUPS * len(PAIR_SLOT_A)
MOE_ROWS = MOE_TILES * MOE_TM
ROUTE_LANES = 128
HX_WIDTH = 2 * D_MODEL + ROUTE_LANES
VMEM_LIMIT = 56 * 1024 * 1024
TRANS_B = (((1,), (1,)), ((), ()))


def _sigmoid(x):
    return 1.0 / (1.0 + jnp.exp(-x))


def _gelu_tanh(x):
    c = math.sqrt(2.0 / math.pi)
    return x * (0.5 * (1.0 + jnp.tanh(c * (x + 0.044715 * (x * x * x)))))


def _split_bf16(a):
    hi = a.astype(BF16)
    return hi, (a - hi.astype(F32)).astype(BF16)


def _rmsnorm(x, w):
    return x * lax.rsqrt(jnp.mean(x * x, axis=-1, keepdims=True) + EPS) * w


def _per_ps(fn, a, *mods):
    rows, d = a.shape
    out = fn(a.reshape(rows // N_PS, N_PS, d), *[m[None] for m in mods])
    return out.reshape(rows, d)


def _params(*sem):
    return pltpu.CompilerParams(dimension_semantics=sem, vmem_limit_bytes=VMEM_LIMIT)


def _mod_kernel(c_ref, w_ref, b_ref, o_ref):
    c = c_ref[...]
    s = c * _sigmoid(c)
    res = jnp.dot(s.astype(BF16), w_ref[...].astype(BF16), preferred_element_type=F32)
    for m in range(MODS_PER_STEP):
        o_ref[m] = res[:, m * D_MODEL:(m + 1) * D_MODEL] + b_ref[m]


def _modulation(cvec, w_mod, b_mod):
    mp = MODS_PER_STEP
    return pl.pallas_call(
        _mod_kernel,
        grid=(DEPTH, N_MOD // mp),
        in_specs=[
            pl.BlockSpec((MOD_ROWS, D_MODEL), lambda l, n: (0, 0)),
            pl.BlockSpec((None, D_MODEL, mp * D_MODEL), lambda l, n: (l, 0, n)),
            pl.BlockSpec((None, mp, 1, D_MODEL), lambda l, n: (l, n, 0, 0)),
        ],
        out_specs=pl.BlockSpec((None, mp, MOD_ROWS, D_MODEL), lambda l, n: (l, n, 0, 0)),
        out_shape=jax.ShapeDtypeStruct((DEPTH, N_MOD, MOD_ROWS, D_MODEL), F32),
        compiler_params=_params("arbitrary", "arbitrary"),
        name="adaln_mod",
    )(cvec, w_mod, b_mod.reshape(DEPTH, N_MOD, 1, D_MODEL))


def _in_kernel(x_ref, mod_ref, nw_ref, wg_ref, wst_ref, lnw_ref, lnb_ref, ug_ref, vn_ref, xt_ref):
    y = _rmsnorm(x_ref[...], nw_ref[...])
    h = _per_ps(lambda a, sc, sh: a * (1.0 + sc) + sh, y, mod_ref[1], mod_ref[0]).astype(BF16)
    zg = _gelu_tanh(jnp.dot(h, wg_ref[...], preferred_element_type=F32))
    v = zg[:, D_GMLP:]
    mu = jnp.mean(v, axis=-1, keepdims=True)
    vc = v - mu
    var = jnp.mean(vc * vc, axis=-1, keepdims=True)
    vn = vc * lax.rsqrt(var + EPS) * lnw_ref[...] + lnb_ref[...]
    for hd in range(GMLP_HEADS):
        cols = slice(hd * GMLP_HEAD_DIM, (hd + 1) * GMLP_HEAD_DIM)
        ug_ref[hd] = zg[:, cols]
        vn_ref[hd] = vn[:, cols]
    xt = lax.dot_general(wst_ref[...], h, TRANS_B, preferred_element_type=F32).astype(BF16)
    for k in range(J_PER_STEP):
        xt_ref[k] = xt[:, k * J_ROWS:(k + 1) * J_ROWS]


def _input_proj(x, mod8, l, norm1_w, w_in_g, w_in_st, ln_w, ln_b):
    tm = TOKEN_TILE
    steps = TILE_TOKENS // tm
    tok = lambda n: pl.BlockSpec((tm, n), lambda t, s: (t * steps + s, 0))
    heads = pl.BlockSpec((GMLP_HEADS, tm, GMLP_HEAD_DIM), lambda t, s: (0, t * steps + s, 0))
    lay = lambda *shape: pl.BlockSpec((None,) + shape, lambda t, s: (l,) + (0,) * len(shape))
    return pl.pallas_call(
        _in_kernel,
        grid=(N_TILES, steps),
        in_specs=[
            tok(D_MODEL),
            pl.BlockSpec((None, None, N_MOD, N_PS, D_MODEL), lambda t, s: (l, t, 0, 0, 0)),
            lay(1, D_MODEL), lay(D_MODEL, 2 * D_GMLP), lay(D_SSM, D_MODEL), lay(1, D_GMLP), lay(1, D_GMLP),
        ],
        out_specs=[heads, heads,
                   pl.BlockSpec((None, J_PER_STEP, D_SSM, J_ROWS), lambda t, s: (t, s, 0, 0))],
        out_shape=[jax.ShapeDtypeStruct((GMLP_HEADS, T_ALL, GMLP_HEAD_DIM), F32)] * 2 + [
            jax.ShapeDtypeStruct((N_TILES, SCAN_CHUNK, D_SSM, J_ROWS), BF16)],
        compiler_params=_params("arbitrary", "arbitrary"),
        name="norm1_in_proj",
    )(x, mod8, norm1_w.reshape(DEPTH, 1, D_MODEL), w_in_g, w_in_st,
      ln_w.reshape(DEPTH, 1, D_GMLP), ln_b.reshape(DEPTH, 1, D_GMLP))


def _shift_lanes_right(a, b, s, lane):
    if s == 0:
        return a, b
    if s == 128:
        return jnp.zeros_like(a), a
    if s < 128:
        ra = pltpu.roll(a, s, 1)
        rb = pltpu.roll(b, s, 1)
        return jnp.where(lane >= s, ra, 0.0), jnp.where(lane >= s, rb, ra)
    t = s - 128
    return jnp.zeros_like(a), jnp.where(lane >= t, pltpu.roll(a, t, 1), 0.0)


def _shift_lanes_left(a, b, s, lane):
    if s == 0:
        return a, b
    if s == 128:
        return b, jnp.zeros_like(b)
    if s < 128:
        ra = pltpu.roll(a, 128 - s, 1)
        rb = pltpu.roll(b, 128 - s, 1)
        return jnp.where(lane < 128 - s, ra, rb), jnp.where(lane < 128 - s, rb, 0.0)
    t = s - 128
    return jnp.where(lane < 128 - t, pltpu.roll(b, 128 - t, 1), 0.0), jnp.zeros_like(b)


def _prep_kernel(*refs):
    for g in range(PREP_GROUPS):
        _prep_group(*[r.at[g] for r in refs])


def _prep_group(vec_ref, mat_ref, wft_ref, cct_ref, ttt_ref, a16_ref, wf_scr, cm_scr, tt_scr):
    a_re = vec_ref[0:1, :]
    a_im = vec_ref[1:2, :]
    dt = jnp.exp(vec_ref[2:3, :])
    d_skip = vec_ref[3:4, :]
    mag = jnp.exp(a_re * dt)
    ang = a_im * dt
    ab_r = mag * jnp.cos(ang)
    ab_i = mag * jnp.sin(ang)
    den = a_re * a_re + a_im * a_im
    nr = ab_r - 1.0
    q_r = (nr * a_re + ab_i * a_im) / den
    q_i = (ab_i * a_re - nr * a_im) / den
    bt_r = mat_ref[0]
    bt_i = mat_ref[1]
    c_r = mat_ref[2]
    c_i = mat_ref[3]
    bb_r = q_r * bt_r - q_i * bt_i
    bb_i = q_r * bt_i + q_i * bt_r
    p_r = [jnp.ones_like(ab_r)]
    p_i = [jnp.zeros_like(ab_r)]
    for _ in range(SCAN_CHUNK):
        pr, pi = p_r[-1], p_i[-1]
        p_r.append(pr * ab_r - pi * ab_i)
        p_i.append(pr * ab_i + pi * ab_r)
    a16_ref[0] = jnp.broadcast_to(p_r[SCAN_CHUNK], (N_PS, 128))
    a16_ref[1] = jnp.broadcast_to(p_i[SCAN_CHUNK], (N_PS, 128))

    lane = lax.broadcasted_iota(jnp.int32, (1, 128), 1)
    is_fwd = lane < SSM_STATE

    def pick(mf, mb):
        return jnp.where(is_fwd, p_r[mf], p_r[mb]), jnp.where(is_fwd, p_i[mf], p_i[mb])

    for j in range(SCAN_CHUNK):
        rows = slice(j * SSM_GROUP, (j + 1) * SSM_GROUP)
        wr, wi = pick(SCAN_CHUNK - 1 - j, j)
        wf_scr[rows, 0:128] = bb_r * wr - bb_i * wi
        wf_scr[rows, 128:256] = bb_r * wi + bb_i * wr
        wr, wi = pick(j + 1, SCAN_CHUNK - j)
        cct_ref[rows, 0:128] = (c_r * wr - c_i * wi).astype(BF16)
        cct_ref[rows, 128:256] = (-(c_r * wi + c_i * wr)).astype(BF16)
        wr, wi = pick(j, SCAN_CHUNK - 1 - j)
        cm_scr[rows, 0:128] = c_r * wr - c_i * wi
        cm_scr[rows, 128:256] = c_r * wi + c_i * wr
    wft_ref[...] = wf_scr[...].T.astype(BF16)

    zero = jnp.zeros_like(bb_r)
    cm_hi, cm_lo = _split_bf16(cm_scr[...])
    dot_t = lambda a, b: lax.dot_general(a, b, TRANS_B, preferred_element_type=F32)

    def lag_rows(keep):
        lhs = jnp.concatenate([jnp.where(keep, bb_r, zero), jnp.where(keep, -bb_i, zero)], axis=1)
        hi, lo = _split_bf16(lhs)
        return dot_t(hi, cm_hi) + (dot_t(hi, cm_lo) + dot_t(lo, cm_hi))

    mf = lag_rows(is_fwd)
    mb = lag_rows(jnp.logical_not(is_fwd))
    mf_a, mf_b = mf[:, 0:128], mf[:, 128:256]
    mb_a, mb_b = mb[:, 0:128], mb[:, 128:256]
    row_h = lax.broadcasted_iota(jnp.int32, (SSM_GROUP, 128), 0)
    lane_h = lax.broadcasted_iota(jnp.int32, (SSM_GROUP, 128), 1)
    for jp in range(SCAN_CHUNK):
        rows = slice(jp * SSM_GROUP, (jp + 1) * SSM_GROUP)
        fa, fb = _shift_lanes_right(mf_a, mf_b, SSM_GROUP * jp, lane)
        ba, bb = _shift_lanes_left(mb_a, mb_b, SSM_GROUP * (SCAN_CHUNK - 1 - jp), lane)
        diag = SSM_GROUP * jp + row_h
        tt_scr[rows, 0:128] = fa + ba + jnp.where(lane_h == diag, d_skip, 0.0)
        tt_scr[rows, 128:256] = fb + bb + jnp.where(lane_h + 128 == diag, d_skip, 0.0)
    ttt_ref[...] = tt_scr[...].T.astype(BF16)


def _ssm_operators(vecs, mats):
    op = jax.ShapeDtypeStruct((DEPTH, N_GROUPS, CHUNK_WIDTH, CHUNK_WIDTH), BF16)
    pg = PREP_GROUPS
    op_spec = pl.BlockSpec((None, pg, CHUNK_WIDTH, CHUNK_WIDTH), lambda l, g: (l, g, 0, 0))
    sq = pltpu.VMEM((pg, CHUNK_WIDTH, CHUNK_WIDTH), F32)
    return pl.pallas_call(
        _prep_kernel,
        grid=(DEPTH, N_GROUPS // pg),
        in_specs=[
            pl.BlockSpec((None, pg, 8, 128), lambda l, g: (l, g, 0, 0)),
            pl.BlockSpec((None, pg, 4, SSM_GROUP, 128), lambda l, g: (l, g, 0, 0, 0)),
        ],
        out_specs=[op_spec, op_spec, op_spec,
                   pl.BlockSpec((None, pg, 2, N_PS, 128), lambda l, g: (l, g, 0, 0, 0))],
        out_shape=[op, op, op, jax.ShapeDtypeStruct((DEPTH, N_GROUPS, 2, N_PS, 128), F32)],
        scratch_shapes=[sq, sq, sq],
        compiler_params=_params("arbitrary", "arbitrary"),
        name="s5_operators",
    )(vecs, mats)


def _ssm_kernel(xt_ref, wft_ref, cct_ref, ttt_ref, a16_ref, s0_ref, yt_ref, fin_ref, s_scr, f_scr, ft_scr):
    tile = pl.program_id(0)
    lane = lax.broadcasted_iota(jnp.int32, (GROUP_BLOCK, N_PS, 128), 2)
    is_fwd = lane < SSM_STATE
    half = SSM_STATE

    def group_x(gl):
        return xt_ref[:, gl * SSM_GROUP:(gl + 1) * SSM_GROUP, :].reshape(CHUNK_WIDTH, J_ROWS)

    for gl in range(GROUP_BLOCK):
        ft_scr[gl] = jnp.dot(wft_ref[gl], group_x(gl), preferred_element_type=F32)
        f_scr[gl] = ft_scr[gl].T

    a_r = a16_ref[:, 0]
    a_i = a16_ref[:, 1]

    def scan(s_r, s_i):
        for i in range(SEG_CHUNKS):
            rf = slice(i * N_PS, (i + 1) * N_PS)
            rb = slice((SEG_CHUNKS - 1 - i) * N_PS, (SEG_CHUNKS - i) * N_PS)
            s_scr[:, rf, 0:half] = s_r[:, :, 0:half]
            s_scr[:, rf, 128:128 + half] = s_i[:, :, 0:half]
            s_scr[:, rb, half:128] = s_r[:, :, half:128]
            s_scr[:, rb, 128 + half:256] = s_i[:, :, half:128]
            f_r = jnp.where(is_fwd, f_scr[:, rf, 0:128], f_scr[:, rb, 0:128])
            f_i = jnp.where(is_fwd, f_scr[:, rf, 128:256], f_scr[:, rb, 128:256])
            s_r, s_i = a_r * s_r - a_i * s_i + f_r, a_r * s_i + a_i * s_r + f_i
        return s_r, s_i

    zeros = jnp.zeros((GROUP_BLOCK, N_PS, 128), F32)
    z_r, z_i = scan(zeros, zeros)
    fin_ref[:, :, 0:128] = z_r
    fin_ref[:, :, 128:256] = z_i

    @pl.when(tile == SAMPLE_TILE)
    def _():
        b_r, b_i = a_r, a_i
        for _ in range(4):
            b_r, b_i = b_r * b_r - b_i * b_i, 2.0 * (b_r * b_i)
        seg = lax.broadcasted_iota(jnp.int32, (GROUP_BLOCK, N_PS, 128), 1) % SEGS_PER_SAMPLE_SEQ
        i_r = s0_ref[:, :, 0:128]
        i_i = s0_ref[:, :, 128:256]
        for step in range(1, SEGS_PER_SAMPLE_SEQ):
            pr = jnp.where(is_fwd, pltpu.roll(i_r, 1, 1), pltpu.roll(i_r, N_PS - 1, 1))
            pi = jnp.where(is_fwd, pltpu.roll(i_i, 1, 1), pltpu.roll(i_i, N_PS - 1, 1))
            zr = jnp.where(is_fwd, pltpu.roll(z_r, 1, 1), pltpu.roll(z_r, N_PS - 1, 1))
            zi = jnp.where(is_fwd, pltpu.roll(z_i, 1, 1), pltpu.roll(z_i, N_PS - 1, 1))
            n_r = b_r * pr - b_i * pi + zr
            n_i = b_r * pi + b_i * pr + zi
            first = jnp.where(is_fwd, step, 0)
            last = jnp.where(is_fwd, SEGS_PER_SAMPLE_SEQ - 1, SEGS_PER_SAMPLE_SEQ - 1 - step)
            upd = jnp.logical_and(seg >= first, seg <= last)
            i_r = jnp.where(upd, n_r, i_r)
            i_i = jnp.where(upd, n_i, i_i)
        scan(i_r, i_i)

    for gl in range(GROUP_BLOCK):
        yt = jnp.dot(ttt_ref[gl], group_x(gl), preferred_element_type=F32)
        yt += lax.dot_general(cct_ref[gl], s_scr[gl].astype(BF16), TRANS_B, preferred_element_type=F32)
        yt_ref[:, gl * SSM_GROUP:(gl + 1) * SSM_GROUP, :] = yt.reshape(SCAN_CHUNK, SSM_GROUP, J_ROWS)


def _ssm_scan(xt, wft, cct, ttt, a16, s0, l):
    gb = GROUP_BLOCK
    op_spec = pl.BlockSpec((None, gb, CHUNK_WIDTH, CHUNK_WIDTH), lambda t, g: (l, g, 0, 0))
    io_spec = pl.BlockSpec((None, SCAN_CHUNK, gb * SSM_GROUP, J_ROWS), lambda t, g: (t, 0, g, 0))
    return pl.pallas_call(
        _ssm_kernel,
        grid=(N_TILES, N_GROUPS // gb),
        in_specs=[
            io_spec, op_spec, op_spec, op_spec,
            pl.BlockSpec((None, gb, 2, N_PS, 128), lambda t, g: (l, g, 0, 0, 0)),
            pl.BlockSpec((gb, N_PS, CHUNK_WIDTH), lambda t, g: (g, 0, 0)),
        ],
        out_specs=[io_spec, pl.BlockSpec((None, gb, N_PS, CHUNK_WIDTH), lambda t, g: (t, g, 0, 0))],
        out_shape=[
            jax.ShapeDtypeStruct((N_TILES, SCAN_CHUNK, D_SSM, J_ROWS), F32),
            jax.ShapeDtypeStruct((N_TILES, N_GROUPS, N_PS, CHUNK_WIDTH), F32),
        ],
        scratch_shapes=[pltpu.VMEM((gb, J_ROWS, CHUNK_WIDTH), F32), pltpu.VMEM((gb, J_ROWS, CHUNK_WIDTH), F32),
                        pltpu.VMEM((gb, CHUNK_WIDTH, J_ROWS), F32)],
        compiler_params=_params("arbitrary", "arbitrary"),
        name="s5_chunk_scan",
    )(xt, wft, cct, ttt, a16, s0)


def _post_kernel(x_ref, yt_ref, ug_ref, vn_ref, mod_ref, wglut_ref, ws_ref, bs_ref, wout_ref,
                 nw_ref, rwt_ref, rb_ref, hx_ref, route_ref, yg_scr, y_scr):
    tile = pl.program_id(0)
    step = pl.program_id(1)

    @pl.when(step == 0)
    def _():
        def chunk(n, carry):
            ps = n % N_PS
            c_hi = n // N_PS
            base = c_hi * (GMLP_CHUNK // SCAN_CHUNK) * N_PS + ps
            rows = [pl.ds(j * J_ROWS + base, GMLP_CHUNK // SCAN_CHUNK, stride=N_PS) for j in range(SCAN_CHUNK)]
            for h in range(GMLP_HEADS):
                v = jnp.concatenate([vn_ref[h, r, :] for r in rows], axis=0).astype(BF16)
                u = jnp.concatenate([ug_ref[h, r, :] for r in rows], axis=0)
                s = jnp.dot(ws_ref[h], v, preferred_element_type=F32) + bs_ref[h]
                yg = u * s
                for j, r in enumerate(rows):
                    yg_scr[h, r, :] = yg[j * 8:(j + 1) * 8]
            return carry

        lax.fori_loop(0, TILE_TOKENS // GMLP_CHUNK, chunk, 0)

    yt = _gelu_tanh(jnp.concatenate([yt_ref[k] for k in range(J_PER_STEP)], axis=1))
    yt = yt * _sigmoid(jnp.dot(wglut_ref[...], yt.astype(BF16), preferred_element_type=F32))
    for k in range(J_PER_STEP):
        y_scr[k * J_ROWS:(k + 1) * J_ROWS, :] = yt[:, k * J_ROWS:(k + 1) * J_ROWS].T
    row0 = pl.multiple_of(step * TOKEN_TILE, TOKEN_TILE)
    proj = jnp.dot(y_scr[...].astype(BF16), wout_ref[0:D_SSM, :], preferred_element_type=F32)
    yg = jnp.concatenate([yg_scr[h, pl.ds(row0, TOKEN_TILE), :] for h in range(GMLP_HEADS)], axis=1)
    proj += jnp.dot(yg.astype(BF16), wout_ref[D_SSM:, :], preferred_element_type=F32)
    x1 = x_ref[...] + _per_ps(lambda a, g: a * g, proj, mod_ref[2])
    h2 = _per_ps(lambda a, sc, sh: a * (1.0 + sc) + sh, _rmsnorm(x1, nw_ref[...]), mod_ref[4], mod_ref[3])
    hx_ref[:, 0:D_MODEL] = h2
    hx_ref[:, D_MODEL:2 * D_MODEL] = x1
    r_hi, r_lo = _split_bf16(rwt_ref[...])
    h_hi, h_lo = _split_bf16(h2)
    dot_t = lambda a, b: lax.dot_general(a, b, TRANS_B, preferred_element_type=F32)
    logits = dot_t(r_hi, h_hi) + (dot_t(r_hi, h_lo) + dot_t(r_lo, h_hi))
    scores = _sigmoid(logits)
    sel = scores + rb_ref[...]
    sc = [scores[e:e + 1, :] for e in range(N_EXPERTS)]
    sl = [sel[e:e + 1, :] for e in range(N_EXPERTS)]
    gscore = []
    for g in range(N_EXPERT_GROUPS):
        v0, v1, v2, v3 = sl[4 * g:4 * g + 4]
        hi01, lo01 = jnp.maximum(v0, v1), jnp.minimum(v0, v1)
        hi23, lo23 = jnp.maximum(v2, v3), jnp.minimum(v2, v3)
        top1 = jnp.maximum(hi01, hi23)
        top2 = jnp.maximum(jnp.minimum(hi01, hi23), jnp.maximum(lo01, lo23))
        gscore.append(top1 + top2)
    best = gscore[0]
    gidx = jnp.zeros_like(best, dtype=jnp.int32)
    for g in range(1, N_EXPERT_GROUPS):
        upd = gscore[g] > best
        gidx = jnp.where(upd, g, gidx)
        best = jnp.where(upd, gscore[g], best)

    def in_group(vals, k):
        out = vals[k]
        for g in range(1, N_EXPERT_GROUPS):
            out = jnp.where(gidx == g, vals[4 * g + k], out)
        return out

    v = [in_group(sl, k) for k in range(EXPERTS_PER_GROUP)]
    s = [in_group(sc, k) for k in range(EXPERTS_PER_GROUP)]
    w = []
    bits = jnp.zeros_like(gidx)
    for k in range(EXPERTS_PER_GROUP):
        rank = jnp.zeros_like(gidx)
        for j in range(EXPERTS_PER_GROUP):
            if j == k:
                continue
            ahead = (v[j] >= v[k]) if j < k else (v[j] > v[k])
            rank = rank + ahead.astype(jnp.int32)
        w.append(jnp.where(rank < 2, s[k], 0.0))
        bits = bits + jnp.where(rank < 2, 1 << k, 0)
    denom = (w[0] + w[1]) + (w[2] + w[3])
    gate = [wk / denom for wk in w]
    pair = jnp.full_like(gidx, len(PAIR_SLOT_A) - 1)
    for p in range(len(PAIR_SLOT_A) - 1):
        pair = jnp.where(bits == (1 << PAIR_SLOT_A[p]) + (1 << PAIR_SLOT_B[p]), p, pair)

    def slot_gate(table):
        out = gate[table[0]]
        for p in range(1, len(table)):
            out = jnp.where(pair == p, gate[table[p]], out)
        return out

    n_tok = route_ref.shape[1]
    route_ref[0:1, :] = (gidx * len(PAIR_SLOT_A) + pair).astype(F32)
    ps = lax.broadcasted_iota(jnp.int32, (1, n_tok), 1) % N_PS
    cond = jnp.where(tile < SAMPLE_TILE, 0, 1 + ps // SEGS_PER_SAMPLE_SEQ).astype(F32)
    lanes = jnp.concatenate([slot_gate(PAIR_SLOT_A), slot_gate(PAIR_SLOT_B), cond,
                             jnp.zeros((ROUTE_LANES - 3, n_tok), F32)], axis=0)
    hx_ref[:, 2 * D_MODEL:] = lanes.T


def _mix_out(x, yt, ug, vn, mod8, l, w_glu_t, w_s, b_s, w_out, norm2_w, rwt, rb):
    tm = TOKEN_TILE
    steps = TILE_TOKENS // tm
    tok = lambda n: pl.BlockSpec((tm, n), lambda t, s: (t * steps + s, 0))
    whole_tile = pl.BlockSpec((GMLP_HEADS, TILE_TOKENS, GMLP_HEAD_DIM), lambda t, s: (0, t, 0))
    lay = lambda *shape: pl.BlockSpec((None,) + shape, lambda t, s: (l,) + (0,) * len(shape))
    return pl.pallas_call(
        _post_kernel,
        grid=(N_TILES, steps),
        in_specs=[
            tok(D_MODEL),
            pl.BlockSpec((None, J_PER_STEP, D_SSM, J_ROWS), lambda t, s: (t, s, 0, 0)),
            whole_tile, whole_tile,
            pl.BlockSpec((None, None, N_MOD, N_PS, D_MODEL), lambda t, s: (l, t, 0, 0, 0)),
            lay(D_SSM, D_SSM), lay(GMLP_HEADS, GMLP_CHUNK, GMLP_CHUNK),
            lay(GMLP_HEADS, GMLP_CHUNK, 1), lay(D_MODEL, D_MODEL), lay(1, D_MODEL),
            pl.BlockSpec((N_EXPERTS, D_MODEL), lambda t, s: (0, 0)),
            pl.BlockSpec((N_EXPERTS, 1), lambda t, s: (0, 0)),
        ],
        out_specs=[tok(HX_WIDTH), pl.BlockSpec((1, tm), lambda t, s: (0, t * steps + s))],
        out_shape=[
            jax.ShapeDtypeStruct((T_ALL, HX_WIDTH), F32),
            jax.ShapeDtypeStruct((1, T_ALL), F32),
        ],
        scratch_shapes=[pltpu.VMEM((GMLP_HEADS, TILE_TOKENS, GMLP_HEAD_DIM), F32), pltpu.VMEM((tm, D_SSM), F32)],
        compiler_params=_params("arbitrary", "arbitrary"),
        name="mixers_out_router",
    )(x, yt, ug, vn, mod8, w_glu_t, w_s, b_s, w_out, norm2_w.reshape(DEPTH, 1, D_MODEL), rwt, rb)


def _moe_kernel(ea_ref, eb_ref, new_a_ref, new_b_ref, nv_ref, half_ref, src_ref, drow_ref,
                hx_hbm, wga_ref, wua_ref, wda_ref, wgb_ref, wub_ref, wdb_ref, g2_ref, fw_ref,
                o_hbm, wga_s, wua_s, wda_s, wgb_s, wub_s, wdb_s, hx_buf, o_buf, z_buf, g_sem, s_sem, z_sem,
                *, final):
    n = pl.program_id(0)
    n_valid = nv_ref[0]
    slot = n % 2

    def by_size(tile, fn):
        @pl.when(half_ref[tile] == 1)
        def _():
            fn(MOE_TM // 2)

        @pl.when(half_ref[tile] == 0)
        def _():
            fn(MOE_TM)

    def start_gather(tile, sl, rows):
        base = tile * MOE_TM
        for r in range(rows):
            pltpu.make_async_copy(hx_hbm.at[pl.ds(src_ref[base + r], 1)], hx_buf.at[sl, pl.ds(r, 1)],
                                  g_sem.at[sl]).start()

    def wait_gather(sl, rows):
        pltpu.make_async_copy(hx_hbm.at[pl.ds(0, rows)], hx_buf.at[sl, pl.ds(0, rows)], g_sem.at[sl]).wait()

    def start_scatter(tile, sl, rows):
        base = tile * MOE_TM
        for r in range(rows):
            pltpu.make_async_copy(o_buf.at[sl, pl.ds(r, 1)], o_hbm.at[pl.ds(drow_ref[base + r], 1)],
                                  s_sem.at[sl]).start()

    def wait_scatter(sl, rows):
        pltpu.make_async_copy(o_buf.at[sl, pl.ds(0, rows)], o_hbm.at[pl.ds(0, rows)], s_sem.at[sl]).wait()

    @pl.when(n == 0)
    def _():
        z_buf[...] = jnp.zeros_like(z_buf)
        spare = [pltpu.make_async_copy(z_buf, o_hbm.at[pl.ds(T_ALL + i * MOE_TM, MOE_TM)], z_sem)
                 for i in range((MOE_ROWS - T_ALL) // MOE_TM)]
        for cp in spare:
            cp.start()
        for cp in spare:
            cp.wait()
        by_size(0, lambda rows: start_gather(0, 0, rows))

    @pl.when(new_a_ref[n] == 1)
    def _():
        wga_s[...] = wga_ref[...].astype(BF16)
        wua_s[...] = wua_ref[...].astype(BF16)
        wda_s[...] = wda_ref[...].astype(BF16)

    @pl.when(new_b_ref[n] == 1)
    def _():
        wgb_s[...] = wgb_ref[...].astype(BF16)
        wub_s[...] = wub_ref[...].astype(BF16)
        wdb_s[...] = wdb_ref[...].astype(BF16)

    @pl.when(n + 1 < n_valid)
    def _():
        by_size(n + 1, lambda rows: start_gather(n + 1, 1 - slot, rows))

    def tile_step(rows):
        wait_gather(slot, rows)

        @pl.when(n >= 2)
        def _():
            by_size(n - 2, lambda r: wait_scatter(slot, r))

        h = hx_buf[slot, 0:rows, 0:D_MODEL].astype(BF16)
        lanes = hx_buf[slot, 0:rows, 2 * D_MODEL:]

        def expert(wg, wu, wd, gate):
            hg = jnp.dot(h, wg[...], preferred_element_type=F32)
            hu = jnp.dot(h, wu[...], preferred_element_type=F32)
            act = hg * _sigmoid(hg) * hu * gate
            return jnp.dot(act.astype(BF16), wd[...], preferred_element_type=F32)

        y = expert(wga_s, wua_s, wda_s, lanes[:, 0:1]) + expert(wgb_s, wub_s, wdb_s, lanes[:, 1:2])
        cond_row = lanes[:, 2:3]
        gate2 = jnp.where(cond_row == 0.0, g2_ref[0:1, :], jnp.where(cond_row == 1.0, g2_ref[1:2, :], g2_ref[2:3, :]))
        x2 = hx_buf[slot, 0:rows, D_MODEL:2 * D_MODEL] + gate2 * y
        o_buf[slot, 0:rows] = _rmsnorm(x2, fw_ref[...]) if final else x2
        start_scatter(n, slot, rows)

    @pl.when(n < n_valid)
    def _():
        by_size(n, tile_step)

    @pl.when(n == MOE_TILES - 1)
    def _():
        last = n_valid - 1
        by_size(last, lambda r: wait_scatter(last % 2, r))
        by_size(last - 1, lambda r: wait_scatter(1 - last % 2, r))


def _experts(tables, hx, mods, l, w_gate, w_up, w_down, final_w, final):
    w_a = lambda r, c: pl.BlockSpec((None, None, r, c), lambda n, ea, eb, *_: (l, ea[n], 0, 0))
    w_b = lambda r, c: pl.BlockSpec((None, None, r, c), lambda n, ea, eb, *_: (l, eb[n], 0, 0))
    up = pltpu.VMEM((D_MODEL, D_EXPERT), BF16)
    down = pltpu.VMEM((D_EXPERT, D_MODEL), BF16)
    return pl.pallas_call(
        functools.partial(_moe_kernel, final=final),
        grid_spec=pltpu.PrefetchScalarGridSpec(
            num_scalar_prefetch=len(tables),
            grid=(MOE_TILES,),
            in_specs=[
                pl.BlockSpec(memory_space=pl.ANY),
                w_a(D_MODEL, D_EXPERT), w_a(D_MODEL, D_EXPERT), w_a(D_EXPERT, D_MODEL),
                w_b(D_MODEL, D_EXPERT), w_b(D_MODEL, D_EXPERT), w_b(D_EXPERT, D_MODEL),
                pl.BlockSpec((None, None, MOD_ROWS, D_MODEL), lambda n, *_: (l, N_MOD - 1, 0, 0)),
                pl.BlockSpec((1, D_MODEL), lambda n, *_: (0, 0)),
            ],
            out_specs=pl.BlockSpec(memory_space=pl.ANY),
            scratch_shapes=[up, up, down, up, up, down,
                            pltpu.VMEM((2, MOE_TM, HX_WIDTH), F32), pltpu.VMEM((2, MOE_TM, D_MODEL), F32),
                            pltpu.VMEM((MOE_TM, D_MODEL), F32),
                            pltpu.SemaphoreType.DMA((2,)), pltpu.SemaphoreType.DMA((2,)), pltpu.SemaphoreType.DMA(())],
        ),
        out_shape=jax.ShapeDtypeStruct((MOE_ROWS, D_MODEL), F32),
        compiler_params=_params("arbitrary"),
        name="experts",
    )(*tables, hx, w_gate, w_up, w_down, w_gate, w_up, w_down, mods, final_w.reshape(1, D_MODEL))


def _count_before(flags):
    n, k = flags.shape
    blocks = flags.reshape(n // 128, 128, k).astype(F32)
    strictly_lower = jnp.tril(jnp.ones((128, 128), F32), -1)
    within = jnp.einsum("ij,bjk->bik", strictly_lower, blocks)
    totals = jnp.sum(blocks, axis=1)
    before = jnp.cumsum(totals, axis=0) - totals
    return (within + before[:, None, :]).reshape(n, k).astype(jnp.int32)


def _routing_tables(cls, to_sequence_order):
    n_cls = N_EXPERT_GROUPS * len(PAIR_SLOT_A)
    onehot = (cls[:, None] == jnp.arange(n_cls, dtype=jnp.int32)[None, :]).astype(jnp.int32)
    counts = jnp.sum(onehot, axis=0)
    tiles = (counts + MOE_TM - 1) // MOE_TM
    tile_end = jnp.cumsum(tiles)
    row_in_class = (tile_end - tiles)[None, :] * MOE_TM + _count_before(onehot)
    dst = jnp.sum(onehot * row_in_class, axis=1)
    token_plus_1 = jnp.zeros((MOE_ROWS,), jnp.int32).at[dst].set(jnp.arange(1, T_ALL + 1, dtype=jnp.int32))
    is_pad = token_plus_1 == 0
    src = jnp.maximum(token_plus_1 - 1, 0)
    spare = T_ALL + _count_before(is_pad.astype(jnp.int32)[:, None])[:, 0]
    if to_sequence_order:
        j, c, ps = (src // J_ROWS) % SCAN_CHUNK, (src // N_PS) % SEG_CHUNKS, src % N_PS
        target = (src // TILE_TOKENS) * TILE_TOKENS + ps * SEG_LEN + c * SCAN_CHUNK + j
    else:
        target = src
    drow = jnp.where(is_pad, spare, target).astype(jnp.int32)
    n_valid = tile_end[-1]
    tile_id = jnp.minimum(jnp.arange(MOE_TILES, dtype=jnp.int32), n_valid - 1)
    tile_cls = jnp.sum((tile_end[None, :] <= tile_id[:, None]).astype(jnp.int32), axis=1)
    group, pair = tile_cls // len(PAIR_SLOT_A), tile_cls % len(PAIR_SLOT_A)
    in_cls = (tile_cls[:, None] == jnp.arange(n_cls, dtype=jnp.int32)[None, :]).astype(jnp.int32)
    left = jnp.sum(in_cls * (counts[None, :] - (tile_id[:, None] - (tile_end - tiles)[None, :]) * MOE_TM), axis=1)
    half = (left <= MOE_TM // 2).astype(jnp.int32)

    def slot_expert(table):
        local = jnp.full_like(pair, table[-1])
        for p in range(len(table) - 1):
            local = jnp.where(pair == p, table[p], local)
        return group * EXPERTS_PER_GROUP + local

    e_a = slot_expert(PAIR_SLOT_A)
    e_b = slot_expert(PAIR_SLOT_B)
    first = jnp.ones((1,), jnp.int32)
    new_a = jnp.concatenate([first, (e_a[1:] != e_a[:-1]).astype(jnp.int32)])
    new_b = jnp.concatenate([first, (e_b[1:] != e_b[:-1]).astype(jnp.int32)])
    return e_a, e_b, new_a, new_b, n_valid.reshape(1).astype(jnp.int32), half, src, drow


def _dirs_on_lanes(p):
    p = jnp.moveaxis(p, 1, -2)
    return p.reshape(p.shape[:-2] + (2 * SSM_STATE,))


def _to_internal_order(x):
    x = x.reshape(N_TILES, N_PS, SEG_CHUNKS, SCAN_CHUNK, D_MODEL)
    return x.transpose(0, 3, 2, 1, 4).reshape(T_ALL, D_MODEL)


def _gmlp_position_order(w):
    n_lo = GMLP_CHUNK // SCAN_CHUNK
    lead = w.shape[:2]
    w = w.reshape(lead + (n_lo, SCAN_CHUNK) + w.shape[3:])
    w = jnp.swapaxes(w, 2, 3)
    return w.reshape(lead + (GMLP_CHUNK,) + w.shape[4:])


def kernel(x_prompt, x_sample, c, state_ssm_re, state_ssm_im, c_ctx, norm1_w, norm2_w, w_mod, b_mod, w_in,
           ssm_a_re, ssm_a_im, ssm_log_dt, ssm_b_re, ssm_b_im, ssm_c_re, ssm_c_im, ssm_d, w_glu,
           gmlp_ln_w, gmlp_ln_b, gmlp_w_s, gmlp_b_s, w_out, router_w, router_b, w_gate, w_up, w_down,
           final_norm_w):
    x = jnp.concatenate([x_prompt.reshape(T_PROMPT, D_MODEL), x_sample.reshape(T_SAMPLE, D_MODEL)], axis=0)
    x = _to_internal_order(x)

    cvec = jnp.concatenate([c_ctx[None, :], c, jnp.zeros((MOD_ROWS - 1 - N_SAMPLE_SEQ, D_MODEL), F32)], axis=0)
    mods = _modulation(cvec, w_mod, b_mod)
    ps_row = [[0] * N_PS] * SAMPLE_TILE + [[1 + p // SEGS_PER_SAMPLE_SEQ for p in range(N_PS)]]
    mod8 = mods[:, :, jnp.array(ps_row, jnp.int32), :].transpose(0, 2, 1, 3, 4)

    log_dt = jnp.broadcast_to(ssm_log_dt[..., None], ssm_a_re.shape)
    d_lanes = jnp.tile(ssm_d.reshape(DEPTH, N_GROUPS, SSM_GROUP), (1, 1, 128 // SSM_GROUP))
    vecs = jnp.stack([_dirs_on_lanes(ssm_a_re), _dirs_on_lanes(ssm_a_im), _dirs_on_lanes(log_dt), d_lanes], axis=2)
    vecs = jnp.concatenate([vecs, jnp.zeros((DEPTH, N_GROUPS, 4, 2 * SSM_STATE), F32)], axis=2)
    mats = jnp.stack([_dirs_on_lanes(jnp.swapaxes(ssm_b_re, -1, -2)), _dirs_on_lanes(jnp.swapaxes(ssm_b_im, -1, -2)),
                      _dirs_on_lanes(ssm_c_re), _dirs_on_lanes(ssm_c_im)], axis=2)
    wft, cct, ttt, a16 = _ssm_operators(vecs, mats)

    w_in_g = w_in[:, :, D_SSM:].astype(BF16)
    w_in_st = jnp.swapaxes(w_in[:, :, :D_SSM], 1, 2).astype(BF16)
    w_glu_t = jnp.swapaxes(w_glu, 1, 2).astype(BF16)
    w_s = jnp.swapaxes(_gmlp_position_order(jnp.swapaxes(_gmlp_position_order(gmlp_w_s), 2, 3)), 2, 3).astype(BF16)
    w_out = w_out.astype(BF16)
    b_s = _gmlp_position_order(gmlp_b_s)[..., None]
    rwt = router_w.T
    rb = router_b.reshape(N_EXPERTS, 1)

    new_re, new_im = [], []
    for l in range(DEPTH):
        ug, vn, xt = _input_proj(x, mod8, l, norm1_w, w_in_g, w_in_st, gmlp_ln_w, gmlp_ln_b)
        s0 = jnp.concatenate([state_ssm_re[:, l].transpose(2, 0, 1, 3).reshape(N_GROUPS, N_SAMPLE_SEQ, 128),
                              state_ssm_im[:, l].transpose(2, 0, 1, 3).reshape(N_GROUPS, N_SAMPLE_SEQ, 128)], axis=-1)
        s0 = jnp.repeat(s0, SEGS_PER_SAMPLE_SEQ, axis=1)
        yt, fin = _ssm_scan(xt, wft, cct, ttt, a16, s0, l)
        fin = fin[:SAMPLE_TILE].reshape(SAMPLE_TILE, N_GROUPS, N_PS, 2, 2, SSM_STATE)
        fin = fin.transpose(3, 0, 2, 4, 1, 5).reshape(2, N_PROMPT_SEQ, 2, N_GROUPS, SSM_STATE)
        new_re.append(fin[0])
        new_im.append(fin[1])
        hx, route = _mix_out(x, yt, ug, vn, mod8, l, w_glu_t, w_s, b_s, w_out, norm2_w, rwt, rb)
        final = l == DEPTH - 1
        tables = _routing_tables(route[0].astype(jnp.int32), to_sequence_order=final)
        x = _experts(tables, hx, mods, l, w_gate, w_up, w_down, final_norm_w, final)

    y_prompt = x[:T_PROMPT].reshape(N_PROMPT_SEQ, PROMPT_LEN, D_MODEL)
    y_sample = x[T_PROMPT:T_ALL].reshape(N_SAMPLE_SEQ, SAMPLE_LEN, D_MODEL)
    return (y_prompt, y_sample, jnp.stack(new_re, axis=1), jnp.stack(new_im, axis=1))
```

```python
import functools
import math

import jax
import jax.numpy as jnp
from jax import lax
from jax.experimental import pallas as pl
from jax.experimental.pallas import tpu as pltpu

F32 = jnp.float32
BF16 = jnp.bfloat16

D_MODEL = 1024
N_PROMPT_SEQ = 16
PROMPT_LEN = 256
N_SAMPLE_SEQ = 2
SAMPLE_LEN = 1024
T_PROMPT = N_PROMPT_SEQ * PROMPT_LEN
T_SAMPLE = N_SAMPLE_SEQ * SAMPLE_LEN
T_ALL = T_PROMPT + T_SAMPLE
DEPTH = 2
D_SSM = 512
SSM_GROUP = 16
N_GROUPS = 32
SSM_STATE = 64
D_GMLP = 512
GMLP_HEADS = 4
GMLP_HEAD_DIM = 128
GMLP_CHUNK = 128
N_EXPERTS = 16
N_EXPERT_GROUPS = 4
EXPERTS_PER_GROUP = 4
D_EXPERT = 512
N_MOD = 6
EPS = 1e-6

SCAN_CHUNK = 16
CHUNK_WIDTH = SCAN_CHUNK * SSM_GROUP
SEG_LEN = 256
SEG_CHUNKS = SEG_LEN // SCAN_CHUNK
N_PS = 8
TILE_TOKENS = N_PS * SEG_LEN
N_TILES = T_ALL // TILE_TOKENS
SAMPLE_TILE = T_PROMPT // TILE_TOKENS
SEGS_PER_SAMPLE_SEQ = SAMPLE_LEN // SEG_LEN
J_ROWS = SEG_CHUNKS * N_PS
GROUP_BLOCK = 8
PREP_GROUPS = 4
MOD_ROWS = 8
MODS_PER_STEP = 2

J_PER_STEP = 4
TOKEN_TILE = J_PER_STEP * J_ROWS
PAIR_SLOT_A = (0, 0, 0, 1, 1, 3)
PAIR_SLOT_B = (1, 2, 3, 3, 2, 2)
MOE_TM = 128
MOE_TILES = T_ALL // MOE_TM + N_EXPERT_GROUPS * len(PAIR_SLOT_A)
MOE_ROWS = MOE_TILES * MOE_TM
ROUTE_LANES = 128
HX_WIDTH = 2 * D_MODEL + ROUTE_LANES
VMEM_LIMIT = 56 * 1024 * 1024
TRANS_B = (((1,), (1,)), ((), ()))


def _sigmoid(x):
    return 1.0 / (1.0 + jnp.exp(-x))


def _gelu_tanh(x):
    c = math.sqrt(2.0 / math.pi)
    return x * (0.5 * (1.0 + jnp.tanh(c * (x + 0.044715 * (x * x * x)))))


def _split_bf16(a):
    hi = a.astype(BF16)
    return hi, (a - hi.astype(F32)).astype(BF16)


def _rmsnorm(x, w):
    return x * lax.rsqrt(jnp.mean(x * x, axis=-1, keepdims=True) + EPS) * w


def _per_ps(fn, a, *mods):
    rows, d = a.shape
    out = fn(a.reshape(rows // N_PS, N_PS, d), *[m[None] for m in mods])
    return out.reshape(rows, d)


def _params(*sem):
    return pltpu.CompilerParams(dimension_semantics=sem, vmem_limit_bytes=VMEM_LIMIT)


def _mod_kernel(c_ref, w_ref, b_ref, o_ref):
    c = c_ref[...]
    s = c * _sigmoid(c)
    res = jnp.dot(s.astype(BF16), w_ref[...].astype(BF16), preferred_element_type=F32)
    for m in range(MODS_PER_STEP):
        o_ref[m] = res[:, m * D_MODEL:(m + 1) * D_MODEL] + b_ref[m]


def _modulation(cvec, w_mod, b_mod):
    mp = MODS_PER_STEP
    return pl.pallas_call(
        _mod_kernel,
        grid=(DEPTH, N_MOD // mp),
        in_specs=[
            pl.BlockSpec((MOD_ROWS, D_MODEL), lambda l, n: (0, 0)),
            pl.BlockSpec((None, D_MODEL, mp * D_MODEL), lambda l, n: (l, 0, n)),
            pl.BlockSpec((None, mp, 1, D_MODEL), lambda l, n: (l, n, 0, 0)),
        ],
        out_specs=pl.BlockSpec((None, mp, MOD_ROWS, D_MODEL), lambda l, n: (l, n, 0, 0)),
        out_shape=jax.ShapeDtypeStruct((DEPTH, N_MOD, MOD_ROWS, D_MODEL), F32),
        compiler_params=_params("arbitrary", "arbitrary"),
        name="adaln_mod",
    )(cvec, w_mod, b_mod.reshape(DEPTH, N_MOD, 1, D_MODEL))


def _first_step():
    return jnp.logical_and(pl.program_id(0) == 0, pl.program_id(1) == 0)


def _in_kernel(x_ref, mod_ref, nw_ref, w_ref, lnw_ref, lnb_ref, ug_ref, vn_ref, xt_ref, wg_ref, wst_ref):
    @pl.when(_first_step())
    def _():
        wg_ref[...] = w_ref[:, D_SSM:].astype(BF16)
        wst_ref[...] = w_ref[:, :D_SSM].T.astype(BF16)

    y = _rmsnorm(x_ref[...], nw_ref[...])
    h = _per_ps(lambda a, sc, sh: a * (1.0 + sc) + sh, y, mod_ref[1], mod_ref[0]).astype(BF16)
    zg = _gelu_tanh(jnp.dot(h, wg_ref[...], preferred_element_type=F32))
    v = zg[:, D_GMLP:]
    mu = jnp.mean(v, axis=-1, keepdims=True)
    vc = v - mu
    var = jnp.mean(vc * vc, axis=-1, keepdims=True)
    vn = vc * lax.rsqrt(var + EPS) * lnw_ref[...] + lnb_ref[...]
    for hd in range(GMLP_HEADS):
        cols = slice(hd * GMLP_HEAD_DIM, (hd + 1) * GMLP_HEAD_DIM)
        ug_ref[hd] = zg[:, cols]
        vn_ref[hd] = vn[:, cols]
    xt = lax.dot_general(wst_ref[...], h, TRANS_B, preferred_element_type=F32).astype(BF16)
    for k in range(J_PER_STEP):
        xt_ref[k] = xt[:, k * J_ROWS:(k + 1) * J_ROWS]


def _input_proj(x, mod8, l, norm1_w, w_in, ln_w, ln_b):
    tm = TOKEN_TILE
    steps = TILE_TOKENS // tm
    tok = lambda n: pl.BlockSpec((tm, n), lambda t, s: (t * steps + s, 0))
    heads = pl.BlockSpec((GMLP_HEADS, tm, GMLP_HEAD_DIM), lambda t, s: (0, t * steps + s, 0))
    lay = lambda *shape: pl.BlockSpec((None,) + shape, lambda t, s: (l,) + (0,) * len(shape))
    return pl.pallas_call(
        _in_kernel,
        grid=(N_TILES, steps),
        in_specs=[
            tok(D_MODEL),
            pl.BlockSpec((None, None, N_MOD, N_PS, D_MODEL), lambda t, s: (l, t, 0, 0, 0)),
            lay(1, D_MODEL), lay(D_MODEL, D_SSM + 2 * D_GMLP), lay(1, D_GMLP), lay(1, D_GMLP),
        ],
        out_specs=[heads, heads,
                   pl.BlockSpec((None, J_PER_STEP, D_SSM, J_ROWS), lambda t, s: (t, s, 0, 0))],
        out_shape=[jax.ShapeDtypeStruct((GMLP_HEADS, T_ALL, GMLP_HEAD_DIM), F32)] * 2 + [
            jax.ShapeDtypeStruct((N_TILES, SCAN_CHUNK, D_SSM, J_ROWS), BF16)],
        scratch_shapes=[pltpu.VMEM((D_MODEL, 2 * D_GMLP), BF16), pltpu.VMEM((D_SSM, D_MODEL), BF16)],
        compiler_params=_params("arbitrary", "arbitrary"),
        name="norm1_in_proj",
    )(x, mod8, norm1_w.reshape(DEPTH, 1, D_MODEL), w_in,
      ln_w.reshape(DEPTH, 1, D_GMLP), ln_b.reshape(DEPTH, 1, D_GMLP))


def _shift_lanes_right(a, b, s, lane):
    if s == 0:
        return a, b
    if s == 128:
        return jnp.zeros_like(a), a
    if s < 128:
        ra = pltpu.roll(a, s, 1)
        rb = pltpu.roll(b, s, 1)
        return jnp.where(lane >= s, ra, 0.0), jnp.where(lane >= s, rb, ra)
    t = s - 128
    return jnp.zeros_like(a), jnp.where(lane >= t, pltpu.roll(a, t, 1), 0.0)


def _shift_lanes_left(a, b, s, lane):
    if s == 0:
        return a, b
    if s == 128:
        return b, jnp.zeros_like(b)
    if s < 128:
        ra = pltpu.roll(a, 128 - s, 1)
        rb = pltpu.roll(b, 128 - s, 1)
        return jnp.where(lane < 128 - s, ra, rb), jnp.where(lane < 128 - s, rb, 0.0)
    t = s - 128
    return jnp.where(lane < 128 - t, pltpu.roll(b, 128 - t, 1), 0.0), jnp.zeros_like(b)


def _prep_kernel(*refs):
    for g in range(PREP_GROUPS):
        _prep_group(*[r.at[g] for r in refs])


def _prep_group(vec_ref, mat_ref, wft_ref, cct_ref, ttt_ref, a16_ref, wf_scr, cm_scr, tt_scr):
    a_re = vec_ref[0:1, :]
    a_im = vec_ref[1:2, :]
    dt = jnp.exp(vec_ref[2:3, :])
    d_skip = vec_ref[3:4, :]
    mag = jnp.exp(a_re * dt)
    ang = a_im * dt
    ab_r = mag * jnp.cos(ang)
    ab_i = mag * jnp.sin(ang)
    den = a_re * a_re + a_im * a_im
    nr = ab_r - 1.0
    q_r = (nr * a_re + ab_i * a_im) / den
    q_i = (ab_i * a_re - nr * a_im) / den
    bt_r = mat_ref[0]
    bt_i = mat_ref[1]
    c_r = mat_ref[2]
    c_i = mat_ref[3]
    bb_r = q_r * bt_r - q_i * bt_i
    bb_i = q_r * bt_i + q_i * bt_r
    p_r = [jnp.ones_like(ab_r)]
    p_i = [jnp.zeros_like(ab_r)]
    for _ in range(SCAN_CHUNK):
        pr, pi = p_r[-1], p_i[-1]
        p_r.append(pr * ab_r - pi * ab_i)
        p_i.append(pr * ab_i + pi * ab_r)
    a16_ref[0] = jnp.broadcast_to(p_r[SCAN_CHUNK], (N_PS, 128))
    a16_ref[1] = jnp.broadcast_to(p_i[SCAN_CHUNK], (N_PS, 128))

    lane = lax.broadcasted_iota(jnp.int32, (1, 128), 1)
    is_fwd = lane < SSM_STATE

    def pick(mf, mb):
        return jnp.where(is_fwd, p_r[mf], p_r[mb]), jnp.where(is_fwd, p_i[mf], p_i[mb])

    for j in range(SCAN_CHUNK):
        rows = slice(j * SSM_GROUP, (j + 1) * SSM_GROUP)
        wr, wi = pick(SCAN_CHUNK - 1 - j, j)
        wf_scr[rows, 0:128] = bb_r * wr - bb_i * wi
        wf_scr[rows, 128:256] = bb_r * wi + bb_i * wr
        wr, wi = pick(j + 1, SCAN_CHUNK - j)
        cct_ref[rows, 0:128] = (c_r * wr - c_i * wi).astype(BF16)
        cct_ref[rows, 128:256] = (-(c_r * wi + c_i * wr)).astype(BF16)
        wr, wi = pick(j, SCAN_CHUNK - 1 - j)
        cm_scr[rows, 0:128] = c_r * wr - c_i * wi
        cm_scr[rows, 128:256] = c_r * wi + c_i * wr
    wft_ref[...] = wf_scr[...].T.astype(BF16)

    zero = jnp.zeros_like(bb_r)
    cm_hi, cm_lo = _split_bf16(cm_scr[...])
    dot_t = lambda a, b: lax.dot_general(a, b, TRANS_B, preferred_element_type=F32)

    def lag_rows(keep):
        lhs = jnp.concatenate([jnp.where(keep, bb_r, zero), jnp.where(keep, -bb_i, zero)], axis=1)
        hi, lo = _split_bf16(lhs)
        return dot_t(hi, cm_hi) + (dot_t(hi, cm_lo) + dot_t(lo, cm_hi))

    mf = lag_rows(is_fwd)
    mb = lag_rows(jnp.logical_not(is_fwd))
    mf_a, mf_b = mf[:, 0:128], mf[:, 128:256]
    mb_a, mb_b = mb[:, 0:128], mb[:, 128:256]
    row_h = lax.broadcasted_iota(jnp.int32, (SSM_GROUP, 128), 0)
    lane_h = lax.broadcasted_iota(jnp.int32, (SSM_GROUP, 128), 1)
    for jp in range(SCAN_CHUNK):
        rows = slice(jp * SSM_GROUP, (jp + 1) * SSM_GROUP)
        fa, fb = _shift_lanes_right(mf_a, mf_b, SSM_GROUP * jp, lane)
        ba, bb = _shift_lanes_left(mb_a, mb_b, SSM_GROUP * (SCAN_CHUNK - 1 - jp), lane)
        diag = SSM_GROUP * jp + row_h
        tt_scr[rows, 0:128] = fa + ba + jnp.where(lane_h == diag, d_skip, 0.0)
        tt_scr[rows, 128:256] = fb + bb + jnp.where(lane_h + 128 == diag, d_skip, 0.0)
    ttt_ref[...] = tt_scr[...].T.astype(BF16)


def _ssm_operators(vecs, mats):
    op = jax.ShapeDtypeStruct((DEPTH, N_GROUPS, CHUNK_WIDTH, CHUNK_WIDTH), BF16)
    pg = PREP_GROUPS
    op_spec = pl.BlockSpec((None, pg, CHUNK_WIDTH, CHUNK_WIDTH), lambda l, g: (l, g, 0, 0))
    sq = pltpu.VMEM((pg, CHUNK_WIDTH, CHUNK_WIDTH), F32)
    return pl.pallas_call(
        _prep_kernel,
        grid=(DEPTH, N_GROUPS // pg),
        in_specs=[
            pl.BlockSpec((None, pg, 8, 128), lambda l, g: (l, g, 0, 0)),
            pl.BlockSpec((None, pg, 4, SSM_GROUP, 128), lambda l, g: (l, g, 0, 0, 0)),
        ],
        out_specs=[op_spec, op_spec, op_spec,
                   pl.BlockSpec((None, pg, 2, N_PS, 128), lambda l, g: (l, g, 0, 0, 0))],
        out_shape=[op, op, op, jax.ShapeDtypeStruct((DEPTH, N_GROUPS, 2, N_PS, 128), F32)],
        scratch_shapes=[sq, sq, sq],
        compiler_params=_params("arbitrary", "arbitrary"),
        name="s5_operators",
    )(vecs, mats)


def _ssm_kernel(xt_ref, wft_ref, cct_ref, ttt_ref, a16_ref, s0_ref, yt_ref, fin_ref, s_scr, f_scr, ft_scr):
    tile = pl.program_id(0)
    lane = lax.broadcasted_iota(jnp.int32, (GROUP_BLOCK, N_PS, 128), 2)
    is_fwd = lane < SSM_STATE
    half = SSM_STATE

    def group_x(gl):
        return xt_ref[:, gl * SSM_GROUP:(gl + 1) * SSM_GROUP, :].reshape(CHUNK_WIDTH, J_ROWS)

    for gl in range(GROUP_BLOCK):
        ft_scr[gl] = jnp.dot(wft_ref[gl], group_x(gl), preferred_element_type=F32)
        f_scr[gl] = ft_scr[gl].T

    a_r = a16_ref[:, 0]
    a_i = a16_ref[:, 1]

    def scan(s_r, s_i):
        for i in range(SEG_CHUNKS):
            rf = slice(i * N_PS, (i + 1) * N_PS)
            rb = slice((SEG_CHUNKS - 1 - i) * N_PS, (SEG_CHUNKS - i) * N_PS)
            s_scr[:, rf, 0:half] = s_r[:, :, 0:half]
            s_scr[:, rf, 128:128 + half] = s_i[:, :, 0:half]
            s_scr[:, rb, half:128] = s_r[:, :, half:128]
            s_scr[:, rb, 128 + half:256] = s_i[:, :, half:128]
            f_r = jnp.where(is_fwd, f_scr[:, rf, 0:128], f_scr[:, rb, 0:128])
            f_i = jnp.where(is_fwd, f_scr[:, rf, 128:256], f_scr[:, rb, 128:256])
            s_r, s_i = a_r * s_r - a_i * s_i + f_r, a_r * s_i + a_i * s_r + f_i
        return s_r, s_i

    zeros = jnp.zeros((GROUP_BLOCK, N_PS, 128), F32)
    z_r, z_i = scan(zeros, zeros)
    fin_ref[:, :, 0:128] = z_r
    fin_ref[:, :, 128:256] = z_i

    @pl.when(tile == SAMPLE_TILE)
    def _():
        b_r, b_i = a_r, a_i
        for _ in range(4):
            b_r, b_i = b_r * b_r - b_i * b_i, 2.0 * (b_r * b_i)
        seg = lax.broadcasted_iota(jnp.int32, (GROUP_BLOCK, N_PS, 128), 1) % SEGS_PER_SAMPLE_SEQ
        i_r = s0_ref[:, :, 0:128]
        i_i = s0_ref[:, :, 128:256]
        for step in range(1, SEGS_PER_SAMPLE_SEQ):
            pr = jnp.where(is_fwd, pltpu.roll(i_r, 1, 1), pltpu.roll(i_r, N_PS - 1, 1))
            pi = jnp.where(is_fwd, pltpu.roll(i_i, 1, 1), pltpu.roll(i_i, N_PS - 1, 1))
            zr = jnp.where(is_fwd, pltpu.roll(z_r, 1, 1), pltpu.roll(z_r, N_PS - 1, 1))
            zi = jnp.where(is_fwd, pltpu.roll(z_i, 1, 1), pltpu.roll(z_i, N_PS - 1, 1))
            n_r = b_r * pr - b_i * pi + zr
            n_i = b_r * pi + b_i * pr + zi
            first = jnp.where(is_fwd, step, 0)
            last = jnp.where(is_fwd, SEGS_PER_SAMPLE_SEQ - 1, SEGS_PER_SAMPLE_SEQ - 1 - step)
            upd = jnp.logical_and(seg >= first, seg <= last)
            i_r = jnp.where(upd, n_r, i_r)
            i_i = jnp.where(upd, n_i, i_i)
        scan(i_r, i_i)

    for gl in range(GROUP_BLOCK):
        yt = jnp.dot(ttt_ref[gl], group_x(gl), preferred_element_type=F32)
        yt += lax.dot_general(cct_ref[gl], s_scr[gl].astype(BF16), TRANS_B, preferred_element_type=F32)
        yt_ref[:, gl * SSM_GROUP:(gl + 1) * SSM_GROUP, :] = yt.reshape(SCAN_CHUNK, SSM_GROUP, J_ROWS)


def _ssm_scan(xt, wft, cct, ttt, a16, s0, l):
    gb = GROUP_BLOCK
    op_spec = pl.BlockSpec((None, gb, CHUNK_WIDTH, CHUNK_WIDTH), lambda t, g: (l, g, 0, 0))
    io_spec = pl.BlockSpec((None, SCAN_CHUNK, gb * SSM_GROUP, J_ROWS), lambda t, g: (t, 0, g, 0))
    return pl.pallas_call(
        _ssm_kernel,
        grid=(N_TILES, N_GROUPS // gb),
        in_specs=[
            io_spec, op_spec, op_spec, op_spec,
            pl.BlockSpec((None, gb, 2, N_PS, 128), lambda t, g: (l, g, 0, 0, 0)),
            pl.BlockSpec((gb, N_PS, CHUNK_WIDTH), lambda t, g: (g, 0, 0)),
        ],
        out_specs=[io_spec, pl.BlockSpec((None, gb, N_PS, CHUNK_WIDTH), lambda t, g: (t, g, 0, 0))],
        out_shape=[
            jax.ShapeDtypeStruct((N_TILES, SCAN_CHUNK, D_SSM, J_ROWS), F32),
            jax.ShapeDtypeStruct((N_TILES, N_GROUPS, N_PS, CHUNK_WIDTH), F32),
        ],
        scratch_shapes=[pltpu.VMEM((gb, J_ROWS, CHUNK_WIDTH), F32), pltpu.VMEM((gb, J_ROWS, CHUNK_WIDTH), F32),
                        pltpu.VMEM((gb, CHUNK_WIDTH, J_ROWS), F32)],
        compiler_params=_params("arbitrary", "arbitrary"),
        name="s5_chunk_scan",
    )(xt, wft, cct, ttt, a16, s0)


def _post_kernel(x_ref, yt_ref, ug_ref, vn_ref, mod_ref, wglu_ref, ws_ref, bs_ref, wout_f32_ref,
                 nw_ref, rwt_ref, rb_ref, hx_ref, route_ref, yg_scr, y_scr, wglut_ref, wout_ref):
    tile = pl.program_id(0)
    step = pl.program_id(1)

    @pl.when(_first_step())
    def _():
        wglut_ref[...] = wglu_ref[...].T.astype(BF16)
        wout_ref[...] = wout_f32_ref[...].astype(BF16)

    @pl.when(step == 0)
    def _():
        def chunk(n, carry):
            ps = n % N_PS
            c_hi = n // N_PS
            base = c_hi * (GMLP_CHUNK // SCAN_CHUNK) * N_PS + ps
            rows = [pl.ds(j * J_ROWS + base, GMLP_CHUNK // SCAN_CHUNK, stride=N_PS) for j in range(SCAN_CHUNK)]
            for h in range(GMLP_HEADS):
                v = jnp.concatenate([vn_ref[h, r, :] for r in rows], axis=0).astype(BF16)
                u = jnp.concatenate([ug_ref[h, r, :] for r in rows], axis=0)
                s = jnp.dot(ws_ref[h], v, preferred_element_type=F32) + bs_ref[h]
                yg = u * s
                for j, r in enumerate(rows):
                    yg_scr[h, r, :] = yg[j * 8:(j + 1) * 8]
            return carry

        lax.fori_loop(0, TILE_TOKENS // GMLP_CHUNK, chunk, 0)

    yt = _gelu_tanh(jnp.concatenate([yt_ref[k] for k in range(J_PER_STEP)], axis=1))
    yt = yt * _sigmoid(jnp.dot(wglut_ref[...], yt.astype(BF16), preferred_element_type=F32))
    for k in range(J_PER_STEP):
        y_scr[k * J_ROWS:(k + 1) * J_ROWS, :] = yt[:, k * J_ROWS:(k + 1) * J_ROWS].T
    row0 = pl.multiple_of(step * TOKEN_TILE, TOKEN_TILE)
    proj = jnp.dot(y_scr[...].astype(BF16), wout_ref[0:D_SSM, :], preferred_element_type=F32)
    yg = jnp.concatenate([yg_scr[h, pl.ds(row0, TOKEN_TILE), :] for h in range(GMLP_HEADS)], axis=1)
    proj += jnp.dot(yg.astype(BF16), wout_ref[D_SSM:, :], preferred_element_type=F32)
    x1 = x_ref[...] + _per_ps(lambda a, g: a * g, proj, mod_ref[2])
    h2 = _per_ps(lambda a, sc, sh: a * (1.0 + sc) + sh, _rmsnorm(x1, nw_ref[...]), mod_ref[4], mod_ref[3])
    hx_ref[:, 0:D_MODEL] = h2
    hx_ref[:, D_MODEL:2 * D_MODEL] = x1
    r_hi, r_lo = _split_bf16(rwt_ref[...])
    h_hi, h_lo = _split_bf16(h2)
    dot_t = lambda a, b: lax.dot_general(a, b, TRANS_B, preferred_element_type=F32)
    logits = dot_t(r_hi, h_hi) + (dot_t(r_hi, h_lo) + dot_t(r_lo, h_hi))
    scores = _sigmoid(logits)
    sel = scores + rb_ref[...]
    sc = [scores[e:e + 1, :] for e in range(N_EXPERTS)]
    sl = [sel[e:e + 1, :] for e in range(N_EXPERTS)]
    gscore = []
    for g in range(N_EXPERT_GROUPS):
        v0, v1, v2, v3 = sl[4 * g:4 * g + 4]
        hi01, lo01 = jnp.maximum(v0, v1), jnp.minimum(v0, v1)
        hi23, lo23 = jnp.maximum(v2, v3), jnp.minimum(v2, v3)
        top1 = jnp.maximum(hi01, hi23)
        top2 = jnp.maximum(jnp.minimum(hi01, hi23), jnp.maximum(lo01, lo23))
        gscore.append(top1 + top2)
    best = gscore[0]
    gidx = jnp.zeros_like(best, dtype=jnp.int32)
    for g in range(1, N_EXPERT_GROUPS):
        upd = gscore[g] > best
        gidx = jnp.where(upd, g, gidx)
        best = jnp.where(upd, gscore[g], best)

    def in_group(vals, k):
        out = vals[k]
        for g in range(1, N_EXPERT_GROUPS):
            out = jnp.where(gidx == g, vals[4 * g + k], out)
        return out

    v = [in_group(sl, k) for k in range(EXPERTS_PER_GROUP)]
    s = [in_group(sc, k) for k in range(EXPERTS_PER_GROUP)]
    w = []
    bits = jnp.zeros_like(gidx)
    for k in range(EXPERTS_PER_GROUP):
        rank = jnp.zeros_like(gidx)
        for j in range(EXPERTS_PER_GROUP):
            if j == k:
                continue
            ahead = (v[j] >= v[k]) if j < k else (v[j] > v[k])
            rank = rank + ahead.astype(jnp.int32)
        w.append(jnp.where(rank < 2, s[k], 0.0))
        bits = bits + jnp.where(rank < 2, 1 << k, 0)
    denom = (w[0] + w[1]) + (w[2] + w[3])
    gate = [wk / denom for wk in w]
    pair = jnp.full_like(gidx, len(PAIR_SLOT_A) - 1)
    for p in range(len(PAIR_SLOT_A) - 1):
        pair = jnp.where(bits == (1 << PAIR_SLOT_A[p]) + (1 << PAIR_SLOT_B[p]), p, pair)

    def slot_gate(table):
        out = gate[table[0]]
        for p in range(1, len(table)):
            out = jnp.where(pair == p, gate[table[p]], out)
        return out

    n_tok = route_ref.shape[1]
    route_ref[0:1, :] = (gidx * len(PAIR_SLOT_A) + pair).astype(F32)
    ps = lax.broadcasted_iota(jnp.int32, (1, n_tok), 1) % N_PS
    cond = jnp.where(tile < SAMPLE_TILE, 0, 1 + ps // SEGS_PER_SAMPLE_SEQ).astype(F32)
    lanes = jnp.concatenate([slot_gate(PAIR_SLOT_A), slot_gate(PAIR_SLOT_B), cond,
                             jnp.zeros((ROUTE_LANES - 3, n_tok), F32)], axis=0)
    hx_ref[:, 2 * D_MODEL:] = lanes.T


def _mix_out(x, yt, ug, vn, mod8, l, w_glu, w_s, b_s, w_out, norm2_w, rwt, rb):
    tm = TOKEN_TILE
    steps = TILE_TOKENS // tm
    tok = lambda n: pl.BlockSpec((tm, n), lambda t, s: (t * steps + s, 0))
    whole_tile = pl.BlockSpec((GMLP_HEADS, TILE_TOKENS, GMLP_HEAD_DIM), lambda t, s: (0, t, 0))
    lay = lambda *shape: pl.BlockSpec((None,) + shape, lambda t, s: (l,) + (0,) * len(shape))
    return pl.pallas_call(
        _post_kernel,
        grid=(N_TILES, steps),
        in_specs=[
            tok(D_MODEL),
            pl.BlockSpec((None, J_PER_STEP, D_SSM, J_ROWS), lambda t, s: (t, s, 0, 0)),
            whole_tile, whole_tile,
            pl.BlockSpec((None, None, N_MOD, N_PS, D_MODEL), lambda t, s: (l, t, 0, 0, 0)),
            lay(D_SSM, D_SSM), lay(GMLP_HEADS, GMLP_CHUNK, GMLP_CHUNK),
            lay(GMLP_HEADS, GMLP_CHUNK, 1), lay(D_MODEL, D_MODEL), lay(1, D_MODEL),
            pl.BlockSpec((N_EXPERTS, D_MODEL), lambda t, s: (0, 0)),
            pl.BlockSpec((N_EXPERTS, 1), lambda t, s: (0, 0)),
        ],
        out_specs=[tok(HX_WIDTH), pl.BlockSpec((1, tm), lambda t, s: (0, t * steps + s))],
        out_shape=[
            jax.ShapeDtypeStruct((T_ALL, HX_WIDTH), F32),
            jax.ShapeDtypeStruct((1, T_ALL), F32),
        ],
        scratch_shapes=[pltpu.VMEM((GMLP_HEADS, TILE_TOKENS, GMLP_HEAD_DIM), F32), pltpu.VMEM((tm, D_SSM), F32),
                        pltpu.VMEM((D_SSM, D_SSM), BF16), pltpu.VMEM((D_MODEL, D_MODEL), BF16)],
        compiler_params=_params("arbitrary", "arbitrary"),
        name="mixers_out_router",
    )(x, yt, ug, vn, mod8, w_glu, w_s, b_s, w_out, norm2_w.reshape(DEPTH, 1, D_MODEL), rwt, rb)


def _moe_kernel(ea_ref, eb_ref, new_a_ref, new_b_ref, nv_ref, half_ref, src_ref, drow_ref,
                hx_hbm, wga_ref, wua_ref, wda_ref, wgb_ref, wub_ref, wdb_ref, g2_ref, fw_ref,
                o_hbm, wga_s, wua_s, wda_s, wgb_s, wub_s, wdb_s, hx_buf, o_buf, z_buf, g_sem, s_sem, z_sem,
                *, final):
    n = pl.program_id(0)
    n_valid = nv_ref[0]
    slot = n % 2

    def by_size(tile, fn):
        @pl.when(half_ref[tile] == 1)
        def _():
            fn(MOE_TM // 2)

        @pl.when(half_ref[tile] == 0)
        def _():
            fn(MOE_TM)

    def start_gather(tile, sl, rows):
        base = tile * MOE_TM
        for r in range(rows):
            pltpu.make_async_copy(hx_hbm.at[pl.ds(src_ref[base + r], 1)], hx_buf.at[sl, pl.ds(r, 1)],
                                  g_sem.at[sl]).start()

    def wait_gather(sl, rows):
        pltpu.make_async_copy(hx_hbm.at[pl.ds(0, rows)], hx_buf.at[sl, pl.ds(0, rows)], g_sem.at[sl]).wait()

    def start_scatter(tile, sl, rows):
        base = tile * MOE_TM
        for r in range(rows):
            pltpu.make_async_copy(o_buf.at[sl, pl.ds(r, 1)], o_hbm.at[pl.ds(drow_ref[base + r], 1)],
                                  s_sem.at[sl]).start()

    def wait_scatter(sl, rows):
        pltpu.make_async_copy(o_buf.at[sl, pl.ds(0, rows)], o_hbm.at[pl.ds(0, rows)], s_sem.at[sl]).wait()

    @pl.when(n == 0)
    def _():
        z_buf[...] = jnp.zeros_like(z_buf)
        spare = [pltpu.make_async_copy(z_buf, o_hbm.at[pl.ds(T_ALL + i * MOE_TM, MOE_TM)], z_sem)
                 for i in range((MOE_ROWS - T_ALL) // MOE_TM)]
        for cp in spare:
            cp.start()
        for cp in spare:
            cp.wait()
        by_size(0, lambda rows: start_gather(0, 0, rows))

    @pl.when(new_a_ref[n] == 1)
    def _():
        wga_s[...] = wga_ref[...].astype(BF16)
        wua_s[...] = wua_ref[...].astype(BF16)
        wda_s[...] = wda_ref[...].astype(BF16)

    @pl.when(new_b_ref[n] == 1)
    def _():
        wgb_s[...] = wgb_ref[...].astype(BF16)
        wub_s[...] = wub_ref[...].astype(BF16)
        wdb_s[...] = wdb_ref[...].astype(BF16)

    @pl.when(n + 1 < n_valid)
    def _():
        by_size(n + 1, lambda rows: start_gather(n + 1, 1 - slot, rows))

    def tile_step(rows):
        wait_gather(slot, rows)

        @pl.when(n >= 2)
        def _():
            by_size(n - 2, lambda r: wait_scatter(slot, r))

        h = hx_buf[slot, 0:rows, 0:D_MODEL].astype(BF16)
        lanes = hx_buf[slot, 0:rows, 2 * D_MODEL:]

        def expert(wg, wu, wd, gate):
            hg = jnp.dot(h, wg[...], preferred_element_type=F32)
            hu = jnp.dot(h, wu[...], preferred_element_type=F32)
            act = hg * _sigmoid(hg) * hu * gate
            return jnp.dot(act.astype(BF16), wd[...], preferred_element_type=F32)

        y = expert(wga_s, wua_s, wda_s, lanes[:, 0:1]) + expert(wgb_s, wub_s, wdb_s, lanes[:, 1:2])
        cond_row = lanes[:, 2:3]
        gate2 = jnp.where(cond_row == 0.0, g2_ref[0:1, :], jnp.where(cond_row == 1.0, g2_ref[1:2, :], g2_ref[2:3, :]))
        x2 = hx_buf[slot, 0:rows, D_MODEL:2 * D_MODEL] + gate2 * y
        o_buf[slot, 0:rows] = _rmsnorm(x2, fw_ref[...]) if final else x2
        start_scatter(n, slot, rows)

    @pl.when(n < n_valid)
    def _():
        by_size(n, tile_step)

    @pl.when(n == MOE_TILES - 1)
    def _():
        last = n_valid - 1
        by_size(last, lambda r: wait_scatter(last % 2, r))
        by_size(last - 1, lambda r: wait_scatter(1 - last % 2, r))


def _experts(tables, hx, mods, l, w_gate, w_up, w_down, final_w, final):
    w_a = lambda r, c: pl.BlockSpec((None, None, r, c), lambda n, ea, eb, *_: (l, ea[n], 0, 0))
    w_b = lambda r, c: pl.BlockSpec((None, None, r, c), lambda n, ea, eb, *_: (l, eb[n], 0, 0))
    up = pltpu.VMEM((D_MODEL, D_EXPERT), BF16)
    down = pltpu.VMEM((D_EXPERT, D_MODEL), BF16)
    return pl.pallas_call(
        functools.partial(_moe_kernel, final=final),
        grid_spec=pltpu.PrefetchScalarGridSpec(
            num_scalar_prefetch=len(tables),
            grid=(MOE_TILES,),
            in_specs=[
                pl.BlockSpec(memory_space=pl.ANY),
                w_a(D_MODEL, D_EXPERT), w_a(D_MODEL, D_EXPERT), w_a(D_EXPERT, D_MODEL),
                w_b(D_MODEL, D_EXPERT), w_b(D_MODEL, D_EXPERT), w_b(D_EXPERT, D_MODEL),
                pl.BlockSpec((None, None, MOD_ROWS, D_MODEL), lambda n, *_: (l, N_MOD - 1, 0, 0)),
                pl.BlockSpec((1, D_MODEL), lambda n, *_: (0, 0)),
            ],
            out_specs=pl.BlockSpec(memory_space=pl.ANY),
            scratch_shapes=[up, up, down, up, up, down,
                            pltpu.VMEM((2, MOE_TM, HX_WIDTH), F32), pltpu.VMEM((2, MOE_TM, D_MODEL), F32),
                            pltpu.VMEM((MOE_TM, D_MODEL), F32),
                            pltpu.SemaphoreType.DMA((2,)), pltpu.SemaphoreType.DMA((2,)), pltpu.SemaphoreType.DMA(())],
        ),
        out_shape=jax.ShapeDtypeStruct((MOE_ROWS, D_MODEL), F32),
        compiler_params=_params("arbitrary"),
        name="experts",
    )(*tables, hx, w_gate, w_up, w_down, w_gate, w_up, w_down, mods, final_w.reshape(1, D_MODEL))


def _count_before(flags):
    n, k = flags.shape
    blocks = flags.reshape(n // 128, 128, k).astype(F32)
    strictly_lower = jnp.tril(jnp.ones((128, 128), F32), -1)
    within = jnp.einsum("ij,bjk->bik", strictly_lower, blocks)
    totals = jnp.sum(blocks, axis=1)
    before = jnp.cumsum(totals, axis=0) - totals
    return (within + before[:, None, :]).reshape(n, k).astype(jnp.int32)


def _routing_tables(cls, to_sequence_order):
    n_cls = N_EXPERT_GROUPS * len(PAIR_SLOT_A)
    onehot = (cls[:, None] == jnp.arange(n_cls, dtype=jnp.int32)[None, :]).astype(jnp.int32)
    counts = jnp.sum(onehot, axis=0)
    tiles = (counts + MOE_TM - 1) // MOE_TM
    tile_end = jnp.cumsum(tiles)
    row_in_class = (tile_end - tiles)[None, :] * MOE_TM + _count_before(onehot)
    dst = jnp.sum(onehot * row_in_class, axis=1)
    token_plus_1 = jnp.zeros((MOE_ROWS,), jnp.int32).at[dst].set(jnp.arange(1, T_ALL + 1, dtype=jnp.int32))
    is_pad = token_plus_1 == 0
    src = jnp.maximum(token_plus_1 - 1, 0)
    spare = T_ALL + _count_before(is_pad.astype(jnp.int32)[:, None])[:, 0]
    if to_sequence_order:
        j, c, ps = (src // J_ROWS) % SCAN_CHUNK, (src // N_PS) % SEG_CHUNKS, src % N_PS
        target = (src // TILE_TOKENS) * TILE_TOKENS + ps * SEG_LEN + c * SCAN_CHUNK + j
    else:
        target = src
    drow = jnp.where(is_pad, spare, target).astype(jnp.int32)
    n_valid = tile_end[-1]
    tile_id = jnp.minimum(jnp.arange(MOE_TILES, dtype=jnp.int32), n_valid - 1)
    tile_cls = jnp.sum((tile_end[None, :] <= tile_id[:, None]).astype(jnp.int32), axis=1)
    group, pair = tile_cls // len(PAIR_SLOT_A), tile_cls % len(PAIR_SLOT_A)
    in_cls = (tile_cls[:, None] == jnp.arange(n_cls, dtype=jnp.int32)[None, :]).astype(jnp.int32)
    left = jnp.sum(in_cls * (counts[None, :] - (tile_id[:, None] - (tile_end - tiles)[None, :]) * MOE_TM), axis=1)
    half = (left <= MOE_TM // 2).astype(jnp.int32)

    def slot_expert(table):
        local = jnp.full_like(pair, table[-1])
        for p in range(len(table) - 1):
            local = jnp.where(pair == p, table[p], local)
        return group * EXPERTS_PER_GROUP + local

    e_a = slot_expert(PAIR_SLOT_A)
    e_b = slot_expert(PAIR_SLOT_B)
    first = jnp.ones((1,), jnp.int32)
    new_a = jnp.concatenate([first, (e_a[1:] != e_a[:-1]).astype(jnp.int32)])
    new_b = jnp.concatenate([first, (e_b[1:] != e_b[:-1]).astype(jnp.int32)])
    return e_a, e_b, new_a, new_b, n_valid.reshape(1).astype(jnp.int32), half, src, drow


def _dirs_on_lanes(p):
    p = jnp.moveaxis(p, 1, -2)
    return p.reshape(p.shape[:-2] + (2 * SSM_STATE,))


def _to_internal_order(x):
    x = x.reshape(N_TILES, N_PS, SEG_CHUNKS, SCAN_CHUNK, D_MODEL)
    return x.transpose(0, 3, 2, 1, 4).reshape(T_ALL, D_MODEL)


def _gmlp_position_order(w):
    n_lo = GMLP_CHUNK // SCAN_CHUNK
    lead = w.shape[:2]
    w = w.reshape(lead + (n_lo, SCAN_CHUNK) + w.shape[3:])
    w = jnp.swapaxes(w, 2, 3)
    return w.reshape(lead + (GMLP_CHUNK,) + w.shape[4:])


def kernel(x_prompt, x_sample, c, state_ssm_re, state_ssm_im, c_ctx, norm1_w, norm2_w, w_mod, b_mod, w_in,
           ssm_a_re, ssm_a_im, ssm_log_dt, ssm_b_re, ssm_b_im, ssm_c_re, ssm_c_im, ssm_d, w_glu,
           gmlp_ln_w, gmlp_ln_b, gmlp_w_s, gmlp_b_s, w_out, router_w, router_b, w_gate, w_up, w_down,
           final_norm_w):
    x = jnp.concatenate([x_prompt.reshape(T_PROMPT, D_MODEL), x_sample.reshape(T_SAMPLE, D_MODEL)], axis=0)
    x = _to_internal_order(x)

    cvec = jnp.concatenate([c_ctx[None, :], c, jnp.zeros((MOD_ROWS - 1 - N_SAMPLE_SEQ, D_MODEL), F32)], axis=0)
    mods = _modulation(cvec, w_mod, b_mod)
    ps_row = [[0] * N_PS] * SAMPLE_TILE + [[1 + p // SEGS_PER_SAMPLE_SEQ for p in range(N_PS)]]
    mod8 = mods[:, :, jnp.array(ps_row, jnp.int32), :].transpose(0, 2, 1, 3, 4)

    log_dt = jnp.broadcast_to(ssm_log_dt[..., None], ssm_a_re.shape)
    d_lanes = jnp.tile(ssm_d.reshape(DEPTH, N_GROUPS, SSM_GROUP), (1, 1, 128 // SSM_GROUP))
    vecs = jnp.stack([_dirs_on_lanes(ssm_a_re), _dirs_on_lanes(ssm_a_im), _dirs_on_lanes(log_dt), d_lanes], axis=2)
    vecs = jnp.concatenate([vecs, jnp.zeros((DEPTH, N_GROUPS, 4, 2 * SSM_STATE), F32)], axis=2)
    mats = jnp.stack([_dirs_on_lanes(jnp.swapaxes(ssm_b_re, -1, -2)), _dirs_on_lanes(jnp.swapaxes(ssm_b_im, -1, -2)),
                      _dirs_on_lanes(ssm_c_re), _dirs_on_lanes(ssm_c_im)], axis=2)
    wft, cct, ttt, a16 = _ssm_operators(vecs, mats)

    w_s = jnp.swapaxes(_gmlp_position_order(jnp.swapaxes(_gmlp_position_order(gmlp_w_s), 2, 3)), 2, 3).astype(BF16)
    b_s = _gmlp_position_order(gmlp_b_s)[..., None]
    rwt = router_w.T
    rb = router_b.reshape(N_EXPERTS, 1)

    new_re, new_im = [], []
    for l in range(DEPTH):
        ug, vn, xt = _input_proj(x, mod8, l, norm1_w, w_in, gmlp_ln_w, gmlp_ln_b)
        s0 = jnp.concatenate([state_ssm_re[:, l].transpose(2, 0, 1, 3).reshape(N_GROUPS, N_SAMPLE_SEQ, 128),
                              state_ssm_im[:, l].transpose(2, 0, 1, 3).reshape(N_GROUPS, N_SAMPLE_SEQ, 128)], axis=-1)
        s0 = jnp.repeat(s0, SEGS_PER_SAMPLE_SEQ, axis=1)
        yt, fin = _ssm_scan(xt, wft, cct, ttt, a16, s0, l)
        fin = fin[:SAMPLE_TILE].reshape(SAMPLE_TILE, N_GROUPS, N_PS, 2, 2, SSM_STATE)
        fin = fin.transpose(3, 0, 2, 4, 1, 5).reshape(2, N_PROMPT_SEQ, 2, N_GROUPS, SSM_STATE)
        new_re.append(fin[0])
        new_im.append(fin[1])
        hx, route = _mix_out(x, yt, ug, vn, mod8, l, w_glu, w_s, b_s, w_out, norm2_w, rwt, rb)
        final = l == DEPTH - 1
        tables = _routing_tables(route[0].astype(jnp.int32), to_sequence_order=final)
        x = _experts(tables, hx, mods, l, w_gate, w_up, w_down, final_norm_w, final)

    y_prompt = x[:T_PROMPT].reshape(N_PROMPT_SEQ, PROMPT_LEN, D_MODEL)
    y_sample = x[T_PROMPT:T_ALL].reshape(N_SAMPLE_SEQ, SAMPLE_LEN, D_MODEL)
    return (y_prompt, y_sample, jnp.stack(new_re, axis=1), jnp.stack(new_im, axis=1))
```

```python
import functools
import math

import jax
import jax.numpy as jnp
from jax import lax
from jax.experimental import pallas as pl
from jax.experimental.pallas import tpu as pltpu

F32 = jnp.float32
BF16 = jnp.bfloat16

D_MODEL = 1024
N_PROMPT_SEQ = 16
PROMPT_LEN = 256
N_SAMPLE_SEQ = 2
SAMPLE_LEN = 1024
T_PROMPT = N_PROMPT_SEQ * PROMPT_LEN
T_SAMPLE = N_SAMPLE_SEQ * SAMPLE_LEN
T_ALL = T_PROMPT + T_SAMPLE
DEPTH = 2
D_SSM = 512
SSM_GROUP = 16
N_GROUPS = 32
SSM_STATE = 64
D_GMLP = 512
GMLP_HEADS = 4
GMLP_HEAD_DIM = 128
GMLP_CHUNK = 128
N_EXPERTS = 16
N_EXPERT_GROUPS = 4
EXPERTS_PER_GROUP = 4
D_EXPERT = 512
N_MOD = 6
EPS = 1e-6

SCAN_CHUNK = 16
CHUNK_WIDTH = SCAN_CHUNK * SSM_GROUP
SEG_LEN = 256
SEG_CHUNKS = SEG_LEN // SCAN_CHUNK
N_PS = 8
TILE_TOKENS = N_PS * SEG_LEN
N_TILES = T_ALL // TILE_TOKENS
SAMPLE_TILE = T_PROMPT // TILE_TOKENS
SEGS_PER_SAMPLE_SEQ = SAMPLE_LEN // SEG_LEN
J_ROWS = SEG_CHUNKS * N_PS
GROUP_BLOCK = 8
PREP_GROUPS = 4
MOD_ROWS = 8
MODS_PER_STEP = 2

J_PER_STEP = 4
TOKEN_TILE = J_PER_STEP * J_ROWS
PAIR_SLOT_A = (0, 0, 0, 1, 1, 3)
PAIR_SLOT_B = (1, 2, 3, 3, 2, 2)
MOE_TM = 128
TAIL_STEPS = 4
MOE_TILES = T_ALL // MOE_TM + N_EXPERT_GROUPS * len(PAIR_SLOT_A)
MOE_ROWS = MOE_TILES * MOE_TM
ROUTE_LANES = 128
HX_WIDTH = 2 * D_MODEL + ROUTE_LANES
VMEM_LIMIT = 56 * 1024 * 1024
TRANS_B = (((1,), (1,)), ((), ()))


def _sigmoid(x):
    return 1.0 / (1.0 + jnp.exp(-x))


def _gelu_tanh(x):
    c = math.sqrt(2.0 / math.pi)
    return x * (0.5 * (1.0 + jnp.tanh(c * (x + 0.044715 * (x * x * x)))))


def _split_bf16(a):
    hi = a.astype(BF16)
    return hi, (a - hi.astype(F32)).astype(BF16)


def _rmsnorm(x, w):
    return x * lax.rsqrt(jnp.mean(x * x, axis=-1, keepdims=True) + EPS) * w


def _per_ps(fn, a, *mods):
    rows, d = a.shape
    out = fn(a.reshape(rows // N_PS, N_PS, d), *[m[None] for m in mods])
    return out.reshape(rows, d)


def _params(*sem):
    return pltpu.CompilerParams(dimension_semantics=sem, vmem_limit_bytes=VMEM_LIMIT)


def _mod_kernel(c_ref, w_top_ref, w_bottom_ref, b_ref, o_ref):
    c = c_ref[...]
    s = (c * _sigmoid(c)).astype(BF16)
    k = D_MODEL // 2
    res = jnp.dot(s[:, :k], w_top_ref[...].astype(BF16), preferred_element_type=F32)
    res += jnp.dot(s[:, k:], w_bottom_ref[...].astype(BF16), preferred_element_type=F32)
    for m in range(MODS_PER_STEP):
        o_ref[m] = res[:, m * D_MODEL:(m + 1) * D_MODEL] + b_ref[m]


def _modulation(cvec, w_mod, b_mod):
    mp = MODS_PER_STEP
    return pl.pallas_call(
        _mod_kernel,
        grid=(DEPTH, N_MOD // mp),
        in_specs=[
            pl.BlockSpec((MOD_ROWS, D_MODEL), lambda l, n: (0, 0)),
            pl.BlockSpec((None, D_MODEL // 2, mp * D_MODEL), lambda l, n: (l, 0, n)),
            pl.BlockSpec((None, D_MODEL // 2, mp * D_MODEL), lambda l, n: (l, 1, n)),
            pl.BlockSpec((None, mp, 1, D_MODEL), lambda l, n: (l, n, 0, 0)),
        ],
        out_specs=pl.BlockSpec((None, mp, MOD_ROWS, D_MODEL), lambda l, n: (l, n, 0, 0)),
        out_shape=jax.ShapeDtypeStruct((DEPTH, N_MOD, MOD_ROWS, D_MODEL), F32),
        compiler_params=_params("arbitrary", "arbitrary"),
        name="adaln_mod",
    )(cvec, w_mod, w_mod, b_mod.reshape(DEPTH, N_MOD, 1, D_MODEL))


def _first_step():
    return jnp.logical_and(pl.program_id(0) == 0, pl.program_id(1) == 0)


def _in_kernel(x_ref, mod_ref, nw_ref, w_ref, lnw_ref, lnb_ref, ug_ref, vn_ref, xt_ref, wg_ref, wst_ref):
    @pl.when(_first_step())
    def _():
        wg_ref[...] = w_ref[:, D_SSM:].astype(BF16)
        wst_ref[...] = w_ref[:, :D_SSM].T.astype(BF16)

    y = _rmsnorm(x_ref[...], nw_ref[...])
    h = _per_ps(lambda a, sc, sh: a * (1.0 + sc) + sh, y, mod_ref[1], mod_ref[0]).astype(BF16)
    zg = _gelu_tanh(jnp.dot(h, wg_ref[...], preferred_element_type=F32))
    v = zg[:, D_GMLP:]
    mu = jnp.mean(v, axis=-1, keepdims=True)
    vc = v - mu
    var = jnp.mean(vc * vc, axis=-1, keepdims=True)
    vn = vc * lax.rsqrt(var + EPS) * lnw_ref[...] + lnb_ref[...]
    for hd in range(GMLP_HEADS):
        cols = slice(hd * GMLP_HEAD_DIM, (hd + 1) * GMLP_HEAD_DIM)
        ug_ref[hd] = zg[:, cols]
        vn_ref[hd] = vn[:, cols]
    xt = lax.dot_general(wst_ref[...], h, TRANS_B, preferred_element_type=F32).astype(BF16)
    for k in range(J_PER_STEP):
        xt_ref[k] = xt[:, k * J_ROWS:(k + 1) * J_ROWS]


def _input_proj(x, mod8, l, norm1_w, w_in, ln_w, ln_b):
    tm = TOKEN_TILE
    steps = TILE_TOKENS // tm
    tok = lambda n: pl.BlockSpec((tm, n), lambda t, s: (t * steps + s, 0))
    heads = pl.BlockSpec((GMLP_HEADS, tm, GMLP_HEAD_DIM), lambda t, s: (0, t * steps + s, 0))
    lay = lambda *shape: pl.BlockSpec((None,) + shape, lambda t, s: (l,) + (0,) * len(shape))
    return pl.pallas_call(
        _in_kernel,
        grid=(N_TILES, steps),
        in_specs=[
            tok(D_MODEL),
            pl.BlockSpec((None, None, N_MOD, N_PS, D_MODEL), lambda t, s: (l, t, 0, 0, 0)),
            lay(1, D_MODEL), lay(D_MODEL, D_SSM + 2 * D_GMLP), lay(1, D_GMLP), lay(1, D_GMLP),
        ],
        out_specs=[heads, heads,
                   pl.BlockSpec((None, J_PER_STEP, D_SSM, J_ROWS), lambda t, s: (t, s, 0, 0))],
        out_shape=[jax.ShapeDtypeStruct((GMLP_HEADS, T_ALL, GMLP_HEAD_DIM), F32)] * 2 + [
            jax.ShapeDtypeStruct((N_TILES, SCAN_CHUNK, D_SSM, J_ROWS), BF16)],
        scratch_shapes=[pltpu.VMEM((D_MODEL, 2 * D_GMLP), BF16), pltpu.VMEM((D_SSM, D_MODEL), BF16)],
        compiler_params=_params("arbitrary", "arbitrary"),
        name="norm1_in_proj",
    )(x, mod8, norm1_w.reshape(DEPTH, 1, D_MODEL), w_in,
      ln_w.reshape(DEPTH, 1, D_GMLP), ln_b.reshape(DEPTH, 1, D_GMLP))


def _shift_lanes_right(a, b, s, lane):
    if s == 0:
        return a, b
    if s == 128:
        return jnp.zeros_like(a), a
    if s < 128:
        ra = pltpu.roll(a, s, 1)
        rb = pltpu.roll(b, s, 1)
        return jnp.where(lane >= s, ra, 0.0), jnp.where(lane >= s, rb, ra)
    t = s - 128
    return jnp.zeros_like(a), jnp.where(lane >= t, pltpu.roll(a, t, 1), 0.0)


def _shift_lanes_left(a, b, s, lane):
    if s == 0:
        return a, b
    if s == 128:
        return b, jnp.zeros_like(b)
    if s < 128:
        ra = pltpu.roll(a, 128 - s, 1)
        rb = pltpu.roll(b, 128 - s, 1)
        return jnp.where(lane < 128 - s, ra, rb), jnp.where(lane < 128 - s, rb, 0.0)
    t = s - 128
    return jnp.where(lane < 128 - t, pltpu.roll(b, 128 - t, 1), 0.0), jnp.zeros_like(b)


def _prep_kernel(*refs):
    for g in range(PREP_GROUPS):
        _prep_group(*[r.at[g] for r in refs])


def _prep_group(vec_ref, mat_ref, wft_ref, cct_ref, ttt_ref, a16_ref, wf_scr, cm_scr, tt_scr):
    a_re = vec_ref[0:1, :]
    a_im = vec_ref[1:2, :]
    dt = jnp.exp(vec_ref[2:3, :])
    d_skip = vec_ref[3:4, :]
    mag = jnp.exp(a_re * dt)
    ang = a_im * dt
    ab_r = mag * jnp.cos(ang)
    ab_i = mag * jnp.sin(ang)
    den = a_re * a_re + a_im * a_im
    nr = ab_r - 1.0
    q_r = (nr * a_re + ab_i * a_im) / den
    q_i = (ab_i * a_re - nr * a_im) / den
    bt_r = mat_ref[0]
    bt_i = mat_ref[1]
    c_r = mat_ref[2]
    c_i = mat_ref[3]
    bb_r = q_r * bt_r - q_i * bt_i
    bb_i = q_r * bt_i + q_i * bt_r
    p_r = [jnp.ones_like(ab_r)]
    p_i = [jnp.zeros_like(ab_r)]
    for _ in range(SCAN_CHUNK):
        pr, pi = p_r[-1], p_i[-1]
        p_r.append(pr * ab_r - pi * ab_i)
        p_i.append(pr * ab_i + pi * ab_r)
    a16_ref[0] = jnp.broadcast_to(p_r[SCAN_CHUNK], (N_PS, 128))
    a16_ref[1] = jnp.broadcast_to(p_i[SCAN_CHUNK], (N_PS, 128))

    lane = lax.broadcasted_iota(jnp.int32, (1, 128), 1)
    is_fwd = lane < SSM_STATE

    def pick(mf, mb):
        return jnp.where(is_fwd, p_r[mf], p_r[mb]), jnp.where(is_fwd, p_i[mf], p_i[mb])

    for j in range(SCAN_CHUNK):
        rows = slice(j * SSM_GROUP, (j + 1) * SSM_GROUP)
        wr, wi = pick(SCAN_CHUNK - 1 - j, j)
        wf_scr[rows, 0:128] = bb_r * wr - bb_i * wi
        wf_scr[rows, 128:256] = bb_r * wi + bb_i * wr
        wr, wi = pick(j + 1, SCAN_CHUNK - j)
        cct_ref[rows, 0:128] = (c_r * wr - c_i * wi).astype(BF16)
        cct_ref[rows, 128:256] = (-(c_r * wi + c_i * wr)).astype(BF16)
        wr, wi = pick(j, SCAN_CHUNK - 1 - j)
        cm_scr[rows, 0:128] = c_r * wr - c_i * wi
        cm_scr[rows, 128:256] = c_r * wi + c_i * wr
    wft_ref[...] = wf_scr[...].T.astype(BF16)

    zero = jnp.zeros_like(bb_r)
    cm_hi, cm_lo = _split_bf16(cm_scr[...])
    dot_t = lambda a, b: lax.dot_general(a, b, TRANS_B, preferred_element_type=F32)

    def lag_rows(keep):
        lhs = jnp.concatenate([jnp.where(keep, bb_r, zero), jnp.where(keep, -bb_i, zero)], axis=1)
        hi, lo = _split_bf16(lhs)
        return dot_t(hi, cm_hi) + (dot_t(hi, cm_lo) + dot_t(lo, cm_hi))

    mf = lag_rows(is_fwd)
    mb = lag_rows(jnp.logical_not(is_fwd))
    mf_a, mf_b = mf[:, 0:128], mf[:, 128:256]
    mb_a, mb_b = mb[:, 0:128], mb[:, 128:256]
    row_h = lax.broadcasted_iota(jnp.int32, (SSM_GROUP, 128), 0)
    lane_h = lax.broadcasted_iota(jnp.int32, (SSM_GROUP, 128), 1)
    for jp in range(SCAN_CHUNK):
        rows = slice(jp * SSM_GROUP, (jp + 1) * SSM_GROUP)
        fa, fb = _shift_lanes_right(mf_a, mf_b, SSM_GROUP * jp, lane)
        ba, bb = _shift_lanes_left(mb_a, mb_b, SSM_GROUP * (SCAN_CHUNK - 1 - jp), lane)
        diag = SSM_GROUP * jp + row_h
        tt_scr[rows, 0:128] = fa + ba + jnp.where(lane_h == diag, d_skip, 0.0)
        tt_scr[rows, 128:256] = fb + bb + jnp.where(lane_h + 128 == diag, d_skip, 0.0)
    ttt_ref[...] = tt_scr[...].T.astype(BF16)


def _ssm_operators(vecs, mats):
    op = jax.ShapeDtypeStruct((DEPTH, N_GROUPS, CHUNK_WIDTH, CHUNK_WIDTH), BF16)
    pg = PREP_GROUPS
    op_spec = pl.BlockSpec((None, pg, CHUNK_WIDTH, CHUNK_WIDTH), lambda l, g: (l, g, 0, 0))
    sq = pltpu.VMEM((pg, CHUNK_WIDTH, CHUNK_WIDTH), F32)
    return pl.pallas_call(
        _prep_kernel,
        grid=(DEPTH, N_GROUPS // pg),
        in_specs=[
            pl.BlockSpec((None, pg, 8, 128), lambda l, g: (l, g, 0, 0)),
            pl.BlockSpec((None, pg, 4, SSM_GROUP, 128), lambda l, g: (l, g, 0, 0, 0)),
        ],
        out_specs=[op_spec, op_spec, op_spec,
                   pl.BlockSpec((None, pg, 2, N_PS, 128), lambda l, g: (l, g, 0, 0, 0))],
        out_shape=[op, op, op, jax.ShapeDtypeStruct((DEPTH, N_GROUPS, 2, N_PS, 128), F32)],
        scratch_shapes=[sq, sq, sq],
        compiler_params=_params("arbitrary", "arbitrary"),
        name="s5_operators",
    )(vecs, mats)


def _ssm_kernel(xt_ref, wft_ref, cct_ref, ttt_ref, a16_ref, s0_ref, yt_ref, fin_ref, s_scr, f_scr, ft_scr):
    tile = pl.program_id(0)
    lane = lax.broadcasted_iota(jnp.int32, (GROUP_BLOCK, N_PS, 128), 2)
    is_fwd = lane < SSM_STATE
    half = SSM_STATE

    def group_x(gl):
        return xt_ref[:, gl * SSM_GROUP:(gl + 1) * SSM_GROUP, :].reshape(CHUNK_WIDTH, J_ROWS)

    for gl in range(GROUP_BLOCK):
        ft_scr[gl] = jnp.dot(wft_ref[gl], group_x(gl), preferred_element_type=F32)
        f_scr[gl] = ft_scr[gl].T

    a_r = a16_ref[:, 0]
    a_i = a16_ref[:, 1]

    def scan(s_r, s_i):
        for i in range(SEG_CHUNKS):
            rf = slice(i * N_PS, (i + 1) * N_PS)
            rb = slice((SEG_CHUNKS - 1 - i) * N_PS, (SEG_CHUNKS - i) * N_PS)
            s_scr[:, rf, 0:half] = s_r[:, :, 0:half]
            s_scr[:, rf, 128:128 + half] = s_i[:, :, 0:half]
            s_scr[:, rb, half:128] = s_r[:, :, half:128]
            s_scr[:, rb, 128 + half:256] = s_i[:, :, half:128]
            f_r = jnp.where(is_fwd, f_scr[:, rf, 0:128], f_scr[:, rb, 0:128])
            f_i = jnp.where(is_fwd, f_scr[:, rf, 128:256], f_scr[:, rb, 128:256])
            s_r, s_i = a_r * s_r - a_i * s_i + f_r, a_r * s_i + a_i * s_r + f_i
        return s_r, s_i

    zeros = jnp.zeros((GROUP_BLOCK, N_PS, 128), F32)
    z_r, z_i = scan(zeros, zeros)
    fin_ref[:, :, 0:128] = z_r
    fin_ref[:, :, 128:256] = z_i

    @pl.when(tile == SAMPLE_TILE)
    def _():
        b_r, b_i = a_r, a_i
        for _ in range(4):
            b_r, b_i = b_r * b_r - b_i * b_i, 2.0 * (b_r * b_i)
        seg = lax.broadcasted_iota(jnp.int32, (GROUP_BLOCK, N_PS, 128), 1) % SEGS_PER_SAMPLE_SEQ
        i_r = s0_ref[:, :, 0:128]
        i_i = s0_ref[:, :, 128:256]
        for step in range(1, SEGS_PER_SAMPLE_SEQ):
            pr = jnp.where(is_fwd, pltpu.roll(i_r, 1, 1), pltpu.roll(i_r, N_PS - 1, 1))
            pi = jnp.where(is_fwd, pltpu.roll(i_i, 1, 1), pltpu.roll(i_i, N_PS - 1, 1))
            zr = jnp.where(is_fwd, pltpu.roll(z_r, 1, 1), pltpu.roll(z_r, N_PS - 1, 1))
            zi = jnp.where(is_fwd, pltpu.roll(z_i, 1, 1), pltpu.roll(z_i, N_PS - 1, 1))
            n_r = b_r * pr - b_i * pi + zr
            n_i = b_r * pi + b_i * pr + zi
            first = jnp.where(is_fwd, step, 0)
            last = jnp.where(is_fwd, SEGS_PER_SAMPLE_SEQ - 1, SEGS_PER_SAMPLE_SEQ - 1 - step)
            upd = jnp.logical_and(seg >= first, seg <= last)
            i_r = jnp.where(upd, n_r, i_r)
            i_i = jnp.where(upd, n_i, i_i)
        scan(i_r, i_i)

    for gl in range(GROUP_BLOCK):
        yt = jnp.dot(ttt_ref[gl], group_x(gl), preferred_element_type=F32)
        yt += lax.dot_general(cct_ref[gl], s_scr[gl].astype(BF16), TRANS_B, preferred_element_type=F32)
        yt_ref[:, gl * SSM_GROUP:(gl + 1) * SSM_GROUP, :] = yt.reshape(SCAN_CHUNK, SSM_GROUP, J_ROWS)


def _ssm_scan(xt, wft, cct, ttt, a16, s0, l):
    gb = GROUP_BLOCK
    op_spec = pl.BlockSpec((None, gb, CHUNK_WIDTH, CHUNK_WIDTH), lambda t, g: (l, g, 0, 0))
    io_spec = pl.BlockSpec((None, SCAN_CHUNK, gb * SSM_GROUP, J_ROWS), lambda t, g: (t, 0, g, 0))
    return pl.pallas_call(
        _ssm_kernel,
        grid=(N_TILES, N_GROUPS // gb),
        in_specs=[
            io_spec, op_spec, op_spec, op_spec,
            pl.BlockSpec((None, gb, 2, N_PS, 128), lambda t, g: (l, g, 0, 0, 0)),
            pl.BlockSpec((gb, N_PS, CHUNK_WIDTH), lambda t, g: (g, 0, 0)),
        ],
        out_specs=[io_spec, pl.BlockSpec((None, gb, N_PS, CHUNK_WIDTH), lambda t, g: (t, g, 0, 0))],
        out_shape=[
            jax.ShapeDtypeStruct((N_TILES, SCAN_CHUNK, D_SSM, J_ROWS), F32),
            jax.ShapeDtypeStruct((N_TILES, N_GROUPS, N_PS, CHUNK_WIDTH), F32),
        ],
        scratch_shapes=[pltpu.VMEM((gb, J_ROWS, CHUNK_WIDTH), F32), pltpu.VMEM((gb, J_ROWS, CHUNK_WIDTH), F32),
                        pltpu.VMEM((gb, CHUNK_WIDTH, J_ROWS), F32)],
        compiler_params=_params("arbitrary", "arbitrary"),
        name="s5_chunk_scan",
    )(xt, wft, cct, ttt, a16, s0)


def _post_kernel(x_ref, yt_ref, ug_ref, vn_ref, mod_ref, wglu_ref, ws_ref, bs_ref, wout_f32_ref,
                 nw_ref, rwt_ref, rb_ref, hx_ref, route_ref, yg_scr, y_scr, wglut_ref, wout_ref):
    tile = pl.program_id(0)
    step = pl.program_id(1)

    @pl.when(_first_step())
    def _():
        wglut_ref[...] = wglu_ref[...].T.astype(BF16)
        wout_ref[...] = wout_f32_ref[...].astype(BF16)

    @pl.when(step == 0)
    def _():
        def chunk(n, carry):
            ps = n % N_PS
            c_hi = n // N_PS
            base = c_hi * (GMLP_CHUNK // SCAN_CHUNK) * N_PS + ps
            rows = [pl.ds(j * J_ROWS + base, GMLP_CHUNK // SCAN_CHUNK, stride=N_PS) for j in range(SCAN_CHUNK)]
            for h in range(GMLP_HEADS):
                v = jnp.concatenate([vn_ref[h, r, :] for r in rows], axis=0).astype(BF16)
                u = jnp.concatenate([ug_ref[h, r, :] for r in rows], axis=0)
                s = jnp.dot(ws_ref[h], v, preferred_element_type=F32) + bs_ref[h]
                yg = u * s
                for j, r in enumerate(rows):
                    yg_scr[h, r, :] = yg[j * 8:(j + 1) * 8]
            return carry

        lax.fori_loop(0, TILE_TOKENS // GMLP_CHUNK, chunk, 0)

    yt = _gelu_tanh(jnp.concatenate([yt_ref[k] for k in range(J_PER_STEP)], axis=1))
    yt = yt * _sigmoid(jnp.dot(wglut_ref[...], yt.astype(BF16), preferred_element_type=F32))
    for k in range(J_PER_STEP):
        y_scr[k * J_ROWS:(k + 1) * J_ROWS, :] = yt[:, k * J_ROWS:(k + 1) * J_ROWS].T
    row0 = pl.multiple_of(step * TOKEN_TILE, TOKEN_TILE)
    proj = jnp.dot(y_scr[...].astype(BF16), wout_ref[0:D_SSM, :], preferred_element_type=F32)
    yg = jnp.concatenate([yg_scr[h, pl.ds(row0, TOKEN_TILE), :] for h in range(GMLP_HEADS)], axis=1)
    proj += jnp.dot(yg.astype(BF16), wout_ref[D_SSM:, :], preferred_element_type=F32)
    x1 = x_ref[...] + _per_ps(lambda a, g: a * g, proj, mod_ref[2])
    h2 = _per_ps(lambda a, sc, sh: a * (1.0 + sc) + sh, _rmsnorm(x1, nw_ref[...]), mod_ref[4], mod_ref[3])
    hx_ref[:, 0:D_MODEL] = h2
    hx_ref[:, D_MODEL:2 * D_MODEL] = x1
    r_hi, r_lo = _split_bf16(rwt_ref[...])
    h_hi, h_lo = _split_bf16(h2)
    dot_t = lambda a, b: lax.dot_general(a, b, TRANS_B, preferred_element_type=F32)
    logits = dot_t(r_hi, h_hi) + (dot_t(r_hi, h_lo) + dot_t(r_lo, h_hi))
    scores = _sigmoid(logits)
    sel = scores + rb_ref[...]
    sc = [scores[e:e + 1, :] for e in range(N_EXPERTS)]
    sl = [sel[e:e + 1, :] for e in range(N_EXPERTS)]
    gscore = []
    for g in range(N_EXPERT_GROUPS):
        v0, v1, v2, v3 = sl[4 * g:4 * g + 4]
        hi01, lo01 = jnp.maximum(v0, v1), jnp.minimum(v0, v1)
        hi23, lo23 = jnp.maximum(v2, v3), jnp.minimum(v2, v3)
        top1 = jnp.maximum(hi01, hi23)
        top2 = jnp.maximum(jnp.minimum(hi01, hi23), jnp.maximum(lo01, lo23))
        gscore.append(top1 + top2)
    best = gscore[0]
    gidx = jnp.zeros_like(best, dtype=jnp.int32)
    for g in range(1, N_EXPERT_GROUPS):
        upd = gscore[g] > best
        gidx = jnp.where(upd, g, gidx)
        best = jnp.where(upd, gscore[g], best)

    def in_group(vals, k):
        out = vals[k]
        for g in range(1, N_EXPERT_GROUPS):
            out = jnp.where(gidx == g, vals[4 * g + k], out)
        return out

    v = [in_group(sl, k) for k in range(EXPERTS_PER_GROUP)]
    s = [in_group(sc, k) for k in range(EXPERTS_PER_GROUP)]
    w = []
    bits = jnp.zeros_like(gidx)
    for k in range(EXPERTS_PER_GROUP):
        rank = jnp.zeros_like(gidx)
        for j in range(EXPERTS_PER_GROUP):
            if j == k:
                continue
            ahead = (v[j] >= v[k]) if j < k else (v[j] > v[k])
            rank = rank + ahead.astype(jnp.int32)
        w.append(jnp.where(rank < 2, s[k], 0.0))
        bits = bits + jnp.where(rank < 2, 1 << k, 0)
    denom = (w[0] + w[1]) + (w[2] + w[3])
    gate = [wk / denom for wk in w]
    pair = jnp.full_like(gidx, len(PAIR_SLOT_A) - 1)
    for p in range(len(PAIR_SLOT_A) - 1):
        pair = jnp.where(bits == (1 << PAIR_SLOT_A[p]) + (1 << PAIR_SLOT_B[p]), p, pair)

    def slot_gate(table):
        out = gate[table[0]]
        for p in range(1, len(table)):
            out = jnp.where(pair == p, gate[table[p]], out)
        return out

    n_tok = route_ref.shape[1]
    route_ref[0:1, :] = (gidx * len(PAIR_SLOT_A) + pair).astype(F32)
    ps = lax.broadcasted_iota(jnp.int32, (1, n_tok), 1) % N_PS
    cond = jnp.where(tile < SAMPLE_TILE, 0, 1 + ps // SEGS_PER_SAMPLE_SEQ).astype(F32)
    lanes = jnp.concatenate([slot_gate(PAIR_SLOT_A), slot_gate(PAIR_SLOT_B), cond,
                             jnp.zeros((ROUTE_LANES - 3, n_tok), F32)], axis=0)
    hx_ref[:, 2 * D_MODEL:] = lanes.T


def _mix_out(x, yt, ug, vn, mod8, l, w_glu, w_s, b_s, w_out, norm2_w, rwt, rb):
    tm = TOKEN_TILE
    steps = TILE_TOKENS // tm
    tok = lambda n: pl.BlockSpec((tm, n), lambda t, s: (t * steps + s, 0))
    whole_tile = pl.BlockSpec((GMLP_HEADS, TILE_TOKENS, GMLP_HEAD_DIM), lambda t, s: (0, t, 0))
    lay = lambda *shape: pl.BlockSpec((None,) + shape, lambda t, s: (l,) + (0,) * len(shape))
    return pl.pallas_call(
        _post_kernel,
        grid=(N_TILES, steps),
        in_specs=[
            tok(D_MODEL),
            pl.BlockSpec((None, J_PER_STEP, D_SSM, J_ROWS), lambda t, s: (t, s, 0, 0)),
            whole_tile, whole_tile,
            pl.BlockSpec((None, None, N_MOD, N_PS, D_MODEL), lambda t, s: (l, t, 0, 0, 0)),
            lay(D_SSM, D_SSM), lay(GMLP_HEADS, GMLP_CHUNK, GMLP_CHUNK),
            lay(GMLP_HEADS, GMLP_CHUNK, 1), lay(D_MODEL, D_MODEL), lay(1, D_MODEL),
            pl.BlockSpec((N_EXPERTS, D_MODEL), lambda t, s: (0, 0)),
            pl.BlockSpec((N_EXPERTS, 1), lambda t, s: (0, 0)),
        ],
        out_specs=[tok(HX_WIDTH), pl.BlockSpec((1, tm), lambda t, s: (0, t * steps + s))],
        out_shape=[
            jax.ShapeDtypeStruct((T_ALL, HX_WIDTH), F32),
            jax.ShapeDtypeStruct((1, T_ALL), F32),
        ],
        scratch_shapes=[pltpu.VMEM((GMLP_HEADS, TILE_TOKENS, GMLP_HEAD_DIM), F32), pltpu.VMEM((tm, D_SSM), F32),
                        pltpu.VMEM((D_SSM, D_SSM), BF16), pltpu.VMEM((D_MODEL, D_MODEL), BF16)],
        compiler_params=_params("arbitrary", "arbitrary"),
        name="mixers_out_router",
    )(x, yt, ug, vn, mod8, w_glu, w_s, b_s, w_out, norm2_w.reshape(DEPTH, 1, D_MODEL), rwt, rb)


def _moe_kernel(ea_ref, eb_ref, new_a_ref, new_b_ref, nv_ref, size_ref, src_ref, drow_ref,
                hx_hbm, wga_ref, wua_ref, wda_ref, wgb_ref, wub_ref, wdb_ref, g2_ref, fw_ref,
                o_hbm, wga_s, wua_s, wda_s, wgb_s, wub_s, wdb_s, hx_buf, o_buf, z_buf, g_sem, s_sem, z_sem,
                *, final):
    n = pl.program_id(0)
    n_valid = nv_ref[0]
    slot = n % 2

    def by_size(tile, fn):
        for quarters in range(1, TAIL_STEPS + 1):
            @pl.when(size_ref[tile] == quarters)
            def _(rows=quarters * (MOE_TM // TAIL_STEPS)):
                fn(rows)

    def start_gather(tile, sl, rows):
        base = tile * MOE_TM
        for r in range(rows):
            pltpu.make_async_copy(hx_hbm.at[pl.ds(src_ref[base + r], 1)], hx_buf.at[sl, pl.ds(r, 1)],
                                  g_sem.at[sl]).start()

    def wait_gather(sl, rows):
        pltpu.make_async_copy(hx_hbm.at[pl.ds(0, rows)], hx_buf.at[sl, pl.ds(0, rows)], g_sem.at[sl]).wait()

    def start_scatter(tile, sl, rows):
        base = tile * MOE_TM
        for r in range(rows):
            pltpu.make_async_copy(o_buf.at[sl, pl.ds(r, 1)], o_hbm.at[pl.ds(drow_ref[base + r], 1)],
                                  s_sem.at[sl]).start()

    def wait_scatter(sl, rows):
        pltpu.make_async_copy(o_buf.at[sl, pl.ds(0, rows)], o_hbm.at[pl.ds(0, rows)], s_sem.at[sl]).wait()

    @pl.when(n == 0)
    def _():
        z_buf[...] = jnp.zeros_like(z_buf)
        spare = [pltpu.make_async_copy(z_buf, o_hbm.at[pl.ds(T_ALL + i * MOE_TM, MOE_TM)], z_sem)
                 for i in range((MOE_ROWS - T_ALL) // MOE_TM)]
        for cp in spare:
            cp.start()
        for cp in spare:
            cp.wait()
        by_size(0, lambda rows: start_gather(0, 0, rows))

    @pl.when(new_a_ref[n] == 1)
    def _():
        wga_s[...] = wga_ref[...].astype(BF16)
        wua_s[...] = wua_ref[...].astype(BF16)
        wda_s[...] = wda_ref[...].astype(BF16)

    @pl.when(new_b_ref[n] == 1)
    def _():
        wgb_s[...] = wgb_ref[...].astype(BF16)
        wub_s[...] = wub_ref[...].astype(BF16)
        wdb_s[...] = wdb_ref[...].astype(BF16)

    @pl.when(n + 1 < n_valid)
    def _():
        by_size(n + 1, lambda rows: start_gather(n + 1, 1 - slot, rows))

    def tile_step(rows):
        wait_gather(slot, rows)

        @pl.when(n >= 2)
        def _():
            by_size(n - 2, lambda r: wait_scatter(slot, r))

        h = hx_buf[slot, 0:rows, 0:D_MODEL].astype(BF16)
        lanes = hx_buf[slot, 0:rows, 2 * D_MODEL:]

        def expert(wg, wu, wd, gate):
            hg = jnp.dot(h, wg[...], preferred_element_type=F32)
            hu = jnp.dot(h, wu[...], preferred_element_type=F32)
            act = hg * _sigmoid(hg) * hu * gate
            return jnp.dot(act.astype(BF16), wd[...], preferred_element_type=F32)

        y = expert(wga_s, wua_s, wda_s, lanes[:, 0:1]) + expert(wgb_s, wub_s, wdb_s, lanes[:, 1:2])
        cond_row = lanes[:, 2:3]
        gate2 = jnp.where(cond_row == 0.0, g2_ref[0:1, :], jnp.where(cond_row == 1.0, g2_ref[1:2, :], g2_ref[2:3, :]))
        x2 = hx_buf[slot, 0:rows, D_MODEL:2 * D_MODEL] + gate2 * y
        o_buf[slot, 0:rows] = _rmsnorm(x2, fw_ref[...]) if final else x2
        start_scatter(n, slot, rows)

    @pl.when(n < n_valid)
    def _():
        by_size(n, tile_step)

    @pl.when(n == MOE_TILES - 1)
    def _():
        last = n_valid - 1
        by_size(last, lambda r: wait_scatter(last % 2, r))
        by_size(last - 1, lambda r: wait_scatter(1 - last % 2, r))


def _experts(tables, hx, mods, l, w_gate, w_up, w_down, final_w, final):
    w_a = lambda r, c: pl.BlockSpec((None, None, r, c), lambda n, ea, eb, *_: (l, ea[n], 0, 0))
    w_b = lambda r, c: pl.BlockSpec((None, None, r, c), lambda n, ea, eb, *_: (l, eb[n], 0, 0))
    up = pltpu.VMEM((D_MODEL, D_EXPERT), BF16)
    down = pltpu.VMEM((D_EXPERT, D_MODEL), BF16)
    return pl.pallas_call(
        functools.partial(_moe_kernel, final=final),
        grid_spec=pltpu.PrefetchScalarGridSpec(
            num_scalar_prefetch=len(tables),
            grid=(MOE_TILES,),
            in_specs=[
                pl.BlockSpec(memory_space=pl.ANY),
                w_a(D_MODEL, D_EXPERT), w_a(D_MODEL, D_EXPERT), w_a(D_EXPERT, D_MODEL),
                w_b(D_MODEL, D_EXPERT), w_b(D_MODEL, D_EXPERT), w_b(D_EXPERT, D_MODEL),
                pl.BlockSpec((None, None, MOD_ROWS, D_MODEL), lambda n, *_: (l, N_MOD - 1, 0, 0)),
                pl.BlockSpec((1, D_MODEL), lambda n, *_: (0, 0)),
            ],
            out_specs=pl.BlockSpec(memory_space=pl.ANY),
            scratch_shapes=[up, up, down, up, up, down,
                            pltpu.VMEM((2, MOE_TM, HX_WIDTH), F32), pltpu.VMEM((2, MOE_TM, D_MODEL), F32),
                            pltpu.VMEM((MOE_TM, D_MODEL), F32),
                            pltpu.SemaphoreType.DMA((2,)), pltpu.SemaphoreType.DMA((2,)), pltpu.SemaphoreType.DMA(())],
        ),
        out_shape=jax.ShapeDtypeStruct((MOE_ROWS, D_MODEL), F32),
        compiler_params=_params("arbitrary"),
        name="experts",
    )(*tables, hx, w_gate, w_up, w_down, w_gate, w_up, w_down, mods, final_w.reshape(1, D_MODEL))


def _count_before(flags):
    n, k = flags.shape
    blocks = flags.reshape(n // 128, 128, k).astype(F32)
    strictly_lower = jnp.tril(jnp.ones((128, 128), F32), -1)
    within = jnp.einsum("ij,bjk->bik", strictly_lower, blocks)
    totals = jnp.sum(blocks, axis=1)
    before = jnp.cumsum(totals, axis=0) - totals
    return (within + before[:, None, :]).reshape(n, k).astype(jnp.int32)


def _routing_tables(cls, to_sequence_order):
    n_cls = N_EXPERT_GROUPS * len(PAIR_SLOT_A)
    onehot = (cls[:, None] == jnp.arange(n_cls, dtype=jnp.int32)[None, :]).astype(jnp.int32)
    counts = jnp.sum(onehot, axis=0)
    tiles = (counts + MOE_TM - 1) // MOE_TM
    tile_end = jnp.cumsum(tiles)
    row_in_class = (tile_end - tiles)[None, :] * MOE_TM + _count_before(onehot)
    dst = jnp.sum(onehot * row_in_class, axis=1)
    token_plus_1 = jnp.zeros((MOE_ROWS,), jnp.int32).at[dst].set(jnp.arange(1, T_ALL + 1, dtype=jnp.int32))
    is_pad = token_plus_1 == 0
    src = jnp.maximum(token_plus_1 - 1, 0)
    spare = T_ALL + _count_before(is_pad.astype(jnp.int32)[:, None])[:, 0]
    if to_sequence_order:
        j, c, ps = (src // J_ROWS) % SCAN_CHUNK, (src // N_PS) % SEG_CHUNKS, src % N_PS
        target = (src // TILE_TOKENS) * TILE_TOKENS + ps * SEG_LEN + c * SCAN_CHUNK + j
    else:
        target = src
    drow = jnp.where(is_pad, spare, target).astype(jnp.int32)
    n_valid = tile_end[-1]
    tile_id = jnp.minimum(jnp.arange(MOE_TILES, dtype=jnp.int32), n_valid - 1)
    tile_cls = jnp.sum((tile_end[None, :] <= tile_id[:, None]).astype(jnp.int32), axis=1)
    group, pair = tile_cls // len(PAIR_SLOT_A), tile_cls % len(PAIR_SLOT_A)
    in_cls = (tile_cls[:, None] == jnp.arange(n_cls, dtype=jnp.int32)[None, :]).astype(jnp.int32)
    left = jnp.sum(in_cls * (counts[None, :] - (tile_id[:, None] - (tile_end - tiles)[None, :]) * MOE_TM), axis=1)
    quarter = MOE_TM // TAIL_STEPS
    size = jnp.clip((left + quarter - 1) // quarter, 1, TAIL_STEPS).astype(jnp.int32)

    def slot_expert(table):
        local = jnp.full_like(pair, table[-1])
        for p in range(len(table) - 1):
            local = jnp.where(pair == p, table[p], local)
        return group * EXPERTS_PER_GROUP + local

    e_a = slot_expert(PAIR_SLOT_A)
    e_b = slot_expert(PAIR_SLOT_B)
    first = jnp.ones((1,), jnp.int32)
    new_a = jnp.concatenate([first, (e_a[1:] != e_a[:-1]).astype(jnp.int32)])
    new_b = jnp.concatenate([first, (e_b[1:] != e_b[:-1]).astype(jnp.int32)])
    return e_a, e_b, new_a, new_b, n_valid.reshape(1).astype(jnp.int32), size, src, drow


def _dirs_on_lanes(p):
    p = jnp.moveaxis(p, 1, -2)
    return p.reshape(p.shape[:-2] + (2 * SSM_STATE,))


def _to_internal_order(x):
    x = x.reshape(N_TILES, N_PS, SEG_CHUNKS, SCAN_CHUNK, D_MODEL)
    return x.transpose(0, 3, 2, 1, 4).reshape(T_ALL, D_MODEL)


def _gmlp_position_order(w):
    n_lo = GMLP_CHUNK // SCAN_CHUNK
    lead = w.shape[:2]
    w = w.reshape(lead + (n_lo, SCAN_CHUNK) + w.shape[3:])
    w = jnp.swapaxes(w, 2, 3)
    return w.reshape(lead + (GMLP_CHUNK,) + w.shape[4:])


def kernel(x_prompt, x_sample, c, state_ssm_re, state_ssm_im, c_ctx, norm1_w, norm2_w, w_mod, b_mod, w_in,
           ssm_a_re, ssm_a_im, ssm_log_dt, ssm_b_re, ssm_b_im, ssm_c_re, ssm_c_im, ssm_d, w_glu,
           gmlp_ln_w, gmlp_ln_b, gmlp_w_s, gmlp_b_s, w_out, router_w, router_b, w_gate, w_up, w_down,
           final_norm_w):
    x = jnp.concatenate([x_prompt.reshape(T_PROMPT, D_MODEL), x_sample.reshape(T_SAMPLE, D_MODEL)], axis=0)
    x = _to_internal_order(x)

    cvec = jnp.concatenate([c_ctx[None, :], c, jnp.zeros((MOD_ROWS - 1 - N_SAMPLE_SEQ, D_MODEL), F32)], axis=0)
    mods = _modulation(cvec, w_mod, b_mod)
    ps_row = [[0] * N_PS] * SAMPLE_TILE + [[1 + p // SEGS_PER_SAMPLE_SEQ for p in range(N_PS)]]
    mod8 = mods[:, :, jnp.array(ps_row, jnp.int32), :].transpose(0, 2, 1, 3, 4)

    log_dt = jnp.broadcast_to(ssm_log_dt[..., None], ssm_a_re.shape)
    d_lanes = jnp.tile(ssm_d.reshape(DEPTH, N_GROUPS, SSM_GROUP), (1, 1, 128 // SSM_GROUP))
    vecs = jnp.stack([_dirs_on_lanes(ssm_a_re), _dirs_on_lanes(ssm_a_im), _dirs_on_lanes(log_dt), d_lanes], axis=2)
    vecs = jnp.concatenate([vecs, jnp.zeros((DEPTH, N_GROUPS, 4, 2 * SSM_STATE), F32)], axis=2)
    mats = jnp.stack([_dirs_on_lanes(jnp.swapaxes(ssm_b_re, -1, -2)), _dirs_on_lanes(jnp.swapaxes(ssm_b_im, -1, -2)),
                      _dirs_on_lanes(ssm_c_re), _dirs_on_lanes(ssm_c_im)], axis=2)
    wft, cct, ttt, a16 = _ssm_operators(vecs, mats)

    w_s = jnp.swapaxes(_gmlp_position_order(jnp.swapaxes(_gmlp_position_order(gmlp_w_s), 2, 3)), 2, 3).astype(BF16)
    b_s = _gmlp_position_order(gmlp_b_s)[..., None]
    rwt = router_w.T
    rb = router_b.reshape(N_EXPERTS, 1)

    new_re, new_im = [], []
    for l in range(DEPTH):
        ug, vn, xt = _input_proj(x, mod8, l, norm1_w, w_in, gmlp_ln_w, gmlp_ln_b)
        s0 = jnp.concatenate([state_ssm_re[:, l].transpose(2, 0, 1, 3).reshape(N_GROUPS, N_SAMPLE_SEQ, 128),
                              state_ssm_im[:, l].transpose(2, 0, 1, 3).reshape(N_GROUPS, N_SAMPLE_SEQ, 128)], axis=-1)
        s0 = jnp.repeat(s0, SEGS_PER_SAMPLE_SEQ, axis=1)
        yt, fin = _ssm_scan(xt, wft, cct, ttt, a16, s0, l)
        fin = fin[:SAMPLE_TILE].reshape(SAMPLE_TILE, N_GROUPS, N_PS, 2, 2, SSM_STATE)
        fin = fin.transpose(3, 0, 2, 4, 1, 5).reshape(2, N_PROMPT_SEQ, 2, N_GROUPS, SSM_STATE)
        new_re.append(fin[0])
        new_im.append(fin[1])
        hx, route = _mix_out(x, yt, ug, vn, mod8, l, w_glu, w_s, b_s, w_out, norm2_w, rwt, rb)
        final = l == DEPTH - 1
        tables = _routing_tables(route[0].astype(jnp.int32), to_sequence_order=final)
        x = _experts(tables, hx, mods, l, w_gate, w_up, w_down, final_norm_w, final)

    y_prompt = x[:T_PROMPT].reshape(N_PROMPT_SEQ, PROMPT_LEN, D_MODEL)
    y_sample = x[T_PROMPT:T_ALL].reshape(N_SAMPLE_SEQ, SAMPLE_LEN, D_MODEL)
    return (y_prompt, y_sample, jnp.stack(new_re, axis=1), jnp.stack(new_im, axis=1))
```

```python
import functools
import math

import jax
import jax.numpy as jnp
from jax import lax
from jax.experimental import pallas as pl
from jax.experimental.pallas import tpu as pltpu

F32 = jnp.float32
BF16 = jnp.bfloat16

D_MODEL = 1024
N_PROMPT_SEQ = 16
PROMPT_LEN = 256
N_SAMPLE_SEQ = 2
SAMPLE_LEN = 1024
T_PROMPT = N_PROMPT_SEQ * PROMPT_LEN
T_SAMPLE = N_SAMPLE_SEQ * SAMPLE_LEN
T_ALL = T_PROMPT + T_SAMPLE
DEPTH = 2
D_SSM = 512
SSM_GROUP = 16
N_GROUPS = 32
SSM_STATE = 64
D_GMLP = 512
GMLP_HEADS = 4
GMLP_HEAD_DIM = 128
GMLP_CHUNK = 128
N_EXPERTS = 16
N_EXPERT_GROUPS = 4
EXPERTS_PER_GROUP = 4
D_EXPERT = 512
N_MOD = 6
EPS = 1e-6

SCAN_CHUNK = 16
CHUNK_WIDTH = SCAN_CHUNK * SSM_GROUP
SEG_LEN = 256
SEG_CHUNKS = SEG_LEN // SCAN_CHUNK
N_PS = 8
TILE_TOKENS = N_PS * SEG_LEN
N_TILES = T_ALL // TILE_TOKENS
SAMPLE_TILE = T_PROMPT // TILE_TOKENS
SEGS_PER_SAMPLE_SEQ = SAMPLE_LEN // SEG_LEN
J_ROWS = SEG_CHUNKS * N_PS
GROUP_BLOCK = 8
PREP_GROUPS = 4
MOD_ROWS = 8
MOD_K_STEPS = 4

J_PER_STEP = 4
TOKEN_TILE = J_PER_STEP * J_ROWS
PAIR_SLOT_A = (0, 0, 0, 1, 1, 3)
PAIR_SLOT_B = (1, 2, 3, 3, 2, 2)
MOE_TM = 128
TAIL_STEPS = 4
TAIL_QUARTER = MOE_TM // TAIL_STEPS
N_CLASSES = N_EXPERT_GROUPS * len(PAIR_SLOT_A)
MOE_TILES = T_ALL // MOE_TM + N_CLASSES
MOE_ROWS = MOE_TILES * MOE_TM
SPARE_ROWS = N_CLASSES * TAIL_QUARTER
OUT_ROWS = T_ALL + SPARE_ROWS
ROUTE_LANES = 128
HX_WIDTH = 2 * D_MODEL + ROUTE_LANES
VMEM_LIMIT = 56 * 1024 * 1024
TRANS_B = (((1,), (1,)), ((), ()))


def _sigmoid(x):
    return 1.0 / (1.0 + jnp.exp(-x))


def _gelu_tanh(x):
    c = math.sqrt(2.0 / math.pi)
    return x * (0.5 * (1.0 + jnp.tanh(c * (x + 0.044715 * (x * x * x)))))


def _split_bf16(a):
    hi = a.astype(BF16)
    return hi, (a - hi.astype(F32)).astype(BF16)


def _rmsnorm(x, w):
    return x * lax.rsqrt(jnp.mean(x * x, axis=-1, keepdims=True) + EPS) * w


def _per_ps(fn, a, *mods):
    rows, d = a.shape
    out = fn(a.reshape(rows // N_PS, N_PS, d), *[m[None] for m in mods])
    return out.reshape(rows, d)


def _params(*sem):
    return pltpu.CompilerParams(dimension_semantics=sem, vmem_limit_bytes=VMEM_LIMIT)


def _mod_kernel(c_ref, w_ref, b_ref, o_ref):
    c = c_ref[...]
    part = jnp.dot((c * _sigmoid(c)).astype(BF16), w_ref[...].astype(BF16), preferred_element_type=F32)

    @pl.when(pl.program_id(1) == 0)
    def _():
        o_ref[...] = part + b_ref[...]

    @pl.when(pl.program_id(1) > 0)
    def _():
        o_ref[...] += part


def _modulation(cvec, w_mod, b_mod):
    kb = D_MODEL // MOD_K_STEPS
    c_blocks = cvec.reshape(MOD_ROWS, MOD_K_STEPS, kb).transpose(1, 0, 2)
    return pl.pallas_call(
        _mod_kernel,
        grid=(DEPTH, MOD_K_STEPS),
        in_specs=[
            pl.BlockSpec((None, MOD_ROWS, kb), lambda l, k: (k, 0, 0)),
            pl.BlockSpec((None, kb, N_MOD * D_MODEL), lambda l, k: (l, k, 0)),
            pl.BlockSpec((None, 1, N_MOD * D_MODEL), lambda l, k: (l, 0, 0)),
        ],
        out_specs=pl.BlockSpec((None, MOD_ROWS, N_MOD * D_MODEL), lambda l, k: (l, 0, 0)),
        out_shape=jax.ShapeDtypeStruct((DEPTH, MOD_ROWS, N_MOD * D_MODEL), F32),
        compiler_params=_params("arbitrary", "arbitrary"),
        name="adaln_mod",
    )(c_blocks, w_mod, b_mod.reshape(DEPTH, 1, N_MOD * D_MODEL))


def _first_step():
    return jnp.logical_and(pl.program_id(0) == 0, pl.program_id(1) == 0)


def _in_kernel(x_ref, mod_ref, nw_ref, w_ref, lnw_ref, lnb_ref, ug_ref, vn_ref, xt_ref, wg_ref, wst_ref):
    @pl.when(_first_step())
    def _():
        wg_ref[...] = w_ref[:, D_SSM:].astype(BF16)
        wst_ref[...] = w_ref[:, :D_SSM].T.astype(BF16)

    y = _rmsnorm(x_ref[...], nw_ref[...])
    h = _per_ps(lambda a, sc, sh: a * (1.0 + sc) + sh, y, mod_ref[1], mod_ref[0]).astype(BF16)
    zg = _gelu_tanh(jnp.dot(h, wg_ref[...], preferred_element_type=F32))
    v = zg[:, D_GMLP:]
    mu = jnp.mean(v, axis=-1, keepdims=True)
    vc = v - mu
    var = jnp.mean(vc * vc, axis=-1, keepdims=True)
    vn = vc * lax.rsqrt(var + EPS) * lnw_ref[...] + lnb_ref[...]
    for hd in range(GMLP_HEADS):
        cols = slice(hd * GMLP_HEAD_DIM, (hd + 1) * GMLP_HEAD_DIM)
        ug_ref[hd] = zg[:, cols]
        vn_ref[hd] = vn[:, cols]
    xt = lax.dot_general(wst_ref[...], h, TRANS_B, preferred_element_type=F32).astype(BF16)
    for k in range(J_PER_STEP):
        xt_ref[k] = xt[:, k * J_ROWS:(k + 1) * J_ROWS]


def _input_proj(x, mod8, l, norm1_w, w_in, ln_w, ln_b):
    tm = TOKEN_TILE
    steps = TILE_TOKENS // tm
    tok = lambda n: pl.BlockSpec((tm, n), lambda t, s: (t * steps + s, 0))
    heads = pl.BlockSpec((GMLP_HEADS, tm, GMLP_HEAD_DIM), lambda t, s: (0, t * steps + s, 0))
    lay = lambda *shape: pl.BlockSpec((None,) + shape, lambda t, s: (l,) + (0,) * len(shape))
    return pl.pallas_call(
        _in_kernel,
        grid=(N_TILES, steps),
        in_specs=[
            tok(D_MODEL),
            pl.BlockSpec((None, None, N_MOD, N_PS, D_MODEL), lambda t, s: (l, t, 0, 0, 0)),
            lay(1, D_MODEL), lay(D_MODEL, D_SSM + 2 * D_GMLP), lay(1, D_GMLP), lay(1, D_GMLP),
        ],
        out_specs=[heads, heads,
                   pl.BlockSpec((None, J_PER_STEP, D_SSM, J_ROWS), lambda t, s: (t, s, 0, 0))],
        out_shape=[jax.ShapeDtypeStruct((GMLP_HEADS, T_ALL, GMLP_HEAD_DIM), F32)] * 2 + [
            jax.ShapeDtypeStruct((N_TILES, SCAN_CHUNK, D_SSM, J_ROWS), BF16)],
        scratch_shapes=[pltpu.VMEM((D_MODEL, 2 * D_GMLP), BF16), pltpu.VMEM((D_SSM, D_MODEL), BF16)],
        compiler_params=_params("arbitrary", "arbitrary"),
        name="norm1_in_proj",
    )(x, mod8, norm1_w.reshape(DEPTH, 1, D_MODEL), w_in,
      ln_w.reshape(DEPTH, 1, D_GMLP), ln_b.reshape(DEPTH, 1, D_GMLP))


def _shift_lanes_right(a, b, s, lane):
    if s == 0:
        return a, b
    if s == 128:
        return jnp.zeros_like(a), a
    if s < 128:
        ra = pltpu.roll(a, s, 1)
        rb = pltpu.roll(b, s, 1)
        return jnp.where(lane >= s, ra, 0.0), jnp.where(lane >= s, rb, ra)
    t = s - 128
    return jnp.zeros_like(a), jnp.where(lane >= t, pltpu.roll(a, t, 1), 0.0)


def _shift_lanes_left(a, b, s, lane):
    if s == 0:
        return a, b
    if s == 128:
        return b, jnp.zeros_like(b)
    if s < 128:
        ra = pltpu.roll(a, 128 - s, 1)
        rb = pltpu.roll(b, 128 - s, 1)
        return jnp.where(lane < 128 - s, ra, rb), jnp.where(lane < 128 - s, rb, 0.0)
    t = s - 128
    return jnp.where(lane < 128 - t, pltpu.roll(b, 128 - t, 1), 0.0), jnp.zeros_like(b)


def _prep_kernel(*refs):
    for g in range(PREP_GROUPS):
        _prep_group(*[r.at[g] for r in refs])


def _prep_group(vec_ref, mat_ref, wft_ref, cct_ref, ttt_ref, a16_ref, wf_scr, cm_scr, tt_scr):
    a_re = vec_ref[0:1, :]
    a_im = vec_ref[1:2, :]
    dt = jnp.exp(vec_ref[2:3, :])
    d_skip = vec_ref[3:4, :]
    mag = jnp.exp(a_re * dt)
    ang = a_im * dt
    ab_r = mag * jnp.cos(ang)
    ab_i = mag * jnp.sin(ang)
    den = a_re * a_re + a_im * a_im
    nr = ab_r - 1.0
    q_r = (nr * a_re + ab_i * a_im) / den
    q_i = (ab_i * a_re - nr * a_im) / den
    bt_r = mat_ref[0]
    bt_i = mat_ref[1]
    c_r = mat_ref[2]
    c_i = mat_ref[3]
    bb_r = q_r * bt_r - q_i * bt_i
    bb_i = q_r * bt_i + q_i * bt_r
    p_r = [jnp.ones_like(ab_r)]
    p_i = [jnp.zeros_like(ab_r)]
    for _ in range(SCAN_CHUNK):
        pr, pi = p_r[-1], p_i[-1]
        p_r.append(pr * ab_r - pi * ab_i)
        p_i.append(pr * ab_i + pi * ab_r)
    a16_ref[0] = jnp.broadcast_to(p_r[SCAN_CHUNK], (N_PS, 128))
    a16_ref[1] = jnp.broadcast_to(p_i[SCAN_CHUNK], (N_PS, 128))

    lane = lax.broadcasted_iota(jnp.int32, (1, 128), 1)
    is_fwd = lane < SSM_STATE

    def pick(mf, mb):
        return jnp.where(is_fwd, p_r[mf], p_r[mb]), jnp.where(is_fwd, p_i[mf], p_i[mb])

    for j in range(SCAN_CHUNK):
        rows = slice(j * SSM_GROUP, (j + 1) * SSM_GROUP)
        wr, wi = pick(SCAN_CHUNK - 1 - j, j)
        wf_scr[rows, 0:128] = bb_r * wr - bb_i * wi
        wf_scr[rows, 128:256] = bb_r * wi + bb_i * wr
        wr, wi = pick(j + 1, SCAN_CHUNK - j)
        cct_ref[rows, 0:128] = (c_r * wr - c_i * wi).astype(BF16)
        cct_ref[rows, 128:256] = (-(c_r * wi + c_i * wr)).astype(BF16)
        wr, wi = pick(j, SCAN_CHUNK - 1 - j)
        cm_scr[rows, 0:128] = c_r * wr - c_i * wi
        cm_scr[rows, 128:256] = c_r * wi + c_i * wr
    wft_ref[...] = wf_scr[...].T.astype(BF16)

    zero = jnp.zeros_like(bb_r)
    cm_hi, cm_lo = _split_bf16(cm_scr[...])
    dot_t = lambda a, b: lax.dot_general(a, b, TRANS_B, preferred_element_type=F32)

    def lag_rows(keep):
        lhs = jnp.concatenate([jnp.where(keep, bb_r, zero), jnp.where(keep, -bb_i, zero)], axis=1)
        hi, lo = _split_bf16(lhs)
        return dot_t(hi, cm_hi) + (dot_t(hi, cm_lo) + dot_t(lo, cm_hi))

    mf = lag_rows(is_fwd)
    mb = lag_rows(jnp.logical_not(is_fwd))
    mf_a, mf_b = mf[:, 0:128], mf[:, 128:256]
    mb_a, mb_b = mb[:, 0:128], mb[:, 128:256]
    row_h = lax.broadcasted_iota(jnp.int32, (SSM_GROUP, 128), 0)
    lane_h = lax.broadcasted_iota(jnp.int32, (SSM_GROUP, 128), 1)
    for jp in range(SCAN_CHUNK):
        rows = slice(jp * SSM_GROUP, (jp + 1) * SSM_GROUP)
        fa, fb = _shift_lanes_right(mf_a, mf_b, SSM_GROUP * jp, lane)
        ba, bb = _shift_lanes_left(mb_a, mb_b, SSM_GROUP * (SCAN_CHUNK - 1 - jp), lane)
        diag = SSM_GROUP * jp + row_h
        tt_scr[rows, 0:128] = fa + ba + jnp.where(lane_h == diag, d_skip, 0.0)
        tt_scr[rows, 128:256] = fb + bb + jnp.where(lane_h + 128 == diag, d_skip, 0.0)
    ttt_ref[...] = tt_scr[...].T.astype(BF16)


def _ssm_operators(vecs, mats):
    op = jax.ShapeDtypeStruct((DEPTH, N_GROUPS, CHUNK_WIDTH, CHUNK_WIDTH), BF16)
    pg = PREP_GROUPS
    op_spec = pl.BlockSpec((None, pg, CHUNK_WIDTH, CHUNK_WIDTH), lambda l, g: (l, g, 0, 0))
    sq = pltpu.VMEM((pg, CHUNK_WIDTH, CHUNK_WIDTH), F32)
    return pl.pallas_call(
        _prep_kernel,
        grid=(DEPTH, N_GROUPS // pg),
        in_specs=[
            pl.BlockSpec((None, pg, 8, 128), lambda l, g: (l, g, 0, 0)),
            pl.BlockSpec((None, pg, 4, SSM_GROUP, 128), lambda l, g: (l, g, 0, 0, 0)),
        ],
        out_specs=[op_spec, op_spec, op_spec,
                   pl.BlockSpec((None, pg, 2, N_PS, 128), lambda l, g: (l, g, 0, 0, 0))],
        out_shape=[op, op, op, jax.ShapeDtypeStruct((DEPTH, N_GROUPS, 2, N_PS, 128), F32)],
        scratch_shapes=[sq, sq, sq],
        compiler_params=_params("arbitrary", "arbitrary"),
        name="s5_operators",
    )(vecs, mats)


def _ssm_kernel(xt_ref, wft_ref, cct_ref, ttt_ref, a16_ref, s0_ref, yt_ref, fin_ref, s_scr, f_scr, ft_scr):
    tile = pl.program_id(0)
    lane = lax.broadcasted_iota(jnp.int32, (GROUP_BLOCK, N_PS, 128), 2)
    is_fwd = lane < SSM_STATE
    half = SSM_STATE

    def group_x(gl):
        return xt_ref[:, gl * SSM_GROUP:(gl + 1) * SSM_GROUP, :].reshape(CHUNK_WIDTH, J_ROWS)

    for gl in range(GROUP_BLOCK):
        ft_scr[gl] = jnp.dot(wft_ref[gl], group_x(gl), preferred_element_type=F32)
        f_scr[gl] = ft_scr[gl].T

    a_r = a16_ref[:, 0]
    a_i = a16_ref[:, 1]

    def scan(s_r, s_i):
        for i in range(SEG_CHUNKS):
            rf = slice(i * N_PS, (i + 1) * N_PS)
            rb = slice((SEG_CHUNKS - 1 - i) * N_PS, (SEG_CHUNKS - i) * N_PS)
            s_scr[:, rf, 0:half] = s_r[:, :, 0:half]
            s_scr[:, rf, 128:128 + half] = s_i[:, :, 0:half]
            s_scr[:, rb, half:128] = s_r[:, :, half:128]
            s_scr[:, rb, 128 + half:256] = s_i[:, :, half:128]
            f_r = jnp.where(is_fwd, f_scr[:, rf, 0:128], f_scr[:, rb, 0:128])
            f_i = jnp.where(is_fwd, f_scr[:, rf, 128:256], f_scr[:, rb, 128:256])
            s_r, s_i = a_r * s_r - a_i * s_i + f_r, a_r * s_i + a_i * s_r + f_i
        return s_r, s_i

    zeros = jnp.zeros((GROUP_BLOCK, N_PS, 128), F32)
    z_r, z_i = scan(zeros, zeros)
    fin_ref[:, :, 0:128] = z_r
    fin_ref[:, :, 128:256] = z_i

    @pl.when(tile == SAMPLE_TILE)
    def _():
        b_r, b_i = a_r, a_i
        for _ in range(4):
            b_r, b_i = b_r * b_r - b_i * b_i, 2.0 * (b_r * b_i)
        seg = lax.broadcasted_iota(jnp.int32, (GROUP_BLOCK, N_PS, 128), 1) % SEGS_PER_SAMPLE_SEQ
        i_r = s0_ref[:, :, 0:128]
        i_i = s0_ref[:, :, 128:256]
        for step in range(1, SEGS_PER_SAMPLE_SEQ):
            pr = jnp.where(is_fwd, pltpu.roll(i_r, 1, 1), pltpu.roll(i_r, N_PS - 1, 1))
            pi = jnp.where(is_fwd, pltpu.roll(i_i, 1, 1), pltpu.roll(i_i, N_PS - 1, 1))
            zr = jnp.where(is_fwd, pltpu.roll(z_r, 1, 1), pltpu.roll(z_r, N_PS - 1, 1))
            zi = jnp.where(is_fwd, pltpu.roll(z_i, 1, 1), pltpu.roll(z_i, N_PS - 1, 1))
            n_r = b_r * pr - b_i * pi + zr
            n_i = b_r * pi + b_i * pr + zi
            first = jnp.where(is_fwd, step, 0)
            last = jnp.where(is_fwd, SEGS_PER_SAMPLE_SEQ - 1, SEGS_PER_SAMPLE_SEQ - 1 - step)
            upd = jnp.logical_and(seg >= first, seg <= last)
            i_r = jnp.where(upd, n_r, i_r)
            i_i = jnp.where(upd, n_i, i_i)
        scan(i_r, i_i)

    for gl in range(GROUP_BLOCK):
        yt = jnp.dot(ttt_ref[gl], group_x(gl), preferred_element_type=F32)
        yt += lax.dot_general(cct_ref[gl], s_scr[gl].astype(BF16), TRANS_B, preferred_element_type=F32)
        yt_ref[:, gl * SSM_GROUP:(gl + 1) * SSM_GROUP, :] = yt.reshape(SCAN_CHUNK, SSM_GROUP, J_ROWS)


def _ssm_scan(xt, wft, cct, ttt, a16, s0, l):
    gb = GROUP_BLOCK
    op_spec = pl.BlockSpec((None, gb, CHUNK_WIDTH, CHUNK_WIDTH), lambda t, g: (l, g, 0, 0))
    io_spec = pl.BlockSpec((None, SCAN_CHUNK, gb * SSM_GROUP, J_ROWS), lambda t, g: (t, 0, g, 0))
    return pl.pallas_call(
        _ssm_kernel,
        grid=(N_TILES, N_GROUPS // gb),
        in_specs=[
            io_spec, op_spec, op_spec, op_spec,
            pl.BlockSpec((None, gb, 2, N_PS, 128), lambda t, g: (l, g, 0, 0, 0)),
            pl.BlockSpec((gb, N_PS, CHUNK_WIDTH), lambda t, g: (g, 0, 0)),
        ],
        out_specs=[io_spec, pl.BlockSpec((None, gb, N_PS, CHUNK_WIDTH), lambda t, g: (t, g, 0, 0))],
        out_shape=[
            jax.ShapeDtypeStruct((N_TILES, SCAN_CHUNK, D_SSM, J_ROWS), F32),
            jax.ShapeDtypeStruct((N_TILES, N_GROUPS, N_PS, CHUNK_WIDTH), F32),
        ],
        scratch_shapes=[pltpu.VMEM((gb, J_ROWS, CHUNK_WIDTH), F32), pltpu.VMEM((gb, J_ROWS, CHUNK_WIDTH), F32),
                        pltpu.VMEM((gb, CHUNK_WIDTH, J_ROWS), F32)],
        compiler_params=_params("arbitrary", "arbitrary"),
        name="s5_chunk_scan",
    )(xt, wft, cct, ttt, a16, s0)


def _post_kernel(x_ref, yt_ref, ug_ref, vn_ref, mod_ref, wglu_ref, ws_ref, bs_ref, wout_f32_ref,
                 nw_ref, rwt_ref, rb_ref, hx_ref, route_ref, yg_scr, y_scr, wglut_ref, wout_ref):
    tile = pl.program_id(0)
    step = pl.program_id(1)

    @pl.when(_first_step())
    def _():
        wglut_ref[...] = wglu_ref[...].T.astype(BF16)
        wout_ref[...] = wout_f32_ref[...].astype(BF16)

    @pl.when(step == 0)
    def _():
        def chunk(n, carry):
            ps = n % N_PS
            c_hi = n // N_PS
            base = c_hi * (GMLP_CHUNK // SCAN_CHUNK) * N_PS + ps
            rows = [pl.ds(j * J_ROWS + base, GMLP_CHUNK // SCAN_CHUNK, stride=N_PS) for j in range(SCAN_CHUNK)]
            for h in range(GMLP_HEADS):
                v = jnp.concatenate([vn_ref[h, r, :] for r in rows], axis=0).astype(BF16)
                u = jnp.concatenate([ug_ref[h, r, :] for r in rows], axis=0)
                s = jnp.dot(ws_ref[h], v, preferred_element_type=F32) + bs_ref[h]
                yg = u * s
                for j, r in enumerate(rows):
                    yg_scr[h, r, :] = yg[j * 8:(j + 1) * 8]
            return carry

        lax.fori_loop(0, TILE_TOKENS // GMLP_CHUNK, chunk, 0)

    yt = _gelu_tanh(jnp.concatenate([yt_ref[k] for k in range(J_PER_STEP)], axis=1))
    yt = yt * _sigmoid(jnp.dot(wglut_ref[...], yt.astype(BF16), preferred_element_type=F32))
    for k in range(J_PER_STEP):
        y_scr[k * J_ROWS:(k + 1) * J_ROWS, :] = yt[:, k * J_ROWS:(k + 1) * J_ROWS].T
    row0 = pl.multiple_of(step * TOKEN_TILE, TOKEN_TILE)
    proj = jnp.dot(y_scr[...].astype(BF16), wout_ref[0:D_SSM, :], preferred_element_type=F32)
    yg = jnp.concatenate([yg_scr[h, pl.ds(row0, TOKEN_TILE), :] for h in range(GMLP_HEADS)], axis=1)
    proj += jnp.dot(yg.astype(BF16), wout_ref[D_SSM:, :], preferred_element_type=F32)
    x1 = x_ref[...] + _per_ps(lambda a, g: a * g, proj, mod_ref[2])
    h2 = _per_ps(lambda a, sc, sh: a * (1.0 + sc) + sh, _rmsnorm(x1, nw_ref[...]), mod_ref[4], mod_ref[3])
    hx_ref[:, 0:D_MODEL] = h2
    hx_ref[:, D_MODEL:2 * D_MODEL] = x1
    r_hi, r_lo = _split_bf16(rwt_ref[...])
    h_hi, h_lo = _split_bf16(h2)
    dot_t = lambda a, b: lax.dot_general(a, b, TRANS_B, preferred_element_type=F32)
    logits = dot_t(r_hi, h_hi) + (dot_t(r_hi, h_lo) + dot_t(r_lo, h_hi))
    scores = _sigmoid(logits)
    sel = scores + rb_ref[...]
    sc = [scores[e:e + 1, :] for e in range(N_EXPERTS)]
    sl = [sel[e:e + 1, :] for e in range(N_EXPERTS)]
    gscore = []
    for g in range(N_EXPERT_GROUPS):
        v0, v1, v2, v3 = sl[4 * g:4 * g + 4]
        hi01, lo01 = jnp.maximum(v0, v1), jnp.minimum(v0, v1)
        hi23, lo23 = jnp.maximum(v2, v3), jnp.minimum(v2, v3)
        top1 = jnp.maximum(hi01, hi23)
        top2 = jnp.maximum(jnp.minimum(hi01, hi23), jnp.maximum(lo01, lo23))
        gscore.append(top1 + top2)
    best = gscore[0]
    gidx = jnp.zeros_like(best, dtype=jnp.int32)
    for g in range(1, N_EXPERT_GROUPS):
        upd = gscore[g] > best
        gidx = jnp.where(upd, g, gidx)
        best = jnp.where(upd, gscore[g], best)

    def in_group(vals, k):
        out = vals[k]
        for g in range(1, N_EXPERT_GROUPS):
            out = jnp.where(gidx == g, vals[4 * g + k], out)
        return out

    v = [in_group(sl, k) for k in range(EXPERTS_PER_GROUP)]
    s = [in_group(sc, k) for k in range(EXPERTS_PER_GROUP)]
    w = []
    bits = jnp.zeros_like(gidx)
    for k in range(EXPERTS_PER_GROUP):
        rank = jnp.zeros_like(gidx)
        for j in range(EXPERTS_PER_GROUP):
            if j == k:
                continue
            ahead = (v[j] >= v[k]) if j < k else (v[j] > v[k])
            rank = rank + ahead.astype(jnp.int32)
        w.append(jnp.where(rank < 2, s[k], 0.0))
        bits = bits + jnp.where(rank < 2, 1 << k, 0)
    denom = (w[0] + w[1]) + (w[2] + w[3])
    gate = [wk / denom for wk in w]
    pair = jnp.full_like(gidx, len(PAIR_SLOT_A) - 1)
    for p in range(len(PAIR_SLOT_A) - 1):
        pair = jnp.where(bits == (1 << PAIR_SLOT_A[p]) + (1 << PAIR_SLOT_B[p]), p, pair)

    def slot_gate(table):
        out = gate[table[0]]
        for p in range(1, len(table)):
            out = jnp.where(pair == p, gate[table[p]], out)
        return out

    n_tok = route_ref.shape[1]
    route_ref[0:1, :] = (gidx * len(PAIR_SLOT_A) + pair).astype(F32)
    ps = lax.broadcasted_iota(jnp.int32, (1, n_tok), 1) % N_PS
    cond = jnp.where(tile < SAMPLE_TILE, 0, 1 + ps // SEGS_PER_SAMPLE_SEQ).astype(F32)
    lanes = jnp.concatenate([slot_gate(PAIR_SLOT_A), slot_gate(PAIR_SLOT_B), cond,
                             jnp.zeros((ROUTE_LANES - 3, n_tok), F32)], axis=0)
    hx_ref[:, 2 * D_MODEL:] = lanes.T


def _mix_out(x, yt, ug, vn, mod8, l, w_glu, w_s, b_s, w_out, norm2_w, rwt, rb):
    tm = TOKEN_TILE
    steps = TILE_TOKENS // tm
    tok = lambda n: pl.BlockSpec((tm, n), lambda t, s: (t * steps + s, 0))
    whole_tile = pl.BlockSpec((GMLP_HEADS, TILE_TOKENS, GMLP_HEAD_DIM), lambda t, s: (0, t, 0))
    lay = lambda *shape: pl.BlockSpec((None,) + shape, lambda t, s: (l,) + (0,) * len(shape))
    return pl.pallas_call(
        _post_kernel,
        grid=(N_TILES, steps),
        in_specs=[
            tok(D_MODEL),
            pl.BlockSpec((None, J_PER_STEP, D_SSM, J_ROWS), lambda t, s: (t, s, 0, 0)),
            whole_tile, whole_tile,
            pl.BlockSpec((None, None, N_MOD, N_PS, D_MODEL), lambda t, s: (l, t, 0, 0, 0)),
            lay(D_SSM, D_SSM), lay(GMLP_HEADS, GMLP_CHUNK, GMLP_CHUNK),
            lay(GMLP_HEADS, GMLP_CHUNK, 1), lay(D_MODEL, D_MODEL), lay(1, D_MODEL),
            pl.BlockSpec((N_EXPERTS, D_MODEL), lambda t, s: (0, 0)),
            pl.BlockSpec((N_EXPERTS, 1), lambda t, s: (0, 0)),
        ],
        out_specs=[tok(HX_WIDTH), pl.BlockSpec((1, tm), lambda t, s: (0, t * steps + s))],
        out_shape=[
            jax.ShapeDtypeStruct((T_ALL, HX_WIDTH), F32),
            jax.ShapeDtypeStruct((1, T_ALL), F32),
        ],
        scratch_shapes=[pltpu.VMEM((GMLP_HEADS, TILE_TOKENS, GMLP_HEAD_DIM), F32), pltpu.VMEM((tm, D_SSM), F32),
                        pltpu.VMEM((D_SSM, D_SSM), BF16), pltpu.VMEM((D_MODEL, D_MODEL), BF16)],
        compiler_params=_params("arbitrary", "arbitrary"),
        name="mixers_out_router",
    )(x, yt, ug, vn, mod8, w_glu, w_s, b_s, w_out, norm2_w.reshape(DEPTH, 1, D_MODEL), rwt, rb)


def _moe_kernel(ea_ref, eb_ref, new_a_ref, new_b_ref, nv_ref, size_ref, src_ref, drow_ref,
                hx_hbm, wga_ref, wua_ref, wda_ref, wgb_ref, wub_ref, wdb_ref, g2_ref, fw_ref,
                o_hbm, wga_s, wua_s, wda_s, wgb_s, wub_s, wdb_s, hx_buf, o_buf, z_buf, g_sem, s_sem, z_sem,
                *, final):
    n = pl.program_id(0)
    n_valid = nv_ref[0]
    slot = n % 2

    def by_size(tile, fn):
        for quarters in range(1, TAIL_STEPS + 1):
            @pl.when(size_ref[tile] == quarters)
            def _(rows=quarters * TAIL_QUARTER):
                fn(rows)

    def start_gather(tile, sl, rows):
        base = tile * MOE_TM
        for r in range(rows):
            pltpu.make_async_copy(hx_hbm.at[pl.ds(src_ref[base + r], 1)], hx_buf.at[sl, pl.ds(r, 1)],
                                  g_sem.at[sl]).start()

    def wait_gather(sl, rows):
        pltpu.make_async_copy(hx_hbm.at[pl.ds(0, rows)], hx_buf.at[sl, pl.ds(0, rows)], g_sem.at[sl]).wait()

    def start_scatter(tile, sl, rows):
        base = tile * MOE_TM
        for r in range(rows):
            pltpu.make_async_copy(o_buf.at[sl, pl.ds(r, 1)], o_hbm.at[pl.ds(drow_ref[base + r], 1)],
                                  s_sem.at[sl]).start()

    def wait_scatter(sl, rows):
        pltpu.make_async_copy(o_buf.at[sl, pl.ds(0, rows)], o_hbm.at[pl.ds(0, rows)], s_sem.at[sl]).wait()

    @pl.when(n == 0)
    def _():
        z_buf[...] = jnp.zeros_like(z_buf)
        spare = [pltpu.make_async_copy(z_buf, o_hbm.at[pl.ds(T_ALL + i * MOE_TM, MOE_TM)], z_sem)
                 for i in range(SPARE_ROWS // MOE_TM)]
        for cp in spare:
            cp.start()
        for cp in spare:
            cp.wait()
        by_size(0, lambda rows: start_gather(0, 0, rows))

    @pl.when(new_a_ref[n] == 1)
    def _():
        wga_s[...] = wga_ref[...].astype(BF16)
        wua_s[...] = wua_ref[...].astype(BF16)
        wda_s[...] = wda_ref[...].astype(BF16)

    @pl.when(new_b_ref[n] == 1)
    def _():
        wgb_s[...] = wgb_ref[...].astype(BF16)
        wub_s[...] = wub_ref[...].astype(BF16)
        wdb_s[...] = wdb_ref[...].astype(BF16)

    @pl.when(n + 1 < n_valid)
    def _():
        by_size(n + 1, lambda rows: start_gather(n + 1, 1 - slot, rows))

    def tile_step(rows):
        wait_gather(slot, rows)

        @pl.when(n >= 2)
        def _():
            by_size(n - 2, lambda r: wait_scatter(slot, r))

        h = hx_buf[slot, 0:rows, 0:D_MODEL].astype(BF16)
        lanes = hx_buf[slot, 0:rows, 2 * D_MODEL:]

        def expert(wg, wu, wd, gate):
            hg = jnp.dot(h, wg[...], preferred_element_type=F32)
            hu = jnp.dot(h, wu[...], preferred_element_type=F32)
            act = hg * _sigmoid(hg) * hu * gate
            return jnp.dot(act.astype(BF16), wd[...], preferred_element_type=F32)

        y = expert(wga_s, wua_s, wda_s, lanes[:, 0:1]) + expert(wgb_s, wub_s, wdb_s, lanes[:, 1:2])
        cond_row = lanes[:, 2:3]
        gate2 = jnp.where(cond_row == 0.0, g2_ref[0:1, :], jnp.where(cond_row == 1.0, g2_ref[1:2, :], g2_ref[2:3, :]))
        x2 = hx_buf[slot, 0:rows, D_MODEL:2 * D_MODEL] + gate2 * y
        o_buf[slot, 0:rows] = _rmsnorm(x2, fw_ref[...]) if final else x2
        start_scatter(n, slot, rows)

    @pl.when(n < n_valid)
    def _():
        by_size(n, tile_step)

    @pl.when(n == MOE_TILES - 1)
    def _():
        last = n_valid - 1
        by_size(last, lambda r: wait_scatter(last % 2, r))
        by_size(last - 1, lambda r: wait_scatter(1 - last % 2, r))


def _experts(tables, hx, mods, l, w_gate, w_up, w_down, final_w, final):
    w_a = lambda r, c: pl.BlockSpec((None, None, r, c), lambda n, ea, eb, *_: (l, ea[n], 0, 0))
    w_b = lambda r, c: pl.BlockSpec((None, None, r, c), lambda n, ea, eb, *_: (l, eb[n], 0, 0))
    up = pltpu.VMEM((D_MODEL, D_EXPERT), BF16)
    down = pltpu.VMEM((D_EXPERT, D_MODEL), BF16)
    return pl.pallas_call(
        functools.partial(_moe_kernel, final=final),
        grid_spec=pltpu.PrefetchScalarGridSpec(
            num_scalar_prefetch=len(tables),
            grid=(MOE_TILES,),
            in_specs=[
                pl.BlockSpec(memory_space=pl.ANY),
                w_a(D_MODEL, D_EXPERT), w_a(D_MODEL, D_EXPERT), w_a(D_EXPERT, D_MODEL),
                w_b(D_MODEL, D_EXPERT), w_b(D_MODEL, D_EXPERT), w_b(D_EXPERT, D_MODEL),
                pl.BlockSpec((None, MOD_ROWS, D_MODEL), lambda n, *_: (l, 0, N_MOD - 1)),
                pl.BlockSpec((1, D_MODEL), lambda n, *_: (0, 0)),
            ],
            out_specs=pl.BlockSpec(memory_space=pl.ANY),
            scratch_shapes=[up, up, down, up, up, down,
                            pltpu.VMEM((2, MOE_TM, HX_WIDTH), F32), pltpu.VMEM((2, MOE_TM, D_MODEL), F32),
                            pltpu.VMEM((MOE_TM, D_MODEL), F32),
                            pltpu.SemaphoreType.DMA((2,)), pltpu.SemaphoreType.DMA((2,)), pltpu.SemaphoreType.DMA(())],
        ),
        out_shape=jax.ShapeDtypeStruct((OUT_ROWS, D_MODEL), F32),
        compiler_params=_params("arbitrary"),
        name="experts",
    )(*tables, hx, w_gate, w_up, w_down, w_gate, w_up, w_down, mods, final_w.reshape(1, D_MODEL))


def _count_before(flags):
    n, k = flags.shape
    blocks = flags.reshape(n // 128, 128, k).astype(F32)
    strictly_lower = jnp.tril(jnp.ones((128, 128), F32), -1)
    within = jnp.einsum("ij,bjk->bik", strictly_lower, blocks)
    totals = jnp.sum(blocks, axis=1)
    before = jnp.cumsum(totals, axis=0) - totals
    return (within + before[:, None, :]).reshape(n, k).astype(jnp.int32)


def _routing_tables(cls, to_sequence_order):
    n_cls = N_EXPERT_GROUPS * len(PAIR_SLOT_A)
    onehot = (cls[:, None] == jnp.arange(n_cls, dtype=jnp.int32)[None, :]).astype(jnp.int32)
    counts = jnp.sum(onehot, axis=0)
    tiles = (counts + MOE_TM - 1) // MOE_TM
    tile_end = jnp.cumsum(tiles)
    row_in_class = (tile_end - tiles)[None, :] * MOE_TM + _count_before(onehot)
    dst = jnp.sum(onehot * row_in_class, axis=1)
    token_plus_1 = jnp.zeros((MOE_ROWS,), jnp.int32).at[dst].set(jnp.arange(1, T_ALL + 1, dtype=jnp.int32))
    is_pad = token_plus_1 == 0
    src = jnp.maximum(token_plus_1 - 1, 0)
    n_valid = tile_end[-1]
    tile_id = jnp.minimum(jnp.arange(MOE_TILES, dtype=jnp.int32), n_valid - 1)
    tile_cls = jnp.sum((tile_end[None, :] <= tile_id[:, None]).astype(jnp.int32), axis=1)
    group, pair = tile_cls // len(PAIR_SLOT_A), tile_cls % len(PAIR_SLOT_A)
    in_cls = (tile_cls[:, None] == jnp.arange(n_cls, dtype=jnp.int32)[None, :]).astype(jnp.int32)
    left = jnp.sum(in_cls * (counts[None, :] - (tile_id[:, None] - (tile_end - tiles)[None, :]) * MOE_TM), axis=1)
    size = jnp.clip((left + TAIL_QUARTER - 1) // TAIL_QUARTER, 1, TAIL_STEPS).astype(jnp.int32)
    row = jnp.arange(MOE_ROWS, dtype=jnp.int32)
    processed = jnp.logical_and(row % MOE_TM < jnp.repeat(size, MOE_TM) * TAIL_QUARTER, row // MOE_TM < n_valid)
    moved_pad = jnp.logical_and(is_pad, processed)
    spare = T_ALL + _count_before(moved_pad.astype(jnp.int32)[:, None])[:, 0]
    if to_sequence_order:
        j, c, ps = (src // J_ROWS) % SCAN_CHUNK, (src // N_PS) % SEG_CHUNKS, src % N_PS
        target = (src // TILE_TOKENS) * TILE_TOKENS + ps * SEG_LEN + c * SCAN_CHUNK + j
    else:
        target = src
    drow = jnp.where(is_pad, jnp.where(moved_pad, spare, T_ALL), target).astype(jnp.int32)

    def slot_expert(table):
        local = jnp.full_like(pair, table[-1])
        for p in range(len(table) - 1):
            local = jnp.where(pair == p, table[p], local)
        return group * EXPERTS_PER_GROUP + local

    e_a = slot_expert(PAIR_SLOT_A)
    e_b = slot_expert(PAIR_SLOT_B)
    first = jnp.ones((1,), jnp.int32)
    new_a = jnp.concatenate([first, (e_a[1:] != e_a[:-1]).astype(jnp.int32)])
    new_b = jnp.concatenate([first, (e_b[1:] != e_b[:-1]).astype(jnp.int32)])
    return e_a, e_b, new_a, new_b, n_valid.reshape(1).astype(jnp.int32), size, src, drow


def _dirs_on_lanes(p):
    p = jnp.moveaxis(p, 1, -2)
    return p.reshape(p.shape[:-2] + (2 * SSM_STATE,))


def _to_internal_order(x):
    x = x.reshape(N_TILES, N_PS, SEG_CHUNKS, SCAN_CHUNK, D_MODEL)
    return x.transpose(0, 3, 2, 1, 4).reshape(T_ALL, D_MODEL)


def _gmlp_position_order(w):
    n_lo = GMLP_CHUNK // SCAN_CHUNK
    lead = w.shape[:2]
    w = w.reshape(lead + (n_lo, SCAN_CHUNK) + w.shape[3:])
    w = jnp.swapaxes(w, 2, 3)
    return w.reshape(lead + (GMLP_CHUNK,) + w.shape[4:])


def kernel(x_prompt, x_sample, c, state_ssm_re, state_ssm_im, c_ctx, norm1_w, norm2_w, w_mod, b_mod, w_in,
           ssm_a_re, ssm_a_im, ssm_log_dt, ssm_b_re, ssm_b_im, ssm_c_re, ssm_c_im, ssm_d, w_glu,
           gmlp_ln_w, gmlp_ln_b, gmlp_w_s, gmlp_b_s, w_out, router_w, router_b, w_gate, w_up, w_down,
           final_norm_w):
    x = jnp.concatenate([x_prompt.reshape(T_PROMPT, D_MODEL), x_sample.reshape(T_SAMPLE, D_MODEL)], axis=0)
    x = _to_internal_order(x)

    cvec = jnp.concatenate([c_ctx[None, :], c, jnp.zeros((MOD_ROWS - 1 - N_SAMPLE_SEQ, D_MODEL), F32)], axis=0)
    mods = _modulation(cvec, w_mod, b_mod)
    ps_row = [[0] * N_PS] * SAMPLE_TILE + [[1 + p // SEGS_PER_SAMPLE_SEQ for p in range(N_PS)]]
    mod8 = mods.reshape(DEPTH, MOD_ROWS, N_MOD, D_MODEL)[:, jnp.array(ps_row, jnp.int32)]
    mod8 = mod8.transpose(0, 1, 3, 2, 4)

    log_dt = jnp.broadcast_to(ssm_log_dt[..., None], ssm_a_re.shape)
    d_lanes = jnp.tile(ssm_d.reshape(DEPTH, N_GROUPS, SSM_GROUP), (1, 1, 128 // SSM_GROUP))
    vecs = jnp.stack([_dirs_on_lanes(ssm_a_re), _dirs_on_lanes(ssm_a_im), _dirs_on_lanes(log_dt), d_lanes], axis=2)
    vecs = jnp.concatenate([vecs, jnp.zeros((DEPTH, N_GROUPS, 4, 2 * SSM_STATE), F32)], axis=2)
    mats = jnp.stack([_dirs_on_lanes(jnp.swapaxes(ssm_b_re, -1, -2)), _dirs_on_lanes(jnp.swapaxes(ssm_b_im, -1, -2)),
                      _dirs_on_lanes(ssm_c_re), _dirs_on_lanes(ssm_c_im)], axis=2)
    wft, cct, ttt, a16 = _ssm_operators(vecs, mats)

    w_s = jnp.swapaxes(_gmlp_position_order(jnp.swapaxes(_gmlp_position_order(gmlp_w_s), 2, 3)), 2, 3).astype(BF16)
    b_s = _gmlp_position_order(gmlp_b_s)[..., None]
    rwt = router_w.T
    rb = router_b.reshape(N_EXPERTS, 1)

    new_re, new_im = [], []
    for l in range(DEPTH):
        ug, vn, xt = _input_proj(x, mod8, l, norm1_w, w_in, gmlp_ln_w, gmlp_ln_b)
        s0 = jnp.concatenate([state_ssm_re[:, l].transpose(2, 0, 1, 3).reshape(N_GROUPS, N_SAMPLE_SEQ, 128),
                              state_ssm_im[:, l].transpose(2, 0, 1, 3).reshape(N_GROUPS, N_SAMPLE_SEQ, 128)], axis=-1)
        s0 = jnp.repeat(s0, SEGS_PER_SAMPLE_SEQ, axis=1)
        yt, fin = _ssm_scan(xt, wft, cct, ttt, a16, s0, l)
        fin = fin[:SAMPLE_TILE].reshape(SAMPLE_TILE, N_GROUPS, N_PS, 2, 2, SSM_STATE)
        fin = fin.transpose(3, 0, 2, 4, 1, 5).reshape(2, N_PROMPT_SEQ, 2, N_GROUPS, SSM_STATE)
        new_re.append(fin[0])
        new_im.append(fin[1])
        hx, route = _mix_out(x, yt, ug, vn, mod8, l, w_glu, w_s, b_s, w_out, norm2_w, rwt, rb)
        final = l == DEPTH - 1
        tables = _routing_tables(route[0].astype(jnp.int32), to_sequence_order=final)
        x = _experts(tables, hx, mods, l, w_gate, w_up, w_down, final_norm_w, final)

    y_prompt = x[:T_PROMPT].reshape(N_PROMPT_SEQ, PROMPT_LEN, D_MODEL)
    y_sample = x[T_PROMPT:T_ALL].reshape(N_SAMPLE_SEQ, SAMPLE_LEN, D_MODEL)
    return (y_prompt, y_sample, jnp.stack(new_re, axis=1), jnp.stack(new_im, axis=1))
```

```python
import functools
import math

import jax
import jax.numpy as jnp
from jax import lax
from jax.experimental import pallas as pl
from jax.experimental.pallas import tpu as pltpu

F32 = jnp.float32
BF16 = jnp.bfloat16

D_MODEL = 1024
N_PROMPT_SEQ = 16
PROMPT_LEN = 256
N_SAMPLE_SEQ = 2
SAMPLE_LEN = 1024
T_PROMPT = N_PROMPT_SEQ * PROMPT_LEN
T_SAMPLE = N_SAMPLE_SEQ * SAMPLE_LEN
T_ALL = T_PROMPT + T_SAMPLE
DEPTH = 2
D_SSM = 512
SSM_GROUP = 16
N_GROUPS = 32
SSM_STATE = 64
D_GMLP = 512
GMLP_HEADS = 4
GMLP_HEAD_DIM = 128
GMLP_CHUNK = 128
N_EXPERTS = 16
N_EXPERT_GROUPS = 4
EXPERTS_PER_GROUP = 4
D_EXPERT = 512
N_MOD = 6
EPS = 1e-6

SCAN_CHUNK = 16
CHUNK_WIDTH = SCAN_CHUNK * SSM_GROUP
SEG_LEN = 256
SEG_CHUNKS = SEG_LEN // SCAN_CHUNK
N_PS = 8
TILE_TOKENS = N_PS * SEG_LEN
N_TILES = T_ALL // TILE_TOKENS
SAMPLE_TILE = T_PROMPT // TILE_TOKENS
SEGS_PER_SAMPLE_SEQ = SAMPLE_LEN // SEG_LEN
J_ROWS = SEG_CHUNKS * N_PS
GROUP_BLOCK = 8
PREP_GROUPS = 4
MOD_ROWS = 8
MOD_K_STEPS = 4

J_PER_STEP = 4
TOKEN_TILE = J_PER_STEP * J_ROWS
PAIR_SLOT_A = (0, 0, 0, 1, 1, 3)
PAIR_SLOT_B = (1, 2, 3, 3, 2, 2)
MOE_TM = 256
TAIL_STEPS = 4
TAIL_QUARTER = MOE_TM // TAIL_STEPS
N_CLASSES = N_EXPERT_GROUPS * len(PAIR_SLOT_A)
MOE_TILES = T_ALL // MOE_TM + N_CLASSES
MOE_ROWS = MOE_TILES * MOE_TM
SPARE_ROWS = N_CLASSES * TAIL_QUARTER
OUT_ROWS = T_ALL + SPARE_ROWS
ROUTE_LANES = 128
HX_WIDTH = 2 * D_MODEL + ROUTE_LANES
VMEM_LIMIT = 56 * 1024 * 1024
TRANS_B = (((1,), (1,)), ((), ()))


def _sigmoid(x):
    return 1.0 / (1.0 + jnp.exp(-x))


def _gelu_tanh(x):
    c = math.sqrt(2.0 / math.pi)
    return x * (0.5 * (1.0 + jnp.tanh(c * (x + 0.044715 * (x * x * x)))))


def _split_bf16(a):
    hi = a.astype(BF16)
    return hi, (a - hi.astype(F32)).astype(BF16)


def _rmsnorm(x, w):
    return x * lax.rsqrt(jnp.mean(x * x, axis=-1, keepdims=True) + EPS) * w


def _per_ps(fn, a, *mods):
    rows, d = a.shape
    out = fn(a.reshape(rows // N_PS, N_PS, d), *[m[None] for m in mods])
    return out.reshape(rows, d)


def _params(*sem):
    return pltpu.CompilerParams(dimension_semantics=sem, vmem_limit_bytes=VMEM_LIMIT)


def _mod_kernel(c_ref, w_ref, b_ref, o_ref):
    c = c_ref[...]
    part = jnp.dot((c * _sigmoid(c)).astype(BF16), w_ref[...].astype(BF16), preferred_element_type=F32)

    @pl.when(pl.program_id(1) == 0)
    def _():
        o_ref[...] = part + b_ref[...]

    @pl.when(pl.program_id(1) > 0)
    def _():
        o_ref[...] += part


def _modulation(cvec, w_mod, b_mod):
    kb = D_MODEL // MOD_K_STEPS
    c_blocks = cvec.reshape(MOD_ROWS, MOD_K_STEPS, kb).transpose(1, 0, 2)
    return pl.pallas_call(
        _mod_kernel,
        grid=(DEPTH, MOD_K_STEPS),
        in_specs=[
            pl.BlockSpec((None, MOD_ROWS, kb), lambda l, k: (k, 0, 0)),
            pl.BlockSpec((None, kb, N_MOD * D_MODEL), lambda l, k: (l, k, 0)),
            pl.BlockSpec((None, 1, N_MOD * D_MODEL), lambda l, k: (l, 0, 0)),
        ],
        out_specs=pl.BlockSpec((None, MOD_ROWS, N_MOD * D_MODEL), lambda l, k: (l, 0, 0)),
        out_shape=jax.ShapeDtypeStruct((DEPTH, MOD_ROWS, N_MOD * D_MODEL), F32),
        compiler_params=_params("arbitrary", "arbitrary"),
        name="adaln_mod",
    )(c_blocks, w_mod, b_mod.reshape(DEPTH, 1, N_MOD * D_MODEL))


def _first_step():
    return jnp.logical_and(pl.program_id(0) == 0, pl.program_id(1) == 0)


def _in_kernel(x_ref, mod_ref, nw_ref, w_ref, lnw_ref, lnb_ref, ug_ref, vn_ref, xt_ref, wg_ref, wst_ref):
    @pl.when(_first_step())
    def _():
        wg_ref[...] = w_ref[:, D_SSM:].astype(BF16)
        wst_ref[...] = w_ref[:, :D_SSM].T.astype(BF16)

    y = _rmsnorm(x_ref[...], nw_ref[...])
    h = _per_ps(lambda a, sc, sh: a * (1.0 + sc) + sh, y, mod_ref[1], mod_ref[0]).astype(BF16)
    zg = _gelu_tanh(jnp.dot(h, wg_ref[...], preferred_element_type=F32))
    v = zg[:, D_GMLP:]
    mu = jnp.mean(v, axis=-1, keepdims=True)
    vc = v - mu
    var = jnp.mean(vc * vc, axis=-1, keepdims=True)
    vn = vc * lax.rsqrt(var + EPS) * lnw_ref[...] + lnb_ref[...]
    for hd in range(GMLP_HEADS):
        cols = slice(hd * GMLP_HEAD_DIM, (hd + 1) * GMLP_HEAD_DIM)
        ug_ref[hd] = zg[:, cols]
        vn_ref[hd] = vn[:, cols]
    xt = lax.dot_general(wst_ref[...], h, TRANS_B, preferred_element_type=F32).astype(BF16)
    for k in range(J_PER_STEP):
        xt_ref[k] = xt[:, k * J_ROWS:(k + 1) * J_ROWS]


def _input_proj(x, mod8, l, norm1_w, w_in, ln_w, ln_b):
    tm = TOKEN_TILE
    steps = TILE_TOKENS // tm
    tok = lambda n: pl.BlockSpec((tm, n), lambda t, s: (t * steps + s, 0))
    heads = pl.BlockSpec((GMLP_HEADS, tm, GMLP_HEAD_DIM), lambda t, s: (0, t * steps + s, 0))
    lay = lambda *shape: pl.BlockSpec((None,) + shape, lambda t, s: (l,) + (0,) * len(shape))
    return pl.pallas_call(
        _in_kernel,
        grid=(N_TILES, steps),
        in_specs=[
            tok(D_MODEL),
            pl.BlockSpec((None, None, N_MOD, N_PS, D_MODEL), lambda t, s: (l, t, 0, 0, 0)),
            lay(1, D_MODEL), lay(D_MODEL, D_SSM + 2 * D_GMLP), lay(1, D_GMLP), lay(1, D_GMLP),
        ],
        out_specs=[heads, heads,
                   pl.BlockSpec((None, J_PER_STEP, D_SSM, J_ROWS), lambda t, s: (t, s, 0, 0))],
        out_shape=[jax.ShapeDtypeStruct((GMLP_HEADS, T_ALL, GMLP_HEAD_DIM), F32)] * 2 + [
            jax.ShapeDtypeStruct((N_TILES, SCAN_CHUNK, D_SSM, J_ROWS), BF16)],
        scratch_shapes=[pltpu.VMEM((D_MODEL, 2 * D_GMLP), BF16), pltpu.VMEM((D_SSM, D_MODEL), BF16)],
        compiler_params=_params("arbitrary", "arbitrary"),
        name="norm1_in_proj",
    )(x, mod8, norm1_w.reshape(DEPTH, 1, D_MODEL), w_in,
      ln_w.reshape(DEPTH, 1, D_GMLP), ln_b.reshape(DEPTH, 1, D_GMLP))


def _shift_lanes_right(a, b, s, lane):
    if s == 0:
        return a, b
    if s == 128:
        return jnp.zeros_like(a), a
    if s < 128:
        ra = pltpu.roll(a, s, 1)
        rb = pltpu.roll(b, s, 1)
        return jnp.where(lane >= s, ra, 0.0), jnp.where(lane >= s, rb, ra)
    t = s - 128
    return jnp.zeros_like(a), jnp.where(lane >= t, pltpu.roll(a, t, 1), 0.0)


def _shift_lanes_left(a, b, s, lane):
    if s == 0:
        return a, b
    if s == 128:
        return b, jnp.zeros_like(b)
    if s < 128:
        ra = pltpu.roll(a, 128 - s, 1)
        rb = pltpu.roll(b, 128 - s, 1)
        return jnp.where(lane < 128 - s, ra, rb), jnp.where(lane < 128 - s, rb, 0.0)
    t = s - 128
    return jnp.where(lane < 128 - t, pltpu.roll(b, 128 - t, 1), 0.0), jnp.zeros_like(b)


def _prep_kernel(*refs):
    for g in range(PREP_GROUPS):
        _prep_group(*[r.at[g] for r in refs])


def _prep_group(vec_ref, mat_ref, wft_ref, cct_ref, ttt_ref, a16_ref, wf_scr, cm_scr, tt_scr):
    a_re = vec_ref[0:1, :]
    a_im = vec_ref[1:2, :]
    dt = jnp.exp(vec_ref[2:3, :])
    d_skip = vec_ref[3:4, :]
    mag = jnp.exp(a_re * dt)
    ang = a_im * dt
    ab_r = mag * jnp.cos(ang)
    ab_i = mag * jnp.sin(ang)
    den = a_re * a_re + a_im * a_im
    nr = ab_r - 1.0
    q_r = (nr * a_re + ab_i * a_im) / den
    q_i = (ab_i * a_re - nr * a_im) / den
    bt_r = mat_ref[0]
    bt_i = mat_ref[1]
    c_r = mat_ref[2]
    c_i = mat_ref[3]
    bb_r = q_r * bt_r - q_i * bt_i
    bb_i = q_r * bt_i + q_i * bt_r
    p_r = [jnp.ones_like(ab_r)]
    p_i = [jnp.zeros_like(ab_r)]
    for _ in range(SCAN_CHUNK):
        pr, pi = p_r[-1], p_i[-1]
        p_r.append(pr * ab_r - pi * ab_i)
        p_i.append(pr * ab_i + pi * ab_r)
    a16_ref[0] = jnp.broadcast_to(p_r[SCAN_CHUNK], (N_PS, 128))
    a16_ref[1] = jnp.broadcast_to(p_i[SCAN_CHUNK], (N_PS, 128))

    lane = lax.broadcasted_iota(jnp.int32, (1, 128), 1)
    is_fwd = lane < SSM_STATE

    def pick(mf, mb):
        return jnp.where(is_fwd, p_r[mf], p_r[mb]), jnp.where(is_fwd, p_i[mf], p_i[mb])

    for j in range(SCAN_CHUNK):
        rows = slice(j * SSM_GROUP, (j + 1) * SSM_GROUP)
        wr, wi = pick(SCAN_CHUNK - 1 - j, j)
        wf_scr[rows, 0:128] = bb_r * wr - bb_i * wi
        wf_scr[rows, 128:256] = bb_r * wi + bb_i * wr
        wr, wi = pick(j + 1, SCAN_CHUNK - j)
        cct_ref[rows, 0:128] = (c_r * wr - c_i * wi).astype(BF16)
        cct_ref[rows, 128:256] = (-(c_r * wi + c_i * wr)).astype(BF16)
        wr, wi = pick(j, SCAN_CHUNK - 1 - j)
        cm_scr[rows, 0:128] = c_r * wr - c_i * wi
        cm_scr[rows, 128:256] = c_r * wi + c_i * wr
    wft_ref[...] = wf_scr[...].T.astype(BF16)

    zero = jnp.zeros_like(bb_r)
    cm_hi, cm_lo = _split_bf16(cm_scr[...])
    dot_t = lambda a, b: lax.dot_general(a, b, TRANS_B, preferred_element_type=F32)

    def lag_rows(keep):
        lhs = jnp.concatenate([jnp.where(keep, bb_r, zero), jnp.where(keep, -bb_i, zero)], axis=1)
        hi, lo = _split_bf16(lhs)
        return dot_t(hi, cm_hi) + (dot_t(hi, cm_lo) + dot_t(lo, cm_hi))

    mf = lag_rows(is_fwd)
    mb = lag_rows(jnp.logical_not(is_fwd))
    mf_a, mf_b = mf[:, 0:128], mf[:, 128:256]
    mb_a, mb_b = mb[:, 0:128], mb[:, 128:256]
    row_h = lax.broadcasted_iota(jnp.int32, (SSM_GROUP, 128), 0)
    lane_h = lax.broadcasted_iota(jnp.int32, (SSM_GROUP, 128), 1)
    for jp in range(SCAN_CHUNK):
        rows = slice(jp * SSM_GROUP, (jp + 1) * SSM_GROUP)
        fa, fb = _shift_lanes_right(mf_a, mf_b, SSM_GROUP * jp, lane)
        ba, bb = _shift_lanes_left(mb_a, mb_b, SSM_GROUP * (SCAN_CHUNK - 1 - jp), lane)
        diag = SSM_GROUP * jp + row_h
        tt_scr[rows, 0:128] = fa + ba + jnp.where(lane_h == diag, d_skip, 0.0)
        tt_scr[rows, 128:256] = fb + bb + jnp.where(lane_h + 128 == diag, d_skip, 0.0)
    ttt_ref[...] = tt_scr[...].T.astype(BF16)


def _ssm_operators(vecs, mats):
    op = jax.ShapeDtypeStruct((DEPTH, N_GROUPS, CHUNK_WIDTH, CHUNK_WIDTH), BF16)
    pg = PREP_GROUPS
    op_spec = pl.BlockSpec((None, pg, CHUNK_WIDTH, CHUNK_WIDTH), lambda l, g: (l, g, 0, 0))
    sq = pltpu.VMEM((pg, CHUNK_WIDTH, CHUNK_WIDTH), F32)
    return pl.pallas_call(
        _prep_kernel,
        grid=(DEPTH, N_GROUPS // pg),
        in_specs=[
            pl.BlockSpec((None, pg, 8, 128), lambda l, g: (l, g, 0, 0)),
            pl.BlockSpec((None, pg, 4, SSM_GROUP, 128), lambda l, g: (l, g, 0, 0, 0)),
        ],
        out_specs=[op_spec, op_spec, op_spec,
                   pl.BlockSpec((None, pg, 2, N_PS, 128), lambda l, g: (l, g, 0, 0, 0))],
        out_shape=[op, op, op, jax.ShapeDtypeStruct((DEPTH, N_GROUPS, 2, N_PS, 128), F32)],
        scratch_shapes=[sq, sq, sq],
        compiler_params=_params("arbitrary", "arbitrary"),
        name="s5_operators",
    )(vecs, mats)


def _ssm_kernel(xt_ref, wft_ref, cct_ref, ttt_ref, a16_ref, s0_ref, yt_ref, fin_ref, s_scr, f_scr, ft_scr):
    tile = pl.program_id(0)
    lane = lax.broadcasted_iota(jnp.int32, (GROUP_BLOCK, N_PS, 128), 2)
    is_fwd = lane < SSM_STATE
    half = SSM_STATE

    def group_x(gl):
        return xt_ref[:, gl * SSM_GROUP:(gl + 1) * SSM_GROUP, :].reshape(CHUNK_WIDTH, J_ROWS)

    for gl in range(GROUP_BLOCK):
        ft_scr[gl] = jnp.dot(wft_ref[gl], group_x(gl), preferred_element_type=F32)
        f_scr[gl] = ft_scr[gl].T

    a_r = a16_ref[:, 0]
    a_i = a16_ref[:, 1]

    def scan(s_r, s_i):
        for i in range(SEG_CHUNKS):
            rf = slice(i * N_PS, (i + 1) * N_PS)
            rb = slice((SEG_CHUNKS - 1 - i) * N_PS, (SEG_CHUNKS - i) * N_PS)
            s_scr[:, rf, 0:half] = s_r[:, :, 0:half]
            s_scr[:, rf, 128:128 + half] = s_i[:, :, 0:half]
            s_scr[:, rb, half:128] = s_r[:, :, half:128]
            s_scr[:, rb, 128 + half:256] = s_i[:, :, half:128]
            f_r = jnp.where(is_fwd, f_scr[:, rf, 0:128], f_scr[:, rb, 0:128])
            f_i = jnp.where(is_fwd, f_scr[:, rf, 128:256], f_scr[:, rb, 128:256])
            s_r, s_i = a_r * s_r - a_i * s_i + f_r, a_r * s_i + a_i * s_r + f_i
        return s_r, s_i

    zeros = jnp.zeros((GROUP_BLOCK, N_PS, 128), F32)
    z_r, z_i = scan(zeros, zeros)
    fin_ref[:, :, 0:128] = z_r
    fin_ref[:, :, 128:256] = z_i

    @pl.when(tile == SAMPLE_TILE)
    def _():
        b_r, b_i = a_r, a_i
        for _ in range(4):
            b_r, b_i = b_r * b_r - b_i * b_i, 2.0 * (b_r * b_i)
        seg = lax.broadcasted_iota(jnp.int32, (GROUP_BLOCK, N_PS, 128), 1) % SEGS_PER_SAMPLE_SEQ
        i_r = s0_ref[:, :, 0:128]
        i_i = s0_ref[:, :, 128:256]
        for step in range(1, SEGS_PER_SAMPLE_SEQ):
            pr = jnp.where(is_fwd, pltpu.roll(i_r, 1, 1), pltpu.roll(i_r, N_PS - 1, 1))
            pi = jnp.where(is_fwd, pltpu.roll(i_i, 1, 1), pltpu.roll(i_i, N_PS - 1, 1))
            zr = jnp.where(is_fwd, pltpu.roll(z_r, 1, 1), pltpu.roll(z_r, N_PS - 1, 1))
            zi = jnp.where(is_fwd, pltpu.roll(z_i, 1, 1), pltpu.roll(z_i, N_PS - 1, 1))
            n_r = b_r * pr - b_i * pi + zr
            n_i = b_r * pi + b_i * pr + zi
            first = jnp.where(is_fwd, step, 0)
            last = jnp.where(is_fwd, SEGS_PER_SAMPLE_SEQ - 1, SEGS_PER_SAMPLE_SEQ - 1 - step)
            upd = jnp.logical_and(seg >= first, seg <= last)
            i_r = jnp.where(upd, n_r, i_r)
            i_i = jnp.where(upd, n_i, i_i)
        scan(i_r, i_i)

    for gl in range(GROUP_BLOCK):
        yt = jnp.dot(ttt_ref[gl], group_x(gl), preferred_element_type=F32)
        yt += lax.dot_general(cct_ref[gl], s_scr[gl].astype(BF16), TRANS_B, preferred_element_type=F32)
        yt_ref[:, gl * SSM_GROUP:(gl + 1) * SSM_GROUP, :] = yt.reshape(SCAN_CHUNK, SSM_GROUP, J_ROWS)


def _ssm_scan(xt, wft, cct, ttt, a16, s0, l):
    gb = GROUP_BLOCK
    op_spec = pl.BlockSpec((None, gb, CHUNK_WIDTH, CHUNK_WIDTH), lambda t, g: (l, g, 0, 0))
    io_spec = pl.BlockSpec((None, SCAN_CHUNK, gb * SSM_GROUP, J_ROWS), lambda t, g: (t, 0, g, 0))
    return pl.pallas_call(
        _ssm_kernel,
        grid=(N_TILES, N_GROUPS // gb),
        in_specs=[
            io_spec, op_spec, op_spec, op_spec,
            pl.BlockSpec((None, gb, 2, N_PS, 128), lambda t, g: (l, g, 0, 0, 0)),
            pl.BlockSpec((gb, N_PS, CHUNK_WIDTH), lambda t, g: (g, 0, 0)),
        ],
        out_specs=[io_spec, pl.BlockSpec((None, gb, N_PS, CHUNK_WIDTH), lambda t, g: (t, g, 0, 0))],
        out_shape=[
            jax.ShapeDtypeStruct((N_TILES, SCAN_CHUNK, D_SSM, J_ROWS), F32),
            jax.ShapeDtypeStruct((N_TILES, N_GROUPS, N_PS, CHUNK_WIDTH), F32),
        ],
        scratch_shapes=[pltpu.VMEM((gb, J_ROWS, CHUNK_WIDTH), F32), pltpu.VMEM((gb, J_ROWS, CHUNK_WIDTH), F32),
                        pltpu.VMEM((gb, CHUNK_WIDTH, J_ROWS), F32)],
        compiler_params=_params("arbitrary", "arbitrary"),
        name="s5_chunk_scan",
    )(xt, wft, cct, ttt, a16, s0)


def _post_kernel(x_ref, yt_ref, ug_ref, vn_ref, mod_ref, wglu_ref, ws_ref, bs_ref, wout_f32_ref,
                 nw_ref, rwt_ref, rb_ref, hx_ref, route_ref, yg_scr, y_scr, wglut_ref, wout_ref):
    tile = pl.program_id(0)
    step = pl.program_id(1)

    @pl.when(_first_step())
    def _():
        wglut_ref[...] = wglu_ref[...].T.astype(BF16)
        wout_ref[...] = wout_f32_ref[...].astype(BF16)

    @pl.when(step == 0)
    def _():
        def chunk(n, carry):
            ps = n % N_PS
            c_hi = n // N_PS
            base = c_hi * (GMLP_CHUNK // SCAN_CHUNK) * N_PS + ps
            rows = [pl.ds(j * J_ROWS + base, GMLP_CHUNK // SCAN_CHUNK, stride=N_PS) for j in range(SCAN_CHUNK)]
            for h in range(GMLP_HEADS):
                v = jnp.concatenate([vn_ref[h, r, :] for r in rows], axis=0).astype(BF16)
                u = jnp.concatenate([ug_ref[h, r, :] for r in rows], axis=0)
                s = jnp.dot(ws_ref[h], v, preferred_element_type=F32) + bs_ref[h]
                yg = u * s
                for j, r in enumerate(rows):
                    yg_scr[h, r, :] = yg[j * 8:(j + 1) * 8]
            return carry

        lax.fori_loop(0, TILE_TOKENS // GMLP_CHUNK, chunk, 0)

    yt = _gelu_tanh(jnp.concatenate([yt_ref[k] for k in range(J_PER_STEP)], axis=1))
    yt = yt * _sigmoid(jnp.dot(wglut_ref[...], yt.astype(BF16), preferred_element_type=F32))
    for k in range(J_PER_STEP):
        y_scr[k * J_ROWS:(k + 1) * J_ROWS, :] = yt[:, k * J_ROWS:(k + 1) * J_ROWS].T
    row0 = pl.multiple_of(step * TOKEN_TILE, TOKEN_TILE)
    proj = jnp.dot(y_scr[...].astype(BF16), wout_ref[0:D_SSM, :], preferred_element_type=F32)
    yg = jnp.concatenate([yg_scr[h, pl.ds(row0, TOKEN_TILE), :] for h in range(GMLP_HEADS)], axis=1)
    proj += jnp.dot(yg.astype(BF16), wout_ref[D_SSM:, :], preferred_element_type=F32)
    x1 = x_ref[...] + _per_ps(lambda a, g: a * g, proj, mod_ref[2])
    h2 = _per_ps(lambda a, sc, sh: a * (1.0 + sc) + sh, _rmsnorm(x1, nw_ref[...]), mod_ref[4], mod_ref[3])
    hx_ref[:, 0:D_MODEL] = h2
    hx_ref[:, D_MODEL:2 * D_MODEL] = x1
    r_hi, r_lo = _split_bf16(rwt_ref[...])
    h_hi, h_lo = _split_bf16(h2)
    dot_t = lambda a, b: lax.dot_general(a, b, TRANS_B, preferred_element_type=F32)
    logits = dot_t(r_hi, h_hi) + (dot_t(r_hi, h_lo) + dot_t(r_lo, h_hi))
    scores = _sigmoid(logits)
    sel = scores + rb_ref[...]
    sc = [scores[e:e + 1, :] for e in range(N_EXPERTS)]
    sl = [sel[e:e + 1, :] for e in range(N_EXPERTS)]
    gscore = []
    for g in range(N_EXPERT_GROUPS):
        v0, v1, v2, v3 = sl[4 * g:4 * g + 4]
        hi01, lo01 = jnp.maximum(v0, v1), jnp.minimum(v0, v1)
        hi23, lo23 = jnp.maximum(v2, v3), jnp.minimum(v2, v3)
        top1 = jnp.maximum(hi01, hi23)
        top2 = jnp.maximum(jnp.minimum(hi01, hi23), jnp.maximum(lo01, lo23))
        gscore.append(top1 + top2)
    best = gscore[0]
    gidx = jnp.zeros_like(best, dtype=jnp.int32)
    for g in range(1, N_EXPERT_GROUPS):
        upd = gscore[g] > best
        gidx = jnp.where(upd, g, gidx)
        best = jnp.where(upd, gscore[g], best)

    def in_group(vals, k):
        out = vals[k]
        for g in range(1, N_EXPERT_GROUPS):
            out = jnp.where(gidx == g, vals[4 * g + k], out)
        return out

    v = [in_group(sl, k) for k in range(EXPERTS_PER_GROUP)]
    s = [in_group(sc, k) for k in range(EXPERTS_PER_GROUP)]
    w = []
    bits = jnp.zeros_like(gidx)
    for k in range(EXPERTS_PER_GROUP):
        rank = jnp.zeros_like(gidx)
        for j in range(EXPERTS_PER_GROUP):
            if j == k:
                continue
            ahead = (v[j] >= v[k]) if j < k else (v[j] > v[k])
            rank = rank + ahead.astype(jnp.int32)
        w.append(jnp.where(rank < 2, s[k], 0.0))
        bits = bits + jnp.where(rank < 2, 1 << k, 0)
    denom = (w[0] + w[1]) + (w[2] + w[3])
    gate = [wk / denom for wk in w]
    pair = jnp.full_like(gidx, len(PAIR_SLOT_A) - 1)
    for p in range(len(PAIR_SLOT_A) - 1):
        pair = jnp.where(bits == (1 << PAIR_SLOT_A[p]) + (1 << PAIR_SLOT_B[p]), p, pair)

    def slot_gate(table):
        out = gate[table[0]]
        for p in range(1, len(table)):
            out = jnp.where(pair == p, gate[table[p]], out)
        return out

    n_tok = route_ref.shape[1]
    route_ref[0:1, :] = (gidx * len(PAIR_SLOT_A) + pair).astype(F32)
    ps = lax.broadcasted_iota(jnp.int32, (1, n_tok), 1) % N_PS
    cond = jnp.where(tile < SAMPLE_TILE, 0, 1 + ps // SEGS_PER_SAMPLE_SEQ).astype(F32)
    lanes = jnp.concatenate([slot_gate(PAIR_SLOT_A), slot_gate(PAIR_SLOT_B), cond,
                             jnp.zeros((ROUTE_LANES - 3, n_tok), F32)], axis=0)
    hx_ref[:, 2 * D_MODEL:] = lanes.T


def _mix_out(x, yt, ug, vn, mod8, l, w_glu, w_s, b_s, w_out, norm2_w, rwt, rb):
    tm = TOKEN_TILE
    steps = TILE_TOKENS // tm
    tok = lambda n: pl.BlockSpec((tm, n), lambda t, s: (t * steps + s, 0))
    whole_tile = pl.BlockSpec((GMLP_HEADS, TILE_TOKENS, GMLP_HEAD_DIM), lambda t, s: (0, t, 0))
    lay = lambda *shape: pl.BlockSpec((None,) + shape, lambda t, s: (l,) + (0,) * len(shape))
    return pl.pallas_call(
        _post_kernel,
        grid=(N_TILES, steps),
        in_specs=[
            tok(D_MODEL),
            pl.BlockSpec((None, J_PER_STEP, D_SSM, J_ROWS), lambda t, s: (t, s, 0, 0)),
            whole_tile, whole_tile,
            pl.BlockSpec((None, None, N_MOD, N_PS, D_MODEL), lambda t, s: (l, t, 0, 0, 0)),
            lay(D_SSM, D_SSM), lay(GMLP_HEADS, GMLP_CHUNK, GMLP_CHUNK),
            lay(GMLP_HEADS, GMLP_CHUNK, 1), lay(D_MODEL, D_MODEL), lay(1, D_MODEL),
            pl.BlockSpec((N_EXPERTS, D_MODEL), lambda t, s: (0, 0)),
            pl.BlockSpec((N_EXPERTS, 1), lambda t, s: (0, 0)),
        ],
        out_specs=[tok(HX_WIDTH), pl.BlockSpec((1, tm), lambda t, s: (0, t * steps + s))],
        out_shape=[
            jax.ShapeDtypeStruct((T_ALL, HX_WIDTH), F32),
            jax.ShapeDtypeStruct((1, T_ALL), F32),
        ],
        scratch_shapes=[pltpu.VMEM((GMLP_HEADS, TILE_TOKENS, GMLP_HEAD_DIM), F32), pltpu.VMEM((tm, D_SSM), F32),
                        pltpu.VMEM((D_SSM, D_SSM), BF16), pltpu.VMEM((D_MODEL, D_MODEL), BF16)],
        compiler_params=_params("arbitrary", "arbitrary"),
        name="mixers_out_router",
    )(x, yt, ug, vn, mod8, w_glu, w_s, b_s, w_out, norm2_w.reshape(DEPTH, 1, D_MODEL), rwt, rb)


def _moe_kernel(ea_ref, eb_ref, new_a_ref, new_b_ref, nv_ref, size_ref, src_ref, drow_ref,
                hx_hbm, wga_ref, wua_ref, wda_ref, wgb_ref, wub_ref, wdb_ref, g2_ref, fw_ref,
                o_hbm, wga_s, wua_s, wda_s, wgb_s, wub_s, wdb_s, hx_buf, o_buf, z_buf, g_sem, s_sem, z_sem,
                *, final):
    n = pl.program_id(0)
    n_valid = nv_ref[0]
    slot = n % 2

    def by_size(tile, fn):
        for quarters in range(1, TAIL_STEPS + 1):
            @pl.when(size_ref[tile] == quarters)
            def _(rows=quarters * TAIL_QUARTER):
                fn(rows)

    def start_gather(tile, sl, rows):
        base = tile * MOE_TM
        for r in range(rows):
            pltpu.make_async_copy(hx_hbm.at[pl.ds(src_ref[base + r], 1)], hx_buf.at[sl, pl.ds(r, 1)],
                                  g_sem.at[sl]).start()

    def wait_gather(sl, rows):
        pltpu.make_async_copy(hx_hbm.at[pl.ds(0, rows)], hx_buf.at[sl, pl.ds(0, rows)], g_sem.at[sl]).wait()

    def start_scatter(tile, sl, rows):
        base = tile * MOE_TM
        for r in range(rows):
            pltpu.make_async_copy(o_buf.at[sl, pl.ds(r, 1)], o_hbm.at[pl.ds(drow_ref[base + r], 1)],
                                  s_sem.at[sl]).start()

    def wait_scatter(sl, rows):
        pltpu.make_async_copy(o_buf.at[sl, pl.ds(0, rows)], o_hbm.at[pl.ds(0, rows)], s_sem.at[sl]).wait()

    @pl.when(n == 0)
    def _():
        z_buf[...] = jnp.zeros_like(z_buf)
        spare = [pltpu.make_async_copy(z_buf, o_hbm.at[pl.ds(T_ALL + i * MOE_TM, MOE_TM)], z_sem)
                 for i in range(SPARE_ROWS // MOE_TM)]
        for cp in spare:
            cp.start()
        for cp in spare:
            cp.wait()
        by_size(0, lambda rows: start_gather(0, 0, rows))

    @pl.when(new_a_ref[n] == 1)
    def _():
        wga_s[...] = wga_ref[...].astype(BF16)
        wua_s[...] = wua_ref[...].astype(BF16)
        wda_s[...] = wda_ref[...].astype(BF16)

    @pl.when(new_b_ref[n] == 1)
    def _():
        wgb_s[...] = wgb_ref[...].astype(BF16)
        wub_s[...] = wub_ref[...].astype(BF16)
        wdb_s[...] = wdb_ref[...].astype(BF16)

    @pl.when(n + 1 < n_valid)
    def _():
        by_size(n + 1, lambda rows: start_gather(n + 1, 1 - slot, rows))

    def tile_step(rows):
        wait_gather(slot, rows)

        @pl.when(n >= 2)
        def _():
            by_size(n - 2, lambda r: wait_scatter(slot, r))

        h = hx_buf[slot, 0:rows, 0:D_MODEL].astype(BF16)
        lanes = hx_buf[slot, 0:rows, 2 * D_MODEL:]

        def expert(wg, wu, wd, gate):
            hg = jnp.dot(h, wg[...], preferred_element_type=F32)
            hu = jnp.dot(h, wu[...], preferred_element_type=F32)
            act = hg * _sigmoid(hg) * hu * gate
            return jnp.dot(act.astype(BF16), wd[...], preferred_element_type=F32)

        y = expert(wga_s, wua_s, wda_s, lanes[:, 0:1]) + expert(wgb_s, wub_s, wdb_s, lanes[:, 1:2])
        cond_row = lanes[:, 2:3]
        gate2 = jnp.where(cond_row == 0.0, g2_ref[0:1, :], jnp.where(cond_row == 1.0, g2_ref[1:2, :], g2_ref[2:3, :]))
        x2 = hx_buf[slot, 0:rows, D_MODEL:2 * D_MODEL] + gate2 * y
        o_buf[slot, 0:rows] = _rmsnorm(x2, fw_ref[...]) if final else x2
        start_scatter(n, slot, rows)

    @pl.when(n < n_valid)
    def _():
        by_size(n, tile_step)

    @pl.when(n == MOE_TILES - 1)
    def _():
        last = n_valid - 1
        by_size(last, lambda r: wait_scatter(last % 2, r))
        by_size(last - 1, lambda r: wait_scatter(1 - last % 2, r))


def _experts(tables, hx, mods, l, w_gate, w_up, w_down, final_w, final):
    w_a = lambda r, c: pl.BlockSpec((None, None, r, c), lambda n, ea, eb, *_: (l, ea[n], 0, 0))
    w_b = lambda r, c: pl.BlockSpec((None, None, r, c), lambda n, ea, eb, *_: (l, eb[n], 0, 0))
    up = pltpu.VMEM((D_MODEL, D_EXPERT), BF16)
    down = pltpu.VMEM((D_EXPERT, D_MODEL), BF16)
    return pl.pallas_call(
        functools.partial(_moe_kernel, final=final),
        grid_spec=pltpu.PrefetchScalarGridSpec(
            num_scalar_prefetch=len(tables),
            grid=(MOE_TILES,),
            in_specs=[
                pl.BlockSpec(memory_space=pl.ANY),
                w_a(D_MODEL, D_EXPERT), w_a(D_MODEL, D_EXPERT), w_a(D_EXPERT, D_MODEL),
                w_b(D_MODEL, D_EXPERT), w_b(D_MODEL, D_EXPERT), w_b(D_EXPERT, D_MODEL),
                pl.BlockSpec((None, MOD_ROWS, D_MODEL), lambda n, *_: (l, 0, N_MOD - 1)),
                pl.BlockSpec((1, D_MODEL), lambda n, *_: (0, 0)),
            ],
            out_specs=pl.BlockSpec(memory_space=pl.ANY),
            scratch_shapes=[up, up, down, up, up, down,
                            pltpu.VMEM((2, MOE_TM, HX_WIDTH), F32), pltpu.VMEM((2, MOE_TM, D_MODEL), F32),
                            pltpu.VMEM((MOE_TM, D_MODEL), F32),
                            pltpu.SemaphoreType.DMA((2,)), pltpu.SemaphoreType.DMA((2,)), pltpu.SemaphoreType.DMA(())],
        ),
        out_shape=jax.ShapeDtypeStruct((OUT_ROWS, D_MODEL), F32),
        compiler_params=_params("arbitrary"),
        name="experts",
    )(*tables, hx, w_gate, w_up, w_down, w_gate, w_up, w_down, mods, final_w.reshape(1, D_MODEL))


def _count_before(flags):
    n, k = flags.shape
    blocks = flags.reshape(n // 128, 128, k).astype(F32)
    strictly_lower = jnp.tril(jnp.ones((128, 128), F32), -1)
    within = jnp.einsum("ij,bjk->bik", strictly_lower, blocks)
    totals = jnp.sum(blocks, axis=1)
    before = jnp.cumsum(totals, axis=0) - totals
    return (within + before[:, None, :]).reshape(n, k).astype(jnp.int32)


def _routing_tables(cls, to_sequence_order):
    n_cls = N_EXPERT_GROUPS * len(PAIR_SLOT_A)
    onehot = (cls[:, None] == jnp.arange(n_cls, dtype=jnp.int32)[None, :]).astype(jnp.int32)
    counts = jnp.sum(onehot, axis=0)
    tiles = (counts + MOE_TM - 1) // MOE_TM
    tile_end = jnp.cumsum(tiles)
    row_in_class = (tile_end - tiles)[None, :] * MOE_TM + _count_before(onehot)
    dst = jnp.sum(onehot * row_in_class, axis=1)
    token_plus_1 = jnp.zeros((MOE_ROWS,), jnp.int32).at[dst].set(jnp.arange(1, T_ALL + 1, dtype=jnp.int32))
    is_pad = token_plus_1 == 0
    src = jnp.maximum(token_plus_1 - 1, 0)
    n_valid = tile_end[-1]
    tile_id = jnp.minimum(jnp.arange(MOE_TILES, dtype=jnp.int32), n_valid - 1)
    tile_cls = jnp.sum((tile_end[None, :] <= tile_id[:, None]).astype(jnp.int32), axis=1)
    group, pair = tile_cls // len(PAIR_SLOT_A), tile_cls % len(PAIR_SLOT_A)
    in_cls = (tile_cls[:, None] == jnp.arange(n_cls, dtype=jnp.int32)[None, :]).astype(jnp.int32)
    left = jnp.sum(in_cls * (counts[None, :] - (tile_id[:, None] - (tile_end - tiles)[None, :]) * MOE_TM), axis=1)
    size = jnp.clip((left + TAIL_QUARTER - 1) // TAIL_QUARTER, 1, TAIL_STEPS).astype(jnp.int32)
    row = jnp.arange(MOE_ROWS, dtype=jnp.int32)
    processed = jnp.logical_and(row % MOE_TM < jnp.repeat(size, MOE_TM) * TAIL_QUARTER, row // MOE_TM < n_valid)
    moved_pad = jnp.logical_and(is_pad, processed)
    spare = T_ALL + _count_before(moved_pad.astype(jnp.int32)[:, None])[:, 0]
    if to_sequence_order:
        j, c, ps = (src // J_ROWS) % SCAN_CHUNK, (src // N_PS) % SEG_CHUNKS, src % N_PS
        target = (src // TILE_TOKENS) * TILE_TOKENS + ps * SEG_LEN + c * SCAN_CHUNK + j
    else:
        target = src
    drow = jnp.where(is_pad, jnp.where(moved_pad, spare, T_ALL), target).astype(jnp.int32)

    def slot_expert(table):
        local = jnp.full_like(pair, table[-1])
        for p in range(len(table) - 1):
            local = jnp.where(pair == p, table[p], local)
        return group * EXPERTS_PER_GROUP + local

    e_a = slot_expert(PAIR_SLOT_A)
    e_b = slot_expert(PAIR_SLOT_B)
    first = jnp.ones((1,), jnp.int32)
    new_a = jnp.concatenate([first, (e_a[1:] != e_a[:-1]).astype(jnp.int32)])
    new_b = jnp.concatenate([first, (e_b[1:] != e_b[:-1]).astype(jnp.int32)])
    return e_a, e_b, new_a, new_b, n_valid.reshape(1).astype(jnp.int32), size, src, drow


def _dirs_on_lanes(p):
    p = jnp.moveaxis(p, 1, -2)
    return p.reshape(p.shape[:-2] + (2 * SSM_STATE,))


def _to_internal_order(x):
    x = x.reshape(N_TILES, N_PS, SEG_CHUNKS, SCAN_CHUNK, D_MODEL)
    return x.transpose(0, 3, 2, 1, 4).reshape(T_ALL, D_MODEL)


def _gmlp_position_order(w):
    n_lo = GMLP_CHUNK // SCAN_CHUNK
    lead = w.shape[:2]
    w = w.reshape(lead + (n_lo, SCAN_CHUNK) + w.shape[3:])
    w = jnp.swapaxes(w, 2, 3)
    return w.reshape(lead + (GMLP_CHUNK,) + w.shape[4:])


def kernel(x_prompt, x_sample, c, state_ssm_re, state_ssm_im, c_ctx, norm1_w, norm2_w, w_mod, b_mod, w_in,
           ssm_a_re, ssm_a_im, ssm_log_dt, ssm_b_re, ssm_b_im, ssm_c_re, ssm_c_im, ssm_d, w_glu,
           gmlp_ln_w, gmlp_ln_b, gmlp_w_s, gmlp_b_s, w_out, router_w, router_b, w_gate, w_up, w_down,
           final_norm_w):
    x = jnp.concatenate([x_prompt.reshape(T_PROMPT, D_MODEL), x_sample.reshape(T_SAMPLE, D_MODEL)], axis=0)
    x = _to_internal_order(x)

    cvec = jnp.concatenate([c_ctx[None, :], c, jnp.zeros((MOD_ROWS - 1 - N_SAMPLE_SEQ, D_MODEL), F32)], axis=0)
    mods = _modulation(cvec, w_mod, b_mod)
    ps_row = [[0] * N_PS] * SAMPLE_TILE + [[1 + p // SEGS_PER_SAMPLE_SEQ for p in range(N_PS)]]
    mod8 = mods.reshape(DEPTH, MOD_ROWS, N_MOD, D_MODEL)[:, jnp.array(ps_row, jnp.int32)]
    mod8 = mod8.transpose(0, 1, 3, 2, 4)

    log_dt = jnp.broadcast_to(ssm_log_dt[..., None], ssm_a_re.shape)
    d_lanes = jnp.tile(ssm_d.reshape(DEPTH, N_GROUPS, SSM_GROUP), (1, 1, 128 // SSM_GROUP))
    vecs = jnp.stack([_dirs_on_lanes(ssm_a_re), _dirs_on_lanes(ssm_a_im), _dirs_on_lanes(log_dt), d_lanes], axis=2)
    vecs = jnp.concatenate([vecs, jnp.zeros((DEPTH, N_GROUPS, 4, 2 * SSM_STATE), F32)], axis=2)
    mats = jnp.stack([_dirs_on_lanes(jnp.swapaxes(ssm_b_re, -1, -2)), _dirs_on_lanes(jnp.swapaxes(ssm_b_im, -1, -2)),
                      _dirs_on_lanes(ssm_c_re), _dirs_on_lanes(ssm_c_im)], axis=2)
    wft, cct, ttt, a16 = _ssm_operators(vecs, mats)

    w_s = jnp.swapaxes(_gmlp_position_order(jnp.swapaxes(_gmlp_position_order(gmlp_w_s), 2, 3)), 2, 3).astype(BF16)
    b_s = _gmlp_position_order(gmlp_b_s)[..., None]
    rwt = router_w.T
    rb = router_b.reshape(N_EXPERTS, 1)

    new_re, new_im = [], []
    for l in range(DEPTH):
        ug, vn, xt = _input_proj(x, mod8, l, norm1_w, w_in, gmlp_ln_w, gmlp_ln_b)
        s0 = jnp.concatenate([state_ssm_re[:, l].transpose(2, 0, 1, 3).reshape(N_GROUPS, N_SAMPLE_SEQ, 128),
                              state_ssm_im[:, l].transpose(2, 0, 1, 3).reshape(N_GROUPS, N_SAMPLE_SEQ, 128)], axis=-1)
        s0 = jnp.repeat(s0, SEGS_PER_SAMPLE_SEQ, axis=1)
        yt, fin = _ssm_scan(xt, wft, cct, ttt, a16, s0, l)
        fin = fin[:SAMPLE_TILE].reshape(SAMPLE_TILE, N_GROUPS, N_PS, 2, 2, SSM_STATE)
        fin = fin.transpose(3, 0, 2, 4, 1, 5).reshape(2, N_PROMPT_SEQ, 2, N_GROUPS, SSM_STATE)
        new_re.append(fin[0])
        new_im.append(fin[1])
        hx, route = _mix_out(x, yt, ug, vn, mod8, l, w_glu, w_s, b_s, w_out, norm2_w, rwt, rb)
        final = l == DEPTH - 1
        tables = _routing_tables(route[0].astype(jnp.int32), to_sequence_order=final)
        x = _experts(tables, hx, mods, l, w_gate, w_up, w_down, final_norm_w, final)

    y_prompt = x[:T_PROMPT].reshape(N_PROMPT_SEQ, PROMPT_LEN, D_MODEL)
    y_sample = x[T_PROMPT:T_ALL].reshape(N_SAMPLE_SEQ, SAMPLE_LEN, D_MODEL)
    return (y_prompt, y_sample, jnp.stack(new_re, axis=1), jnp.stack(new_im, axis=1))
```

```python
import functools
import math

import jax
import jax.numpy as jnp
from jax import lax
from jax.experimental import pallas as pl
from jax.experimental.pallas import tpu as pltpu

F32 = jnp.float32
BF16 = jnp.bfloat16

D_MODEL = 1024
N_PROMPT_SEQ = 16
PROMPT_LEN = 256
N_SAMPLE_SEQ = 2
SAMPLE_LEN = 1024
T_PROMPT = N_PROMPT_SEQ * PROMPT_LEN
T_SAMPLE = N_SAMPLE_SEQ * SAMPLE_LEN
T_ALL = T_PROMPT + T_SAMPLE
DEPTH = 2
D_SSM = 512
SSM_GROUP = 16
N_GROUPS = 32
SSM_STATE = 64
D_GMLP = 512
GMLP_HEADS = 4
GMLP_HEAD_DIM = 128
GMLP_CHUNK = 128
N_EXPERTS = 16
N_EXPERT_GROUPS = 4
EXPERTS_PER_GROUP = 4
D_EXPERT = 512
N_MOD = 6
EPS = 1e-6

SCAN_CHUNK = 16
CHUNK_WIDTH = SCAN_CHUNK * SSM_GROUP
SEG_LEN = 256
SEG_CHUNKS = SEG_LEN // SCAN_CHUNK
N_PS = 8
TILE_TOKENS = N_PS * SEG_LEN
N_TILES = T_ALL // TILE_TOKENS
SAMPLE_TILE = T_PROMPT // TILE_TOKENS
SEGS_PER_SAMPLE_SEQ = SAMPLE_LEN // SEG_LEN
J_ROWS = SEG_CHUNKS * N_PS
GROUP_BLOCK = 8
PREP_GROUPS = 4
MOD_ROWS = 8
MOD_K_STEPS = 4

J_PER_STEP = 4
TOKEN_TILE = J_PER_STEP * J_ROWS
PAIR_SLOT_A = (0, 0, 0, 1, 1, 3)
PAIR_SLOT_B = (1, 2, 3, 3, 2, 2)
MOE_TM = 256
TAIL_STEPS = 8
TAIL_QUARTER = MOE_TM // TAIL_STEPS
N_CLASSES = N_EXPERT_GROUPS * len(PAIR_SLOT_A)
MOE_TILES = T_ALL // MOE_TM + N_CLASSES
MOE_ROWS = MOE_TILES * MOE_TM
SPARE_ROWS = N_CLASSES * TAIL_QUARTER
OUT_ROWS = T_ALL + SPARE_ROWS
ROUTE_LANES = 128
HX_WIDTH = 2 * D_MODEL + ROUTE_LANES
VMEM_LIMIT = 56 * 1024 * 1024
TRANS_B = (((1,), (1,)), ((), ()))


def _sigmoid(x):
    return 1.0 / (1.0 + jnp.exp(-x))


def _gelu_tanh(x):
    c = math.sqrt(2.0 / math.pi)
    return x * (0.5 * (1.0 + jnp.tanh(c * (x + 0.044715 * (x * x * x)))))


def _split_bf16(a):
    hi = a.astype(BF16)
    return hi, (a - hi.astype(F32)).astype(BF16)


def _rmsnorm(x, w):
    return x * lax.rsqrt(jnp.mean(x * x, axis=-1, keepdims=True) + EPS) * w


def _per_ps(fn, a, *mods):
    rows, d = a.shape
    out = fn(a.reshape(rows // N_PS, N_PS, d), *[m[None] for m in mods])
    return out.reshape(rows, d)


def _params(*sem):
    return pltpu.CompilerParams(dimension_semantics=sem, vmem_limit_bytes=VMEM_LIMIT)


def _mod_kernel(c_ref, w_ref, b_ref, o_ref):
    c = c_ref[...]
    part = jnp.dot((c * _sigmoid(c)).astype(BF16), w_ref[...].astype(BF16), preferred_element_type=F32)

    @pl.when(pl.program_id(1) == 0)
    def _():
        o_ref[...] = part + b_ref[...]

    @pl.when(pl.program_id(1) > 0)
    def _():
        o_ref[...] += part


def _modulation(cvec, w_mod, b_mod):
    kb = D_MODEL // MOD_K_STEPS
    c_blocks = cvec.reshape(MOD_ROWS, MOD_K_STEPS, kb).transpose(1, 0, 2)
    return pl.pallas_call(
        _mod_kernel,
        grid=(DEPTH, MOD_K_STEPS),
        in_specs=[
            pl.BlockSpec((None, MOD_ROWS, kb), lambda l, k: (k, 0, 0)),
            pl.BlockSpec((None, kb, N_MOD * D_MODEL), lambda l, k: (l, k, 0)),
            pl.BlockSpec((None, 1, N_MOD * D_MODEL), lambda l, k: (l, 0, 0)),
        ],
        out_specs=pl.BlockSpec((None, MOD_ROWS, N_MOD * D_MODEL), lambda l, k: (l, 0, 0)),
        out_shape=jax.ShapeDtypeStruct((DEPTH, MOD_ROWS, N_MOD * D_MODEL), F32),
        compiler_params=_params("arbitrary", "arbitrary"),
        name="adaln_mod",
    )(c_blocks, w_mod, b_mod.reshape(DEPTH, 1, N_MOD * D_MODEL))


def _first_step():
    return jnp.logical_and(pl.program_id(0) == 0, pl.program_id(1) == 0)


def _in_kernel(x_ref, mod_ref, nw_ref, w_ref, lnw_ref, lnb_ref, ug_ref, vn_ref, xt_ref, wg_ref, wst_ref):
    @pl.when(_first_step())
    def _():
        wg_ref[...] = w_ref[:, D_SSM:].astype(BF16)
        wst_ref[...] = w_ref[:, :D_SSM].T.astype(BF16)

    y = _rmsnorm(x_ref[...], nw_ref[...])
    h = _per_ps(lambda a, sc, sh: a * (1.0 + sc) + sh, y, mod_ref[1], mod_ref[0]).astype(BF16)
    zg = _gelu_tanh(jnp.dot(h, wg_ref[...], preferred_element_type=F32))
    v = zg[:, D_GMLP:]
    mu = jnp.mean(v, axis=-1, keepdims=True)
    vc = v - mu
    var = jnp.mean(vc * vc, axis=-1, keepdims=True)
    vn = vc * lax.rsqrt(var + EPS) * lnw_ref[...] + lnb_ref[...]
    for hd in range(GMLP_HEADS):
        cols = slice(hd * GMLP_HEAD_DIM, (hd + 1) * GMLP_HEAD_DIM)
        ug_ref[hd] = zg[:, cols]
        vn_ref[hd] = vn[:, cols]
    xt = lax.dot_general(wst_ref[...], h, TRANS_B, preferred_element_type=F32).astype(BF16)
    for k in range(J_PER_STEP):
        xt_ref[k] = xt[:, k * J_ROWS:(k + 1) * J_ROWS]


def _input_proj(x, mod8, l, norm1_w, w_in, ln_w, ln_b):
    tm = TOKEN_TILE
    steps = TILE_TOKENS // tm
    tok = lambda n: pl.BlockSpec((tm, n), lambda t, s: (t * steps + s, 0))
    heads = pl.BlockSpec((GMLP_HEADS, tm, GMLP_HEAD_DIM), lambda t, s: (0, t * steps + s, 0))
    lay = lambda *shape: pl.BlockSpec((None,) + shape, lambda t, s: (l,) + (0,) * len(shape))
    return pl.pallas_call(
        _in_kernel,
        grid=(N_TILES, steps),
        in_specs=[
            tok(D_MODEL),
            pl.BlockSpec((None, None, N_MOD, N_PS, D_MODEL), lambda t, s: (l, t, 0, 0, 0)),
            lay(1, D_MODEL), lay(D_MODEL, D_SSM + 2 * D_GMLP), lay(1, D_GMLP), lay(1, D_GMLP),
        ],
        out_specs=[heads, heads,
                   pl.BlockSpec((None, J_PER_STEP, D_SSM, J_ROWS), lambda t, s: (t, s, 0, 0))],
        out_shape=[jax.ShapeDtypeStruct((GMLP_HEADS, T_ALL, GMLP_HEAD_DIM), F32)] * 2 + [
            jax.ShapeDtypeStruct((N_TILES, SCAN_CHUNK, D_SSM, J_ROWS), BF16)],
        scratch_shapes=[pltpu.VMEM((D_MODEL, 2 * D_GMLP), BF16), pltpu.VMEM((D_SSM, D_MODEL), BF16)],
        compiler_params=_params("arbitrary", "arbitrary"),
        name="norm1_in_proj",
    )(x, mod8, norm1_w.reshape(DEPTH, 1, D_MODEL), w_in,
      ln_w.reshape(DEPTH, 1, D_GMLP), ln_b.reshape(DEPTH, 1, D_GMLP))


def _shift_lanes_right(a, b, s, lane):
    if s == 0:
        return a, b
    if s == 128:
        return jnp.zeros_like(a), a
    if s < 128:
        ra = pltpu.roll(a, s, 1)
        rb = pltpu.roll(b, s, 1)
        return jnp.where(lane >= s, ra, 0.0), jnp.where(lane >= s, rb, ra)
    t = s - 128
    return jnp.zeros_like(a), jnp.where(lane >= t, pltpu.roll(a, t, 1), 0.0)


def _shift_lanes_left(a, b, s, lane):
    if s == 0:
        return a, b
    if s == 128:
        return b, jnp.zeros_like(b)
    if s < 128:
        ra = pltpu.roll(a, 128 - s, 1)
        rb = pltpu.roll(b, 128 - s, 1)
        return jnp.where(lane < 128 - s, ra, rb), jnp.where(lane < 128 - s, rb, 0.0)
    t = s - 128
    return jnp.where(lane < 128 - t, pltpu.roll(b, 128 - t, 1), 0.0), jnp.zeros_like(b)


def _prep_kernel(*refs):
    for g in range(PREP_GROUPS):
        _prep_group(*[r.at[g] for r in refs])


def _prep_group(vec_ref, mat_ref, wft_ref, cct_ref, ttt_ref, a16_ref, wf_scr, cm_scr, tt_scr):
    a_re = vec_ref[0:1, :]
    a_im = vec_ref[1:2, :]
    dt = jnp.exp(vec_ref[2:3, :])
    d_skip = vec_ref[3:4, :]
    mag = jnp.exp(a_re * dt)
    ang = a_im * dt
    ab_r = mag * jnp.cos(ang)
    ab_i = mag * jnp.sin(ang)
    den = a_re * a_re + a_im * a_im
    nr = ab_r - 1.0
    q_r = (nr * a_re + ab_i * a_im) / den
    q_i = (ab_i * a_re - nr * a_im) / den
    bt_r = mat_ref[0]
    bt_i = mat_ref[1]
    c_r = mat_ref[2]
    c_i = mat_ref[3]
    bb_r = q_r * bt_r - q_i * bt_i
    bb_i = q_r * bt_i + q_i * bt_r
    p_r = [jnp.ones_like(ab_r)]
    p_i = [jnp.zeros_like(ab_r)]
    for _ in range(SCAN_CHUNK):
        pr, pi = p_r[-1], p_i[-1]
        p_r.append(pr * ab_r - pi * ab_i)
        p_i.append(pr * ab_i + pi * ab_r)
    a16_ref[0] = jnp.broadcast_to(p_r[SCAN_CHUNK], (N_PS, 128))
    a16_ref[1] = jnp.broadcast_to(p_i[SCAN_CHUNK], (N_PS, 128))

    lane = lax.broadcasted_iota(jnp.int32, (1, 128), 1)
    is_fwd = lane < SSM_STATE

    def pick(mf, mb):
        return jnp.where(is_fwd, p_r[mf], p_r[mb]), jnp.where(is_fwd, p_i[mf], p_i[mb])

    for j in range(SCAN_CHUNK):
        rows = slice(j * SSM_GROUP, (j + 1) * SSM_GROUP)
        wr, wi = pick(SCAN_CHUNK - 1 - j, j)
        wf_scr[rows, 0:128] = bb_r * wr - bb_i * wi
        wf_scr[rows, 128:256] = bb_r * wi + bb_i * wr
        wr, wi = pick(j + 1, SCAN_CHUNK - j)
        cct_ref[rows, 0:128] = (c_r * wr - c_i * wi).astype(BF16)
        cct_ref[rows, 128:256] = (-(c_r * wi + c_i * wr)).astype(BF16)
        wr, wi = pick(j, SCAN_CHUNK - 1 - j)
        cm_scr[rows, 0:128] = c_r * wr - c_i * wi
        cm_scr[rows, 128:256] = c_r * wi + c_i * wr
    wft_ref[...] = wf_scr[...].T.astype(BF16)

    zero = jnp.zeros_like(bb_r)
    cm_hi, cm_lo = _split_bf16(cm_scr[...])
    dot_t = lambda a, b: lax.dot_general(a, b, TRANS_B, preferred_element_type=F32)

    def lag_rows(keep):
        lhs = jnp.concatenate([jnp.where(keep, bb_r, zero), jnp.where(keep, -bb_i, zero)], axis=1)
        hi, lo = _split_bf16(lhs)
        return dot_t(hi, cm_hi) + (dot_t(hi, cm_lo) + dot_t(lo, cm_hi))

    mf = lag_rows(is_fwd)
    mb = lag_rows(jnp.logical_not(is_fwd))
    mf_a, mf_b = mf[:, 0:128], mf[:, 128:256]
    mb_a, mb_b = mb[:, 0:128], mb[:, 128:256]
    row_h = lax.broadcasted_iota(jnp.int32, (SSM_GROUP, 128), 0)
    lane_h = lax.broadcasted_iota(jnp.int32, (SSM_GROUP, 128), 1)
    for jp in range(SCAN_CHUNK):
        rows = slice(jp * SSM_GROUP, (jp + 1) * SSM_GROUP)
        fa, fb = _shift_lanes_right(mf_a, mf_b, SSM_GROUP * jp, lane)
        ba, bb = _shift_lanes_left(mb_a, mb_b, SSM_GROUP * (SCAN_CHUNK - 1 - jp), lane)
        diag = SSM_GROUP * jp + row_h
        tt_scr[rows, 0:128] = fa + ba + jnp.where(lane_h == diag, d_skip, 0.0)
        tt_scr[rows, 128:256] = fb + bb + jnp.where(lane_h + 128 == diag, d_skip, 0.0)
    ttt_ref[...] = tt_scr[...].T.astype(BF16)


def _ssm_operators(vecs, mats):
    op = jax.ShapeDtypeStruct((DEPTH, N_GROUPS, CHUNK_WIDTH, CHUNK_WIDTH), BF16)
    pg = PREP_GROUPS
    op_spec = pl.BlockSpec((None, pg, CHUNK_WIDTH, CHUNK_WIDTH), lambda l, g: (l, g, 0, 0))
    sq = pltpu.VMEM((pg, CHUNK_WIDTH, CHUNK_WIDTH), F32)
    return pl.pallas_call(
        _prep_kernel,
        grid=(DEPTH, N_GROUPS // pg),
        in_specs=[
            pl.BlockSpec((None, pg, 8, 128), lambda l, g: (l, g, 0, 0)),
            pl.BlockSpec((None, pg, 4, SSM_GROUP, 128), lambda l, g: (l, g, 0, 0, 0)),
        ],
        out_specs=[op_spec, op_spec, op_spec,
                   pl.BlockSpec((None, pg, 2, N_PS, 128), lambda l, g: (l, g, 0, 0, 0))],
        out_shape=[op, op, op, jax.ShapeDtypeStruct((DEPTH, N_GROUPS, 2, N_PS, 128), F32)],
        scratch_shapes=[sq, sq, sq],
        compiler_params=_params("arbitrary", "arbitrary"),
        name="s5_operators",
    )(vecs, mats)


def _ssm_kernel(xt_ref, wft_ref, cct_ref, ttt_ref, a16_ref, s0_ref, yt_ref, fin_ref, s_scr, f_scr, ft_scr):
    tile = pl.program_id(0)
    lane = lax.broadcasted_iota(jnp.int32, (GROUP_BLOCK, N_PS, 128), 2)
    is_fwd = lane < SSM_STATE
    half = SSM_STATE

    def group_x(gl):
        return xt_ref[:, gl * SSM_GROUP:(gl + 1) * SSM_GROUP, :].reshape(CHUNK_WIDTH, J_ROWS)

    for gl in range(GROUP_BLOCK):
        ft_scr[gl] = jnp.dot(wft_ref[gl], group_x(gl), preferred_element_type=F32)
        f_scr[gl] = ft_scr[gl].T

    a_r = a16_ref[:, 0]
    a_i = a16_ref[:, 1]

    def scan(s_r, s_i):
        for i in range(SEG_CHUNKS):
            rf = slice(i * N_PS, (i + 1) * N_PS)
            rb = slice((SEG_CHUNKS - 1 - i) * N_PS, (SEG_CHUNKS - i) * N_PS)
            s_scr[:, rf, 0:half] = s_r[:, :, 0:half]
            s_scr[:, rf, 128:128 + half] = s_i[:, :, 0:half]
            s_scr[:, rb, half:128] = s_r[:, :, half:128]
            s_scr[:, rb, 128 + half:256] = s_i[:, :, half:128]
            f_r = jnp.where(is_fwd, f_scr[:, rf, 0:128], f_scr[:, rb, 0:128])
            f_i = jnp.where(is_fwd, f_scr[:, rf, 128:256], f_scr[:, rb, 128:256])
            s_r, s_i = a_r * s_r - a_i * s_i + f_r, a_r * s_i + a_i * s_r + f_i
        return s_r, s_i

    zeros = jnp.zeros((GROUP_BLOCK, N_PS, 128), F32)
    z_r, z_i = scan(zeros, zeros)
    fin_ref[:, :, 0:128] = z_r
    fin_ref[:, :, 128:256] = z_i

    @pl.when(tile == SAMPLE_TILE)
    def _():
        b_r, b_i = a_r, a_i
        for _ in range(4):
            b_r, b_i = b_r * b_r - b_i * b_i, 2.0 * (b_r * b_i)
        seg = lax.broadcasted_iota(jnp.int32, (GROUP_BLOCK, N_PS, 128), 1) % SEGS_PER_SAMPLE_SEQ
        i_r = s0_ref[:, :, 0:128]
        i_i = s0_ref[:, :, 128:256]
        for step in range(1, SEGS_PER_SAMPLE_SEQ):
            pr = jnp.where(is_fwd, pltpu.roll(i_r, 1, 1), pltpu.roll(i_r, N_PS - 1, 1))
            pi = jnp.where(is_fwd, pltpu.roll(i_i, 1, 1), pltpu.roll(i_i, N_PS - 1, 1))
            zr = jnp.where(is_fwd, pltpu.roll(z_r, 1, 1), pltpu.roll(z_r, N_PS - 1, 1))
            zi = jnp.where(is_fwd, pltpu.roll(z_i, 1, 1), pltpu.roll(z_i, N_PS - 1, 1))
            n_r = b_r * pr - b_i * pi + zr
            n_i = b_r * pi + b_i * pr + zi
            first = jnp.where(is_fwd, step, 0)
            last = jnp.where(is_fwd, SEGS_PER_SAMPLE_SEQ - 1, SEGS_PER_SAMPLE_SEQ - 1 - step)
            upd = jnp.logical_and(seg >= first, seg <= last)
            i_r = jnp.where(upd, n_r, i_r)
            i_i = jnp.where(upd, n_i, i_i)
        scan(i_r, i_i)

    for gl in range(GROUP_BLOCK):
        yt = jnp.dot(ttt_ref[gl], group_x(gl), preferred_element_type=F32)
        yt += lax.dot_general(cct_ref[gl], s_scr[gl].astype(BF16), TRANS_B, preferred_element_type=F32)
        yt_ref[:, gl * SSM_GROUP:(gl + 1) * SSM_GROUP, :] = yt.reshape(SCAN_CHUNK, SSM_GROUP, J_ROWS)


def _ssm_scan(xt, wft, cct, ttt, a16, s0, l):
    gb = GROUP_BLOCK
    op_spec = pl.BlockSpec((None, gb, CHUNK_WIDTH, CHUNK_WIDTH), lambda t, g: (l, g, 0, 0))
    io_spec = pl.BlockSpec((None, SCAN_CHUNK, gb * SSM_GROUP, J_ROWS), lambda t, g: (t, 0, g, 0))
    return pl.pallas_call(
        _ssm_kernel,
        grid=(N_TILES, N_GROUPS // gb),
        in_specs=[
            io_spec, op_spec, op_spec, op_spec,
            pl.BlockSpec((None, gb, 2, N_PS, 128), lambda t, g: (l, g, 0, 0, 0)),
            pl.BlockSpec((gb, N_PS, CHUNK_WIDTH), lambda t, g: (g, 0, 0)),
        ],
        out_specs=[io_spec, pl.BlockSpec((None, gb, N_PS, CHUNK_WIDTH), lambda t, g: (t, g, 0, 0))],
        out_shape=[
            jax.ShapeDtypeStruct((N_TILES, SCAN_CHUNK, D_SSM, J_ROWS), F32),
            jax.ShapeDtypeStruct((N_TILES, N_GROUPS, N_PS, CHUNK_WIDTH), F32),
        ],
        scratch_shapes=[pltpu.VMEM((gb, J_ROWS, CHUNK_WIDTH), F32), pltpu.VMEM((gb, J_ROWS, CHUNK_WIDTH), F32),
                        pltpu.VMEM((gb, CHUNK_WIDTH, J_ROWS), F32)],
        compiler_params=_params("arbitrary", "arbitrary"),
        name="s5_chunk_scan",
    )(xt, wft, cct, ttt, a16, s0)


def _post_kernel(x_ref, yt_ref, ug_ref, vn_ref, mod_ref, wglu_ref, ws_ref, bs_ref, wout_f32_ref,
                 nw_ref, rwt_ref, rb_ref, hx_ref, route_ref, yg_scr, y_scr, wglut_ref, wout_ref):
    tile = pl.program_id(0)
    step = pl.program_id(1)

    @pl.when(_first_step())
    def _():
        wglut_ref[...] = wglu_ref[...].T.astype(BF16)
        wout_ref[...] = wout_f32_ref[...].astype(BF16)

    @pl.when(step == 0)
    def _():
        def chunk(n, carry):
            ps = n % N_PS
            c_hi = n // N_PS
            base = c_hi * (GMLP_CHUNK // SCAN_CHUNK) * N_PS + ps
            rows = [pl.ds(j * J_ROWS + base, GMLP_CHUNK // SCAN_CHUNK, stride=N_PS) for j in range(SCAN_CHUNK)]
            for h in range(GMLP_HEADS):
                v = jnp.concatenate([vn_ref[h, r, :] for r in rows], axis=0).astype(BF16)
                u = jnp.concatenate([ug_ref[h, r, :] for r in rows], axis=0)
                s = jnp.dot(ws_ref[h], v, preferred_element_type=F32) + bs_ref[h]
                yg = u * s
                for j, r in enumerate(rows):
                    yg_scr[h, r, :] = yg[j * 8:(j + 1) * 8]
            return carry

        lax.fori_loop(0, TILE_TOKENS // GMLP_CHUNK, chunk, 0)

    yt = _gelu_tanh(jnp.concatenate([yt_ref[k] for k in range(J_PER_STEP)], axis=1))
    yt = yt * _sigmoid(jnp.dot(wglut_ref[...], yt.astype(BF16), preferred_element_type=F32))
    for k in range(J_PER_STEP):
        y_scr[k * J_ROWS:(k + 1) * J_ROWS, :] = yt[:, k * J_ROWS:(k + 1) * J_ROWS].T
    row0 = pl.multiple_of(step * TOKEN_TILE, TOKEN_TILE)
    proj = jnp.dot(y_scr[...].astype(BF16), wout_ref[0:D_SSM, :], preferred_element_type=F32)
    yg = jnp.concatenate([yg_scr[h, pl.ds(row0, TOKEN_TILE), :] for h in range(GMLP_HEADS)], axis=1)
    proj += jnp.dot(yg.astype(BF16), wout_ref[D_SSM:, :], preferred_element_type=F32)
    x1 = x_ref[...] + _per_ps(lambda a, g: a * g, proj, mod_ref[2])
    h2 = _per_ps(lambda a, sc, sh: a * (1.0 + sc) + sh, _rmsnorm(x1, nw_ref[...]), mod_ref[4], mod_ref[3])
    hx_ref[:, 0:D_MODEL] = h2
    hx_ref[:, D_MODEL:2 * D_MODEL] = x1
    r_hi, r_lo = _split_bf16(rwt_ref[...])
    h_hi, h_lo = _split_bf16(h2)
    dot_t = lambda a, b: lax.dot_general(a, b, TRANS_B, preferred_element_type=F32)
    logits = dot_t(r_hi, h_hi) + (dot_t(r_hi, h_lo) + dot_t(r_lo, h_hi))
    scores = _sigmoid(logits)
    sel = scores + rb_ref[...]
    sc = [scores[e:e + 1, :] for e in range(N_EXPERTS)]
    sl = [sel[e:e + 1, :] for e in range(N_EXPERTS)]
    gscore = []
    for g in range(N_EXPERT_GROUPS):
        v0, v1, v2, v3 = sl[4 * g:4 * g + 4]
        hi01, lo01 = jnp.maximum(v0, v1), jnp.minimum(v0, v1)
        hi23, lo23 = jnp.maximum(v2, v3), jnp.minimum(v2, v3)
        top1 = jnp.maximum(hi01, hi23)
        top2 = jnp.maximum(jnp.minimum(hi01, hi23), jnp.maximum(lo01, lo23))
        gscore.append(top1 + top2)
    best = gscore[0]
    gidx = jnp.zeros_like(best, dtype=jnp.int32)
    for g in range(1, N_EXPERT_GROUPS):
        upd = gscore[g] > best
        gidx = jnp.where(upd, g, gidx)
        best = jnp.where(upd, gscore[g], best)

    def in_group(vals, k):
        out = vals[k]
        for g in range(1, N_EXPERT_GROUPS):
            out = jnp.where(gidx == g, vals[4 * g + k], out)
        return out

    v = [in_group(sl, k) for k in range(EXPERTS_PER_GROUP)]
    s = [in_group(sc, k) for k in range(EXPERTS_PER_GROUP)]
    w = []
    bits = jnp.zeros_like(gidx)
    for k in range(EXPERTS_PER_GROUP):
        rank = jnp.zeros_like(gidx)
        for j in range(EXPERTS_PER_GROUP):
            if j == k:
                continue
            ahead = (v[j] >= v[k]) if j < k else (v[j] > v[k])
            rank = rank + ahead.astype(jnp.int32)
        w.append(jnp.where(rank < 2, s[k], 0.0))
        bits = bits + jnp.where(rank < 2, 1 << k, 0)
    denom = (w[0] + w[1]) + (w[2] + w[3])
    gate = [wk / denom for wk in w]
    pair = jnp.full_like(gidx, len(PAIR_SLOT_A) - 1)
    for p in range(len(PAIR_SLOT_A) - 1):
        pair = jnp.where(bits == (1 << PAIR_SLOT_A[p]) + (1 << PAIR_SLOT_B[p]), p, pair)

    def slot_gate(table):
        out = gate[table[0]]
        for p in range(1, len(table)):
            out = jnp.where(pair == p, gate[table[p]], out)
        return out

    n_tok = route_ref.shape[1]
    route_ref[0:1, :] = (gidx * len(PAIR_SLOT_A) + pair).astype(F32)
    ps = lax.broadcasted_iota(jnp.int32, (1, n_tok), 1) % N_PS
    cond = jnp.where(tile < SAMPLE_TILE, 0, 1 + ps // SEGS_PER_SAMPLE_SEQ).astype(F32)
    lanes = jnp.concatenate([slot_gate(PAIR_SLOT_A), slot_gate(PAIR_SLOT_B), cond,
                             jnp.zeros((ROUTE_LANES - 3, n_tok), F32)], axis=0)
    hx_ref[:, 2 * D_MODEL:] = lanes.T


def _mix_out(x, yt, ug, vn, mod8, l, w_glu, w_s, b_s, w_out, norm2_w, rwt, rb):
    tm = TOKEN_TILE
    steps = TILE_TOKENS // tm
    tok = lambda n: pl.BlockSpec((tm, n), lambda t, s: (t * steps + s, 0))
    whole_tile = pl.BlockSpec((GMLP_HEADS, TILE_TOKENS, GMLP_HEAD_DIM), lambda t, s: (0, t, 0))
    lay = lambda *shape: pl.BlockSpec((None,) + shape, lambda t, s: (l,) + (0,) * len(shape))
    return pl.pallas_call(
        _post_kernel,
        grid=(N_TILES, steps),
        in_specs=[
            tok(D_MODEL),
            pl.BlockSpec((None, J_PER_STEP, D_SSM, J_ROWS), lambda t, s: (t, s, 0, 0)),
            whole_tile, whole_tile,
            pl.BlockSpec((None, None, N_MOD, N_PS, D_MODEL), lambda t, s: (l, t, 0, 0, 0)),
            lay(D_SSM, D_SSM), lay(GMLP_HEADS, GMLP_CHUNK, GMLP_CHUNK),
            lay(GMLP_HEADS, GMLP_CHUNK, 1), lay(D_MODEL, D_MODEL), lay(1, D_MODEL),
            pl.BlockSpec((N_EXPERTS, D_MODEL), lambda t, s: (0, 0)),
            pl.BlockSpec((N_EXPERTS, 1), lambda t, s: (0, 0)),
        ],
        out_specs=[tok(HX_WIDTH), pl.BlockSpec((1, tm), lambda t, s: (0, t * steps + s))],
        out_shape=[
            jax.ShapeDtypeStruct((T_ALL, HX_WIDTH), F32),
            jax.ShapeDtypeStruct((1, T_ALL), F32),
        ],
        scratch_shapes=[pltpu.VMEM((GMLP_HEADS, TILE_TOKENS, GMLP_HEAD_DIM), F32), pltpu.VMEM((tm, D_SSM), F32),
                        pltpu.VMEM((D_SSM, D_SSM), BF16), pltpu.VMEM((D_MODEL, D_MODEL), BF16)],
        compiler_params=_params("arbitrary", "arbitrary"),
        name="mixers_out_router",
    )(x, yt, ug, vn, mod8, w_glu, w_s, b_s, w_out, norm2_w.reshape(DEPTH, 1, D_MODEL), rwt, rb)


def _moe_kernel(ea_ref, eb_ref, new_a_ref, new_b_ref, nv_ref, size_ref, src_ref, drow_ref,
                hx_hbm, wga_ref, wua_ref, wda_ref, wgb_ref, wub_ref, wdb_ref, g2_ref, fw_ref,
                o_hbm, wga_s, wua_s, wda_s, wgb_s, wub_s, wdb_s, hx_buf, o_buf, z_buf, g_sem, s_sem, z_sem,
                *, final):
    n = pl.program_id(0)
    n_valid = nv_ref[0]
    slot = n % 2

    def by_size(tile, fn):
        for quarters in range(1, TAIL_STEPS + 1):
            @pl.when(size_ref[tile] == quarters)
            def _(rows=quarters * TAIL_QUARTER):
                fn(rows)

    def start_gather(tile, sl, rows):
        base = tile * MOE_TM
        for r in range(rows):
            pltpu.make_async_copy(hx_hbm.at[pl.ds(src_ref[base + r], 1)], hx_buf.at[sl, pl.ds(r, 1)],
                                  g_sem.at[sl]).start()

    def wait_gather(sl, rows):
        pltpu.make_async_copy(hx_hbm.at[pl.ds(0, rows)], hx_buf.at[sl, pl.ds(0, rows)], g_sem.at[sl]).wait()

    def start_scatter(tile, sl, rows):
        base = tile * MOE_TM
        for r in range(rows):
            pltpu.make_async_copy(o_buf.at[sl, pl.ds(r, 1)], o_hbm.at[pl.ds(drow_ref[base + r], 1)],
                                  s_sem.at[sl]).start()

    def wait_scatter(sl, rows):
        pltpu.make_async_copy(o_buf.at[sl, pl.ds(0, rows)], o_hbm.at[pl.ds(0, rows)], s_sem.at[sl]).wait()

    @pl.when(n == 0)
    def _():
        z_buf[...] = jnp.zeros_like(z_buf)
        spare = [pltpu.make_async_copy(z_buf, o_hbm.at[pl.ds(T_ALL + i * MOE_TM, MOE_TM)], z_sem)
                 for i in range(SPARE_ROWS // MOE_TM)]
        for cp in spare:
            cp.start()
        for cp in spare:
            cp.wait()
        by_size(0, lambda rows: start_gather(0, 0, rows))

    @pl.when(new_a_ref[n] == 1)
    def _():
        wga_s[...] = wga_ref[...].astype(BF16)
        wua_s[...] = wua_ref[...].astype(BF16)
        wda_s[...] = wda_ref[...].astype(BF16)

    @pl.when(new_b_ref[n] == 1)
    def _():
        wgb_s[...] = wgb_ref[...].astype(BF16)
        wub_s[...] = wub_ref[...].astype(BF16)
        wdb_s[...] = wdb_ref[...].astype(BF16)

    @pl.when(n + 1 < n_valid)
    def _():
        by_size(n + 1, lambda rows: start_gather(n + 1, 1 - slot, rows))

    def tile_step(rows):
        wait_gather(slot, rows)

        @pl.when(n >= 2)
        def _():
            by_size(n - 2, lambda r: wait_scatter(slot, r))

        h = hx_buf[slot, 0:rows, 0:D_MODEL].astype(BF16)
        lanes = hx_buf[slot, 0:rows, 2 * D_MODEL:]

        def expert(wg, wu, wd, gate):
            hg = jnp.dot(h, wg[...], preferred_element_type=F32)
            hu = jnp.dot(h, wu[...], preferred_element_type=F32)
            act = hg * _sigmoid(hg) * hu * gate
            return jnp.dot(act.astype(BF16), wd[...], preferred_element_type=F32)

        y = expert(wga_s, wua_s, wda_s, lanes[:, 0:1]) + expert(wgb_s, wub_s, wdb_s, lanes[:, 1:2])
        cond_row = lanes[:, 2:3]
        gate2 = jnp.where(cond_row == 0.0, g2_ref[0:1, :], jnp.where(cond_row == 1.0, g2_ref[1:2, :], g2_ref[2:3, :]))
        x2 = hx_buf[slot, 0:rows, D_MODEL:2 * D_MODEL] + gate2 * y
        o_buf[slot, 0:rows] = _rmsnorm(x2, fw_ref[...]) if final else x2
        start_scatter(n, slot, rows)

    @pl.when(n < n_valid)
    def _():
        by_size(n, tile_step)

    @pl.when(n == MOE_TILES - 1)
    def _():
        last = n_valid - 1
        by_size(last, lambda r: wait_scatter(last % 2, r))
        by_size(last - 1, lambda r: wait_scatter(1 - last % 2, r))


def _experts(tables, hx, mods, l, w_gate, w_up, w_down, final_w, final):
    w_a = lambda r, c: pl.BlockSpec((None, None, r, c), lambda n, ea, eb, *_: (l, ea[n], 0, 0))
    w_b = lambda r, c: pl.BlockSpec((None, None, r, c), lambda n, ea, eb, *_: (l, eb[n], 0, 0))
    up = pltpu.VMEM((D_MODEL, D_EXPERT), BF16)
    down = pltpu.VMEM((D_EXPERT, D_MODEL), BF16)
    return pl.pallas_call(
        functools.partial(_moe_kernel, final=final),
        grid_spec=pltpu.PrefetchScalarGridSpec(
            num_scalar_prefetch=len(tables),
            grid=(MOE_TILES,),
            in_specs=[
                pl.BlockSpec(memory_space=pl.ANY),
                w_a(D_MODEL, D_EXPERT), w_a(D_MODEL, D_EXPERT), w_a(D_EXPERT, D_MODEL),
                w_b(D_MODEL, D_EXPERT), w_b(D_MODEL, D_EXPERT), w_b(D_EXPERT, D_MODEL),
                pl.BlockSpec((None, MOD_ROWS, D_MODEL), lambda n, *_: (l, 0, N_MOD - 1)),
                pl.BlockSpec((1, D_MODEL), lambda n, *_: (0, 0)),
            ],
            out_specs=pl.BlockSpec(memory_space=pl.ANY),
            scratch_shapes=[up, up, down, up, up, down,
                            pltpu.VMEM((2, MOE_TM, HX_WIDTH), F32), pltpu.VMEM((2, MOE_TM, D_MODEL), F32),
                            pltpu.VMEM((MOE_TM, D_MODEL), F32),
                            pltpu.SemaphoreType.DMA((2,)), pltpu.SemaphoreType.DMA((2,)), pltpu.SemaphoreType.DMA(())],
        ),
        out_shape=jax.ShapeDtypeStruct((OUT_ROWS, D_MODEL), F32),
        compiler_params=_params("arbitrary"),
        name="experts",
    )(*tables, hx, w_gate, w_up, w_down, w_gate, w_up, w_down, mods, final_w.reshape(1, D_MODEL))


def _count_before(flags):
    n, k = flags.shape
    blocks = flags.reshape(n // 128, 128, k).astype(F32)
    strictly_lower = jnp.tril(jnp.ones((128, 128), F32), -1)
    within = jnp.einsum("ij,bjk->bik", strictly_lower, blocks)
    totals = jnp.sum(blocks, axis=1)
    before = jnp.cumsum(totals, axis=0) - totals
    return (within + before[:, None, :]).reshape(n, k).astype(jnp.int32)


def _routing_tables(cls, to_sequence_order):
    n_cls = N_EXPERT_GROUPS * len(PAIR_SLOT_A)
    onehot = (cls[:, None] == jnp.arange(n_cls, dtype=jnp.int32)[None, :]).astype(jnp.int32)
    counts = jnp.sum(onehot, axis=0)
    tiles = (counts + MOE_TM - 1) // MOE_TM
    tile_end = jnp.cumsum(tiles)
    row_in_class = (tile_end - tiles)[None, :] * MOE_TM + _count_before(onehot)
    dst = jnp.sum(onehot * row_in_class, axis=1)
    token_plus_1 = jnp.zeros((MOE_ROWS,), jnp.int32).at[dst].set(jnp.arange(1, T_ALL + 1, dtype=jnp.int32))
    is_pad = token_plus_1 == 0
    src = jnp.maximum(token_plus_1 - 1, 0)
    n_valid = tile_end[-1]
    tile_id = jnp.minimum(jnp.arange(MOE_TILES, dtype=jnp.int32), n_valid - 1)
    tile_cls = jnp.sum((tile_end[None, :] <= tile_id[:, None]).astype(jnp.int32), axis=1)
    group, pair = tile_cls // len(PAIR_SLOT_A), tile_cls % len(PAIR_SLOT_A)
    in_cls = (tile_cls[:, None] == jnp.arange(n_cls, dtype=jnp.int32)[None, :]).astype(jnp.int32)
    left = jnp.sum(in_cls * (counts[None, :] - (tile_id[:, None] - (tile_end - tiles)[None, :]) * MOE_TM), axis=1)
    size = jnp.clip((left + TAIL_QUARTER - 1) // TAIL_QUARTER, 1, TAIL_STEPS).astype(jnp.int32)
    row = jnp.arange(MOE_ROWS, dtype=jnp.int32)
    processed = jnp.logical_and(row % MOE_TM < jnp.repeat(size, MOE_TM) * TAIL_QUARTER, row // MOE_TM < n_valid)
    moved_pad = jnp.logical_and(is_pad, processed)
    spare = T_ALL + _count_before(moved_pad.astype(jnp.int32)[:, None])[:, 0]
    if to_sequence_order:
        j, c, ps = (src // J_ROWS) % SCAN_CHUNK, (src // N_PS) % SEG_CHUNKS, src % N_PS
        target = (src // TILE_TOKENS) * TILE_TOKENS + ps * SEG_LEN + c * SCAN_CHUNK + j
    else:
        target = src
    drow = jnp.where(is_pad, jnp.where(moved_pad, spare, T_ALL), target).astype(jnp.int32)

    def slot_expert(table):
        local = jnp.full_like(pair, table[-1])
        for p in range(len(table) - 1):
            local = jnp.where(pair == p, table[p], local)
        return group * EXPERTS_PER_GROUP + local

    e_a = slot_expert(PAIR_SLOT_A)
    e_b = slot_expert(PAIR_SLOT_B)
    first = jnp.ones((1,), jnp.int32)
    new_a = jnp.concatenate([first, (e_a[1:] != e_a[:-1]).astype(jnp.int32)])
    new_b = jnp.concatenate([first, (e_b[1:] != e_b[:-1]).astype(jnp.int32)])
    return e_a, e_b, new_a, new_b, n_valid.reshape(1).astype(jnp.int32), size, src, drow


def _dirs_on_lanes(p):
    p = jnp.moveaxis(p, 1, -2)
    return p.reshape(p.shape[:-2] + (2 * SSM_STATE,))


def _to_internal_order(x):
    x = x.reshape(N_TILES, N_PS, SEG_CHUNKS, SCAN_CHUNK, D_MODEL)
    return x.transpose(0, 3, 2, 1, 4).reshape(T_ALL, D_MODEL)


def _gmlp_position_order(w):
    n_lo = GMLP_CHUNK // SCAN_CHUNK
    lead = w.shape[:2]
    w = w.reshape(lead + (n_lo, SCAN_CHUNK) + w.shape[3:])
    w = jnp.swapaxes(w, 2, 3)
    return w.reshape(lead + (GMLP_CHUNK,) + w.shape[4:])


def kernel(x_prompt, x_sample, c, state_ssm_re, state_ssm_im, c_ctx, norm1_w, norm2_w, w_mod, b_mod, w_in,
           ssm_a_re, ssm_a_im, ssm_log_dt, ssm_b_re, ssm_b_im, ssm_c_re, ssm_c_im, ssm_d, w_glu,
           gmlp_ln_w, gmlp_ln_b, gmlp_w_s, gmlp_b_s, w_out, router_w, router_b, w_gate, w_up, w_down,
           final_norm_w):
    x = jnp.concatenate([x_prompt.reshape(T_PROMPT, D_MODEL), x_sample.reshape(T_SAMPLE, D_MODEL)], axis=0)
    x = _to_internal_order(x)

    cvec = jnp.concatenate([c_ctx[None, :], c, jnp.zeros((MOD_ROWS - 1 - N_SAMPLE_SEQ, D_MODEL), F32)], axis=0)
    mods = _modulation(cvec, w_mod, b_mod)
    ps_row = [[0] * N_PS] * SAMPLE_TILE + [[1 + p // SEGS_PER_SAMPLE_SEQ for p in range(N_PS)]]
    mod8 = mods.reshape(DEPTH, MOD_ROWS, N_MOD, D_MODEL)[:, jnp.array(ps_row, jnp.int32)]
    mod8 = mod8.transpose(0, 1, 3, 2, 4)

    log_dt = jnp.broadcast_to(ssm_log_dt[..., None], ssm_a_re.shape)
    d_lanes = jnp.tile(ssm_d.reshape(DEPTH, N_GROUPS, SSM_GROUP), (1, 1, 128 // SSM_GROUP))
    vecs = jnp.stack([_dirs_on_lanes(ssm_a_re), _dirs_on_lanes(ssm_a_im), _dirs_on_lanes(log_dt), d_lanes], axis=2)
    vecs = jnp.concatenate([vecs, jnp.zeros((DEPTH, N_GROUPS, 4, 2 * SSM_STATE), F32)], axis=2)
    mats = jnp.stack([_dirs_on_lanes(jnp.swapaxes(ssm_b_re, -1, -2)), _dirs_on_lanes(jnp.swapaxes(ssm_b_im, -1, -2)),
                      _dirs_on_lanes(ssm_c_re), _dirs_on_lanes(ssm_c_im)], axis=2)
    wft, cct, ttt, a16 = _ssm_operators(vecs, mats)

    w_s = jnp.swapaxes(_gmlp_position_order(jnp.swapaxes(_gmlp_position_order(gmlp_w_s), 2, 3)), 2, 3).astype(BF16)
    b_s = _gmlp_position_order(gmlp_b_s)[..., None]
    rwt = router_w.T
    rb = router_b.reshape(N_EXPERTS, 1)

    new_re, new_im = [], []
    for l in range(DEPTH):
        ug, vn, xt = _input_proj(x, mod8, l, norm1_w, w_in, gmlp_ln_w, gmlp_ln_b)
        s0 = jnp.concatenate([state_ssm_re[:, l].transpose(2, 0, 1, 3).reshape(N_GROUPS, N_SAMPLE_SEQ, 128),
                              state_ssm_im[:, l].transpose(2, 0, 1, 3).reshape(N_GROUPS, N_SAMPLE_SEQ, 128)], axis=-1)
        s0 = jnp.repeat(s0, SEGS_PER_SAMPLE_SEQ, axis=1)
        yt, fin = _ssm_scan(xt, wft, cct, ttt, a16, s0, l)
        fin = fin[:SAMPLE_TILE].reshape(SAMPLE_TILE, N_GROUPS, N_PS, 2, 2, SSM_STATE)
        fin = fin.transpose(3, 0, 2, 4, 1, 5).reshape(2, N_PROMPT_SEQ, 2, N_GROUPS, SSM_STATE)
        new_re.append(fin[0])
        new_im.append(fin[1])
        hx, route = _mix_out(x, yt, ug, vn, mod8, l, w_glu, w_s, b_s, w_out, norm2_w, rwt, rb)
        final = l == DEPTH - 1
        tables = _routing_tables(route[0].astype(jnp.int32), to_sequence_order=final)
        x = _experts(tables, hx, mods, l, w_gate, w_up, w_down, final_norm_w, final)

    y_prompt = x[:T_PROMPT].reshape(N_PROMPT_SEQ, PROMPT_LEN, D_MODEL)
    y_sample = x[T_PROMPT:T_ALL].reshape(N_SAMPLE_SEQ, SAMPLE_LEN, D_MODEL)
    return (y_prompt, y_sample, jnp.stack(new_re, axis=1), jnp.stack(new_im, axis=1))
```

```python
import functools
import math

import jax
import jax.numpy as jnp
from jax import lax
from jax.experimental import pallas as pl
from jax.experimental.pallas import tpu as pltpu

F32 = jnp.float32
BF16 = jnp.bfloat16

D_MODEL = 1024
N_PROMPT_SEQ = 16
PROMPT_LEN = 256
N_SAMPLE_SEQ = 2
SAMPLE_LEN = 1024
T_PROMPT = N_PROMPT_SEQ * PROMPT_LEN
T_SAMPLE = N_SAMPLE_SEQ * SAMPLE_LEN
T_ALL = T_PROMPT + T_SAMPLE
DEPTH = 2
D_SSM = 512
SSM_GROUP = 16
N_GROUPS = 32
SSM_STATE = 64
D_GMLP = 512
GMLP_HEADS = 4
GMLP_HEAD_DIM = 128
GMLP_CHUNK = 128
N_EXPERTS = 16
N_EXPERT_GROUPS = 4
EXPERTS_PER_GROUP = 4
D_EXPERT = 512
N_MOD = 6
EPS = 1e-6

SCAN_CHUNK = 16
CHUNK_WIDTH = SCAN_CHUNK * SSM_GROUP
SEG_LEN = 256
SEG_CHUNKS = SEG_LEN // SCAN_CHUNK
N_PS = 8
TILE_TOKENS = N_PS * SEG_LEN
N_TILES = T_ALL // TILE_TOKENS
SAMPLE_TILE = T_PROMPT // TILE_TOKENS
SEGS_PER_SAMPLE_SEQ = SAMPLE_LEN // SEG_LEN
J_ROWS = SEG_CHUNKS * N_PS
GROUP_BLOCK = 8
PREP_GROUPS = 4
MOD_ROWS = 8
MOD_K_STEPS = 4

J_PER_STEP = 4
TOKEN_TILE = J_PER_STEP * J_ROWS
PAIR_SLOT_A = (0, 0, 0, 1, 1, 3)
PAIR_SLOT_B = (1, 2, 3, 3, 2, 2)
MOE_TM = 256
TAIL_STEPS = 4
TAIL_QUARTER = MOE_TM // TAIL_STEPS
N_CLASSES = N_EXPERT_GROUPS * len(PAIR_SLOT_A)
MOE_TILES = T_ALL // MOE_TM + N_CLASSES
MOE_ROWS = MOE_TILES * MOE_TM
SPARE_ROWS = N_CLASSES * TAIL_QUARTER
OUT_ROWS = T_ALL + SPARE_ROWS
ROUTE_LANES = 128
HX_WIDTH = 2 * D_MODEL + ROUTE_LANES
VMEM_LIMIT = 56 * 1024 * 1024
TRANS_B = (((1,), (1,)), ((), ()))


def _sigmoid(x):
    return 1.0 / (1.0 + jnp.exp(-x))


def _gelu_tanh(x):
    c = math.sqrt(2.0 / math.pi)
    return x * (0.5 * (1.0 + jnp.tanh(c * (x + 0.044715 * (x * x * x)))))


def _split_bf16(a):
    hi = a.astype(BF16)
    return hi, (a - hi.astype(F32)).astype(BF16)


def _rmsnorm(x, w):
    return x * lax.rsqrt(jnp.mean(x * x, axis=-1, keepdims=True) + EPS) * w


def _per_ps(fn, a, *mods):
    rows, d = a.shape
    out = fn(a.reshape(rows // N_PS, N_PS, d), *[m[None] for m in mods])
    return out.reshape(rows, d)


def _params(*sem):
    return pltpu.CompilerParams(dimension_semantics=sem, vmem_limit_bytes=VMEM_LIMIT)


def _mod_kernel(c_ref, w_ref, b_ref, o_ref):
    c = c_ref[...]
    part = jnp.dot((c * _sigmoid(c)).astype(BF16), w_ref[...].astype(BF16), preferred_element_type=F32)

    @pl.when(pl.program_id(1) == 0)
    def _():
        o_ref[...] = part + b_ref[...]

    @pl.when(pl.program_id(1) > 0)
    def _():
        o_ref[...] += part


def _modulation(cvec, w_mod, b_mod):
    kb = D_MODEL // MOD_K_STEPS
    c_blocks = cvec.reshape(MOD_ROWS, MOD_K_STEPS, kb).transpose(1, 0, 2)
    return pl.pallas_call(
        _mod_kernel,
        grid=(DEPTH, MOD_K_STEPS),
        in_specs=[
            pl.BlockSpec((None, MOD_ROWS, kb), lambda l, k: (k, 0, 0)),
            pl.BlockSpec((None, kb, N_MOD * D_MODEL), lambda l, k: (l, k, 0)),
            pl.BlockSpec((None, 1, N_MOD * D_MODEL), lambda l, k: (l, 0, 0)),
        ],
        out_specs=pl.BlockSpec((None, MOD_ROWS, N_MOD * D_MODEL), lambda l, k: (l, 0, 0)),
        out_shape=jax.ShapeDtypeStruct((DEPTH, MOD_ROWS, N_MOD * D_MODEL), F32),
        compiler_params=_params("arbitrary", "arbitrary"),
        name="adaln_mod",
    )(c_blocks, w_mod, b_mod.reshape(DEPTH, 1, N_MOD * D_MODEL))


def _first_step():
    return jnp.logical_and(pl.program_id(0) == 0, pl.program_id(1) == 0)


def _in_kernel(x_ref, mod_ref, nw_ref, w_ref, lnw_ref, lnb_ref, ug_ref, vn_ref, xt_ref, wg_ref, wst_ref):
    @pl.when(_first_step())
    def _():
        wg_ref[...] = w_ref[:, D_SSM:].astype(BF16)
        wst_ref[...] = w_ref[:, :D_SSM].T.astype(BF16)

    y = _rmsnorm(x_ref[...], nw_ref[...])
    h = _per_ps(lambda a, sc, sh: a * (1.0 + sc) + sh, y, mod_ref[1], mod_ref[0]).astype(BF16)
    zg = _gelu_tanh(jnp.dot(h, wg_ref[...], preferred_element_type=F32))
    v = zg[:, D_GMLP:]
    mu = jnp.mean(v, axis=-1, keepdims=True)
    vc = v - mu
    var = jnp.mean(vc * vc, axis=-1, keepdims=True)
    vn = vc * lax.rsqrt(var + EPS) * lnw_ref[...] + lnb_ref[...]
    for hd in range(GMLP_HEADS):
        cols = slice(hd * GMLP_HEAD_DIM, (hd + 1) * GMLP_HEAD_DIM)
        ug_ref[hd] = zg[:, cols]
        vn_ref[hd] = vn[:, cols]
    xt = lax.dot_general(wst_ref[...], h, TRANS_B, preferred_element_type=F32).astype(BF16)
    for k in range(J_PER_STEP):
        xt_ref[k] = xt[:, k * J_ROWS:(k + 1) * J_ROWS]


def _input_proj(x, mod8, l, norm1_w, w_in, ln_w, ln_b):
    tm = TOKEN_TILE
    steps = TILE_TOKENS // tm
    tok = lambda n: pl.BlockSpec((tm, n), lambda t, s: (t * steps + s, 0))
    heads = pl.BlockSpec((GMLP_HEADS, tm, GMLP_HEAD_DIM), lambda t, s: (0, t * steps + s, 0))
    lay = lambda *shape: pl.BlockSpec((None,) + shape, lambda t, s: (l,) + (0,) * len(shape))
    return pl.pallas_call(
        _in_kernel,
        grid=(N_TILES, steps),
        in_specs=[
            tok(D_MODEL),
            pl.BlockSpec((None, None, N_MOD, N_PS, D_MODEL), lambda t, s: (l, t, 0, 0, 0)),
            lay(1, D_MODEL), lay(D_MODEL, D_SSM + 2 * D_GMLP), lay(1, D_GMLP), lay(1, D_GMLP),
        ],
        out_specs=[heads, heads,
                   pl.BlockSpec((None, J_PER_STEP, D_SSM, J_ROWS), lambda t, s: (t, s, 0, 0))],
        out_shape=[jax.ShapeDtypeStruct((GMLP_HEADS, T_ALL, GMLP_HEAD_DIM), F32)] * 2 + [
            jax.ShapeDtypeStruct((N_TILES, SCAN_CHUNK, D_SSM, J_ROWS), BF16)],
        scratch_shapes=[pltpu.VMEM((D_MODEL, 2 * D_GMLP), BF16), pltpu.VMEM((D_SSM, D_MODEL), BF16)],
        compiler_params=_params("arbitrary", "arbitrary"),
        name="norm1_in_proj",
    )(x, mod8, norm1_w.reshape(DEPTH, 1, D_MODEL), w_in,
      ln_w.reshape(DEPTH, 1, D_GMLP), ln_b.reshape(DEPTH, 1, D_GMLP))


def _shift_lanes_right(a, b, s, lane):
    if s == 0:
        return a, b
    if s == 128:
        return jnp.zeros_like(a), a
    if s < 128:
        ra = pltpu.roll(a, s, 1)
        rb = pltpu.roll(b, s, 1)
        return jnp.where(lane >= s, ra, 0.0), jnp.where(lane >= s, rb, ra)
    t = s - 128
    return jnp.zeros_like(a), jnp.where(lane >= t, pltpu.roll(a, t, 1), 0.0)


def _shift_lanes_left(a, b, s, lane):
    if s == 0:
        return a, b
    if s == 128:
        return b, jnp.zeros_like(b)
    if s < 128:
        ra = pltpu.roll(a, 128 - s, 1)
        rb = pltpu.roll(b, 128 - s, 1)
        return jnp.where(lane < 128 - s, ra, rb), jnp.where(lane < 128 - s, rb, 0.0)
    t = s - 128
    return jnp.where(lane < 128 - t, pltpu.roll(b, 128 - t, 1), 0.0), jnp.zeros_like(b)


def _prep_kernel(*refs):
    for g in range(PREP_GROUPS):
        _prep_group(*[r.at[g] for r in refs])


def _prep_group(vec_ref, mat_ref, wft_ref, cct_ref, ttt_ref, a16_ref, wf_scr, cm_scr, tt_scr):
    a_re = vec_ref[0:1, :]
    a_im = vec_ref[1:2, :]
    dt = jnp.exp(vec_ref[2:3, :])
    d_skip = vec_ref[3:4, :]
    mag = jnp.exp(a_re * dt)
    ang = a_im * dt
    ab_r = mag * jnp.cos(ang)
    ab_i = mag * jnp.sin(ang)
    den = a_re * a_re + a_im * a_im
    nr = ab_r - 1.0
    q_r = (nr * a_re + ab_i * a_im) / den
    q_i = (ab_i * a_re - nr * a_im) / den
    bt_r = mat_ref[0]
    bt_i = mat_ref[1]
    c_r = mat_ref[2]
    c_i = mat_ref[3]
    bb_r = q_r * bt_r - q_i * bt_i
    bb_i = q_r * bt_i + q_i * bt_r
    p_r = [jnp.ones_like(ab_r)]
    p_i = [jnp.zeros_like(ab_r)]
    for _ in range(SCAN_CHUNK):
        pr, pi = p_r[-1], p_i[-1]
        p_r.append(pr * ab_r - pi * ab_i)
        p_i.append(pr * ab_i + pi * ab_r)
    a16_ref[0] = jnp.broadcast_to(p_r[SCAN_CHUNK], (N_PS, 128))
    a16_ref[1] = jnp.broadcast_to(p_i[SCAN_CHUNK], (N_PS, 128))

    lane = lax.broadcasted_iota(jnp.int32, (1, 128), 1)
    is_fwd = lane < SSM_STATE

    def pick(mf, mb):
        return jnp.where(is_fwd, p_r[mf], p_r[mb]), jnp.where(is_fwd, p_i[mf], p_i[mb])

    for j in range(SCAN_CHUNK):
        rows = slice(j * SSM_GROUP, (j + 1) * SSM_GROUP)
        wr, wi = pick(SCAN_CHUNK - 1 - j, j)
        wf_scr[rows, 0:128] = bb_r * wr - bb_i * wi
        wf_scr[rows, 128:256] = bb_r * wi + bb_i * wr
        wr, wi = pick(j + 1, SCAN_CHUNK - j)
        cct_ref[rows, 0:128] = (c_r * wr - c_i * wi).astype(BF16)
        cct_ref[rows, 128:256] = (-(c_r * wi + c_i * wr)).astype(BF16)
        wr, wi = pick(j, SCAN_CHUNK - 1 - j)
        cm_scr[rows, 0:128] = c_r * wr - c_i * wi
        cm_scr[rows, 128:256] = c_r * wi + c_i * wr
    wft_ref[...] = wf_scr[...].T.astype(BF16)

    zero = jnp.zeros_like(bb_r)
    cm_hi, cm_lo = _split_bf16(cm_scr[...])
    dot_t = lambda a, b: lax.dot_general(a, b, TRANS_B, preferred_element_type=F32)

    def lag_rows(keep):
        lhs = jnp.concatenate([jnp.where(keep, bb_r, zero), jnp.where(keep, -bb_i, zero)], axis=1)
        hi, lo = _split_bf16(lhs)
        return dot_t(hi, cm_hi) + (dot_t(hi, cm_lo) + dot_t(lo, cm_hi))

    mf = lag_rows(is_fwd)
    mb = lag_rows(jnp.logical_not(is_fwd))
    mf_a, mf_b = mf[:, 0:128], mf[:, 128:256]
    mb_a, mb_b = mb[:, 0:128], mb[:, 128:256]
    row_h = lax.broadcasted_iota(jnp.int32, (SSM_GROUP, 128), 0)
    lane_h = lax.broadcasted_iota(jnp.int32, (SSM_GROUP, 128), 1)
    for jp in range(SCAN_CHUNK):
        rows = slice(jp * SSM_GROUP, (jp + 1) * SSM_GROUP)
        fa, fb = _shift_lanes_right(mf_a, mf_b, SSM_GROUP * jp, lane)
        ba, bb = _shift_lanes_left(mb_a, mb_b, SSM_GROUP * (SCAN_CHUNK - 1 - jp), lane)
        diag = SSM_GROUP * jp + row_h
        tt_scr[rows, 0:128] = fa + ba + jnp.where(lane_h == diag, d_skip, 0.0)
        tt_scr[rows, 128:256] = fb + bb + jnp.where(lane_h + 128 == diag, d_skip, 0.0)
    ttt_ref[...] = tt_scr[...].T.astype(BF16)


def _ssm_operators(vecs, mats):
    op = jax.ShapeDtypeStruct((DEPTH, N_GROUPS, CHUNK_WIDTH, CHUNK_WIDTH), BF16)
    pg = PREP_GROUPS
    op_spec = pl.BlockSpec((None, pg, CHUNK_WIDTH, CHUNK_WIDTH), lambda l, g: (l, g, 0, 0))
    sq = pltpu.VMEM((pg, CHUNK_WIDTH, CHUNK_WIDTH), F32)
    return pl.pallas_call(
        _prep_kernel,
        grid=(DEPTH, N_GROUPS // pg),
        in_specs=[
            pl.BlockSpec((None, pg, 8, 128), lambda l, g: (l, g, 0, 0)),
            pl.BlockSpec((None, pg, 4, SSM_GROUP, 128), lambda l, g: (l, g, 0, 0, 0)),
        ],
        out_specs=[op_spec, op_spec, op_spec,
                   pl.BlockSpec((None, pg, 2, N_PS, 128), lambda l, g: (l, g, 0, 0, 0))],
        out_shape=[op, op, op, jax.ShapeDtypeStruct((DEPTH, N_GROUPS, 2, N_PS, 128), F32)],
        scratch_shapes=[sq, sq, sq],
        compiler_params=_params("arbitrary", "arbitrary"),
        name="s5_operators",
    )(vecs, mats)


def _ssm_kernel(xt_ref, wft_ref, cct_ref, ttt_ref, a16_ref, s0_ref, yt_ref, fin_ref, s_scr, f_scr, ft_scr):
    tile = pl.program_id(0)
    lane = lax.broadcasted_iota(jnp.int32, (GROUP_BLOCK, N_PS, 128), 2)
    is_fwd = lane < SSM_STATE
    half = SSM_STATE

    def group_x(gl):
        return xt_ref[:, gl * SSM_GROUP:(gl + 1) * SSM_GROUP, :].reshape(CHUNK_WIDTH, J_ROWS)

    for gl in range(GROUP_BLOCK):
        ft_scr[gl] = jnp.dot(wft_ref[gl], group_x(gl), preferred_element_type=F32)
        f_scr[gl] = ft_scr[gl].T

    a_r = a16_ref[:, 0]
    a_i = a16_ref[:, 1]

    def scan(s_r, s_i):
        for i in range(SEG_CHUNKS):
            rf = slice(i * N_PS, (i + 1) * N_PS)
            rb = slice((SEG_CHUNKS - 1 - i) * N_PS, (SEG_CHUNKS - i) * N_PS)
            s_scr[:, rf, 0:half] = s_r[:, :, 0:half]
            s_scr[:, rf, 128:128 + half] = s_i[:, :, 0:half]
            s_scr[:, rb, half:128] = s_r[:, :, half:128]
            s_scr[:, rb, 128 + half:256] = s_i[:, :, half:128]
            f_r = jnp.where(is_fwd, f_scr[:, rf, 0:128], f_scr[:, rb, 0:128])
            f_i = jnp.where(is_fwd, f_scr[:, rf, 128:256], f_scr[:, rb, 128:256])
            s_r, s_i = a_r * s_r - a_i * s_i + f_r, a_r * s_i + a_i * s_r + f_i
        return s_r, s_i

    zeros = jnp.zeros((GROUP_BLOCK, N_PS, 128), F32)
    z_r, z_i = scan(zeros, zeros)
    fin_ref[:, :, 0:128] = z_r
    fin_ref[:, :, 128:256] = z_i

    @pl.when(tile == SAMPLE_TILE)
    def _():
        b_r, b_i = a_r, a_i
        for _ in range(4):
            b_r, b_i = b_r * b_r - b_i * b_i, 2.0 * (b_r * b_i)
        seg = lax.broadcasted_iota(jnp.int32, (GROUP_BLOCK, N_PS, 128), 1) % SEGS_PER_SAMPLE_SEQ
        i_r = s0_ref[:, :, 0:128]
        i_i = s0_ref[:, :, 128:256]
        for step in range(1, SEGS_PER_SAMPLE_SEQ):
            pr = jnp.where(is_fwd, pltpu.roll(i_r, 1, 1), pltpu.roll(i_r, N_PS - 1, 1))
            pi = jnp.where(is_fwd, pltpu.roll(i_i, 1, 1), pltpu.roll(i_i, N_PS - 1, 1))
            zr = jnp.where(is_fwd, pltpu.roll(z_r, 1, 1), pltpu.roll(z_r, N_PS - 1, 1))
            zi = jnp.where(is_fwd, pltpu.roll(z_i, 1, 1), pltpu.roll(z_i, N_PS - 1, 1))
            n_r = b_r * pr - b_i * pi + zr
            n_i = b_r * pi + b_i * pr + zi
            first = jnp.where(is_fwd, step, 0)
            last = jnp.where(is_fwd, SEGS_PER_SAMPLE_SEQ - 1, SEGS_PER_SAMPLE_SEQ - 1 - step)
            upd = jnp.logical_and(seg >= first, seg <= last)
            i_r = jnp.where(upd, n_r, i_r)
            i_i = jnp.where(upd, n_i, i_i)
        scan(i_r, i_i)

    for gl in range(GROUP_BLOCK):
        yt = jnp.dot(ttt_ref[gl], group_x(gl), preferred_element_type=F32)
        yt += lax.dot_general(cct_ref[gl], s_scr[gl].astype(BF16), TRANS_B, preferred_element_type=F32)
        yt_ref[:, gl * SSM_GROUP:(gl + 1) * SSM_GROUP, :] = yt.reshape(SCAN_CHUNK, SSM_GROUP, J_ROWS)


def _ssm_scan(xt, wft, cct, ttt, a16, s0, l):
    gb = GROUP_BLOCK
    op_spec = pl.BlockSpec((None, gb, CHUNK_WIDTH, CHUNK_WIDTH), lambda t, g: (l, g, 0, 0))
    io_spec = pl.BlockSpec((None, SCAN_CHUNK, gb * SSM_GROUP, J_ROWS), lambda t, g: (t, 0, g, 0))
    return pl.pallas_call(
        _ssm_kernel,
        grid=(N_TILES, N_GROUPS // gb),
        in_specs=[
            io_spec, op_spec, op_spec, op_spec,
            pl.BlockSpec((None, gb, 2, N_PS, 128), lambda t, g: (l, g, 0, 0, 0)),
            pl.BlockSpec((gb, N_PS, CHUNK_WIDTH), lambda t, g: (g, 0, 0)),
        ],
        out_specs=[io_spec, pl.BlockSpec((None, gb, N_PS, CHUNK_WIDTH), lambda t, g: (t, g, 0, 0))],
        out_shape=[
            jax.ShapeDtypeStruct((N_TILES, SCAN_CHUNK, D_SSM, J_ROWS), F32),
            jax.ShapeDtypeStruct((N_TILES, N_GROUPS, N_PS, CHUNK_WIDTH), F32),
        ],
        scratch_shapes=[pltpu.VMEM((gb, J_ROWS, CHUNK_WIDTH), F32), pltpu.VMEM((gb, J_ROWS, CHUNK_WIDTH), F32),
                        pltpu.VMEM((gb, CHUNK_WIDTH, J_ROWS), F32)],
        compiler_params=_params("arbitrary", "arbitrary"),
        name="s5_chunk_scan",
    )(xt, wft, cct, ttt, a16, s0)


def _post_kernel(x_ref, yt_ref, ug_ref, vn_ref, mod_ref, wglu_ref, ws_ref, bs_ref, wout_f32_ref,
                 nw_ref, rwt_ref, rb_ref, hx_ref, route_ref, yg_scr, y_scr, wglut_ref, wout_ref):
    tile = pl.program_id(0)
    step = pl.program_id(1)

    @pl.when(_first_step())
    def _():
        wglut_ref[...] = wglu_ref[...].T.astype(BF16)
        wout_ref[...] = wout_f32_ref[...].astype(BF16)

    @pl.when(step == 0)
    def _():
        def chunk(n, carry):
            ps = n % N_PS
            c_hi = n // N_PS
            base = c_hi * (GMLP_CHUNK // SCAN_CHUNK) * N_PS + ps
            rows = [pl.ds(j * J_ROWS + base, GMLP_CHUNK // SCAN_CHUNK, stride=N_PS) for j in range(SCAN_CHUNK)]
            for h in range(GMLP_HEADS):
                v = jnp.concatenate([vn_ref[h, r, :] for r in rows], axis=0).astype(BF16)
                u = jnp.concatenate([ug_ref[h, r, :] for r in rows], axis=0)
                s = jnp.dot(ws_ref[h], v, preferred_element_type=F32) + bs_ref[h]
                yg = u * s
                for j, r in enumerate(rows):
                    yg_scr[h, r, :] = yg[j * 8:(j + 1) * 8]
            return carry

        lax.fori_loop(0, TILE_TOKENS // GMLP_CHUNK, chunk, 0)

    yt = _gelu_tanh(jnp.concatenate([yt_ref[k] for k in range(J_PER_STEP)], axis=1))
    yt = yt * _sigmoid(jnp.dot(wglut_ref[...], yt.astype(BF16), preferred_element_type=F32))
    for k in range(J_PER_STEP):
        y_scr[k * J_ROWS:(k + 1) * J_ROWS, :] = yt[:, k * J_ROWS:(k + 1) * J_ROWS].T
    row0 = pl.multiple_of(step * TOKEN_TILE, TOKEN_TILE)
    proj = jnp.dot(y_scr[...].astype(BF16), wout_ref[0:D_SSM, :], preferred_element_type=F32)
    yg = jnp.concatenate([yg_scr[h, pl.ds(row0, TOKEN_TILE), :] for h in range(GMLP_HEADS)], axis=1)
    proj += jnp.dot(yg.astype(BF16), wout_ref[D_SSM:, :], preferred_element_type=F32)
    x1 = x_ref[...] + _per_ps(lambda a, g: a * g, proj, mod_ref[2])
    h2 = _per_ps(lambda a, sc, sh: a * (1.0 + sc) + sh, _rmsnorm(x1, nw_ref[...]), mod_ref[4], mod_ref[3])
    hx_ref[:, 0:D_MODEL] = h2
    hx_ref[:, D_MODEL:2 * D_MODEL] = x1
    r_hi, r_lo = _split_bf16(rwt_ref[...])
    h_hi, h_lo = _split_bf16(h2)
    dot_t = lambda a, b: lax.dot_general(a, b, TRANS_B, preferred_element_type=F32)
    logits = dot_t(r_hi, h_hi) + (dot_t(r_hi, h_lo) + dot_t(r_lo, h_hi))
    scores = _sigmoid(logits)
    sel = scores + rb_ref[...]
    sc = [scores[e:e + 1, :] for e in range(N_EXPERTS)]
    sl = [sel[e:e + 1, :] for e in range(N_EXPERTS)]
    gscore = []
    for g in range(N_EXPERT_GROUPS):
        v0, v1, v2, v3 = sl[4 * g:4 * g + 4]
        hi01, lo01 = jnp.maximum(v0, v1), jnp.minimum(v0, v1)
        hi23, lo23 = jnp.maximum(v2, v3), jnp.minimum(v2, v3)
        top1 = jnp.maximum(hi01, hi23)
        top2 = jnp.maximum(jnp.minimum(hi01, hi23), jnp.maximum(lo01, lo23))
        gscore.append(top1 + top2)
    best = gscore[0]
    gidx = jnp.zeros_like(best, dtype=jnp.int32)
    for g in range(1, N_EXPERT_GROUPS):
        upd = gscore[g] > best
        gidx = jnp.where(upd, g, gidx)
        best = jnp.where(upd, gscore[g], best)

    def in_group(vals, k):
        out = vals[k]
        for g in range(1, N_EXPERT_GROUPS):
            out = jnp.where(gidx == g, vals[4 * g + k], out)
        return out

    v = [in_group(sl, k) for k in range(EXPERTS_PER_GROUP)]
    s = [in_group(sc, k) for k in range(EXPERTS_PER_GROUP)]
    w = []
    bits = jnp.zeros_like(gidx)
    for k in range(EXPERTS_PER_GROUP):
        rank = jnp.zeros_like(gidx)
        for j in range(EXPERTS_PER_GROUP):
            if j == k:
                continue
            ahead = (v[j] >= v[k]) if j < k else (v[j] > v[k])
            rank = rank + ahead.astype(jnp.int32)
        w.append(jnp.where(rank < 2, s[k], 0.0))
        bits = bits + jnp.where(rank < 2, 1 << k, 0)
    denom = (w[0] + w[1]) + (w[2] + w[3])
    gate = [wk / denom for wk in w]
    pair = jnp.full_like(gidx, len(PAIR_SLOT_A) - 1)
    for p in range(len(PAIR_SLOT_A) - 1):
        pair = jnp.where(bits == (1 << PAIR_SLOT_A[p]) + (1 << PAIR_SLOT_B[p]), p, pair)

    def slot_gate(table):
        out = gate[table[0]]
        for p in range(1, len(table)):
            out = jnp.where(pair == p, gate[table[p]], out)
        return out

    n_tok = route_ref.shape[1]
    route_ref[0:1, :] = (gidx * len(PAIR_SLOT_A) + pair).astype(F32)
    ps = lax.broadcasted_iota(jnp.int32, (1, n_tok), 1) % N_PS
    cond = jnp.where(tile < SAMPLE_TILE, 0, 1 + ps // SEGS_PER_SAMPLE_SEQ).astype(F32)
    lanes = jnp.concatenate([slot_gate(PAIR_SLOT_A), slot_gate(PAIR_SLOT_B), cond,
                             jnp.zeros((ROUTE_LANES - 3, n_tok), F32)], axis=0)
    hx_ref[:, 2 * D_MODEL:] = lanes.T


def _mix_out(x, yt, ug, vn, mod8, l, w_glu, w_s, b_s, w_out, norm2_w, rwt, rb):
    tm = TOKEN_TILE
    steps = TILE_TOKENS // tm
    tok = lambda n: pl.BlockSpec((tm, n), lambda t, s: (t * steps + s, 0))
    whole_tile = pl.BlockSpec((GMLP_HEADS, TILE_TOKENS, GMLP_HEAD_DIM), lambda t, s: (0, t, 0))
    lay = lambda *shape: pl.BlockSpec((None,) + shape, lambda t, s: (l,) + (0,) * len(shape))
    return pl.pallas_call(
        _post_kernel,
        grid=(N_TILES, steps),
        in_specs=[
            tok(D_MODEL),
            pl.BlockSpec((None, J_PER_STEP, D_SSM, J_ROWS), lambda t, s: (t, s, 0, 0)),
            whole_tile, whole_tile,
            pl.BlockSpec((None, None, N_MOD, N_PS, D_MODEL), lambda t, s: (l, t, 0, 0, 0)),
            lay(D_SSM, D_SSM), lay(GMLP_HEADS, GMLP_CHUNK, GMLP_CHUNK),
            lay(GMLP_HEADS, GMLP_CHUNK, 1), lay(D_MODEL, D_MODEL), lay(1, D_MODEL),
            pl.BlockSpec((N_EXPERTS, D_MODEL), lambda t, s: (0, 0)),
            pl.BlockSpec((N_EXPERTS, 1), lambda t, s: (0, 0)),
        ],
        out_specs=[tok(HX_WIDTH), pl.BlockSpec((1, tm), lambda t, s: (0, t * steps + s))],
        out_shape=[
            jax.ShapeDtypeStruct((T_ALL, HX_WIDTH), F32),
            jax.ShapeDtypeStruct((1, T_ALL), F32),
        ],
        scratch_shapes=[pltpu.VMEM((GMLP_HEADS, TILE_TOKENS, GMLP_HEAD_DIM), F32), pltpu.VMEM((tm, D_SSM), F32),
                        pltpu.VMEM((D_SSM, D_SSM), BF16), pltpu.VMEM((D_MODEL, D_MODEL), BF16)],
        compiler_params=_params("arbitrary", "arbitrary"),
        name="mixers_out_router",
    )(x, yt, ug, vn, mod8, w_glu, w_s, b_s, w_out, norm2_w.reshape(DEPTH, 1, D_MODEL), rwt, rb)


def _moe_kernel(ea_ref, eb_ref, new_a_ref, new_b_ref, nv_ref, size_ref, src_ref, drow_ref,
                hx_hbm, wga_ref, wua_ref, wda_ref, wgb_ref, wub_ref, wdb_ref, g2_ref, fw_ref,
                o_hbm, wga_s, wua_s, wda_s, wgb_s, wub_s, wdb_s, hx_buf, o_buf, z_buf, g_sem, s_sem, z_sem,
                *, final):
    n = pl.program_id(0)
    n_valid = nv_ref[0]
    slot = n % 2

    def by_size(tile, fn):
        for quarters in range(1, TAIL_STEPS + 1):
            @pl.when(size_ref[tile] == quarters)
            def _(rows=quarters * TAIL_QUARTER):
                fn(rows)

    def start_gather(tile, sl, rows):
        base = tile * MOE_TM
        for r in range(rows):
            pltpu.make_async_copy(hx_hbm.at[pl.ds(src_ref[base + r], 1)], hx_buf.at[sl, pl.ds(r, 1)],
                                  g_sem.at[sl]).start()

    def wait_gather(sl, rows):
        pltpu.make_async_copy(hx_hbm.at[pl.ds(0, rows)], hx_buf.at[sl, pl.ds(0, rows)], g_sem.at[sl]).wait()

    def start_scatter(tile, sl, rows):
        base = tile * MOE_TM
        for r in range(rows):
            pltpu.make_async_copy(o_buf.at[sl, pl.ds(r, 1)], o_hbm.at[pl.ds(drow_ref[base + r], 1)],
                                  s_sem.at[sl]).start()

    def wait_scatter(sl, rows):
        pltpu.make_async_copy(o_buf.at[sl, pl.ds(0, rows)], o_hbm.at[pl.ds(0, rows)], s_sem.at[sl]).wait()

    @pl.when(n == 0)
    def _():
        z_buf[...] = jnp.zeros_like(z_buf)
        spare = [pltpu.make_async_copy(z_buf, o_hbm.at[pl.ds(T_ALL + i * MOE_TM, MOE_TM)], z_sem)
                 for i in range(SPARE_ROWS // MOE_TM)]
        for cp in spare:
            cp.start()
        for cp in spare:
            cp.wait()
        by_size(0, lambda rows: start_gather(0, 0, rows))

    @pl.when(n + 1 < n_valid)
    def _():
        by_size(n + 1, lambda rows: start_gather(n + 1, 1 - slot, rows))

    def tile_step(rows):
        wait_gather(slot, rows)

        @pl.when(n >= 2)
        def _():
            by_size(n - 2, lambda r: wait_scatter(slot, r))

        h = hx_buf[slot, 0:rows, 0:D_MODEL].astype(BF16)
        lanes = hx_buf[slot, 0:rows, 2 * D_MODEL:]

        def expert(wg, wu, wd, gate):
            hg = jnp.dot(h, wg[...].astype(BF16), preferred_element_type=F32)
            hu = jnp.dot(h, wu[...].astype(BF16), preferred_element_type=F32)
            act = hg * _sigmoid(hg) * hu * gate
            return jnp.dot(act.astype(BF16), wd[...].astype(BF16), preferred_element_type=F32)

        y = expert(wga_ref, wua_ref, wda_ref, lanes[:, 0:1]) + expert(wgb_ref, wub_ref, wdb_ref, lanes[:, 1:2])
        cond_row = lanes[:, 2:3]
        gate2 = jnp.where(cond_row == 0.0, g2_ref[0:1, :], jnp.where(cond_row == 1.0, g2_ref[1:2, :], g2_ref[2:3, :]))
        x2 = hx_buf[slot, 0:rows, D_MODEL:2 * D_MODEL] + gate2 * y
        o_buf[slot, 0:rows] = _rmsnorm(x2, fw_ref[...]) if final else x2
        start_scatter(n, slot, rows)

    @pl.when(n < n_valid)
    def _():
        by_size(n, tile_step)

    @pl.when(n == MOE_TILES - 1)
    def _():
        last = n_valid - 1
        by_size(last, lambda r: wait_scatter(last % 2, r))
        by_size(last - 1, lambda r: wait_scatter(1 - last % 2, r))


def _experts(tables, hx, mods, l, w_gate, w_up, w_down, final_w, final):
    w_a = lambda r, c: pl.BlockSpec((None, None, r, c), lambda n, ea, eb, *_: (l, ea[n], 0, 0))
    w_b = lambda r, c: pl.BlockSpec((None, None, r, c), lambda n, ea, eb, *_: (l, eb[n], 0, 0))
    up = pltpu.VMEM((D_MODEL, D_EXPERT), BF16)
    down = pltpu.VMEM((D_EXPERT, D_MODEL), BF16)
    return pl.pallas_call(
        functools.partial(_moe_kernel, final=final),
        grid_spec=pltpu.PrefetchScalarGridSpec(
            num_scalar_prefetch=len(tables),
            grid=(MOE_TILES,),
            in_specs=[
                pl.BlockSpec(memory_space=pl.ANY),
                w_a(D_MODEL, D_EXPERT), w_a(D_MODEL, D_EXPERT), w_a(D_EXPERT, D_MODEL),
                w_b(D_MODEL, D_EXPERT), w_b(D_MODEL, D_EXPERT), w_b(D_EXPERT, D_MODEL),
                pl.BlockSpec((None, MOD_ROWS, D_MODEL), lambda n, *_: (l, 0, N_MOD - 1)),
                pl.BlockSpec((1, D_MODEL), lambda n, *_: (0, 0)),
            ],
            out_specs=pl.BlockSpec(memory_space=pl.ANY),
            scratch_shapes=[up, up, down, up, up, down,
                            pltpu.VMEM((2, MOE_TM, HX_WIDTH), F32), pltpu.VMEM((2, MOE_TM, D_MODEL), F32),
                            pltpu.VMEM((MOE_TM, D_MODEL), F32),
                            pltpu.SemaphoreType.DMA((2,)), pltpu.SemaphoreType.DMA((2,)), pltpu.SemaphoreType.DMA(())],
        ),
        out_shape=jax.ShapeDtypeStruct((OUT_ROWS, D_MODEL), F32),
        compiler_params=_params("arbitrary"),
        name="experts",
    )(*tables, hx, w_gate, w_up, w_down, w_gate, w_up, w_down, mods, final_w.reshape(1, D_MODEL))


def _count_before(flags):
    n, k = flags.shape
    blocks = flags.reshape(n // 128, 128, k).astype(F32)
    strictly_lower = jnp.tril(jnp.ones((128, 128), F32), -1)
    within = jnp.einsum("ij,bjk->bik", strictly_lower, blocks)
    totals = jnp.sum(blocks, axis=1)
    before = jnp.cumsum(totals, axis=0) - totals
    return (within + before[:, None, :]).reshape(n, k).astype(jnp.int32)


def _routing_tables(cls, to_sequence_order):
    n_cls = N_EXPERT_GROUPS * len(PAIR_SLOT_A)
    onehot = (cls[:, None] == jnp.arange(n_cls, dtype=jnp.int32)[None, :]).astype(jnp.int32)
    counts = jnp.sum(onehot, axis=0)
    tiles = (counts + MOE_TM - 1) // MOE_TM
    tile_end = jnp.cumsum(tiles)
    row_in_class = (tile_end - tiles)[None, :] * MOE_TM + _count_before(onehot)
    dst = jnp.sum(onehot * row_in_class, axis=1)
    token_plus_1 = jnp.zeros((MOE_ROWS,), jnp.int32).at[dst].set(jnp.arange(1, T_ALL + 1, dtype=jnp.int32))
    is_pad = token_plus_1 == 0
    src = jnp.maximum(token_plus_1 - 1, 0)
    n_valid = tile_end[-1]
    tile_id = jnp.minimum(jnp.arange(MOE_TILES, dtype=jnp.int32), n_valid - 1)
    tile_cls = jnp.sum((tile_end[None, :] <= tile_id[:, None]).astype(jnp.int32), axis=1)
    group, pair = tile_cls // len(PAIR_SLOT_A), tile_cls % len(PAIR_SLOT_A)
    in_cls = (tile_cls[:, None] == jnp.arange(n_cls, dtype=jnp.int32)[None, :]).astype(jnp.int32)
    left = jnp.sum(in_cls * (counts[None, :] - (tile_id[:, None] - (tile_end - tiles)[None, :]) * MOE_TM), axis=1)
    size = jnp.clip((left + TAIL_QUARTER - 1) // TAIL_QUARTER, 1, TAIL_STEPS).astype(jnp.int32)
    row = jnp.arange(MOE_ROWS, dtype=jnp.int32)
    processed = jnp.logical_and(row % MOE_TM < jnp.repeat(size, MOE_TM) * TAIL_QUARTER, row // MOE_TM < n_valid)
    moved_pad = jnp.logical_and(is_pad, processed)
    spare = T_ALL + _count_before(moved_pad.astype(jnp.int32)[:, None])[:, 0]
    if to_sequence_order:
        j, c, ps = (src // J_ROWS) % SCAN_CHUNK, (src // N_PS) % SEG_CHUNKS, src % N_PS
        target = (src // TILE_TOKENS) * TILE_TOKENS + ps * SEG_LEN + c * SCAN_CHUNK + j
    else:
        target = src
    drow = jnp.where(is_pad, jnp.where(moved_pad, spare, T_ALL), target).astype(jnp.int32)

    def slot_expert(table):
        local = jnp.full_like(pair, table[-1])
        for p in range(len(table) - 1):
            local = jnp.where(pair == p, table[p], local)
        return group * EXPERTS_PER_GROUP + local

    e_a = slot_expert(PAIR_SLOT_A)
    e_b = slot_expert(PAIR_SLOT_B)
    first = jnp.ones((1,), jnp.int32)
    new_a = jnp.concatenate([first, (e_a[1:] != e_a[:-1]).astype(jnp.int32)])
    new_b = jnp.concatenate([first, (e_b[1:] != e_b[:-1]).astype(jnp.int32)])
    return e_a, e_b, new_a, new_b, n_valid.reshape(1).astype(jnp.int32), size, src, drow


def _dirs_on_lanes(p):
    p = jnp.moveaxis(p, 1, -2)
    return p.reshape(p.shape[:-2] + (2 * SSM_STATE,))


def _to_internal_order(x):
    x = x.reshape(N_TILES, N_PS, SEG_CHUNKS, SCAN_CHUNK, D_MODEL)
    return x.transpose(0, 3, 2, 1, 4).reshape(T_ALL, D_MODEL)


def _gmlp_position_order(w):
    n_lo = GMLP_CHUNK // SCAN_CHUNK
    lead = w.shape[:2]
    w = w.reshape(lead + (n_lo, SCAN_CHUNK) + w.shape[3:])
    w = jnp.swapaxes(w, 2, 3)
    return w.reshape(lead + (GMLP_CHUNK,) + w.shape[4:])


def kernel(x_prompt, x_sample, c, state_ssm_re, state_ssm_im, c_ctx, norm1_w, norm2_w, w_mod, b_mod, w_in,
           ssm_a_re, ssm_a_im, ssm_log_dt, ssm_b_re, ssm_b_im, ssm_c_re, ssm_c_im, ssm_d, w_glu,
           gmlp_ln_w, gmlp_ln_b, gmlp_w_s, gmlp_b_s, w_out, router_w, router_b, w_gate, w_up, w_down,
           final_norm_w):
    x = jnp.concatenate([x_prompt.reshape(T_PROMPT, D_MODEL), x_sample.reshape(T_SAMPLE, D_MODEL)], axis=0)
    x = _to_internal_order(x)

    cvec = jnp.concatenate([c_ctx[None, :], c, jnp.zeros((MOD_ROWS - 1 - N_SAMPLE_SEQ, D_MODEL), F32)], axis=0)
    mods = _modulation(cvec, w_mod, b_mod)
    ps_row = [[0] * N_PS] * SAMPLE_TILE + [[1 + p // SEGS_PER_SAMPLE_SEQ for p in range(N_PS)]]
    mod8 = mods.reshape(DEPTH, MOD_ROWS, N_MOD, D_MODEL)[:, jnp.array(ps_row, jnp.int32)]
    mod8 = mod8.transpose(0, 1, 3, 2, 4)

    log_dt = jnp.broadcast_to(ssm_log_dt[..., None], ssm_a_re.shape)
    d_lanes = jnp.tile(ssm_d.reshape(DEPTH, N_GROUPS, SSM_GROUP), (1, 1, 128 // SSM_GROUP))
    vecs = jnp.stack([_dirs_on_lanes(ssm_a_re), _dirs_on_lanes(ssm_a_im), _dirs_on_lanes(log_dt), d_lanes], axis=2)
    vecs = jnp.concatenate([vecs, jnp.zeros((DEPTH, N_GROUPS, 4, 2 * SSM_STATE), F32)], axis=2)
    mats = jnp.stack([_dirs_on_lanes(jnp.swapaxes(ssm_b_re, -1, -2)), _dirs_on_lanes(jnp.swapaxes(ssm_b_im, -1, -2)),
                      _dirs_on_lanes(ssm_c_re), _dirs_on_lanes(ssm_c_im)], axis=2)
    wft, cct, ttt, a16 = _ssm_operators(vecs, mats)

    w_s = jnp.swapaxes(_gmlp_position_order(jnp.swapaxes(_gmlp_position_order(gmlp_w_s), 2, 3)), 2, 3).astype(BF16)
    b_s = _gmlp_position_order(gmlp_b_s)[..., None]
    rwt = router_w.T
    rb = router_b.reshape(N_EXPERTS, 1)

    new_re, new_im = [], []
    for l in range(DEPTH):
        ug, vn, xt = _input_proj(x, mod8, l, norm1_w, w_in, gmlp_ln_w, gmlp_ln_b)
        s0 = jnp.concatenate([state_ssm_re[:, l].transpose(2, 0, 1, 3).reshape(N_GROUPS, N_SAMPLE_SEQ, 128),
                              state_ssm_im[:, l].transpose(2, 0, 1, 3).reshape(N_GROUPS, N_SAMPLE_SEQ, 128)], axis=-1)
        s0 = jnp.repeat(s0, SEGS_PER_SAMPLE_SEQ, axis=1)
        yt, fin = _ssm_scan(xt, wft, cct, ttt, a16, s0, l)
        fin = fin[:SAMPLE_TILE].reshape(SAMPLE_TILE, N_GROUPS, N_PS, 2, 2, SSM_STATE)
        fin = fin.transpose(3, 0, 2, 4, 1, 5).reshape(2, N_PROMPT_SEQ, 2, N_GROUPS, SSM_STATE)
        new_re.append(fin[0])
        new_im.append(fin[1])
        hx, route = _mix_out(x, yt, ug, vn, mod8, l, w_glu, w_s, b_s, w_out, norm2_w, rwt, rb)
        final = l == DEPTH - 1
        tables = _routing_tables(route[0].astype(jnp.int32), to_sequence_order=final)
        x = _experts(tables, hx, mods, l, w_gate, w_up, w_down, final_norm_w, final)

    y_prompt = x[:T_PROMPT].reshape(N_PROMPT_SEQ, PROMPT_LEN, D_MODEL)
    y_sample = x[T_PROMPT:T_ALL].reshape(N_SAMPLE_SEQ, SAMPLE_LEN, D_MODEL)
    return (y_prompt, y_sample, jnp.stack(new_re, axis=1), jnp.stack(new_im, axis=1))
```

```python
import functools
import math

import jax
import jax.numpy as jnp
from jax import lax
from jax.experimental import pallas as pl
from jax.experimental.pallas import tpu as pltpu

F32 = jnp.float32
BF16 = jnp.bfloat16

D_MODEL = 1024
N_PROMPT_SEQ = 16
PROMPT_LEN = 256
N_SAMPLE_SEQ = 2
SAMPLE_LEN = 1024
T_PROMPT = N_PROMPT_SEQ * PROMPT_LEN
T_SAMPLE = N_SAMPLE_SEQ * SAMPLE_LEN
T_ALL = T_PROMPT + T_SAMPLE
DEPTH = 2
D_SSM = 512
SSM_GROUP = 16
N_GROUPS = 32
SSM_STATE = 64
D_GMLP = 512
GMLP_HEADS = 4
GMLP_HEAD_DIM = 128
GMLP_CHUNK = 128
N_EXPERTS = 16
N_EXPERT_GROUPS = 4
EXPERTS_PER_GROUP = 4
D_EXPERT = 512
N_MOD = 6
EPS = 1e-6

SCAN_CHUNK = 16
CHUNK_WIDTH = SCAN_CHUNK * SSM_GROUP
SEG_LEN = 256
SEG_CHUNKS = SEG_LEN // SCAN_CHUNK
N_PS = 8
TILE_TOKENS = N_PS * SEG_LEN
N_TILES = T_ALL // TILE_TOKENS
SAMPLE_TILE = T_PROMPT // TILE_TOKENS
SEGS_PER_SAMPLE_SEQ = SAMPLE_LEN // SEG_LEN
J_ROWS = SEG_CHUNKS * N_PS
GROUP_BLOCK = 8
PREP_GROUPS = 8
MOD_ROWS = 8
MOD_K_STEPS = 4

J_PER_STEP = 4
TOKEN_TILE = J_PER_STEP * J_ROWS
PAIR_SLOT_A = (0, 0, 0, 1, 1, 3)
PAIR_SLOT_B = (1, 2, 3, 3, 2, 2)
MOE_TM = 256
TAIL_STEPS = 4
TAIL_QUARTER = MOE_TM // TAIL_STEPS
N_CLASSES = N_EXPERT_GROUPS * len(PAIR_SLOT_A)
MOE_TILES = T_ALL // MOE_TM + N_CLASSES
MOE_ROWS = MOE_TILES * MOE_TM
SPARE_ROWS = N_CLASSES * TAIL_QUARTER
OUT_ROWS = T_ALL + SPARE_ROWS
ROUTE_LANES = 128
HX_WIDTH = 2 * D_MODEL + ROUTE_LANES
VMEM_LIMIT = 56 * 1024 * 1024
TRANS_B = (((1,), (1,)), ((), ()))


def _sigmoid(x):
    return 1.0 / (1.0 + jnp.exp(-x))


def _gelu_tanh(x):
    c = math.sqrt(2.0 / math.pi)
    return x * (0.5 * (1.0 + jnp.tanh(c * (x + 0.044715 * (x * x * x)))))


def _split_bf16(a):
    hi = a.astype(BF16)
    return hi, (a - hi.astype(F32)).astype(BF16)


def _rmsnorm(x, w):
    return x * lax.rsqrt(jnp.mean(x * x, axis=-1, keepdims=True) + EPS) * w


def _per_ps(fn, a, *mods):
    rows, d = a.shape
    out = fn(a.reshape(rows // N_PS, N_PS, d), *[m[None] for m in mods])
    return out.reshape(rows, d)


def _params(*sem):
    return pltpu.CompilerParams(dimension_semantics=sem, vmem_limit_bytes=VMEM_LIMIT)


def _mod_kernel(c_ref, w_ref, b_ref, o_ref):
    c = c_ref[...]
    part = jnp.dot((c * _sigmoid(c)).astype(BF16), w_ref[...].astype(BF16), preferred_element_type=F32)

    @pl.when(pl.program_id(1) == 0)
    def _():
        o_ref[...] = part + b_ref[...]

    @pl.when(pl.program_id(1) > 0)
    def _():
        o_ref[...] += part


def _modulation(cvec, w_mod, b_mod):
    kb = D_MODEL // MOD_K_STEPS
    c_blocks = cvec.reshape(MOD_ROWS, MOD_K_STEPS, kb).transpose(1, 0, 2)
    return pl.pallas_call(
        _mod_kernel,
        grid=(DEPTH, MOD_K_STEPS),
        in_specs=[
            pl.BlockSpec((None, MOD_ROWS, kb), lambda l, k: (k, 0, 0)),
            pl.BlockSpec((None, kb, N_MOD * D_MODEL), lambda l, k: (l, k, 0)),
            pl.BlockSpec((None, 1, N_MOD * D_MODEL), lambda l, k: (l, 0, 0)),
        ],
        out_specs=pl.BlockSpec((None, MOD_ROWS, N_MOD * D_MODEL), lambda l, k: (l, 0, 0)),
        out_shape=jax.ShapeDtypeStruct((DEPTH, MOD_ROWS, N_MOD * D_MODEL), F32),
        compiler_params=_params("arbitrary", "arbitrary"),
        name="adaln_mod",
    )(c_blocks, w_mod, b_mod.reshape(DEPTH, 1, N_MOD * D_MODEL))


def _first_step():
    return jnp.logical_and(pl.program_id(0) == 0, pl.program_id(1) == 0)


def _in_kernel(x_ref, mod_ref, nw_ref, w_ref, lnw_ref, lnb_ref, ug_ref, vn_ref, xt_ref, wg_ref, wst_ref):
    @pl.when(_first_step())
    def _():
        wg_ref[...] = w_ref[:, D_SSM:].astype(BF16)
        wst_ref[...] = w_ref[:, :D_SSM].T.astype(BF16)

    y = _rmsnorm(x_ref[...], nw_ref[...])
    h = _per_ps(lambda a, sc, sh: a * (1.0 + sc) + sh, y, mod_ref[1], mod_ref[0]).astype(BF16)
    zg = _gelu_tanh(jnp.dot(h, wg_ref[...], preferred_element_type=F32))
    v = zg[:, D_GMLP:]
    mu = jnp.mean(v, axis=-1, keepdims=True)
    vc = v - mu
    var = jnp.mean(vc * vc, axis=-1, keepdims=True)
    vn = vc * lax.rsqrt(var + EPS) * lnw_ref[...] + lnb_ref[...]
    for hd in range(GMLP_HEADS):
        cols = slice(hd * GMLP_HEAD_DIM, (hd + 1) * GMLP_HEAD_DIM)
        ug_ref[hd] = zg[:, cols]
        vn_ref[hd] = vn[:, cols]
    xt = lax.dot_general(wst_ref[...], h, TRANS_B, preferred_element_type=F32).astype(BF16)
    for k in range(J_PER_STEP):
        xt_ref[k] = xt[:, k * J_ROWS:(k + 1) * J_ROWS]


def _input_proj(x, mod8, l, norm1_w, w_in, ln_w, ln_b):
    tm = TOKEN_TILE
    steps = TILE_TOKENS // tm
    tok = lambda n: pl.BlockSpec((tm, n), lambda t, s: (t * steps + s, 0))
    heads = pl.BlockSpec((GMLP_HEADS, tm, GMLP_HEAD_DIM), lambda t, s: (0, t * steps + s, 0))
    lay = lambda *shape: pl.BlockSpec((None,) + shape, lambda t, s: (l,) + (0,) * len(shape))
    return pl.pallas_call(
        _in_kernel,
        grid=(N_TILES, steps),
        in_specs=[
            tok(D_MODEL),
            pl.BlockSpec((None, None, N_MOD, N_PS, D_MODEL), lambda t, s: (l, t, 0, 0, 0)),
            lay(1, D_MODEL), lay(D_MODEL, D_SSM + 2 * D_GMLP), lay(1, D_GMLP), lay(1, D_GMLP),
        ],
        out_specs=[heads, heads,
                   pl.BlockSpec((None, J_PER_STEP, D_SSM, J_ROWS), lambda t, s: (t, s, 0, 0))],
        out_shape=[jax.ShapeDtypeStruct((GMLP_HEADS, T_ALL, GMLP_HEAD_DIM), F32)] * 2 + [
            jax.ShapeDtypeStruct((N_TILES, SCAN_CHUNK, D_SSM, J_ROWS), BF16)],
        scratch_shapes=[pltpu.VMEM((D_MODEL, 2 * D_GMLP), BF16), pltpu.VMEM((D_SSM, D_MODEL), BF16)],
        compiler_params=_params("arbitrary", "arbitrary"),
        name="norm1_in_proj",
    )(x, mod8, norm1_w.reshape(DEPTH, 1, D_MODEL), w_in,
      ln_w.reshape(DEPTH, 1, D_GMLP), ln_b.reshape(DEPTH, 1, D_GMLP))


def _shift_lanes_right(a, b, s, lane):
    if s == 0:
        return a, b
    if s == 128:
        return jnp.zeros_like(a), a
    if s < 128:
        ra = pltpu.roll(a, s, 1)
        rb = pltpu.roll(b, s, 1)
        return jnp.where(lane >= s, ra, 0.0), jnp.where(lane >= s, rb, ra)
    t = s - 128
    return jnp.zeros_like(a), jnp.where(lane >= t, pltpu.roll(a, t, 1), 0.0)


def _shift_lanes_left(a, b, s, lane):
    if s == 0:
        return a, b
    if s == 128:
        return b, jnp.zeros_like(b)
    if s < 128:
        ra = pltpu.roll(a, 128 - s, 1)
        rb = pltpu.roll(b, 128 - s, 1)
        return jnp.where(lane < 128 - s, ra, rb), jnp.where(lane < 128 - s, rb, 0.0)
    t = s - 128
    return jnp.where(lane < 128 - t, pltpu.roll(b, 128 - t, 1), 0.0), jnp.zeros_like(b)


def _prep_kernel(*refs):
    for g in range(PREP_GROUPS):
        _prep_group(*[r.at[g] for r in refs])


def _prep_group(vec_ref, mat_ref, wft_ref, cct_ref, ttt_ref, a16_ref, wf_scr, cm_scr, tt_scr):
    a_re = vec_ref[0:1, :]
    a_im = vec_ref[1:2, :]
    dt = jnp.exp(vec_ref[2:3, :])
    d_skip = vec_ref[3:4, :]
    mag = jnp.exp(a_re * dt)
    ang = a_im * dt
    ab_r = mag * jnp.cos(ang)
    ab_i = mag * jnp.sin(ang)
    den = a_re * a_re + a_im * a_im
    nr = ab_r - 1.0
    q_r = (nr * a_re + ab_i * a_im) / den
    q_i = (ab_i * a_re - nr * a_im) / den
    bt_r = mat_ref[0]
    bt_i = mat_ref[1]
    c_r = mat_ref[2]
    c_i = mat_ref[3]
    bb_r = q_r * bt_r - q_i * bt_i
    bb_i = q_r * bt_i + q_i * bt_r
    p_r = [jnp.ones_like(ab_r)]
    p_i = [jnp.zeros_like(ab_r)]
    for _ in range(SCAN_CHUNK):
        pr, pi = p_r[-1], p_i[-1]
        p_r.append(pr * ab_r - pi * ab_i)
        p_i.append(pr * ab_i + pi * ab_r)
    a16_ref[0] = jnp.broadcast_to(p_r[SCAN_CHUNK], (N_PS, 128))
    a16_ref[1] = jnp.broadcast_to(p_i[SCAN_CHUNK], (N_PS, 128))

    lane = lax.broadcasted_iota(jnp.int32, (1, 128), 1)
    is_fwd = lane < SSM_STATE

    def pick(mf, mb):
        return jnp.where(is_fwd, p_r[mf], p_r[mb]), jnp.where(is_fwd, p_i[mf], p_i[mb])

    for j in range(SCAN_CHUNK):
        rows = slice(j * SSM_GROUP, (j + 1) * SSM_GROUP)
        wr, wi = pick(SCAN_CHUNK - 1 - j, j)
        wf_scr[rows, 0:128] = bb_r * wr - bb_i * wi
        wf_scr[rows, 128:256] = bb_r * wi + bb_i * wr
        wr, wi = pick(j + 1, SCAN_CHUNK - j)
        cct_ref[rows, 0:128] = (c_r * wr - c_i * wi).astype(BF16)
        cct_ref[rows, 128:256] = (-(c_r * wi + c_i * wr)).astype(BF16)
        wr, wi = pick(j, SCAN_CHUNK - 1 - j)
        cm_scr[rows, 0:128] = c_r * wr - c_i * wi
        cm_scr[rows, 128:256] = c_r * wi + c_i * wr
    wft_ref[...] = wf_scr[...].T.astype(BF16)

    zero = jnp.zeros_like(bb_r)
    cm_hi, cm_lo = _split_bf16(cm_scr[...])
    dot_t = lambda a, b: lax.dot_general(a, b, TRANS_B, preferred_element_type=F32)

    def lag_rows(keep):
        lhs = jnp.concatenate([jnp.where(keep, bb_r, zero), jnp.where(keep, -bb_i, zero)], axis=1)
        hi, lo = _split_bf16(lhs)
        return dot_t(hi, cm_hi) + (dot_t(hi, cm_lo) + dot_t(lo, cm_hi))

    mf = lag_rows(is_fwd)
    mb = lag_rows(jnp.logical_not(is_fwd))
    mf_a, mf_b = mf[:, 0:128], mf[:, 128:256]
    mb_a, mb_b = mb[:, 0:128], mb[:, 128:256]
    row_h = lax.broadcasted_iota(jnp.int32, (SSM_GROUP, 128), 0)
    lane_h = lax.broadcasted_iota(jnp.int32, (SSM_GROUP, 128), 1)
    for jp in range(SCAN_CHUNK):
        rows = slice(jp * SSM_GROUP, (jp + 1) * SSM_GROUP)
        fa, fb = _shift_lanes_right(mf_a, mf_b, SSM_GROUP * jp, lane)
        ba, bb = _shift_lanes_left(mb_a, mb_b, SSM_GROUP * (SCAN_CHUNK - 1 - jp), lane)
        diag = SSM_GROUP * jp + row_h
        tt_scr[rows, 0:128] = fa + ba + jnp.where(lane_h == diag, d_skip, 0.0)
        tt_scr[rows, 128:256] = fb + bb + jnp.where(lane_h + 128 == diag, d_skip, 0.0)
    ttt_ref[...] = tt_scr[...].T.astype(BF16)


def _ssm_operators(vecs, mats):
    op = jax.ShapeDtypeStruct((DEPTH, N_GROUPS, CHUNK_WIDTH, CHUNK_WIDTH), BF16)
    pg = PREP_GROUPS
    op_spec = pl.BlockSpec((None, pg, CHUNK_WIDTH, CHUNK_WIDTH), lambda l, g: (l, g, 0, 0))
    sq = pltpu.VMEM((pg, CHUNK_WIDTH, CHUNK_WIDTH), F32)
    return pl.pallas_call(
        _prep_kernel,
        grid=(DEPTH, N_GROUPS // pg),
        in_specs=[
            pl.BlockSpec((None, pg, 8, 128), lambda l, g: (l, g, 0, 0)),
            pl.BlockSpec((None, pg, 4, SSM_GROUP, 128), lambda l, g: (l, g, 0, 0, 0)),
        ],
        out_specs=[op_spec, op_spec, op_spec,
                   pl.BlockSpec((None, pg, 2, N_PS, 128), lambda l, g: (l, g, 0, 0, 0))],
        out_shape=[op, op, op, jax.ShapeDtypeStruct((DEPTH, N_GROUPS, 2, N_PS, 128), F32)],
        scratch_shapes=[sq, sq, sq],
        compiler_params=_params("arbitrary", "arbitrary"),
        name="s5_operators",
    )(vecs, mats)


def _ssm_kernel(xt_ref, wft_ref, cct_ref, ttt_ref, a16_ref, s0_ref, yt_ref, fin_ref, s_scr, f_scr, ft_scr):
    tile = pl.program_id(0)
    lane = lax.broadcasted_iota(jnp.int32, (GROUP_BLOCK, N_PS, 128), 2)
    is_fwd = lane < SSM_STATE
    half = SSM_STATE

    def group_x(gl):
        return xt_ref[:, gl * SSM_GROUP:(gl + 1) * SSM_GROUP, :].reshape(CHUNK_WIDTH, J_ROWS)

    for gl in range(GROUP_BLOCK):
        ft_scr[gl] = jnp.dot(wft_ref[gl], group_x(gl), preferred_element_type=F32)
        f_scr[gl] = ft_scr[gl].T

    a_r = a16_ref[:, 0]
    a_i = a16_ref[:, 1]

    def scan(s_r, s_i):
        for i in range(SEG_CHUNKS):
            rf = slice(i * N_PS, (i + 1) * N_PS)
            rb = slice((SEG_CHUNKS - 1 - i) * N_PS, (SEG_CHUNKS - i) * N_PS)
            s_scr[:, rf, 0:half] = s_r[:, :, 0:half]
            s_scr[:, rf, 128:128 + half] = s_i[:, :, 0:half]
            s_scr[:, rb, half:128] = s_r[:, :, half:128]
            s_scr[:, rb, 128 + half:256] = s_i[:, :, half:128]
            f_r = jnp.where(is_fwd, f_scr[:, rf, 0:128], f_scr[:, rb, 0:128])
            f_i = jnp.where(is_fwd, f_scr[:, rf, 128:256], f_scr[:, rb, 128:256])
            s_r, s_i = a_r * s_r - a_i * s_i + f_r, a_r * s_i + a_i * s_r + f_i
        return s_r, s_i

    zeros = jnp.zeros((GROUP_BLOCK, N_PS, 128), F32)
    z_r, z_i = scan(zeros, zeros)
    fin_ref[:, :, 0:128] = z_r
    fin_ref[:, :, 128:256] = z_i

    @pl.when(tile == SAMPLE_TILE)
    def _():
        b_r, b_i = a_r, a_i
        for _ in range(4):
            b_r, b_i = b_r * b_r - b_i * b_i, 2.0 * (b_r * b_i)
        seg = lax.broadcasted_iota(jnp.int32, (GROUP_BLOCK, N_PS, 128), 1) % SEGS_PER_SAMPLE_SEQ
        i_r = s0_ref[:, :, 0:128]
        i_i = s0_ref[:, :, 128:256]
        for step in range(1, SEGS_PER_SAMPLE_SEQ):
            pr = jnp.where(is_fwd, pltpu.roll(i_r, 1, 1), pltpu.roll(i_r, N_PS - 1, 1))
            pi = jnp.where(is_fwd, pltpu.roll(i_i, 1, 1), pltpu.roll(i_i, N_PS - 1, 1))
            zr = jnp.where(is_fwd, pltpu.roll(z_r, 1, 1), pltpu.roll(z_r, N_PS - 1, 1))
            zi = jnp.where(is_fwd, pltpu.roll(z_i, 1, 1), pltpu.roll(z_i, N_PS - 1, 1))
            n_r = b_r * pr - b_i * pi + zr
            n_i = b_r * pi + b_i * pr + zi
            first = jnp.where(is_fwd, step, 0)
            last = jnp.where(is_fwd, SEGS_PER_SAMPLE_SEQ - 1, SEGS_PER_SAMPLE_SEQ - 1 - step)
            upd = jnp.logical_and(seg >= first, seg <= last)
            i_r = jnp.where(upd, n_r, i_r)
            i_i = jnp.where(upd, n_i, i_i)
        scan(i_r, i_i)

    for gl in range(GROUP_BLOCK):
        yt = jnp.dot(ttt_ref[gl], group_x(gl), preferred_element_type=F32)
        yt += lax.dot_general(cct_ref[gl], s_scr[gl].astype(BF16), TRANS_B, preferred_element_type=F32)
        yt_ref[:, gl * SSM_GROUP:(gl + 1) * SSM_GROUP, :] = yt.reshape(SCAN_CHUNK, SSM_GROUP, J_ROWS)


def _ssm_scan(xt, wft, cct, ttt, a16, s0, l):
    gb = GROUP_BLOCK
    op_spec = pl.BlockSpec((None, gb, CHUNK_WIDTH, CHUNK_WIDTH), lambda t, g: (l, g, 0, 0))
    io_spec = pl.BlockSpec((None, SCAN_CHUNK, gb * SSM_GROUP, J_ROWS), lambda t, g: (t, 0, g, 0))
    return pl.pallas_call(
        _ssm_kernel,
        grid=(N_TILES, N_GROUPS // gb),
        in_specs=[
            io_spec, op_spec, op_spec, op_spec,
            pl.BlockSpec((None, gb, 2, N_PS, 128), lambda t, g: (l, g, 0, 0, 0)),
            pl.BlockSpec((gb, N_PS, CHUNK_WIDTH), lambda t, g: (g, 0, 0)),
        ],
        out_specs=[io_spec, pl.BlockSpec((None, gb, N_PS, CHUNK_WIDTH), lambda t, g: (t, g, 0, 0))],
        out_shape=[
            jax.ShapeDtypeStruct((N_TILES, SCAN_CHUNK, D_SSM, J_ROWS), F32),
            jax.ShapeDtypeStruct((N_TILES, N_GROUPS, N_PS, CHUNK_WIDTH), F32),
        ],
        scratch_shapes=[pltpu.VMEM((gb, J_ROWS, CHUNK_WIDTH), F32), pltpu.VMEM((gb, J_ROWS, CHUNK_WIDTH), F32),
                        pltpu.VMEM((gb, CHUNK_WIDTH, J_ROWS), F32)],
        compiler_params=_params("arbitrary", "arbitrary"),
        name="s5_chunk_scan",
    )(xt, wft, cct, ttt, a16, s0)


def _post_kernel(x_ref, yt_ref, ug_ref, vn_ref, mod_ref, wglu_ref, ws_ref, bs_ref, wout_f32_ref,
                 nw_ref, rwt_ref, rb_ref, hx_ref, route_ref, yg_scr, y_scr, wglut_ref, wout_ref):
    tile = pl.program_id(0)
    step = pl.program_id(1)

    @pl.when(_first_step())
    def _():
        wglut_ref[...] = wglu_ref[...].T.astype(BF16)
        wout_ref[...] = wout_f32_ref[...].astype(BF16)

    @pl.when(step == 0)
    def _():
        def chunk(n, carry):
            ps = n % N_PS
            c_hi = n // N_PS
            base = c_hi * (GMLP_CHUNK // SCAN_CHUNK) * N_PS + ps
            rows = [pl.ds(j * J_ROWS + base, GMLP_CHUNK // SCAN_CHUNK, stride=N_PS) for j in range(SCAN_CHUNK)]
            for h in range(GMLP_HEADS):
                v = jnp.concatenate([vn_ref[h, r, :] for r in rows], axis=0).astype(BF16)
                u = jnp.concatenate([ug_ref[h, r, :] for r in rows], axis=0)
                s = jnp.dot(ws_ref[h], v, preferred_element_type=F32) + bs_ref[h]
                yg = u * s
                for j, r in enumerate(rows):
                    yg_scr[h, r, :] = yg[j * 8:(j + 1) * 8]
            return carry

        lax.fori_loop(0, TILE_TOKENS // GMLP_CHUNK, chunk, 0)

    yt = _gelu_tanh(jnp.concatenate([yt_ref[k] for k in range(J_PER_STEP)], axis=1))
    yt = yt * _sigmoid(jnp.dot(wglut_ref[...], yt.astype(BF16), preferred_element_type=F32))
    for k in range(J_PER_STEP):
        y_scr[k * J_ROWS:(k + 1) * J_ROWS, :] = yt[:, k * J_ROWS:(k + 1) * J_ROWS].T
    row0 = pl.multiple_of(step * TOKEN_TILE, TOKEN_TILE)
    proj = jnp.dot(y_scr[...].astype(BF16), wout_ref[0:D_SSM, :], preferred_element_type=F32)
    yg = jnp.concatenate([yg_scr[h, pl.ds(row0, TOKEN_TILE), :] for h in range(GMLP_HEADS)], axis=1)
    proj += jnp.dot(yg.astype(BF16), wout_ref[D_SSM:, :], preferred_element_type=F32)
    x1 = x_ref[...] + _per_ps(lambda a, g: a * g, proj, mod_ref[2])
    h2 = _per_ps(lambda a, sc, sh: a * (1.0 + sc) + sh, _rmsnorm(x1, nw_ref[...]), mod_ref[4], mod_ref[3])
    hx_ref[:, 0:D_MODEL] = h2
    hx_ref[:, D_MODEL:2 * D_MODEL] = x1
    r_hi, r_lo = _split_bf16(rwt_ref[...])
    h_hi, h_lo = _split_bf16(h2)
    dot_t = lambda a, b: lax.dot_general(a, b, TRANS_B, preferred_element_type=F32)
    logits = dot_t(r_hi, h_hi) + (dot_t(r_hi, h_lo) + dot_t(r_lo, h_hi))
    scores = _sigmoid(logits)
    sel = scores + rb_ref[...]
    sc = [scores[e:e + 1, :] for e in range(N_EXPERTS)]
    sl = [sel[e:e + 1, :] for e in range(N_EXPERTS)]
    gscore = []
    for g in range(N_EXPERT_GROUPS):
        v0, v1, v2, v3 = sl[4 * g:4 * g + 4]
        hi01, lo01 = jnp.maximum(v0, v1), jnp.minimum(v0, v1)
        hi23, lo23 = jnp.maximum(v2, v3), jnp.minimum(v2, v3)
        top1 = jnp.maximum(hi01, hi23)
        top2 = jnp.maximum(jnp.minimum(hi01, hi23), jnp.maximum(lo01, lo23))
        gscore.append(top1 + top2)
    best = gscore[0]
    gidx = jnp.zeros_like(best, dtype=jnp.int32)
    for g in range(1, N_EXPERT_GROUPS):
        upd = gscore[g] > best
        gidx = jnp.where(upd, g, gidx)
        best = jnp.where(upd, gscore[g], best)

    def in_group(vals, k):
        out = vals[k]
        for g in range(1, N_EXPERT_GROUPS):
            out = jnp.where(gidx == g, vals[4 * g + k], out)
        return out

    v = [in_group(sl, k) for k in range(EXPERTS_PER_GROUP)]
    s = [in_group(sc, k) for k in range(EXPERTS_PER_GROUP)]
    w = []
    bits = jnp.zeros_like(gidx)
    for k in range(EXPERTS_PER_GROUP):
        rank = jnp.zeros_like(gidx)
        for j in range(EXPERTS_PER_GROUP):
            if j == k:
                continue
            ahead = (v[j] >= v[k]) if j < k else (v[j] > v[k])
            rank = rank + ahead.astype(jnp.int32)
        w.append(jnp.where(rank < 2, s[k], 0.0))
        bits = bits + jnp.where(rank < 2, 1 << k, 0)
    denom = (w[0] + w[1]) + (w[2] + w[3])
    gate = [wk / denom for wk in w]
    pair = jnp.full_like(gidx, len(PAIR_SLOT_A) - 1)
    for p in range(len(PAIR_SLOT_A) - 1):
        pair = jnp.where(bits == (1 << PAIR_SLOT_A[p]) + (1 << PAIR_SLOT_B[p]), p, pair)

    def slot_gate(table):
        out = gate[table[0]]
        for p in range(1, len(table)):
            out = jnp.where(pair == p, gate[table[p]], out)
        return out

    n_tok = route_ref.shape[1]
    route_ref[0:1, :] = (gidx * len(PAIR_SLOT_A) + pair).astype(F32)
    ps = lax.broadcasted_iota(jnp.int32, (1, n_tok), 1) % N_PS
    cond = jnp.where(tile < SAMPLE_TILE, 0, 1 + ps // SEGS_PER_SAMPLE_SEQ).astype(F32)
    lanes = jnp.concatenate([slot_gate(PAIR_SLOT_A), slot_gate(PAIR_SLOT_B), cond,
                             jnp.zeros((ROUTE_LANES - 3, n_tok), F32)], axis=0)
    hx_ref[:, 2 * D_MODEL:] = lanes.T


def _mix_out(x, yt, ug, vn, mod8, l, w_glu, w_s, b_s, w_out, norm2_w, rwt, rb):
    tm = TOKEN_TILE
    steps = TILE_TOKENS // tm
    tok = lambda n: pl.BlockSpec((tm, n), lambda t, s: (t * steps + s, 0))
    whole_tile = pl.BlockSpec((GMLP_HEADS, TILE_TOKENS, GMLP_HEAD_DIM), lambda t, s: (0, t, 0))
    lay = lambda *shape: pl.BlockSpec((None,) + shape, lambda t, s: (l,) + (0,) * len(shape))
    return pl.pallas_call(
        _post_kernel,
        grid=(N_TILES, steps),
        in_specs=[
            tok(D_MODEL),
            pl.BlockSpec((None, J_PER_STEP, D_SSM, J_ROWS), lambda t, s: (t, s, 0, 0)),
            whole_tile, whole_tile,
            pl.BlockSpec((None, None, N_MOD, N_PS, D_MODEL), lambda t, s: (l, t, 0, 0, 0)),
            lay(D_SSM, D_SSM), lay(GMLP_HEADS, GMLP_CHUNK, GMLP_CHUNK),
            lay(GMLP_HEADS, GMLP_CHUNK, 1), lay(D_MODEL, D_MODEL), lay(1, D_MODEL),
            pl.BlockSpec((N_EXPERTS, D_MODEL), lambda t, s: (0, 0)),
            pl.BlockSpec((N_EXPERTS, 1), lambda t, s: (0, 0)),
        ],
        out_specs=[tok(HX_WIDTH), pl.BlockSpec((1, tm), lambda t, s: (0, t * steps + s))],
        out_shape=[
            jax.ShapeDtypeStruct((T_ALL, HX_WIDTH), F32),
            jax.ShapeDtypeStruct((1, T_ALL), F32),
        ],
        scratch_shapes=[pltpu.VMEM((GMLP_HEADS, TILE_TOKENS, GMLP_HEAD_DIM), F32), pltpu.VMEM((tm, D_SSM), F32),
                        pltpu.VMEM((D_SSM, D_SSM), BF16), pltpu.VMEM((D_MODEL, D_MODEL), BF16)],
        compiler_params=_params("arbitrary", "arbitrary"),
        name="mixers_out_router",
    )(x, yt, ug, vn, mod8, w_glu, w_s, b_s, w_out, norm2_w.reshape(DEPTH, 1, D_MODEL), rwt, rb)


def _moe_kernel(ea_ref, eb_ref, nv_ref, size_ref, src_ref, drow_ref,
                hx_hbm, wga_ref, wua_ref, wda_ref, wgb_ref, wub_ref, wdb_ref, g2_ref, fw_ref,
                o_hbm, hx_buf, o_buf, z_buf, g_sem, s_sem, z_sem, *, final):
    n = pl.program_id(0)
    n_valid = nv_ref[0]
    slot = n % 2

    def by_size(tile, fn):
        for quarters in range(1, TAIL_STEPS + 1):
            @pl.when(size_ref[tile] == quarters)
            def _(rows=quarters * TAIL_QUARTER):
                fn(rows)

    def start_gather(tile, sl, rows):
        base = tile * MOE_TM
        for r in range(rows):
            pltpu.make_async_copy(hx_hbm.at[pl.ds(src_ref[base + r], 1)], hx_buf.at[sl, pl.ds(r, 1)],
                                  g_sem.at[sl]).start()

    def wait_gather(sl, rows):
        pltpu.make_async_copy(hx_hbm.at[pl.ds(0, rows)], hx_buf.at[sl, pl.ds(0, rows)], g_sem.at[sl]).wait()

    def start_scatter(tile, sl, rows):
        base = tile * MOE_TM
        for r in range(rows):
            pltpu.make_async_copy(o_buf.at[sl, pl.ds(r, 1)], o_hbm.at[pl.ds(drow_ref[base + r], 1)],
                                  s_sem.at[sl]).start()

    def wait_scatter(sl, rows):
        pltpu.make_async_copy(o_buf.at[sl, pl.ds(0, rows)], o_hbm.at[pl.ds(0, rows)], s_sem.at[sl]).wait()

    @pl.when(n == 0)
    def _():
        z_buf[...] = jnp.zeros_like(z_buf)
        spare = [pltpu.make_async_copy(z_buf, o_hbm.at[pl.ds(T_ALL + i * MOE_TM, MOE_TM)], z_sem)
                 for i in range(SPARE_ROWS // MOE_TM)]
        for cp in spare:
            cp.start()
        for cp in spare:
            cp.wait()
        by_size(0, lambda rows: start_gather(0, 0, rows))

    @pl.when(n + 1 < n_valid)
    def _():
        by_size(n + 1, lambda rows: start_gather(n + 1, 1 - slot, rows))

    def tile_step(rows):
        wait_gather(slot, rows)

        @pl.when(n >= 2)
        def _():
            by_size(n - 2, lambda r: wait_scatter(slot, r))

        h = hx_buf[slot, 0:rows, 0:D_MODEL].astype(BF16)
        lanes = hx_buf[slot, 0:rows, 2 * D_MODEL:]

        def expert(wg, wu, wd, gate):
            hg = jnp.dot(h, wg[...].astype(BF16), preferred_element_type=F32)
            hu = jnp.dot(h, wu[...].astype(BF16), preferred_element_type=F32)
            act = hg * _sigmoid(hg) * hu * gate
            return jnp.dot(act.astype(BF16), wd[...].astype(BF16), preferred_element_type=F32)

        y = expert(wga_ref, wua_ref, wda_ref, lanes[:, 0:1]) + expert(wgb_ref, wub_ref, wdb_ref, lanes[:, 1:2])
        cond_row = lanes[:, 2:3]
        gate2 = jnp.where(cond_row == 0.0, g2_ref[0:1, :], jnp.where(cond_row == 1.0, g2_ref[1:2, :], g2_ref[2:3, :]))
        x2 = hx_buf[slot, 0:rows, D_MODEL:2 * D_MODEL] + gate2 * y
        o_buf[slot, 0:rows] = _rmsnorm(x2, fw_ref[...]) if final else x2
        start_scatter(n, slot, rows)

    @pl.when(n < n_valid)
    def _():
        by_size(n, tile_step)

    @pl.when(n == MOE_TILES - 1)
    def _():
        last = n_valid - 1
        by_size(last, lambda r: wait_scatter(last % 2, r))
        by_size(last - 1, lambda r: wait_scatter(1 - last % 2, r))


def _experts(tables, hx, mods, l, w_gate, w_up, w_down, final_w, final):
    w_a = lambda r, c: pl.BlockSpec((None, None, r, c), lambda n, ea, eb, *_: (l, ea[n], 0, 0))
    w_b = lambda r, c: pl.BlockSpec((None, None, r, c), lambda n, ea, eb, *_: (l, eb[n], 0, 0))
    return pl.pallas_call(
        functools.partial(_moe_kernel, final=final),
        grid_spec=pltpu.PrefetchScalarGridSpec(
            num_scalar_prefetch=len(tables),
            grid=(MOE_TILES,),
            in_specs=[
                pl.BlockSpec(memory_space=pl.ANY),
                w_a(D_MODEL, D_EXPERT), w_a(D_MODEL, D_EXPERT), w_a(D_EXPERT, D_MODEL),
                w_b(D_MODEL, D_EXPERT), w_b(D_MODEL, D_EXPERT), w_b(D_EXPERT, D_MODEL),
                pl.BlockSpec((None, MOD_ROWS, D_MODEL), lambda n, *_: (l, 0, N_MOD - 1)),
                pl.BlockSpec((1, D_MODEL), lambda n, *_: (0, 0)),
            ],
            out_specs=pl.BlockSpec(memory_space=pl.ANY),
            scratch_shapes=[pltpu.VMEM((2, MOE_TM, HX_WIDTH), F32), pltpu.VMEM((2, MOE_TM, D_MODEL), F32),
                            pltpu.VMEM((MOE_TM, D_MODEL), F32),
                            pltpu.SemaphoreType.DMA((2,)), pltpu.SemaphoreType.DMA((2,)), pltpu.SemaphoreType.DMA(())],
        ),
        out_shape=jax.ShapeDtypeStruct((OUT_ROWS, D_MODEL), F32),
        compiler_params=_params("arbitrary"),
        name="experts",
    )(*tables, hx, w_gate, w_up, w_down, w_gate, w_up, w_down, mods, final_w.reshape(1, D_MODEL))


def _count_before(flags):
    n, k = flags.shape
    blocks = flags.reshape(n // 128, 128, k).astype(F32)
    strictly_lower = jnp.tril(jnp.ones((128, 128), F32), -1)
    within = jnp.einsum("ij,bjk->bik", strictly_lower, blocks)
    totals = jnp.sum(blocks, axis=1)
    before = jnp.cumsum(totals, axis=0) - totals
    return (within + before[:, None, :]).reshape(n, k).astype(jnp.int32)


def _routing_tables(cls, to_sequence_order):
    n_cls = N_EXPERT_GROUPS * len(PAIR_SLOT_A)
    onehot = (cls[:, None] == jnp.arange(n_cls, dtype=jnp.int32)[None, :]).astype(jnp.int32)
    counts = jnp.sum(onehot, axis=0)
    tiles = (counts + MOE_TM - 1) // MOE_TM
    tile_end = jnp.cumsum(tiles)
    row_in_class = (tile_end - tiles)[None, :] * MOE_TM + _count_before(onehot)
    dst = jnp.sum(onehot * row_in_class, axis=1)
    token_plus_1 = jnp.zeros((MOE_ROWS,), jnp.int32).at[dst].set(jnp.arange(1, T_ALL + 1, dtype=jnp.int32))
    is_pad = token_plus_1 == 0
    src = jnp.maximum(token_plus_1 - 1, 0)
    n_valid = tile_end[-1]
    tile_id = jnp.minimum(jnp.arange(MOE_TILES, dtype=jnp.int32), n_valid - 1)
    tile_cls = jnp.sum((tile_end[None, :] <= tile_id[:, None]).astype(jnp.int32), axis=1)
    group, pair = tile_cls // len(PAIR_SLOT_A), tile_cls % len(PAIR_SLOT_A)
    in_cls = (tile_cls[:, None] == jnp.arange(n_cls, dtype=jnp.int32)[None, :]).astype(jnp.int32)
    left = jnp.sum(in_cls * (counts[None, :] - (tile_id[:, None] - (tile_end - tiles)[None, :]) * MOE_TM), axis=1)
    size = jnp.clip((left + TAIL_QUARTER - 1) // TAIL_QUARTER, 1, TAIL_STEPS).astype(jnp.int32)
    row = jnp.arange(MOE_ROWS, dtype=jnp.int32)
    processed = jnp.logical_and(row % MOE_TM < jnp.repeat(size, MOE_TM) * TAIL_QUARTER, row // MOE_TM < n_valid)
    moved_pad = jnp.logical_and(is_pad, processed)
    spare = T_ALL + _count_before(moved_pad.astype(jnp.int32)[:, None])[:, 0]
    if to_sequence_order:
        j, c, ps = (src // J_ROWS) % SCAN_CHUNK, (src // N_PS) % SEG_CHUNKS, src % N_PS
        target = (src // TILE_TOKENS) * TILE_TOKENS + ps * SEG_LEN + c * SCAN_CHUNK + j
    else:
        target = src
    drow = jnp.where(is_pad, jnp.where(moved_pad, spare, T_ALL), target).astype(jnp.int32)

    def slot_expert(table):
        local = jnp.full_like(pair, table[-1])
        for p in range(len(table) - 1):
            local = jnp.where(pair == p, table[p], local)
        return group * EXPERTS_PER_GROUP + local

    e_a = slot_expert(PAIR_SLOT_A)
    e_b = slot_expert(PAIR_SLOT_B)
    return e_a, e_b, n_valid.reshape(1).astype(jnp.int32), size, src, drow


def _dirs_on_lanes(p):
    p = jnp.moveaxis(p, 1, -2)
    return p.reshape(p.shape[:-2] + (2 * SSM_STATE,))


def _to_internal_order(x):
    x = x.reshape(N_TILES, N_PS, SEG_CHUNKS, SCAN_CHUNK, D_MODEL)
    return x.transpose(0, 3, 2, 1, 4).reshape(T_ALL, D_MODEL)


def _gmlp_position_order(w):
    n_lo = GMLP_CHUNK // SCAN_CHUNK
    lead = w.shape[:2]
    w = w.reshape(lead + (n_lo, SCAN_CHUNK) + w.shape[3:])
    w = jnp.swapaxes(w, 2, 3)
    return w.reshape(lead + (GMLP_CHUNK,) + w.shape[4:])


def kernel(x_prompt, x_sample, c, state_ssm_re, state_ssm_im, c_ctx, norm1_w, norm2_w, w_mod, b_mod, w_in,
           ssm_a_re, ssm_a_im, ssm_log_dt, ssm_b_re, ssm_b_im, ssm_c_re, ssm_c_im, ssm_d, w_glu,
           gmlp_ln_w, gmlp_ln_b, gmlp_w_s, gmlp_b_s, w_out, router_w, router_b, w_gate, w_up, w_down,
           final_norm_w):
    x = jnp.concatenate([x_prompt.reshape(T_PROMPT, D_MODEL), x_sample.reshape(T_SAMPLE, D_MODEL)], axis=0)
    x = _to_internal_order(x)

    cvec = jnp.concatenate([c_ctx[None, :], c, jnp.zeros((MOD_ROWS - 1 - N_SAMPLE_SEQ, D_MODEL), F32)], axis=0)
    mods = _modulation(cvec, w_mod, b_mod)
    ps_row = [[0] * N_PS] * SAMPLE_TILE + [[1 + p // SEGS_PER_SAMPLE_SEQ for p in range(N_PS)]]
    mod8 = mods.reshape(DEPTH, MOD_ROWS, N_MOD, D_MODEL)[:, jnp.array(ps_row, jnp.int32)]
    mod8 = mod8.transpose(0, 1, 3, 2, 4)

    log_dt = jnp.broadcast_to(ssm_log_dt[..., None], ssm_a_re.shape)
    d_lanes = jnp.tile(ssm_d.reshape(DEPTH, N_GROUPS, SSM_GROUP), (1, 1, 128 // SSM_GROUP))
    vecs = jnp.stack([_dirs_on_lanes(ssm_a_re), _dirs_on_lanes(ssm_a_im), _dirs_on_lanes(log_dt), d_lanes], axis=2)
    vecs = jnp.concatenate([vecs, jnp.zeros((DEPTH, N_GROUPS, 4, 2 * SSM_STATE), F32)], axis=2)
    mats = jnp.stack([_dirs_on_lanes(jnp.swapaxes(ssm_b_re, -1, -2)), _dirs_on_lanes(jnp.swapaxes(ssm_b_im, -1, -2)),
                      _dirs_on_lanes(ssm_c_re), _dirs_on_lanes(ssm_c_im)], axis=2)
    wft, cct, ttt, a16 = _ssm_operators(vecs, mats)

    w_s = jnp.swapaxes(_gmlp_position_order(jnp.swapaxes(_gmlp_position_order(gmlp_w_s), 2, 3)), 2, 3).astype(BF16)
    b_s = _gmlp_position_order(gmlp_b_s)[..., None]
    rwt = router_w.T
    rb = router_b.reshape(N_EXPERTS, 1)

    new_re, new_im = [], []
    for l in range(DEPTH):
        ug, vn, xt = _input_proj(x, mod8, l, norm1_w, w_in, gmlp_ln_w, gmlp_ln_b)
        s0 = jnp.concatenate([state_ssm_re[:, l].transpose(2, 0, 1, 3).reshape(N_GROUPS, N_SAMPLE_SEQ, 128),
                              state_ssm_im[:, l].transpose(2, 0, 1, 3).reshape(N_GROUPS, N_SAMPLE_SEQ, 128)], axis=-1)
        s0 = jnp.repeat(s0, SEGS_PER_SAMPLE_SEQ, axis=1)
        yt, fin = _ssm_scan(xt, wft, cct, ttt, a16, s0, l)
        fin = fin[:SAMPLE_TILE].reshape(SAMPLE_TILE, N_GROUPS, N_PS, 2, 2, SSM_STATE)
        fin = fin.transpose(3, 0, 2, 4, 1, 5).reshape(2, N_PROMPT_SEQ, 2, N_GROUPS, SSM_STATE)
        new_re.append(fin[0])
        new_im.append(fin[1])
        hx, route = _mix_out(x, yt, ug, vn, mod8, l, w_glu, w_s, b_s, w_out, norm2_w, rwt, rb)
        final = l == DEPTH - 1
        tables = _routing_tables(route[0].astype(jnp.int32), to_sequence_order=final)
        x = _experts(tables, hx, mods, l, w_gate, w_up, w_down, final_norm_w, final)

    y_prompt = x[:T_PROMPT].reshape(N_PROMPT_SEQ, PROMPT_LEN, D_MODEL)
    y_sample = x[T_PROMPT:T_ALL].reshape(N_SAMPLE_SEQ, SAMPLE_LEN, D_MODEL)
    return (y_prompt, y_sample, jnp.stack(new_re, axis=1), jnp.stack(new_im, axis=1))
```

```python
import functools
import math

import jax
import jax.numpy as jnp
from jax import lax
from jax.experimental import pallas as pl
from jax.experimental.pallas import tpu as pltpu

F32 = jnp.float32
BF16 = jnp.bfloat16

D_MODEL = 1024
N_PROMPT_SEQ = 16
PROMPT_LEN = 256
N_SAMPLE_SEQ = 2
SAMPLE_LEN = 1024
T_PROMPT = N_PROMPT_SEQ * PROMPT_LEN
T_SAMPLE = N_SAMPLE_SEQ * SAMPLE_LEN
T_ALL = T_PROMPT + T_SAMPLE
DEPTH = 2
D_SSM = 512
SSM_GROUP = 16
N_GROUPS = 32
SSM_STATE = 64
D_GMLP = 512
GMLP_HEADS = 4
GMLP_HEAD_DIM = 128
GMLP_CHUNK = 128
N_EXPERTS = 16
N_EXPERT_GROUPS = 4
EXPERTS_PER_GROUP = 4
D_EXPERT = 512
N_MOD = 6
EPS = 1e-6

SCAN_CHUNK = 16
CHUNK_WIDTH = SCAN_CHUNK * SSM_GROUP
SEG_LEN = 256
SEG_CHUNKS = SEG_LEN // SCAN_CHUNK
N_PS = 8
TILE_TOKENS = N_PS * SEG_LEN
N_TILES = T_ALL // TILE_TOKENS
SAMPLE_TILE = T_PROMPT // TILE_TOKENS
SEGS_PER_SAMPLE_SEQ = SAMPLE_LEN // SEG_LEN
J_ROWS = SEG_CHUNKS * N_PS
GROUP_BLOCK = 8
PREP_GROUPS = 8
MOD_ROWS = 8
MOD_K_STEPS = 4

J_PER_STEP = 4
TOKEN_TILE = J_PER_STEP * J_ROWS
PAIR_SLOT_A = (0, 0, 0, 1, 1, 3)
PAIR_SLOT_B = (1, 2, 3, 3, 2, 2)
MOE_TM = 256
TAIL_STEPS = 4
TAIL_QUARTER = MOE_TM // TAIL_STEPS
N_CLASSES = N_EXPERT_GROUPS * len(PAIR_SLOT_A)
MOE_TILES = T_ALL // MOE_TM + N_CLASSES
MOE_ROWS = MOE_TILES * MOE_TM
SPARE_ROWS = N_CLASSES * TAIL_QUARTER + MOE_TM
OUT_ROWS = T_ALL + SPARE_ROWS
ROUTE_LANES = 128
HX_WIDTH = 2 * D_MODEL + ROUTE_LANES
VMEM_LIMIT = 56 * 1024 * 1024
TRANS_B = (((1,), (1,)), ((), ()))


def _sigmoid(x):
    return 1.0 / (1.0 + jnp.exp(-x))


def _gelu_tanh(x):
    c = math.sqrt(2.0 / math.pi)
    return x * (0.5 * (1.0 + jnp.tanh(c * (x + 0.044715 * (x * x * x)))))


def _split_bf16(a):
    hi = a.astype(BF16)
    return hi, (a - hi.astype(F32)).astype(BF16)


def _rmsnorm(x, w):
    return x * lax.rsqrt(jnp.mean(x * x, axis=-1, keepdims=True) + EPS) * w


def _per_ps(fn, a, *mods):
    rows, d = a.shape
    out = fn(a.reshape(rows // N_PS, N_PS, d), *[m[None] for m in mods])
    return out.reshape(rows, d)


def _params(*sem):
    return pltpu.CompilerParams(dimension_semantics=sem, vmem_limit_bytes=VMEM_LIMIT)


def _mod_kernel(c_ref, w_ref, b_ref, o_ref):
    c = c_ref[...]
    part = jnp.dot((c * _sigmoid(c)).astype(BF16), w_ref[...].astype(BF16), preferred_element_type=F32)

    @pl.when(pl.program_id(1) == 0)
    def _():
        o_ref[...] = part + b_ref[...]

    @pl.when(pl.program_id(1) > 0)
    def _():
        o_ref[...] += part


def _modulation(cvec, w_mod, b_mod):
    kb = D_MODEL // MOD_K_STEPS
    c_blocks = cvec.reshape(MOD_ROWS, MOD_K_STEPS, kb).transpose(1, 0, 2)
    return pl.pallas_call(
        _mod_kernel,
        grid=(DEPTH, MOD_K_STEPS),
        in_specs=[
            pl.BlockSpec((None, MOD_ROWS, kb), lambda l, k: (k, 0, 0)),
            pl.BlockSpec((None, kb, N_MOD * D_MODEL), lambda l, k: (l, k, 0)),
            pl.BlockSpec((None, 1, N_MOD * D_MODEL), lambda l, k: (l, 0, 0)),
        ],
        out_specs=pl.BlockSpec((None, MOD_ROWS, N_MOD * D_MODEL), lambda l, k: (l, 0, 0)),
        out_shape=jax.ShapeDtypeStruct((DEPTH, MOD_ROWS, N_MOD * D_MODEL), F32),
        compiler_params=_params("arbitrary", "arbitrary"),
        name="adaln_mod",
    )(c_blocks, w_mod, b_mod.reshape(DEPTH, 1, N_MOD * D_MODEL))


def _first_step():
    return jnp.logical_and(pl.program_id(0) == 0, pl.program_id(1) == 0)


def _in_kernel(x_ref, mod_ref, nw_ref, w_ref, lnw_ref, lnb_ref, ug_ref, vn_ref, xt_ref, wg_ref, wst_ref):
    @pl.when(_first_step())
    def _():
        wg_ref[...] = w_ref[:, D_SSM:].astype(BF16)
        wst_ref[...] = w_ref[:, :D_SSM].T.astype(BF16)

    y = _rmsnorm(x_ref[...], nw_ref[...])
    h = _per_ps(lambda a, sc, sh: a * (1.0 + sc) + sh, y, mod_ref[1], mod_ref[0]).astype(BF16)
    zg = _gelu_tanh(jnp.dot(h, wg_ref[...], preferred_element_type=F32))
    v = zg[:, D_GMLP:]
    mu = jnp.mean(v, axis=-1, keepdims=True)
    vc = v - mu
    var = jnp.mean(vc * vc, axis=-1, keepdims=True)
    vn = vc * lax.rsqrt(var + EPS) * lnw_ref[...] + lnb_ref[...]
    for hd in range(GMLP_HEADS):
        cols = slice(hd * GMLP_HEAD_DIM, (hd + 1) * GMLP_HEAD_DIM)
        ug_ref[hd] = zg[:, cols]
        vn_ref[hd] = vn[:, cols]
    xt = lax.dot_general(wst_ref[...], h, TRANS_B, preferred_element_type=F32).astype(BF16)
    for k in range(J_PER_STEP):
        xt_ref[k] = xt[:, k * J_ROWS:(k + 1) * J_ROWS]


def _input_proj(x, mod8, l, norm1_w, w_in, ln_w, ln_b):
    tm = TOKEN_TILE
    steps = TILE_TOKENS // tm
    tok = lambda n: pl.BlockSpec((tm, n), lambda t, s: (t * steps + s, 0))
    heads = pl.BlockSpec((GMLP_HEADS, tm, GMLP_HEAD_DIM), lambda t, s: (0, t * steps + s, 0))
    lay = lambda *shape: pl.BlockSpec((None,) + shape, lambda t, s: (l,) + (0,) * len(shape))
    return pl.pallas_call(
        _in_kernel,
        grid=(N_TILES, steps),
        in_specs=[
            tok(D_MODEL),
            pl.BlockSpec((None, None, N_MOD, N_PS, D_MODEL), lambda t, s: (l, t, 0, 0, 0)),
            lay(1, D_MODEL), lay(D_MODEL, D_SSM + 2 * D_GMLP), lay(1, D_GMLP), lay(1, D_GMLP),
        ],
        out_specs=[heads, heads,
                   pl.BlockSpec((None, J_PER_STEP, D_SSM, J_ROWS), lambda t, s: (t, s, 0, 0))],
        out_shape=[jax.ShapeDtypeStruct((GMLP_HEADS, T_ALL, GMLP_HEAD_DIM), F32)] * 2 + [
            jax.ShapeDtypeStruct((N_TILES, SCAN_CHUNK, D_SSM, J_ROWS), BF16)],
        scratch_shapes=[pltpu.VMEM((D_MODEL, 2 * D_GMLP), BF16), pltpu.VMEM((D_SSM, D_MODEL), BF16)],
        compiler_params=_params("arbitrary", "arbitrary"),
        name="norm1_in_proj",
    )(x, mod8, norm1_w.reshape(DEPTH, 1, D_MODEL), w_in,
      ln_w.reshape(DEPTH, 1, D_GMLP), ln_b.reshape(DEPTH, 1, D_GMLP))


def _shift_lanes_right(a, b, s, lane):
    if s == 0:
        return a, b
    if s == 128:
        return jnp.zeros_like(a), a
    if s < 128:
        ra = pltpu.roll(a, s, 1)
        rb = pltpu.roll(b, s, 1)
        return jnp.where(lane >= s, ra, 0.0), jnp.where(lane >= s, rb, ra)
    t = s - 128
    return jnp.zeros_like(a), jnp.where(lane >= t, pltpu.roll(a, t, 1), 0.0)


def _shift_lanes_left(a, b, s, lane):
    if s == 0:
        return a, b
    if s == 128:
        return b, jnp.zeros_like(b)
    if s < 128:
        ra = pltpu.roll(a, 128 - s, 1)
        rb = pltpu.roll(b, 128 - s, 1)
        return jnp.where(lane < 128 - s, ra, rb), jnp.where(lane < 128 - s, rb, 0.0)
    t = s - 128
    return jnp.where(lane < 128 - t, pltpu.roll(b, 128 - t, 1), 0.0), jnp.zeros_like(b)


def _prep_kernel(*refs):
    for g in range(PREP_GROUPS):
        _prep_group(*[r.at[g] for r in refs])


def _prep_group(vec_ref, mat_ref, wft_ref, cct_ref, ttt_ref, a16_ref, wf_scr, cm_scr, tt_scr):
    a_re = vec_ref[0:1, :]
    a_im = vec_ref[1:2, :]
    dt = jnp.exp(vec_ref[2:3, :])
    d_skip = vec_ref[3:4, :]
    mag = jnp.exp(a_re * dt)
    ang = a_im * dt
    ab_r = mag * jnp.cos(ang)
    ab_i = mag * jnp.sin(ang)
    den = a_re * a_re + a_im * a_im
    nr = ab_r - 1.0
    q_r = (nr * a_re + ab_i * a_im) / den
    q_i = (ab_i * a_re - nr * a_im) / den
    bt_r = mat_ref[0]
    bt_i = mat_ref[1]
    c_r = mat_ref[2]
    c_i = mat_ref[3]
    bb_r = q_r * bt_r - q_i * bt_i
    bb_i = q_r * bt_i + q_i * bt_r
    p_r = [jnp.ones_like(ab_r)]
    p_i = [jnp.zeros_like(ab_r)]
    for _ in range(SCAN_CHUNK):
        pr, pi = p_r[-1], p_i[-1]
        p_r.append(pr * ab_r - pi * ab_i)
        p_i.append(pr * ab_i + pi * ab_r)
    a16_ref[0] = jnp.broadcast_to(p_r[SCAN_CHUNK], (N_PS, 128))
    a16_ref[1] = jnp.broadcast_to(p_i[SCAN_CHUNK], (N_PS, 128))

    lane = lax.broadcasted_iota(jnp.int32, (1, 128), 1)
    is_fwd = lane < SSM_STATE

    def pick(mf, mb):
        return jnp.where(is_fwd, p_r[mf], p_r[mb]), jnp.where(is_fwd, p_i[mf], p_i[mb])

    for j in range(SCAN_CHUNK):
        rows = slice(j * SSM_GROUP, (j + 1) * SSM_GROUP)
        wr, wi = pick(SCAN_CHUNK - 1 - j, j)
        wf_scr[rows, 0:128] = bb_r * wr - bb_i * wi
        wf_scr[rows, 128:256] = bb_r * wi + bb_i * wr
        wr, wi = pick(j + 1, SCAN_CHUNK - j)
        cct_ref[rows, 0:128] = (c_r * wr - c_i * wi).astype(BF16)
        cct_ref[rows, 128:256] = (-(c_r * wi + c_i * wr)).astype(BF16)
        wr, wi = pick(j, SCAN_CHUNK - 1 - j)
        cm_scr[rows, 0:128] = c_r * wr - c_i * wi
        cm_scr[rows, 128:256] = c_r * wi + c_i * wr
    wft_ref[...] = wf_scr[...].T.astype(BF16)

    zero = jnp.zeros_like(bb_r)
    cm_hi, cm_lo = _split_bf16(cm_scr[...])
    dot_t = lambda a, b: lax.dot_general(a, b, TRANS_B, preferred_element_type=F32)

    def lag_rows(keep):
        lhs = jnp.concatenate([jnp.where(keep, bb_r, zero), jnp.where(keep, -bb_i, zero)], axis=1)
        hi, lo = _split_bf16(lhs)
        return dot_t(hi, cm_hi) + (dot_t(hi, cm_lo) + dot_t(lo, cm_hi))

    mf = lag_rows(is_fwd)
    mb = lag_rows(jnp.logical_not(is_fwd))
    mf_a, mf_b = mf[:, 0:128], mf[:, 128:256]
    mb_a, mb_b = mb[:, 0:128], mb[:, 128:256]
    row_h = lax.broadcasted_iota(jnp.int32, (SSM_GROUP, 128), 0)
    lane_h = lax.broadcasted_iota(jnp.int32, (SSM_GROUP, 128), 1)
    for jp in range(SCAN_CHUNK):
        rows = slice(jp * SSM_GROUP, (jp + 1) * SSM_GROUP)
        fa, fb = _shift_lanes_right(mf_a, mf_b, SSM_GROUP * jp, lane)
        ba, bb = _shift_lanes_left(mb_a, mb_b, SSM_GROUP * (SCAN_CHUNK - 1 - jp), lane)
        diag = SSM_GROUP * jp + row_h
        tt_scr[rows, 0:128] = fa + ba + jnp.where(lane_h == diag, d_skip, 0.0)
        tt_scr[rows, 128:256] = fb + bb + jnp.where(lane_h + 128 == diag, d_skip, 0.0)
    ttt_ref[...] = tt_scr[...].T.astype(BF16)


def _ssm_operators(vecs, mats):
    op = jax.ShapeDtypeStruct((DEPTH, N_GROUPS, CHUNK_WIDTH, CHUNK_WIDTH), BF16)
    pg = PREP_GROUPS
    op_spec = pl.BlockSpec((None, pg, CHUNK_WIDTH, CHUNK_WIDTH), lambda l, g: (l, g, 0, 0))
    sq = pltpu.VMEM((pg, CHUNK_WIDTH, CHUNK_WIDTH), F32)
    return pl.pallas_call(
        _prep_kernel,
        grid=(DEPTH, N_GROUPS // pg),
        in_specs=[
            pl.BlockSpec((None, pg, 8, 128), lambda l, g: (l, g, 0, 0)),
            pl.BlockSpec((None, pg, 4, SSM_GROUP, 128), lambda l, g: (l, g, 0, 0, 0)),
        ],
        out_specs=[op_spec, op_spec, op_spec,
                   pl.BlockSpec((None, pg, 2, N_PS, 128), lambda l, g: (l, g, 0, 0, 0))],
        out_shape=[op, op, op, jax.ShapeDtypeStruct((DEPTH, N_GROUPS, 2, N_PS, 128), F32)],
        scratch_shapes=[sq, sq, sq],
        compiler_params=_params("arbitrary", "arbitrary"),
        name="s5_operators",
    )(vecs, mats)


def _ssm_kernel(xt_ref, wft_ref, cct_ref, ttt_ref, a16_ref, s0_ref, yt_ref, fin_ref, s_scr, f_scr, ft_scr):
    tile = pl.program_id(0)
    lane = lax.broadcasted_iota(jnp.int32, (GROUP_BLOCK, N_PS, 128), 2)
    is_fwd = lane < SSM_STATE
    half = SSM_STATE

    def group_x(gl):
        return xt_ref[:, gl * SSM_GROUP:(gl + 1) * SSM_GROUP, :].reshape(CHUNK_WIDTH, J_ROWS)

    for gl in range(GROUP_BLOCK):
        ft_scr[gl] = jnp.dot(wft_ref[gl], group_x(gl), preferred_element_type=F32)
        f_scr[gl] = ft_scr[gl].T

    a_r = a16_ref[:, 0]
    a_i = a16_ref[:, 1]

    def scan(s_r, s_i):
        for i in range(SEG_CHUNKS):
            rf = slice(i * N_PS, (i + 1) * N_PS)
            rb = slice((SEG_CHUNKS - 1 - i) * N_PS, (SEG_CHUNKS - i) * N_PS)
            s_scr[:, rf, 0:half] = s_r[:, :, 0:half]
            s_scr[:, rf, 128:128 + half] = s_i[:, :, 0:half]
            s_scr[:, rb, half:128] = s_r[:, :, half:128]
            s_scr[:, rb, 128 + half:256] = s_i[:, :, half:128]
            f_r = jnp.where(is_fwd, f_scr[:, rf, 0:128], f_scr[:, rb, 0:128])
            f_i = jnp.where(is_fwd, f_scr[:, rf, 128:256], f_scr[:, rb, 128:256])
            s_r, s_i = a_r * s_r - a_i * s_i + f_r, a_r * s_i + a_i * s_r + f_i
        return s_r, s_i

    zeros = jnp.zeros((GROUP_BLOCK, N_PS, 128), F32)
    z_r, z_i = scan(zeros, zeros)
    fin_ref[:, :, 0:128] = z_r
    fin_ref[:, :, 128:256] = z_i

    @pl.when(tile == SAMPLE_TILE)
    def _():
        b_r, b_i = a_r, a_i
        for _ in range(4):
            b_r, b_i = b_r * b_r - b_i * b_i, 2.0 * (b_r * b_i)
        seg = lax.broadcasted_iota(jnp.int32, (GROUP_BLOCK, N_PS, 128), 1) % SEGS_PER_SAMPLE_SEQ
        i_r = s0_ref[:, :, 0:128]
        i_i = s0_ref[:, :, 128:256]
        for step in range(1, SEGS_PER_SAMPLE_SEQ):
            pr = jnp.where(is_fwd, pltpu.roll(i_r, 1, 1), pltpu.roll(i_r, N_PS - 1, 1))
            pi = jnp.where(is_fwd, pltpu.roll(i_i, 1, 1), pltpu.roll(i_i, N_PS - 1, 1))
            zr = jnp.where(is_fwd, pltpu.roll(z_r, 1, 1), pltpu.roll(z_r, N_PS - 1, 1))
            zi = jnp.where(is_fwd, pltpu.roll(z_i, 1, 1), pltpu.roll(z_i, N_PS - 1, 1))
            n_r = b_r * pr - b_i * pi + zr
            n_i = b_r * pi + b_i * pr + zi
            first = jnp.where(is_fwd, step, 0)
            last = jnp.where(is_fwd, SEGS_PER_SAMPLE_SEQ - 1, SEGS_PER_SAMPLE_SEQ - 1 - step)
            upd = jnp.logical_and(seg >= first, seg <= last)
            i_r = jnp.where(upd, n_r, i_r)
            i_i = jnp.where(upd, n_i, i_i)
        scan(i_r, i_i)

    for gl in range(GROUP_BLOCK):
        yt = jnp.dot(ttt_ref[gl], group_x(gl), preferred_element_type=F32)
        yt += lax.dot_general(cct_ref[gl], s_scr[gl].astype(BF16), TRANS_B, preferred_element_type=F32)
        yt_ref[:, gl * SSM_GROUP:(gl + 1) * SSM_GROUP, :] = yt.reshape(SCAN_CHUNK, SSM_GROUP, J_ROWS)


def _ssm_scan(xt, wft, cct, ttt, a16, s0, l):
    gb = GROUP_BLOCK
    op_spec = pl.BlockSpec((None, gb, CHUNK_WIDTH, CHUNK_WIDTH), lambda t, g: (l, g, 0, 0))
    io_spec = pl.BlockSpec((None, SCAN_CHUNK, gb * SSM_GROUP, J_ROWS), lambda t, g: (t, 0, g, 0))
    return pl.pallas_call(
        _ssm_kernel,
        grid=(N_TILES, N_GROUPS // gb),
        in_specs=[
            io_spec, op_spec, op_spec, op_spec,
            pl.BlockSpec((None, gb, 2, N_PS, 128), lambda t, g: (l, g, 0, 0, 0)),
            pl.BlockSpec((gb, N_PS, CHUNK_WIDTH), lambda t, g: (g, 0, 0)),
        ],
        out_specs=[io_spec, pl.BlockSpec((None, gb, N_PS, CHUNK_WIDTH), lambda t, g: (t, g, 0, 0))],
        out_shape=[
            jax.ShapeDtypeStruct((N_TILES, SCAN_CHUNK, D_SSM, J_ROWS), F32),
            jax.ShapeDtypeStruct((N_TILES, N_GROUPS, N_PS, CHUNK_WIDTH), F32),
        ],
        scratch_shapes=[pltpu.VMEM((gb, J_ROWS, CHUNK_WIDTH), F32), pltpu.VMEM((gb, J_ROWS, CHUNK_WIDTH), F32),
                        pltpu.VMEM((gb, CHUNK_WIDTH, J_ROWS), F32)],
        compiler_params=_params("arbitrary", "arbitrary"),
        name="s5_chunk_scan",
    )(xt, wft, cct, ttt, a16, s0)


def _post_kernel(x_ref, yt_ref, ug_ref, vn_ref, mod_ref, wglu_ref, ws_ref, bs_ref, wout_f32_ref,
                 nw_ref, rwt_ref, rb_ref, hx_ref, route_ref, yg_scr, y_scr, wglut_ref, wout_ref):
    tile = pl.program_id(0)
    step = pl.program_id(1)

    @pl.when(_first_step())
    def _():
        wglut_ref[...] = wglu_ref[...].T.astype(BF16)
        wout_ref[...] = wout_f32_ref[...].astype(BF16)

    @pl.when(step == 0)
    def _():
        def chunk(n, carry):
            ps = n % N_PS
            c_hi = n // N_PS
            base = c_hi * (GMLP_CHUNK // SCAN_CHUNK) * N_PS + ps
            rows = [pl.ds(j * J_ROWS + base, GMLP_CHUNK // SCAN_CHUNK, stride=N_PS) for j in range(SCAN_CHUNK)]
            for h in range(GMLP_HEADS):
                v = jnp.concatenate([vn_ref[h, r, :] for r in rows], axis=0).astype(BF16)
                u = jnp.concatenate([ug_ref[h, r, :] for r in rows], axis=0)
                s = jnp.dot(ws_ref[h], v, preferred_element_type=F32) + bs_ref[h]
                yg = u * s
                for j, r in enumerate(rows):
                    yg_scr[h, r, :] = yg[j * 8:(j + 1) * 8]
            return carry

        lax.fori_loop(0, TILE_TOKENS // GMLP_CHUNK, chunk, 0)

    yt = _gelu_tanh(jnp.concatenate([yt_ref[k] for k in range(J_PER_STEP)], axis=1))
    yt = yt * _sigmoid(jnp.dot(wglut_ref[...], yt.astype(BF16), preferred_element_type=F32))
    for k in range(J_PER_STEP):
        y_scr[k * J_ROWS:(k + 1) * J_ROWS, :] = yt[:, k * J_ROWS:(k + 1) * J_ROWS].T
    row0 = pl.multiple_of(step * TOKEN_TILE, TOKEN_TILE)
    proj = jnp.dot(y_scr[...].astype(BF16), wout_ref[0:D_SSM, :], preferred_element_type=F32)
    yg = jnp.concatenate([yg_scr[h, pl.ds(row0, TOKEN_TILE), :] for h in range(GMLP_HEADS)], axis=1)
    proj += jnp.dot(yg.astype(BF16), wout_ref[D_SSM:, :], preferred_element_type=F32)
    x1 = x_ref[...] + _per_ps(lambda a, g: a * g, proj, mod_ref[2])
    h2 = _per_ps(lambda a, sc, sh: a * (1.0 + sc) + sh, _rmsnorm(x1, nw_ref[...]), mod_ref[4], mod_ref[3])
    hx_ref[:, 0:D_MODEL] = h2
    hx_ref[:, D_MODEL:2 * D_MODEL] = x1
    r_hi, r_lo = _split_bf16(rwt_ref[...])
    h_hi, h_lo = _split_bf16(h2)
    dot_t = lambda a, b: lax.dot_general(a, b, TRANS_B, preferred_element_type=F32)
    logits = dot_t(r_hi, h_hi) + (dot_t(r_hi, h_lo) + dot_t(r_lo, h_hi))
    scores = _sigmoid(logits)
    sel = scores + rb_ref[...]
    sc = [scores[e:e + 1, :] for e in range(N_EXPERTS)]
    sl = [sel[e:e + 1, :] for e in range(N_EXPERTS)]
    gscore = []
    for g in range(N_EXPERT_GROUPS):
        v0, v1, v2, v3 = sl[4 * g:4 * g + 4]
        hi01, lo01 = jnp.maximum(v0, v1), jnp.minimum(v0, v1)
        hi23, lo23 = jnp.maximum(v2, v3), jnp.minimum(v2, v3)
        top1 = jnp.maximum(hi01, hi23)
        top2 = jnp.maximum(jnp.minimum(hi01, hi23), jnp.maximum(lo01, lo23))
        gscore.append(top1 + top2)
    best = gscore[0]
    gidx = jnp.zeros_like(best, dtype=jnp.int32)
    for g in range(1, N_EXPERT_GROUPS):
        upd = gscore[g] > best
        gidx = jnp.where(upd, g, gidx)
        best = jnp.where(upd, gscore[g], best)

    def in_group(vals, k):
        out = vals[k]
        for g in range(1, N_EXPERT_GROUPS):
            out = jnp.where(gidx == g, vals[4 * g + k], out)
        return out

    v = [in_group(sl, k) for k in range(EXPERTS_PER_GROUP)]
    s = [in_group(sc, k) for k in range(EXPERTS_PER_GROUP)]
    w = []
    bits = jnp.zeros_like(gidx)
    for k in range(EXPERTS_PER_GROUP):
        rank = jnp.zeros_like(gidx)
        for j in range(EXPERTS_PER_GROUP):
            if j == k:
                continue
            ahead = (v[j] >= v[k]) if j < k else (v[j] > v[k])
            rank = rank + ahead.astype(jnp.int32)
        w.append(jnp.where(rank < 2, s[k], 0.0))
        bits = bits + jnp.where(rank < 2, 1 << k, 0)
    denom = (w[0] + w[1]) + (w[2] + w[3])
    gate = [wk / denom for wk in w]
    pair = jnp.full_like(gidx, len(PAIR_SLOT_A) - 1)
    for p in range(len(PAIR_SLOT_A) - 1):
        pair = jnp.where(bits == (1 << PAIR_SLOT_A[p]) + (1 << PAIR_SLOT_B[p]), p, pair)

    def slot_gate(table):
        out = gate[table[0]]
        for p in range(1, len(table)):
            out = jnp.where(pair == p, gate[table[p]], out)
        return out

    n_tok = route_ref.shape[1]
    route_ref[0:1, :] = (gidx * len(PAIR_SLOT_A) + pair).astype(F32)
    ps = lax.broadcasted_iota(jnp.int32, (1, n_tok), 1) % N_PS
    cond = jnp.where(tile < SAMPLE_TILE, 0, 1 + ps // SEGS_PER_SAMPLE_SEQ).astype(F32)
    lanes = jnp.concatenate([slot_gate(PAIR_SLOT_A), slot_gate(PAIR_SLOT_B), cond,
                             jnp.zeros((ROUTE_LANES - 3, n_tok), F32)], axis=0)
    hx_ref[:, 2 * D_MODEL:] = lanes.T


def _mix_out(x, yt, ug, vn, mod8, l, w_glu, w_s, b_s, w_out, norm2_w, rwt, rb):
    tm = TOKEN_TILE
    steps = TILE_TOKENS // tm
    tok = lambda n: pl.BlockSpec((tm, n), lambda t, s: (t * steps + s, 0))
    whole_tile = pl.BlockSpec((GMLP_HEADS, TILE_TOKENS, GMLP_HEAD_DIM), lambda t, s: (0, t, 0))
    lay = lambda *shape: pl.BlockSpec((None,) + shape, lambda t, s: (l,) + (0,) * len(shape))
    return pl.pallas_call(
        _post_kernel,
        grid=(N_TILES, steps),
        in_specs=[
            tok(D_MODEL),
            pl.BlockSpec((None, J_PER_STEP, D_SSM, J_ROWS), lambda t, s: (t, s, 0, 0)),
            whole_tile, whole_tile,
            pl.BlockSpec((None, None, N_MOD, N_PS, D_MODEL), lambda t, s: (l, t, 0, 0, 0)),
            lay(D_SSM, D_SSM), lay(GMLP_HEADS, GMLP_CHUNK, GMLP_CHUNK),
            lay(GMLP_HEADS, GMLP_CHUNK, 1), lay(D_MODEL, D_MODEL), lay(1, D_MODEL),
            pl.BlockSpec((N_EXPERTS, D_MODEL), lambda t, s: (0, 0)),
            pl.BlockSpec((N_EXPERTS, 1), lambda t, s: (0, 0)),
        ],
        out_specs=[tok(HX_WIDTH), pl.BlockSpec((1, tm), lambda t, s: (0, t * steps + s))],
        out_shape=[
            jax.ShapeDtypeStruct((T_ALL, HX_WIDTH), F32),
            jax.ShapeDtypeStruct((1, T_ALL), F32),
        ],
        scratch_shapes=[pltpu.VMEM((GMLP_HEADS, TILE_TOKENS, GMLP_HEAD_DIM), F32), pltpu.VMEM((tm, D_SSM), F32),
                        pltpu.VMEM((D_SSM, D_SSM), BF16), pltpu.VMEM((D_MODEL, D_MODEL), BF16)],
        compiler_params=_params("arbitrary", "arbitrary"),
        name="mixers_out_router",
    )(x, yt, ug, vn, mod8, w_glu, w_s, b_s, w_out, norm2_w.reshape(DEPTH, 1, D_MODEL), rwt, rb)


def _moe_kernel(ea_ref, eb_ref, nv_ref, size_ref, src_ref, drow_ref,
                hx_hbm, wga_ref, wua_ref, wda_ref, wgb_ref, wub_ref, wdb_ref, g2_ref, fw_ref,
                o_hbm, hx_buf, o_buf, z_buf, g_sem, s_sem, z_sem, *, final):
    n = pl.program_id(0)
    n_valid = nv_ref[0]
    slot = n % 2

    def by_size(tile, fn):
        for quarters in range(1, TAIL_STEPS + 1):
            @pl.when(size_ref[tile] == quarters)
            def _(rows=quarters * TAIL_QUARTER):
                fn(rows)

    def start_gather(tile, sl, rows):
        base = tile * MOE_TM
        for r in range(rows):
            pltpu.make_async_copy(hx_hbm.at[pl.ds(src_ref[base + r], 1)], hx_buf.at[sl, pl.ds(r, 1)],
                                  g_sem.at[sl]).start()

    def wait_gather(sl, rows):
        pltpu.make_async_copy(hx_hbm.at[pl.ds(0, rows)], hx_buf.at[sl, pl.ds(0, rows)], g_sem.at[sl]).wait()

    def start_scatter(tile, sl, rows):
        base = tile * MOE_TM
        for r in range(rows):
            pltpu.make_async_copy(o_buf.at[sl, pl.ds(r, 1)], o_hbm.at[pl.ds(drow_ref[base + r], 1)],
                                  s_sem.at[sl]).start()

    def wait_scatter(sl, rows):
        pltpu.make_async_copy(o_buf.at[sl, pl.ds(0, rows)], o_hbm.at[pl.ds(0, rows)], s_sem.at[sl]).wait()

    @pl.when(n == 0)
    def _():
        z_buf[...] = jnp.zeros_like(z_buf)
        spare = [pltpu.make_async_copy(z_buf, o_hbm.at[pl.ds(T_ALL + i * MOE_TM, MOE_TM)], z_sem)
                 for i in range(SPARE_ROWS // MOE_TM)]
        for cp in spare:
            cp.start()
        for cp in spare:
            cp.wait()
        start_gather(0, 0, MOE_TM)

    @pl.when(n + 1 < n_valid)
    def _():
        by_size(n + 1, lambda rows: start_gather(n + 1, 1 - slot, rows))

    def tile_step(rows):
        wait_gather(slot, rows)

        @pl.when(n >= 2)
        def _():
            by_size(n - 2, lambda r: wait_scatter(slot, r))

        h = hx_buf[slot, 0:rows, 0:D_MODEL].astype(BF16)
        lanes = hx_buf[slot, 0:rows, 2 * D_MODEL:]

        def expert(wg, wu, wd, gate):
            hg = jnp.dot(h, wg[...].astype(BF16), preferred_element_type=F32)
            hu = jnp.dot(h, wu[...].astype(BF16), preferred_element_type=F32)
            act = hg * _sigmoid(hg) * hu * gate
            return jnp.dot(act.astype(BF16), wd[...].astype(BF16), preferred_element_type=F32)

        y = expert(wga_ref, wua_ref, wda_ref, lanes[:, 0:1]) + expert(wgb_ref, wub_ref, wdb_ref, lanes[:, 1:2])
        cond_row = lanes[:, 2:3]
        gate2 = jnp.where(cond_row == 0.0, g2_ref[0:1, :], jnp.where(cond_row == 1.0, g2_ref[1:2, :], g2_ref[2:3, :]))
        x2 = hx_buf[slot, 0:rows, D_MODEL:2 * D_MODEL] + gate2 * y
        o_buf[slot, 0:rows] = _rmsnorm(x2, fw_ref[...]) if final else x2
        start_scatter(n, slot, rows)

    @pl.when(n < n_valid)
    def _():
        by_size(n, tile_step)

    @pl.when(n == MOE_TILES - 1)
    def _():
        last = n_valid - 1
        by_size(last, lambda r: wait_scatter(last % 2, r))
        by_size(last - 1, lambda r: wait_scatter(1 - last % 2, r))


def _experts(tables, hx, mods, l, w_gate, w_up, w_down, final_w, final):
    w_a = lambda r, c: pl.BlockSpec((None, None, r, c), lambda n, ea, eb, *_: (l, ea[n], 0, 0))
    w_b = lambda r, c: pl.BlockSpec((None, None, r, c), lambda n, ea, eb, *_: (l, eb[n], 0, 0))
    return pl.pallas_call(
        functools.partial(_moe_kernel, final=final),
        grid_spec=pltpu.PrefetchScalarGridSpec(
            num_scalar_prefetch=len(tables),
            grid=(MOE_TILES,),
            in_specs=[
                pl.BlockSpec(memory_space=pl.ANY),
                w_a(D_MODEL, D_EXPERT), w_a(D_MODEL, D_EXPERT), w_a(D_EXPERT, D_MODEL),
                w_b(D_MODEL, D_EXPERT), w_b(D_MODEL, D_EXPERT), w_b(D_EXPERT, D_MODEL),
                pl.BlockSpec((None, MOD_ROWS, D_MODEL), lambda n, *_: (l, 0, N_MOD - 1)),
                pl.BlockSpec((1, D_MODEL), lambda n, *_: (0, 0)),
            ],
            out_specs=pl.BlockSpec(memory_space=pl.ANY),
            scratch_shapes=[pltpu.VMEM((2, MOE_TM, HX_WIDTH), F32), pltpu.VMEM((2, MOE_TM, D_MODEL), F32),
                            pltpu.VMEM((MOE_TM, D_MODEL), F32),
                            pltpu.SemaphoreType.DMA((2,)), pltpu.SemaphoreType.DMA((2,)), pltpu.SemaphoreType.DMA(())],
        ),
        out_shape=jax.ShapeDtypeStruct((OUT_ROWS, D_MODEL), F32),
        compiler_params=_params("arbitrary"),
        name="experts",
    )(*tables, hx, w_gate, w_up, w_down, w_gate, w_up, w_down, mods, final_w.reshape(1, D_MODEL))


def _count_before(flags):
    n, k = flags.shape
    blocks = flags.reshape(n // 128, 128, k).astype(F32)
    strictly_lower = jnp.tril(jnp.ones((128, 128), F32), -1)
    within = jnp.einsum("ij,bjk->bik", strictly_lower, blocks)
    totals = jnp.sum(blocks, axis=1)
    before = jnp.cumsum(totals, axis=0) - totals
    return (within + before[:, None, :]).reshape(n, k).astype(jnp.int32)


def _routing_tables(cls, to_sequence_order):
    n_cls = N_EXPERT_GROUPS * len(PAIR_SLOT_A)
    onehot = (cls[:, None] == jnp.arange(n_cls, dtype=jnp.int32)[None, :]).astype(jnp.int32)
    counts = jnp.sum(onehot, axis=0)
    tiles = (counts + MOE_TM - 1) // MOE_TM
    tile_end = jnp.cumsum(tiles)
    row_in_class = (tile_end - tiles)[None, :] * MOE_TM + _count_before(onehot)
    dst = jnp.sum(onehot * row_in_class, axis=1)
    token_plus_1 = jnp.zeros((MOE_ROWS,), jnp.int32).at[dst].set(jnp.arange(1, T_ALL + 1, dtype=jnp.int32))
    is_pad = token_plus_1 == 0
    src = jnp.maximum(token_plus_1 - 1, 0)
    n_valid = tile_end[-1]
    tile_id = jnp.minimum(jnp.arange(MOE_TILES, dtype=jnp.int32), n_valid - 1)
    tile_cls = jnp.sum((tile_end[None, :] <= tile_id[:, None]).astype(jnp.int32), axis=1)
    group, pair = tile_cls // len(PAIR_SLOT_A), tile_cls % len(PAIR_SLOT_A)
    in_cls = (tile_cls[:, None] == jnp.arange(n_cls, dtype=jnp.int32)[None, :]).astype(jnp.int32)
    left = jnp.sum(in_cls * (counts[None, :] - (tile_id[:, None] - (tile_end - tiles)[None, :]) * MOE_TM), axis=1)
    size = jnp.clip((left + TAIL_QUARTER - 1) // TAIL_QUARTER, 1, TAIL_STEPS).astype(jnp.int32)
    size = size.at[0].set(TAIL_STEPS)
    row = jnp.arange(MOE_ROWS, dtype=jnp.int32)
    processed = jnp.logical_and(row % MOE_TM < jnp.repeat(size, MOE_TM) * TAIL_QUARTER, row // MOE_TM < n_valid)
    moved_pad = jnp.logical_and(is_pad, processed)
    spare = T_ALL + _count_before(moved_pad.astype(jnp.int32)[:, None])[:, 0]
    if to_sequence_order:
        j, c, ps = (src // J_ROWS) % SCAN_CHUNK, (src // N_PS) % SEG_CHUNKS, src % N_PS
        target = (src // TILE_TOKENS) * TILE_TOKENS + ps * SEG_LEN + c * SCAN_CHUNK + j
    else:
        target = src
    drow = jnp.where(is_pad, jnp.where(moved_pad, spare, T_ALL), target).astype(jnp.int32)

    def slot_expert(table):
        local = jnp.full_like(pair, table[-1])
        for p in range(len(table) - 1):
            local = jnp.where(pair == p, table[p], local)
        return group * EXPERTS_PER_GROUP + local

    e_a = slot_expert(PAIR_SLOT_A)
    e_b = slot_expert(PAIR_SLOT_B)
    return e_a, e_b, n_valid.reshape(1).astype(jnp.int32), size, src, drow


def _dirs_on_lanes(p):
    p = jnp.moveaxis(p, 1, -2)
    return p.reshape(p.shape[:-2] + (2 * SSM_STATE,))


def _to_internal_order(x):
    x = x.reshape(N_TILES, N_PS, SEG_CHUNKS, SCAN_CHUNK, D_MODEL)
    return x.transpose(0, 3, 2, 1, 4).reshape(T_ALL, D_MODEL)


def _gmlp_position_order(w):
    n_lo = GMLP_CHUNK // SCAN_CHUNK
    lead = w.shape[:2]
    w = w.reshape(lead + (n_lo, SCAN_CHUNK) + w.shape[3:])
    w = jnp.swapaxes(w, 2, 3)
    return w.reshape(lead + (GMLP_CHUNK,) + w.shape[4:])


def kernel(x_prompt, x_sample, c, state_ssm_re, state_ssm_im, c_ctx, norm1_w, norm2_w, w_mod, b_mod, w_in,
           ssm_a_re, ssm_a_im, ssm_log_dt, ssm_b_re, ssm_b_im, ssm_c_re, ssm_c_im, ssm_d, w_glu,
           gmlp_ln_w, gmlp_ln_b, gmlp_w_s, gmlp_b_s, w_out, router_w, router_b, w_gate, w_up, w_down,
           final_norm_w):
    x = jnp.concatenate([x_prompt.reshape(T_PROMPT, D_MODEL), x_sample.reshape(T_SAMPLE, D_MODEL)], axis=0)
    x = _to_internal_order(x)

    cvec = jnp.concatenate([c_ctx[None, :], c, jnp.zeros((MOD_ROWS - 1 - N_SAMPLE_SEQ, D_MODEL), F32)], axis=0)
    mods = _modulation(cvec, w_mod, b_mod)
    ps_row = [[0] * N_PS] * SAMPLE_TILE + [[1 + p // SEGS_PER_SAMPLE_SEQ for p in range(N_PS)]]
    mod8 = mods.reshape(DEPTH, MOD_ROWS, N_MOD, D_MODEL)[:, jnp.array(ps_row, jnp.int32)]
    mod8 = mod8.transpose(0, 1, 3, 2, 4)

    log_dt = jnp.broadcast_to(ssm_log_dt[..., None], ssm_a_re.shape)
    d_lanes = jnp.tile(ssm_d.reshape(DEPTH, N_GROUPS, SSM_GROUP), (1, 1, 128 // SSM_GROUP))
    vecs = jnp.stack([_dirs_on_lanes(ssm_a_re), _dirs_on_lanes(ssm_a_im), _dirs_on_lanes(log_dt), d_lanes], axis=2)
    vecs = jnp.concatenate([vecs, jnp.zeros((DEPTH, N_GROUPS, 4, 2 * SSM_STATE), F32)], axis=2)
    mats = jnp.stack([_dirs_on_lanes(jnp.swapaxes(ssm_b_re, -1, -2)), _dirs_on_lanes(jnp.swapaxes(ssm_b_im, -1, -2)),
                      _dirs_on_lanes(ssm_c_re), _dirs_on_lanes(ssm_c_im)], axis=2)
    wft, cct, ttt, a16 = _ssm_operators(vecs, mats)

    w_s = jnp.swapaxes(_gmlp_position_order(jnp.swapaxes(_gmlp_position_order(gmlp_w_s), 2, 3)), 2, 3).astype(BF16)
    b_s = _gmlp_position_order(gmlp_b_s)[..., None]
    rwt = router_w.T
    rb = router_b.reshape(N_EXPERTS, 1)

    new_re, new_im = [], []
    for l in range(DEPTH):
        ug, vn, xt = _input_proj(x, mod8, l, norm1_w, w_in, gmlp_ln_w, gmlp_ln_b)
        s0 = jnp.concatenate([state_ssm_re[:, l].transpose(2, 0, 1, 3).reshape(N_GROUPS, N_SAMPLE_SEQ, 128),
                              state_ssm_im[:, l].transpose(2, 0, 1, 3).reshape(N_GROUPS, N_SAMPLE_SEQ, 128)], axis=-1)
        s0 = jnp.repeat(s0, SEGS_PER_SAMPLE_SEQ, axis=1)
        yt, fin = _ssm_scan(xt, wft, cct, ttt, a16, s0, l)
        fin = fin[:SAMPLE_TILE].reshape(SAMPLE_TILE, N_GROUPS, N_PS, 2, 2, SSM_STATE)
        fin = fin.transpose(3, 0, 2, 4, 1, 5).reshape(2, N_PROMPT_SEQ, 2, N_GROUPS, SSM_STATE)
        new_re.append(fin[0])
        new_im.append(fin[1])
        hx, route = _mix_out(x, yt, ug, vn, mod8, l, w_glu, w_s, b_s, w_out, norm2_w, rwt, rb)
        final = l == DEPTH - 1
        tables = _routing_tables(route[0].astype(jnp.int32), to_sequence_order=final)
        x = _experts(tables, hx, mods, l, w_gate, w_up, w_down, final_norm_w, final)

    y_prompt = x[:T_PROMPT].reshape(N_PROMPT_SEQ, PROMPT_LEN, D_MODEL)
    y_sample = x[T_PROMPT:T_ALL].reshape(N_SAMPLE_SEQ, SAMPLE_LEN, D_MODEL)
    return (y_prompt, y_sample, jnp.stack(new_re, axis=1), jnp.stack(new_im, axis=1))
```

```python
import functools
import math

import jax
import jax.numpy as jnp
from jax import lax
from jax.experimental import pallas as pl
from jax.experimental.pallas import tpu as pltpu

F32 = jnp.float32
BF16 = jnp.bfloat16

D_MODEL = 1024
N_PROMPT_SEQ = 16
PROMPT_LEN = 256
N_SAMPLE_SEQ = 2
SAMPLE_LEN = 1024
T_PROMPT = N_PROMPT_SEQ * PROMPT_LEN
T_SAMPLE = N_SAMPLE_SEQ * SAMPLE_LEN
T_ALL = T_PROMPT + T_SAMPLE
DEPTH = 2
D_SSM = 512
SSM_GROUP = 16
N_GROUPS = 32
SSM_STATE = 64
D_GMLP = 512
GMLP_HEADS = 4
GMLP_HEAD_DIM = 128
GMLP_CHUNK = 128
N_EXPERTS = 16
N_EXPERT_GROUPS = 4
EXPERTS_PER_GROUP = 4
D_EXPERT = 512
N_MOD = 6
EPS = 1e-6

SCAN_CHUNK = 16
CHUNK_WIDTH = SCAN_CHUNK * SSM_GROUP
SEG_LEN = 256
SEG_CHUNKS = SEG_LEN // SCAN_CHUNK
N_PS = 8
TILE_TOKENS = N_PS * SEG_LEN
N_TILES = T_ALL // TILE_TOKENS
SAMPLE_TILE = T_PROMPT // TILE_TOKENS
SEGS_PER_SAMPLE_SEQ = SAMPLE_LEN // SEG_LEN
J_ROWS = SEG_CHUNKS * N_PS
GROUP_BLOCK = 8
PREP_GROUPS = 8
MOD_ROWS = 8
MOD_K_STEPS = 4

J_PER_STEP = 4
TOKEN_TILE = J_PER_STEP * J_ROWS
PAIR_SLOT_A = (0, 0, 0, 1, 1, 3)
PAIR_SLOT_B = (1, 2, 3, 3, 2, 2)
MOE_TM = 256
TAIL_STEPS = 4
TAIL_QUARTER = MOE_TM // TAIL_STEPS
N_CLASSES = N_EXPERT_GROUPS * len(PAIR_SLOT_A)
MOE_TILES = T_ALL // MOE_TM + N_CLASSES
MOE_ROWS = MOE_TILES * MOE_TM
SPARE_ROWS = N_CLASSES * TAIL_QUARTER
OUT_ROWS = T_ALL + SPARE_ROWS
ROUTE_LANES = 128
HX_WIDTH = 2 * D_MODEL + ROUTE_LANES
VMEM_LIMIT = 56 * 1024 * 1024
TRANS_B = (((1,), (1,)), ((), ()))


def _sigmoid(x):
    return 1.0 / (1.0 + jnp.exp(-x))


def _gelu_tanh(x):
    c = math.sqrt(2.0 / math.pi)
    return x * (0.5 * (1.0 + jnp.tanh(c * (x + 0.044715 * (x * x * x)))))


def _split_bf16(a):
    hi = a.astype(BF16)
    return hi, (a - hi.astype(F32)).astype(BF16)


def _rmsnorm(x, w):
    return x * lax.rsqrt(jnp.mean(x * x, axis=-1, keepdims=True) + EPS) * w


def _per_ps(fn, a, *mods):
    rows, d = a.shape
    out = fn(a.reshape(rows // N_PS, N_PS, d), *[m[None] for m in mods])
    return out.reshape(rows, d)


def _params(*sem):
    return pltpu.CompilerParams(dimension_semantics=sem, vmem_limit_bytes=VMEM_LIMIT)


def _mod_kernel(c_ref, w_ref, b_ref, o_ref):
    c = c_ref[...]
    part = jnp.dot((c * _sigmoid(c)).astype(BF16), w_ref[...].astype(BF16), preferred_element_type=F32)

    @pl.when(pl.program_id(1) == 0)
    def _():
        o_ref[...] = part + b_ref[...]

    @pl.when(pl.program_id(1) > 0)
    def _():
        o_ref[...] += part


def _modulation(cvec, w_mod, b_mod):
    kb = D_MODEL // MOD_K_STEPS
    c_blocks = cvec.reshape(MOD_ROWS, MOD_K_STEPS, kb).transpose(1, 0, 2)
    return pl.pallas_call(
        _mod_kernel,
        grid=(DEPTH, MOD_K_STEPS),
        in_specs=[
            pl.BlockSpec((None, MOD_ROWS, kb), lambda l, k: (k, 0, 0)),
            pl.BlockSpec((None, kb, N_MOD * D_MODEL), lambda l, k: (l, k, 0)),
            pl.BlockSpec((None, 1, N_MOD * D_MODEL), lambda l, k: (l, 0, 0)),
        ],
        out_specs=pl.BlockSpec((None, MOD_ROWS, N_MOD * D_MODEL), lambda l, k: (l, 0, 0)),
        out_shape=jax.ShapeDtypeStruct((DEPTH, MOD_ROWS, N_MOD * D_MODEL), F32),
        compiler_params=_params("arbitrary", "arbitrary"),
        name="adaln_mod",
    )(c_blocks, w_mod, b_mod.reshape(DEPTH, 1, N_MOD * D_MODEL))


def _first_step():
    return jnp.logical_and(pl.program_id(0) == 0, pl.program_id(1) == 0)


def _in_kernel(x_ref, mod_ref, nw_ref, w_ref, lnw_ref, lnb_ref, ug_ref, vn_ref, xt_ref, wg_ref, wst_ref):
    @pl.when(_first_step())
    def _():
        wg_ref[...] = w_ref[:, D_SSM:].astype(BF16)
        wst_ref[...] = w_ref[:, :D_SSM].T.astype(BF16)

    y = _rmsnorm(x_ref[...], nw_ref[...])
    h = _per_ps(lambda a, sc, sh: a * (1.0 + sc) + sh, y, mod_ref[1], mod_ref[0]).astype(BF16)
    zg = _gelu_tanh(jnp.dot(h, wg_ref[...], preferred_element_type=F32))
    v = zg[:, D_GMLP:]
    mu = jnp.mean(v, axis=-1, keepdims=True)
    vc = v - mu
    var = jnp.mean(vc * vc, axis=-1, keepdims=True)
    vn = vc * lax.rsqrt(var + EPS) * lnw_ref[...] + lnb_ref[...]
    for hd in range(GMLP_HEADS):
        cols = slice(hd * GMLP_HEAD_DIM, (hd + 1) * GMLP_HEAD_DIM)
        ug_ref[hd] = zg[:, cols]
        vn_ref[hd] = vn[:, cols]
    xt = lax.dot_general(wst_ref[...], h, TRANS_B, preferred_element_type=F32).astype(BF16)
    for k in range(J_PER_STEP):
        xt_ref[k] = xt[:, k * J_ROWS:(k + 1) * J_ROWS]


def _input_proj(x, mod8, l, norm1_w, w_in, ln_w, ln_b):
    tm = TOKEN_TILE
    steps = TILE_TOKENS // tm
    tok = lambda n: pl.BlockSpec((tm, n), lambda t, s: (t * steps + s, 0))
    heads = pl.BlockSpec((GMLP_HEADS, tm, GMLP_HEAD_DIM), lambda t, s: (0, t * steps + s, 0))
    lay = lambda *shape: pl.BlockSpec((None,) + shape, lambda t, s: (l,) + (0,) * len(shape))
    return pl.pallas_call(
        _in_kernel,
        grid=(N_TILES, steps),
        in_specs=[
            tok(D_MODEL),
            pl.BlockSpec((None, None, N_MOD, N_PS, D_MODEL), lambda t, s: (l, t, 0, 0, 0)),
            lay(1, D_MODEL), lay(D_MODEL, D_SSM + 2 * D_GMLP), lay(1, D_GMLP), lay(1, D_GMLP),
        ],
        out_specs=[heads, heads,
                   pl.BlockSpec((None, J_PER_STEP, D_SSM, J_ROWS), lambda t, s: (t, s, 0, 0))],
        out_shape=[jax.ShapeDtypeStruct((GMLP_HEADS, T_ALL, GMLP_HEAD_DIM), F32)] * 2 + [
            jax.ShapeDtypeStruct((N_TILES, SCAN_CHUNK, D_SSM, J_ROWS), BF16)],
        scratch_shapes=[pltpu.VMEM((D_MODEL, 2 * D_GMLP), BF16), pltpu.VMEM((D_SSM, D_MODEL), BF16)],
        compiler_params=_params("arbitrary", "arbitrary"),
        name="norm1_in_proj",
    )(x, mod8, norm1_w.reshape(DEPTH, 1, D_MODEL), w_in,
      ln_w.reshape(DEPTH, 1, D_GMLP), ln_b.reshape(DEPTH, 1, D_GMLP))


def _shift_lanes_right(a, b, s, lane):
    if s == 0:
        return a, b
    if s == 128:
        return jnp.zeros_like(a), a
    if s < 128:
        ra = pltpu.roll(a, s, 1)
        rb = pltpu.roll(b, s, 1)
        return jnp.where(lane >= s, ra, 0.0), jnp.where(lane >= s, rb, ra)
    t = s - 128
    return jnp.zeros_like(a), jnp.where(lane >= t, pltpu.roll(a, t, 1), 0.0)


def _shift_lanes_left(a, b, s, lane):
    if s == 0:
        return a, b
    if s == 128:
        return b, jnp.zeros_like(b)
    if s < 128:
        ra = pltpu.roll(a, 128 - s, 1)
        rb = pltpu.roll(b, 128 - s, 1)
        return jnp.where(lane < 128 - s, ra, rb), jnp.where(lane < 128 - s, rb, 0.0)
    t = s - 128
    return jnp.where(lane < 128 - t, pltpu.roll(b, 128 - t, 1), 0.0), jnp.zeros_like(b)


def _prep_kernel(*refs):
    for g in range(PREP_GROUPS):
        _prep_group(*[r.at[g] for r in refs])


def _prep_group(vec_ref, mat_ref, wft_ref, cct_ref, ttt_ref, a16_ref, wf_scr, cm_scr, tt_scr):
    a_re = vec_ref[0:1, :]
    a_im = vec_ref[1:2, :]
    dt = jnp.exp(vec_ref[2:3, :])
    d_skip = vec_ref[3:4, :]
    mag = jnp.exp(a_re * dt)
    ang = a_im * dt
    ab_r = mag * jnp.cos(ang)
    ab_i = mag * jnp.sin(ang)
    den = a_re * a_re + a_im * a_im
    nr = ab_r - 1.0
    q_r = (nr * a_re + ab_i * a_im) / den
    q_i = (ab_i * a_re - nr * a_im) / den
    bt_r = mat_ref[0]
    bt_i = mat_ref[1]
    c_r = mat_ref[2]
    c_i = mat_ref[3]
    bb_r = q_r * bt_r - q_i * bt_i
    bb_i = q_r * bt_i + q_i * bt_r
    p_r = [jnp.ones_like(ab_r)]
    p_i = [jnp.zeros_like(ab_r)]
    for _ in range(SCAN_CHUNK):
        pr, pi = p_r[-1], p_i[-1]
        p_r.append(pr * ab_r - pi * ab_i)
        p_i.append(pr * ab_i + pi * ab_r)
    a16_ref[0] = jnp.broadcast_to(p_r[SCAN_CHUNK], (N_PS, 128))
    a16_ref[1] = jnp.broadcast_to(p_i[SCAN_CHUNK], (N_PS, 128))

    lane = lax.broadcasted_iota(jnp.int32, (1, 128), 1)
    is_fwd = lane < SSM_STATE

    def pick(mf, mb):
        return jnp.where(is_fwd, p_r[mf], p_r[mb]), jnp.where(is_fwd, p_i[mf], p_i[mb])

    for j in range(SCAN_CHUNK):
        rows = slice(j * SSM_GROUP, (j + 1) * SSM_GROUP)
        wr, wi = pick(SCAN_CHUNK - 1 - j, j)
        wf_scr[rows, 0:128] = bb_r * wr - bb_i * wi
        wf_scr[rows, 128:256] = bb_r * wi + bb_i * wr
        wr, wi = pick(j + 1, SCAN_CHUNK - j)
        cct_ref[rows, 0:128] = (c_r * wr - c_i * wi).astype(BF16)
        cct_ref[rows, 128:256] = (-(c_r * wi + c_i * wr)).astype(BF16)
        wr, wi = pick(j, SCAN_CHUNK - 1 - j)
        cm_scr[rows, 0:128] = c_r * wr - c_i * wi
        cm_scr[rows, 128:256] = c_r * wi + c_i * wr
    wft_ref[...] = wf_scr[...].T.astype(BF16)

    zero = jnp.zeros_like(bb_r)
    cm_hi, cm_lo = _split_bf16(cm_scr[...])
    dot_t = lambda a, b: lax.dot_general(a, b, TRANS_B, preferred_element_type=F32)

    def lag_rows(keep):
        lhs = jnp.concatenate([jnp.where(keep, bb_r, zero), jnp.where(keep, -bb_i, zero)], axis=1)
        hi, lo = _split_bf16(lhs)
        return dot_t(hi, cm_hi) + (dot_t(hi, cm_lo) + dot_t(lo, cm_hi))

    mf = lag_rows(is_fwd)
    mb = lag_rows(jnp.logical_not(is_fwd))
    mf_a, mf_b = mf[:, 0:128], mf[:, 128:256]
    mb_a, mb_b = mb[:, 0:128], mb[:, 128:256]
    row_h = lax.broadcasted_iota(jnp.int32, (SSM_GROUP, 128), 0)
    lane_h = lax.broadcasted_iota(jnp.int32, (SSM_GROUP, 128), 1)
    for jp in range(SCAN_CHUNK):
        rows = slice(jp * SSM_GROUP, (jp + 1) * SSM_GROUP)
        fa, fb = _shift_lanes_right(mf_a, mf_b, SSM_GROUP * jp, lane)
        ba, bb = _shift_lanes_left(mb_a, mb_b, SSM_GROUP * (SCAN_CHUNK - 1 - jp), lane)
        diag = SSM_GROUP * jp + row_h
        tt_scr[rows, 0:128] = fa + ba + jnp.where(lane_h == diag, d_skip, 0.0)
        tt_scr[rows, 128:256] = fb + bb + jnp.where(lane_h + 128 == diag, d_skip, 0.0)
    ttt_ref[...] = tt_scr[...].T.astype(BF16)


def _ssm_operators(vecs, mats):
    op = jax.ShapeDtypeStruct((DEPTH, N_GROUPS, CHUNK_WIDTH, CHUNK_WIDTH), BF16)
    pg = PREP_GROUPS
    op_spec = pl.BlockSpec((None, pg, CHUNK_WIDTH, CHUNK_WIDTH), lambda l, g: (l, g, 0, 0))
    sq = pltpu.VMEM((pg, CHUNK_WIDTH, CHUNK_WIDTH), F32)
    return pl.pallas_call(
        _prep_kernel,
        grid=(DEPTH, N_GROUPS // pg),
        in_specs=[
            pl.BlockSpec((None, pg, 8, 128), lambda l, g: (l, g, 0, 0)),
            pl.BlockSpec((None, pg, 4, SSM_GROUP, 128), lambda l, g: (l, g, 0, 0, 0)),
        ],
        out_specs=[op_spec, op_spec, op_spec,
                   pl.BlockSpec((None, pg, 2, N_PS, 128), lambda l, g: (l, g, 0, 0, 0))],
        out_shape=[op, op, op, jax.ShapeDtypeStruct((DEPTH, N_GROUPS, 2, N_PS, 128), F32)],
        scratch_shapes=[sq, sq, sq],
        compiler_params=_params("arbitrary", "arbitrary"),
        name="s5_operators",
    )(vecs, mats)


def _ssm_kernel(xt_ref, wft_ref, cct_ref, ttt_ref, a16_ref, s0_ref, yt_ref, fin_ref, s_scr, f_scr, ft_scr):
    tile = pl.program_id(0)
    lane = lax.broadcasted_iota(jnp.int32, (GROUP_BLOCK, N_PS, 128), 2)
    is_fwd = lane < SSM_STATE
    half = SSM_STATE

    def group_x(gl):
        return xt_ref[:, gl * SSM_GROUP:(gl + 1) * SSM_GROUP, :].reshape(CHUNK_WIDTH, J_ROWS)

    for gl in range(GROUP_BLOCK):
        ft_scr[gl] = jnp.dot(wft_ref[gl], group_x(gl), preferred_element_type=F32)
        f_scr[gl] = ft_scr[gl].T

    a_r = a16_ref[:, 0]
    a_i = a16_ref[:, 1]

    def scan(s_r, s_i):
        for i in range(SEG_CHUNKS):
            rf = slice(i * N_PS, (i + 1) * N_PS)
            rb = slice((SEG_CHUNKS - 1 - i) * N_PS, (SEG_CHUNKS - i) * N_PS)
            s_scr[:, rf, 0:half] = s_r[:, :, 0:half]
            s_scr[:, rf, 128:128 + half] = s_i[:, :, 0:half]
            s_scr[:, rb, half:128] = s_r[:, :, half:128]
            s_scr[:, rb, 128 + half:256] = s_i[:, :, half:128]
            f_r = jnp.where(is_fwd, f_scr[:, rf, 0:128], f_scr[:, rb, 0:128])
            f_i = jnp.where(is_fwd, f_scr[:, rf, 128:256], f_scr[:, rb, 128:256])
            s_r, s_i = a_r * s_r - a_i * s_i + f_r, a_r * s_i + a_i * s_r + f_i
        return s_r, s_i

    zeros = jnp.zeros((GROUP_BLOCK, N_PS, 128), F32)
    z_r, z_i = scan(zeros, zeros)
    fin_ref[:, :, 0:128] = z_r
    fin_ref[:, :, 128:256] = z_i

    @pl.when(tile == SAMPLE_TILE)
    def _():
        b_r, b_i = a_r, a_i
        for _ in range(4):
            b_r, b_i = b_r * b_r - b_i * b_i, 2.0 * (b_r * b_i)
        seg = lax.broadcasted_iota(jnp.int32, (GROUP_BLOCK, N_PS, 128), 1) % SEGS_PER_SAMPLE_SEQ
        i_r = s0_ref[:, :, 0:128]
        i_i = s0_ref[:, :, 128:256]
        for step in range(1, SEGS_PER_SAMPLE_SEQ):
            pr = jnp.where(is_fwd, pltpu.roll(i_r, 1, 1), pltpu.roll(i_r, N_PS - 1, 1))
            pi = jnp.where(is_fwd, pltpu.roll(i_i, 1, 1), pltpu.roll(i_i, N_PS - 1, 1))
            zr = jnp.where(is_fwd, pltpu.roll(z_r, 1, 1), pltpu.roll(z_r, N_PS - 1, 1))
            zi = jnp.where(is_fwd, pltpu.roll(z_i, 1, 1), pltpu.roll(z_i, N_PS - 1, 1))
            n_r = b_r * pr - b_i * pi + zr
            n_i = b_r * pi + b_i * pr + zi
            first = jnp.where(is_fwd, step, 0)
            last = jnp.where(is_fwd, SEGS_PER_SAMPLE_SEQ - 1, SEGS_PER_SAMPLE_SEQ - 1 - step)
            upd = jnp.logical_and(seg >= first, seg <= last)
            i_r = jnp.where(upd, n_r, i_r)
            i_i = jnp.where(upd, n_i, i_i)
        scan(i_r, i_i)

    for gl in range(GROUP_BLOCK):
        yt = jnp.dot(ttt_ref[gl], group_x(gl), preferred_element_type=F32)
        yt += lax.dot_general(cct_ref[gl], s_scr[gl].astype(BF16), TRANS_B, preferred_element_type=F32)
        yt_ref[:, gl * SSM_GROUP:(gl + 1) * SSM_GROUP, :] = yt.reshape(SCAN_CHUNK, SSM_GROUP, J_ROWS)


def _ssm_scan(xt, wft, cct, ttt, a16, s0, l):
    gb = GROUP_BLOCK
    op_spec = pl.BlockSpec((None, gb, CHUNK_WIDTH, CHUNK_WIDTH), lambda t, g: (l, g, 0, 0))
    io_spec = pl.BlockSpec((None, SCAN_CHUNK, gb * SSM_GROUP, J_ROWS), lambda t, g: (t, 0, g, 0))
    return pl.pallas_call(
        _ssm_kernel,
        grid=(N_TILES, N_GROUPS // gb),
        in_specs=[
            io_spec, op_spec, op_spec, op_spec,
            pl.BlockSpec((None, gb, 2, N_PS, 128), lambda t, g: (l, g, 0, 0, 0)),
            pl.BlockSpec((gb, N_PS, CHUNK_WIDTH), lambda t, g: (g, 0, 0)),
        ],
        out_specs=[io_spec, pl.BlockSpec((None, gb, N_PS, CHUNK_WIDTH), lambda t, g: (t, g, 0, 0))],
        out_shape=[
            jax.ShapeDtypeStruct((N_TILES, SCAN_CHUNK, D_SSM, J_ROWS), F32),
            jax.ShapeDtypeStruct((N_TILES, N_GROUPS, N_PS, CHUNK_WIDTH), F32),
        ],
        scratch_shapes=[pltpu.VMEM((gb, J_ROWS, CHUNK_WIDTH), F32), pltpu.VMEM((gb, J_ROWS, CHUNK_WIDTH), F32),
                        pltpu.VMEM((gb, CHUNK_WIDTH, J_ROWS), F32)],
        compiler_params=_params("arbitrary", "arbitrary"),
        name="s5_chunk_scan",
    )(xt, wft, cct, ttt, a16, s0)


def _post_kernel(x_ref, yt_ref, ug_ref, vn_ref, mod_ref, wglu_ref, ws_ref, bs_ref, wout_f32_ref,
                 nw_ref, rwt_ref, rb_ref, hx_ref, route_ref, yg_scr, y_scr, wglut_ref, wout_ref):
    tile = pl.program_id(0)
    step = pl.program_id(1)

    @pl.when(_first_step())
    def _():
        wglut_ref[...] = wglu_ref[...].T.astype(BF16)
        wout_ref[...] = wout_f32_ref[...].astype(BF16)

    @pl.when(step == 0)
    def _():
        def chunk(n, carry):
            ps = n % N_PS
            c_hi = n // N_PS
            base = c_hi * (GMLP_CHUNK // SCAN_CHUNK) * N_PS + ps
            rows = [pl.ds(j * J_ROWS + base, GMLP_CHUNK // SCAN_CHUNK, stride=N_PS) for j in range(SCAN_CHUNK)]
            for h in range(GMLP_HEADS):
                v = jnp.concatenate([vn_ref[h, r, :] for r in rows], axis=0).astype(BF16)
                u = jnp.concatenate([ug_ref[h, r, :] for r in rows], axis=0)
                s = jnp.dot(ws_ref[h], v, preferred_element_type=F32) + bs_ref[h]
                yg = u * s
                for j, r in enumerate(rows):
                    yg_scr[h, r, :] = yg[j * 8:(j + 1) * 8]
            return carry

        for n in range(TILE_TOKENS // GMLP_CHUNK):
            chunk(n, 0)

    yt = _gelu_tanh(jnp.concatenate([yt_ref[k] for k in range(J_PER_STEP)], axis=1))
    yt = yt * _sigmoid(jnp.dot(wglut_ref[...], yt.astype(BF16), preferred_element_type=F32))
    for k in range(J_PER_STEP):
        y_scr[k * J_ROWS:(k + 1) * J_ROWS, :] = yt[:, k * J_ROWS:(k + 1) * J_ROWS].T
    row0 = pl.multiple_of(step * TOKEN_TILE, TOKEN_TILE)
    proj = jnp.dot(y_scr[...].astype(BF16), wout_ref[0:D_SSM, :], preferred_element_type=F32)
    yg = jnp.concatenate([yg_scr[h, pl.ds(row0, TOKEN_TILE), :] for h in range(GMLP_HEADS)], axis=1)
    proj += jnp.dot(yg.astype(BF16), wout_ref[D_SSM:, :], preferred_element_type=F32)
    x1 = x_ref[...] + _per_ps(lambda a, g: a * g, proj, mod_ref[2])
    h2 = _per_ps(lambda a, sc, sh: a * (1.0 + sc) + sh, _rmsnorm(x1, nw_ref[...]), mod_ref[4], mod_ref[3])
    hx_ref[:, 0:D_MODEL] = h2
    hx_ref[:, D_MODEL:2 * D_MODEL] = x1
    r_hi, r_lo = _split_bf16(rwt_ref[...])
    h_hi, h_lo = _split_bf16(h2)
    dot_t = lambda a, b: lax.dot_general(a, b, TRANS_B, preferred_element_type=F32)
    logits = dot_t(r_hi, h_hi) + (dot_t(r_hi, h_lo) + dot_t(r_lo, h_hi))
    scores = _sigmoid(logits)
    sel = scores + rb_ref[...]
    sc = [scores[e:e + 1, :] for e in range(N_EXPERTS)]
    sl = [sel[e:e + 1, :] for e in range(N_EXPERTS)]
    gscore = []
    for g in range(N_EXPERT_GROUPS):
        v0, v1, v2, v3 = sl[4 * g:4 * g + 4]
        hi01, lo01 = jnp.maximum(v0, v1), jnp.minimum(v0, v1)
        hi23, lo23 = jnp.maximum(v2, v3), jnp.minimum(v2, v3)
        top1 = jnp.maximum(hi01, hi23)
        top2 = jnp.maximum(jnp.minimum(hi01, hi23), jnp.maximum(lo01, lo23))
        gscore.append(top1 + top2)
    best = gscore[0]
    gidx = jnp.zeros_like(best, dtype=jnp.int32)
    for g in range(1, N_EXPERT_GROUPS):
        upd = gscore[g] > best
        gidx = jnp.where(upd, g, gidx)
        best = jnp.where(upd, gscore[g], best)

    def in_group(vals, k):
        out = vals[k]
        for g in range(1, N_EXPERT_GROUPS):
            out = jnp.where(gidx == g, vals[4 * g + k], out)
        return out

    v = [in_group(sl, k) for k in range(EXPERTS_PER_GROUP)]
    s = [in_group(sc, k) for k in range(EXPERTS_PER_GROUP)]
    w = []
    bits = jnp.zeros_like(gidx)
    for k in range(EXPERTS_PER_GROUP):
        rank = jnp.zeros_like(gidx)
        for j in range(EXPERTS_PER_GROUP):
            if j == k:
                continue
            ahead = (v[j] >= v[k]) if j < k else (v[j] > v[k])
            rank = rank + ahead.astype(jnp.int32)
        w.append(jnp.where(rank < 2, s[k], 0.0))
        bits = bits + jnp.where(rank < 2, 1 << k, 0)
    denom = (w[0] + w[1]) + (w[2] + w[3])
    gate = [wk / denom for wk in w]
    pair = jnp.full_like(gidx, len(PAIR_SLOT_A) - 1)
    for p in range(len(PAIR_SLOT_A) - 1):
        pair = jnp.where(bits == (1 << PAIR_SLOT_A[p]) + (1 << PAIR_SLOT_B[p]), p, pair)

    def slot_gate(table):
        out = gate[table[0]]
        for p in range(1, len(table)):
            out = jnp.where(pair == p, gate[table[p]], out)
        return out

    n_tok = route_ref.shape[1]
    route_ref[0:1, :] = (gidx * len(PAIR_SLOT_A) + pair).astype(F32)
    ps = lax.broadcasted_iota(jnp.int32, (1, n_tok), 1) % N_PS
    cond = jnp.where(tile < SAMPLE_TILE, 0, 1 + ps // SEGS_PER_SAMPLE_SEQ).astype(F32)
    lanes = jnp.concatenate([slot_gate(PAIR_SLOT_A), slot_gate(PAIR_SLOT_B), cond,
                             jnp.zeros((ROUTE_LANES - 3, n_tok), F32)], axis=0)
    hx_ref[:, 2 * D_MODEL:] = lanes.T


def _mix_out(x, yt, ug, vn, mod8, l, w_glu, w_s, b_s, w_out, norm2_w, rwt, rb):
    tm = TOKEN_TILE
    steps = TILE_TOKENS // tm
    tok = lambda n: pl.BlockSpec((tm, n), lambda t, s: (t * steps + s, 0))
    whole_tile = pl.BlockSpec((GMLP_HEADS, TILE_TOKENS, GMLP_HEAD_DIM), lambda t, s: (0, t, 0))
    lay = lambda *shape: pl.BlockSpec((None,) + shape, lambda t, s: (l,) + (0,) * len(shape))
    return pl.pallas_call(
        _post_kernel,
        grid=(N_TILES, steps),
        in_specs=[
            tok(D_MODEL),
            pl.BlockSpec((None, J_PER_STEP, D_SSM, J_ROWS), lambda t, s: (t, s, 0, 0)),
            whole_tile, whole_tile,
            pl.BlockSpec((None, None, N_MOD, N_PS, D_MODEL), lambda t, s: (l, t, 0, 0, 0)),
            lay(D_SSM, D_SSM), lay(GMLP_HEADS, GMLP_CHUNK, GMLP_CHUNK),
            lay(GMLP_HEADS, GMLP_CHUNK, 1), lay(D_MODEL, D_MODEL), lay(1, D_MODEL),
            pl.BlockSpec((N_EXPERTS, D_MODEL), lambda t, s: (0, 0)),
            pl.BlockSpec((N_EXPERTS, 1), lambda t, s: (0, 0)),
        ],
        out_specs=[tok(HX_WIDTH), pl.BlockSpec((1, tm), lambda t, s: (0, t * steps + s))],
        out_shape=[
            jax.ShapeDtypeStruct((T_ALL, HX_WIDTH), F32),
            jax.ShapeDtypeStruct((1, T_ALL), F32),
        ],
        scratch_shapes=[pltpu.VMEM((GMLP_HEADS, TILE_TOKENS, GMLP_HEAD_DIM), F32), pltpu.VMEM((tm, D_SSM), F32),
                        pltpu.VMEM((D_SSM, D_SSM), BF16), pltpu.VMEM((D_MODEL, D_MODEL), BF16)],
        compiler_params=_params("arbitrary", "arbitrary"),
        name="mixers_out_router",
    )(x, yt, ug, vn, mod8, w_glu, w_s, b_s, w_out, norm2_w.reshape(DEPTH, 1, D_MODEL), rwt, rb)


def _moe_kernel(ea_ref, eb_ref, nv_ref, size_ref, src_ref, drow_ref,
                hx_hbm, wga_ref, wua_ref, wda_ref, wgb_ref, wub_ref, wdb_ref, g2_ref, fw_ref,
                o_hbm, hx_buf, o_buf, z_buf, g_sem, s_sem, z_sem, *, final):
    n = pl.program_id(0)
    n_valid = nv_ref[0]
    slot = n % 2

    def by_size(tile, fn):
        for quarters in range(1, TAIL_STEPS + 1):
            @pl.when(size_ref[tile] == quarters)
            def _(rows=quarters * TAIL_QUARTER):
                fn(rows)

    def start_gather(tile, sl, rows):
        base = tile * MOE_TM
        for r in range(rows):
            pltpu.make_async_copy(hx_hbm.at[pl.ds(src_ref[base + r], 1)], hx_buf.at[sl, pl.ds(r, 1)],
                                  g_sem.at[sl]).start()

    def wait_gather(sl, rows):
        pltpu.make_async_copy(hx_hbm.at[pl.ds(0, rows)], hx_buf.at[sl, pl.ds(0, rows)], g_sem.at[sl]).wait()

    def start_scatter(tile, sl, rows):
        base = tile * MOE_TM
        for r in range(rows):
            pltpu.make_async_copy(o_buf.at[sl, pl.ds(r, 1)], o_hbm.at[pl.ds(drow_ref[base + r], 1)],
                                  s_sem.at[sl]).start()

    def wait_scatter(sl, rows):
        pltpu.make_async_copy(o_buf.at[sl, pl.ds(0, rows)], o_hbm.at[pl.ds(0, rows)], s_sem.at[sl]).wait()

    @pl.when(n == 0)
    def _():
        z_buf[...] = jnp.zeros_like(z_buf)
        spare = [pltpu.make_async_copy(z_buf, o_hbm.at[pl.ds(T_ALL + i * MOE_TM, MOE_TM)], z_sem)
                 for i in range(SPARE_ROWS // MOE_TM)]
        for cp in spare:
            cp.start()
        for cp in spare:
            cp.wait()
        by_size(0, lambda rows: start_gather(0, 0, rows))

    @pl.when(n + 1 < n_valid)
    def _():
        by_size(n + 1, lambda rows: start_gather(n + 1, 1 - slot, rows))

    def tile_step(rows):
        wait_gather(slot, rows)

        @pl.when(n >= 2)
        def _():
            by_size(n - 2, lambda r: wait_scatter(slot, r))

        h = hx_buf[slot, 0:rows, 0:D_MODEL].astype(BF16)
        lanes = hx_buf[slot, 0:rows, 2 * D_MODEL:]

        def expert(wg, wu, wd, gate):
            hg = jnp.dot(h, wg[...].astype(BF16), preferred_element_type=F32)
            hu = jnp.dot(h, wu[...].astype(BF16), preferred_element_type=F32)
            act = hg * _sigmoid(hg) * hu * gate
            return jnp.dot(act.astype(BF16), wd[...].astype(BF16), preferred_element_type=F32)

        y = expert(wga_ref, wua_ref, wda_ref, lanes[:, 0:1]) + expert(wgb_ref, wub_ref, wdb_ref, lanes[:, 1:2])
        cond_row = lanes[:, 2:3]
        gate2 = jnp.where(cond_row == 0.0, g2_ref[0:1, :], jnp.where(cond_row == 1.0, g2_ref[1:2, :], g2_ref[2:3, :]))
        x2 = hx_buf[slot, 0:rows, D_MODEL:2 * D_MODEL] + gate2 * y
        o_buf[slot, 0:rows] = _rmsnorm(x2, fw_ref[...]) if final else x2
        start_scatter(n, slot, rows)

    @pl.when(n < n_valid)
    def _():
        by_size(n, tile_step)

    @pl.when(n == MOE_TILES - 1)
    def _():
        last = n_valid - 1
        by_size(last, lambda r: wait_scatter(last % 2, r))
        by_size(last - 1, lambda r: wait_scatter(1 - last % 2, r))


def _experts(tables, hx, mods, l, w_gate, w_up, w_down, final_w, final):
    w_a = lambda r, c: pl.BlockSpec((None, None, r, c), lambda n, ea, eb, *_: (l, ea[n], 0, 0))
    w_b = lambda r, c: pl.BlockSpec((None, None, r, c), lambda n, ea, eb, *_: (l, eb[n], 0, 0))
    return pl.pallas_call(
        functools.partial(_moe_kernel, final=final),
        grid_spec=pltpu.PrefetchScalarGridSpec(
            num_scalar_prefetch=len(tables),
            grid=(MOE_TILES,),
            in_specs=[
                pl.BlockSpec(memory_space=pl.ANY),
                w_a(D_MODEL, D_EXPERT), w_a(D_MODEL, D_EXPERT), w_a(D_EXPERT, D_MODEL),
                w_b(D_MODEL, D_EXPERT), w_b(D_MODEL, D_EXPERT), w_b(D_EXPERT, D_MODEL),
                pl.BlockSpec((None, MOD_ROWS, D_MODEL), lambda n, *_: (l, 0, N_MOD - 1)),
                pl.BlockSpec((1, D_MODEL), lambda n, *_: (0, 0)),
            ],
            out_specs=pl.BlockSpec(memory_space=pl.ANY),
            scratch_shapes=[pltpu.VMEM((2, MOE_TM, HX_WIDTH), F32), pltpu.VMEM((2, MOE_TM, D_MODEL), F32),
                            pltpu.VMEM((MOE_TM, D_MODEL), F32),
                            pltpu.SemaphoreType.DMA((2,)), pltpu.SemaphoreType.DMA((2,)), pltpu.SemaphoreType.DMA(())],
        ),
        out_shape=jax.ShapeDtypeStruct((OUT_ROWS, D_MODEL), F32),
        compiler_params=_params("arbitrary"),
        name="experts",
    )(*tables, hx, w_gate, w_up, w_down, w_gate, w_up, w_down, mods, final_w.reshape(1, D_MODEL))


def _count_before(flags):
    n, k = flags.shape
    blocks = flags.reshape(n // 128, 128, k).astype(F32)
    strictly_lower = jnp.tril(jnp.ones((128, 128), F32), -1)
    within = jnp.einsum("ij,bjk->bik", strictly_lower, blocks)
    totals = jnp.sum(blocks, axis=1)
    before = jnp.cumsum(totals, axis=0) - totals
    return (within + before[:, None, :]).reshape(n, k).astype(jnp.int32)


def _routing_tables(cls, to_sequence_order):
    n_cls = N_EXPERT_GROUPS * len(PAIR_SLOT_A)
    onehot = (cls[:, None] == jnp.arange(n_cls, dtype=jnp.int32)[None, :]).astype(jnp.int32)
    counts = jnp.sum(onehot, axis=0)
    tiles = (counts + MOE_TM - 1) // MOE_TM
    tile_end = jnp.cumsum(tiles)
    row_in_class = (tile_end - tiles)[None, :] * MOE_TM + _count_before(onehot)
    dst = jnp.sum(onehot * row_in_class, axis=1)
    token_plus_1 = jnp.zeros((MOE_ROWS,), jnp.int32).at[dst].set(jnp.arange(1, T_ALL + 1, dtype=jnp.int32))
    is_pad = token_plus_1 == 0
    src = jnp.maximum(token_plus_1 - 1, 0)
    n_valid = tile_end[-1]
    tile_id = jnp.minimum(jnp.arange(MOE_TILES, dtype=jnp.int32), n_valid - 1)
    tile_cls = jnp.sum((tile_end[None, :] <= tile_id[:, None]).astype(jnp.int32), axis=1)
    group, pair = tile_cls // len(PAIR_SLOT_A), tile_cls % len(PAIR_SLOT_A)
    in_cls = (tile_cls[:, None] == jnp.arange(n_cls, dtype=jnp.int32)[None, :]).astype(jnp.int32)
    left = jnp.sum(in_cls * (counts[None, :] - (tile_id[:, None] - (tile_end - tiles)[None, :]) * MOE_TM), axis=1)
    size = jnp.clip((left + TAIL_QUARTER - 1) // TAIL_QUARTER, 1, TAIL_STEPS).astype(jnp.int32)
    row = jnp.arange(MOE_ROWS, dtype=jnp.int32)
    processed = jnp.logical_and(row % MOE_TM < jnp.repeat(size, MOE_TM) * TAIL_QUARTER, row // MOE_TM < n_valid)
    moved_pad = jnp.logical_and(is_pad, processed)
    spare = T_ALL + _count_before(moved_pad.astype(jnp.int32)[:, None])[:, 0]
    if to_sequence_order:
        j, c, ps = (src // J_ROWS) % SCAN_CHUNK, (src // N_PS) % SEG_CHUNKS, src % N_PS
        target = (src // TILE_TOKENS) * TILE_TOKENS + ps * SEG_LEN + c * SCAN_CHUNK + j
    else:
        target = src
    drow = jnp.where(is_pad, jnp.where(moved_pad, spare, T_ALL), target).astype(jnp.int32)

    def slot_expert(table):
        local = jnp.full_like(pair, table[-1])
        for p in range(len(table) - 1):
            local = jnp.where(pair == p, table[p], local)
        return group * EXPERTS_PER_GROUP + local

    e_a = slot_expert(PAIR_SLOT_A)
    e_b = slot_expert(PAIR_SLOT_B)
    return e_a, e_b, n_valid.reshape(1).astype(jnp.int32), size, src, drow


def _dirs_on_lanes(p):
    p = jnp.moveaxis(p, 1, -2)
    return p.reshape(p.shape[:-2] + (2 * SSM_STATE,))


def _to_internal_order(x):
    x = x.reshape(N_TILES, N_PS, SEG_CHUNKS, SCAN_CHUNK, D_MODEL)
    return x.transpose(0, 3, 2, 1, 4).reshape(T_ALL, D_MODEL)


def _gmlp_position_order(w):
    n_lo = GMLP_CHUNK // SCAN_CHUNK
    lead = w.shape[:2]
    w = w.reshape(lead + (n_lo, SCAN_CHUNK) + w.shape[3:])
    w = jnp.swapaxes(w, 2, 3)
    return w.reshape(lead + (GMLP_CHUNK,) + w.shape[4:])


def kernel(x_prompt, x_sample, c, state_ssm_re, state_ssm_im, c_ctx, norm1_w, norm2_w, w_mod, b_mod, w_in,
           ssm_a_re, ssm_a_im, ssm_log_dt, ssm_b_re, ssm_b_im, ssm_c_re, ssm_c_im, ssm_d, w_glu,
           gmlp_ln_w, gmlp_ln_b, gmlp_w_s, gmlp_b_s, w_out, router_w, router_b, w_gate, w_up, w_down,
           final_norm_w):
    x = jnp.concatenate([x_prompt.reshape(T_PROMPT, D_MODEL), x_sample.reshape(T_SAMPLE, D_MODEL)], axis=0)
    x = _to_internal_order(x)

    cvec = jnp.concatenate([c_ctx[None, :], c, jnp.zeros((MOD_ROWS - 1 - N_SAMPLE_SEQ, D_MODEL), F32)], axis=0)
    mods = _modulation(cvec, w_mod, b_mod)
    ps_row = [[0] * N_PS] * SAMPLE_TILE + [[1 + p // SEGS_PER_SAMPLE_SEQ for p in range(N_PS)]]
    mod8 = mods.reshape(DEPTH, MOD_ROWS, N_MOD, D_MODEL)[:, jnp.array(ps_row, jnp.int32)]
    mod8 = mod8.transpose(0, 1, 3, 2, 4)

    log_dt = jnp.broadcast_to(ssm_log_dt[..., None], ssm_a_re.shape)
    d_lanes = jnp.tile(ssm_d.reshape(DEPTH, N_GROUPS, SSM_GROUP), (1, 1, 128 // SSM_GROUP))
    vecs = jnp.stack([_dirs_on_lanes(ssm_a_re), _dirs_on_lanes(ssm_a_im), _dirs_on_lanes(log_dt), d_lanes], axis=2)
    vecs = jnp.concatenate([vecs, jnp.zeros((DEPTH, N_GROUPS, 4, 2 * SSM_STATE), F32)], axis=2)
    mats = jnp.stack([_dirs_on_lanes(jnp.swapaxes(ssm_b_re, -1, -2)), _dirs_on_lanes(jnp.swapaxes(ssm_b_im, -1, -2)),
                      _dirs_on_lanes(ssm_c_re), _dirs_on_lanes(ssm_c_im)], axis=2)
    wft, cct, ttt, a16 = _ssm_operators(vecs, mats)

    w_s = jnp.swapaxes(_gmlp_position_order(jnp.swapaxes(_gmlp_position_order(gmlp_w_s), 2, 3)), 2, 3).astype(BF16)
    b_s = _gmlp_position_order(gmlp_b_s)[..., None]
    rwt = router_w.T
    rb = router_b.reshape(N_EXPERTS, 1)

    new_re, new_im = [], []
    for l in range(DEPTH):
        ug, vn, xt = _input_proj(x, mod8, l, norm1_w, w_in, gmlp_ln_w, gmlp_ln_b)
        s0 = jnp.concatenate([state_ssm_re[:, l].transpose(2, 0, 1, 3).reshape(N_GROUPS, N_SAMPLE_SEQ, 128),
                              state_ssm_im[:, l].transpose(2, 0, 1, 3).reshape(N_GROUPS, N_SAMPLE_SEQ, 128)], axis=-1)
        s0 = jnp.repeat(s0, SEGS_PER_SAMPLE_SEQ, axis=1)
        yt, fin = _ssm_scan(xt, wft, cct, ttt, a16, s0, l)
        fin = fin[:SAMPLE_TILE].reshape(SAMPLE_TILE, N_GROUPS, N_PS, 2, 2, SSM_STATE)
        fin = fin.transpose(3, 0, 2, 4, 1, 5).reshape(2, N_PROMPT_SEQ, 2, N_GROUPS, SSM_STATE)
        new_re.append(fin[0])
        new_im.append(fin[1])
        hx, route = _mix_out(x, yt, ug, vn, mod8, l, w_glu, w_s, b_s, w_out, norm2_w, rwt, rb)
        final = l == DEPTH - 1
        tables = _routing_tables(route[0].astype(jnp.int32), to_sequence_order=final)
        x = _experts(tables, hx, mods, l, w_gate, w_up, w_down, final_norm_w, final)

    y_prompt = x[:T_PROMPT].reshape(N_PROMPT_SEQ, PROMPT_LEN, D_MODEL)
    y_sample = x[T_PROMPT:T_ALL].reshape(N_SAMPLE_SEQ, SAMPLE_LEN, D_MODEL)
    return (y_prompt, y_sample, jnp.stack(new_re, axis=1), jnp.stack(new_im, axis=1))
```

```python
import functools
import math

import jax
import jax.numpy as jnp
from jax import lax
from jax.experimental import pallas as pl
from jax.experimental.pallas import tpu as pltpu

F32 = jnp.float32
BF16 = jnp.bfloat16

D_MODEL = 1024
N_PROMPT_SEQ = 16
PROMPT_LEN = 256
N_SAMPLE_SEQ = 2
SAMPLE_LEN = 1024
T_PROMPT = N_PROMPT_SEQ * PROMPT_LEN
T_SAMPLE = N_SAMPLE_SEQ * SAMPLE_LEN
T_ALL = T_PROMPT + T_SAMPLE
DEPTH = 2
D_SSM = 512
SSM_GROUP = 16
N_GROUPS = 32
SSM_STATE = 64
D_GMLP = 512
GMLP_HEADS = 4
GMLP_HEAD_DIM = 128
GMLP_CHUNK = 128
N_EXPERTS = 16
N_EXPERT_GROUPS = 4
EXPERTS_PER_GROUP = 4
D_EXPERT = 512
N_MOD = 6
EPS = 1e-6

SCAN_CHUNK = 16
CHUNK_WIDTH = SCAN_CHUNK * SSM_GROUP
SEG_LEN = 256
SEG_CHUNKS = SEG_LEN // SCAN_CHUNK
N_PS = 8
TILE_TOKENS = N_PS * SEG_LEN
N_TILES = T_ALL // TILE_TOKENS
SAMPLE_TILE = T_PROMPT // TILE_TOKENS
SEGS_PER_SAMPLE_SEQ = SAMPLE_LEN // SEG_LEN
J_ROWS = SEG_CHUNKS * N_PS
ALL_J_ROWS = N_TILES * J_ROWS
GROUP_BLOCK = 8
PREP_GROUPS = 8
MOD_ROWS = 8
MOD_K_STEPS = 4

J_PER_STEP = 4
TOKEN_TILE = J_PER_STEP * J_ROWS
PAIR_SLOT_A = (0, 0, 0, 1, 1, 3)
PAIR_SLOT_B = (1, 2, 3, 3, 2, 2)
MOE_TM = 256
TAIL_STEPS = 4
TAIL_QUARTER = MOE_TM // TAIL_STEPS
N_CLASSES = N_EXPERT_GROUPS * len(PAIR_SLOT_A)
MOE_TILES = T_ALL // MOE_TM + N_CLASSES
MOE_ROWS = MOE_TILES * MOE_TM
SPARE_ROWS = N_CLASSES * TAIL_QUARTER
OUT_ROWS = T_ALL + SPARE_ROWS
ROUTE_LANES = 128
HX_WIDTH = 2 * D_MODEL + ROUTE_LANES
VMEM_LIMIT = 56 * 1024 * 1024
TRANS_B = (((1,), (1,)), ((), ()))


def _sigmoid(x):
    return 1.0 / (1.0 + jnp.exp(-x))


def _gelu_tanh(x):
    c = math.sqrt(2.0 / math.pi)
    return x * (0.5 * (1.0 + jnp.tanh(c * (x + 0.044715 * (x * x * x)))))


def _split_bf16(a):
    hi = a.astype(BF16)
    return hi, (a - hi.astype(F32)).astype(BF16)


def _rmsnorm(x, w):
    return x * lax.rsqrt(jnp.mean(x * x, axis=-1, keepdims=True) + EPS) * w


def _per_ps(fn, a, *mods):
    rows, d = a.shape
    out = fn(a.reshape(rows // N_PS, N_PS, d), *[m[None] for m in mods])
    return out.reshape(rows, d)


def _params(*sem):
    return pltpu.CompilerParams(dimension_semantics=sem, vmem_limit_bytes=VMEM_LIMIT)


def _mod_kernel(c_ref, w_ref, b_ref, o_ref):
    c = c_ref[...]
    part = jnp.dot((c * _sigmoid(c)).astype(BF16), w_ref[...].astype(BF16), preferred_element_type=F32)

    @pl.when(pl.program_id(1) == 0)
    def _():
        o_ref[...] = part + b_ref[...]

    @pl.when(pl.program_id(1) > 0)
    def _():
        o_ref[...] += part


def _modulation(cvec, w_mod, b_mod):
    kb = D_MODEL // MOD_K_STEPS
    c_blocks = cvec.reshape(MOD_ROWS, MOD_K_STEPS, kb).transpose(1, 0, 2)
    return pl.pallas_call(
        _mod_kernel,
        grid=(DEPTH, MOD_K_STEPS),
        in_specs=[
            pl.BlockSpec((None, MOD_ROWS, kb), lambda l, k: (k, 0, 0)),
            pl.BlockSpec((None, kb, N_MOD * D_MODEL), lambda l, k: (l, k, 0)),
            pl.BlockSpec((None, 1, N_MOD * D_MODEL), lambda l, k: (l, 0, 0)),
        ],
        out_specs=pl.BlockSpec((None, MOD_ROWS, N_MOD * D_MODEL), lambda l, k: (l, 0, 0)),
        out_shape=jax.ShapeDtypeStruct((DEPTH, MOD_ROWS, N_MOD * D_MODEL), F32),
        compiler_params=_params("arbitrary", "arbitrary"),
        name="adaln_mod",
    )(c_blocks, w_mod, b_mod.reshape(DEPTH, 1, N_MOD * D_MODEL))


def _first_step():
    return jnp.logical_and(pl.program_id(0) == 0, pl.program_id(1) == 0)


def _in_kernel(x_ref, mod_ref, nw_ref, w_ref, lnw_ref, lnb_ref, ug_ref, vn_ref, xt_ref, wg_ref, wst_ref):
    @pl.when(_first_step())
    def _():
        wg_ref[...] = w_ref[:, D_SSM:].astype(BF16)
        wst_ref[...] = w_ref[:, :D_SSM].T.astype(BF16)

    y = _rmsnorm(x_ref[...], nw_ref[...])
    h = _per_ps(lambda a, sc, sh: a * (1.0 + sc) + sh, y, mod_ref[1], mod_ref[0]).astype(BF16)
    zg = _gelu_tanh(jnp.dot(h, wg_ref[...], preferred_element_type=F32))
    v = zg[:, D_GMLP:]
    mu = jnp.mean(v, axis=-1, keepdims=True)
    vc = v - mu
    var = jnp.mean(vc * vc, axis=-1, keepdims=True)
    vn = vc * lax.rsqrt(var + EPS) * lnw_ref[...] + lnb_ref[...]
    for hd in range(GMLP_HEADS):
        cols = slice(hd * GMLP_HEAD_DIM, (hd + 1) * GMLP_HEAD_DIM)
        ug_ref[hd] = zg[:, cols]
        vn_ref[hd] = vn[:, cols]
    xt = lax.dot_general(wst_ref[...], h, TRANS_B, preferred_element_type=F32).astype(BF16)
    for k in range(J_PER_STEP):
        xt_ref[k] = xt[:, k * J_ROWS:(k + 1) * J_ROWS]


def _input_proj(x, mod8, l, norm1_w, w_in, ln_w, ln_b):
    tm = TOKEN_TILE
    steps = TILE_TOKENS // tm
    tok = lambda n: pl.BlockSpec((tm, n), lambda t, s: (t * steps + s, 0))
    heads = pl.BlockSpec((GMLP_HEADS, tm, GMLP_HEAD_DIM), lambda t, s: (0, t * steps + s, 0))
    lay = lambda *shape: pl.BlockSpec((None,) + shape, lambda t, s: (l,) + (0,) * len(shape))
    return pl.pallas_call(
        _in_kernel,
        grid=(N_TILES, steps),
        in_specs=[
            tok(D_MODEL),
            pl.BlockSpec((None, None, N_MOD, N_PS, D_MODEL), lambda t, s: (l, t, 0, 0, 0)),
            lay(1, D_MODEL), lay(D_MODEL, D_SSM + 2 * D_GMLP), lay(1, D_GMLP), lay(1, D_GMLP),
        ],
        out_specs=[heads, heads,
                   pl.BlockSpec((J_PER_STEP, D_SSM, J_ROWS), lambda t, s: (s, 0, t))],
        out_shape=[jax.ShapeDtypeStruct((GMLP_HEADS, T_ALL, GMLP_HEAD_DIM), F32)] * 2 + [
            jax.ShapeDtypeStruct((SCAN_CHUNK, D_SSM, ALL_J_ROWS), BF16)],
        scratch_shapes=[pltpu.VMEM((D_MODEL, 2 * D_GMLP), BF16), pltpu.VMEM((D_SSM, D_MODEL), BF16)],
        compiler_params=_params("arbitrary", "arbitrary"),
        name="norm1_in_proj",
    )(x, mod8, norm1_w.reshape(DEPTH, 1, D_MODEL), w_in,
      ln_w.reshape(DEPTH, 1, D_GMLP), ln_b.reshape(DEPTH, 1, D_GMLP))


def _shift_lanes_right(a, b, s, lane):
    if s == 0:
        return a, b
    if s == 128:
        return jnp.zeros_like(a), a
    if s < 128:
        ra = pltpu.roll(a, s, 1)
        rb = pltpu.roll(b, s, 1)
        return jnp.where(lane >= s, ra, 0.0), jnp.where(lane >= s, rb, ra)
    t = s - 128
    return jnp.zeros_like(a), jnp.where(lane >= t, pltpu.roll(a, t, 1), 0.0)


def _shift_lanes_left(a, b, s, lane):
    if s == 0:
        return a, b
    if s == 128:
        return b, jnp.zeros_like(b)
    if s < 128:
        ra = pltpu.roll(a, 128 - s, 1)
        rb = pltpu.roll(b, 128 - s, 1)
        return jnp.where(lane < 128 - s, ra, rb), jnp.where(lane < 128 - s, rb, 0.0)
    t = s - 128
    return jnp.where(lane < 128 - t, pltpu.roll(b, 128 - t, 1), 0.0), jnp.zeros_like(b)


def _prep_kernel(*refs):
    for g in range(PREP_GROUPS):
        _prep_group(*[r.at[g] for r in refs])


def _prep_group(vec_ref, mat_ref, wft_ref, cct_ref, ttt_ref, a16_ref, wf_scr, cm_scr, tt_scr):
    a_re = vec_ref[0:1, :]
    a_im = vec_ref[1:2, :]
    dt = jnp.exp(vec_ref[2:3, :])
    d_skip = vec_ref[3:4, :]
    mag = jnp.exp(a_re * dt)
    ang = a_im * dt
    ab_r = mag * jnp.cos(ang)
    ab_i = mag * jnp.sin(ang)
    den = a_re * a_re + a_im * a_im
    nr = ab_r - 1.0
    q_r = (nr * a_re + ab_i * a_im) / den
    q_i = (ab_i * a_re - nr * a_im) / den
    bt_r = mat_ref[0]
    bt_i = mat_ref[1]
    c_r = mat_ref[2]
    c_i = mat_ref[3]
    bb_r = q_r * bt_r - q_i * bt_i
    bb_i = q_r * bt_i + q_i * bt_r
    p_r = [jnp.ones_like(ab_r)]
    p_i = [jnp.zeros_like(ab_r)]
    for _ in range(SCAN_CHUNK):
        pr, pi = p_r[-1], p_i[-1]
        p_r.append(pr * ab_r - pi * ab_i)
        p_i.append(pr * ab_i + pi * ab_r)
    a16_ref[0] = jnp.broadcast_to(p_r[SCAN_CHUNK], (N_PS, 128))
    a16_ref[1] = jnp.broadcast_to(p_i[SCAN_CHUNK], (N_PS, 128))

    lane = lax.broadcasted_iota(jnp.int32, (1, 128), 1)
    is_fwd = lane < SSM_STATE

    def pick(mf, mb):
        return jnp.where(is_fwd, p_r[mf], p_r[mb]), jnp.where(is_fwd, p_i[mf], p_i[mb])

    for j in range(SCAN_CHUNK):
        rows = slice(j * SSM_GROUP, (j + 1) * SSM_GROUP)
        wr, wi = pick(SCAN_CHUNK - 1 - j, j)
        wf_scr[rows, 0:128] = bb_r * wr - bb_i * wi
        wf_scr[rows, 128:256] = bb_r * wi + bb_i * wr
        wr, wi = pick(j + 1, SCAN_CHUNK - j)
        cct_ref[rows, 0:128] = (c_r * wr - c_i * wi).astype(BF16)
        cct_ref[rows, 128:256] = (-(c_r * wi + c_i * wr)).astype(BF16)
        wr, wi = pick(j, SCAN_CHUNK - 1 - j)
        cm_scr[rows, 0:128] = c_r * wr - c_i * wi
        cm_scr[rows, 128:256] = c_r * wi + c_i * wr
    wft_ref[...] = wf_scr[...].T.astype(BF16)

    zero = jnp.zeros_like(bb_r)
    cm_hi, cm_lo = _split_bf16(cm_scr[...])
    dot_t = lambda a, b: lax.dot_general(a, b, TRANS_B, preferred_element_type=F32)

    def lag_rows(keep):
        lhs = jnp.concatenate([jnp.where(keep, bb_r, zero), jnp.where(keep, -bb_i, zero)], axis=1)
        hi, lo = _split_bf16(lhs)
        return dot_t(hi, cm_hi) + (dot_t(hi, cm_lo) + dot_t(lo, cm_hi))

    mf = lag_rows(is_fwd)
    mb = lag_rows(jnp.logical_not(is_fwd))
    mf_a, mf_b = mf[:, 0:128], mf[:, 128:256]
    mb_a, mb_b = mb[:, 0:128], mb[:, 128:256]
    row_h = lax.broadcasted_iota(jnp.int32, (SSM_GROUP, 128), 0)
    lane_h = lax.broadcasted_iota(jnp.int32, (SSM_GROUP, 128), 1)
    for jp in range(SCAN_CHUNK):
        rows = slice(jp * SSM_GROUP, (jp + 1) * SSM_GROUP)
        fa, fb = _shift_lanes_right(mf_a, mf_b, SSM_GROUP * jp, lane)
        ba, bb = _shift_lanes_left(mb_a, mb_b, SSM_GROUP * (SCAN_CHUNK - 1 - jp), lane)
        diag = SSM_GROUP * jp + row_h
        tt_scr[rows, 0:128] = fa + ba + jnp.where(lane_h == diag, d_skip, 0.0)
        tt_scr[rows, 128:256] = fb + bb + jnp.where(lane_h + 128 == diag, d_skip, 0.0)
    ttt_ref[...] = tt_scr[...].T.astype(BF16)


def _ssm_operators(vecs, mats):
    op = jax.ShapeDtypeStruct((DEPTH, N_GROUPS, CHUNK_WIDTH, CHUNK_WIDTH), BF16)
    pg = PREP_GROUPS
    op_spec = pl.BlockSpec((None, pg, CHUNK_WIDTH, CHUNK_WIDTH), lambda l, g: (l, g, 0, 0))
    sq = pltpu.VMEM((pg, CHUNK_WIDTH, CHUNK_WIDTH), F32)
    return pl.pallas_call(
        _prep_kernel,
        grid=(DEPTH, N_GROUPS // pg),
        in_specs=[
            pl.BlockSpec((None, pg, 8, 128), lambda l, g: (l, g, 0, 0)),
            pl.BlockSpec((None, pg, 4, SSM_GROUP, 128), lambda l, g: (l, g, 0, 0, 0)),
        ],
        out_specs=[op_spec, op_spec, op_spec,
                   pl.BlockSpec((None, pg, 2, N_PS, 128), lambda l, g: (l, g, 0, 0, 0))],
        out_shape=[op, op, op, jax.ShapeDtypeStruct((DEPTH, N_GROUPS, 2, N_PS, 128), F32)],
        scratch_shapes=[sq, sq, sq],
        compiler_params=_params("arbitrary", "arbitrary"),
        name="s5_operators",
    )(vecs, mats)


def _ssm_kernel(xt_ref, wft_ref, cct_ref, ttt_ref, a16_ref, s0_ref, yt_ref, fin_ref, s_scr, f_scr, ft_scr):
    gb = GROUP_BLOCK
    is_fwd = lax.broadcasted_iota(jnp.int32, (1, 1, 128), 2) < SSM_STATE
    half = SSM_STATE

    def group_x(gl):
        return xt_ref[:, gl * SSM_GROUP:(gl + 1) * SSM_GROUP, :].reshape(CHUNK_WIDTH, ALL_J_ROWS)

    for gl in range(gb):
        ft_scr[gl] = jnp.dot(wft_ref[gl], group_x(gl), preferred_element_type=F32)
        f_scr[gl] = ft_scr[gl].T

    def scan(tiles, s_r, s_i):
        rows = len(tiles) * N_PS
        a_r = jnp.broadcast_to(a16_ref[:, 0, 0:1, :], (gb, rows, 128))
        a_i = jnp.broadcast_to(a16_ref[:, 1, 0:1, :], (gb, rows, 128))
        for i in range(SEG_CHUNKS):
            f_r, f_i = [], []
            for k, t in enumerate(tiles):
                st = slice(k * N_PS, (k + 1) * N_PS)
                rf = slice(t * J_ROWS + i * N_PS, t * J_ROWS + (i + 1) * N_PS)
                rb = slice(t * J_ROWS + (SEG_CHUNKS - 1 - i) * N_PS, t * J_ROWS + (SEG_CHUNKS - i) * N_PS)
                s_scr[:, rf, 0:half] = s_r[:, st, 0:half]
                s_scr[:, rf, 128:128 + half] = s_i[:, st, 0:half]
                s_scr[:, rb, half:128] = s_r[:, st, half:128]
                s_scr[:, rb, 128 + half:256] = s_i[:, st, half:128]
                f_r.append(jnp.where(is_fwd, f_scr[:, rf, 0:128], f_scr[:, rb, 0:128]))
                f_i.append(jnp.where(is_fwd, f_scr[:, rf, 128:256], f_scr[:, rb, 128:256]))
            f_r = jnp.concatenate(f_r, axis=1)
            f_i = jnp.concatenate(f_i, axis=1)
            s_r, s_i = a_r * s_r - a_i * s_i + f_r, a_r * s_i + a_i * s_r + f_i
        return s_r, s_i

    zeros = jnp.zeros((gb, N_TILES * N_PS, 128), F32)
    z_r, z_i = scan(list(range(N_TILES)), zeros, zeros)
    fin_ref[:, :, 0:128] = z_r
    fin_ref[:, :, 128:256] = z_i

    z_r = z_r[:, SAMPLE_TILE * N_PS:, :]
    z_i = z_i[:, SAMPLE_TILE * N_PS:, :]
    b_r, b_i = a16_ref[:, 0], a16_ref[:, 1]
    for _ in range(4):
        b_r, b_i = b_r * b_r - b_i * b_i, 2.0 * (b_r * b_i)
    seg = lax.broadcasted_iota(jnp.int32, (gb, N_PS, 128), 1) % SEGS_PER_SAMPLE_SEQ
    i_r = s0_ref[:, :, 0:128]
    i_i = s0_ref[:, :, 128:256]
    for step in range(1, SEGS_PER_SAMPLE_SEQ):
        pr = jnp.where(is_fwd, pltpu.roll(i_r, 1, 1), pltpu.roll(i_r, N_PS - 1, 1))
        pi = jnp.where(is_fwd, pltpu.roll(i_i, 1, 1), pltpu.roll(i_i, N_PS - 1, 1))
        zr = jnp.where(is_fwd, pltpu.roll(z_r, 1, 1), pltpu.roll(z_r, N_PS - 1, 1))
        zi = jnp.where(is_fwd, pltpu.roll(z_i, 1, 1), pltpu.roll(z_i, N_PS - 1, 1))
        n_r = b_r * pr - b_i * pi + zr
        n_i = b_r * pi + b_i * pr + zi
        first = jnp.where(is_fwd, step, 0)
        last = jnp.where(is_fwd, SEGS_PER_SAMPLE_SEQ - 1, SEGS_PER_SAMPLE_SEQ - 1 - step)
        upd = jnp.logical_and(seg >= first, seg <= last)
        i_r = jnp.where(upd, n_r, i_r)
        i_i = jnp.where(upd, n_i, i_i)
    scan([SAMPLE_TILE], i_r, i_i)

    for gl in range(gb):
        yt = jnp.dot(ttt_ref[gl], group_x(gl), preferred_element_type=F32)
        yt += lax.dot_general(cct_ref[gl], s_scr[gl].astype(BF16), TRANS_B, preferred_element_type=F32)
        yt_ref[:, gl * SSM_GROUP:(gl + 1) * SSM_GROUP, :] = yt.reshape(SCAN_CHUNK, SSM_GROUP, ALL_J_ROWS)


def _ssm_scan(xt, wft, cct, ttt, a16, s0, l):
    gb = GROUP_BLOCK
    op_spec = pl.BlockSpec((None, gb, CHUNK_WIDTH, CHUNK_WIDTH), lambda g: (l, g, 0, 0))
    io_spec = pl.BlockSpec((SCAN_CHUNK, gb * SSM_GROUP, ALL_J_ROWS), lambda g: (0, g, 0))
    rows = pltpu.VMEM((gb, ALL_J_ROWS, CHUNK_WIDTH), F32)
    return pl.pallas_call(
        _ssm_kernel,
        grid=(N_GROUPS // gb,),
        in_specs=[
            io_spec, op_spec, op_spec, op_spec,
            pl.BlockSpec((None, gb, 2, N_PS, 128), lambda g: (l, g, 0, 0, 0)),
            pl.BlockSpec((gb, N_PS, CHUNK_WIDTH), lambda g: (g, 0, 0)),
        ],
        out_specs=[io_spec, pl.BlockSpec((gb, N_TILES * N_PS, CHUNK_WIDTH), lambda g: (g, 0, 0))],
        out_shape=[
            jax.ShapeDtypeStruct((SCAN_CHUNK, D_SSM, ALL_J_ROWS), F32),
            jax.ShapeDtypeStruct((N_GROUPS, N_TILES * N_PS, CHUNK_WIDTH), F32),
        ],
        scratch_shapes=[rows, rows, pltpu.VMEM((gb, CHUNK_WIDTH, ALL_J_ROWS), F32)],
        compiler_params=_params("arbitrary"),
        name="s5_chunk_scan",
    )(xt, wft, cct, ttt, a16, s0)


def _post_kernel(x_ref, yt_ref, ug_ref, vn_ref, mod_ref, wglu_ref, ws_ref, bs_ref, wout_f32_ref,
                 nw_ref, rwt_ref, rb_ref, hx_ref, route_ref, yg_scr, y_scr, wglut_ref, wout_ref):
    tile = pl.program_id(0)
    step = pl.program_id(1)

    @pl.when(_first_step())
    def _():
        wglut_ref[...] = wglu_ref[...].T.astype(BF16)
        wout_ref[...] = wout_f32_ref[...].astype(BF16)

    @pl.when(step == 0)
    def _():
        def chunk(n, carry):
            ps = n % N_PS
            c_hi = n // N_PS
            base = c_hi * (GMLP_CHUNK // SCAN_CHUNK) * N_PS + ps
            rows = [pl.ds(j * J_ROWS + base, GMLP_CHUNK // SCAN_CHUNK, stride=N_PS) for j in range(SCAN_CHUNK)]
            for h in range(GMLP_HEADS):
                v = jnp.concatenate([vn_ref[h, r, :] for r in rows], axis=0).astype(BF16)
                u = jnp.concatenate([ug_ref[h, r, :] for r in rows], axis=0)
                s = jnp.dot(ws_ref[h], v, preferred_element_type=F32) + bs_ref[h]
                yg = u * s
                for j, r in enumerate(rows):
                    yg_scr[h, r, :] = yg[j * 8:(j + 1) * 8]
            return carry

        for n in range(TILE_TOKENS // GMLP_CHUNK):
            chunk(n, 0)

    yt = _gelu_tanh(jnp.concatenate([yt_ref[k] for k in range(J_PER_STEP)], axis=1))
    yt = yt * _sigmoid(jnp.dot(wglut_ref[...], yt.astype(BF16), preferred_element_type=F32))
    for k in range(J_PER_STEP):
        y_scr[k * J_ROWS:(k + 1) * J_ROWS, :] = yt[:, k * J_ROWS:(k + 1) * J_ROWS].T
    row0 = pl.multiple_of(step * TOKEN_TILE, TOKEN_TILE)
    proj = jnp.dot(y_scr[...].astype(BF16), wout_ref[0:D_SSM, :], preferred_element_type=F32)
    yg = jnp.concatenate([yg_scr[h, pl.ds(row0, TOKEN_TILE), :] for h in range(GMLP_HEADS)], axis=1)
    proj += jnp.dot(yg.astype(BF16), wout_ref[D_SSM:, :], preferred_element_type=F32)
    x1 = x_ref[...] + _per_ps(lambda a, g: a * g, proj, mod_ref[2])
    h2 = _per_ps(lambda a, sc, sh: a * (1.0 + sc) + sh, _rmsnorm(x1, nw_ref[...]), mod_ref[4], mod_ref[3])
    hx_ref[:, 0:D_MODEL] = h2
    hx_ref[:, D_MODEL:2 * D_MODEL] = x1
    r_hi, r_lo = _split_bf16(rwt_ref[...])
    h_hi, h_lo = _split_bf16(h2)
    dot_t = lambda a, b: lax.dot_general(a, b, TRANS_B, preferred_element_type=F32)
    logits = dot_t(r_hi, h_hi) + (dot_t(r_hi, h_lo) + dot_t(r_lo, h_hi))
    scores = _sigmoid(logits)
    sel = scores + rb_ref[...]
    sc = [scores[e:e + 1, :] for e in range(N_EXPERTS)]
    sl = [sel[e:e + 1, :] for e in range(N_EXPERTS)]
    gscore = []
    for g in range(N_EXPERT_GROUPS):
        v0, v1, v2, v3 = sl[4 * g:4 * g + 4]
        hi01, lo01 = jnp.maximum(v0, v1), jnp.minimum(v0, v1)
        hi23, lo23 = jnp.maximum(v2, v3), jnp.minimum(v2, v3)
        top1 = jnp.maximum(hi01, hi23)
        top2 = jnp.maximum(jnp.minimum(hi01, hi23), jnp.maximum(lo01, lo23))
        gscore.append(top1 + top2)
    best = gscore[0]
    gidx = jnp.zeros_like(best, dtype=jnp.int32)
    for g in range(1, N_EXPERT_GROUPS):
        upd = gscore[g] > best
        gidx = jnp.where(upd, g, gidx)
        best = jnp.where(upd, gscore[g], best)

    def in_group(vals, k):
        out = vals[k]
        for g in range(1, N_EXPERT_GROUPS):
            out = jnp.where(gidx == g, vals[4 * g + k], out)
        return out

    v = [in_group(sl, k) for k in range(EXPERTS_PER_GROUP)]
    s = [in_group(sc, k) for k in range(EXPERTS_PER_GROUP)]
    w = []
    bits = jnp.zeros_like(gidx)
    for k in range(EXPERTS_PER_GROUP):
        rank = jnp.zeros_like(gidx)
        for j in range(EXPERTS_PER_GROUP):
            if j == k:
                continue
            ahead = (v[j] >= v[k]) if j < k else (v[j] > v[k])
            rank = rank + ahead.astype(jnp.int32)
        w.append(jnp.where(rank < 2, s[k], 0.0))
        bits = bits + jnp.where(rank < 2, 1 << k, 0)
    denom = (w[0] + w[1]) + (w[2] + w[3])
    gate = [wk / denom for wk in w]
    pair = jnp.full_like(gidx, len(PAIR_SLOT_A) - 1)
    for p in range(len(PAIR_SLOT_A) - 1):
        pair = jnp.where(bits == (1 << PAIR_SLOT_A[p]) + (1 << PAIR_SLOT_B[p]), p, pair)

    def slot_gate(table):
        out = gate[table[0]]
        for p in range(1, len(table)):
            out = jnp.where(pair == p, gate[table[p]], out)
        return out

    n_tok = route_ref.shape[1]
    route_ref[0:1, :] = (gidx * len(PAIR_SLOT_A) + pair).astype(F32)
    ps = lax.broadcasted_iota(jnp.int32, (1, n_tok), 1) % N_PS
    cond = jnp.where(tile < SAMPLE_TILE, 0, 1 + ps // SEGS_PER_SAMPLE_SEQ).astype(F32)
    lanes = jnp.concatenate([slot_gate(PAIR_SLOT_A), slot_gate(PAIR_SLOT_B), cond,
                             jnp.zeros((ROUTE_LANES - 3, n_tok), F32)], axis=0)
    hx_ref[:, 2 * D_MODEL:] = lanes.T


def _mix_out(x, yt, ug, vn, mod8, l, w_glu, w_s, b_s, w_out, norm2_w, rwt, rb):
    tm = TOKEN_TILE
    steps = TILE_TOKENS // tm
    tok = lambda n: pl.BlockSpec((tm, n), lambda t, s: (t * steps + s, 0))
    whole_tile = pl.BlockSpec((GMLP_HEADS, TILE_TOKENS, GMLP_HEAD_DIM), lambda t, s: (0, t, 0))
    lay = lambda *shape: pl.BlockSpec((None,) + shape, lambda t, s: (l,) + (0,) * len(shape))
    return pl.pallas_call(
        _post_kernel,
        grid=(N_TILES, steps),
        in_specs=[
            tok(D_MODEL),
            pl.BlockSpec((J_PER_STEP, D_SSM, J_ROWS), lambda t, s: (s, 0, t)),
            whole_tile, whole_tile,
            pl.BlockSpec((None, None, N_MOD, N_PS, D_MODEL), lambda t, s: (l, t, 0, 0, 0)),
            lay(D_SSM, D_SSM), lay(GMLP_HEADS, GMLP_CHUNK, GMLP_CHUNK),
            lay(GMLP_HEADS, GMLP_CHUNK, 1), lay(D_MODEL, D_MODEL), lay(1, D_MODEL),
            pl.BlockSpec((N_EXPERTS, D_MODEL), lambda t, s: (0, 0)),
            pl.BlockSpec((N_EXPERTS, 1), lambda t, s: (0, 0)),
        ],
        out_specs=[tok(HX_WIDTH), pl.BlockSpec((1, tm), lambda t, s: (0, t * steps + s))],
        out_shape=[
            jax.ShapeDtypeStruct((T_ALL, HX_WIDTH), F32),
            jax.ShapeDtypeStruct((1, T_ALL), F32),
        ],
        scratch_shapes=[pltpu.VMEM((GMLP_HEADS, TILE_TOKENS, GMLP_HEAD_DIM), F32), pltpu.VMEM((tm, D_SSM), F32),
                        pltpu.VMEM((D_SSM, D_SSM), BF16), pltpu.VMEM((D_MODEL, D_MODEL), BF16)],
        compiler_params=_params("arbitrary", "arbitrary"),
        name="mixers_out_router",
    )(x, yt, ug, vn, mod8, w_glu, w_s, b_s, w_out, norm2_w.reshape(DEPTH, 1, D_MODEL), rwt, rb)


def _moe_kernel(ea_ref, eb_ref, nv_ref, size_ref, src_ref, drow_ref,
                hx_hbm, wga_ref, wua_ref, wda_ref, wgb_ref, wub_ref, wdb_ref, g2_ref, fw_ref,
                o_hbm, hx_buf, o_buf, z_buf, g_sem, s_sem, z_sem, *, final):
    n = pl.program_id(0)
    n_valid = nv_ref[0]
    slot = n % 2

    def by_size(tile, fn):
        for quarters in range(1, TAIL_STEPS + 1):
            @pl.when(size_ref[tile] == quarters)
            def _(rows=quarters * TAIL_QUARTER):
                fn(rows)

    def start_gather(tile, sl, rows):
        base = tile * MOE_TM
        for r in range(rows):
            pltpu.make_async_copy(hx_hbm.at[pl.ds(src_ref[base + r], 1)], hx_buf.at[sl, pl.ds(r, 1)],
                                  g_sem.at[sl]).start()

    def wait_gather(sl, rows):
        pltpu.make_async_copy(hx_hbm.at[pl.ds(0, rows)], hx_buf.at[sl, pl.ds(0, rows)], g_sem.at[sl]).wait()

    def start_scatter(tile, sl, rows):
        base = tile * MOE_TM
        for r in range(rows):
            pltpu.make_async_copy(o_buf.at[sl, pl.ds(r, 1)], o_hbm.at[pl.ds(drow_ref[base + r], 1)],
                                  s_sem.at[sl]).start()

    def wait_scatter(sl, rows):
        pltpu.make_async_copy(o_buf.at[sl, pl.ds(0, rows)], o_hbm.at[pl.ds(0, rows)], s_sem.at[sl]).wait()

    @pl.when(n == 0)
    def _():
        z_buf[...] = jnp.zeros_like(z_buf)
        spare = [pltpu.make_async_copy(z_buf, o_hbm.at[pl.ds(T_ALL + i * MOE_TM, MOE_TM)], z_sem)
                 for i in range(SPARE_ROWS // MOE_TM)]
        for cp in spare:
            cp.start()
        for cp in spare:
            cp.wait()
        by_size(0, lambda rows: start_gather(0, 0, rows))

    @pl.when(n + 1 < n_valid)
    def _():
        by_size(n + 1, lambda rows: start_gather(n + 1, 1 - slot, rows))

    def tile_step(rows):
        wait_gather(slot, rows)

        @pl.when(n >= 2)
        def _():
            by_size(n - 2, lambda r: wait_scatter(slot, r))

        h = hx_buf[slot, 0:rows, 0:D_MODEL].astype(BF16)
        lanes = hx_buf[slot, 0:rows, 2 * D_MODEL:]

        def expert(wg, wu, wd, gate):
            hg = jnp.dot(h, wg[...].astype(BF16), preferred_element_type=F32)
            hu = jnp.dot(h, wu[...].astype(BF16), preferred_element_type=F32)
            act = hg * _sigmoid(hg) * hu * gate
            return jnp.dot(act.astype(BF16), wd[...].astype(BF16), preferred_element_type=F32)

        y = expert(wga_ref, wua_ref, wda_ref, lanes[:, 0:1]) + expert(wgb_ref, wub_ref, wdb_ref, lanes[:, 1:2])
        cond_row = lanes[:, 2:3]
        gate2 = jnp.where(cond_row == 0.0, g2_ref[0:1, :], jnp.where(cond_row == 1.0, g2_ref[1:2, :], g2_ref[2:3, :]))
        x2 = hx_buf[slot, 0:rows, D_MODEL:2 * D_MODEL] + gate2 * y
        o_buf[slot, 0:rows] = _rmsnorm(x2, fw_ref[...]) if final else x2
        start_scatter(n, slot, rows)

    @pl.when(n < n_valid)
    def _():
        by_size(n, tile_step)

    @pl.when(n == MOE_TILES - 1)
    def _():
        last = n_valid - 1
        by_size(last, lambda r: wait_scatter(last % 2, r))
        by_size(last - 1, lambda r: wait_scatter(1 - last % 2, r))


def _experts(tables, hx, mods, l, w_gate, w_up, w_down, final_w, final):
    w_a = lambda r, c: pl.BlockSpec((None, None, r, c), lambda n, ea, eb, *_: (l, ea[n], 0, 0))
    w_b = lambda r, c: pl.BlockSpec((None, None, r, c), lambda n, ea, eb, *_: (l, eb[n], 0, 0))
    return pl.pallas_call(
        functools.partial(_moe_kernel, final=final),
        grid_spec=pltpu.PrefetchScalarGridSpec(
            num_scalar_prefetch=len(tables),
            grid=(MOE_TILES,),
            in_specs=[
                pl.BlockSpec(memory_space=pl.ANY),
                w_a(D_MODEL, D_EXPERT), w_a(D_MODEL, D_EXPERT), w_a(D_EXPERT, D_MODEL),
                w_b(D_MODEL, D_EXPERT), w_b(D_MODEL, D_EXPERT), w_b(D_EXPERT, D_MODEL),
                pl.BlockSpec((None, MOD_ROWS, D_MODEL), lambda n, *_: (l, 0, N_MOD - 1)),
                pl.BlockSpec((1, D_MODEL), lambda n, *_: (0, 0)),
            ],
            out_specs=pl.BlockSpec(memory_space=pl.ANY),
            scratch_shapes=[pltpu.VMEM((2, MOE_TM, HX_WIDTH), F32), pltpu.VMEM((2, MOE_TM, D_MODEL), F32),
                            pltpu.VMEM((MOE_TM, D_MODEL), F32),
                            pltpu.SemaphoreType.DMA((2,)), pltpu.SemaphoreType.DMA((2,)), pltpu.SemaphoreType.DMA(())],
        ),
        out_shape=jax.ShapeDtypeStruct((OUT_ROWS, D_MODEL), F32),
        compiler_params=_params("arbitrary"),
        name="experts",
    )(*tables, hx, w_gate, w_up, w_down, w_gate, w_up, w_down, mods, final_w.reshape(1, D_MODEL))


def _count_before(flags):
    n, k = flags.shape
    blocks = flags.reshape(n // 128, 128, k).astype(F32)
    strictly_lower = jnp.tril(jnp.ones((128, 128), F32), -1)
    within = jnp.einsum("ij,bjk->bik", strictly_lower, blocks)
    totals = jnp.sum(blocks, axis=1)
    before = jnp.cumsum(totals, axis=0) - totals
    return (within + before[:, None, :]).reshape(n, k).astype(jnp.int32)


def _routing_tables(cls, to_sequence_order):
    n_cls = N_EXPERT_GROUPS * len(PAIR_SLOT_A)
    onehot = (cls[:, None] == jnp.arange(n_cls, dtype=jnp.int32)[None, :]).astype(jnp.int32)
    counts = jnp.sum(onehot, axis=0)
    tiles = (counts + MOE_TM - 1) // MOE_TM
    tile_end = jnp.cumsum(tiles)
    row_in_class = (tile_end - tiles)[None, :] * MOE_TM + _count_before(onehot)
    dst = jnp.sum(onehot * row_in_class, axis=1)
    token_plus_1 = jnp.zeros((MOE_ROWS,), jnp.int32).at[dst].set(jnp.arange(1, T_ALL + 1, dtype=jnp.int32))
    is_pad = token_plus_1 == 0
    src = jnp.maximum(token_plus_1 - 1, 0)
    n_valid = tile_end[-1]
    tile_id = jnp.minimum(jnp.arange(MOE_TILES, dtype=jnp.int32), n_valid - 1)
    tile_cls = jnp.sum((tile_end[None, :] <= tile_id[:, None]).astype(jnp.int32), axis=1)
    group, pair = tile_cls // len(PAIR_SLOT_A), tile_cls % len(PAIR_SLOT_A)
    in_cls = (tile_cls[:, None] == jnp.arange(n_cls, dtype=jnp.int32)[None, :]).astype(jnp.int32)
    left = jnp.sum(in_cls * (counts[None, :] - (tile_id[:, None] - (tile_end - tiles)[None, :]) * MOE_TM), axis=1)
    size = jnp.clip((left + TAIL_QUARTER - 1) // TAIL_QUARTER, 1, TAIL_STEPS).astype(jnp.int32)
    row = jnp.arange(MOE_ROWS, dtype=jnp.int32)
    processed = jnp.logical_and(row % MOE_TM < jnp.repeat(size, MOE_TM) * TAIL_QUARTER, row // MOE_TM < n_valid)
    moved_pad = jnp.logical_and(is_pad, processed)
    spare = T_ALL + _count_before(moved_pad.astype(jnp.int32)[:, None])[:, 0]
    if to_sequence_order:
        j, c, ps = (src // J_ROWS) % SCAN_CHUNK, (src // N_PS) % SEG_CHUNKS, src % N_PS
        target = (src // TILE_TOKENS) * TILE_TOKENS + ps * SEG_LEN + c * SCAN_CHUNK + j
    else:
        target = src
    drow = jnp.where(is_pad, jnp.where(moved_pad, spare, T_ALL), target).astype(jnp.int32)

    def slot_expert(table):
        local = jnp.full_like(pair, table[-1])
        for p in range(len(table) - 1):
            local = jnp.where(pair == p, table[p], local)
        return group * EXPERTS_PER_GROUP + local

    e_a = slot_expert(PAIR_SLOT_A)
    e_b = slot_expert(PAIR_SLOT_B)
    return e_a, e_b, n_valid.reshape(1).astype(jnp.int32), size, src, drow


def _dirs_on_lanes(p):
    p = jnp.moveaxis(p, 1, -2)
    return p.reshape(p.shape[:-2] + (2 * SSM_STATE,))


def _to_internal_order(x):
    x = x.reshape(N_TILES, N_PS, SEG_CHUNKS, SCAN_CHUNK, D_MODEL)
    return x.transpose(0, 3, 2, 1, 4).reshape(T_ALL, D_MODEL)


def _gmlp_position_order(w):
    n_lo = GMLP_CHUNK // SCAN_CHUNK
    lead = w.shape[:2]
    w = w.reshape(lead + (n_lo, SCAN_CHUNK) + w.shape[3:])
    w = jnp.swapaxes(w, 2, 3)
    return w.reshape(lead + (GMLP_CHUNK,) + w.shape[4:])


def kernel(x_prompt, x_sample, c, state_ssm_re, state_ssm_im, c_ctx, norm1_w, norm2_w, w_mod, b_mod, w_in,
           ssm_a_re, ssm_a_im, ssm_log_dt, ssm_b_re, ssm_b_im, ssm_c_re, ssm_c_im, ssm_d, w_glu,
           gmlp_ln_w, gmlp_ln_b, gmlp_w_s, gmlp_b_s, w_out, router_w, router_b, w_gate, w_up, w_down,
           final_norm_w):
    x = jnp.concatenate([x_prompt.reshape(T_PROMPT, D_MODEL), x_sample.reshape(T_SAMPLE, D_MODEL)], axis=0)
    x = _to_internal_order(x)

    cvec = jnp.concatenate([c_ctx[None, :], c, jnp.zeros((MOD_ROWS - 1 - N_SAMPLE_SEQ, D_MODEL), F32)], axis=0)
    mods = _modulation(cvec, w_mod, b_mod)
    ps_row = [[0] * N_PS] * SAMPLE_TILE + [[1 + p // SEGS_PER_SAMPLE_SEQ for p in range(N_PS)]]
    mod8 = mods.reshape(DEPTH, MOD_ROWS, N_MOD, D_MODEL)[:, jnp.array(ps_row, jnp.int32)]
    mod8 = mod8.transpose(0, 1, 3, 2, 4)

    log_dt = jnp.broadcast_to(ssm_log_dt[..., None], ssm_a_re.shape)
    d_lanes = jnp.tile(ssm_d.reshape(DEPTH, N_GROUPS, SSM_GROUP), (1, 1, 128 // SSM_GROUP))
    vecs = jnp.stack([_dirs_on_lanes(ssm_a_re), _dirs_on_lanes(ssm_a_im), _dirs_on_lanes(log_dt), d_lanes], axis=2)
    vecs = jnp.concatenate([vecs, jnp.zeros((DEPTH, N_GROUPS, 4, 2 * SSM_STATE), F32)], axis=2)
    mats = jnp.stack([_dirs_on_lanes(jnp.swapaxes(ssm_b_re, -1, -2)), _dirs_on_lanes(jnp.swapaxes(ssm_b_im, -1, -2)),
                      _dirs_on_lanes(ssm_c_re), _dirs_on_lanes(ssm_c_im)], axis=2)
    wft, cct, ttt, a16 = _ssm_operators(vecs, mats)

    w_s = jnp.swapaxes(_gmlp_position_order(jnp.swapaxes(_gmlp_position_order(gmlp_w_s), 2, 3)), 2, 3).astype(BF16)
    b_s = _gmlp_position_order(gmlp_b_s)[..., None]
    rwt = router_w.T
    rb = router_b.reshape(N_EXPERTS, 1)

    new_re, new_im = [], []
    for l in range(DEPTH):
        ug, vn, xt = _input_proj(x, mod8, l, norm1_w, w_in, gmlp_ln_w, gmlp_ln_b)
        s0 = jnp.concatenate([state_ssm_re[:, l].transpose(2, 0, 1, 3).reshape(N_GROUPS, N_SAMPLE_SEQ, 128),
                              state_ssm_im[:, l].transpose(2, 0, 1, 3).reshape(N_GROUPS, N_SAMPLE_SEQ, 128)], axis=-1)
        s0 = jnp.repeat(s0, SEGS_PER_SAMPLE_SEQ, axis=1)
        yt, fin = _ssm_scan(xt, wft, cct, ttt, a16, s0, l)
        fin = fin[:, :N_PROMPT_SEQ].reshape(N_GROUPS, N_PROMPT_SEQ, 2, 2, SSM_STATE)
        fin = fin.transpose(2, 1, 3, 0, 4)
        new_re.append(fin[0])
        new_im.append(fin[1])
        hx, route = _mix_out(x, yt, ug, vn, mod8, l, w_glu, w_s, b_s, w_out, norm2_w, rwt, rb)
        final = l == DEPTH - 1
        tables = _routing_tables(route[0].astype(jnp.int32), to_sequence_order=final)
        x = _experts(tables, hx, mods, l, w_gate, w_up, w_down, final_norm_w, final)

    y_prompt = x[:T_PROMPT].reshape(N_PROMPT_SEQ, PROMPT_LEN, D_MODEL)
    y_sample = x[T_PROMPT:T_ALL].reshape(N_SAMPLE_SEQ, SAMPLE_LEN, D_MODEL)
    return (y_prompt, y_sample, jnp.stack(new_re, axis=1), jnp.stack(new_im, axis=1))
```

```python
import functools
import math

import jax
import jax.numpy as jnp
from jax import lax
from jax.experimental import pallas as pl
from jax.experimental.pallas import tpu as pltpu

F32 = jnp.float32
BF16 = jnp.bfloat16

D_MODEL = 1024
N_PROMPT_SEQ = 16
PROMPT_LEN = 256
N_SAMPLE_SEQ = 2
SAMPLE_LEN = 1024
T_PROMPT = N_PROMPT_SEQ * PROMPT_LEN
T_SAMPLE = N_SAMPLE_SEQ * SAMPLE_LEN
T_ALL = T_PROMPT + T_SAMPLE
DEPTH = 2
D_SSM = 512
SSM_GROUP = 16
N_GROUPS = 32
SSM_STATE = 64
D_GMLP = 512
GMLP_HEADS = 4
GMLP_HEAD_DIM = 128
GMLP_CHUNK = 128
N_EXPERTS = 16
N_EXPERT_GROUPS = 4
EXPERTS_PER_GROUP = 4
D_EXPERT = 512
N_MOD = 6
EPS = 1e-6

SCAN_CHUNK = 16
CHUNK_WIDTH = SCAN_CHUNK * SSM_GROUP
SEG_LEN = 256
SEG_CHUNKS = SEG_LEN // SCAN_CHUNK
N_PS = 8
TILE_TOKENS = N_PS * SEG_LEN
N_TILES = T_ALL // TILE_TOKENS
SAMPLE_TILE = T_PROMPT // TILE_TOKENS
SEGS_PER_SAMPLE_SEQ = SAMPLE_LEN // SEG_LEN
J_ROWS = SEG_CHUNKS * N_PS
ALL_J_ROWS = N_TILES * J_ROWS
GROUP_BLOCK = 8
PREP_GROUPS = 8
MOD_ROWS = 8
MOD_K_STEPS = 4

J_PER_STEP = 4
TOKEN_TILE = J_PER_STEP * J_ROWS
PAIR_SLOT_A = (0, 0, 0, 1, 1, 3)
PAIR_SLOT_B = (1, 2, 3, 3, 2, 2)
MOE_TM = 256
TAIL_STEPS = 4
TAIL_QUARTER = MOE_TM // TAIL_STEPS
N_CLASSES = N_EXPERT_GROUPS * len(PAIR_SLOT_A)
MOE_TILES = T_ALL // MOE_TM + N_CLASSES
MOE_ROWS = MOE_TILES * MOE_TM
SPARE_ROWS = N_CLASSES * TAIL_QUARTER
OUT_ROWS = T_ALL + SPARE_ROWS
ROUTE_LANES = 128
HX_WIDTH = 2 * D_MODEL + ROUTE_LANES
VMEM_LIMIT = 56 * 1024 * 1024
TRANS_B = (((1,), (1,)), ((), ()))


def _sigmoid(x):
    return 1.0 / (1.0 + jnp.exp(-x))


def _gelu_tanh(x):
    c = math.sqrt(2.0 / math.pi)
    return x * (0.5 * (1.0 + jnp.tanh(c * (x + 0.044715 * (x * x * x)))))


def _split_bf16(a):
    hi = a.astype(BF16)
    return hi, (a - hi.astype(F32)).astype(BF16)


def _rmsnorm(x, w):
    return x * lax.rsqrt(jnp.mean(x * x, axis=-1, keepdims=True) + EPS) * w


def _per_ps(fn, a, *mods):
    rows, d = a.shape
    out = fn(a.reshape(rows // N_PS, N_PS, d), *[m[None] for m in mods])
    return out.reshape(rows, d)


def _params(*sem):
    return pltpu.CompilerParams(dimension_semantics=sem, vmem_limit_bytes=VMEM_LIMIT)


def _mod_kernel(c_ref, w_ref, b_ref, o_ref):
    c = c_ref[...]
    part = jnp.dot((c * _sigmoid(c)).astype(BF16), w_ref[...].astype(BF16), preferred_element_type=F32)

    @pl.when(pl.program_id(1) == 0)
    def _():
        o_ref[...] = part + b_ref[...]

    @pl.when(pl.program_id(1) > 0)
    def _():
        o_ref[...] += part


def _first_step():
    return jnp.logical_and(pl.program_id(0) == 0, pl.program_id(1) == 0)


def _in_kernel(x_ref, mod_ref, nw_ref, w_ref, lnw_ref, lnb_ref, ug_ref, vn_ref, xt_ref, wg_ref, wst_ref):
    @pl.when(_first_step())
    def _():
        wg_ref[...] = w_ref[:, D_SSM:].astype(BF16)
        wst_ref[...] = w_ref[:, :D_SSM].T.astype(BF16)

    y = _rmsnorm(x_ref[...], nw_ref[...])
    h = _per_ps(lambda a, sc, sh: a * (1.0 + sc) + sh, y, mod_ref[1], mod_ref[0]).astype(BF16)
    zg = _gelu_tanh(jnp.dot(h, wg_ref[...], preferred_element_type=F32))
    v = zg[:, D_GMLP:]
    mu = jnp.mean(v, axis=-1, keepdims=True)
    vc = v - mu
    var = jnp.mean(vc * vc, axis=-1, keepdims=True)
    vn = vc * lax.rsqrt(var + EPS) * lnw_ref[...] + lnb_ref[...]
    for hd in range(GMLP_HEADS):
        cols = slice(hd * GMLP_HEAD_DIM, (hd + 1) * GMLP_HEAD_DIM)
        ug_ref[hd] = zg[:, cols]
        vn_ref[hd] = vn[:, cols]
    xt = lax.dot_general(wst_ref[...], h, TRANS_B, preferred_element_type=F32).astype(BF16)
    for k in range(J_PER_STEP):
        xt_ref[k] = xt[:, k * J_ROWS:(k + 1) * J_ROWS]


def _input_proj(x, mod8, l, norm1_w, w_in, ln_w, ln_b):
    tm = TOKEN_TILE
    steps = TILE_TOKENS // tm
    tok = lambda n: pl.BlockSpec((tm, n), lambda t, s: (t * steps + s, 0))
    heads = pl.BlockSpec((GMLP_HEADS, tm, GMLP_HEAD_DIM), lambda t, s: (0, t * steps + s, 0))
    lay = lambda *shape: pl.BlockSpec((None,) + shape, lambda t, s: (l,) + (0,) * len(shape))
    return pl.pallas_call(
        _in_kernel,
        grid=(N_TILES, steps),
        in_specs=[
            tok(D_MODEL),
            pl.BlockSpec((None, None, N_MOD, N_PS, D_MODEL), lambda t, s: (l, t, 0, 0, 0)),
            lay(1, D_MODEL), lay(D_MODEL, D_SSM + 2 * D_GMLP), lay(1, D_GMLP), lay(1, D_GMLP),
        ],
        out_specs=[heads, heads,
                   pl.BlockSpec((J_PER_STEP, D_SSM, J_ROWS), lambda t, s: (s, 0, t))],
        out_shape=[jax.ShapeDtypeStruct((GMLP_HEADS, T_ALL, GMLP_HEAD_DIM), F32)] * 2 + [
            jax.ShapeDtypeStruct((SCAN_CHUNK, D_SSM, ALL_J_ROWS), BF16)],
        scratch_shapes=[pltpu.VMEM((D_MODEL, 2 * D_GMLP), BF16), pltpu.VMEM((D_SSM, D_MODEL), BF16)],
        compiler_params=_params("arbitrary", "arbitrary"),
        name="norm1_in_proj",
    )(x, mod8, norm1_w.reshape(DEPTH, 1, D_MODEL), w_in,
      ln_w.reshape(DEPTH, 1, D_GMLP), ln_b.reshape(DEPTH, 1, D_GMLP))


def _shift_lanes_right(a, b, s, lane):
    if s == 0:
        return a, b
    if s == 128:
        return jnp.zeros_like(a), a
    if s < 128:
        ra = pltpu.roll(a, s, 1)
        rb = pltpu.roll(b, s, 1)
        return jnp.where(lane >= s, ra, 0.0), jnp.where(lane >= s, rb, ra)
    t = s - 128
    return jnp.zeros_like(a), jnp.where(lane >= t, pltpu.roll(a, t, 1), 0.0)


def _shift_lanes_left(a, b, s, lane):
    if s == 0:
        return a, b
    if s == 128:
        return b, jnp.zeros_like(b)
    if s < 128:
        ra = pltpu.roll(a, 128 - s, 1)
        rb = pltpu.roll(b, 128 - s, 1)
        return jnp.where(lane < 128 - s, ra, rb), jnp.where(lane < 128 - s, rb, 0.0)
    t = s - 128
    return jnp.where(lane < 128 - t, pltpu.roll(b, 128 - t, 1), 0.0), jnp.zeros_like(b)


def _prep_kernel(*refs):
    for g in range(PREP_GROUPS):
        _prep_group(*[r.at[g] for r in refs])


def _prep_group(vec_ref, mat_ref, wft_ref, cct_ref, ttt_ref, a16_ref, wf_scr, cm_scr, tt_scr):
    a_re = vec_ref[0:1, :]
    a_im = vec_ref[1:2, :]
    dt = jnp.exp(vec_ref[2:3, :])
    d_skip = vec_ref[3:4, :]
    mag = jnp.exp(a_re * dt)
    ang = a_im * dt
    ab_r = mag * jnp.cos(ang)
    ab_i = mag * jnp.sin(ang)
    den = a_re * a_re + a_im * a_im
    nr = ab_r - 1.0
    q_r = (nr * a_re + ab_i * a_im) / den
    q_i = (ab_i * a_re - nr * a_im) / den
    bt_r = mat_ref[0]
    bt_i = mat_ref[1]
    c_r = mat_ref[2]
    c_i = mat_ref[3]
    bb_r = q_r * bt_r - q_i * bt_i
    bb_i = q_r * bt_i + q_i * bt_r
    p_r = [jnp.ones_like(ab_r)]
    p_i = [jnp.zeros_like(ab_r)]
    for _ in range(SCAN_CHUNK):
        pr, pi = p_r[-1], p_i[-1]
        p_r.append(pr * ab_r - pi * ab_i)
        p_i.append(pr * ab_i + pi * ab_r)
    a16_ref[0] = jnp.broadcast_to(p_r[SCAN_CHUNK], (N_PS, 128))
    a16_ref[1] = jnp.broadcast_to(p_i[SCAN_CHUNK], (N_PS, 128))

    lane = lax.broadcasted_iota(jnp.int32, (1, 128), 1)
    is_fwd = lane < SSM_STATE

    def pick(mf, mb):
        return jnp.where(is_fwd, p_r[mf], p_r[mb]), jnp.where(is_fwd, p_i[mf], p_i[mb])

    for j in range(SCAN_CHUNK):
        rows = slice(j * SSM_GROUP, (j + 1) * SSM_GROUP)
        wr, wi = pick(SCAN_CHUNK - 1 - j, j)
        wf_scr[rows, 0:128] = bb_r * wr - bb_i * wi
        wf_scr[rows, 128:256] = bb_r * wi + bb_i * wr
        wr, wi = pick(j + 1, SCAN_CHUNK - j)
        cct_ref[rows, 0:128] = (c_r * wr - c_i * wi).astype(BF16)
        cct_ref[rows, 128:256] = (-(c_r * wi + c_i * wr)).astype(BF16)
        wr, wi = pick(j, SCAN_CHUNK - 1 - j)
        cm_scr[rows, 0:128] = c_r * wr - c_i * wi
        cm_scr[rows, 128:256] = c_r * wi + c_i * wr
    wft_ref[...] = wf_scr[...].T.astype(BF16)

    zero = jnp.zeros_like(bb_r)
    cm_hi, cm_lo = _split_bf16(cm_scr[...])
    dot_t = lambda a, b: lax.dot_general(a, b, TRANS_B, preferred_element_type=F32)

    def lag_rows(keep):
        lhs = jnp.concatenate([jnp.where(keep, bb_r, zero), jnp.where(keep, -bb_i, zero)], axis=1)
        hi, lo = _split_bf16(lhs)
        return dot_t(hi, cm_hi) + (dot_t(hi, cm_lo) + dot_t(lo, cm_hi))

    mf = lag_rows(is_fwd)
    mb = lag_rows(jnp.logical_not(is_fwd))
    mf_a, mf_b = mf[:, 0:128], mf[:, 128:256]
    mb_a, mb_b = mb[:, 0:128], mb[:, 128:256]
    row_h = lax.broadcasted_iota(jnp.int32, (SSM_GROUP, 128), 0)
    lane_h = lax.broadcasted_iota(jnp.int32, (SSM_GROUP, 128), 1)
    for jp in range(SCAN_CHUNK):
        rows = slice(jp * SSM_GROUP, (jp + 1) * SSM_GROUP)
        fa, fb = _shift_lanes_right(mf_a, mf_b, SSM_GROUP * jp, lane)
        ba, bb = _shift_lanes_left(mb_a, mb_b, SSM_GROUP * (SCAN_CHUNK - 1 - jp), lane)
        diag = SSM_GROUP * jp + row_h
        tt_scr[rows, 0:128] = fa + ba + jnp.where(lane_h == diag, d_skip, 0.0)
        tt_scr[rows, 128:256] = fb + bb + jnp.where(lane_h + 128 == diag, d_skip, 0.0)
    ttt_ref[...] = tt_scr[...].T.astype(BF16)


def _mod_and_operators_kernel(c_ref, w_ref, b_ref, vec_ref, mat_ref, o_ref, *operator_refs):
    _mod_kernel(c_ref, w_ref, b_ref, o_ref)
    _prep_kernel(vec_ref, mat_ref, *operator_refs)


def _modulation_and_operators(cvec, w_mod, b_mod, vecs, mats):
    assert N_GROUPS // PREP_GROUPS == MOD_K_STEPS
    kb = D_MODEL // MOD_K_STEPS
    c_blocks = cvec.reshape(MOD_ROWS, MOD_K_STEPS, kb).transpose(1, 0, 2)
    op = jax.ShapeDtypeStruct((DEPTH, N_GROUPS, CHUNK_WIDTH, CHUNK_WIDTH), BF16)
    pg = PREP_GROUPS
    op_spec = pl.BlockSpec((None, pg, CHUNK_WIDTH, CHUNK_WIDTH), lambda l, g: (l, g, 0, 0))
    sq = pltpu.VMEM((pg, CHUNK_WIDTH, CHUNK_WIDTH), F32)
    return pl.pallas_call(
        _mod_and_operators_kernel,
        grid=(DEPTH, MOD_K_STEPS),
        in_specs=[
            pl.BlockSpec((None, MOD_ROWS, kb), lambda l, k: (k, 0, 0)),
            pl.BlockSpec((None, kb, N_MOD * D_MODEL), lambda l, k: (l, k, 0)),
            pl.BlockSpec((None, 1, N_MOD * D_MODEL), lambda l, k: (l, 0, 0)),
            pl.BlockSpec((None, pg, 8, 128), lambda l, g: (l, g, 0, 0)),
            pl.BlockSpec((None, pg, 4, SSM_GROUP, 128), lambda l, g: (l, g, 0, 0, 0)),
        ],
        out_specs=[pl.BlockSpec((None, MOD_ROWS, N_MOD * D_MODEL), lambda l, k: (l, 0, 0)),
                   op_spec, op_spec, op_spec,
                   pl.BlockSpec((None, pg, 2, N_PS, 128), lambda l, g: (l, g, 0, 0, 0))],
        out_shape=[jax.ShapeDtypeStruct((DEPTH, MOD_ROWS, N_MOD * D_MODEL), F32),
                   op, op, op, jax.ShapeDtypeStruct((DEPTH, N_GROUPS, 2, N_PS, 128), F32)],
        scratch_shapes=[sq, sq, sq],
        compiler_params=_params("arbitrary", "arbitrary"),
        name="adaln_and_s5_operators",
    )(c_blocks, w_mod, b_mod.reshape(DEPTH, 1, N_MOD * D_MODEL), vecs, mats)


def _ssm_kernel(xt_ref, wft_ref, cct_ref, ttt_ref, a16_ref, s0_ref, yt_ref, fin_ref, s_scr, f_scr, ft_scr):
    gb = GROUP_BLOCK
    is_fwd = lax.broadcasted_iota(jnp.int32, (1, 1, 128), 2) < SSM_STATE
    half = SSM_STATE

    def group_x(gl):
        return xt_ref[:, gl * SSM_GROUP:(gl + 1) * SSM_GROUP, :].reshape(CHUNK_WIDTH, ALL_J_ROWS)

    for gl in range(gb):
        ft_scr[gl] = jnp.dot(wft_ref[gl], group_x(gl), preferred_element_type=F32)
        f_scr[gl] = ft_scr[gl].T

    def scan(tiles, s_r, s_i):
        rows = len(tiles) * N_PS
        a_r = jnp.broadcast_to(a16_ref[:, 0, 0:1, :], (gb, rows, 128))
        a_i = jnp.broadcast_to(a16_ref[:, 1, 0:1, :], (gb, rows, 128))
        for i in range(SEG_CHUNKS):
            f_r, f_i = [], []
            for k, t in enumerate(tiles):
                st = slice(k * N_PS, (k + 1) * N_PS)
                rf = slice(t * J_ROWS + i * N_PS, t * J_ROWS + (i + 1) * N_PS)
                rb = slice(t * J_ROWS + (SEG_CHUNKS - 1 - i) * N_PS, t * J_ROWS + (SEG_CHUNKS - i) * N_PS)
                s_scr[:, rf, 0:half] = s_r[:, st, 0:half]
                s_scr[:, rf, 128:128 + half] = s_i[:, st, 0:half]
                s_scr[:, rb, half:128] = s_r[:, st, half:128]
                s_scr[:, rb, 128 + half:256] = s_i[:, st, half:128]
                f_r.append(jnp.where(is_fwd, f_scr[:, rf, 0:128], f_scr[:, rb, 0:128]))
                f_i.append(jnp.where(is_fwd, f_scr[:, rf, 128:256], f_scr[:, rb, 128:256]))
            f_r = jnp.concatenate(f_r, axis=1)
            f_i = jnp.concatenate(f_i, axis=1)
            s_r, s_i = a_r * s_r - a_i * s_i + f_r, a_r * s_i + a_i * s_r + f_i
        return s_r, s_i

    zeros = jnp.zeros((gb, N_TILES * N_PS, 128), F32)
    z_r, z_i = scan(list(range(N_TILES)), zeros, zeros)
    fin_ref[:, :, 0:128] = z_r
    fin_ref[:, :, 128:256] = z_i

    z_r = z_r[:, SAMPLE_TILE * N_PS:, :]
    z_i = z_i[:, SAMPLE_TILE * N_PS:, :]
    b_r, b_i = a16_ref[:, 0], a16_ref[:, 1]
    for _ in range(4):
        b_r, b_i = b_r * b_r - b_i * b_i, 2.0 * (b_r * b_i)
    seg = lax.broadcasted_iota(jnp.int32, (gb, N_PS, 128), 1) % SEGS_PER_SAMPLE_SEQ
    i_r = s0_ref[:, :, 0:128]
    i_i = s0_ref[:, :, 128:256]
    for step in range(1, SEGS_PER_SAMPLE_SEQ):
        pr = jnp.where(is_fwd, pltpu.roll(i_r, 1, 1), pltpu.roll(i_r, N_PS - 1, 1))
        pi = jnp.where(is_fwd, pltpu.roll(i_i, 1, 1), pltpu.roll(i_i, N_PS - 1, 1))
        zr = jnp.where(is_fwd, pltpu.roll(z_r, 1, 1), pltpu.roll(z_r, N_PS - 1, 1))
        zi = jnp.where(is_fwd, pltpu.roll(z_i, 1, 1), pltpu.roll(z_i, N_PS - 1, 1))
        n_r = b_r * pr - b_i * pi + zr
        n_i = b_r * pi + b_i * pr + zi
        first = jnp.where(is_fwd, step, 0)
        last = jnp.where(is_fwd, SEGS_PER_SAMPLE_SEQ - 1, SEGS_PER_SAMPLE_SEQ - 1 - step)
        upd = jnp.logical_and(seg >= first, seg <= last)
        i_r = jnp.where(upd, n_r, i_r)
        i_i = jnp.where(upd, n_i, i_i)
    scan([SAMPLE_TILE], i_r, i_i)

    for gl in range(gb):
        yt = jnp.dot(ttt_ref[gl], group_x(gl), preferred_element_type=F32)
        yt += lax.dot_general(cct_ref[gl], s_scr[gl].astype(BF16), TRANS_B, preferred_element_type=F32)
        yt_ref[:, gl * SSM_GROUP:(gl + 1) * SSM_GROUP, :] = yt.reshape(SCAN_CHUNK, SSM_GROUP, ALL_J_ROWS)


def _ssm_scan(xt, wft, cct, ttt, a16, s0, l):
    gb = GROUP_BLOCK
    op_spec = pl.BlockSpec((None, gb, CHUNK_WIDTH, CHUNK_WIDTH), lambda g: (l, g, 0, 0))
    io_spec = pl.BlockSpec((SCAN_CHUNK, gb * SSM_GROUP, ALL_J_ROWS), lambda g: (0, g, 0))
    rows = pltpu.VMEM((gb, ALL_J_ROWS, CHUNK_WIDTH), F32)
    return pl.pallas_call(
        _ssm_kernel,
        grid=(N_GROUPS // gb,),
        in_specs=[
            io_spec, op_spec, op_spec, op_spec,
            pl.BlockSpec((None, gb, 2, N_PS, 128), lambda g: (l, g, 0, 0, 0)),
            pl.BlockSpec((gb, N_PS, CHUNK_WIDTH), lambda g: (g, 0, 0)),
        ],
        out_specs=[io_spec, pl.BlockSpec((gb, N_TILES * N_PS, CHUNK_WIDTH), lambda g: (g, 0, 0))],
        out_shape=[
            jax.ShapeDtypeStruct((SCAN_CHUNK, D_SSM, ALL_J_ROWS), F32),
            jax.ShapeDtypeStruct((N_GROUPS, N_TILES * N_PS, CHUNK_WIDTH), F32),
        ],
        scratch_shapes=[rows, rows, pltpu.VMEM((gb, CHUNK_WIDTH, ALL_J_ROWS), F32)],
        compiler_params=_params("arbitrary"),
        name="s5_chunk_scan",
    )(xt, wft, cct, ttt, a16, s0)


def _post_kernel(x_ref, yt_ref, ug_ref, vn_ref, mod_ref, wglu_ref, ws_ref, bs_ref, wout_f32_ref,
                 nw_ref, rwt_ref, rb_ref, hx_ref, route_ref, yg_scr, y_scr, wglut_ref, wout_ref):
    tile = pl.program_id(0)
    step = pl.program_id(1)

    @pl.when(_first_step())
    def _():
        wglut_ref[...] = wglu_ref[...].T.astype(BF16)
        wout_ref[...] = wout_f32_ref[...].astype(BF16)

    @pl.when(step == 0)
    def _():
        def chunk(n, carry):
            ps = n % N_PS
            c_hi = n // N_PS
            base = c_hi * (GMLP_CHUNK // SCAN_CHUNK) * N_PS + ps
            rows = [pl.ds(j * J_ROWS + base, GMLP_CHUNK // SCAN_CHUNK, stride=N_PS) for j in range(SCAN_CHUNK)]
            for h in range(GMLP_HEADS):
                v = jnp.concatenate([vn_ref[h, r, :] for r in rows], axis=0).astype(BF16)
                u = jnp.concatenate([ug_ref[h, r, :] for r in rows], axis=0)
                s = jnp.dot(ws_ref[h], v, preferred_element_type=F32) + bs_ref[h]
                yg = u * s
                for j, r in enumerate(rows):
                    yg_scr[h, r, :] = yg[j * 8:(j + 1) * 8]
            return carry

        for n in range(TILE_TOKENS // GMLP_CHUNK):
            chunk(n, 0)

    yt = _gelu_tanh(jnp.concatenate([yt_ref[k] for k in range(J_PER_STEP)], axis=1))
    yt = yt * _sigmoid(jnp.dot(wglut_ref[...], yt.astype(BF16), preferred_element_type=F32))
    for k in range(J_PER_STEP):
        y_scr[k * J_ROWS:(k + 1) * J_ROWS, :] = yt[:, k * J_ROWS:(k + 1) * J_ROWS].T
    row0 = pl.multiple_of(step * TOKEN_TILE, TOKEN_TILE)
    proj = jnp.dot(y_scr[...].astype(BF16), wout_ref[0:D_SSM, :], preferred_element_type=F32)
    yg = jnp.concatenate([yg_scr[h, pl.ds(row0, TOKEN_TILE), :] for h in range(GMLP_HEADS)], axis=1)
    proj += jnp.dot(yg.astype(BF16), wout_ref[D_SSM:, :], preferred_element_type=F32)
    x1 = x_ref[...] + _per_ps(lambda a, g: a * g, proj, mod_ref[2])
    h2 = _per_ps(lambda a, sc, sh: a * (1.0 + sc) + sh, _rmsnorm(x1, nw_ref[...]), mod_ref[4], mod_ref[3])
    hx_ref[:, 0:D_MODEL] = h2
    hx_ref[:, D_MODEL:2 * D_MODEL] = x1
    r_hi, r_lo = _split_bf16(rwt_ref[...])
    h_hi, h_lo = _split_bf16(h2)
    dot_t = lambda a, b: lax.dot_general(a, b, TRANS_B, preferred_element_type=F32)
    logits = dot_t(r_hi, h_hi) + (dot_t(r_hi, h_lo) + dot_t(r_lo, h_hi))
    scores = _sigmoid(logits)
    sel = scores + rb_ref[...]
    sc = [scores[e:e + 1, :] for e in range(N_EXPERTS)]
    sl = [sel[e:e + 1, :] for e in range(N_EXPERTS)]
    gscore = []
    for g in range(N_EXPERT_GROUPS):
        v0, v1, v2, v3 = sl[4 * g:4 * g + 4]
        hi01, lo01 = jnp.maximum(v0, v1), jnp.minimum(v0, v1)
        hi23, lo23 = jnp.maximum(v2, v3), jnp.minimum(v2, v3)
        top1 = jnp.maximum(hi01, hi23)
        top2 = jnp.maximum(jnp.minimum(hi01, hi23), jnp.maximum(lo01, lo23))
        gscore.append(top1 + top2)
    best = gscore[0]
    gidx = jnp.zeros_like(best, dtype=jnp.int32)
    for g in range(1, N_EXPERT_GROUPS):
        upd = gscore[g] > best
        gidx = jnp.where(upd, g, gidx)
        best = jnp.where(upd, gscore[g], best)

    def in_group(vals, k):
        out = vals[k]
        for g in range(1, N_EXPERT_GROUPS):
            out = jnp.where(gidx == g, vals[4 * g + k], out)
        return out

    v = [in_group(sl, k) for k in range(EXPERTS_PER_GROUP)]
    s = [in_group(sc, k) for k in range(EXPERTS_PER_GROUP)]
    w = []
    bits = jnp.zeros_like(gidx)
    for k in range(EXPERTS_PER_GROUP):
        rank = jnp.zeros_like(gidx)
        for j in range(EXPERTS_PER_GROUP):
            if j == k:
                continue
            ahead = (v[j] >= v[k]) if j < k else (v[j] > v[k])
            rank = rank + ahead.astype(jnp.int32)
        w.append(jnp.where(rank < 2, s[k], 0.0))
        bits = bits + jnp.where(rank < 2, 1 << k, 0)
    denom = (w[0] + w[1]) + (w[2] + w[3])
    gate = [wk / denom for wk in w]
    pair = jnp.full_like(gidx, len(PAIR_SLOT_A) - 1)
    for p in range(len(PAIR_SLOT_A) - 1):
        pair = jnp.where(bits == (1 << PAIR_SLOT_A[p]) + (1 << PAIR_SLOT_B[p]), p, pair)

    def slot_gate(table):
        out = gate[table[0]]
        for p in range(1, len(table)):
            out = jnp.where(pair == p, gate[table[p]], out)
        return out

    n_tok = route_ref.shape[1]
    route_ref[0:1, :] = (gidx * len(PAIR_SLOT_A) + pair).astype(F32)
    ps = lax.broadcasted_iota(jnp.int32, (1, n_tok), 1) % N_PS
    cond = jnp.where(tile < SAMPLE_TILE, 0, 1 + ps // SEGS_PER_SAMPLE_SEQ).astype(F32)
    lanes = jnp.concatenate([slot_gate(PAIR_SLOT_A), slot_gate(PAIR_SLOT_B), cond,
                             jnp.zeros((ROUTE_LANES - 3, n_tok), F32)], axis=0)
    hx_ref[:, 2 * D_MODEL:] = lanes.T


def _mix_out(x, yt, ug, vn, mod8, l, w_glu, w_s, b_s, w_out, norm2_w, rwt, rb):
    tm = TOKEN_TILE
    steps = TILE_TOKENS // tm
    tok = lambda n: pl.BlockSpec((tm, n), lambda t, s: (t * steps + s, 0))
    whole_tile = pl.BlockSpec((GMLP_HEADS, TILE_TOKENS, GMLP_HEAD_DIM), lambda t, s: (0, t, 0))
    lay = lambda *shape: pl.BlockSpec((None,) + shape, lambda t, s: (l,) + (0,) * len(shape))
    return pl.pallas_call(
        _post_kernel,
        grid=(N_TILES, steps),
        in_specs=[
            tok(D_MODEL),
            pl.BlockSpec((J_PER_STEP, D_SSM, J_ROWS), lambda t, s: (s, 0, t)),
            whole_tile, whole_tile,
            pl.BlockSpec((None, None, N_MOD, N_PS, D_MODEL), lambda t, s: (l, t, 0, 0, 0)),
            lay(D_SSM, D_SSM), lay(GMLP_HEADS, GMLP_CHUNK, GMLP_CHUNK),
            lay(GMLP_HEADS, GMLP_CHUNK, 1), lay(D_MODEL, D_MODEL), lay(1, D_MODEL),
            pl.BlockSpec((N_EXPERTS, D_MODEL), lambda t, s: (0, 0)),
            pl.BlockSpec((N_EXPERTS, 1), lambda t, s: (0, 0)),
        ],
        out_specs=[tok(HX_WIDTH), pl.BlockSpec((1, tm), lambda t, s: (0, t * steps + s))],
        out_shape=[
            jax.ShapeDtypeStruct((T_ALL, HX_WIDTH), F32),
            jax.ShapeDtypeStruct((1, T_ALL), F32),
        ],
        scratch_shapes=[pltpu.VMEM((GMLP_HEADS, TILE_TOKENS, GMLP_HEAD_DIM), F32), pltpu.VMEM((tm, D_SSM), F32),
                        pltpu.VMEM((D_SSM, D_SSM), BF16), pltpu.VMEM((D_MODEL, D_MODEL), BF16)],
        compiler_params=_params("arbitrary", "arbitrary"),
        name="mixers_out_router",
    )(x, yt, ug, vn, mod8, w_glu, w_s, b_s, w_out, norm2_w.reshape(DEPTH, 1, D_MODEL), rwt, rb)


def _moe_kernel(ea_ref, eb_ref, nv_ref, size_ref, src_ref, drow_ref,
                hx_hbm, wga_ref, wua_ref, wda_ref, wgb_ref, wub_ref, wdb_ref, g2_ref, fw_ref,
                o_hbm, hx_buf, o_buf, z_buf, g_sem, s_sem, z_sem, *, final):
    n = pl.program_id(0)
    n_valid = nv_ref[0]
    slot = n % 2

    def by_size(tile, fn):
        for quarters in range(1, TAIL_STEPS + 1):
            @pl.when(size_ref[tile] == quarters)
            def _(rows=quarters * TAIL_QUARTER):
                fn(rows)

    def start_gather(tile, sl, rows):
        base = tile * MOE_TM
        for r in range(rows):
            pltpu.make_async_copy(hx_hbm.at[pl.ds(src_ref[base + r], 1)], hx_buf.at[sl, pl.ds(r, 1)],
                                  g_sem.at[sl]).start()

    def wait_gather(sl, rows):
        pltpu.make_async_copy(hx_hbm.at[pl.ds(0, rows)], hx_buf.at[sl, pl.ds(0, rows)], g_sem.at[sl]).wait()

    def start_scatter(tile, sl, rows):
        base = tile * MOE_TM
        for r in range(rows):
            pltpu.make_async_copy(o_buf.at[sl, pl.ds(r, 1)], o_hbm.at[pl.ds(drow_ref[base + r], 1)],
                                  s_sem.at[sl]).start()

    def wait_scatter(sl, rows):
        pltpu.make_async_copy(o_buf.at[sl, pl.ds(0, rows)], o_hbm.at[pl.ds(0, rows)], s_sem.at[sl]).wait()

    @pl.when(n == 0)
    def _():
        z_buf[...] = jnp.zeros_like(z_buf)
        spare = [pltpu.make_async_copy(z_buf, o_hbm.at[pl.ds(T_ALL + i * MOE_TM, MOE_TM)], z_sem)
                 for i in range(SPARE_ROWS // MOE_TM)]
        for cp in spare:
            cp.start()
        for cp in spare:
            cp.wait()
        by_size(0, lambda rows: start_gather(0, 0, rows))

    @pl.when(n + 1 < n_valid)
    def _():
        by_size(n + 1, lambda rows: start_gather(n + 1, 1 - slot, rows))

    def tile_step(rows):
        wait_gather(slot, rows)

        @pl.when(n >= 2)
        def _():
            by_size(n - 2, lambda r: wait_scatter(slot, r))

        h = hx_buf[slot, 0:rows, 0:D_MODEL].astype(BF16)
        lanes = hx_buf[slot, 0:rows, 2 * D_MODEL:]

        def expert(wg, wu, wd, gate):
            hg = jnp.dot(h, wg[...].astype(BF16), preferred_element_type=F32)
            hu = jnp.dot(h, wu[...].astype(BF16), preferred_element_type=F32)
            act = hg * _sigmoid(hg) * hu * gate
            return jnp.dot(act.astype(BF16), wd[...].astype(BF16), preferred_element_type=F32)

        y = expert(wga_ref, wua_ref, wda_ref, lanes[:, 0:1]) + expert(wgb_ref, wub_ref, wdb_ref, lanes[:, 1:2])
        cond_row = lanes[:, 2:3]
        gate2 = jnp.where(cond_row == 0.0, g2_ref[0:1, :], jnp.where(cond_row == 1.0, g2_ref[1:2, :], g2_ref[2:3, :]))
        x2 = hx_buf[slot, 0:rows, D_MODEL:2 * D_MODEL] + gate2 * y
        o_buf[slot, 0:rows] = _rmsnorm(x2, fw_ref[...]) if final else x2
        start_scatter(n, slot, rows)

    @pl.when(n < n_valid)
    def _():
        by_size(n, tile_step)

    @pl.when(n == MOE_TILES - 1)
    def _():
        last = n_valid - 1
        by_size(last, lambda r: wait_scatter(last % 2, r))
        by_size(last - 1, lambda r: wait_scatter(1 - last % 2, r))


def _experts(tables, hx, mods, l, w_gate, w_up, w_down, final_w, final):
    w_a = lambda r, c: pl.BlockSpec((None, None, r, c), lambda n, ea, eb, *_: (l, ea[n], 0, 0))
    w_b = lambda r, c: pl.BlockSpec((None, None, r, c), lambda n, ea, eb, *_: (l, eb[n], 0, 0))
    return pl.pallas_call(
        functools.partial(_moe_kernel, final=final),
        grid_spec=pltpu.PrefetchScalarGridSpec(
            num_scalar_prefetch=len(tables),
            grid=(MOE_TILES,),
            in_specs=[
                pl.BlockSpec(memory_space=pl.ANY),
                w_a(D_MODEL, D_EXPERT), w_a(D_MODEL, D_EXPERT), w_a(D_EXPERT, D_MODEL),
                w_b(D_MODEL, D_EXPERT), w_b(D_MODEL, D_EXPERT), w_b(D_EXPERT, D_MODEL),
                pl.BlockSpec((None, MOD_ROWS, D_MODEL), lambda n, *_: (l, 0, N_MOD - 1)),
                pl.BlockSpec((1, D_MODEL), lambda n, *_: (0, 0)),
            ],
            out_specs=pl.BlockSpec(memory_space=pl.ANY),
            scratch_shapes=[pltpu.VMEM((2, MOE_TM, HX_WIDTH), F32), pltpu.VMEM((2, MOE_TM, D_MODEL), F32),
                            pltpu.VMEM((MOE_TM, D_MODEL), F32),
                            pltpu.SemaphoreType.DMA((2,)), pltpu.SemaphoreType.DMA((2,)), pltpu.SemaphoreType.DMA(())],
        ),
        out_shape=jax.ShapeDtypeStruct((OUT_ROWS, D_MODEL), F32),
        compiler_params=_params("arbitrary"),
        name="experts",
    )(*tables, hx, w_gate, w_up, w_down, w_gate, w_up, w_down, mods, final_w.reshape(1, D_MODEL))


def _count_before(flags):
    n, k = flags.shape
    blocks = flags.reshape(n // 128, 128, k).astype(F32)
    strictly_lower = jnp.tril(jnp.ones((128, 128), F32), -1)
    within = jnp.einsum("ij,bjk->bik", strictly_lower, blocks)
    totals = jnp.sum(blocks, axis=1)
    before = jnp.cumsum(totals, axis=0) - totals
    return (within + before[:, None, :]).reshape(n, k).astype(jnp.int32)


def _routing_tables(cls, to_sequence_order):
    n_cls = N_EXPERT_GROUPS * len(PAIR_SLOT_A)
    onehot = (cls[:, None] == jnp.arange(n_cls, dtype=jnp.int32)[None, :]).astype(jnp.int32)
    counts = jnp.sum(onehot, axis=0)
    tiles = (counts + MOE_TM - 1) // MOE_TM
    tile_end = jnp.cumsum(tiles)
    row_in_class = (tile_end - tiles)[None, :] * MOE_TM + _count_before(onehot)
    dst = jnp.sum(onehot * row_in_class, axis=1)
    token_plus_1 = jnp.zeros((MOE_ROWS,), jnp.int32).at[dst].set(jnp.arange(1, T_ALL + 1, dtype=jnp.int32))
    is_pad = token_plus_1 == 0
    src = jnp.maximum(token_plus_1 - 1, 0)
    n_valid = tile_end[-1]
    tile_id = jnp.minimum(jnp.arange(MOE_TILES, dtype=jnp.int32), n_valid - 1)
    tile_cls = jnp.sum((tile_end[None, :] <= tile_id[:, None]).astype(jnp.int32), axis=1)
    group, pair = tile_cls // len(PAIR_SLOT_A), tile_cls % len(PAIR_SLOT_A)
    in_cls = (tile_cls[:, None] == jnp.arange(n_cls, dtype=jnp.int32)[None, :]).astype(jnp.int32)
    left = jnp.sum(in_cls * (counts[None, :] - (tile_id[:, None] - (tile_end - tiles)[None, :]) * MOE_TM), axis=1)
    size = jnp.clip((left + TAIL_QUARTER - 1) // TAIL_QUARTER, 1, TAIL_STEPS).astype(jnp.int32)
    row = jnp.arange(MOE_ROWS, dtype=jnp.int32)
    processed = jnp.logical_and(row % MOE_TM < jnp.repeat(size, MOE_TM) * TAIL_QUARTER, row // MOE_TM < n_valid)
    moved_pad = jnp.logical_and(is_pad, processed)
    spare = T_ALL + _count_before(moved_pad.astype(jnp.int32)[:, None])[:, 0]
    if to_sequence_order:
        j, c, ps = (src // J_ROWS) % SCAN_CHUNK, (src // N_PS) % SEG_CHUNKS, src % N_PS
        target = (src // TILE_TOKENS) * TILE_TOKENS + ps * SEG_LEN + c * SCAN_CHUNK + j
    else:
        target = src
    drow = jnp.where(is_pad, jnp.where(moved_pad, spare, T_ALL), target).astype(jnp.int32)

    def slot_expert(table):
        local = jnp.full_like(pair, table[-1])
        for p in range(len(table) - 1):
            local = jnp.where(pair == p, table[p], local)
        return group * EXPERTS_PER_GROUP + local

    e_a = slot_expert(PAIR_SLOT_A)
    e_b = slot_expert(PAIR_SLOT_B)
    return e_a, e_b, n_valid.reshape(1).astype(jnp.int32), size, src, drow


def _dirs_on_lanes(p):
    p = jnp.moveaxis(p, 1, -2)
    return p.reshape(p.shape[:-2] + (2 * SSM_STATE,))


def _to_internal_order(x):
    x = x.reshape(N_TILES, N_PS, SEG_CHUNKS, SCAN_CHUNK, D_MODEL)
    return x.transpose(0, 3, 2, 1, 4).reshape(T_ALL, D_MODEL)


def _gmlp_position_order(w):
    n_lo = GMLP_CHUNK // SCAN_CHUNK
    lead = w.shape[:2]
    w = w.reshape(lead + (n_lo, SCAN_CHUNK) + w.shape[3:])
    w = jnp.swapaxes(w, 2, 3)
    return w.reshape(lead + (GMLP_CHUNK,) + w.shape[4:])


def kernel(x_prompt, x_sample, c, state_ssm_re, state_ssm_im, c_ctx, norm1_w, norm2_w, w_mod, b_mod, w_in,
           ssm_a_re, ssm_a_im, ssm_log_dt, ssm_b_re, ssm_b_im, ssm_c_re, ssm_c_im, ssm_d, w_glu,
           gmlp_ln_w, gmlp_ln_b, gmlp_w_s, gmlp_b_s, w_out, router_w, router_b, w_gate, w_up, w_down,
           final_norm_w):
    x = jnp.concatenate([x_prompt.reshape(T_PROMPT, D_MODEL), x_sample.reshape(T_SAMPLE, D_MODEL)], axis=0)
    x = _to_internal_order(x)

    cvec = jnp.concatenate([c_ctx[None, :], c, jnp.zeros((MOD_ROWS - 1 - N_SAMPLE_SEQ, D_MODEL), F32)], axis=0)
    log_dt = jnp.broadcast_to(ssm_log_dt[..., None], ssm_a_re.shape)
    d_lanes = jnp.tile(ssm_d.reshape(DEPTH, N_GROUPS, SSM_GROUP), (1, 1, 128 // SSM_GROUP))
    vecs = jnp.stack([_dirs_on_lanes(ssm_a_re), _dirs_on_lanes(ssm_a_im), _dirs_on_lanes(log_dt), d_lanes], axis=2)
    vecs = jnp.concatenate([vecs, jnp.zeros((DEPTH, N_GROUPS, 4, 2 * SSM_STATE), F32)], axis=2)
    mats = jnp.stack([_dirs_on_lanes(jnp.swapaxes(ssm_b_re, -1, -2)), _dirs_on_lanes(jnp.swapaxes(ssm_b_im, -1, -2)),
                      _dirs_on_lanes(ssm_c_re), _dirs_on_lanes(ssm_c_im)], axis=2)
    mods, wft, cct, ttt, a16 = _modulation_and_operators(cvec, w_mod, b_mod, vecs, mats)
    ps_row = [[0] * N_PS] * SAMPLE_TILE + [[1 + p // SEGS_PER_SAMPLE_SEQ for p in range(N_PS)]]
    mod8 = mods.reshape(DEPTH, MOD_ROWS, N_MOD, D_MODEL)[:, jnp.array(ps_row, jnp.int32)]
    mod8 = mod8.transpose(0, 1, 3, 2, 4)

    w_s = jnp.swapaxes(_gmlp_position_order(jnp.swapaxes(_gmlp_position_order(gmlp_w_s), 2, 3)), 2, 3).astype(BF16)
    b_s = _gmlp_position_order(gmlp_b_s)[..., None]
    rwt = router_w.T
    rb = router_b.reshape(N_EXPERTS, 1)

    new_re, new_im = [], []
    for l in range(DEPTH):
        ug, vn, xt = _input_proj(x, mod8, l, norm1_w, w_in, gmlp_ln_w, gmlp_ln_b)
        s0 = jnp.concatenate([state_ssm_re[:, l].transpose(2, 0, 1, 3).reshape(N_GROUPS, N_SAMPLE_SEQ, 128),
                              state_ssm_im[:, l].transpose(2, 0, 1, 3).reshape(N_GROUPS, N_SAMPLE_SEQ, 128)], axis=-1)
        s0 = jnp.repeat(s0, SEGS_PER_SAMPLE_SEQ, axis=1)
        yt, fin = _ssm_scan(xt, wft, cct, ttt, a16, s0, l)
        fin = fin[:, :N_PROMPT_SEQ].reshape(N_GROUPS, N_PROMPT_SEQ, 2, 2, SSM_STATE)
        fin = fin.transpose(2, 1, 3, 0, 4)
        new_re.append(fin[0])
        new_im.append(fin[1])
        hx, route = _mix_out(x, yt, ug, vn, mod8, l, w_glu, w_s, b_s, w_out, norm2_w, rwt, rb)
        final = l == DEPTH - 1
        tables = _routing_tables(route[0].astype(jnp.int32), to_sequence_order=final)
        x = _experts(tables, hx, mods, l, w_gate, w_up, w_down, final_norm_w, final)

    y_prompt = x[:T_PROMPT].reshape(N_PROMPT_SEQ, PROMPT_LEN, D_MODEL)
    y_sample = x[T_PROMPT:T_ALL].reshape(N_SAMPLE_SEQ, SAMPLE_LEN, D_MODEL)
    return (y_prompt, y_sample, jnp.stack(new_re, axis=1), jnp.stack(new_im, axis=1))
```

```python
import functools
import math

import jax
import jax.numpy as jnp
from jax import lax
from jax.experimental import pallas as pl
from jax.experimental.pallas import tpu as pltpu

F32 = jnp.float32
BF16 = jnp.bfloat16

D_MODEL = 1024
N_PROMPT_SEQ = 16
PROMPT_LEN = 256
N_SAMPLE_SEQ = 2
SAMPLE_LEN = 1024
T_PROMPT = N_PROMPT_SEQ * PROMPT_LEN
T_SAMPLE = N_SAMPLE_SEQ * SAMPLE_LEN
T_ALL = T_PROMPT + T_SAMPLE
DEPTH = 2
D_SSM = 512
SSM_GROUP = 16
N_GROUPS = 32
SSM_STATE = 64
D_GMLP = 512
GMLP_HEADS = 4
GMLP_HEAD_DIM = 128
GMLP_CHUNK = 128
N_EXPERTS = 16
N_EXPERT_GROUPS = 4
EXPERTS_PER_GROUP = 4
D_EXPERT = 512
N_MOD = 6
EPS = 1e-6

SCAN_CHUNK = 16
CHUNK_WIDTH = SCAN_CHUNK * SSM_GROUP
SEG_LEN = 256
SEG_CHUNKS = SEG_LEN // SCAN_CHUNK
N_PS = 8
TILE_TOKENS = N_PS * SEG_LEN
N_TILES = T_ALL // TILE_TOKENS
SAMPLE_TILE = T_PROMPT // TILE_TOKENS
SEGS_PER_SAMPLE_SEQ = SAMPLE_LEN // SEG_LEN
J_ROWS = SEG_CHUNKS * N_PS
ALL_J_ROWS = N_TILES * J_ROWS
GROUP_BLOCK = 8
PREP_GROUPS = 8
MOD_ROWS = 8
MOD_K_STEPS = 4

J_PER_STEP = 4
TOKEN_TILE = J_PER_STEP * J_ROWS
PAIR_SLOT_A = (0, 0, 0, 1, 1, 3)
PAIR_SLOT_B = (1, 2, 3, 3, 2, 2)
MOE_TM = 256
TAIL_STEPS = 4
TAIL_QUARTER = MOE_TM // TAIL_STEPS
N_CLASSES = N_EXPERT_GROUPS * len(PAIR_SLOT_A)
MOE_TILES = T_ALL // MOE_TM + N_CLASSES
MOE_ROWS = MOE_TILES * MOE_TM
SPARE_ROWS = N_CLASSES * TAIL_QUARTER
OUT_ROWS = T_ALL + SPARE_ROWS
ROUTE_LANES = 128
HX_WIDTH = 2 * D_MODEL + ROUTE_LANES
VMEM_LIMIT = 56 * 1024 * 1024
TRANS_B = (((1,), (1,)), ((), ()))


def _sigmoid(x):
    return 1.0 / (1.0 + jnp.exp(-x))


def _gelu_tanh(x):
    c = math.sqrt(2.0 / math.pi)
    return x * (0.5 * (1.0 + jnp.tanh(c * (x + 0.044715 * (x * x * x)))))


def _split_bf16(a):
    hi = a.astype(BF16)
    return hi, (a - hi.astype(F32)).astype(BF16)


def _rmsnorm(x, w):
    return x * lax.rsqrt(jnp.mean(x * x, axis=-1, keepdims=True) + EPS) * w


def _per_ps(fn, a, *mods):
    rows, d = a.shape
    out = fn(a.reshape(rows // N_PS, N_PS, d), *[m[None] for m in mods])
    return out.reshape(rows, d)


def _params(*sem):
    return pltpu.CompilerParams(dimension_semantics=sem, vmem_limit_bytes=VMEM_LIMIT)


def _mod_kernel(c_ref, w_ref, b_ref, o_ref):
    c = c_ref[...]
    part = jnp.dot((c * _sigmoid(c)).astype(BF16), w_ref[...].astype(BF16), preferred_element_type=F32)

    @pl.when(pl.program_id(1) == 0)
    def _():
        o_ref[...] = part + b_ref[...]

    @pl.when(pl.program_id(1) > 0)
    def _():
        o_ref[...] += part


def _first_step():
    return jnp.logical_and(pl.program_id(0) == 0, pl.program_id(1) == 0)


def _in_kernel(x_ref, mod_ref, nw_ref, w_ref, lnw_ref, lnb_ref, ug_ref, vn_ref, xt_ref, wg_ref, wst_ref):
    @pl.when(_first_step())
    def _():
        wg_ref[...] = w_ref[:, D_SSM:].astype(BF16)
        wst_ref[...] = w_ref[:, :D_SSM].T.astype(BF16)

    y = _rmsnorm(x_ref[...], nw_ref[...])
    h = _per_ps(lambda a, sc, sh: a * (1.0 + sc) + sh, y, mod_ref[1], mod_ref[0]).astype(BF16)
    zg = _gelu_tanh(jnp.dot(h, wg_ref[...], preferred_element_type=F32))
    v = zg[:, D_GMLP:]
    mu = jnp.mean(v, axis=-1, keepdims=True)
    vc = v - mu
    var = jnp.mean(vc * vc, axis=-1, keepdims=True)
    vn = vc * lax.rsqrt(var + EPS) * lnw_ref[...] + lnb_ref[...]
    for hd in range(GMLP_HEADS):
        cols = slice(hd * GMLP_HEAD_DIM, (hd + 1) * GMLP_HEAD_DIM)
        ug_ref[hd] = zg[:, cols]
        vn_ref[hd] = vn[:, cols]
    xt = lax.dot_general(wst_ref[...], h, TRANS_B, preferred_element_type=F32).astype(BF16)
    for k in range(J_PER_STEP):
        xt_ref[k] = xt[:, k * J_ROWS:(k + 1) * J_ROWS]


def _input_proj(x, mod8, l, norm1_w, w_in, ln_w, ln_b):
    tm = TOKEN_TILE
    steps = TILE_TOKENS // tm
    tok = lambda n: pl.BlockSpec((tm, n), lambda t, s: (t * steps + s, 0))
    heads = pl.BlockSpec((GMLP_HEADS, tm, GMLP_HEAD_DIM), lambda t, s: (0, t * steps + s, 0))
    lay = lambda *shape: pl.BlockSpec((None,) + shape, lambda t, s: (l,) + (0,) * len(shape))
    return pl.pallas_call(
        _in_kernel,
        grid=(N_TILES, steps),
        in_specs=[
            tok(D_MODEL),
            pl.BlockSpec((None, None, N_MOD, N_PS, D_MODEL), lambda t, s: (l, t, 0, 0, 0)),
            lay(1, D_MODEL), lay(D_MODEL, D_SSM + 2 * D_GMLP), lay(1, D_GMLP), lay(1, D_GMLP),
        ],
        out_specs=[heads, heads,
                   pl.BlockSpec((J_PER_STEP, D_SSM, J_ROWS), lambda t, s: (s, 0, t))],
        out_shape=[jax.ShapeDtypeStruct((GMLP_HEADS, T_ALL, GMLP_HEAD_DIM), F32)] * 2 + [
            jax.ShapeDtypeStruct((SCAN_CHUNK, D_SSM, ALL_J_ROWS), BF16)],
        scratch_shapes=[pltpu.VMEM((D_MODEL, 2 * D_GMLP), BF16), pltpu.VMEM((D_SSM, D_MODEL), BF16)],
        compiler_params=_params("arbitrary", "arbitrary"),
        name="norm1_in_proj",
    )(x, mod8, norm1_w.reshape(DEPTH, 1, D_MODEL), w_in,
      ln_w.reshape(DEPTH, 1, D_GMLP), ln_b.reshape(DEPTH, 1, D_GMLP))


def _shift_lanes_right(a, b, s, lane):
    if s == 0:
        return a, b
    if s == 128:
        return jnp.zeros_like(a), a
    if s < 128:
        ra = pltpu.roll(a, s, 1)
        rb = pltpu.roll(b, s, 1)
        return jnp.where(lane >= s, ra, 0.0), jnp.where(lane >= s, rb, ra)
    t = s - 128
    return jnp.zeros_like(a), jnp.where(lane >= t, pltpu.roll(a, t, 1), 0.0)


def _shift_lanes_left(a, b, s, lane):
    if s == 0:
        return a, b
    if s == 128:
        return b, jnp.zeros_like(b)
    if s < 128:
        ra = pltpu.roll(a, 128 - s, 1)
        rb = pltpu.roll(b, 128 - s, 1)
        return jnp.where(lane < 128 - s, ra, rb), jnp.where(lane < 128 - s, rb, 0.0)
    t = s - 128
    return jnp.where(lane < 128 - t, pltpu.roll(b, 128 - t, 1), 0.0), jnp.zeros_like(b)


def _prep_kernel(*refs):
    for g in range(PREP_GROUPS):
        _prep_group(*[r.at[g] for r in refs])


def _prep_group(vec_ref, mat_ref, wft_ref, cct_ref, ttt_ref, a16_ref, wf_scr, cm_scr, tt_scr):
    a_re = vec_ref[0:1, :]
    a_im = vec_ref[1:2, :]
    dt = jnp.exp(vec_ref[2:3, :])
    d_skip = vec_ref[3:4, :]
    mag = jnp.exp(a_re * dt)
    ang = a_im * dt
    ab_r = mag * jnp.cos(ang)
    ab_i = mag * jnp.sin(ang)
    den = a_re * a_re + a_im * a_im
    nr = ab_r - 1.0
    q_r = (nr * a_re + ab_i * a_im) / den
    q_i = (ab_i * a_re - nr * a_im) / den
    bt_r = mat_ref[0]
    bt_i = mat_ref[1]
    c_r = mat_ref[2]
    c_i = mat_ref[3]
    bb_r = q_r * bt_r - q_i * bt_i
    bb_i = q_r * bt_i + q_i * bt_r
    p_r = [jnp.ones_like(ab_r)]
    p_i = [jnp.zeros_like(ab_r)]
    for _ in range(SCAN_CHUNK):
        pr, pi = p_r[-1], p_i[-1]
        p_r.append(pr * ab_r - pi * ab_i)
        p_i.append(pr * ab_i + pi * ab_r)
    a16_ref[0] = jnp.broadcast_to(p_r[SCAN_CHUNK], (N_PS, 128))
    a16_ref[1] = jnp.broadcast_to(p_i[SCAN_CHUNK], (N_PS, 128))

    lane = lax.broadcasted_iota(jnp.int32, (1, 128), 1)
    is_fwd = lane < SSM_STATE

    def pick(mf, mb):
        return jnp.where(is_fwd, p_r[mf], p_r[mb]), jnp.where(is_fwd, p_i[mf], p_i[mb])

    for j in range(SCAN_CHUNK):
        rows = slice(j * SSM_GROUP, (j + 1) * SSM_GROUP)
        wr, wi = pick(SCAN_CHUNK - 1 - j, j)
        wf_scr[rows, 0:128] = bb_r * wr - bb_i * wi
        wf_scr[rows, 128:256] = bb_r * wi + bb_i * wr
        wr, wi = pick(j + 1, SCAN_CHUNK - j)
        cct_ref[rows, 0:128] = (c_r * wr - c_i * wi).astype(BF16)
        cct_ref[rows, 128:256] = (-(c_r * wi + c_i * wr)).astype(BF16)
        wr, wi = pick(j, SCAN_CHUNK - 1 - j)
        cm_scr[rows, 0:128] = c_r * wr - c_i * wi
        cm_scr[rows, 128:256] = c_r * wi + c_i * wr
    wft_ref[...] = wf_scr[...].T.astype(BF16)

    zero = jnp.zeros_like(bb_r)
    cm_hi, cm_lo = _split_bf16(cm_scr[...])
    dot_t = lambda a, b: lax.dot_general(a, b, TRANS_B, preferred_element_type=F32)

    def lag_rows(keep):
        lhs = jnp.concatenate([jnp.where(keep, bb_r, zero), jnp.where(keep, -bb_i, zero)], axis=1)
        hi, lo = _split_bf16(lhs)
        return dot_t(hi, cm_hi) + (dot_t(hi, cm_lo) + dot_t(lo, cm_hi))

    mf = lag_rows(is_fwd)
    mb = lag_rows(jnp.logical_not(is_fwd))
    mf_a, mf_b = mf[:, 0:128], mf[:, 128:256]
    mb_a, mb_b = mb[:, 0:128], mb[:, 128:256]
    row_h = lax.broadcasted_iota(jnp.int32, (SSM_GROUP, 128), 0)
    lane_h = lax.broadcasted_iota(jnp.int32, (SSM_GROUP, 128), 1)
    for jp in range(SCAN_CHUNK):
        rows = slice(jp * SSM_GROUP, (jp + 1) * SSM_GROUP)
        fa, fb = _shift_lanes_right(mf_a, mf_b, SSM_GROUP * jp, lane)
        ba, bb = _shift_lanes_left(mb_a, mb_b, SSM_GROUP * (SCAN_CHUNK - 1 - jp), lane)
        diag = SSM_GROUP * jp + row_h
        tt_scr[rows, 0:128] = fa + ba + jnp.where(lane_h == diag, d_skip, 0.0)
        tt_scr[rows, 128:256] = fb + bb + jnp.where(lane_h + 128 == diag, d_skip, 0.0)
    ttt_ref[...] = tt_scr[...].T.astype(BF16)


def _mod_and_operators_kernel(c_ref, w_ref, b_ref, vec_ref, mat_ref, o_ref, *operator_refs):
    _mod_kernel(c_ref, w_ref, b_ref, o_ref)
    _prep_kernel(vec_ref, mat_ref, *operator_refs)


def _modulation_and_operators(cvec, w_mod, b_mod, vecs, mats):
    assert N_GROUPS // PREP_GROUPS == MOD_K_STEPS
    kb = D_MODEL // MOD_K_STEPS
    c_blocks = cvec.reshape(MOD_ROWS, MOD_K_STEPS, kb).transpose(1, 0, 2)
    op = jax.ShapeDtypeStruct((DEPTH, N_GROUPS, CHUNK_WIDTH, CHUNK_WIDTH), BF16)
    pg = PREP_GROUPS
    op_spec = pl.BlockSpec((None, pg, CHUNK_WIDTH, CHUNK_WIDTH), lambda l, g: (l, g, 0, 0))
    sq = pltpu.VMEM((pg, CHUNK_WIDTH, CHUNK_WIDTH), F32)
    return pl.pallas_call(
        _mod_and_operators_kernel,
        grid=(DEPTH, MOD_K_STEPS),
        in_specs=[
            pl.BlockSpec((None, MOD_ROWS, kb), lambda l, k: (k, 0, 0)),
            pl.BlockSpec((None, kb, N_MOD * D_MODEL), lambda l, k: (l, k, 0)),
            pl.BlockSpec((None, 1, N_MOD * D_MODEL), lambda l, k: (l, 0, 0)),
            pl.BlockSpec((None, pg, 8, 128), lambda l, g: (l, g, 0, 0)),
            pl.BlockSpec((None, pg, 4, SSM_GROUP, 128), lambda l, g: (l, g, 0, 0, 0)),
        ],
        out_specs=[pl.BlockSpec((None, MOD_ROWS, N_MOD * D_MODEL), lambda l, k: (l, 0, 0)),
                   op_spec, op_spec, op_spec,
                   pl.BlockSpec((None, pg, 2, N_PS, 128), lambda l, g: (l, g, 0, 0, 0))],
        out_shape=[jax.ShapeDtypeStruct((DEPTH, MOD_ROWS, N_MOD * D_MODEL), F32),
                   op, op, op, jax.ShapeDtypeStruct((DEPTH, N_GROUPS, 2, N_PS, 128), F32)],
        scratch_shapes=[sq, sq, sq],
        compiler_params=_params("arbitrary", "arbitrary"),
        name="adaln_and_s5_operators",
    )(c_blocks, w_mod, b_mod.reshape(DEPTH, 1, N_MOD * D_MODEL), vecs, mats)


def _ssm_kernel(xt_ref, wft_ref, cct_ref, ttt_ref, a16_ref, s0_ref, yt_ref, fin_ref, s_scr, f_scr, ft_scr):
    gb = GROUP_BLOCK
    is_fwd = lax.broadcasted_iota(jnp.int32, (1, 1, 128), 2) < SSM_STATE
    half = SSM_STATE

    def group_x(gl):
        return xt_ref[:, gl * SSM_GROUP:(gl + 1) * SSM_GROUP, :].reshape(CHUNK_WIDTH, ALL_J_ROWS)

    for gl in range(gb):
        ft_scr[gl] = jnp.dot(wft_ref[gl], group_x(gl), preferred_element_type=F32)
        f_scr[gl] = ft_scr[gl].T

    def scan(tiles, s_r, s_i):
        rows = len(tiles) * N_PS
        a_r = jnp.broadcast_to(a16_ref[:, 0, 0:1, :], (gb, rows, 128))
        a_i = jnp.broadcast_to(a16_ref[:, 1, 0:1, :], (gb, rows, 128))
        for i in range(SEG_CHUNKS):
            f_r, f_i = [], []
            for k, t in enumerate(tiles):
                st = slice(k * N_PS, (k + 1) * N_PS)
                rf = slice(t * J_ROWS + i * N_PS, t * J_ROWS + (i + 1) * N_PS)
                rb = slice(t * J_ROWS + (SEG_CHUNKS - 1 - i) * N_PS, t * J_ROWS + (SEG_CHUNKS - i) * N_PS)
                s_scr[:, rf, 0:half] = s_r[:, st, 0:half]
                s_scr[:, rf, 128:128 + half] = s_i[:, st, 0:half]
                s_scr[:, rb, half:128] = s_r[:, st, half:128]
                s_scr[:, rb, 128 + half:256] = s_i[:, st, half:128]
                f_r.append(jnp.where(is_fwd, f_scr[:, rf, 0:128], f_scr[:, rb, 0:128]))
                f_i.append(jnp.where(is_fwd, f_scr[:, rf, 128:256], f_scr[:, rb, 128:256]))
            f_r = jnp.concatenate(f_r, axis=1)
            f_i = jnp.concatenate(f_i, axis=1)
            s_r, s_i = a_r * s_r - a_i * s_i + f_r, a_r * s_i + a_i * s_r + f_i
        return s_r, s_i

    zeros = jnp.zeros((gb, N_TILES * N_PS, 128), F32)
    z_r, z_i = scan(list(range(N_TILES)), zeros, zeros)
    fin_ref[:, :, 0:128] = z_r
    fin_ref[:, :, 128:256] = z_i

    z_r = z_r[:, SAMPLE_TILE * N_PS:, :]
    z_i = z_i[:, SAMPLE_TILE * N_PS:, :]
    b_r, b_i = a16_ref[:, 0], a16_ref[:, 1]
    for _ in range(4):
        b_r, b_i = b_r * b_r - b_i * b_i, 2.0 * (b_r * b_i)
    seg = lax.broadcasted_iota(jnp.int32, (gb, N_PS, 128), 1) % SEGS_PER_SAMPLE_SEQ
    i_r = s0_ref[:, :, 0:128]
    i_i = s0_ref[:, :, 128:256]
    for step in range(1, SEGS_PER_SAMPLE_SEQ):
        pr = jnp.where(is_fwd, pltpu.roll(i_r, 1, 1), pltpu.roll(i_r, N_PS - 1, 1))
        pi = jnp.where(is_fwd, pltpu.roll(i_i, 1, 1), pltpu.roll(i_i, N_PS - 1, 1))
        zr = jnp.where(is_fwd, pltpu.roll(z_r, 1, 1), pltpu.roll(z_r, N_PS - 1, 1))
        zi = jnp.where(is_fwd, pltpu.roll(z_i, 1, 1), pltpu.roll(z_i, N_PS - 1, 1))
        n_r = b_r * pr - b_i * pi + zr
        n_i = b_r * pi + b_i * pr + zi
        first = jnp.where(is_fwd, step, 0)
        last = jnp.where(is_fwd, SEGS_PER_SAMPLE_SEQ - 1, SEGS_PER_SAMPLE_SEQ - 1 - step)
        upd = jnp.logical_and(seg >= first, seg <= last)
        i_r = jnp.where(upd, n_r, i_r)
        i_i = jnp.where(upd, n_i, i_i)
    scan([SAMPLE_TILE], i_r, i_i)

    for gl in range(gb):
        yt = jnp.dot(ttt_ref[gl], group_x(gl), preferred_element_type=F32)
        yt += lax.dot_general(cct_ref[gl], s_scr[gl].astype(BF16), TRANS_B, preferred_element_type=F32)
        yt_ref[:, gl * SSM_GROUP:(gl + 1) * SSM_GROUP, :] = yt.reshape(SCAN_CHUNK, SSM_GROUP, ALL_J_ROWS)


def _ssm_scan(xt, wft, cct, ttt, a16, s0, l):
    gb = GROUP_BLOCK
    op_spec = pl.BlockSpec((None, gb, CHUNK_WIDTH, CHUNK_WIDTH), lambda g: (l, g, 0, 0))
    io_spec = pl.BlockSpec((SCAN_CHUNK, gb * SSM_GROUP, ALL_J_ROWS), lambda g: (0, g, 0))
    rows = pltpu.VMEM((gb, ALL_J_ROWS, CHUNK_WIDTH), F32)
    return pl.pallas_call(
        _ssm_kernel,
        grid=(N_GROUPS // gb,),
        in_specs=[
            io_spec, op_spec, op_spec, op_spec,
            pl.BlockSpec((None, gb, 2, N_PS, 128), lambda g: (l, g, 0, 0, 0)),
            pl.BlockSpec((gb, N_PS, CHUNK_WIDTH), lambda g: (g, 0, 0)),
        ],
        out_specs=[io_spec, pl.BlockSpec((gb, N_TILES * N_PS, CHUNK_WIDTH), lambda g: (g, 0, 0))],
        out_shape=[
            jax.ShapeDtypeStruct((SCAN_CHUNK, D_SSM, ALL_J_ROWS), F32),
            jax.ShapeDtypeStruct((N_GROUPS, N_TILES * N_PS, CHUNK_WIDTH), F32),
        ],
        scratch_shapes=[rows, rows, pltpu.VMEM((gb, CHUNK_WIDTH, ALL_J_ROWS), F32)],
        compiler_params=_params("arbitrary"),
        name="s5_chunk_scan",
    )(xt, wft, cct, ttt, a16, s0)


def _post_kernel(x_ref, yt_ref, ug_ref, vn_ref, mod_ref, wglu_ref, ws_ref, bs_ref, wout_f32_ref,
                 nw_ref, rwt_ref, rb_ref, hx_ref, route_ref, yg_scr, y_scr, wglut_ref, wout_ref):
    tile = pl.program_id(0)
    step = pl.program_id(1)

    @pl.when(_first_step())
    def _():
        wglut_ref[...] = wglu_ref[...].T.astype(BF16)
        wout_ref[...] = wout_f32_ref[...].astype(BF16)

    @pl.when(step == 0)
    def _():
        def chunk(n, carry):
            ps = n % N_PS
            c_hi = n // N_PS
            base = c_hi * (GMLP_CHUNK // SCAN_CHUNK) * N_PS + ps
            rows = [pl.ds(j * J_ROWS + base, GMLP_CHUNK // SCAN_CHUNK, stride=N_PS) for j in range(SCAN_CHUNK)]
            for h in range(GMLP_HEADS):
                v = jnp.concatenate([vn_ref[h, r, :] for r in rows], axis=0).astype(BF16)
                u = jnp.concatenate([ug_ref[h, r, :] for r in rows], axis=0)
                s = jnp.dot(ws_ref[h], v, preferred_element_type=F32) + bs_ref[h]
                yg = u * s
                for j, r in enumerate(rows):
                    yg_scr[h, r, :] = yg[j * 8:(j + 1) * 8]
            return carry

        for n in range(TILE_TOKENS // GMLP_CHUNK):
            chunk(n, 0)

    yt = _gelu_tanh(jnp.concatenate([yt_ref[k] for k in range(J_PER_STEP)], axis=1))
    yt = yt * _sigmoid(jnp.dot(wglut_ref[...], yt.astype(BF16), preferred_element_type=F32))
    for k in range(J_PER_STEP):
        y_scr[k * J_ROWS:(k + 1) * J_ROWS, :] = yt[:, k * J_ROWS:(k + 1) * J_ROWS].T
    row0 = pl.multiple_of(step * TOKEN_TILE, TOKEN_TILE)
    proj = jnp.dot(y_scr[...].astype(BF16), wout_ref[0:D_SSM, :], preferred_element_type=F32)
    yg = jnp.concatenate([yg_scr[h, pl.ds(row0, TOKEN_TILE), :] for h in range(GMLP_HEADS)], axis=1)
    proj += jnp.dot(yg.astype(BF16), wout_ref[D_SSM:, :], preferred_element_type=F32)
    x1 = x_ref[...] + _per_ps(lambda a, g: a * g, proj, mod_ref[2])
    h2 = _per_ps(lambda a, sc, sh: a * (1.0 + sc) + sh, _rmsnorm(x1, nw_ref[...]), mod_ref[4], mod_ref[3])
    hx_ref[:, 0:D_MODEL] = h2
    hx_ref[:, D_MODEL:2 * D_MODEL] = x1
    r_hi, r_lo = _split_bf16(rwt_ref[...])
    h_hi, h_lo = _split_bf16(h2)
    dot_t = lambda a, b: lax.dot_general(a, b, TRANS_B, preferred_element_type=F32)
    logits = dot_t(r_hi, h_hi) + (dot_t(r_hi, h_lo) + dot_t(r_lo, h_hi))
    scores = _sigmoid(logits)
    sel = scores + rb_ref[...]
    sc = [scores[e:e + 1, :] for e in range(N_EXPERTS)]
    sl = [sel[e:e + 1, :] for e in range(N_EXPERTS)]
    gscore = []
    for g in range(N_EXPERT_GROUPS):
        v0, v1, v2, v3 = sl[4 * g:4 * g + 4]
        hi01, lo01 = jnp.maximum(v0, v1), jnp.minimum(v0, v1)
        hi23, lo23 = jnp.maximum(v2, v3), jnp.minimum(v2, v3)
        top1 = jnp.maximum(hi01, hi23)
        top2 = jnp.maximum(jnp.minimum(hi01, hi23), jnp.maximum(lo01, lo23))
        gscore.append(top1 + top2)
    best = gscore[0]
    gidx = jnp.zeros_like(best, dtype=jnp.int32)
    for g in range(1, N_EXPERT_GROUPS):
        upd = gscore[g] > best
        gidx = jnp.where(upd, g, gidx)
        best = jnp.where(upd, gscore[g], best)

    def in_group(vals, k):
        out = vals[k]
        for g in range(1, N_EXPERT_GROUPS):
            out = jnp.where(gidx == g, vals[4 * g + k], out)
        return out

    v = [in_group(sl, k) for k in range(EXPERTS_PER_GROUP)]
    s = [in_group(sc, k) for k in range(EXPERTS_PER_GROUP)]
    w = []
    bits = jnp.zeros_like(gidx)
    for k in range(EXPERTS_PER_GROUP):
        rank = jnp.zeros_like(gidx)
        for j in range(EXPERTS_PER_GROUP):
            if j == k:
                continue
            ahead = (v[j] >= v[k]) if j < k else (v[j] > v[k])
            rank = rank + ahead.astype(jnp.int32)
        w.append(jnp.where(rank < 2, s[k], 0.0))
        bits = bits + jnp.where(rank < 2, 1 << k, 0)
    denom = (w[0] + w[1]) + (w[2] + w[3])
    gate = [wk / denom for wk in w]
    pair = jnp.full_like(gidx, len(PAIR_SLOT_A) - 1)
    for p in range(len(PAIR_SLOT_A) - 1):
        pair = jnp.where(bits == (1 << PAIR_SLOT_A[p]) + (1 << PAIR_SLOT_B[p]), p, pair)

    def slot_gate(table):
        out = gate[table[0]]
        for p in range(1, len(table)):
            out = jnp.where(pair == p, gate[table[p]], out)
        return out

    n_tok = route_ref.shape[1]
    route_ref[0:1, :] = (gidx * len(PAIR_SLOT_A) + pair).astype(F32)
    ps = lax.broadcasted_iota(jnp.int32, (1, n_tok), 1) % N_PS
    cond = jnp.where(tile < SAMPLE_TILE, 0, 1 + ps // SEGS_PER_SAMPLE_SEQ).astype(F32)
    lanes = jnp.concatenate([slot_gate(PAIR_SLOT_A), slot_gate(PAIR_SLOT_B), cond,
                             jnp.zeros((ROUTE_LANES - 3, n_tok), F32)], axis=0)
    hx_ref[:, 2 * D_MODEL:] = lanes.T


def _mix_out(x, yt, ug, vn, mod8, l, w_glu, w_s, b_s, w_out, norm2_w, rwt, rb):
    tm = TOKEN_TILE
    steps = TILE_TOKENS // tm
    tok = lambda n: pl.BlockSpec((tm, n), lambda t, s: (t * steps + s, 0))
    whole_tile = pl.BlockSpec((GMLP_HEADS, TILE_TOKENS, GMLP_HEAD_DIM), lambda t, s: (0, t, 0))
    lay = lambda *shape: pl.BlockSpec((None,) + shape, lambda t, s: (l,) + (0,) * len(shape))
    return pl.pallas_call(
        _post_kernel,
        grid=(N_TILES, steps),
        in_specs=[
            tok(D_MODEL),
            pl.BlockSpec((J_PER_STEP, D_SSM, J_ROWS), lambda t, s: (s, 0, t)),
            whole_tile, whole_tile,
            pl.BlockSpec((None, None, N_MOD, N_PS, D_MODEL), lambda t, s: (l, t, 0, 0, 0)),
            lay(D_SSM, D_SSM), lay(GMLP_HEADS, GMLP_CHUNK, GMLP_CHUNK),
            lay(GMLP_HEADS, GMLP_CHUNK, 1), lay(D_MODEL, D_MODEL), lay(1, D_MODEL),
            pl.BlockSpec((N_EXPERTS, D_MODEL), lambda t, s: (0, 0)),
            pl.BlockSpec((N_EXPERTS, 1), lambda t, s: (0, 0)),
        ],
        out_specs=[tok(HX_WIDTH), pl.BlockSpec((1, tm), lambda t, s: (0, t * steps + s))],
        out_shape=[
            jax.ShapeDtypeStruct((T_ALL, HX_WIDTH), F32),
            jax.ShapeDtypeStruct((1, T_ALL), F32),
        ],
        scratch_shapes=[pltpu.VMEM((GMLP_HEADS, TILE_TOKENS, GMLP_HEAD_DIM), F32), pltpu.VMEM((tm, D_SSM), F32),
                        pltpu.VMEM((D_SSM, D_SSM), BF16), pltpu.VMEM((D_MODEL, D_MODEL), BF16)],
        compiler_params=_params("arbitrary", "arbitrary"),
        name="mixers_out_router",
    )(x, yt, ug, vn, mod8, w_glu, w_s, b_s, w_out, norm2_w.reshape(DEPTH, 1, D_MODEL), rwt, rb)


def _moe_kernel(ea_ref, eb_ref, nv_ref, size_ref, src_ref, drow_ref,
                hx_hbm, wga_ref, wua_ref, wda_ref, wgb_ref, wub_ref, wdb_ref, g2_ref, fw_ref,
                o_hbm, hx_buf, o_buf, z_buf, g_sem, s_sem, z_sem, *, final):
    n = pl.program_id(0)
    n_valid = nv_ref[0]
    slot = n % 2

    def by_size(tile, fn):
        for quarters in range(1, TAIL_STEPS + 1):
            @pl.when(size_ref[tile] == quarters)
            def _(rows=quarters * TAIL_QUARTER):
                fn(rows)

    def start_gather(tile, sl, rows):
        base = tile * MOE_TM
        for r in range(rows):
            pltpu.make_async_copy(hx_hbm.at[pl.ds(src_ref[base + r], 1)], hx_buf.at[sl, pl.ds(r, 1)],
                                  g_sem.at[sl]).start()

    def wait_gather(sl, rows):
        pltpu.make_async_copy(hx_hbm.at[pl.ds(0, rows)], hx_buf.at[sl, pl.ds(0, rows)], g_sem.at[sl]).wait()

    def start_scatter(tile, sl, rows):
        base = tile * MOE_TM
        for r in range(rows):
            pltpu.make_async_copy(o_buf.at[sl, pl.ds(r, 1)], o_hbm.at[pl.ds(drow_ref[base + r], 1)],
                                  s_sem.at[sl]).start()

    def wait_scatter(sl, rows):
        pltpu.make_async_copy(o_buf.at[sl, pl.ds(0, rows)], o_hbm.at[pl.ds(0, rows)], s_sem.at[sl]).wait()

    @pl.when(n == 0)
    def _():
        z_buf[...] = jnp.zeros_like(z_buf)
        spare = [pltpu.make_async_copy(z_buf, o_hbm.at[pl.ds(T_ALL + i * MOE_TM, MOE_TM)], z_sem)
                 for i in range(SPARE_ROWS // MOE_TM)]
        for cp in spare:
            cp.start()
        for cp in spare:
            cp.wait()
        by_size(0, lambda rows: start_gather(0, 0, rows))

    @pl.when(n + 1 < n_valid)
    def _():
        by_size(n + 1, lambda rows: start_gather(n + 1, 1 - slot, rows))

    def tile_step(rows):
        wait_gather(slot, rows)

        @pl.when(n >= 2)
        def _():
            by_size(n - 2, lambda r: wait_scatter(slot, r))

        h = hx_buf[slot, 0:rows, 0:D_MODEL].astype(BF16)
        lanes = hx_buf[slot, 0:rows, 2 * D_MODEL:]

        def expert(wg, wu, wd, gate):
            hg = jnp.dot(h, wg[...].astype(BF16), preferred_element_type=F32)
            hu = jnp.dot(h, wu[...].astype(BF16), preferred_element_type=F32)
            act = hg * _sigmoid(hg) * hu * gate
            return jnp.dot(act.astype(BF16), wd[...].astype(BF16), preferred_element_type=F32)

        y = expert(wga_ref, wua_ref, wda_ref, lanes[:, 0:1]) + expert(wgb_ref, wub_ref, wdb_ref, lanes[:, 1:2])
        cond_row = lanes[:, 2:3]
        gate2 = jnp.where(cond_row == 0.0, g2_ref[0:1, :], jnp.where(cond_row == 1.0, g2_ref[1:2, :], g2_ref[2:3, :]))
        x2 = hx_buf[slot, 0:rows, D_MODEL:2 * D_MODEL] + gate2 * y
        o_buf[slot, 0:rows] = _rmsnorm(x2, fw_ref[...]) if final else x2
        start_scatter(n, slot, rows)

    @pl.when(n < n_valid)
    def _():
        by_size(n, tile_step)

    @pl.when(n == MOE_TILES - 1)
    def _():
        last = n_valid - 1
        by_size(last, lambda r: wait_scatter(last % 2, r))
        by_size(last - 1, lambda r: wait_scatter(1 - last % 2, r))


def _experts(tables, hx, mods, l, w_gate, w_up, w_down, final_w, final):
    w_a = lambda r, c: pl.BlockSpec((None, None, r, c), lambda n, ea, eb, *_: (l, ea[n], 0, 0))
    w_b = lambda r, c: pl.BlockSpec((None, None, r, c), lambda n, ea, eb, *_: (l, eb[n], 0, 0))
    return pl.pallas_call(
        functools.partial(_moe_kernel, final=final),
        grid_spec=pltpu.PrefetchScalarGridSpec(
            num_scalar_prefetch=len(tables),
            grid=(MOE_TILES,),
            in_specs=[
                pl.BlockSpec(memory_space=pl.ANY),
                w_a(D_MODEL, D_EXPERT), w_a(D_MODEL, D_EXPERT), w_a(D_EXPERT, D_MODEL),
                w_b(D_MODEL, D_EXPERT), w_b(D_MODEL, D_EXPERT), w_b(D_EXPERT, D_MODEL),
                pl.BlockSpec((None, MOD_ROWS, D_MODEL), lambda n, *_: (l, 0, N_MOD - 1)),
                pl.BlockSpec((1, D_MODEL), lambda n, *_: (0, 0)),
            ],
            out_specs=pl.BlockSpec(memory_space=pl.ANY),
            scratch_shapes=[pltpu.VMEM((2, MOE_TM, HX_WIDTH), F32), pltpu.VMEM((2, MOE_TM, D_MODEL), F32),
                            pltpu.VMEM((MOE_TM, D_MODEL), F32),
                            pltpu.SemaphoreType.DMA((2,)), pltpu.SemaphoreType.DMA((2,)), pltpu.SemaphoreType.DMA(())],
        ),
        out_shape=jax.ShapeDtypeStruct((OUT_ROWS, D_MODEL), F32),
        compiler_params=_params("arbitrary"),
        name="experts",
    )(*tables, hx, w_gate, w_up, w_down, w_gate, w_up, w_down, mods, final_w.reshape(1, D_MODEL))


def _count_before(flags):
    n, k = flags.shape
    blocks = flags.reshape(n // 128, 128, k).astype(F32)
    strictly_lower = jnp.tril(jnp.ones((128, 128), F32), -1)
    within = jnp.einsum("ij,bjk->bik", strictly_lower, blocks)
    totals = jnp.sum(blocks, axis=1)
    before = jnp.cumsum(totals, axis=0) - totals
    return (within + before[:, None, :]).reshape(n, k).astype(jnp.int32)


def _routing_tables(cls, to_sequence_order):
    n_cls = N_EXPERT_GROUPS * len(PAIR_SLOT_A)
    onehot = (cls[:, None] == jnp.arange(n_cls, dtype=jnp.int32)[None, :]).astype(jnp.int32)
    counts = jnp.sum(onehot, axis=0)
    tiles = (counts + MOE_TM - 1) // MOE_TM
    tile_end = jnp.cumsum(tiles)
    row_in_class = (tile_end - tiles)[None, :] * MOE_TM + _count_before(onehot)
    dst = jnp.sum(onehot * row_in_class, axis=1)
    token_plus_1 = jnp.zeros((MOE_ROWS,), jnp.int32).at[dst].set(
        jnp.arange(1, T_ALL + 1, dtype=jnp.int32), unique_indices=True, mode="promise_in_bounds")
    is_pad = token_plus_1 == 0
    src = jnp.maximum(token_plus_1 - 1, 0)
    n_valid = tile_end[-1]
    tile_id = jnp.minimum(jnp.arange(MOE_TILES, dtype=jnp.int32), n_valid - 1)
    tile_cls = jnp.sum((tile_end[None, :] <= tile_id[:, None]).astype(jnp.int32), axis=1)
    group, pair = tile_cls // len(PAIR_SLOT_A), tile_cls % len(PAIR_SLOT_A)
    in_cls = (tile_cls[:, None] == jnp.arange(n_cls, dtype=jnp.int32)[None, :]).astype(jnp.int32)
    left = jnp.sum(in_cls * (counts[None, :] - (tile_id[:, None] - (tile_end - tiles)[None, :]) * MOE_TM), axis=1)
    size = jnp.clip((left + TAIL_QUARTER - 1) // TAIL_QUARTER, 1, TAIL_STEPS).astype(jnp.int32)
    row = jnp.arange(MOE_ROWS, dtype=jnp.int32)
    processed = jnp.logical_and(row % MOE_TM < jnp.repeat(size, MOE_TM) * TAIL_QUARTER, row // MOE_TM < n_valid)
    moved_pad = jnp.logical_and(is_pad, processed)
    spare = T_ALL + _count_before(moved_pad.astype(jnp.int32)[:, None])[:, 0]
    if to_sequence_order:
        j, c, ps = (src // J_ROWS) % SCAN_CHUNK, (src // N_PS) % SEG_CHUNKS, src % N_PS
        target = (src // TILE_TOKENS) * TILE_TOKENS + ps * SEG_LEN + c * SCAN_CHUNK + j
    else:
        target = src
    drow = jnp.where(is_pad, jnp.where(moved_pad, spare, T_ALL), target).astype(jnp.int32)

    def slot_expert(table):
        local = jnp.full_like(pair, table[-1])
        for p in range(len(table) - 1):
            local = jnp.where(pair == p, table[p], local)
        return group * EXPERTS_PER_GROUP + local

    e_a = slot_expert(PAIR_SLOT_A)
    e_b = slot_expert(PAIR_SLOT_B)
    return e_a, e_b, n_valid.reshape(1).astype(jnp.int32), size, src, drow


def _dirs_on_lanes(p):
    p = jnp.moveaxis(p, 1, -2)
    return p.reshape(p.shape[:-2] + (2 * SSM_STATE,))


def _to_internal_order(x):
    x = x.reshape(N_TILES, N_PS, SEG_CHUNKS, SCAN_CHUNK, D_MODEL)
    return x.transpose(0, 3, 2, 1, 4).reshape(T_ALL, D_MODEL)


def _gmlp_position_order(w):
    n_lo = GMLP_CHUNK // SCAN_CHUNK
    lead = w.shape[:2]
    w = w.reshape(lead + (n_lo, SCAN_CHUNK) + w.shape[3:])
    w = jnp.swapaxes(w, 2, 3)
    return w.reshape(lead + (GMLP_CHUNK,) + w.shape[4:])


def kernel(x_prompt, x_sample, c, state_ssm_re, state_ssm_im, c_ctx, norm1_w, norm2_w, w_mod, b_mod, w_in,
           ssm_a_re, ssm_a_im, ssm_log_dt, ssm_b_re, ssm_b_im, ssm_c_re, ssm_c_im, ssm_d, w_glu,
           gmlp_ln_w, gmlp_ln_b, gmlp_w_s, gmlp_b_s, w_out, router_w, router_b, w_gate, w_up, w_down,
           final_norm_w):
    x = jnp.concatenate([x_prompt.reshape(T_PROMPT, D_MODEL), x_sample.reshape(T_SAMPLE, D_MODEL)], axis=0)
    x = _to_internal_order(x)

    cvec = jnp.concatenate([c_ctx[None, :], c, jnp.zeros((MOD_ROWS - 1 - N_SAMPLE_SEQ, D_MODEL), F32)], axis=0)
    log_dt = jnp.broadcast_to(ssm_log_dt[..., None], ssm_a_re.shape)
    d_lanes = jnp.tile(ssm_d.reshape(DEPTH, N_GROUPS, SSM_GROUP), (1, 1, 128 // SSM_GROUP))
    vecs = jnp.stack([_dirs_on_lanes(ssm_a_re), _dirs_on_lanes(ssm_a_im), _dirs_on_lanes(log_dt), d_lanes], axis=2)
    vecs = jnp.concatenate([vecs, jnp.zeros((DEPTH, N_GROUPS, 4, 2 * SSM_STATE), F32)], axis=2)
    mats = jnp.stack([_dirs_on_lanes(jnp.swapaxes(ssm_b_re, -1, -2)), _dirs_on_lanes(jnp.swapaxes(ssm_b_im, -1, -2)),
                      _dirs_on_lanes(ssm_c_re), _dirs_on_lanes(ssm_c_im)], axis=2)
    mods, wft, cct, ttt, a16 = _modulation_and_operators(cvec, w_mod, b_mod, vecs, mats)
    ps_row = [[0] * N_PS] * SAMPLE_TILE + [[1 + p // SEGS_PER_SAMPLE_SEQ for p in range(N_PS)]]
    mod8 = mods.reshape(DEPTH, MOD_ROWS, N_MOD, D_MODEL)[:, jnp.array(ps_row, jnp.int32)]
    mod8 = mod8.transpose(0, 1, 3, 2, 4)

    w_s = jnp.swapaxes(_gmlp_position_order(jnp.swapaxes(_gmlp_position_order(gmlp_w_s), 2, 3)), 2, 3).astype(BF16)
    b_s = _gmlp_position_order(gmlp_b_s)[..., None]
    rwt = router_w.T
    rb = router_b.reshape(N_EXPERTS, 1)

    new_re, new_im = [], []
    for l in range(DEPTH):
        ug, vn, xt = _input_proj(x, mod8, l, norm1_w, w_in, gmlp_ln_w, gmlp_ln_b)
        s0 = jnp.concatenate([state_ssm_re[:, l].transpose(2, 0, 1, 3).reshape(N_GROUPS, N_SAMPLE_SEQ, 128),
                              state_ssm_im[:, l].transpose(2, 0, 1, 3).reshape(N_GROUPS, N_SAMPLE_SEQ, 128)], axis=-1)
        s0 = jnp.repeat(s0, SEGS_PER_SAMPLE_SEQ, axis=1)
        yt, fin = _ssm_scan(xt, wft, cct, ttt, a16, s0, l)
        fin = fin[:, :N_PROMPT_SEQ].reshape(N_GROUPS, N_PROMPT_SEQ, 2, 2, SSM_STATE)
        fin = fin.transpose(2, 1, 3, 0, 4)
        new_re.append(fin[0])
        new_im.append(fin[1])
        hx, route = _mix_out(x, yt, ug, vn, mod8, l, w_glu, w_s, b_s, w_out, norm2_w, rwt, rb)
        final = l == DEPTH - 1
        tables = _routing_tables(route[0].astype(jnp.int32), to_sequence_order=final)
        x = _experts(tables, hx, mods, l, w_gate, w_up, w_down, final_norm_w, final)

    y_prompt = x[:T_PROMPT].reshape(N_PROMPT_SEQ, PROMPT_LEN, D_MODEL)
    y_sample = x[T_PROMPT:T_ALL].reshape(N_SAMPLE_SEQ, SAMPLE_LEN, D_MODEL)
    return (y_prompt, y_sample, jnp.stack(new_re, axis=1), jnp.stack(new_im, axis=1))
```

```python
import functools
import math

import jax
import jax.numpy as jnp
from jax import lax
from jax.experimental import pallas as pl
from jax.experimental.pallas import tpu as pltpu

F32 = jnp.float32
BF16 = jnp.bfloat16

D_MODEL = 1024
N_PROMPT_SEQ = 16
PROMPT_LEN = 256
N_SAMPLE_SEQ = 2
SAMPLE_LEN = 1024
T_PROMPT = N_PROMPT_SEQ * PROMPT_LEN
T_SAMPLE = N_SAMPLE_SEQ * SAMPLE_LEN
T_ALL = T_PROMPT + T_SAMPLE
DEPTH = 2
D_SSM = 512
SSM_GROUP = 16
N_GROUPS = 32
SSM_STATE = 64
D_GMLP = 512
GMLP_HEADS = 4
GMLP_HEAD_DIM = 128
GMLP_CHUNK = 128
N_EXPERTS = 16
N_EXPERT_GROUPS = 4
EXPERTS_PER_GROUP = 4
D_EXPERT = 512
N_MOD = 6
EPS = 1e-6

SCAN_CHUNK = 16
CHUNK_WIDTH = SCAN_CHUNK * SSM_GROUP
SEG_LEN = 256
SEG_CHUNKS = SEG_LEN // SCAN_CHUNK
N_PS = 8
TILE_TOKENS = N_PS * SEG_LEN
N_TILES = T_ALL // TILE_TOKENS
SAMPLE_TILE = T_PROMPT // TILE_TOKENS
SEGS_PER_SAMPLE_SEQ = SAMPLE_LEN // SEG_LEN
J_ROWS = SEG_CHUNKS * N_PS
ALL_J_ROWS = N_TILES * J_ROWS
GROUP_BLOCK = 8
PREP_GROUPS = 8
MOD_ROWS = 8
MOD_K_STEPS = 4

J_PER_STEP = 4
TOKEN_TILE = J_PER_STEP * J_ROWS
IN_J_PER_STEP = 8
PAIR_SLOT_A = (0, 0, 0, 1, 1, 3)
PAIR_SLOT_B = (1, 2, 3, 3, 2, 2)
MOE_TM = 256
TAIL_STEPS = 4
TAIL_QUARTER = MOE_TM // TAIL_STEPS
N_CLASSES = N_EXPERT_GROUPS * len(PAIR_SLOT_A)
MOE_TILES = T_ALL // MOE_TM + N_CLASSES
MOE_ROWS = MOE_TILES * MOE_TM
SPARE_ROWS = N_CLASSES * TAIL_QUARTER
OUT_ROWS = T_ALL + SPARE_ROWS
ROUTE_LANES = 128
HX_WIDTH = 2 * D_MODEL + ROUTE_LANES
VMEM_LIMIT = 56 * 1024 * 1024
TRANS_B = (((1,), (1,)), ((), ()))


def _sigmoid(x):
    return 1.0 / (1.0 + jnp.exp(-x))


def _gelu_tanh(x):
    c = math.sqrt(2.0 / math.pi)
    return x * (0.5 * (1.0 + jnp.tanh(c * (x + 0.044715 * (x * x * x)))))


def _split_bf16(a):
    hi = a.astype(BF16)
    return hi, (a - hi.astype(F32)).astype(BF16)


def _rmsnorm(x, w):
    return x * lax.rsqrt(jnp.mean(x * x, axis=-1, keepdims=True) + EPS) * w


def _per_ps(fn, a, *mods):
    rows, d = a.shape
    out = fn(a.reshape(rows // N_PS, N_PS, d), *[m[None] for m in mods])
    return out.reshape(rows, d)


def _params(*sem):
    return pltpu.CompilerParams(dimension_semantics=sem, vmem_limit_bytes=VMEM_LIMIT)


def _mod_kernel(c_ref, w_ref, b_ref, o_ref):
    c = c_ref[...]
    part = jnp.dot((c * _sigmoid(c)).astype(BF16), w_ref[...].astype(BF16), preferred_element_type=F32)

    @pl.when(pl.program_id(1) == 0)
    def _():
        o_ref[...] = part + b_ref[...]

    @pl.when(pl.program_id(1) > 0)
    def _():
        o_ref[...] += part


def _first_step():
    return jnp.logical_and(pl.program_id(0) == 0, pl.program_id(1) == 0)


def _in_kernel(x_ref, mod_ref, nw_ref, w_ref, lnw_ref, lnb_ref, ug_ref, vn_ref, xt_ref, wg_ref, wst_ref):
    @pl.when(_first_step())
    def _():
        wg_ref[...] = w_ref[:, D_SSM:].astype(BF16)
        wst_ref[...] = w_ref[:, :D_SSM].T.astype(BF16)

    y = _rmsnorm(x_ref[...], nw_ref[...])
    h = _per_ps(lambda a, sc, sh: a * (1.0 + sc) + sh, y, mod_ref[1], mod_ref[0]).astype(BF16)
    zg = _gelu_tanh(jnp.dot(h, wg_ref[...], preferred_element_type=F32))
    v = zg[:, D_GMLP:]
    mu = jnp.mean(v, axis=-1, keepdims=True)
    vc = v - mu
    var = jnp.mean(vc * vc, axis=-1, keepdims=True)
    vn = vc * lax.rsqrt(var + EPS) * lnw_ref[...] + lnb_ref[...]
    for hd in range(GMLP_HEADS):
        cols = slice(hd * GMLP_HEAD_DIM, (hd + 1) * GMLP_HEAD_DIM)
        ug_ref[hd] = zg[:, cols]
        vn_ref[hd] = vn[:, cols]
    xt = lax.dot_general(wst_ref[...], h, TRANS_B, preferred_element_type=F32).astype(BF16)
    for k in range(IN_J_PER_STEP):
        xt_ref[k] = xt[:, k * J_ROWS:(k + 1) * J_ROWS]


def _input_proj(x, mod8, l, norm1_w, w_in, ln_w, ln_b):
    tm = IN_J_PER_STEP * J_ROWS
    steps = TILE_TOKENS // tm
    tok = lambda n: pl.BlockSpec((tm, n), lambda t, s: (t * steps + s, 0))
    heads = pl.BlockSpec((GMLP_HEADS, tm, GMLP_HEAD_DIM), lambda t, s: (0, t * steps + s, 0))
    lay = lambda *shape: pl.BlockSpec((None,) + shape, lambda t, s: (l,) + (0,) * len(shape))
    return pl.pallas_call(
        _in_kernel,
        grid=(N_TILES, steps),
        in_specs=[
            tok(D_MODEL),
            pl.BlockSpec((None, None, N_MOD, N_PS, D_MODEL), lambda t, s: (l, t, 0, 0, 0)),
            lay(1, D_MODEL), lay(D_MODEL, D_SSM + 2 * D_GMLP), lay(1, D_GMLP), lay(1, D_GMLP),
        ],
        out_specs=[heads, heads,
                   pl.BlockSpec((IN_J_PER_STEP, D_SSM, J_ROWS), lambda t, s: (s, 0, t))],
        out_shape=[jax.ShapeDtypeStruct((GMLP_HEADS, T_ALL, GMLP_HEAD_DIM), F32)] * 2 + [
            jax.ShapeDtypeStruct((SCAN_CHUNK, D_SSM, ALL_J_ROWS), BF16)],
        scratch_shapes=[pltpu.VMEM((D_MODEL, 2 * D_GMLP), BF16), pltpu.VMEM((D_SSM, D_MODEL), BF16)],
        compiler_params=_params("arbitrary", "arbitrary"),
        name="norm1_in_proj",
    )(x, mod8, norm1_w.reshape(DEPTH, 1, D_MODEL), w_in,
      ln_w.reshape(DEPTH, 1, D_GMLP), ln_b.reshape(DEPTH, 1, D_GMLP))


def _shift_lanes_right(a, b, s, lane):
    if s == 0:
        return a, b
    if s == 128:
        return jnp.zeros_like(a), a
    if s < 128:
        ra = pltpu.roll(a, s, 1)
        rb = pltpu.roll(b, s, 1)
        return jnp.where(lane >= s, ra, 0.0), jnp.where(lane >= s, rb, ra)
    t = s - 128
    return jnp.zeros_like(a), jnp.where(lane >= t, pltpu.roll(a, t, 1), 0.0)


def _shift_lanes_left(a, b, s, lane):
    if s == 0:
        return a, b
    if s == 128:
        return b, jnp.zeros_like(b)
    if s < 128:
        ra = pltpu.roll(a, 128 - s, 1)
        rb = pltpu.roll(b, 128 - s, 1)
        return jnp.where(lane < 128 - s, ra, rb), jnp.where(lane < 128 - s, rb, 0.0)
    t = s - 128
    return jnp.where(lane < 128 - t, pltpu.roll(b, 128 - t, 1), 0.0), jnp.zeros_like(b)


def _prep_kernel(*refs):
    for g in range(PREP_GROUPS):
        _prep_group(*[r.at[g] for r in refs])


def _prep_group(vec_ref, mat_ref, wft_ref, cct_ref, ttt_ref, a16_ref, wf_scr, cm_scr, tt_scr):
    a_re = vec_ref[0:1, :]
    a_im = vec_ref[1:2, :]
    dt = jnp.exp(vec_ref[2:3, :])
    d_skip = vec_ref[3:4, :]
    mag = jnp.exp(a_re * dt)
    ang = a_im * dt
    ab_r = mag * jnp.cos(ang)
    ab_i = mag * jnp.sin(ang)
    den = a_re * a_re + a_im * a_im
    nr = ab_r - 1.0
    q_r = (nr * a_re + ab_i * a_im) / den
    q_i = (ab_i * a_re - nr * a_im) / den
    bt_r = mat_ref[0]
    bt_i = mat_ref[1]
    c_r = mat_ref[2]
    c_i = mat_ref[3]
    bb_r = q_r * bt_r - q_i * bt_i
    bb_i = q_r * bt_i + q_i * bt_r
    p_r = [jnp.ones_like(ab_r)]
    p_i = [jnp.zeros_like(ab_r)]
    for _ in range(SCAN_CHUNK):
        pr, pi = p_r[-1], p_i[-1]
        p_r.append(pr * ab_r - pi * ab_i)
        p_i.append(pr * ab_i + pi * ab_r)
    a16_ref[0] = jnp.broadcast_to(p_r[SCAN_CHUNK], (N_PS, 128))
    a16_ref[1] = jnp.broadcast_to(p_i[SCAN_CHUNK], (N_PS, 128))

    lane = lax.broadcasted_iota(jnp.int32, (1, 128), 1)
    is_fwd = lane < SSM_STATE

    def pick(mf, mb):
        return jnp.where(is_fwd, p_r[mf], p_r[mb]), jnp.where(is_fwd, p_i[mf], p_i[mb])

    for j in range(SCAN_CHUNK):
        rows = slice(j * SSM_GROUP, (j + 1) * SSM_GROUP)
        wr, wi = pick(SCAN_CHUNK - 1 - j, j)
        wf_scr[rows, 0:128] = bb_r * wr - bb_i * wi
        wf_scr[rows, 128:256] = bb_r * wi + bb_i * wr
        wr, wi = pick(j + 1, SCAN_CHUNK - j)
        cct_ref[rows, 0:128] = (c_r * wr - c_i * wi).astype(BF16)
        cct_ref[rows, 128:256] = (-(c_r * wi + c_i * wr)).astype(BF16)
        wr, wi = pick(j, SCAN_CHUNK - 1 - j)
        cm_scr[rows, 0:128] = c_r * wr - c_i * wi
        cm_scr[rows, 128:256] = c_r * wi + c_i * wr
    wft_ref[...] = wf_scr[...].T.astype(BF16)

    zero = jnp.zeros_like(bb_r)
    cm_hi, cm_lo = _split_bf16(cm_scr[...])
    dot_t = lambda a, b: lax.dot_general(a, b, TRANS_B, preferred_element_type=F32)

    def lag_rows(keep):
        lhs = jnp.concatenate([jnp.where(keep, bb_r, zero), jnp.where(keep, -bb_i, zero)], axis=1)
        hi, lo = _split_bf16(lhs)
        return dot_t(hi, cm_hi) + (dot_t(hi, cm_lo) + dot_t(lo, cm_hi))

    mf = lag_rows(is_fwd)
    mb = lag_rows(jnp.logical_not(is_fwd))
    mf_a, mf_b = mf[:, 0:128], mf[:, 128:256]
    mb_a, mb_b = mb[:, 0:128], mb[:, 128:256]
    row_h = lax.broadcasted_iota(jnp.int32, (SSM_GROUP, 128), 0)
    lane_h = lax.broadcasted_iota(jnp.int32, (SSM_GROUP, 128), 1)
    for jp in range(SCAN_CHUNK):
        rows = slice(jp * SSM_GROUP, (jp + 1) * SSM_GROUP)
        fa, fb = _shift_lanes_right(mf_a, mf_b, SSM_GROUP * jp, lane)
        ba, bb = _shift_lanes_left(mb_a, mb_b, SSM_GROUP * (SCAN_CHUNK - 1 - jp), lane)
        diag = SSM_GROUP * jp + row_h
        tt_scr[rows, 0:128] = fa + ba + jnp.where(lane_h == diag, d_skip, 0.0)
        tt_scr[rows, 128:256] = fb + bb + jnp.where(lane_h + 128 == diag, d_skip, 0.0)
    ttt_ref[...] = tt_scr[...].T.astype(BF16)


def _mod_and_operators_kernel(c_ref, w_ref, b_ref, vec_ref, mat_ref, o_ref, *operator_refs):
    _mod_kernel(c_ref, w_ref, b_ref, o_ref)
    _prep_kernel(vec_ref, mat_ref, *operator_refs)


def _modulation_and_operators(cvec, w_mod, b_mod, vecs, mats):
    assert N_GROUPS // PREP_GROUPS == MOD_K_STEPS
    kb = D_MODEL // MOD_K_STEPS
    c_blocks = cvec.reshape(MOD_ROWS, MOD_K_STEPS, kb).transpose(1, 0, 2)
    op = jax.ShapeDtypeStruct((DEPTH, N_GROUPS, CHUNK_WIDTH, CHUNK_WIDTH), BF16)
    pg = PREP_GROUPS
    op_spec = pl.BlockSpec((None, pg, CHUNK_WIDTH, CHUNK_WIDTH), lambda l, g: (l, g, 0, 0))
    sq = pltpu.VMEM((pg, CHUNK_WIDTH, CHUNK_WIDTH), F32)
    return pl.pallas_call(
        _mod_and_operators_kernel,
        grid=(DEPTH, MOD_K_STEPS),
        in_specs=[
            pl.BlockSpec((None, MOD_ROWS, kb), lambda l, k: (k, 0, 0)),
            pl.BlockSpec((None, kb, N_MOD * D_MODEL), lambda l, k: (l, k, 0)),
            pl.BlockSpec((None, 1, N_MOD * D_MODEL), lambda l, k: (l, 0, 0)),
            pl.BlockSpec((None, pg, 8, 128), lambda l, g: (l, g, 0, 0)),
            pl.BlockSpec((None, pg, 4, SSM_GROUP, 128), lambda l, g: (l, g, 0, 0, 0)),
        ],
        out_specs=[pl.BlockSpec((None, MOD_ROWS, N_MOD * D_MODEL), lambda l, k: (l, 0, 0)),
                   op_spec, op_spec, op_spec,
                   pl.BlockSpec((None, pg, 2, N_PS, 128), lambda l, g: (l, g, 0, 0, 0))],
        out_shape=[jax.ShapeDtypeStruct((DEPTH, MOD_ROWS, N_MOD * D_MODEL), F32),
                   op, op, op, jax.ShapeDtypeStruct((DEPTH, N_GROUPS, 2, N_PS, 128), F32)],
        scratch_shapes=[sq, sq, sq],
        compiler_params=_params("arbitrary", "arbitrary"),
        name="adaln_and_s5_operators",
    )(c_blocks, w_mod, b_mod.reshape(DEPTH, 1, N_MOD * D_MODEL), vecs, mats)


def _ssm_kernel(xt_ref, wft_ref, cct_ref, ttt_ref, a16_ref, s0_ref, yt_ref, fin_ref, s_scr, f_scr, ft_scr):
    gb = GROUP_BLOCK
    is_fwd = lax.broadcasted_iota(jnp.int32, (1, 1, 128), 2) < SSM_STATE
    half = SSM_STATE

    def group_x(gl):
        return xt_ref[:, gl * SSM_GROUP:(gl + 1) * SSM_GROUP, :].reshape(CHUNK_WIDTH, ALL_J_ROWS)

    for gl in range(gb):
        ft_scr[gl] = jnp.dot(wft_ref[gl], group_x(gl), preferred_element_type=F32)
        f_scr[gl] = ft_scr[gl].T

    def scan(tiles, s_r, s_i):
        rows = len(tiles) * N_PS
        a_r = jnp.broadcast_to(a16_ref[:, 0, 0:1, :], (gb, rows, 128))
        a_i = jnp.broadcast_to(a16_ref[:, 1, 0:1, :], (gb, rows, 128))
        for i in range(SEG_CHUNKS):
            f_r, f_i = [], []
            for k, t in enumerate(tiles):
                st = slice(k * N_PS, (k + 1) * N_PS)
                rf = slice(t * J_ROWS + i * N_PS, t * J_ROWS + (i + 1) * N_PS)
                rb = slice(t * J_ROWS + (SEG_CHUNKS - 1 - i) * N_PS, t * J_ROWS + (SEG_CHUNKS - i) * N_PS)
                s_scr[:, rf, 0:half] = s_r[:, st, 0:half]
                s_scr[:, rf, 128:128 + half] = s_i[:, st, 0:half]
                s_scr[:, rb, half:128] = s_r[:, st, half:128]
                s_scr[:, rb, 128 + half:256] = s_i[:, st, half:128]
                f_r.append(jnp.where(is_fwd, f_scr[:, rf, 0:128], f_scr[:, rb, 0:128]))
                f_i.append(jnp.where(is_fwd, f_scr[:, rf, 128:256], f_scr[:, rb, 128:256]))
            f_r = jnp.concatenate(f_r, axis=1)
            f_i = jnp.concatenate(f_i, axis=1)
            s_r, s_i = a_r * s_r - a_i * s_i + f_r, a_r * s_i + a_i * s_r + f_i
        return s_r, s_i

    zeros = jnp.zeros((gb, N_TILES * N_PS, 128), F32)
    z_r, z_i = scan(list(range(N_TILES)), zeros, zeros)
    fin_ref[:, :, 0:128] = z_r
    fin_ref[:, :, 128:256] = z_i

    z_r = z_r[:, SAMPLE_TILE * N_PS:, :]
    z_i = z_i[:, SAMPLE_TILE * N_PS:, :]
    b_r, b_i = a16_ref[:, 0], a16_ref[:, 1]
    for _ in range(4):
        b_r, b_i = b_r * b_r - b_i * b_i, 2.0 * (b_r * b_i)
    seg = lax.broadcasted_iota(jnp.int32, (gb, N_PS, 128), 1) % SEGS_PER_SAMPLE_SEQ
    i_r = s0_ref[:, :, 0:128]
    i_i = s0_ref[:, :, 128:256]
    for step in range(1, SEGS_PER_SAMPLE_SEQ):
        pr = jnp.where(is_fwd, pltpu.roll(i_r, 1, 1), pltpu.roll(i_r, N_PS - 1, 1))
        pi = jnp.where(is_fwd, pltpu.roll(i_i, 1, 1), pltpu.roll(i_i, N_PS - 1, 1))
        zr = jnp.where(is_fwd, pltpu.roll(z_r, 1, 1), pltpu.roll(z_r, N_PS - 1, 1))
        zi = jnp.where(is_fwd, pltpu.roll(z_i, 1, 1), pltpu.roll(z_i, N_PS - 1, 1))
        n_r = b_r * pr - b_i * pi + zr
        n_i = b_r * pi + b_i * pr + zi
        first = jnp.where(is_fwd, step, 0)
        last = jnp.where(is_fwd, SEGS_PER_SAMPLE_SEQ - 1, SEGS_PER_SAMPLE_SEQ - 1 - step)
        upd = jnp.logical_and(seg >= first, seg <= last)
        i_r = jnp.where(upd, n_r, i_r)
        i_i = jnp.where(upd, n_i, i_i)
    scan([SAMPLE_TILE], i_r, i_i)

    for gl in range(gb):
        yt = jnp.dot(ttt_ref[gl], group_x(gl), preferred_element_type=F32)
        yt += lax.dot_general(cct_ref[gl], s_scr[gl].astype(BF16), TRANS_B, preferred_element_type=F32)
        yt_ref[:, gl * SSM_GROUP:(gl + 1) * SSM_GROUP, :] = yt.reshape(SCAN_CHUNK, SSM_GROUP, ALL_J_ROWS)


def _ssm_scan(xt, wft, cct, ttt, a16, s0, l):
    gb = GROUP_BLOCK
    op_spec = pl.BlockSpec((None, gb, CHUNK_WIDTH, CHUNK_WIDTH), lambda g: (l, g, 0, 0))
    io_spec = pl.BlockSpec((SCAN_CHUNK, gb * SSM_GROUP, ALL_J_ROWS), lambda g: (0, g, 0))
    rows = pltpu.VMEM((gb, ALL_J_ROWS, CHUNK_WIDTH), F32)
    return pl.pallas_call(
        _ssm_kernel,
        grid=(N_GROUPS // gb,),
        in_specs=[
            io_spec, op_spec, op_spec, op_spec,
            pl.BlockSpec((None, gb, 2, N_PS, 128), lambda g: (l, g, 0, 0, 0)),
            pl.BlockSpec((gb, N_PS, CHUNK_WIDTH), lambda g: (g, 0, 0)),
        ],
        out_specs=[io_spec, pl.BlockSpec((gb, N_TILES * N_PS, CHUNK_WIDTH), lambda g: (g, 0, 0))],
        out_shape=[
            jax.ShapeDtypeStruct((SCAN_CHUNK, D_SSM, ALL_J_ROWS), F32),
            jax.ShapeDtypeStruct((N_GROUPS, N_TILES * N_PS, CHUNK_WIDTH), F32),
        ],
        scratch_shapes=[rows, rows, pltpu.VMEM((gb, CHUNK_WIDTH, ALL_J_ROWS), F32)],
        compiler_params=_params("arbitrary"),
        name="s5_chunk_scan",
    )(xt, wft, cct, ttt, a16, s0)


def _post_kernel(x_ref, yt_ref, ug_ref, vn_ref, mod_ref, wglu_ref, ws_ref, bs_ref, wout_f32_ref,
                 nw_ref, rwt_ref, rb_ref, hx_ref, route_ref, yg_scr, y_scr, wglut_ref, wout_ref):
    tile = pl.program_id(0)
    step = pl.program_id(1)

    @pl.when(_first_step())
    def _():
        wglut_ref[...] = wglu_ref[...].T.astype(BF16)
        wout_ref[...] = wout_f32_ref[...].astype(BF16)

    @pl.when(step == 0)
    def _():
        def chunk(n, carry):
            ps = n % N_PS
            c_hi = n // N_PS
            base = c_hi * (GMLP_CHUNK // SCAN_CHUNK) * N_PS + ps
            rows = [pl.ds(j * J_ROWS + base, GMLP_CHUNK // SCAN_CHUNK, stride=N_PS) for j in range(SCAN_CHUNK)]
            for h in range(GMLP_HEADS):
                v = jnp.concatenate([vn_ref[h, r, :] for r in rows], axis=0).astype(BF16)
                u = jnp.concatenate([ug_ref[h, r, :] for r in rows], axis=0)
                s = jnp.dot(ws_ref[h], v, preferred_element_type=F32) + bs_ref[h]
                yg = u * s
                for j, r in enumerate(rows):
                    yg_scr[h, r, :] = yg[j * 8:(j + 1) * 8]
            return carry

        for n in range(TILE_TOKENS // GMLP_CHUNK):
            chunk(n, 0)

    yt = _gelu_tanh(jnp.concatenate([yt_ref[k] for k in range(J_PER_STEP)], axis=1))
    yt = yt * _sigmoid(jnp.dot(wglut_ref[...], yt.astype(BF16), preferred_element_type=F32))
    for k in range(J_PER_STEP):
        y_scr[k * J_ROWS:(k + 1) * J_ROWS, :] = yt[:, k * J_ROWS:(k + 1) * J_ROWS].T
    row0 = pl.multiple_of(step * TOKEN_TILE, TOKEN_TILE)
    proj = jnp.dot(y_scr[...].astype(BF16), wout_ref[0:D_SSM, :], preferred_element_type=F32)
    yg = jnp.concatenate([yg_scr[h, pl.ds(row0, TOKEN_TILE), :] for h in range(GMLP_HEADS)], axis=1)
    proj += jnp.dot(yg.astype(BF16), wout_ref[D_SSM:, :], preferred_element_type=F32)
    x1 = x_ref[...] + _per_ps(lambda a, g: a * g, proj, mod_ref[2])
    h2 = _per_ps(lambda a, sc, sh: a * (1.0 + sc) + sh, _rmsnorm(x1, nw_ref[...]), mod_ref[4], mod_ref[3])
    hx_ref[:, 0:D_MODEL] = h2
    hx_ref[:, D_MODEL:2 * D_MODEL] = x1
    r_hi, r_lo = _split_bf16(rwt_ref[...])
    h_hi, h_lo = _split_bf16(h2)
    dot_t = lambda a, b: lax.dot_general(a, b, TRANS_B, preferred_element_type=F32)
    logits = dot_t(r_hi, h_hi) + (dot_t(r_hi, h_lo) + dot_t(r_lo, h_hi))
    scores = _sigmoid(logits)
    sel = scores + rb_ref[...]
    sc = [scores[e:e + 1, :] for e in range(N_EXPERTS)]
    sl = [sel[e:e + 1, :] for e in range(N_EXPERTS)]
    gscore = []
    for g in range(N_EXPERT_GROUPS):
        v0, v1, v2, v3 = sl[4 * g:4 * g + 4]
        hi01, lo01 = jnp.maximum(v0, v1), jnp.minimum(v0, v1)
        hi23, lo23 = jnp.maximum(v2, v3), jnp.minimum(v2, v3)
        top1 = jnp.maximum(hi01, hi23)
        top2 = jnp.maximum(jnp.minimum(hi01, hi23), jnp.maximum(lo01, lo23))
        gscore.append(top1 + top2)
    best = gscore[0]
    gidx = jnp.zeros_like(best, dtype=jnp.int32)
    for g in range(1, N_EXPERT_GROUPS):
        upd = gscore[g] > best
        gidx = jnp.where(upd, g, gidx)
        best = jnp.where(upd, gscore[g], best)

    def in_group(vals, k):
        out = vals[k]
        for g in range(1, N_EXPERT_GROUPS):
            out = jnp.where(gidx == g, vals[4 * g + k], out)
        return out

    v = [in_group(sl, k) for k in range(EXPERTS_PER_GROUP)]
    s = [in_group(sc, k) for k in range(EXPERTS_PER_GROUP)]
    w = []
    bits = jnp.zeros_like(gidx)
    for k in range(EXPERTS_PER_GROUP):
        rank = jnp.zeros_like(gidx)
        for j in range(EXPERTS_PER_GROUP):
            if j == k:
                continue
            ahead = (v[j] >= v[k]) if j < k else (v[j] > v[k])
            rank = rank + ahead.astype(jnp.int32)
        w.append(jnp.where(rank < 2, s[k], 0.0))
        bits = bits + jnp.where(rank < 2, 1 << k, 0)
    denom = (w[0] + w[1]) + (w[2] + w[3])
    gate = [wk / denom for wk in w]
    pair = jnp.full_like(gidx, len(PAIR_SLOT_A) - 1)
    for p in range(len(PAIR_SLOT_A) - 1):
        pair = jnp.where(bits == (1 << PAIR_SLOT_A[p]) + (1 << PAIR_SLOT_B[p]), p, pair)

    def slot_gate(table):
        out = gate[table[0]]
        for p in range(1, len(table)):
            out = jnp.where(pair == p, gate[table[p]], out)
        return out

    n_tok = route_ref.shape[1]
    route_ref[0:1, :] = (gidx * len(PAIR_SLOT_A) + pair).astype(F32)
    ps = lax.broadcasted_iota(jnp.int32, (1, n_tok), 1) % N_PS
    cond = jnp.where(tile < SAMPLE_TILE, 0, 1 + ps // SEGS_PER_SAMPLE_SEQ).astype(F32)
    lanes = jnp.concatenate([slot_gate(PAIR_SLOT_A), slot_gate(PAIR_SLOT_B), cond,
                             jnp.zeros((ROUTE_LANES - 3, n_tok), F32)], axis=0)
    hx_ref[:, 2 * D_MODEL:] = lanes.T


def _mix_out(x, yt, ug, vn, mod8, l, w_glu, w_s, b_s, w_out, norm2_w, rwt, rb):
    tm = TOKEN_TILE
    steps = TILE_TOKENS // tm
    tok = lambda n: pl.BlockSpec((tm, n), lambda t, s: (t * steps + s, 0))
    whole_tile = pl.BlockSpec((GMLP_HEADS, TILE_TOKENS, GMLP_HEAD_DIM), lambda t, s: (0, t, 0))
    lay = lambda *shape: pl.BlockSpec((None,) + shape, lambda t, s: (l,) + (0,) * len(shape))
    return pl.pallas_call(
        _post_kernel,
        grid=(N_TILES, steps),
        in_specs=[
            tok(D_MODEL),
            pl.BlockSpec((J_PER_STEP, D_SSM, J_ROWS), lambda t, s: (s, 0, t)),
            whole_tile, whole_tile,
            pl.BlockSpec((None, None, N_MOD, N_PS, D_MODEL), lambda t, s: (l, t, 0, 0, 0)),
            lay(D_SSM, D_SSM), lay(GMLP_HEADS, GMLP_CHUNK, GMLP_CHUNK),
            lay(GMLP_HEADS, GMLP_CHUNK, 1), lay(D_MODEL, D_MODEL), lay(1, D_MODEL),
            pl.BlockSpec((N_EXPERTS, D_MODEL), lambda t, s: (0, 0)),
            pl.BlockSpec((N_EXPERTS, 1), lambda t, s: (0, 0)),
        ],
        out_specs=[tok(HX_WIDTH), pl.BlockSpec((1, tm), lambda t, s: (0, t * steps + s))],
        out_shape=[
            jax.ShapeDtypeStruct((T_ALL, HX_WIDTH), F32),
            jax.ShapeDtypeStruct((1, T_ALL), F32),
        ],
        scratch_shapes=[pltpu.VMEM((GMLP_HEADS, TILE_TOKENS, GMLP_HEAD_DIM), F32), pltpu.VMEM((tm, D_SSM), F32),
                        pltpu.VMEM((D_SSM, D_SSM), BF16), pltpu.VMEM((D_MODEL, D_MODEL), BF16)],
        compiler_params=_params("arbitrary", "arbitrary"),
        name="mixers_out_router",
    )(x, yt, ug, vn, mod8, w_glu, w_s, b_s, w_out, norm2_w.reshape(DEPTH, 1, D_MODEL), rwt, rb)


def _moe_kernel(ea_ref, eb_ref, nv_ref, size_ref, src_ref, drow_ref,
                hx_hbm, wga_ref, wua_ref, wda_ref, wgb_ref, wub_ref, wdb_ref, g2_ref, fw_ref,
                o_hbm, hx_buf, o_buf, z_buf, g_sem, s_sem, z_sem, *, final):
    n = pl.program_id(0)
    n_valid = nv_ref[0]
    slot = n % 2

    def by_size(tile, fn):
        for quarters in range(1, TAIL_STEPS + 1):
            @pl.when(size_ref[tile] == quarters)
            def _(rows=quarters * TAIL_QUARTER):
                fn(rows)

    def start_gather(tile, sl, rows):
        base = tile * MOE_TM
        for r in range(rows):
            pltpu.make_async_copy(hx_hbm.at[pl.ds(src_ref[base + r], 1)], hx_buf.at[sl, pl.ds(r, 1)],
                                  g_sem.at[sl]).start()

    def wait_gather(sl, rows):
        pltpu.make_async_copy(hx_hbm.at[pl.ds(0, rows)], hx_buf.at[sl, pl.ds(0, rows)], g_sem.at[sl]).wait()

    def start_scatter(tile, sl, rows):
        base = tile * MOE_TM
        for r in range(rows):
            pltpu.make_async_copy(o_buf.at[sl, pl.ds(r, 1)], o_hbm.at[pl.ds(drow_ref[base + r], 1)],
                                  s_sem.at[sl]).start()

    def wait_scatter(sl, rows):
        pltpu.make_async_copy(o_buf.at[sl, pl.ds(0, rows)], o_hbm.at[pl.ds(0, rows)], s_sem.at[sl]).wait()

    @pl.when(n == 0)
    def _():
        z_buf[...] = jnp.zeros_like(z_buf)
        spare = [pltpu.make_async_copy(z_buf, o_hbm.at[pl.ds(T_ALL + i * MOE_TM, MOE_TM)], z_sem)
                 for i in range(SPARE_ROWS // MOE_TM)]
        for cp in spare:
            cp.start()
        for cp in spare:
            cp.wait()
        by_size(0, lambda rows: start_gather(0, 0, rows))

    @pl.when(n + 1 < n_valid)
    def _():
        by_size(n + 1, lambda rows: start_gather(n + 1, 1 - slot, rows))

    def tile_step(rows):
        wait_gather(slot, rows)

        @pl.when(n >= 2)
        def _():
            by_size(n - 2, lambda r: wait_scatter(slot, r))

        h = hx_buf[slot, 0:rows, 0:D_MODEL].astype(BF16)
        lanes = hx_buf[slot, 0:rows, 2 * D_MODEL:]

        def expert(wg, wu, wd, gate):
            hg = jnp.dot(h, wg[...].astype(BF16), preferred_element_type=F32)
            hu = jnp.dot(h, wu[...].astype(BF16), preferred_element_type=F32)
            act = hg * _sigmoid(hg) * hu * gate
            return jnp.dot(act.astype(BF16), wd[...].astype(BF16), preferred_element_type=F32)

        y = expert(wga_ref, wua_ref, wda_ref, lanes[:, 0:1]) + expert(wgb_ref, wub_ref, wdb_ref, lanes[:, 1:2])
        cond_row = lanes[:, 2:3]
        gate2 = jnp.where(cond_row == 0.0, g2_ref[0:1, :], jnp.where(cond_row == 1.0, g2_ref[1:2, :], g2_ref[2:3, :]))
        x2 = hx_buf[slot, 0:rows, D_MODEL:2 * D_MODEL] + gate2 * y
        o_buf[slot, 0:rows] = _rmsnorm(x2, fw_ref[...]) if final else x2
        start_scatter(n, slot, rows)

    @pl.when(n < n_valid)
    def _():
        by_size(n, tile_step)

    @pl.when(n == MOE_TILES - 1)
    def _():
        last = n_valid - 1
        by_size(last, lambda r: wait_scatter(last % 2, r))
        by_size(last - 1, lambda r: wait_scatter(1 - last % 2, r))


def _experts(tables, hx, mods, l, w_gate, w_up, w_down, final_w, final):
    w_a = lambda r, c: pl.BlockSpec((None, None, r, c), lambda n, ea, eb, *_: (l, ea[n], 0, 0))
    w_b = lambda r, c: pl.BlockSpec((None, None, r, c), lambda n, ea, eb, *_: (l, eb[n], 0, 0))
    return pl.pallas_call(
        functools.partial(_moe_kernel, final=final),
        grid_spec=pltpu.PrefetchScalarGridSpec(
            num_scalar_prefetch=len(tables),
            grid=(MOE_TILES,),
            in_specs=[
                pl.BlockSpec(memory_space=pl.ANY),
                w_a(D_MODEL, D_EXPERT), w_a(D_MODEL, D_EXPERT), w_a(D_EXPERT, D_MODEL),
                w_b(D_MODEL, D_EXPERT), w_b(D_MODEL, D_EXPERT), w_b(D_EXPERT, D_MODEL),
                pl.BlockSpec((None, MOD_ROWS, D_MODEL), lambda n, *_: (l, 0, N_MOD - 1)),
                pl.BlockSpec((1, D_MODEL), lambda n, *_: (0, 0)),
            ],
            out_specs=pl.BlockSpec(memory_space=pl.ANY),
            scratch_shapes=[pltpu.VMEM((2, MOE_TM, HX_WIDTH), F32), pltpu.VMEM((2, MOE_TM, D_MODEL), F32),
                            pltpu.VMEM((MOE_TM, D_MODEL), F32),
                            pltpu.SemaphoreType.DMA((2,)), pltpu.SemaphoreType.DMA((2,)), pltpu.SemaphoreType.DMA(())],
        ),
        out_shape=jax.ShapeDtypeStruct((OUT_ROWS, D_MODEL), F32),
        compiler_params=_params("arbitrary"),
        name="experts",
    )(*tables, hx, w_gate, w_up, w_down, w_gate, w_up, w_down, mods, final_w.reshape(1, D_MODEL))


def _count_before(flags):
    n, k = flags.shape
    blocks = flags.reshape(n // 128, 128, k).astype(F32)
    strictly_lower = jnp.tril(jnp.ones((128, 128), F32), -1)
    within = jnp.einsum("ij,bjk->bik", strictly_lower, blocks)
    totals = jnp.sum(blocks, axis=1)
    before = jnp.cumsum(totals, axis=0) - totals
    return (within + before[:, None, :]).reshape(n, k).astype(jnp.int32)


def _routing_tables(cls, to_sequence_order):
    n_cls = N_EXPERT_GROUPS * len(PAIR_SLOT_A)
    onehot = (cls[:, None] == jnp.arange(n_cls, dtype=jnp.int32)[None, :]).astype(jnp.int32)
    counts = jnp.sum(onehot, axis=0)
    tiles = (counts + MOE_TM - 1) // MOE_TM
    tile_end = jnp.cumsum(tiles)
    row_in_class = (tile_end - tiles)[None, :] * MOE_TM + _count_before(onehot)
    dst = jnp.sum(onehot * row_in_class, axis=1)
    token_plus_1 = jnp.zeros((MOE_ROWS,), jnp.int32).at[dst].set(jnp.arange(1, T_ALL + 1, dtype=jnp.int32))
    is_pad = token_plus_1 == 0
    src = jnp.maximum(token_plus_1 - 1, 0)
    n_valid = tile_end[-1]
    tile_id = jnp.minimum(jnp.arange(MOE_TILES, dtype=jnp.int32), n_valid - 1)
    tile_cls = jnp.sum((tile_end[None, :] <= tile_id[:, None]).astype(jnp.int32), axis=1)
    group, pair = tile_cls // len(PAIR_SLOT_A), tile_cls % len(PAIR_SLOT_A)
    in_cls = (tile_cls[:, None] == jnp.arange(n_cls, dtype=jnp.int32)[None, :]).astype(jnp.int32)
    left = jnp.sum(in_cls * (counts[None, :] - (tile_id[:, None] - (tile_end - tiles)[None, :]) * MOE_TM), axis=1)
    size = jnp.clip((left + TAIL_QUARTER - 1) // TAIL_QUARTER, 1, TAIL_STEPS).astype(jnp.int32)
    row = jnp.arange(MOE_ROWS, dtype=jnp.int32)
    processed = jnp.logical_and(row % MOE_TM < jnp.repeat(size, MOE_TM) * TAIL_QUARTER, row // MOE_TM < n_valid)
    moved_pad = jnp.logical_and(is_pad, processed)
    spare = T_ALL + _count_before(moved_pad.astype(jnp.int32)[:, None])[:, 0]
    if to_sequence_order:
        j, c, ps = (src // J_ROWS) % SCAN_CHUNK, (src // N_PS) % SEG_CHUNKS, src % N_PS
        target = (src // TILE_TOKENS) * TILE_TOKENS + ps * SEG_LEN + c * SCAN_CHUNK + j
    else:
        target = src
    drow = jnp.where(is_pad, jnp.where(moved_pad, spare, T_ALL), target).astype(jnp.int32)

    def slot_expert(table):
        local = jnp.full_like(pair, table[-1])
        for p in range(len(table) - 1):
            local = jnp.where(pair == p, table[p], local)
        return group * EXPERTS_PER_GROUP + local

    e_a = slot_expert(PAIR_SLOT_A)
    e_b = slot_expert(PAIR_SLOT_B)
    return e_a, e_b, n_valid.reshape(1).astype(jnp.int32), size, src, drow


def _dirs_on_lanes(p):
    p = jnp.moveaxis(p, 1, -2)
    return p.reshape(p.shape[:-2] + (2 * SSM_STATE,))


def _to_internal_order(x):
    x = x.reshape(N_TILES, N_PS, SEG_CHUNKS, SCAN_CHUNK, D_MODEL)
    return x.transpose(0, 3, 2, 1, 4).reshape(T_ALL, D_MODEL)


def _gmlp_position_order(w):
    n_lo = GMLP_CHUNK // SCAN_CHUNK
    lead = w.shape[:2]
    w = w.reshape(lead + (n_lo, SCAN_CHUNK) + w.shape[3:])
    w = jnp.swapaxes(w, 2, 3)
    return w.reshape(lead + (GMLP_CHUNK,) + w.shape[4:])


def kernel(x_prompt, x_sample, c, state_ssm_re, state_ssm_im, c_ctx, norm1_w, norm2_w, w_mod, b_mod, w_in,
           ssm_a_re, ssm_a_im, ssm_log_dt, ssm_b_re, ssm_b_im, ssm_c_re, ssm_c_im, ssm_d, w_glu,
           gmlp_ln_w, gmlp_ln_b, gmlp_w_s, gmlp_b_s, w_out, router_w, router_b, w_gate, w_up, w_down,
           final_norm_w):
    x = jnp.concatenate([x_prompt.reshape(T_PROMPT, D_MODEL), x_sample.reshape(T_SAMPLE, D_MODEL)], axis=0)
    x = _to_internal_order(x)

    cvec = jnp.concatenate([c_ctx[None, :], c, jnp.zeros((MOD_ROWS - 1 - N_SAMPLE_SEQ, D_MODEL), F32)], axis=0)
    log_dt = jnp.broadcast_to(ssm_log_dt[..., None], ssm_a_re.shape)
    d_lanes = jnp.tile(ssm_d.reshape(DEPTH, N_GROUPS, SSM_GROUP), (1, 1, 128 // SSM_GROUP))
    vecs = jnp.stack([_dirs_on_lanes(ssm_a_re), _dirs_on_lanes(ssm_a_im), _dirs_on_lanes(log_dt), d_lanes], axis=2)
    vecs = jnp.concatenate([vecs, jnp.zeros((DEPTH, N_GROUPS, 4, 2 * SSM_STATE), F32)], axis=2)
    mats = jnp.stack([_dirs_on_lanes(jnp.swapaxes(ssm_b_re, -1, -2)), _dirs_on_lanes(jnp.swapaxes(ssm_b_im, -1, -2)),
                      _dirs_on_lanes(ssm_c_re), _dirs_on_lanes(ssm_c_im)], axis=2)
    mods, wft, cct, ttt, a16 = _modulation_and_operators(cvec, w_mod, b_mod, vecs, mats)
    ps_row = [[0] * N_PS] * SAMPLE_TILE + [[1 + p // SEGS_PER_SAMPLE_SEQ for p in range(N_PS)]]
    mod8 = mods.reshape(DEPTH, MOD_ROWS, N_MOD, D_MODEL)[:, jnp.array(ps_row, jnp.int32)]
    mod8 = mod8.transpose(0, 1, 3, 2, 4)

    w_s = jnp.swapaxes(_gmlp_position_order(jnp.swapaxes(_gmlp_position_order(gmlp_w_s), 2, 3)), 2, 3).astype(BF16)
    b_s = _gmlp_position_order(gmlp_b_s)[..., None]
    rwt = router_w.T
    rb = router_b.reshape(N_EXPERTS, 1)

    new_re, new_im = [], []
    for l in range(DEPTH):
        ug, vn, xt = _input_proj(x, mod8, l, norm1_w, w_in, gmlp_ln_w, gmlp_ln_b)
        s0 = jnp.concatenate([state_ssm_re[:, l].transpose(2, 0, 1, 3).reshape(N_GROUPS, N_SAMPLE_SEQ, 128),
                              state_ssm_im[:, l].transpose(2, 0, 1, 3).reshape(N_GROUPS, N_SAMPLE_SEQ, 128)], axis=-1)
        s0 = jnp.repeat(s0, SEGS_PER_SAMPLE_SEQ, axis=1)
        yt, fin = _ssm_scan(xt, wft, cct, ttt, a16, s0, l)
        fin = fin[:, :N_PROMPT_SEQ].reshape(N_GROUPS, N_PROMPT_SEQ, 2, 2, SSM_STATE)
        fin = fin.transpose(2, 1, 3, 0, 4)
        new_re.append(fin[0])
        new_im.append(fin[1])
        hx, route = _mix_out(x, yt, ug, vn, mod8, l, w_glu, w_s, b_s, w_out, norm2_w, rwt, rb)
        final = l == DEPTH - 1
        tables = _routing_tables(route[0].astype(jnp.int32), to_sequence_order=final)
        x = _experts(tables, hx, mods, l, w_gate, w_up, w_down, final_norm_w, final)

    y_prompt = x[:T_PROMPT].reshape(N_PROMPT_SEQ, PROMPT_LEN, D_MODEL)
    y_sample = x[T_PROMPT:T_ALL].reshape(N_SAMPLE_SEQ, SAMPLE_LEN, D_MODEL)
    return (y_prompt, y_sample, jnp.stack(new_re, axis=1), jnp.stack(new_im, axis=1))
```

```python
import functools
import math

import jax
import jax.numpy as jnp
from jax import lax
from jax.experimental import pallas as pl
from jax.experimental.pallas import tpu as pltpu

F32 = jnp.float32
BF16 = jnp.bfloat16

D_MODEL = 1024
N_PROMPT_SEQ = 16
PROMPT_LEN = 256
N_SAMPLE_SEQ = 2
SAMPLE_LEN = 1024
T_PROMPT = N_PROMPT_SEQ * PROMPT_LEN
T_SAMPLE = N_SAMPLE_SEQ * SAMPLE_LEN
T_ALL = T_PROMPT + T_SAMPLE
DEPTH = 2
D_SSM = 512
SSM_GROUP = 16
N_GROUPS = 32
SSM_STATE = 64
D_GMLP = 512
GMLP_HEADS = 4
GMLP_HEAD_DIM = 128
GMLP_CHUNK = 128
N_EXPERTS = 16
N_EXPERT_GROUPS = 4
EXPERTS_PER_GROUP = 4
D_EXPERT = 512
N_MOD = 6
EPS = 1e-6

SCAN_CHUNK = 16
CHUNK_WIDTH = SCAN_CHUNK * SSM_GROUP
SEG_LEN = 256
SEG_CHUNKS = SEG_LEN // SCAN_CHUNK
N_PS = 8
TILE_TOKENS = N_PS * SEG_LEN
N_TILES = T_ALL // TILE_TOKENS
SAMPLE_TILE = T_PROMPT // TILE_TOKENS
SEGS_PER_SAMPLE_SEQ = SAMPLE_LEN // SEG_LEN
J_ROWS = SEG_CHUNKS * N_PS
ALL_J_ROWS = N_TILES * J_ROWS
GROUP_BLOCK = 8
PREP_GROUPS = 8
MOD_ROWS = 8
MOD_K_STEPS = 4

J_PER_STEP = 4
TOKEN_TILE = J_PER_STEP * J_ROWS
PAIR_SLOT_A = (0, 0, 0, 1, 1, 3)
PAIR_SLOT_B = (1, 2, 3, 3, 2, 2)
MOE_TM = 256
TAIL_STEPS = 4
TAIL_QUARTER = MOE_TM // TAIL_STEPS
N_CLASSES = N_EXPERT_GROUPS * len(PAIR_SLOT_A)
MOE_TILES = T_ALL // MOE_TM + N_CLASSES
MOE_ROWS = MOE_TILES * MOE_TM
SPARE_ROWS = N_CLASSES * TAIL_QUARTER
OUT_ROWS = T_ALL + SPARE_ROWS
ROUTE_LANES = 128
HX_WIDTH = 2 * D_MODEL + ROUTE_LANES
VMEM_LIMIT = 56 * 1024 * 1024
TRANS_B = (((1,), (1,)), ((), ()))


def _sigmoid(x):
    return 1.0 / (1.0 + jnp.exp(-x))


def _gelu_tanh(x):
    c = math.sqrt(2.0 / math.pi)
    return x * (0.5 * (1.0 + jnp.tanh(c * (x + 0.044715 * (x * x * x)))))


def _split_bf16(a):
    hi = a.astype(BF16)
    return hi, (a - hi.astype(F32)).astype(BF16)


def _rmsnorm(x, w):
    return x * lax.rsqrt(jnp.mean(x * x, axis=-1, keepdims=True) + EPS) * w


def _per_ps(fn, a, *mods):
    rows, d = a.shape
    out = fn(a.reshape(rows // N_PS, N_PS, d), *[m[None] for m in mods])
    return out.reshape(rows, d)


def _params(*sem):
    return pltpu.CompilerParams(dimension_semantics=sem, vmem_limit_bytes=VMEM_LIMIT)


def _mod_kernel(c_ref, w_ref, b_ref, o_ref):
    c = c_ref[...]
    part = jnp.dot((c * _sigmoid(c)).astype(BF16), w_ref[...].astype(BF16), preferred_element_type=F32)

    @pl.when(pl.program_id(1) == 0)
    def _():
        o_ref[...] = part + b_ref[...]

    @pl.when(pl.program_id(1) > 0)
    def _():
        o_ref[...] += part


def _first_step():
    return jnp.logical_and(pl.program_id(0) == 0, pl.program_id(1) == 0)


def _in_kernel(x_ref, mod_ref, nw_ref, w_ref, lnw_ref, lnb_ref, ug_ref, vn_ref, xt_ref, wg_ref, wst_ref):
    @pl.when(_first_step())
    def _():
        wg_ref[...] = w_ref[:, D_SSM:].astype(BF16)
        wst_ref[...] = w_ref[:, :D_SSM].T.astype(BF16)

    y = _rmsnorm(x_ref[...], nw_ref[...])
    h = _per_ps(lambda a, sc, sh: a * (1.0 + sc) + sh, y, mod_ref[1], mod_ref[0]).astype(BF16)
    zg = _gelu_tanh(jnp.dot(h, wg_ref[...], preferred_element_type=F32))
    v = zg[:, D_GMLP:]
    mu = jnp.mean(v, axis=-1, keepdims=True)
    vc = v - mu
    var = jnp.mean(vc * vc, axis=-1, keepdims=True)
    vn = vc * lax.rsqrt(var + EPS) * lnw_ref[...] + lnb_ref[...]
    for hd in range(GMLP_HEADS):
        cols = slice(hd * GMLP_HEAD_DIM, (hd + 1) * GMLP_HEAD_DIM)
        ug_ref[hd] = zg[:, cols]
        vn_ref[hd] = vn[:, cols]
    xt = lax.dot_general(wst_ref[...], h, TRANS_B, preferred_element_type=F32).astype(BF16)
    for k in range(J_PER_STEP):
        xt_ref[k] = xt[:, k * J_ROWS:(k + 1) * J_ROWS]


def _input_proj(x, mod8, l, norm1_w, w_in, ln_w, ln_b):
    tm = TOKEN_TILE
    steps = TILE_TOKENS // tm
    tok = lambda n: pl.BlockSpec((tm, n), lambda t, s: (t * steps + s, 0))
    heads = pl.BlockSpec((GMLP_HEADS, tm, GMLP_HEAD_DIM), lambda t, s: (0, t * steps + s, 0))
    lay = lambda *shape: pl.BlockSpec((None,) + shape, lambda t, s: (l,) + (0,) * len(shape))
    return pl.pallas_call(
        _in_kernel,
        grid=(N_TILES, steps),
        in_specs=[
            tok(D_MODEL),
            pl.BlockSpec((None, None, N_MOD, N_PS, D_MODEL), lambda t, s: (l, t, 0, 0, 0)),
            lay(1, D_MODEL), lay(D_MODEL, D_SSM + 2 * D_GMLP), lay(1, D_GMLP), lay(1, D_GMLP),
        ],
        out_specs=[heads, heads,
                   pl.BlockSpec((J_PER_STEP, D_SSM, J_ROWS), lambda t, s: (s, 0, t))],
        out_shape=[jax.ShapeDtypeStruct((GMLP_HEADS, T_ALL, GMLP_HEAD_DIM), F32)] * 2 + [
            jax.ShapeDtypeStruct((SCAN_CHUNK, D_SSM, ALL_J_ROWS), BF16)],
        scratch_shapes=[pltpu.VMEM((D_MODEL, 2 * D_GMLP), BF16), pltpu.VMEM((D_SSM, D_MODEL), BF16)],
        compiler_params=_params("arbitrary", "arbitrary"),
        name="norm1_in_proj",
    )(x, mod8, norm1_w.reshape(DEPTH, 1, D_MODEL), w_in,
      ln_w.reshape(DEPTH, 1, D_GMLP), ln_b.reshape(DEPTH, 1, D_GMLP))


def _shift_lanes_right(a, b, s, lane):
    if s == 0:
        return a, b
    if s == 128:
        return jnp.zeros_like(a), a
    if s < 128:
        ra = pltpu.roll(a, s, 1)
        rb = pltpu.roll(b, s, 1)
        return jnp.where(lane >= s, ra, 0.0), jnp.where(lane >= s, rb, ra)
    t = s - 128
    return jnp.zeros_like(a), jnp.where(lane >= t, pltpu.roll(a, t, 1), 0.0)


def _shift_lanes_left(a, b, s, lane):
    if s == 0:
        return a, b
    if s == 128:
        return b, jnp.zeros_like(b)
    if s < 128:
        ra = pltpu.roll(a, 128 - s, 1)
        rb = pltpu.roll(b, 128 - s, 1)
        return jnp.where(lane < 128 - s, ra, rb), jnp.where(lane < 128 - s, rb, 0.0)
    t = s - 128
    return jnp.where(lane < 128 - t, pltpu.roll(b, 128 - t, 1), 0.0), jnp.zeros_like(b)


def _prep_kernel(*refs):
    for g in range(PREP_GROUPS):
        _prep_group(*[r.at[g] for r in refs])


def _prep_group(vec_ref, mat_ref, wft_ref, cct_ref, ttt_ref, a16_ref, wf_scr, cm_scr, tt_scr):
    a_re = vec_ref[0:1, :]
    a_im = vec_ref[1:2, :]
    dt = jnp.exp(vec_ref[2:3, :])
    d_skip = vec_ref[3:4, :]
    mag = jnp.exp(a_re * dt)
    ang = a_im * dt
    ab_r = mag * jnp.cos(ang)
    ab_i = mag * jnp.sin(ang)
    den = a_re * a_re + a_im * a_im
    nr = ab_r - 1.0
    q_r = (nr * a_re + ab_i * a_im) / den
    q_i = (ab_i * a_re - nr * a_im) / den
    bt_r = mat_ref[0]
    bt_i = mat_ref[1]
    c_r = mat_ref[2]
    c_i = mat_ref[3]
    bb_r = q_r * bt_r - q_i * bt_i
    bb_i = q_r * bt_i + q_i * bt_r
    p_r = [jnp.ones_like(ab_r)]
    p_i = [jnp.zeros_like(ab_r)]
    for _ in range(SCAN_CHUNK):
        pr, pi = p_r[-1], p_i[-1]
        p_r.append(pr * ab_r - pi * ab_i)
        p_i.append(pr * ab_i + pi * ab_r)
    a16_ref[0] = jnp.broadcast_to(p_r[SCAN_CHUNK], (N_PS, 128))
    a16_ref[1] = jnp.broadcast_to(p_i[SCAN_CHUNK], (N_PS, 128))

    lane = lax.broadcasted_iota(jnp.int32, (1, 128), 1)
    is_fwd = lane < SSM_STATE

    def pick(mf, mb):
        return jnp.where(is_fwd, p_r[mf], p_r[mb]), jnp.where(is_fwd, p_i[mf], p_i[mb])

    for j in range(SCAN_CHUNK):
        rows = slice(j * SSM_GROUP, (j + 1) * SSM_GROUP)
        wr, wi = pick(SCAN_CHUNK - 1 - j, j)
        wf_scr[rows, 0:128] = bb_r * wr - bb_i * wi
        wf_scr[rows, 128:256] = bb_r * wi + bb_i * wr
        wr, wi = pick(j + 1, SCAN_CHUNK - j)
        cct_ref[rows, 0:128] = (c_r * wr - c_i * wi).astype(BF16)
        cct_ref[rows, 128:256] = (-(c_r * wi + c_i * wr)).astype(BF16)
        wr, wi = pick(j, SCAN_CHUNK - 1 - j)
        cm_scr[rows, 0:128] = c_r * wr - c_i * wi
        cm_scr[rows, 128:256] = c_r * wi + c_i * wr
    wft_ref[...] = wf_scr[...].T.astype(BF16)

    zero = jnp.zeros_like(bb_r)
    cm_hi, cm_lo = _split_bf16(cm_scr[...])
    dot_t = lambda a, b: lax.dot_general(a, b, TRANS_B, preferred_element_type=F32)

    def lag_rows(keep):
        lhs = jnp.concatenate([jnp.where(keep, bb_r, zero), jnp.where(keep, -bb_i, zero)], axis=1)
        hi, lo = _split_bf16(lhs)
        return dot_t(hi, cm_hi) + (dot_t(hi, cm_lo) + dot_t(lo, cm_hi))

    mf = lag_rows(is_fwd)
    mb = lag_rows(jnp.logical_not(is_fwd))
    mf_a, mf_b = mf[:, 0:128], mf[:, 128:256]
    mb_a, mb_b = mb[:, 0:128], mb[:, 128:256]
    row_h = lax.broadcasted_iota(jnp.int32, (SSM_GROUP, 128), 0)
    lane_h = lax.broadcasted_iota(jnp.int32, (SSM_GROUP, 128), 1)
    for jp in range(SCAN_CHUNK):
        rows = slice(jp * SSM_GROUP, (jp + 1) * SSM_GROUP)
        fa, fb = _shift_lanes_right(mf_a, mf_b, SSM_GROUP * jp, lane)
        ba, bb = _shift_lanes_left(mb_a, mb_b, SSM_GROUP * (SCAN_CHUNK - 1 - jp), lane)
        diag = SSM_GROUP * jp + row_h
        tt_scr[rows, 0:128] = fa + ba + jnp.where(lane_h == diag, d_skip, 0.0)
        tt_scr[rows, 128:256] = fb + bb + jnp.where(lane_h + 128 == diag, d_skip, 0.0)
    ttt_ref[...] = tt_scr[...].T.astype(BF16)


def _mod_and_operators_kernel(c_ref, w_ref, b_ref, vec_ref, mat_ref, o_ref, *operator_refs):
    _mod_kernel(c_ref, w_ref, b_ref, o_ref)
    _prep_kernel(vec_ref, mat_ref, *operator_refs)


def _modulation_and_operators(cvec, w_mod, b_mod, vecs, mats):
    assert N_GROUPS // PREP_GROUPS == MOD_K_STEPS
    kb = D_MODEL // MOD_K_STEPS
    c_blocks = cvec.reshape(MOD_ROWS, MOD_K_STEPS, kb).transpose(1, 0, 2)
    op = jax.ShapeDtypeStruct((DEPTH, N_GROUPS, CHUNK_WIDTH, CHUNK_WIDTH), BF16)
    pg = PREP_GROUPS
    op_spec = pl.BlockSpec((None, pg, CHUNK_WIDTH, CHUNK_WIDTH), lambda l, g: (l, g, 0, 0))
    sq = pltpu.VMEM((pg, CHUNK_WIDTH, CHUNK_WIDTH), F32)
    return pl.pallas_call(
        _mod_and_operators_kernel,
        grid=(DEPTH, MOD_K_STEPS),
        in_specs=[
            pl.BlockSpec((None, MOD_ROWS, kb), lambda l, k: (k, 0, 0)),
            pl.BlockSpec((None, kb, N_MOD * D_MODEL), lambda l, k: (l, k, 0)),
            pl.BlockSpec((None, 1, N_MOD * D_MODEL), lambda l, k: (l, 0, 0)),
            pl.BlockSpec((None, pg, 8, 128), lambda l, g: (l, g, 0, 0)),
            pl.BlockSpec((None, pg, 4, SSM_GROUP, 128), lambda l, g: (l, g, 0, 0, 0)),
        ],
        out_specs=[pl.BlockSpec((None, MOD_ROWS, N_MOD * D_MODEL), lambda l, k: (l, 0, 0)),
                   op_spec, op_spec, op_spec,
                   pl.BlockSpec((None, pg, 2, N_PS, 128), lambda l, g: (l, g, 0, 0, 0))],
        out_shape=[jax.ShapeDtypeStruct((DEPTH, MOD_ROWS, N_MOD * D_MODEL), F32),
                   op, op, op, jax.ShapeDtypeStruct((DEPTH, N_GROUPS, 2, N_PS, 128), F32)],
        scratch_shapes=[sq, sq, sq],
        compiler_params=_params("arbitrary", "arbitrary"),
        name="adaln_and_s5_operators",
    )(c_blocks, w_mod, b_mod.reshape(DEPTH, 1, N_MOD * D_MODEL), vecs, mats)


def _ssm_kernel(xt_ref, wft_ref, cct_ref, ttt_ref, a16_ref, s0_ref, yt_ref, fin_ref, s_scr, f_scr, ft_scr):
    gb = GROUP_BLOCK
    is_fwd = lax.broadcasted_iota(jnp.int32, (1, 1, 128), 2) < SSM_STATE
    half = SSM_STATE

    def group_x(gl):
        return xt_ref[:, gl * SSM_GROUP:(gl + 1) * SSM_GROUP, :].reshape(CHUNK_WIDTH, ALL_J_ROWS)

    for gl in range(gb):
        ft_scr[gl] = jnp.dot(wft_ref[gl], group_x(gl), preferred_element_type=F32)
        f_scr[gl] = ft_scr[gl].T

    def scan(tiles, s_r, s_i):
        rows = len(tiles) * N_PS
        a_r = jnp.broadcast_to(a16_ref[:, 0, 0:1, :], (gb, rows, 128))
        a_i = jnp.broadcast_to(a16_ref[:, 1, 0:1, :], (gb, rows, 128))
        for i in range(SEG_CHUNKS):
            f_r, f_i = [], []
            for k, t in enumerate(tiles):
                st = slice(k * N_PS, (k + 1) * N_PS)
                rf = slice(t * J_ROWS + i * N_PS, t * J_ROWS + (i + 1) * N_PS)
                rb = slice(t * J_ROWS + (SEG_CHUNKS - 1 - i) * N_PS, t * J_ROWS + (SEG_CHUNKS - i) * N_PS)
                s_scr[:, rf, 0:half] = s_r[:, st, 0:half]
                s_scr[:, rf, 128:128 + half] = s_i[:, st, 0:half]
                s_scr[:, rb, half:128] = s_r[:, st, half:128]
                s_scr[:, rb, 128 + half:256] = s_i[:, st, half:128]
                f_r.append(jnp.where(is_fwd, f_scr[:, rf, 0:128], f_scr[:, rb, 0:128]))
                f_i.append(jnp.where(is_fwd, f_scr[:, rf, 128:256], f_scr[:, rb, 128:256]))
            f_r = jnp.concatenate(f_r, axis=1)
            f_i = jnp.concatenate(f_i, axis=1)
            s_r, s_i = a_r * s_r - a_i * s_i + f_r, a_r * s_i + a_i * s_r + f_i
        return s_r, s_i

    zeros = jnp.zeros((gb, N_TILES * N_PS, 128), F32)
    z_r, z_i = scan(list(range(N_TILES)), zeros, zeros)
    fin_ref[:, :, 0:128] = z_r
    fin_ref[:, :, 128:256] = z_i

    z_r = z_r[:, SAMPLE_TILE * N_PS:, :]
    z_i = z_i[:, SAMPLE_TILE * N_PS:, :]
    b_r, b_i = a16_ref[:, 0], a16_ref[:, 1]
    for _ in range(4):
        b_r, b_i = b_r * b_r - b_i * b_i, 2.0 * (b_r * b_i)
    seg = lax.broadcasted_iota(jnp.int32, (gb, N_PS, 128), 1) % SEGS_PER_SAMPLE_SEQ
    i_r = s0_ref[:, :, 0:128]
    i_i = s0_ref[:, :, 128:256]
    for step in range(1, SEGS_PER_SAMPLE_SEQ):
        pr = jnp.where(is_fwd, pltpu.roll(i_r, 1, 1), pltpu.roll(i_r, N_PS - 1, 1))
        pi = jnp.where(is_fwd, pltpu.roll(i_i, 1, 1), pltpu.roll(i_i, N_PS - 1, 1))
        zr = jnp.where(is_fwd, pltpu.roll(z_r, 1, 1), pltpu.roll(z_r, N_PS - 1, 1))
        zi = jnp.where(is_fwd, pltpu.roll(z_i, 1, 1), pltpu.roll(z_i, N_PS - 1, 1))
        n_r = b_r * pr - b_i * pi + zr
        n_i = b_r * pi + b_i * pr + zi
        first = jnp.where(is_fwd, step, 0)
        last = jnp.where(is_fwd, SEGS_PER_SAMPLE_SEQ - 1, SEGS_PER_SAMPLE_SEQ - 1 - step)
        upd = jnp.logical_and(seg >= first, seg <= last)
        i_r = jnp.where(upd, n_r, i_r)
        i_i = jnp.where(upd, n_i, i_i)
    scan([SAMPLE_TILE], i_r, i_i)

    for gl in range(gb):
        yt = jnp.dot(ttt_ref[gl], group_x(gl), preferred_element_type=F32)
        yt += lax.dot_general(cct_ref[gl], s_scr[gl].astype(BF16), TRANS_B, preferred_element_type=F32)
        yt_ref[:, gl * SSM_GROUP:(gl + 1) * SSM_GROUP, :] = yt.reshape(SCAN_CHUNK, SSM_GROUP, ALL_J_ROWS)


def _ssm_scan(xt, wft, cct, ttt, a16, s0, l):
    gb = GROUP_BLOCK
    op_spec = pl.BlockSpec((None, gb, CHUNK_WIDTH, CHUNK_WIDTH), lambda g: (l, g, 0, 0))
    io_spec = pl.BlockSpec((SCAN_CHUNK, gb * SSM_GROUP, ALL_J_ROWS), lambda g: (0, g, 0))
    rows = pltpu.VMEM((gb, ALL_J_ROWS, CHUNK_WIDTH), F32)
    return pl.pallas_call(
        _ssm_kernel,
        grid=(N_GROUPS // gb,),
        in_specs=[
            io_spec, op_spec, op_spec, op_spec,
            pl.BlockSpec((None, gb, 2, N_PS, 128), lambda g: (l, g, 0, 0, 0)),
            pl.BlockSpec((gb, N_PS, CHUNK_WIDTH), lambda g: (g, 0, 0)),
        ],
        out_specs=[io_spec, pl.BlockSpec((gb, N_TILES * N_PS, CHUNK_WIDTH), lambda g: (g, 0, 0))],
        out_shape=[
            jax.ShapeDtypeStruct((SCAN_CHUNK, D_SSM, ALL_J_ROWS), F32),
            jax.ShapeDtypeStruct((N_GROUPS, N_TILES * N_PS, CHUNK_WIDTH), F32),
        ],
        scratch_shapes=[rows, rows, pltpu.VMEM((gb, CHUNK_WIDTH, ALL_J_ROWS), F32)],
        compiler_params=_params("arbitrary"),
        name="s5_chunk_scan",
    )(xt, wft, cct, ttt, a16, s0)


def _post_kernel(x_ref, yt_ref, ug_ref, vn_ref, mod_ref, wglu_ref, ws_ref, bs_ref, wout_f32_ref,
                 nw_ref, rwt_ref, rb_ref, hx_ref, route_ref, yg_scr, y_scr, wglut_ref, wout_ref):
    tile = pl.program_id(0)
    step = pl.program_id(1)

    @pl.when(_first_step())
    def _():
        wglut_ref[...] = wglu_ref[...].T.astype(BF16)
        wout_ref[...] = wout_f32_ref[...].astype(BF16)

    @pl.when(step == 0)
    def _():
        def chunk(n, carry):
            ps = n % N_PS
            c_hi = n // N_PS
            base = c_hi * (GMLP_CHUNK // SCAN_CHUNK) * N_PS + ps
            rows = [pl.ds(j * J_ROWS + base, GMLP_CHUNK // SCAN_CHUNK, stride=N_PS) for j in range(SCAN_CHUNK)]
            for h in range(GMLP_HEADS):
                v = jnp.concatenate([vn_ref[h, r, :] for r in rows], axis=0).astype(BF16)
                u = jnp.concatenate([ug_ref[h, r, :] for r in rows], axis=0)
                s = jnp.dot(ws_ref[h], v, preferred_element_type=F32) + bs_ref[h]
                yg = u * s
                for j, r in enumerate(rows):
                    yg_scr[h, r, :] = yg[j * 8:(j + 1) * 8]
            return carry

        for n in range(TILE_TOKENS // GMLP_CHUNK):
            chunk(n, 0)

    yt = _gelu_tanh(jnp.concatenate([yt_ref[k] for k in range(J_PER_STEP)], axis=1))
    yt = yt * _sigmoid(jnp.dot(wglut_ref[...], yt.astype(BF16), preferred_element_type=F32))
    for k in range(J_PER_STEP):
        y_scr[k * J_ROWS:(k + 1) * J_ROWS, :] = yt[:, k * J_ROWS:(k + 1) * J_ROWS].T
    row0 = pl.multiple_of(step * TOKEN_TILE, TOKEN_TILE)
    proj = jnp.dot(y_scr[...].astype(BF16), wout_ref[0:D_SSM, :], preferred_element_type=F32)
    yg = jnp.concatenate([yg_scr[h, pl.ds(row0, TOKEN_TILE), :] for h in range(GMLP_HEADS)], axis=1)
    proj += jnp.dot(yg.astype(BF16), wout_ref[D_SSM:, :], preferred_element_type=F32)
    x1 = x_ref[...] + _per_ps(lambda a, g: a * g, proj, mod_ref[2])
    h2 = _per_ps(lambda a, sc, sh: a * (1.0 + sc) + sh, _rmsnorm(x1, nw_ref[...]), mod_ref[4], mod_ref[3])
    hx_ref[:, 0:D_MODEL] = h2
    hx_ref[:, D_MODEL:2 * D_MODEL] = x1
    r_hi, r_lo = _split_bf16(rwt_ref[...])
    h_hi, h_lo = _split_bf16(h2)
    dot_t = lambda a, b: lax.dot_general(a, b, TRANS_B, preferred_element_type=F32)
    logits = dot_t(r_hi, h_hi) + (dot_t(r_hi, h_lo) + dot_t(r_lo, h_hi))
    scores = _sigmoid(logits)
    sel = scores + rb_ref[...]
    sc = [scores[e:e + 1, :] for e in range(N_EXPERTS)]
    sl = [sel[e:e + 1, :] for e in range(N_EXPERTS)]
    gscore = []
    for g in range(N_EXPERT_GROUPS):
        v0, v1, v2, v3 = sl[4 * g:4 * g + 4]
        hi01, lo01 = jnp.maximum(v0, v1), jnp.minimum(v0, v1)
        hi23, lo23 = jnp.maximum(v2, v3), jnp.minimum(v2, v3)
        top1 = jnp.maximum(hi01, hi23)
        top2 = jnp.maximum(jnp.minimum(hi01, hi23), jnp.maximum(lo01, lo23))
        gscore.append(top1 + top2)
    best = gscore[0]
    gidx = jnp.zeros_like(best, dtype=jnp.int32)
    for g in range(1, N_EXPERT_GROUPS):
        upd = gscore[g] > best
        gidx = jnp.where(upd, g, gidx)
        best = jnp.where(upd, gscore[g], best)

    def in_group(vals, k):
        out = vals[k]
        for g in range(1, N_EXPERT_GROUPS):
            out = jnp.where(gidx == g, vals[4 * g + k], out)
        return out

    v = [in_group(sl, k) for k in range(EXPERTS_PER_GROUP)]
    s = [in_group(sc, k) for k in range(EXPERTS_PER_GROUP)]
    w = []
    bits = jnp.zeros_like(gidx)
    for k in range(EXPERTS_PER_GROUP):
        rank = jnp.zeros_like(gidx)
        for j in range(EXPERTS_PER_GROUP):
            if j == k:
                continue
            ahead = (v[j] >= v[k]) if j < k else (v[j] > v[k])
            rank = rank + ahead.astype(jnp.int32)
        w.append(jnp.where(rank < 2, s[k], 0.0))
        bits = bits + jnp.where(rank < 2, 1 << k, 0)
    denom = (w[0] + w[1]) + (w[2] + w[3])
    gate = [wk / denom for wk in w]
    pair = jnp.full_like(gidx, len(PAIR_SLOT_A) - 1)
    for p in range(len(PAIR_SLOT_A) - 1):
        pair = jnp.where(bits == (1 << PAIR_SLOT_A[p]) + (1 << PAIR_SLOT_B[p]), p, pair)

    def slot_gate(table):
        out = gate[table[0]]
        for p in range(1, len(table)):
            out = jnp.where(pair == p, gate[table[p]], out)
        return out

    n_tok = route_ref.shape[1]
    route_ref[0:1, :] = (gidx * len(PAIR_SLOT_A) + pair).astype(F32)
    ps = lax.broadcasted_iota(jnp.int32, (1, n_tok), 1) % N_PS
    cond = jnp.where(tile < SAMPLE_TILE, 0, 1 + ps // SEGS_PER_SAMPLE_SEQ).astype(F32)
    lanes = jnp.concatenate([slot_gate(PAIR_SLOT_A), slot_gate(PAIR_SLOT_B), cond,
                             jnp.zeros((ROUTE_LANES - 3, n_tok), F32)], axis=0)
    hx_ref[:, 2 * D_MODEL:] = lanes.T


def _mix_out(x, yt, ug, vn, mod8, l, w_glu, w_s, b_s, w_out, norm2_w, rwt, rb):
    tm = TOKEN_TILE
    steps = TILE_TOKENS // tm
    tok = lambda n: pl.BlockSpec((tm, n), lambda t, s: (t * steps + s, 0))
    whole_tile = pl.BlockSpec((GMLP_HEADS, TILE_TOKENS, GMLP_HEAD_DIM), lambda t, s: (0, t, 0))
    lay = lambda *shape: pl.BlockSpec((None,) + shape, lambda t, s: (l,) + (0,) * len(shape))
    return pl.pallas_call(
        _post_kernel,
        grid=(N_TILES, steps),
        in_specs=[
            tok(D_MODEL),
            pl.BlockSpec((J_PER_STEP, D_SSM, J_ROWS), lambda t, s: (s, 0, t)),
            whole_tile, whole_tile,
            pl.BlockSpec((None, None, N_MOD, N_PS, D_MODEL), lambda t, s: (l, t, 0, 0, 0)),
            lay(D_SSM, D_SSM), lay(GMLP_HEADS, GMLP_CHUNK, GMLP_CHUNK),
            lay(GMLP_HEADS, GMLP_CHUNK, 1), lay(D_MODEL, D_MODEL), lay(1, D_MODEL),
            pl.BlockSpec((N_EXPERTS, D_MODEL), lambda t, s: (0, 0)),
            pl.BlockSpec((N_EXPERTS, 1), lambda t, s: (0, 0)),
        ],
        out_specs=[tok(HX_WIDTH), pl.BlockSpec((1, tm), lambda t, s: (0, t * steps + s))],
        out_shape=[
            jax.ShapeDtypeStruct((T_ALL, HX_WIDTH), F32),
            jax.ShapeDtypeStruct((1, T_ALL), F32),
        ],
        scratch_shapes=[pltpu.VMEM((GMLP_HEADS, TILE_TOKENS, GMLP_HEAD_DIM), F32), pltpu.VMEM((tm, D_SSM), F32),
                        pltpu.VMEM((D_SSM, D_SSM), BF16), pltpu.VMEM((D_MODEL, D_MODEL), BF16)],
        compiler_params=_params("arbitrary", "arbitrary"),
        name="mixers_out_router",
    )(x, yt, ug, vn, mod8, w_glu, w_s, b_s, w_out, norm2_w.reshape(DEPTH, 1, D_MODEL), rwt, rb)


def _moe_kernel(ea_ref, eb_ref, nv_ref, size_ref, src_ref, drow_ref,
                hx_hbm, wga_ref, wua_ref, wda_ref, wgb_ref, wub_ref, wdb_ref, g2_ref, fw_ref,
                o_hbm, hx_buf, o_buf, z_buf, g_sem, s_sem, z_sem, *, final):
    n = pl.program_id(0)
    n_valid = nv_ref[0]
    slot = n % 2

    def by_size(tile, fn):
        for quarters in range(1, TAIL_STEPS + 1):
            @pl.when(size_ref[tile] == quarters)
            def _(rows=quarters * TAIL_QUARTER):
                fn(rows)

    def start_gather(tile, sl, rows):
        base = tile * MOE_TM
        for r in range(rows):
            pltpu.make_async_copy(hx_hbm.at[pl.ds(src_ref[base + r], 1)], hx_buf.at[sl, pl.ds(r, 1)],
                                  g_sem.at[sl]).start(priority=r % 2)

    def wait_gather(sl, rows):
        pltpu.make_async_copy(hx_hbm.at[pl.ds(0, rows)], hx_buf.at[sl, pl.ds(0, rows)], g_sem.at[sl]).wait()

    def start_scatter(tile, sl, rows):
        base = tile * MOE_TM
        for r in range(rows):
            pltpu.make_async_copy(o_buf.at[sl, pl.ds(r, 1)], o_hbm.at[pl.ds(drow_ref[base + r], 1)],
                                  s_sem.at[sl]).start(priority=r % 2)

    def wait_scatter(sl, rows):
        pltpu.make_async_copy(o_buf.at[sl, pl.ds(0, rows)], o_hbm.at[pl.ds(0, rows)], s_sem.at[sl]).wait()

    @pl.when(n == 0)
    def _():
        z_buf[...] = jnp.zeros_like(z_buf)
        spare = [pltpu.make_async_copy(z_buf, o_hbm.at[pl.ds(T_ALL + i * MOE_TM, MOE_TM)], z_sem)
                 for i in range(SPARE_ROWS // MOE_TM)]
        for cp in spare:
            cp.start()
        for cp in spare:
            cp.wait()
        by_size(0, lambda rows: start_gather(0, 0, rows))

    @pl.when(n + 1 < n_valid)
    def _():
        by_size(n + 1, lambda rows: start_gather(n + 1, 1 - slot, rows))

    def tile_step(rows):
        wait_gather(slot, rows)

        @pl.when(n >= 2)
        def _():
            by_size(n - 2, lambda r: wait_scatter(slot, r))

        h = hx_buf[slot, 0:rows, 0:D_MODEL].astype(BF16)
        lanes = hx_buf[slot, 0:rows, 2 * D_MODEL:]

        def expert(wg, wu, wd, gate):
            hg = jnp.dot(h, wg[...].astype(BF16), preferred_element_type=F32)
            hu = jnp.dot(h, wu[...].astype(BF16), preferred_element_type=F32)
            act = hg * _sigmoid(hg) * hu * gate
            return jnp.dot(act.astype(BF16), wd[...].astype(BF16), preferred_element_type=F32)

        y = expert(wga_ref, wua_ref, wda_ref, lanes[:, 0:1]) + expert(wgb_ref, wub_ref, wdb_ref, lanes[:, 1:2])
        cond_row = lanes[:, 2:3]
        gate2 = jnp.where(cond_row == 0.0, g2_ref[0:1, :], jnp.where(cond_row == 1.0, g2_ref[1:2, :], g2_ref[2:3, :]))
        x2 = hx_buf[slot, 0:rows, D_MODEL:2 * D_MODEL] + gate2 * y
        o_buf[slot, 0:rows] = _rmsnorm(x2, fw_ref[...]) if final else x2
        start_scatter(n, slot, rows)

    @pl.when(n < n_valid)
    def _():
        by_size(n, tile_step)

    @pl.when(n == MOE_TILES - 1)
    def _():
        last = n_valid - 1
        by_size(last, lambda r: wait_scatter(last % 2, r))
        by_size(last - 1, lambda r: wait_scatter(1 - last % 2, r))


def _experts(tables, hx, mods, l, w_gate, w_up, w_down, final_w, final):
    w_a = lambda r, c: pl.BlockSpec((None, None, r, c), lambda n, ea, eb, *_: (l, ea[n], 0, 0))
    w_b = lambda r, c: pl.BlockSpec((None, None, r, c), lambda n, ea, eb, *_: (l, eb[n], 0, 0))
    return pl.pallas_call(
        functools.partial(_moe_kernel, final=final),
        grid_spec=pltpu.PrefetchScalarGridSpec(
            num_scalar_prefetch=len(tables),
            grid=(MOE_TILES,),
            in_specs=[
                pl.BlockSpec(memory_space=pl.ANY),
                w_a(D_MODEL, D_EXPERT), w_a(D_MODEL, D_EXPERT), w_a(D_EXPERT, D_MODEL),
                w_b(D_MODEL, D_EXPERT), w_b(D_MODEL, D_EXPERT), w_b(D_EXPERT, D_MODEL),
                pl.BlockSpec((None, MOD_ROWS, D_MODEL), lambda n, *_: (l, 0, N_MOD - 1)),
                pl.BlockSpec((1, D_MODEL), lambda n, *_: (0, 0)),
            ],
            out_specs=pl.BlockSpec(memory_space=pl.ANY),
            scratch_shapes=[pltpu.VMEM((2, MOE_TM, HX_WIDTH), F32), pltpu.VMEM((2, MOE_TM, D_MODEL), F32),
                            pltpu.VMEM((MOE_TM, D_MODEL), F32),
                            pltpu.SemaphoreType.DMA((2,)), pltpu.SemaphoreType.DMA((2,)), pltpu.SemaphoreType.DMA(())],
        ),
        out_shape=jax.ShapeDtypeStruct((OUT_ROWS, D_MODEL), F32),
        compiler_params=_params("arbitrary"),
        name="experts",
    )(*tables, hx, w_gate, w_up, w_down, w_gate, w_up, w_down, mods, final_w.reshape(1, D_MODEL))


def _count_before(flags):
    n, k = flags.shape
    blocks = flags.reshape(n // 128, 128, k).astype(F32)
    strictly_lower = jnp.tril(jnp.ones((128, 128), F32), -1)
    within = jnp.einsum("ij,bjk->bik", strictly_lower, blocks)
    totals = jnp.sum(blocks, axis=1)
    before = jnp.cumsum(totals, axis=0) - totals
    return (within + before[:, None, :]).reshape(n, k).astype(jnp.int32)


def _routing_tables(cls, to_sequence_order):
    n_cls = N_EXPERT_GROUPS * len(PAIR_SLOT_A)
    onehot = (cls[:, None] == jnp.arange(n_cls, dtype=jnp.int32)[None, :]).astype(jnp.int32)
    counts = jnp.sum(onehot, axis=0)
    tiles = (counts + MOE_TM - 1) // MOE_TM
    tile_end = jnp.cumsum(tiles)
    row_in_class = (tile_end - tiles)[None, :] * MOE_TM + _count_before(onehot)
    dst = jnp.sum(onehot * row_in_class, axis=1)
    token_plus_1 = jnp.zeros((MOE_ROWS,), jnp.int32).at[dst].set(jnp.arange(1, T_ALL + 1, dtype=jnp.int32))
    is_pad = token_plus_1 == 0
    src = jnp.maximum(token_plus_1 - 1, 0)
    n_valid = tile_end[-1]
    tile_id = jnp.minimum(jnp.arange(MOE_TILES, dtype=jnp.int32), n_valid - 1)
    tile_cls = jnp.sum((tile_end[None, :] <= tile_id[:, None]).astype(jnp.int32), axis=1)
    group, pair = tile_cls // len(PAIR_SLOT_A), tile_cls % len(PAIR_SLOT_A)
    in_cls = (tile_cls[:, None] == jnp.arange(n_cls, dtype=jnp.int32)[None, :]).astype(jnp.int32)
    left = jnp.sum(in_cls * (counts[None, :] - (tile_id[:, None] - (tile_end - tiles)[None, :]) * MOE_TM), axis=1)
    size = jnp.clip((left + TAIL_QUARTER - 1) // TAIL_QUARTER, 1, TAIL_STEPS).astype(jnp.int32)
    row = jnp.arange(MOE_ROWS, dtype=jnp.int32)
    processed = jnp.logical_and(row % MOE_TM < jnp.repeat(size, MOE_TM) * TAIL_QUARTER, row // MOE_TM < n_valid)
    moved_pad = jnp.logical_and(is_pad, processed)
    spare = T_ALL + _count_before(moved_pad.astype(jnp.int32)[:, None])[:, 0]
    if to_sequence_order:
        j, c, ps = (src // J_ROWS) % SCAN_CHUNK, (src // N_PS) % SEG_CHUNKS, src % N_PS
        target = (src // TILE_TOKENS) * TILE_TOKENS + ps * SEG_LEN + c * SCAN_CHUNK + j
    else:
        target = src
    drow = jnp.where(is_pad, jnp.where(moved_pad, spare, T_ALL), target).astype(jnp.int32)

    def slot_expert(table):
        local = jnp.full_like(pair, table[-1])
        for p in range(len(table) - 1):
            local = jnp.where(pair == p, table[p], local)
        return group * EXPERTS_PER_GROUP + local

    e_a = slot_expert(PAIR_SLOT_A)
    e_b = slot_expert(PAIR_SLOT_B)
    return e_a, e_b, n_valid.reshape(1).astype(jnp.int32), size, src, drow


def _dirs_on_lanes(p):
    p = jnp.moveaxis(p, 1, -2)
    return p.reshape(p.shape[:-2] + (2 * SSM_STATE,))


def _to_internal_order(x):
    x = x.reshape(N_TILES, N_PS, SEG_CHUNKS, SCAN_CHUNK, D_MODEL)
    return x.transpose(0, 3, 2, 1, 4).reshape(T_ALL, D_MODEL)


def _gmlp_position_order(w):
    n_lo = GMLP_CHUNK // SCAN_CHUNK
    lead = w.shape[:2]
    w = w.reshape(lead + (n_lo, SCAN_CHUNK) + w.shape[3:])
    w = jnp.swapaxes(w, 2, 3)
    return w.reshape(lead + (GMLP_CHUNK,) + w.shape[4:])


def kernel(x_prompt, x_sample, c, state_ssm_re, state_ssm_im, c_ctx, norm1_w, norm2_w, w_mod, b_mod, w_in,
           ssm_a_re, ssm_a_im, ssm_log_dt, ssm_b_re, ssm_b_im, ssm_c_re, ssm_c_im, ssm_d, w_glu,
           gmlp_ln_w, gmlp_ln_b, gmlp_w_s, gmlp_b_s, w_out, router_w, router_b, w_gate, w_up, w_down,
           final_norm_w):
    x = jnp.concatenate([x_prompt.reshape(T_PROMPT, D_MODEL), x_sample.reshape(T_SAMPLE, D_MODEL)], axis=0)
    x = _to_internal_order(x)

    cvec = jnp.concatenate([c_ctx[None, :], c, jnp.zeros((MOD_ROWS - 1 - N_SAMPLE_SEQ, D_MODEL), F32)], axis=0)
    log_dt = jnp.broadcast_to(ssm_log_dt[..., None], ssm_a_re.shape)
    d_lanes = jnp.tile(ssm_d.reshape(DEPTH, N_GROUPS, SSM_GROUP), (1, 1, 128 // SSM_GROUP))
    vecs = jnp.stack([_dirs_on_lanes(ssm_a_re), _dirs_on_lanes(ssm_a_im), _dirs_on_lanes(log_dt), d_lanes], axis=2)
    vecs = jnp.concatenate([vecs, jnp.zeros((DEPTH, N_GROUPS, 4, 2 * SSM_STATE), F32)], axis=2)
    mats = jnp.stack([_dirs_on_lanes(jnp.swapaxes(ssm_b_re, -1, -2)), _dirs_on_lanes(jnp.swapaxes(ssm_b_im, -1, -2)),
                      _dirs_on_lanes(ssm_c_re), _dirs_on_lanes(ssm_c_im)], axis=2)
    mods, wft, cct, ttt, a16 = _modulation_and_operators(cvec, w_mod, b_mod, vecs, mats)
    ps_row = [[0] * N_PS] * SAMPLE_TILE + [[1 + p // SEGS_PER_SAMPLE_SEQ for p in range(N_PS)]]
    mod8 = mods.reshape(DEPTH, MOD_ROWS, N_MOD, D_MODEL)[:, jnp.array(ps_row, jnp.int32)]
    mod8 = mod8.transpose(0, 1, 3, 2, 4)

    w_s = jnp.swapaxes(_gmlp_position_order(jnp.swapaxes(_gmlp_position_order(gmlp_w_s), 2, 3)), 2, 3).astype(BF16)
    b_s = _gmlp_position_order(gmlp_b_s)[..., None]
    rwt = router_w.T
    rb = router_b.reshape(N_EXPERTS, 1)

    new_re, new_im = [], []
    for l in range(DEPTH):
        ug, vn, xt = _input_proj(x, mod8, l, norm1_w, w_in, gmlp_ln_w, gmlp_ln_b)
        s0 = jnp.concatenate([state_ssm_re[:, l].transpose(2, 0, 1, 3).reshape(N_GROUPS, N_SAMPLE_SEQ, 128),
                              state_ssm_im[:, l].transpose(2, 0, 1, 3).reshape(N_GROUPS, N_SAMPLE_SEQ, 128)], axis=-1)
        s0 = jnp.repeat(s0, SEGS_PER_SAMPLE_SEQ, axis=1)
        yt, fin = _ssm_scan(xt, wft, cct, ttt, a16, s0, l)
        fin = fin[:, :N_PROMPT_SEQ].reshape(N_GROUPS, N_PROMPT_SEQ, 2, 2, SSM_STATE)
        fin = fin.transpose(2, 1, 3, 0, 4)
        new_re.append(fin[0])
        new_im.append(fin[1])
        hx, route = _mix_out(x, yt, ug, vn, mod8, l, w_glu, w_s, b_s, w_out, norm2_w, rwt, rb)
        final = l == DEPTH - 1
        tables = _routing_tables(route[0].astype(jnp.int32), to_sequence_order=final)
        x = _experts(tables, hx, mods, l, w_gate, w_up, w_down, final_norm_w, final)

    y_prompt = x[:T_PROMPT].reshape(N_PROMPT_SEQ, PROMPT_LEN, D_MODEL)
    y_sample = x[T_PROMPT:T_ALL].reshape(N_SAMPLE_SEQ, SAMPLE_LEN, D_MODEL)
    return (y_prompt, y_sample, jnp.stack(new_re, axis=1), jnp.stack(new_im, axis=1))
```
